```python
import jax, jax.numpy as jnp
from jax import lax
import numpy as np

D_MODEL = 1024
BATCH = 8
SEQ = 4096
DEPTH = 2

CHUNK = 64
CONV_W = D_MODEL
CONV_A_K = 3
LRU_W = D_MODEL
LRU_HEADS = 8
LRU_BW = LRU_W // LRU_HEADS
CONV_B_K = 4
LRU_C = 8.0
D_FF = 4 * D_MODEL
N_MOD = 6
N_IN = 3 * CONV_W + 2 * LRU_W + 2 * D_MODEL
EPS = 1e-6
SPLITS = (CONV_W, 2 * CONV_W, 3 * CONV_W, 3 * CONV_W + LRU_W,
          3 * CONV_W + 2 * LRU_W, 3 * CONV_W + 2 * LRU_W + D_MODEL)

kernel_name = "hybrid_shortconv_rglru_sandwich_adaln"


def rms_norm(x, g):
    xf = x.astype(jnp.float32)
    y = xf * lax.rsqrt(jnp.mean(xf * xf, axis=-1, keepdims=True) + EPS)
    return (y * g.astype(jnp.float32)).astype(x.dtype)


def causal_dwconv(x, w, b):
    k_w = w.shape[0]
    s = x.shape[1]
    xp = jnp.pad(x, ((0, 0), (k_w - 1, 0), (0, 0)))
    y = b
    for k in range(k_w):
        y = y + xp[:, k:k + s, :] * w[k]
    return y


def block_diag_linear(x, w, b):
    bsz, s, _ = x.shape
    xh = x.reshape(bsz, s, LRU_HEADS, LRU_BW)
    y = jnp.einsum("bshi,hij->bshj", xh, w).reshape(bsz, s, LRU_W)
    return y + b


def _lin_combine(left, right):
    a1, b1 = left
    a2, b2 = right
    return a1 * a2, a2 * b1 + b2


def chunked_linear_scan(a, b):
    bsz, s, w = a.shape
    n = s // CHUNK
    a_c = a.reshape(bsz, n, CHUNK, w)
    b_c = b.reshape(bsz, n, CHUNK, w)
    a_cum, h_loc = lax.associative_scan(_lin_combine, (a_c, b_c), axis=2)

    def step(h_prev, inp):
        a_k, h_k = inp
        h_full = h_k + a_k * h_prev[:, None, :]
        return h_full[:, -1, :], h_full

    h0 = jnp.zeros((bsz, w), a.dtype)
    _, hs = lax.scan(step, h0, (jnp.swapaxes(a_cum, 0, 1), jnp.swapaxes(h_loc, 0, 1)))
    return jnp.swapaxes(hs, 0, 1).reshape(bsz, s, w)


def rg_lru(x, w_r, b_r, w_i, b_i, lam):
    xf = x.astype(jnp.float32)
    r = jax.nn.sigmoid(block_diag_linear(xf, w_r.astype(jnp.float32), b_r.astype(jnp.float32)))
    i = jax.nn.sigmoid(block_diag_linear(xf, w_i.astype(jnp.float32), b_i.astype(jnp.float32)))
    log_a = -LRU_C * r * jax.nn.softplus(-lam.astype(jnp.float32))
    a = jnp.exp(log_a)
    mult = jnp.sqrt(-jnp.expm1(2.0 * log_a))
    h = chunked_linear_scan(a, mult * (i * xf))
    return h.astype(x.dtype)


def hybrid_mixer(h, w_in, conv_a_w, conv_a_b, w_a_out, conv_b_w, conv_b_b,
                 w_gate_r, b_gate_r, w_gate_i, b_gate_i, lru_lambda, w_b_out, w_o):
    proj = h @ w_in
    b_a, c_a, v_a, x_b, g_b, u_a, u_b = jnp.split(proj, SPLITS, axis=-1)
    y_a = b_a * causal_dwconv(c_a * v_a, conv_a_w, conv_a_b)
    x_b = causal_dwconv(x_b, conv_b_w, conv_b_b)
    y_b = rg_lru(x_b, w_gate_r, b_gate_r, w_gate_i, b_gate_i, lru_lambda) * jax.nn.gelu(g_b)
    m = jax.nn.sigmoid(u_a) * (y_a @ w_a_out) + jax.nn.sigmoid(u_b) * (y_b @ w_b_out)
    return m @ w_o


def sq_relu_mlp(h, w_up, w_down):
    return jnp.square(jax.nn.relu(h @ w_up)) @ w_down


def _fwd_setup_inputs(seed: int = 0) -> dict:
    key = jax.random.key(seed)
    ks = jax.random.split(key, 24)
    L, D = DEPTH, D_MODEL
    f32 = jnp.float32

    def nrm(k, shape, scale):
        return jax.random.normal(k, shape, f32) * scale

    u = jax.random.uniform(ks[15], (L, LRU_W), f32, 0.9, 0.999)
    return {
        "x": nrm(ks[0], (BATCH, SEQ, D), 1.0),
        "c": nrm(ks[1], (BATCH, D), 1.0),
        "w_mod": nrm(ks[2], (L, D, N_MOD * D), 0.5 * D ** -0.5),
        "b_mod": nrm(ks[3], (L, N_MOD * D), 0.02),
        "g_pre_mix": 1.0 + nrm(ks[4], (L, D), 0.02),
        "g_post_mix": 1.0 + nrm(ks[5], (L, D), 0.02),
        "w_in": nrm(ks[6], (L, D, N_IN), D ** -0.5),
        "conv_a_w": nrm(ks[7], (L, CONV_A_K, CONV_W), CONV_A_K ** -0.5),
        "conv_a_b": nrm(ks[8], (L, CONV_W), 0.02),
        "w_a_out": nrm(ks[9], (L, CONV_W, D), CONV_W ** -0.5),
        "conv_b_w": nrm(ks[10], (L, CONV_B_K, LRU_W), CONV_B_K ** -0.5),
        "conv_b_b": nrm(ks[11], (L, LRU_W), 0.02),
        "w_gate_r": nrm(ks[12], (L, LRU_HEADS, LRU_BW, LRU_BW), LRU_BW ** -0.5),
        "b_gate_r": nrm(ks[13], (L, LRU_W), 0.02),
        "w_gate_i": nrm(ks[14], (L, LRU_HEADS, LRU_BW, LRU_BW), LRU_BW ** -0.5),
        "b_gate_i": nrm(ks[16], (L, LRU_W), 0.02),
        "lru_lambda": jnp.log(u) - jnp.log1p(-u),
        "w_b_out": nrm(ks[17], (L, LRU_W, D), LRU_W ** -0.5),
        "w_o": nrm(ks[18], (L, D, D), D ** -0.5),
        "g_pre_mlp": 1.0 + nrm(ks[19], (L, D), 0.02),
        "g_post_mlp": 1.0 + nrm(ks[20], (L, D), 0.02),
        "w_mlp_up": nrm(ks[21], (L, D, D_FF), D ** -0.5),
        "w_mlp_down": nrm(ks[22], (L, D_FF, D), D_FF ** -0.5),
    }


def _fwd_reference(x, c, w_mod, b_mod, g_pre_mix, g_post_mix, w_in, conv_a_w, conv_a_b,
              w_a_out, conv_b_w, conv_b_b, w_gate_r, b_gate_r, w_gate_i, b_gate_i,
              lru_lambda, w_b_out, w_o, g_pre_mlp, g_post_mlp, w_mlp_up, w_mlp_down):
    c_act = jax.nn.silu(c)
    for l in range(DEPTH):
        mod = (c_act @ w_mod[l] + b_mod[l])[:, None, :]
        sh_m, sc_m, gt_m, sh_f, sc_f, gt_f = jnp.split(mod, N_MOD, axis=-1)
        h = rms_norm(x, g_pre_mix[l]) * (1.0 + sc_m) + sh_m
        y = hybrid_mixer(h, w_in[l], conv_a_w[l], conv_a_b[l], w_a_out[l],
                         conv_b_w[l], conv_b_b[l], w_gate_r[l], b_gate_r[l],
                         w_gate_i[l], b_gate_i[l], lru_lambda[l], w_b_out[l], w_o[l])
        x = x + gt_m * rms_norm(y, g_post_mix[l])
        h = rms_norm(x, g_pre_mlp[l]) * (1.0 + sc_f) + sh_f
        y = sq_relu_mlp(h, w_mlp_up[l], w_mlp_down[l])
        x = x + gt_f * rms_norm(y, g_post_mlp[l])
    return x


import jax as _jax
import jax.numpy as _jnp

TWIN_FORMAT = 'train_step'
FWD_PARAMS = ['x', 'c', 'w_mod', 'b_mod', 'g_pre_mix', 'g_post_mix', 'w_in', 'conv_a_w', 'conv_a_b', 'w_a_out', 'conv_b_w', 'conv_b_b', 'w_gate_r', 'b_gate_r', 'w_gate_i', 'b_gate_i', 'lru_lambda', 'w_b_out', 'w_o', 'g_pre_mlp', 'g_post_mlp', 'w_mlp_up', 'w_mlp_down']
TWIN_WEIGHTS = ['w_mod', 'b_mod', 'g_pre_mix', 'g_post_mix', 'w_in', 'conv_a_w', 'conv_a_b', 'w_a_out', 'conv_b_w', 'conv_b_b', 'w_gate_r', 'b_gate_r', 'w_gate_i', 'b_gate_i', 'lru_lambda', 'w_b_out', 'w_o', 'g_pre_mlp', 'g_post_mlp', 'w_mlp_up', 'w_mlp_down']
TWIN_DIFF_INPUT = 'x'
TWIN_INPUTS = ['x', 'c', 'w_mod', 'b_mod', 'g_pre_mix', 'g_post_mix', 'w_in', 'conv_a_w', 'conv_a_b', 'w_a_out', 'conv_b_w', 'conv_b_b', 'w_gate_r', 'b_gate_r', 'w_gate_i', 'b_gate_i', 'lru_lambda', 'w_b_out', 'w_o', 'g_pre_mlp', 'g_post_mlp', 'w_mlp_up', 'w_mlp_down', 'loss_target', 'm_w_mod', 'm_b_mod', 'm_g_pre_mix', 'm_g_post_mix', 'm_w_in', 'm_conv_a_w', 'm_conv_a_b', 'm_w_a_out', 'm_conv_b_w', 'm_conv_b_b', 'm_w_gate_r', 'm_b_gate_r', 'm_w_gate_i', 'm_b_gate_i', 'm_lru_lambda', 'm_w_b_out', 'm_w_o', 'm_g_pre_mlp', 'm_g_post_mlp', 'm_w_mlp_up', 'm_w_mlp_down', 'v_w_mod', 'v_b_mod', 'v_g_pre_mix', 'v_g_post_mix', 'v_w_in', 'v_conv_a_w', 'v_conv_a_b', 'v_w_a_out', 'v_conv_b_w', 'v_conv_b_b', 'v_w_gate_r', 'v_b_gate_r', 'v_w_gate_i', 'v_b_gate_i', 'v_lru_lambda', 'v_w_b_out', 'v_w_o', 'v_g_pre_mlp', 'v_g_post_mlp', 'v_w_mlp_up', 'v_w_mlp_down']
TWIN_OUTPUTS = ['loss', 'grad_x', 'grad_w_mod', 'grad_b_mod', 'grad_g_pre_mix', 'grad_g_post_mix', 'grad_w_in', 'grad_conv_a_w', 'grad_conv_a_b', 'grad_w_a_out', 'grad_conv_b_w', 'grad_conv_b_b', 'grad_w_gate_r', 'grad_b_gate_r', 'grad_w_gate_i', 'grad_b_gate_i', 'grad_lru_lambda', 'grad_w_b_out', 'grad_w_o', 'grad_g_pre_mlp', 'grad_g_post_mlp', 'grad_w_mlp_up', 'grad_w_mlp_down', 'delta_w_mod', 'delta_b_mod', 'delta_g_pre_mix', 'delta_g_post_mix', 'delta_w_in', 'delta_conv_a_w', 'delta_conv_a_b', 'delta_w_a_out', 'delta_conv_b_w', 'delta_conv_b_b', 'delta_w_gate_r', 'delta_b_gate_r', 'delta_w_gate_i', 'delta_b_gate_i', 'delta_lru_lambda', 'delta_w_b_out', 'delta_w_o', 'delta_g_pre_mlp', 'delta_g_post_mlp', 'delta_w_mlp_up', 'delta_w_mlp_down', 'new_m_w_mod', 'new_m_b_mod', 'new_m_g_pre_mix', 'new_m_g_post_mix', 'new_m_w_in', 'new_m_conv_a_w', 'new_m_conv_a_b', 'new_m_w_a_out', 'new_m_conv_b_w', 'new_m_conv_b_b', 'new_m_w_gate_r', 'new_m_b_gate_r', 'new_m_w_gate_i', 'new_m_b_gate_i', 'new_m_lru_lambda', 'new_m_w_b_out', 'new_m_w_o', 'new_m_g_pre_mlp', 'new_m_g_post_mlp', 'new_m_w_mlp_up', 'new_m_w_mlp_down', 'new_v_w_mod', 'new_v_b_mod', 'new_v_g_pre_mix', 'new_v_g_post_mix', 'new_v_w_in', 'new_v_conv_a_w', 'new_v_conv_a_b', 'new_v_w_a_out', 'new_v_conv_b_w', 'new_v_conv_b_b', 'new_v_w_gate_r', 'new_v_b_gate_r', 'new_v_w_gate_i', 'new_v_b_gate_i', 'new_v_lru_lambda', 'new_v_w_b_out', 'new_v_w_o', 'new_v_g_pre_mlp', 'new_v_g_post_mlp', 'new_v_w_mlp_up', 'new_v_w_mlp_down']
TWIN_LEAF_KINDS = {'loss': 'loss', 'grad_x': 'grad_x', 'grad_w_mod': 'grad_w', 'grad_b_mod': 'grad_w', 'grad_g_pre_mix': 'grad_w', 'grad_g_post_mix': 'grad_w', 'grad_w_in': 'grad_w', 'grad_conv_a_w': 'grad_w', 'grad_conv_a_b': 'grad_w', 'grad_w_a_out': 'grad_w', 'grad_conv_b_w': 'grad_w', 'grad_conv_b_b': 'grad_w', 'grad_w_gate_r': 'grad_w', 'grad_b_gate_r': 'grad_w', 'grad_w_gate_i': 'grad_w', 'grad_b_gate_i': 'grad_w', 'grad_lru_lambda': 'grad_w', 'grad_w_b_out': 'grad_w', 'grad_w_o': 'grad_w', 'grad_g_pre_mlp': 'grad_w', 'grad_g_post_mlp': 'grad_w', 'grad_w_mlp_up': 'grad_w', 'grad_w_mlp_down': 'grad_w', 'delta_w_mod': 'delta_w', 'delta_b_mod': 'delta_w', 'delta_g_pre_mix': 'delta_w', 'delta_g_post_mix': 'delta_w', 'delta_w_in': 'delta_w', 'delta_conv_a_w': 'delta_w', 'delta_conv_a_b': 'delta_w', 'delta_w_a_out': 'delta_w', 'delta_conv_b_w': 'delta_w', 'delta_conv_b_b': 'delta_w', 'delta_w_gate_r': 'delta_w', 'delta_b_gate_r': 'delta_w', 'delta_w_gate_i': 'delta_w', 'delta_b_gate_i': 'delta_w', 'delta_lru_lambda': 'delta_w', 'delta_w_b_out': 'delta_w', 'delta_w_o': 'delta_w', 'delta_g_pre_mlp': 'delta_w', 'delta_g_post_mlp': 'delta_w', 'delta_w_mlp_up': 'delta_w', 'delta_w_mlp_down': 'delta_w', 'new_m_w_mod': 'new_m', 'new_m_b_mod': 'new_m', 'new_m_g_pre_mix': 'new_m', 'new_m_g_post_mix': 'new_m', 'new_m_w_in': 'new_m', 'new_m_conv_a_w': 'new_m', 'new_m_conv_a_b': 'new_m', 'new_m_w_a_out': 'new_m', 'new_m_conv_b_w': 'new_m', 'new_m_conv_b_b': 'new_m', 'new_m_w_gate_r': 'new_m', 'new_m_b_gate_r': 'new_m', 'new_m_w_gate_i': 'new_m', 'new_m_b_gate_i': 'new_m', 'new_m_lru_lambda': 'new_m', 'new_m_w_b_out': 'new_m', 'new_m_w_o': 'new_m', 'new_m_g_pre_mlp': 'new_m', 'new_m_g_post_mlp': 'new_m', 'new_m_w_mlp_up': 'new_m', 'new_m_w_mlp_down': 'new_m', 'new_v_w_mod': 'new_v', 'new_v_b_mod': 'new_v', 'new_v_g_pre_mix': 'new_v', 'new_v_g_post_mix': 'new_v', 'new_v_w_in': 'new_v', 'new_v_conv_a_w': 'new_v', 'new_v_conv_a_b': 'new_v', 'new_v_w_a_out': 'new_v', 'new_v_conv_b_w': 'new_v', 'new_v_conv_b_b': 'new_v', 'new_v_w_gate_r': 'new_v', 'new_v_b_gate_r': 'new_v', 'new_v_w_gate_i': 'new_v', 'new_v_b_gate_i': 'new_v', 'new_v_lru_lambda': 'new_v', 'new_v_w_b_out': 'new_v', 'new_v_w_o': 'new_v', 'new_v_g_pre_mlp': 'new_v', 'new_v_g_post_mlp': 'new_v', 'new_v_w_mlp_up': 'new_v', 'new_v_w_mlp_down': 'new_v'}


def _forward(args):
    return _fwd_reference(*[args[k] for k in FWD_PARAMS])


def _output_shape():
    out = _jax.eval_shape(lambda: _forward(_fwd_setup_inputs(0)))
    return out.shape, out.dtype

N_MICROBATCH = 1
ADAM_LR = 0.001
ADAM_B1 = 0.9
ADAM_B2 = 0.999
ADAM_EPS = 1e-08
ADAM_WD = 0.01
ADAM_STEP = 10
PER_EXAMPLE_BATCH_AXIS = {'x': 0, 'c': 0, 'loss_target': 0}
SHARED_INPUTS = []
_WEIGHT_DTYPES = {'w_mod': _jnp.float32, 'b_mod': _jnp.float32, 'g_pre_mix': _jnp.float32, 'g_post_mix': _jnp.float32, 'w_in': _jnp.float32, 'conv_a_w': _jnp.float32, 'conv_a_b': _jnp.float32, 'w_a_out': _jnp.float32, 'conv_b_w': _jnp.float32, 'conv_b_b': _jnp.float32, 'w_gate_r': _jnp.float32, 'b_gate_r': _jnp.float32, 'w_gate_i': _jnp.float32, 'b_gate_i': _jnp.float32, 'lru_lambda': _jnp.float32, 'w_b_out': _jnp.float32, 'w_o': _jnp.float32, 'g_pre_mlp': _jnp.float32, 'g_post_mlp': _jnp.float32, 'w_mlp_up': _jnp.float32, 'w_mlp_down': _jnp.float32}
MOMENT_SCALE = {'w_mod': 1.831618e+00, 'b_mod': 3.435227e+00, 'g_pre_mix': 1.640531e-01, 'g_post_mix': 3.863900e+00, 'w_in': 8.260416e-02, 'conv_a_w': 9.134922e-02, 'conv_a_b': 1.961906e-01, 'w_a_out': 9.226499e-02, 'conv_b_w': 2.088770e-01, 'conv_b_b': 6.059457e-01, 'w_gate_r': 1.510000e-02, 'b_gate_r': 3.761067e-02, 'w_gate_i': 3.452431e-02, 'b_gate_i': 8.826279e-02, 'lru_lambda': 1.016918e-01, 'w_b_out': 1.805561e-01, 'w_o': 1.706378e-01, 'g_pre_mlp': 1.443780e-01, 'g_post_mlp': 3.851259e+00, 'w_mlp_up': 1.104829e-01, 'w_mlp_down': 5.073566e-01}


def _to_microbatches(a, axis):
    t = _jnp.moveaxis(a, axis, 0)
    t = t.reshape((N_MICROBATCH, t.shape[0] // N_MICROBATCH) + t.shape[1:])
    return _jnp.moveaxis(t, 1, axis + 1)


def setup_inputs(seed: int = 0) -> dict:
    inp = _fwd_setup_inputs(seed)
    key = _jax.random.fold_in(_jax.random.key(seed), 7919)
    shape, _ = _output_shape()
    out = dict(inp)
    out["loss_target"] = _jax.random.normal(_jax.random.fold_in(key, 0), shape, _jnp.float32)
    for i, name in enumerate(TWIN_WEIGHTS):
        w = inp[name].astype(_jnp.float32)
        if MOMENT_SCALE is None:
            s = _jnp.sqrt(_jnp.mean(_jnp.square(w)) + 1e-30)
        else:
            s = MOMENT_SCALE[name]
        km, kv = _jax.random.split(_jax.random.fold_in(key, i + 1))
        out[name] = w
        out["m_" + name] = s * _jax.random.normal(km, w.shape, _jnp.float32)
        out["v_" + name] = (s * s) * _jax.random.uniform(kv, w.shape, _jnp.float32, 0.5, 1.5)
    if N_MICROBATCH > 1:
        for name, axis in PER_EXAMPLE_BATCH_AXIS.items():
            out[name] = _to_microbatches(out[name], axis)
    return {'x': out['x'], 'c': out['c'], 'w_mod': out['w_mod'], 'b_mod': out['b_mod'], 'g_pre_mix': out['g_pre_mix'], 'g_post_mix': out['g_post_mix'], 'w_in': out['w_in'], 'conv_a_w': out['conv_a_w'], 'conv_a_b': out['conv_a_b'], 'w_a_out': out['w_a_out'], 'conv_b_w': out['conv_b_w'], 'conv_b_b': out['conv_b_b'], 'w_gate_r': out['w_gate_r'], 'b_gate_r': out['b_gate_r'], 'w_gate_i': out['w_gate_i'], 'b_gate_i': out['b_gate_i'], 'lru_lambda': out['lru_lambda'], 'w_b_out': out['w_b_out'], 'w_o': out['w_o'], 'g_pre_mlp': out['g_pre_mlp'], 'g_post_mlp': out['g_post_mlp'], 'w_mlp_up': out['w_mlp_up'], 'w_mlp_down': out['w_mlp_down'], 'loss_target': out['loss_target'], 'm_w_mod': out['m_w_mod'], 'm_b_mod': out['m_b_mod'], 'm_g_pre_mix': out['m_g_pre_mix'], 'm_g_post_mix': out['m_g_post_mix'], 'm_w_in': out['m_w_in'], 'm_conv_a_w': out['m_conv_a_w'], 'm_conv_a_b': out['m_conv_a_b'], 'm_w_a_out': out['m_w_a_out'], 'm_conv_b_w': out['m_conv_b_w'], 'm_conv_b_b': out['m_conv_b_b'], 'm_w_gate_r': out['m_w_gate_r'], 'm_b_gate_r': out['m_b_gate_r'], 'm_w_gate_i': out['m_w_gate_i'], 'm_b_gate_i': out['m_b_gate_i'], 'm_lru_lambda': out['m_lru_lambda'], 'm_w_b_out': out['m_w_b_out'], 'm_w_o': out['m_w_o'], 'm_g_pre_mlp': out['m_g_pre_mlp'], 'm_g_post_mlp': out['m_g_post_mlp'], 'm_w_mlp_up': out['m_w_mlp_up'], 'm_w_mlp_down': out['m_w_mlp_down'], 'v_w_mod': out['v_w_mod'], 'v_b_mod': out['v_b_mod'], 'v_g_pre_mix': out['v_g_pre_mix'], 'v_g_post_mix': out['v_g_post_mix'], 'v_w_in': out['v_w_in'], 'v_conv_a_w': out['v_conv_a_w'], 'v_conv_a_b': out['v_conv_a_b'], 'v_w_a_out': out['v_w_a_out'], 'v_conv_b_w': out['v_conv_b_w'], 'v_conv_b_b': out['v_conv_b_b'], 'v_w_gate_r': out['v_w_gate_r'], 'v_b_gate_r': out['v_b_gate_r'], 'v_w_gate_i': out['v_w_gate_i'], 'v_b_gate_i': out['v_b_gate_i'], 'v_lru_lambda': out['v_lru_lambda'], 'v_w_b_out': out['v_w_b_out'], 'v_w_o': out['v_w_o'], 'v_g_pre_mlp': out['v_g_pre_mlp'], 'v_g_post_mlp': out['v_g_post_mlp'], 'v_w_mlp_up': out['v_w_mlp_up'], 'v_w_mlp_down': out['v_w_mlp_down']}


def _loss(weights, diff, rest, loss_target):
    with _jax.named_scope("forward"):
        args = {**rest, TWIN_DIFF_INPUT: diff, **{k: w.astype(_WEIGHT_DTYPES[k]) for k, w in weights.items()}}
        y = _forward(args)
    with _jax.named_scope("loss_head"):
        err = _jnp.square(y.astype(_jnp.float32) - loss_target)
        return 0.5 * _jnp.sum(_jnp.mean(err, axis=-1)) if err.ndim else 0.5 * err


def _adamw(w, g, m, v):
    m = ADAM_B1 * m + (1.0 - ADAM_B1) * g
    v = ADAM_B2 * v + (1.0 - ADAM_B2) * _jnp.square(g)
    m_hat = m / (1.0 - ADAM_B1 ** ADAM_STEP)
    v_hat = v / (1.0 - ADAM_B2 ** ADAM_STEP)
    delta = -ADAM_LR * (m_hat / (_jnp.sqrt(v_hat) + ADAM_EPS) + ADAM_WD * w)
    return delta, m, v


def reference(x, c, w_mod, b_mod, g_pre_mix, g_post_mix, w_in, conv_a_w, conv_a_b, w_a_out, conv_b_w, conv_b_b, w_gate_r, b_gate_r, w_gate_i, b_gate_i, lru_lambda, w_b_out, w_o, g_pre_mlp, g_post_mlp, w_mlp_up, w_mlp_down, loss_target, m_w_mod, m_b_mod, m_g_pre_mix, m_g_post_mix, m_w_in, m_conv_a_w, m_conv_a_b, m_w_a_out, m_conv_b_w, m_conv_b_b, m_w_gate_r, m_b_gate_r, m_w_gate_i, m_b_gate_i, m_lru_lambda, m_w_b_out, m_w_o, m_g_pre_mlp, m_g_post_mlp, m_w_mlp_up, m_w_mlp_down, v_w_mod, v_b_mod, v_g_pre_mix, v_g_post_mix, v_w_in, v_conv_a_w, v_conv_a_b, v_w_a_out, v_conv_b_w, v_conv_b_b, v_w_gate_r, v_b_gate_r, v_w_gate_i, v_b_gate_i, v_lru_lambda, v_w_b_out, v_w_o, v_g_pre_mlp, v_g_post_mlp, v_w_mlp_up, v_w_mlp_down):
    given = dict(x=x, c=c, w_mod=w_mod, b_mod=b_mod, g_pre_mix=g_pre_mix, g_post_mix=g_post_mix, w_in=w_in, conv_a_w=conv_a_w, conv_a_b=conv_a_b, w_a_out=w_a_out, conv_b_w=conv_b_w, conv_b_b=conv_b_b, w_gate_r=w_gate_r, b_gate_r=b_gate_r, w_gate_i=w_gate_i, b_gate_i=b_gate_i, lru_lambda=lru_lambda, w_b_out=w_b_out, w_o=w_o, g_pre_mlp=g_pre_mlp, g_post_mlp=g_post_mlp, w_mlp_up=w_mlp_up, w_mlp_down=w_mlp_down, loss_target=loss_target, m_w_mod=m_w_mod, m_b_mod=m_b_mod, m_g_pre_mix=m_g_pre_mix, m_g_post_mix=m_g_post_mix, m_w_in=m_w_in, m_conv_a_w=m_conv_a_w, m_conv_a_b=m_conv_a_b, m_w_a_out=m_w_a_out, m_conv_b_w=m_conv_b_w, m_conv_b_b=m_conv_b_b, m_w_gate_r=m_w_gate_r, m_b_gate_r=m_b_gate_r, m_w_gate_i=m_w_gate_i, m_b_gate_i=m_b_gate_i, m_lru_lambda=m_lru_lambda, m_w_b_out=m_w_b_out, m_w_o=m_w_o, m_g_pre_mlp=m_g_pre_mlp, m_g_post_mlp=m_g_post_mlp, m_w_mlp_up=m_w_mlp_up, m_w_mlp_down=m_w_mlp_down, v_w_mod=v_w_mod, v_b_mod=v_b_mod, v_g_pre_mix=v_g_pre_mix, v_g_post_mix=v_g_post_mix, v_w_in=v_w_in, v_conv_a_w=v_conv_a_w, v_conv_a_b=v_conv_a_b, v_w_a_out=v_w_a_out, v_conv_b_w=v_conv_b_w, v_conv_b_b=v_conv_b_b, v_w_gate_r=v_w_gate_r, v_b_gate_r=v_b_gate_r, v_w_gate_i=v_w_gate_i, v_b_gate_i=v_b_gate_i, v_lru_lambda=v_lru_lambda, v_w_b_out=v_w_b_out, v_w_o=v_w_o, v_g_pre_mlp=v_g_pre_mlp, v_g_post_mlp=v_g_post_mlp, v_w_mlp_up=v_w_mlp_up, v_w_mlp_down=v_w_mlp_down)
    weights = {n: given[n] for n in TWIN_WEIGHTS}
    shared = {n: given[n] for n in SHARED_INPUTS}
    per_example = {n: given[n] for n in ['x', 'c']}
    grad_fn = _jax.value_and_grad(_loss, argnums=(0, 1))

    def one_microbatch(ex, loss_target):
        ex = dict(ex)
        diff = ex.pop(TWIN_DIFF_INPUT)
        return grad_fn(weights, diff, {**shared, **ex}, loss_target)

    if N_MICROBATCH == 1:
        loss, (grad_w, grad_x) = one_microbatch(per_example, given["loss_target"])
    else:
        def body(carry, xs):
            loss_sum, grad_sum = carry
            l_k, (gw_k, gx_k) = one_microbatch(xs[0], xs[1])
            with _jax.named_scope("update"):
                return (loss_sum + l_k, _jax.tree.map(_jnp.add, grad_sum, gw_k)), gx_k

        init = (_jnp.zeros((), _jnp.float32), _jax.tree.map(_jnp.zeros_like, weights))
        (loss, grad_w), grad_x = _jax.lax.scan(body, init, (per_example, given["loss_target"]))
    with _jax.named_scope("update"):
        delta_w, new_m, new_v = {}, {}, {}
        for n in TWIN_WEIGHTS:
            delta_w[n], new_m[n], new_v[n] = _adamw(weights[n], grad_w[n], given["m_" + n], given["v_" + n])
    return (loss, grad_x, *[grad_w[n] for n in TWIN_WEIGHTS], *[delta_w[n] for n in TWIN_WEIGHTS],
            *[new_m[n] for n in TWIN_WEIGHTS], *[new_v[n] for n in TWIN_WEIGHTS])
```

```python
import functools

import jax
import jax.numpy as jnp
from jax import lax
from jax.experimental import pallas as pl
from jax.experimental.pallas import tpu as pltpu

F32 = jnp.float32
BF16 = jnp.bfloat16
MESH = pl.DeviceIdType.MESH

EPS = 1e-6
LRU_C = 8.0
N_CHIP = 4
N_DEV = 8
ADAM_LR = 0.001
ADAM_B1 = 0.9
ADAM_B2 = 0.999
ADAM_EPS = 1e-08
ADAM_WD = 0.01
ADAM_STEP = 10

VMEM_LIMIT_BYTES = 56 * 1024 * 1024
SUBLANES = 8
TOKENS_MATMUL_TILE = 512
TOKENS_MIXER_TILE = 256
GELU_K0 = 0.7978845608028654
GELU_K1 = 0.044715

V_G_PRE_MIX, V_G_POST_MIX, V_CONV_A_B, V_CONV_B_B, V_B_GATE_R, V_B_GATE_I, V_LAMBDA, V_G_PRE_MLP, V_G_POST_MLP = range(9)
V_CONV_A_W = 9
V_CONV_B_W = 12
M_SH_M, M_SC_M, M_GT_M, M_SH_F, M_SC_F, M_GT_F = range(6)


def _cparams(n_grid=0):
    sem = ("arbitrary",) * n_grid if n_grid else None
    return pltpu.CompilerParams(dimension_semantics=sem, vmem_limit_bytes=VMEM_LIMIT_BYTES)


def _full(shape):
    return pl.BlockSpec(shape, lambda *_: (0,) * len(shape))


def _dot(a, b):
    return jnp.dot(a, b, preferred_element_type=F32)


def _dot_tb(a, b):
    return lax.dot_general(a, b, (((1,), (1,)), ((), ())), preferred_element_type=F32)


def _dot_ta(a, b):
    return lax.dot_general(a, b, (((0,), (0,)), ((), ())), preferred_element_type=F32)


def _sigmoid(x):
    return 1.0 / (1.0 + jnp.exp(-x))


def _softplus(x):
    return jnp.maximum(x, 0.0) + jnp.log1p(jnp.exp(-jnp.abs(x)))


def _neg_expm1(x):
    series = -x * (1.0 + 0.5 * x * (1.0 + (x / 3.0) * (1.0 + 0.25 * x)))
    return jnp.where(x > -1e-2, series, 1.0 - jnp.exp(x))


def _gelu(x):
    t = jnp.tanh(GELU_K0 * (x + GELU_K1 * x * x * x))
    return 0.5 * x * (1.0 + t), t


def _gelu_grad(x, t):
    return 0.5 * (1.0 + t) + 0.5 * x * (1.0 - t * t) * GELU_K0 * (1.0 + 3.0 * GELU_K1 * x * x)


def _rms(x):
    r = lax.rsqrt(jnp.mean(x * x, axis=-1, keepdims=True) + EPS)
    return x * r, r


def _rms_bwd(dxn, xn, r):
    return r * (dxn - xn * jnp.mean(dxn * xn, axis=-1, keepdims=True))


def _colsum(x):
    return jnp.sum(x, axis=0, keepdims=True)


def _rows(t, w):
    return lax.broadcasted_iota(jnp.int32, (t, w), 0)


def _shift_down(x, k, prev8):
    t, w = x.shape
    rolled = pltpu.roll(x, k, 0)
    head = jnp.where(_rows(SUBLANES, w) < k, pltpu.roll(prev8, k, 0), rolled[:SUBLANES])
    return jnp.concatenate([head, rolled[SUBLANES:]], axis=0)


def _shift_up(x, k, next8):
    t, w = x.shape
    rolled = pltpu.roll(x, t - k, 0)
    tail = jnp.where(_rows(SUBLANES, w) >= SUBLANES - k, pltpu.roll(next8, SUBLANES - k, 0), rolled[t - SUBLANES:])
    return jnp.concatenate([rolled[:t - SUBLANES], tail], axis=0)


def _scan_fwd(a, b):
    t, w = a.shape
    rows = _rows(t, w)
    s = 1
    while s < t:
        keep = rows >= s
        b = b + a * jnp.where(keep, pltpu.roll(b, s, 0), 0.0)
        a = a * jnp.where(keep, pltpu.roll(a, s, 0), 1.0)
        s *= 2
    return b, a


def _scan_bwd(a, b):
    t, w = a.shape
    rows = _rows(t, w)
    s = 1
    while s < t:
        keep = rows < t - s
        b = b + a * jnp.where(keep, pltpu.roll(b, t - s, 0), 0.0)
        a = a * jnp.where(keep, pltpu.roll(a, t - s, 0), 1.0)
        s *= 2
    return b, a


def _row_tile(rows, cols, itemsize=4, target_bytes=2 * 1024 * 1024):
    if rows * cols * itemsize <= target_bytes or rows % SUBLANES:
        return rows
    t = max(SUBLANES, (target_bytes // (cols * itemsize)) // SUBLANES * SUBLANES)
    while rows % t:
        t -= SUBLANES
    return t


def _place():
    return lax.axis_index("x"), lax.axis_index("y"), lax.axis_index("c")


def _other_chips(x, y):
    chips = [(1 - x, y), (x, 1 - y), (1 - x, 1 - y)]
    return chips, [2 * cx + cy for cx, cy in chips]


def _all_gather_small(block, name):
    m_per, n = block.shape

    def body(x_ref, out_ref, send_sems, recv_sems, local_sem):
        x, y, c = _place()
        me, sibling = (x, y, c), (x, y, 1 - c)
        chips, _ = _other_chips(x, y)

        def rows(px, py, pc):
            return out_ref.at[pl.ds((4 * px + 2 * py + pc) * m_per, m_per), :]

        def copy(k, blk, to, src=None):
            return pltpu.make_async_remote_copy(
                src_ref=rows(*blk) if src is None else src, dst_ref=rows(*blk),
                send_sem=send_sems.at[k], recv_sem=recv_sems.at[k], device_id=to, device_id_type=MESH)

        mine = pltpu.make_async_copy(x_ref, rows(*me), local_sem)
        mine.start()
        first = [copy(0, me, sibling, src=x_ref)]
        first += [copy(1 + j, me, (*chip, c), src=x_ref) for j, chip in enumerate(chips)]
        for cp in first:
            cp.start()
        passed = [copy(4 + j, (*chip, c), sibling) for j, chip in enumerate(chips)]
        for j, chip in enumerate(chips):
            copy(1 + j, (*chip, c), me).wait_recv()
            passed[j].start()
        copy(0, sibling, me).wait_recv()
        for j, chip in enumerate(chips):
            copy(4 + j, (*chip, 1 - c), me).wait_recv()
        for cp in first + passed:
            cp.wait_send()
        mine.wait()

    return pl.pallas_call(
        body, name=name,
        out_shape=jax.ShapeDtypeStruct((N_DEV * m_per, n), block.dtype),
        in_specs=[pl.BlockSpec(memory_space=pltpu.VMEM)],
        out_specs=pl.BlockSpec(memory_space=pltpu.VMEM),
        scratch_shapes=[pltpu.SemaphoreType.DMA((7,)), pltpu.SemaphoreType.DMA((7,)), pltpu.SemaphoreType.DMA],
        compiler_params=pltpu.CompilerParams(vmem_limit_bytes=VMEM_LIMIT_BYTES),
    )(block)


def _gather_weights(shards, name):
    n = len(shards)
    arrs = [a for a, _ in shards]

    def body(*refs):
        w_refs, o_refs = refs[:n], refs[n:2 * n]
        send_sems, recv_sems, local_sems = refs[2 * n:]
        x, y, c = _place()
        q = 2 * x + y
        sibling = (x, y, 1 - c)
        chips, qs = _other_chips(x, y)

        def half(w, pc):
            _, layer = shards[w]
            rh = arrs[w].shape[1] // 2
            return layer, pl.ds(pc * rh, rh)

        def src_half(w, pc):
            layer, rows = half(w, pc)
            return w_refs[w].at[layer, rows, :]

        def dst_half(w, shard, pc):
            _, rows = half(w, pc)
            return o_refs[w].at[shard, rows, :]

        def over_ici(w, j, shard, src):
            return pltpu.make_async_remote_copy(
                src_ref=src, dst_ref=dst_half(w, shard, c), send_sem=send_sems.at[w, j], recv_sem=recv_sems.at[w, j],
                device_id=(*chips[j], c), device_id_type=MESH)

        def to_sibling(w, j, pc):
            blk = dst_half(w, qs[j], pc)
            return pltpu.make_async_remote_copy(
                src_ref=blk, dst_ref=blk, send_sem=send_sems.at[w, 3 + j], recv_sem=recv_sems.at[w, 3 + j],
                device_id=sibling, device_id_type=MESH)

        own = []
        for w in range(n):
            _, layer = shards[w]
            cp = pltpu.make_async_copy(w_refs[w].at[layer], o_refs[w].at[q], local_sems.at[w])
            cp.start()
            own.append(cp)
        sent = []
        for w in range(n):
            for j in range(3):
                cp = over_ici(w, j, q, src_half(w, c))
                cp.start()
                sent.append(cp)
        for w in range(n):
            for j in range(3):
                over_ici(w, j, qs[j], src_half(w, c)).wait_recv()
                cp = to_sibling(w, j, c)
                cp.start()
                sent.append(cp)
        for w in range(n):
            for j in range(3):
                to_sibling(w, j, 1 - c).wait_recv()
        for cp in sent:
            cp.wait_send()
        for cp in own:
            cp.wait()

    any_spec = pl.BlockSpec(memory_space=pl.ANY)
    return pl.pallas_call(
        body, name=name,
        out_shape=[jax.ShapeDtypeStruct((N_CHIP,) + a.shape[1:], a.dtype) for a in arrs],
        in_specs=[any_spec] * n, out_specs=[any_spec] * n,
        scratch_shapes=[pltpu.SemaphoreType.DMA((n, 6)), pltpu.SemaphoreType.DMA((n, 6)), pltpu.SemaphoreType.DMA((n,))],
    )(*arrs)


def _exchange_halves(grads, name):
    n = len(grads)

    def body(*refs):
        g_refs, l_refs = refs[:n], refs[n:2 * n]
        send_sems, recv_sems = refs[2 * n:]
        x, y, c = _place()
        sibling = (x, y, 1 - c)
        cps = []
        for w in range(n):
            rh = grads[w].shape[1] // 2
            cp = pltpu.make_async_remote_copy(
                src_ref=g_refs[w].at[:, pl.ds((1 - c) * rh, rh), :], dst_ref=l_refs[w],
                send_sem=send_sems.at[w], recv_sem=recv_sems.at[w], device_id=sibling, device_id_type=MESH)
            cp.start()
            cps.append(cp)
        for cp in cps:
            cp.wait()

    any_spec = pl.BlockSpec(memory_space=pl.ANY)
    return pl.pallas_call(
        body, name=name,
        out_shape=[jax.ShapeDtypeStruct((N_CHIP, g.shape[1] // 2, g.shape[2]), g.dtype) for g in grads],
        in_specs=[any_spec] * n, out_specs=[any_spec] * n,
        scratch_shapes=[pltpu.SemaphoreType.DMA((n,)), pltpu.SemaphoreType.DMA((n,))],
    )(*grads)


def _scatter_to_chips(sums, name):
    n = len(sums)

    def body(*refs):
        s_refs, l_refs = refs[:n], refs[n:2 * n]
        send_sems, recv_sems = refs[2 * n:]
        x, y, c = _place()
        chips, qs = _other_chips(x, y)
        cps = []
        for w in range(n):
            for j in range(3):
                cp = pltpu.make_async_remote_copy(
                    src_ref=s_refs[w].at[qs[j]], dst_ref=l_refs[w].at[j],
                    send_sem=send_sems.at[w, j], recv_sem=recv_sems.at[w, j], device_id=(*chips[j], c), device_id_type=MESH)
                cp.start()
                cps.append(cp)
        for cp in cps:
            cp.wait()

    any_spec = pl.BlockSpec(memory_space=pl.ANY)
    return pl.pallas_call(
        body, name=name,
        out_shape=[jax.ShapeDtypeStruct((3,) + s.shape[1:], s.dtype) for s in sums],
        in_specs=[any_spec] * n, out_specs=[any_spec] * n,
        scratch_shapes=[pltpu.SemaphoreType.DMA((n, 3)), pltpu.SemaphoreType.DMA((n, 3))],
    )(*sums)


def _join_halves(groups, name):
    flat = [h for grp in groups for h in grp]
    n, n_out = len(flat), len(groups)

    def body(*refs):
        h_refs, o_refs = refs[:n], refs[n:n + n_out]
        send_sems, recv_sems, local_sems = refs[n + n_out:]
        x, y, c = _place()
        sibling = (x, y, 1 - c)
        cps = []
        k = 0
        for o, grp in enumerate(groups):
            for layer in range(len(grp)):
                rh = grp[layer].shape[0]
                dst = o_refs[o].at[layer, pl.ds(c * rh, rh), :]
                loc = pltpu.make_async_copy(h_refs[k], dst, local_sems.at[k])
                loc.start()
                cps.append(loc)
                rem = pltpu.make_async_remote_copy(
                    src_ref=h_refs[k], dst_ref=dst, send_sem=send_sems.at[k], recv_sem=recv_sems.at[k],
                    device_id=sibling, device_id_type=MESH)
                rem.start()
                cps.append(rem)
                k += 1
        for cp in cps:
            cp.wait()

    any_spec = pl.BlockSpec(memory_space=pl.ANY)
    return pl.pallas_call(
        body, name=name,
        out_shape=[jax.ShapeDtypeStruct((len(grp), 2 * grp[0].shape[0], grp[0].shape[1]), F32) for grp in groups],
        in_specs=[any_spec] * n, out_specs=[any_spec] * n_out,
        scratch_shapes=[pltpu.SemaphoreType.DMA((n,)), pltpu.SemaphoreType.DMA((n,)), pltpu.SemaphoreType.DMA((n,))],
    )(*flat)


def _add_sibling_half(g, landed, c_arr, name):
    _, r, cols = g.shape
    rh = r // 2
    tr = _row_tile(rh, cols)
    nr = rh // tr

    def body(c_ref, g_ref, l_ref, o_ref):
        o_ref[...] = g_ref[...] + l_ref[...]

    return pl.pallas_call(
        body, name=name,
        out_shape=jax.ShapeDtypeStruct((N_CHIP, rh, cols), F32),
        grid_spec=pltpu.PrefetchScalarGridSpec(
            num_scalar_prefetch=1, grid=(N_CHIP, nr),
            in_specs=[pl.BlockSpec((1, tr, cols), lambda qb, i, c_ref: (qb, c_ref[0] * nr + i, 0)),
                      pl.BlockSpec((1, tr, cols), lambda qb, i, c_ref: (qb, i, 0))],
            out_specs=pl.BlockSpec((1, tr, cols), lambda qb, i, c_ref: (qb, i, 0))),
        compiler_params=_cparams(2),
    )(c_arr, g, landed)


def _add_chips(s, landed, q_arr, name):
    _, rh, cols = s.shape
    tr = _row_tile(rh, cols)

    def body(q_ref, s_ref, l_ref, o_ref):
        o_ref[...] = ((s_ref[0] + l_ref[0]) + l_ref[1]) + l_ref[2]

    return pl.pallas_call(
        body, name=name,
        out_shape=jax.ShapeDtypeStruct((rh, cols), F32),
        grid_spec=pltpu.PrefetchScalarGridSpec(
            num_scalar_prefetch=1, grid=(rh // tr,),
            in_specs=[pl.BlockSpec((1, tr, cols), lambda i, q_ref: (q_ref[0], i, 0)),
                      pl.BlockSpec((3, tr, cols), lambda i, q_ref: (0, i, 0))],
            out_specs=pl.BlockSpec((tr, cols), lambda i, q_ref: (i, 0))),
        compiler_params=_cparams(1),
    )(q_arr, s, landed)


def _sum_devices(gathered, name):
    rows, n = gathered.shape
    m = rows // N_DEV
    tr = _row_tile(m, n, target_bytes=512 * 1024)
    g3 = gathered.reshape(N_DEV, m, n)

    def body(g_ref, o_ref):
        acc = g_ref[0]
        for d in range(1, N_DEV):
            acc = acc + g_ref[d]
        o_ref[...] = acc

    return pl.pallas_call(
        body, name=name, grid=(m // tr,),
        out_shape=jax.ShapeDtypeStruct((m, n), F32),
        in_specs=[pl.BlockSpec((N_DEV, tr, n), lambda i: (0, i, 0))],
        out_specs=pl.BlockSpec((tr, n), lambda i: (i, 0)),
        compiler_params=_cparams(1),
    )(g3)


def _mod_forward(c_all, w_mod, b_mod_shard, name):
    n_layer, d, mq = w_mod.shape

    def body(c_ref, w_ref, b_ref, o_ref):
        cv = c_ref[...]
        o_ref[...] = _dot(cv * _sigmoid(cv), w_ref[0]) + b_ref[0]

    return pl.pallas_call(
        body, name=name, grid=(n_layer,),
        out_shape=jax.ShapeDtypeStruct((n_layer * N_DEV, mq), F32),
        in_specs=[_full((N_DEV, d)), pl.BlockSpec((1, d, mq), lambda l: (l, 0, 0)),
                  pl.BlockSpec((1, 1, mq), lambda l: (l, 0, 0))],
        out_specs=pl.BlockSpec((N_DEV, mq), lambda l: (l, 0)),
        compiler_params=_cparams(1),
    )(c_all, w_mod, b_mod_shard.reshape(n_layer, 1, mq))


def _mod_backward(c_all_t, dmod_shard, name):
    n_layer, _, mq = dmod_shard.shape
    d = c_all_t.shape[0]

    def body(c_ref, dm_ref, o_ref):
        cv = c_ref[...]
        o_ref[0] = _dot(cv * _sigmoid(cv), dm_ref[0])

    return pl.pallas_call(
        body, name=name, grid=(n_layer,),
        out_shape=jax.ShapeDtypeStruct((n_layer, d, mq), F32),
        in_specs=[_full((d, N_DEV)), pl.BlockSpec((1, N_DEV, mq), lambda l: (l, 0, 0))],
        out_specs=pl.BlockSpec((1, d, mq), lambda l: (l, 0, 0)),
        compiler_params=_cparams(1),
    )(c_all_t, dmod_shard)


def _norm_proj(x, mod, vec, w_in, name):
    s, d = x.shape
    nq = w_in.shape[2]
    ts = min(TOKENS_MATMUL_TILE, s)

    def body(x_ref, mod_ref, vec_ref, w_ref, h_ref, p_ref):
        xn, _ = _rms(x_ref[...])
        gm = vec_ref[V_G_PRE_MIX:V_G_PRE_MIX + 1, :] * (1.0 + mod_ref[M_SC_M:M_SC_M + 1, :])
        h = (xn * gm + mod_ref[M_SH_M:M_SH_M + 1, :]).astype(BF16)
        h_ref[...] = h
        for qb in range(N_CHIP):
            p_ref[:, qb * nq:(qb + 1) * nq] = _dot(h, w_ref[qb]).astype(BF16)

    return pl.pallas_call(
        body, name=name, grid=(s // ts,),
        out_shape=[jax.ShapeDtypeStruct((s, d), BF16), jax.ShapeDtypeStruct((s, N_CHIP * nq), BF16)],
        in_specs=[pl.BlockSpec((ts, d), lambda i: (i, 0)), _full(mod.shape), _full(vec.shape), _full(w_in.shape)],
        out_specs=[pl.BlockSpec((ts, d), lambda i: (i, 0)), pl.BlockSpec((ts, N_CHIP * nq), lambda i: (i, 0))],
        compiler_params=_cparams(1),
    )(x, mod, vec, w_in)


def _gate_pre(xb2_b, wg_ref, n_head, bw):
    zr, zi = [], []
    for hd in range(n_head):
        z = _dot(xb2_b[:, hd * bw:(hd + 1) * bw], wg_ref[hd])
        zr.append(z[:, :bw])
        zi.append(z[:, bw:])
    return jnp.concatenate(zr, axis=1), jnp.concatenate(zi, axis=1)


def _lru_coeffs(xb2, wg_ref, vec_ref, n_head, bw):
    zr, zi = _gate_pre(xb2.astype(BF16), wg_ref, n_head, bw)
    r = _sigmoid(zr + vec_ref[V_B_GATE_R:V_B_GATE_R + 1, :])
    gi = _sigmoid(zi + vec_ref[V_B_GATE_I:V_B_GATE_I + 1, :])
    sp = _softplus(-vec_ref[V_LAMBDA:V_LAMBDA + 1, :])
    log_a = (-LRU_C) * r * sp
    a = jnp.exp(log_a)
    mult = jnp.sqrt(_neg_expm1(2.0 * log_a))
    return r, gi, sp, a, mult


def _mixer_forward(x, proj, mod, vec, wg, w_a_out, w_b_out, w_o, name):
    s, d = x.shape
    n_head, bw, _ = wg.shape
    ts = min(TOKENS_MIXER_TILE, s)

    def body(x_ref, p_ref, mod_ref, vec_ref, wg_ref, wa_ref, wb_ref, wo_ref,
             x1_ref, conva_ref, xb2_ref, hh_ref, ya_ref, yb_ref, pa_ref, pb_ref, m_ref, y_ref,
             cv_tail, xb_tail, h_last):
        i = pl.program_id(0)

        @pl.when(i == 0)
        def _():
            cv_tail[...] = jnp.zeros_like(cv_tail)
            xb_tail[...] = jnp.zeros_like(xb_tail)
            h_last[...] = jnp.zeros_like(h_last)

        def seg(k):
            return p_ref[:, k * d:(k + 1) * d].astype(F32)

        def vrow(k):
            return vec_ref[k:k + 1, :]

        b_a, c_a, v_a, x_b, g_b, u_a, u_b = (seg(k) for k in range(7))
        cv = c_a * v_a
        prev_cv = cv_tail[...]
        conv_a = (vrow(V_CONV_A_B) + vrow(V_CONV_A_W) * _shift_down(cv, 2, prev_cv)
                  + vrow(V_CONV_A_W + 1) * _shift_down(cv, 1, prev_cv) + vrow(V_CONV_A_W + 2) * cv)
        cv_tail[...] = cv[ts - SUBLANES:]
        y_a = b_a * conv_a
        prev_xb = xb_tail[...]
        xb2 = (vrow(V_CONV_B_B) + vrow(V_CONV_B_W) * _shift_down(x_b, 3, prev_xb)
               + vrow(V_CONV_B_W + 1) * _shift_down(x_b, 2, prev_xb)
               + vrow(V_CONV_B_W + 2) * _shift_down(x_b, 1, prev_xb) + vrow(V_CONV_B_W + 3) * x_b)
        xb_tail[...] = x_b[ts - SUBLANES:]
        _, gi, _, a, mult = _lru_coeffs(xb2, wg_ref, vec_ref, n_head, bw)
        h_loc, a_cum = _scan_fwd(a, mult * gi * xb2)
        hh = h_loc + a_cum * h_last[SUBLANES - 1:SUBLANES, :]
        h_last[...] = hh[ts - SUBLANES:]
        gel, _ = _gelu(g_b)
        y_b = hh * gel
        ya_b, yb_b = y_a.astype(BF16), y_b.astype(BF16)
        pa = _dot(ya_b, wa_ref[...])
        pb = _dot(yb_b, wb_ref[...])
        m = (_sigmoid(u_a) * pa + _sigmoid(u_b) * pb).astype(BF16)
        y = _dot(m, wo_ref[...])
        yn, _ = _rms(y)
        gg = mod_ref[M_GT_M:M_GT_M + 1, :] * vrow(V_G_POST_MIX)
        x1_ref[...] = x_ref[...] + yn * gg
        conva_ref[...] = conv_a.astype(BF16)
        xb2_ref[...] = xb2
        hh_ref[...] = hh
        ya_ref[...] = ya_b
        yb_ref[...] = yb_b
        pa_ref[...] = pa.astype(BF16)
        pb_ref[...] = pb.astype(BF16)
        m_ref[...] = m
        y_ref[...] = y.astype(BF16)

    tile = pl.BlockSpec((ts, d), lambda i: (i, 0))
    sd = lambda dt: jax.ShapeDtypeStruct((s, d), dt)
    return pl.pallas_call(
        body, name=name, grid=(s // ts,),
        out_shape=[sd(F32), sd(BF16), sd(F32), sd(F32), sd(BF16), sd(BF16), sd(BF16), sd(BF16), sd(BF16), sd(BF16)],
        in_specs=[tile, pl.BlockSpec((ts, 7 * d), lambda i: (i, 0)), _full(mod.shape), _full(vec.shape),
                  _full(wg.shape), _full(w_a_out.shape), _full(w_b_out.shape), _full(w_o.shape)],
        out_specs=[tile] * 10,
        scratch_shapes=[pltpu.VMEM((SUBLANES, d), F32), pltpu.VMEM((SUBLANES, d), F32), pltpu.VMEM((SUBLANES, d), F32)],
        compiler_params=_cparams(1),
    )(x, proj, mod, vec, wg, w_a_out, w_b_out, w_o)


def _mlp_forward(x1, mod, vec, w_up, w_down, name):
    s, d = x1.shape
    fq = w_up.shape[2]
    ts = min(TOKENS_MATMUL_TILE, s)

    def body(x_ref, mod_ref, vec_ref, wu_ref, wd_ref, x2_ref, h2_ref, up_ref, y2_ref):
        x = x_ref[...]
        xn, _ = _rms(x)
        gm = vec_ref[V_G_PRE_MLP:V_G_PRE_MLP + 1, :] * (1.0 + mod_ref[M_SC_F:M_SC_F + 1, :])
        h2 = (xn * gm + mod_ref[M_SH_F:M_SH_F + 1, :]).astype(BF16)
        h2_ref[...] = h2
        y2 = jnp.zeros((ts, d), F32)
        for qb in range(N_CHIP):
            up = _dot(h2, wu_ref[qb])
            up_ref[:, qb * fq:(qb + 1) * fq] = up.astype(BF16)
            ru = jnp.maximum(up, 0.0)
            y2 = y2 + _dot((ru * ru).astype(BF16), wd_ref[qb])
        y2_ref[...] = y2.astype(BF16)
        yn, _ = _rms(y2)
        gg = mod_ref[M_GT_F:M_GT_F + 1, :] * vec_ref[V_G_POST_MLP:V_G_POST_MLP + 1, :]
        x2_ref[...] = x + yn * gg

    tile = pl.BlockSpec((ts, d), lambda i: (i, 0))
    return pl.pallas_call(
        body, name=name, grid=(s // ts,),
        out_shape=[jax.ShapeDtypeStruct((s, d), F32), jax.ShapeDtypeStruct((s, d), BF16),
                   jax.ShapeDtypeStruct((s, N_CHIP * fq), BF16), jax.ShapeDtypeStruct((s, d), BF16)],
        in_specs=[tile, _full(mod.shape), _full(vec.shape), _full(w_up.shape), _full(w_down.shape)],
        out_specs=[tile, tile, pl.BlockSpec((ts, N_CHIP * fq), lambda i: (i, 0)), tile],
        compiler_params=_cparams(1),
    )(x1, mod, vec, w_up, w_down)


def _loss_head(xf, target, name):
    s, d = xf.shape
    ts = min(TOKENS_MATMUL_TILE, s)

    def body(x_ref, t_ref, dx_ref, loss_ref):
        @pl.when(pl.program_id(0) == 0)
        def _():
            loss_ref[...] = jnp.zeros_like(loss_ref)

        err = x_ref[...] - t_ref[...]
        dx_ref[...] = err * (1.0 / d)
        part = jnp.sum(jnp.sum(err * err, axis=1, keepdims=True), axis=0, keepdims=True) * (0.5 / d)
        loss_ref[...] = loss_ref[...] + part

    tile = pl.BlockSpec((ts, d), lambda i: (i, 0))
    return pl.pallas_call(
        body, name=name, grid=(s // ts,),
        out_shape=[jax.ShapeDtypeStruct((s, d), F32), jax.ShapeDtypeStruct((SUBLANES, 128), F32)],
        in_specs=[tile, tile], out_specs=[tile, _full((SUBLANES, 128))],
        compiler_params=_cparams(1),
    )(xf, target)


SB3_DSH, SB3_DSC, SB3_DGT, SB3_DG_PRE, SB3_DG_POST = range(5)
SB1_DSH, SB1_DSC, SB1_DG_PRE = range(3)
(SB2_DGT, SB2_DG_POST, SB2_DWA, SB2_DBA, SB2_DWB, SB2_DBB, SB2_DLAM, SB2_DBR, SB2_DBI) = (0, 1, 2, 5, 6, 10, 11, 12, 13)


def _mlp_backward(dx2, x1, y2, up, mod, vec, w_up, w_down, name):
    s, d = dx2.shape
    fq = w_up.shape[2]
    ts = min(TOKENS_MIXER_TILE, s)
    n_t = s // ts

    def body(dx2_ref, x_ref, y2_ref, up_ref, mod_ref, vec_ref, wu_ref, wd_ref,
             dx1_ref, dy2_ref, dup_ref, act_ref, small_ref):
        i = pl.program_id(0)

        @pl.when(i == 0)
        def _():
            small_ref[...] = jnp.zeros_like(small_ref)

        dout = dx2_ref[...]
        y2n, ry = _rms(y2_ref[...].astype(F32))
        g_post = vec_ref[V_G_POST_MLP:V_G_POST_MLP + 1, :]
        gt = mod_ref[M_GT_F:M_GT_F + 1, :]
        dgg = _colsum(dout * y2n)
        dy2 = _rms_bwd(dout * (gt * g_post), y2n, ry).astype(BF16)
        dy2_ref[...] = dy2
        dh2 = jnp.zeros((ts, d), F32)
        for qb in range(N_CHIP):
            cols = slice(qb * fq, (qb + 1) * fq)
            dact = _dot_tb(dy2, wd_ref[qb])
            ru = jnp.maximum(up_ref[:, cols].astype(F32), 0.0)
            dup = (dact * (2.0 * ru)).astype(BF16)
            dup_ref[:, cols] = dup
            act_ref[:, cols] = (ru * ru).astype(BF16)
            dh2 = dh2 + _dot_tb(dup, wu_ref[qb])
        xn, r = _rms(x_ref[...])
        g_pre = vec_ref[V_G_PRE_MLP:V_G_PRE_MLP + 1, :]
        sc1 = 1.0 + mod_ref[M_SC_F:M_SC_F + 1, :]
        dsh = _colsum(dh2)
        dgm = _colsum(dh2 * xn)
        dx1_ref[...] = dout + _rms_bwd(dh2 * (g_pre * sc1), xn, r)
        small_ref[SB3_DSH:SB3_DSH + 1, :] += dsh
        small_ref[SB3_DSC:SB3_DSC + 1, :] += dgm
        small_ref[SB3_DGT:SB3_DGT + 1, :] += dgg

        @pl.when(i == n_t - 1)
        def _():
            dgm_t = small_ref[SB3_DSC:SB3_DSC + 1, :]
            dgg_t = small_ref[SB3_DGT:SB3_DGT + 1, :]
            small_ref[SB3_DSC:SB3_DSC + 1, :] = dgm_t * g_pre
            small_ref[SB3_DG_PRE:SB3_DG_PRE + 1, :] = dgm_t * sc1
            small_ref[SB3_DGT:SB3_DGT + 1, :] = dgg_t * g_post
            small_ref[SB3_DG_POST:SB3_DG_POST + 1, :] = dgg_t * gt

    tile = pl.BlockSpec((ts, d), lambda i: (i, 0))
    wide = pl.BlockSpec((ts, N_CHIP * fq), lambda i: (i, 0))
    return pl.pallas_call(
        body, name=name, grid=(n_t,),
        out_shape=[jax.ShapeDtypeStruct((s, d), F32), jax.ShapeDtypeStruct((s, d), BF16),
                   jax.ShapeDtypeStruct((s, N_CHIP * fq), BF16), jax.ShapeDtypeStruct((s, N_CHIP * fq), BF16),
                   jax.ShapeDtypeStruct((SUBLANES, d), F32)],
        in_specs=[tile, tile, tile, wide, _full(mod.shape), _full(vec.shape), _full(w_up.shape), _full(w_down.shape)],
        out_specs=[tile, tile, wide, wide, _full((SUBLANES, d))],
        compiler_params=_cparams(1),
    )(dx2, x1, y2, up, mod, vec, w_up, w_down)


def _mixer_backward(dx1, proj, conva, xb2s, hhs, pas, pbs, ys, mod, vec, wg, w_a_out, w_b_out, w_o, name):
    s, d = dx1.shape
    n_head, bw, _ = wg.shape
    ts = min(TOKENS_MIXER_TILE, s)
    n_t = s // ts

    def body(dx1_ref, p_ref, conva_ref, xb2_ref, hh_ref, pa_ref, pb_ref, y_ref, mod_ref, vec_ref,
             wg_ref, wa_ref, wb_ref, wo_ref,
             dp_ref, dy_ref, dpa_ref, dpb_ref, small_ref, dwg_ref,
             dconv_head, dxb2_head, a_head, g_head):
        i = pl.program_id(0)

        @pl.when(i == 0)
        def _():
            small_ref[...] = jnp.zeros_like(small_ref)
            dwg_ref[...] = jnp.zeros_like(dwg_ref)
            dconv_head[...] = jnp.zeros_like(dconv_head)
            dxb2_head[...] = jnp.zeros_like(dxb2_head)
            a_head[...] = jnp.zeros_like(a_head)
            g_head[...] = jnp.zeros_like(g_head)

        def seg(k):
            return p_ref[:, k * d:(k + 1) * d].astype(F32)

        def vrow(k):
            return vec_ref[k:k + 1, :]

        def acc(row, val):
            small_ref[row:row + 1, :] += val

        dout = dx1_ref[...]
        yn, ry = _rms(y_ref[...].astype(F32))
        g_post = vrow(V_G_POST_MIX)
        gt = mod_ref[M_GT_M:M_GT_M + 1, :]
        acc(SB2_DGT, _colsum(dout * yn))
        dy = _rms_bwd(dout * (gt * g_post), yn, ry).astype(BF16)
        dy_ref[...] = dy
        dm = _dot_tb(dy, wo_ref[...])
        u_a, u_b = seg(5), seg(6)
        sa, sb = _sigmoid(u_a), _sigmoid(u_b)
        dpa = (dm * sa).astype(BF16)
        dpb = (dm * sb).astype(BF16)
        dpa_ref[...] = dpa
        dpb_ref[...] = dpb
        du_a = dm * pa_ref[...].astype(F32) * (sa * (1.0 - sa))
        du_b = dm * pb_ref[...].astype(F32) * (sb * (1.0 - sb))
        dp_ref[:, 5 * d:6 * d] = du_a.astype(BF16)
        dp_ref[:, 6 * d:7 * d] = du_b.astype(BF16)
        dy_a = _dot_tb(dpa, wa_ref[...])
        dy_b = _dot_tb(dpb, wb_ref[...])

        b_a, c_a, v_a = seg(0), seg(1), seg(2)
        dp_ref[:, 0:d] = (dy_a * conva_ref[...].astype(F32)).astype(BF16)
        dconv = dy_a * b_a
        nxt = dconv_head[...]
        d1 = _shift_up(dconv, 1, nxt)
        d2 = _shift_up(dconv, 2, nxt)
        dconv_head[...] = dconv[:SUBLANES]
        dcv = vrow(V_CONV_A_W + 2) * dconv + vrow(V_CONV_A_W + 1) * d1 + vrow(V_CONV_A_W) * d2
        cv = c_a * v_a
        acc(SB2_DWA + 2, _colsum(cv * dconv))
        acc(SB2_DWA + 1, _colsum(cv * d1))
        acc(SB2_DWA, _colsum(cv * d2))
        acc(SB2_DBA, _colsum(dconv))
        dp_ref[:, d:2 * d] = (dcv * v_a).astype(BF16)
        dp_ref[:, 2 * d:3 * d] = (dcv * c_a).astype(BF16)

        x_b, g_b = seg(3), seg(4)
        hh = hh_ref[...]
        gel, th = _gelu(g_b)
        dp_ref[:, 4 * d:5 * d] = (dy_b * hh * _gelu_grad(g_b, th)).astype(BF16)
        dhh = dy_b * gel
        xb2 = xb2_ref[...]
        r, gi, sp, a, mult = _lru_coeffs(xb2, wg_ref, vec_ref, n_head, bw)
        a_next = _shift_up(a, 1, a_head[...])
        g_loc, a_cum = _scan_bwd(a_next, dhh)
        g = g_loc + a_cum * g_head[0:1, :]
        a_head[...] = a[:SUBLANES]
        g_head[...] = g[:SUBLANES]
        bb = mult * gi * xb2
        dlog_a = g * (hh - bb) - g * gi * xb2 * (a * a / mult)
        dgi = g * mult * xb2
        dxb2 = g * mult * gi
        acc(SB2_DLAM, _colsum(dlog_a * r))
        dzr = dlog_a * ((-LRU_C) * sp) * (r * (1.0 - r))
        dzi = dgi * (gi * (1.0 - gi))
        acc(SB2_DBR, _colsum(dzr))
        acc(SB2_DBI, _colsum(dzi))
        xb2_b = xb2.astype(BF16)
        back = []
        for hd in range(n_head):
            cols = slice(hd * bw, (hd + 1) * bw)
            dz = jnp.concatenate([dzr[:, cols], dzi[:, cols]], axis=1).astype(BF16)
            back.append(_dot_tb(dz, wg_ref[hd]))
            dwg_ref[hd] += _dot_ta(xb2_b[:, cols], dz)
        dxb2 = dxb2 + jnp.concatenate(back, axis=1)
        nxt = dxb2_head[...]
        e1 = _shift_up(dxb2, 1, nxt)
        e2 = _shift_up(dxb2, 2, nxt)
        e3 = _shift_up(dxb2, 3, nxt)
        dxb2_head[...] = dxb2[:SUBLANES]
        dp_ref[:, 3 * d:4 * d] = (vrow(V_CONV_B_W + 3) * dxb2 + vrow(V_CONV_B_W + 2) * e1
                                  + vrow(V_CONV_B_W + 1) * e2 + vrow(V_CONV_B_W) * e3).astype(BF16)
        acc(SB2_DWB + 3, _colsum(x_b * dxb2))
        acc(SB2_DWB + 2, _colsum(x_b * e1))
        acc(SB2_DWB + 1, _colsum(x_b * e2))
        acc(SB2_DWB, _colsum(x_b * e3))
        acc(SB2_DBB, _colsum(dxb2))

        @pl.when(i == n_t - 1)
        def _():
            dgg_t = small_ref[SB2_DGT:SB2_DGT + 1, :]
            small_ref[SB2_DGT:SB2_DGT + 1, :] = dgg_t * g_post
            small_ref[SB2_DG_POST:SB2_DG_POST + 1, :] = dgg_t * gt
            lam = vrow(V_LAMBDA)
            small_ref[SB2_DLAM:SB2_DLAM + 1, :] = small_ref[SB2_DLAM:SB2_DLAM + 1, :] * (LRU_C * _sigmoid(-lam))

    rev = lambda i: (n_t - 1 - i, 0)
    tile = pl.BlockSpec((ts, d), rev)
    wide = pl.BlockSpec((ts, 7 * d), rev)
    sd = lambda dt: jax.ShapeDtypeStruct((s, d), dt)
    return pl.pallas_call(
        body, name=name, grid=(n_t,),
        out_shape=[jax.ShapeDtypeStruct((s, 7 * d), BF16), sd(BF16), sd(BF16), sd(BF16),
                   jax.ShapeDtypeStruct((2 * SUBLANES, d), F32), jax.ShapeDtypeStruct(wg.shape, F32)],
        in_specs=[tile, wide, tile, tile, tile, tile, tile, tile, _full(mod.shape), _full(vec.shape),
                  _full(wg.shape), _full(w_a_out.shape), _full(w_b_out.shape), _full(w_o.shape)],
        out_specs=[wide, tile, tile, tile, _full((2 * SUBLANES, d)), _full(wg.shape)],
        scratch_shapes=[pltpu.VMEM((SUBLANES, d), F32)] * 4,
        compiler_params=_cparams(1),
    )(dx1, proj, conva, xb2s, hhs, pas, pbs, ys, mod, vec, wg, w_a_out, w_b_out, w_o)


def _proj_backward(dproj, dx1, x, mod, vec, w_in, name):
    s, d = x.shape
    nq = w_in.shape[2]
    ts = min(TOKENS_MATMUL_TILE, s)
    n_t = s // ts

    def body(dp_ref, dx1_ref, x_ref, mod_ref, vec_ref, w_ref, dx_ref, small_ref):
        i = pl.program_id(0)

        @pl.when(i == 0)
        def _():
            small_ref[...] = jnp.zeros_like(small_ref)

        dh = jnp.zeros((ts, d), F32)
        for qb in range(N_CHIP):
            dh = dh + _dot_tb(dp_ref[:, qb * nq:(qb + 1) * nq], w_ref[qb])
        xn, r = _rms(x_ref[...])
        g_pre = vec_ref[V_G_PRE_MIX:V_G_PRE_MIX + 1, :]
        sc1 = 1.0 + mod_ref[M_SC_M:M_SC_M + 1, :]
        dx_ref[...] = dx1_ref[...] + _rms_bwd(dh * (g_pre * sc1), xn, r)
        small_ref[SB1_DSH:SB1_DSH + 1, :] += _colsum(dh)
        small_ref[SB1_DSC:SB1_DSC + 1, :] += _colsum(dh * xn)

        @pl.when(i == n_t - 1)
        def _():
            dgm_t = small_ref[SB1_DSC:SB1_DSC + 1, :]
            small_ref[SB1_DSC:SB1_DSC + 1, :] = dgm_t * g_pre
            small_ref[SB1_DG_PRE:SB1_DG_PRE + 1, :] = dgm_t * sc1

    tile = pl.BlockSpec((ts, d), lambda i: (i, 0))
    return pl.pallas_call(
        body, name=name, grid=(n_t,),
        out_shape=[jax.ShapeDtypeStruct((s, d), F32), jax.ShapeDtypeStruct((SUBLANES, d), F32)],
        in_specs=[pl.BlockSpec((ts, N_CHIP * nq), lambda i: (i, 0)), tile, tile, _full(mod.shape), _full(vec.shape),
                  _full(w_in.shape)],
        out_specs=[tile, _full((SUBLANES, d))],
        compiler_params=_cparams(1),
    )(dproj, dx1, x, mod, vec, w_in)


def _weight_grad(a, b, name, col_blocks=1, tk=512):
    s, k = a.shape
    n = b.shape[1]
    tn = n // col_blocks
    tk = min(tk, k)

    def body(a_ref, b_ref, o_ref):
        o_ref[0] = _dot_ta(a_ref[...], b_ref[...])

    return pl.pallas_call(
        body, name=name, grid=(col_blocks, k // tk),
        out_shape=jax.ShapeDtypeStruct((col_blocks, k, tn), F32),
        in_specs=[pl.BlockSpec((s, tk), lambda j, i: (0, i)), pl.BlockSpec((s, tn), lambda j, i: (0, j))],
        out_specs=pl.BlockSpec((1, tk, tn), lambda j, i: (j, i, 0)),
        compiler_params=_cparams(2),
    )(a, b)


def _adamw(w, g, m, v, name):
    shape = w.shape
    cols = shape[-1]
    rows = w.size // cols
    tr = _row_tile(rows, cols, target_bytes=1024 * 1024)
    c1 = 1.0 - ADAM_B1 ** ADAM_STEP
    c2 = 1.0 - ADAM_B2 ** ADAM_STEP

    def body(w_ref, g_ref, m_ref, v_ref, d_ref, nm_ref, nv_ref):
        gv = g_ref[...]
        nm = ADAM_B1 * m_ref[...] + (1.0 - ADAM_B1) * gv
        nv = ADAM_B2 * v_ref[...] + (1.0 - ADAM_B2) * (gv * gv)
        nm_ref[...] = nm
        nv_ref[...] = nv
        d_ref[...] = (-ADAM_LR) * ((nm / c1) / (jnp.sqrt(nv / c2) + ADAM_EPS) + ADAM_WD * w_ref[...])

    spec = pl.BlockSpec((tr, cols), lambda i: (i, 0))
    outs = pl.pallas_call(
        body, name=name, grid=(rows // tr,),
        out_shape=[jax.ShapeDtypeStruct((rows, cols), F32)] * 3,
        in_specs=[spec] * 4, out_specs=[spec] * 3,
        compiler_params=_cparams(1),
    )(*(t.reshape(rows, cols) for t in (w, g, m, v)))
    return tuple(o.reshape(shape) for o in outs)


def kernel(x, c, w_mod, b_mod, g_pre_mix, g_post_mix, w_in, conv_a_w, conv_a_b, w_a_out, conv_b_w, conv_b_b, w_gate_r, b_gate_r, w_gate_i, b_gate_i, lru_lambda, w_b_out, w_o, g_pre_mlp, g_post_mlp, w_mlp_up, w_mlp_down, loss_target, m_w_mod, m_b_mod, m_g_pre_mix, m_g_post_mix, m_w_in, m_conv_a_w, m_conv_a_b, m_w_a_out, m_conv_b_w, m_conv_b_b, m_w_gate_r, m_b_gate_r, m_w_gate_i, m_b_gate_i, m_lru_lambda, m_w_b_out, m_w_o, m_g_pre_mlp, m_g_post_mlp, m_w_mlp_up, m_w_mlp_down, v_w_mod, v_b_mod, v_g_pre_mix, v_g_post_mix, v_w_in, v_conv_a_w, v_conv_a_b, v_w_a_out, v_conv_b_w, v_conv_b_b, v_w_gate_r, v_b_gate_r, v_w_gate_i, v_b_gate_i, v_lru_lambda, v_w_b_out, v_w_o, v_g_pre_mlp, v_g_post_mlp, v_w_mlp_up, v_w_mlp_down):
    weights = dict(w_mod=w_mod, b_mod=b_mod, g_pre_mix=g_pre_mix, g_post_mix=g_post_mix, w_in=w_in, conv_a_w=conv_a_w,
                   conv_a_b=conv_a_b, w_a_out=w_a_out, conv_b_w=conv_b_w, conv_b_b=conv_b_b, w_gate_r=w_gate_r,
                   b_gate_r=b_gate_r, w_gate_i=w_gate_i, b_gate_i=b_gate_i, lru_lambda=lru_lambda, w_b_out=w_b_out,
                   w_o=w_o, g_pre_mlp=g_pre_mlp, g_post_mlp=g_post_mlp, w_mlp_up=w_mlp_up, w_mlp_down=w_mlp_down)
    mom1 = dict(w_mod=m_w_mod, b_mod=m_b_mod, g_pre_mix=m_g_pre_mix, g_post_mix=m_g_post_mix, w_in=m_w_in,
                conv_a_w=m_conv_a_w, conv_a_b=m_conv_a_b, w_a_out=m_w_a_out, conv_b_w=m_conv_b_w, conv_b_b=m_conv_b_b,
                w_gate_r=m_w_gate_r, b_gate_r=m_b_gate_r, w_gate_i=m_w_gate_i, b_gate_i=m_b_gate_i,
                lru_lambda=m_lru_lambda, w_b_out=m_w_b_out, w_o=m_w_o, g_pre_mlp=m_g_pre_mlp, g_post_mlp=m_g_post_mlp,
                w_mlp_up=m_w_mlp_up, w_mlp_down=m_w_mlp_down)
    mom2 = dict(w_mod=v_w_mod, b_mod=v_b_mod, g_pre_mix=v_g_pre_mix, g_post_mix=v_g_post_mix, w_in=v_w_in,
                conv_a_w=v_conv_a_w, conv_a_b=v_conv_a_b, w_a_out=v_w_a_out, conv_b_w=v_conv_b_w, conv_b_b=v_conv_b_b,
                w_gate_r=v_w_gate_r, b_gate_r=v_b_gate_r, w_gate_i=v_w_gate_i, b_gate_i=v_b_gate_i,
                lru_lambda=v_lru_lambda, w_b_out=v_w_b_out, w_o=v_w_o, g_pre_mlp=v_g_pre_mlp, g_post_mlp=v_g_post_mlp,
                w_mlp_up=v_w_mlp_up, w_mlp_down=v_w_mlp_down)
    names = list(weights)

    n_layer = w_in.shape[0]
    s, d = x.shape[1], x.shape[2]
    n_head, bw = w_gate_r.shape[1], w_gate_r.shape[2]
    dq = d // N_CHIP
    mq = w_mod.shape[2]
    n_mod = (N_CHIP * mq) // d
    ka, kb = conv_a_w.shape[1], conv_b_w.shape[1]

    mx, my, mc = _place()
    q_me = 2 * mx + my
    c_arr = jnp.reshape(mc, (1,)).astype(jnp.int32)
    q_arr = jnp.reshape(q_me, (1,)).astype(jnp.int32)

    n_conv_rows = n_layer * (ka + kb)
    blk_rows = -(-(1 + n_conv_rows) // SUBLANES) * SUBLANES
    conv_rows = jnp.concatenate([jnp.concatenate([conv_a_w[l], conv_b_w[l]], axis=0) for l in range(n_layer)], axis=0)
    conv_rows = jnp.pad(conv_rows, ((0, blk_rows - 1 - n_conv_rows), (0, d - dq)))
    gathered1 = _all_gather_small(jnp.concatenate([c[0:1], conv_rows], axis=0), "gather_c_conv").reshape(N_DEV, blk_rows, d)
    c_all = gathered1[:, 0, :]
    conv_full = jnp.concatenate([gathered1[2 * qb, 1:1 + n_conv_rows, :dq] for qb in range(N_CHIP)], axis=1)

    b_mod_shard = lax.dynamic_slice_in_dim(b_mod, q_me * mq, mq, axis=1)
    mod_part = _mod_forward(c_all, w_mod, b_mod_shard, "mod_forward")
    gathered2 = _all_gather_small(mod_part, "gather_mod").reshape(N_DEV, n_layer, N_DEV, mq)
    me = 4 * mx + 2 * my + mc
    mod_rows = jnp.concatenate(
        [lax.dynamic_index_in_dim(gathered2[2 * qb], me, axis=1, keepdims=False) for qb in range(N_CHIP)], axis=1)
    mods = [jnp.pad(mod_rows[l].reshape(n_mod, d), ((0, SUBLANES - n_mod), (0, 0))) for l in range(n_layer)]

    vecs = []
    for l in range(n_layer):
        base = l * (ka + kb)
        rows = [g_pre_mix[l], g_post_mix[l], conv_a_b[l], conv_b_b[l], b_gate_r[l], b_gate_i[l], lru_lambda[l],
                g_pre_mlp[l], g_post_mlp[l]]
        vecs.append(jnp.concatenate([jnp.stack(rows, axis=0), conv_full[base:base + ka + kb]], axis=0))

    big = dict(w_in=w_in.astype(BF16), w_a_out=w_a_out.astype(BF16), w_b_out=w_b_out.astype(BF16),
               w_o=w_o.astype(BF16), w_mlp_up=w_mlp_up.astype(BF16), w_mlp_down=w_mlp_down.astype(BF16))
    big_names = list(big)
    gathered_w = _gather_weights([(big[nm], l) for l in range(n_layer) for nm in big_names], "gather_weights")
    wfull = [dict(zip(big_names, gathered_w[l * len(big_names):(l + 1) * len(big_names)])) for l in range(n_layer)]
    for wl in wfull:
        for nm in ("w_a_out", "w_b_out", "w_o"):
            wl[nm] = wl[nm].reshape(d, d)
    wgs = [jnp.concatenate([w_gate_r[l], w_gate_i[l]], axis=-1).astype(BF16) for l in range(n_layer)]

    xs = x[0]
    saved = []
    for l in range(n_layer):
        wl = wfull[l]
        h, proj = _norm_proj(xs, mods[l], vecs[l], wl["w_in"], f"norm_proj_{l}")
        x1, conva, xb2, hh, ya, yb, pa, pb, mm, yy = _mixer_forward(
            xs, proj, mods[l], vecs[l], wgs[l], wl["w_a_out"], wl["w_b_out"], wl["w_o"], f"mixer_forward_{l}")
        x2, h2, up, y2 = _mlp_forward(x1, mods[l], vecs[l], wl["w_mlp_up"], wl["w_mlp_down"], f"mlp_forward_{l}")
        saved.append(dict(x=xs, h=h, proj=proj, x1=x1, conva=conva, xb2=xb2, hh=hh, ya=ya, yb=yb, pa=pa, pb=pb, m=mm,
                          y=yy, h2=h2, up=up, y2=y2))
        xs = x2
    dxs, loss_tile = _loss_head(xs, loss_target[0], "loss_head")
    loss = lax.psum(loss_tile[0, 0], ("x", "y", "c"))

    wgrads = [None] * n_layer
    smalls = [None] * n_layer
    for l in reversed(range(n_layer)):
        wl, sv = wfull[l], saved[l]
        dx1, dy2, dup, act, small3 = _mlp_backward(dxs, sv["x1"], sv["y2"], sv["up"], mods[l], vecs[l],
                                                   wl["w_mlp_up"], wl["w_mlp_down"], f"mlp_backward_{l}")
        dproj, dy, dpa, dpb, small2, dwg = _mixer_backward(
            dx1, sv["proj"], sv["conva"], sv["xb2"], sv["hh"], sv["pa"], sv["pb"], sv["y"], mods[l], vecs[l], wgs[l],
            wl["w_a_out"], wl["w_b_out"], wl["w_o"], f"mixer_backward_{l}")
        dxs, small1 = _proj_backward(dproj, dx1, sv["x"], mods[l], vecs[l], wl["w_in"], f"proj_backward_{l}")
        rowblk = lambda t: t.reshape(N_CHIP, t.shape[1] // N_CHIP, t.shape[2])
        wgrads[l] = dict(
            w_in=_weight_grad(sv["h"], dproj, f"grad_w_in_{l}", col_blocks=N_CHIP),
            w_a_out=rowblk(_weight_grad(sv["ya"], dpa, f"grad_w_a_out_{l}")),
            w_b_out=rowblk(_weight_grad(sv["yb"], dpb, f"grad_w_b_out_{l}")),
            w_o=rowblk(_weight_grad(sv["m"], dy, f"grad_w_o_{l}")),
            w_mlp_up=_weight_grad(sv["h2"], dup, f"grad_w_mlp_up_{l}", col_blocks=N_CHIP),
            w_mlp_down=rowblk(_weight_grad(act, dy2, f"grad_w_mlp_down_{l}")))
        smalls[l] = (small1, small2, small3, dwg)
    grad_x = dxs[None]

    dmod = jnp.stack([jnp.concatenate([smalls[l][0][SB1_DSH], smalls[l][0][SB1_DSC], smalls[l][1][SB2_DGT],
                                       smalls[l][2][SB3_DSH], smalls[l][2][SB3_DSC], smalls[l][2][SB3_DGT]], axis=0)
                      for l in range(n_layer)], axis=0)
    dmod_blk = jnp.pad(dmod, ((0, SUBLANES - n_layer), (0, 0)))
    dmod_gathered = _all_gather_small(dmod_blk, "gather_dmod")
    dmod_all = jnp.swapaxes(dmod_gathered.reshape(N_DEV, SUBLANES, N_CHIP * mq)[:, :n_layer, :], 0, 1)
    grads = {}
    grads["w_mod"] = _mod_backward(c_all.T, lax.dynamic_slice_in_dim(dmod_all, q_me * mq, mq, axis=2), "mod_backward")
    grads["b_mod"] = _sum_devices(dmod_gathered, "sum_b_mod")[:n_layer]

    per_layer = []
    for l in range(n_layer):
        s1, s2, s3, dwg = smalls[l]
        per_layer.append(jnp.concatenate([
            s1[SB1_DG_PRE:SB1_DG_PRE + 1], s2[SB2_DG_POST:SB2_DG_POST + 1], s2[SB2_DWA:SB2_DWA + 12],
            s3[SB3_DG_PRE:SB3_DG_PRE + 2], dwg.reshape(-1, d)], axis=0))
    n_small = per_layer[0].shape[0]
    small_all = _all_gather_small(jnp.concatenate(per_layer, axis=0), "gather_small_grads")
    small_sum = _sum_devices(small_all, "sum_small_grads").reshape(n_layer, n_small, d)
    grads["g_pre_mix"] = small_sum[:, 0]
    grads["g_post_mix"] = small_sum[:, 1]
    grads["conv_a_w"] = lax.dynamic_slice_in_dim(small_sum[:, 2:2 + ka], q_me * dq, dq, axis=2)
    grads["conv_a_b"] = small_sum[:, 5]
    grads["conv_b_w"] = lax.dynamic_slice_in_dim(small_sum[:, 6:6 + kb], q_me * dq, dq, axis=2)
    grads["conv_b_b"] = small_sum[:, 10]
    grads["lru_lambda"] = small_sum[:, 11]
    grads["b_gate_r"] = small_sum[:, 12]
    grads["b_gate_i"] = small_sum[:, 13]
    grads["g_pre_mlp"] = small_sum[:, 14]
    grads["g_post_mlp"] = small_sum[:, 15]
    dwg_sum = small_sum[:, 16:].reshape(n_layer, n_head, bw, 2 * bw)
    grads["w_gate_r"] = dwg_sum[..., :bw]
    grads["w_gate_i"] = dwg_sum[..., bw:]

    flat = [wgrads[l][nm] for nm in big_names for l in range(n_layer)]
    landed = _exchange_halves(flat, "rs_exchange_halves")
    chip_sums = [_add_sibling_half(g, ld, c_arr, f"rs_add_sibling_{k}") for k, (g, ld) in enumerate(zip(flat, landed))]
    from_chips = _scatter_to_chips(chip_sums, "rs_scatter_to_chips")
    halves = [_add_chips(sm, ld, q_arr, f"rs_add_chips_{k}") for k, (sm, ld) in enumerate(zip(chip_sums, from_chips))]
    joined = _join_halves([halves[k * n_layer:(k + 1) * n_layer] for k in range(len(big_names))], "rs_join_halves")
    for nm, gj in zip(big_names, joined):
        grads[nm] = gj.reshape(weights[nm].shape)

    deltas, new_m, new_v = {}, {}, {}
    for nm in names:
        deltas[nm], new_m[nm], new_v[nm] = _adamw(weights[nm], grads[nm], mom1[nm], mom2[nm], f"adamw_{nm}")
    return (loss, grad_x, *[grads[nm] for nm in names], *[deltas[nm] for nm in names],
            *[new_m[nm] for nm in names], *[new_v[nm] for nm in names])
```

```python
import functools

import jax
import jax.numpy as jnp
from jax import lax
from jax.experimental import pallas as pl
from jax.experimental.pallas import tpu as pltpu

F32 = jnp.float32
BF16 = jnp.bfloat16
MESH = pl.DeviceIdType.MESH

EPS = 1e-6
LRU_C = 8.0
N_CHIP = 4
N_DEV = 8
ADAM_LR = 0.001
ADAM_B1 = 0.9
ADAM_B2 = 0.999
ADAM_EPS = 1e-08
ADAM_WD = 0.01
ADAM_STEP = 10

VMEM_LIMIT_BYTES = 56 * 1024 * 1024
SUBLANES = 8
TOKENS_MATMUL_TILE = 512
TOKENS_MIXER_TILE = 256
GELU_K0 = 0.7978845608028654
GELU_K1 = 0.044715

V_G_PRE_MIX, V_G_POST_MIX, V_CONV_A_B, V_CONV_B_B, V_B_GATE_R, V_B_GATE_I, V_LAMBDA, V_G_PRE_MLP, V_G_POST_MLP = range(9)
V_CONV_A_W = 9
V_CONV_B_W = 12
M_SH_M, M_SC_M, M_GT_M, M_SH_F, M_SC_F, M_GT_F = range(6)


def _cparams(n_grid=0):
    sem = ("arbitrary",) * n_grid if n_grid else None
    return pltpu.CompilerParams(dimension_semantics=sem, vmem_limit_bytes=VMEM_LIMIT_BYTES)


def _full(shape):
    return pl.BlockSpec(shape, lambda *_: (0,) * len(shape))


def _dot(a, b):
    return jnp.dot(a, b, preferred_element_type=F32)


def _dot_tb(a, b):
    return lax.dot_general(a, b, (((1,), (1,)), ((), ())), preferred_element_type=F32)


def _dot_ta(a, b):
    return lax.dot_general(a, b, (((0,), (0,)), ((), ())), preferred_element_type=F32)


def _sigmoid(x):
    return 1.0 / (1.0 + jnp.exp(-x))


def _softplus(x):
    return jnp.maximum(x, 0.0) + jnp.log1p(jnp.exp(-jnp.abs(x)))


def _neg_expm1(x):
    series = -x * (1.0 + 0.5 * x * (1.0 + (x / 3.0) * (1.0 + 0.25 * x)))
    return jnp.where(x > -1e-2, series, 1.0 - jnp.exp(x))


def _gelu(x):
    t = jnp.tanh(GELU_K0 * (x + GELU_K1 * x * x * x))
    return 0.5 * x * (1.0 + t), t


def _gelu_grad(x, t):
    return 0.5 * (1.0 + t) + 0.5 * x * (1.0 - t * t) * GELU_K0 * (1.0 + 3.0 * GELU_K1 * x * x)


def _rms(x):
    r = lax.rsqrt(jnp.mean(x * x, axis=-1, keepdims=True) + EPS)
    return x * r, r


def _rms_bwd(dxn, xn, r):
    return r * (dxn - xn * jnp.mean(dxn * xn, axis=-1, keepdims=True))


def _colsum(x):
    return jnp.sum(x, axis=0, keepdims=True)


def _rows(t, w):
    return lax.broadcasted_iota(jnp.int32, (t, w), 0)


def _shift_down(x, k, prev8):
    t, w = x.shape
    rolled = pltpu.roll(x, k, 0)
    head = jnp.where(_rows(SUBLANES, w) < k, pltpu.roll(prev8, k, 0), rolled[:SUBLANES])
    return jnp.concatenate([head, rolled[SUBLANES:]], axis=0)


def _shift_up(x, k, next8):
    t, w = x.shape
    rolled = pltpu.roll(x, t - k, 0)
    tail = jnp.where(_rows(SUBLANES, w) >= SUBLANES - k, pltpu.roll(next8, SUBLANES - k, 0), rolled[t - SUBLANES:])
    return jnp.concatenate([rolled[:t - SUBLANES], tail], axis=0)


def _scan_fwd(a, b):
    t, w = a.shape
    rows = _rows(t, w)
    s = 1
    while s < t:
        keep = rows >= s
        b = b + a * jnp.where(keep, pltpu.roll(b, s, 0), 0.0)
        a = a * jnp.where(keep, pltpu.roll(a, s, 0), 1.0)
        s *= 2
    return b, a


def _scan_bwd(a, b):
    t, w = a.shape
    rows = _rows(t, w)
    s = 1
    while s < t:
        keep = rows < t - s
        b = b + a * jnp.where(keep, pltpu.roll(b, t - s, 0), 0.0)
        a = a * jnp.where(keep, pltpu.roll(a, t - s, 0), 1.0)
        s *= 2
    return b, a


def _row_tile(rows, cols, itemsize=4, target_bytes=2 * 1024 * 1024):
    if rows * cols * itemsize <= target_bytes or rows % SUBLANES:
        return rows
    t = max(SUBLANES, (target_bytes // (cols * itemsize)) // SUBLANES * SUBLANES)
    while rows % t:
        t -= SUBLANES
    return t


def _place():
    return lax.axis_index("x"), lax.axis_index("y"), lax.axis_index("c")


def _other_chips(x, y):
    chips = [(1 - x, y), (x, 1 - y), (1 - x, 1 - y)]
    return chips, [2 * cx + cy for cx, cy in chips]


def _all_gather_small(block, name):
    m_per, n = block.shape

    def body(x_ref, out_ref, send_sems, recv_sems, local_sem):
        x, y, c = _place()
        me, sibling = (x, y, c), (x, y, 1 - c)
        chips, _ = _other_chips(x, y)

        def rows(px, py, pc):
            return out_ref.at[pl.ds((4 * px + 2 * py + pc) * m_per, m_per), :]

        def copy(k, blk, to, src=None):
            return pltpu.make_async_remote_copy(
                src_ref=rows(*blk) if src is None else src, dst_ref=rows(*blk),
                send_sem=send_sems.at[k], recv_sem=recv_sems.at[k], device_id=to, device_id_type=MESH)

        mine = pltpu.make_async_copy(x_ref, rows(*me), local_sem)
        mine.start()
        first = [copy(0, me, sibling, src=x_ref)]
        first += [copy(1 + j, me, (*chip, c), src=x_ref) for j, chip in enumerate(chips)]
        for cp in first:
            cp.start()
        passed = [copy(4 + j, (*chip, c), sibling) for j, chip in enumerate(chips)]
        for j, chip in enumerate(chips):
            copy(1 + j, (*chip, c), me).wait_recv()
            passed[j].start()
        copy(0, sibling, me).wait_recv()
        for j, chip in enumerate(chips):
            copy(4 + j, (*chip, 1 - c), me).wait_recv()
        for cp in first + passed:
            cp.wait_send()
        mine.wait()

    return pl.pallas_call(
        body, name=name,
        out_shape=jax.ShapeDtypeStruct((N_DEV * m_per, n), block.dtype),
        in_specs=[pl.BlockSpec(memory_space=pltpu.VMEM)],
        out_specs=pl.BlockSpec(memory_space=pltpu.VMEM),
        scratch_shapes=[pltpu.SemaphoreType.DMA((7,)), pltpu.SemaphoreType.DMA((7,)), pltpu.SemaphoreType.DMA],
        compiler_params=pltpu.CompilerParams(vmem_limit_bytes=VMEM_LIMIT_BYTES),
    )(block)


def _cast_place(w, layer, q_arr, name):
    _, r, cols = w.shape
    tr = _row_tile(r, cols)

    def body(q_ref, w_ref, o_ref):
        o_ref[...] = w_ref[...].astype(BF16)

    return pl.pallas_call(
        body, name=name,
        out_shape=jax.ShapeDtypeStruct((N_CHIP, r, cols), BF16),
        grid_spec=pltpu.PrefetchScalarGridSpec(
            num_scalar_prefetch=1, grid=(r // tr,),
            in_specs=[pl.BlockSpec((1, tr, cols), lambda i, q_ref: (layer, i, 0))],
            out_specs=pl.BlockSpec((1, tr, cols), lambda i, q_ref: (q_ref[0], i, 0))),
        compiler_params=_cparams(1),
    )(q_arr, w)


def _gather_weights(bufs, name):
    n = len(bufs)

    def body(*refs):
        o_refs = refs[n:2 * n]
        send_sems, recv_sems = refs[2 * n:]
        x, y, c = _place()
        q = 2 * x + y
        sibling = (x, y, 1 - c)
        chips, qs = _other_chips(x, y)

        def half(w, shard, pc):
            rh = bufs[w].shape[1] // 2
            return o_refs[w].at[shard, pl.ds(pc * rh, rh), :]

        def over_ici(w, j, shard):
            blk = half(w, shard, c)
            return pltpu.make_async_remote_copy(
                src_ref=blk, dst_ref=blk, send_sem=send_sems.at[w, j], recv_sem=recv_sems.at[w, j],
                device_id=(*chips[j], c), device_id_type=MESH)

        def to_sibling(w, j, pc):
            blk = half(w, qs[j], pc)
            return pltpu.make_async_remote_copy(
                src_ref=blk, dst_ref=blk, send_sem=send_sems.at[w, 3 + j], recv_sem=recv_sems.at[w, 3 + j],
                device_id=sibling, device_id_type=MESH)

        sent = []
        for w in range(n):
            for j in range(3):
                cp = over_ici(w, j, q)
                cp.start()
                sent.append(cp)
        for w in range(n):
            for j in range(3):
                over_ici(w, j, qs[j]).wait_recv()
                cp = to_sibling(w, j, c)
                cp.start()
                sent.append(cp)
        for w in range(n):
            for j in range(3):
                to_sibling(w, j, 1 - c).wait_recv()
        for cp in sent:
            cp.wait_send()

    any_spec = pl.BlockSpec(memory_space=pl.ANY)
    return pl.pallas_call(
        body, name=name,
        out_shape=[jax.ShapeDtypeStruct(b.shape, b.dtype) for b in bufs],
        in_specs=[any_spec] * n, out_specs=[any_spec] * n,
        input_output_aliases={w: w for w in range(n)},
        scratch_shapes=[pltpu.SemaphoreType.DMA((n, 6)), pltpu.SemaphoreType.DMA((n, 6))],
    )(*bufs)


def _exchange_halves(grads, name):
    n = len(grads)

    def body(*refs):
        g_refs, l_refs = refs[:n], refs[n:2 * n]
        send_sems, recv_sems = refs[2 * n:]
        x, y, c = _place()
        sibling = (x, y, 1 - c)
        cps = []
        for w in range(n):
            rh = grads[w].shape[1] // 2
            cp = pltpu.make_async_remote_copy(
                src_ref=g_refs[w].at[:, pl.ds((1 - c) * rh, rh), :], dst_ref=l_refs[w],
                send_sem=send_sems.at[w], recv_sem=recv_sems.at[w], device_id=sibling, device_id_type=MESH)
            cp.start()
            cps.append(cp)
        for cp in cps:
            cp.wait()

    any_spec = pl.BlockSpec(memory_space=pl.ANY)
    return pl.pallas_call(
        body, name=name,
        out_shape=[jax.ShapeDtypeStruct((N_CHIP, g.shape[1] // 2, g.shape[2]), g.dtype) for g in grads],
        in_specs=[any_spec] * n, out_specs=[any_spec] * n,
        scratch_shapes=[pltpu.SemaphoreType.DMA((n,)), pltpu.SemaphoreType.DMA((n,))],
    )(*grads)


def _scatter_to_chips(sums, name):
    n = len(sums)

    def body(*refs):
        s_refs, l_refs = refs[:n], refs[n:2 * n]
        send_sems, recv_sems = refs[2 * n:]
        x, y, c = _place()
        chips, _ = _other_chips(x, y)
        cps = []
        for w in range(n):
            for j in range(3):
                cp = pltpu.make_async_remote_copy(
                    src_ref=s_refs[w].at[j], dst_ref=l_refs[w].at[j],
                    send_sem=send_sems.at[w, j], recv_sem=recv_sems.at[w, j], device_id=(*chips[j], c), device_id_type=MESH)
                cp.start()
                cps.append(cp)
        for cp in cps:
            cp.wait()

    any_spec = pl.BlockSpec(memory_space=pl.ANY)
    return pl.pallas_call(
        body, name=name,
        out_shape=[jax.ShapeDtypeStruct((3,) + s.shape[1:], s.dtype) for s in sums],
        in_specs=[any_spec] * n, out_specs=[any_spec] * n,
        scratch_shapes=[pltpu.SemaphoreType.DMA((n, 3)), pltpu.SemaphoreType.DMA((n, 3))],
    )(*sums)


def _join_halves(outs, name):
    n = len(outs)

    def body(*refs):
        o_refs = refs[n:2 * n]
        send_sems, recv_sems = refs[2 * n:]
        x, y, c = _place()
        sibling = (x, y, 1 - c)
        cps = []
        for w in range(n):
            n_layer, r, _ = outs[w].shape
            for layer in range(n_layer):
                mine = o_refs[w].at[layer, pl.ds(c * (r // 2), r // 2), :]
                cp = pltpu.make_async_remote_copy(
                    src_ref=mine, dst_ref=mine, send_sem=send_sems.at[w, layer], recv_sem=recv_sems.at[w, layer],
                    device_id=sibling, device_id_type=MESH)
                cp.start()
                cps.append(cp)
        for cp in cps:
            cp.wait_send()
        for w in range(n):
            n_layer, r, _ = outs[w].shape
            for layer in range(n_layer):
                theirs = o_refs[w].at[layer, pl.ds((1 - c) * (r // 2), r // 2), :]
                pltpu.make_async_remote_copy(
                    src_ref=theirs, dst_ref=theirs, send_sem=send_sems.at[w, layer], recv_sem=recv_sems.at[w, layer],
                    device_id=sibling, device_id_type=MESH).wait_recv()

    n_layer = outs[0].shape[0]
    any_spec = pl.BlockSpec(memory_space=pl.ANY)
    return pl.pallas_call(
        body, name=name,
        out_shape=[jax.ShapeDtypeStruct(o.shape, o.dtype) for o in outs],
        in_specs=[any_spec] * n, out_specs=[any_spec] * n,
        input_output_aliases={w: w for w in range(n)},
        scratch_shapes=[pltpu.SemaphoreType.DMA((n, n_layer)), pltpu.SemaphoreType.DMA((n, n_layer))],
    )(*outs)


PF_C, PF_Q, PF_QS = 0, 1, 2


def _add_sibling_half(g, landed, pf, name):
    _, r, cols = g.shape
    rh = r // 2
    tr = _row_tile(rh, cols)
    nr = rh // tr

    def body(pf_ref, g_ref, l_ref, o_ref):
        o_ref[...] = (g_ref[...] + l_ref[...]).astype(BF16)

    return pl.pallas_call(
        body, name=name,
        out_shape=jax.ShapeDtypeStruct((3, rh, cols), BF16),
        grid_spec=pltpu.PrefetchScalarGridSpec(
            num_scalar_prefetch=1, grid=(3, nr),
            in_specs=[pl.BlockSpec((1, tr, cols), lambda j, i, pf_ref: (pf_ref[PF_QS + j], pf_ref[PF_C] * nr + i, 0)),
                      pl.BlockSpec((1, tr, cols), lambda j, i, pf_ref: (pf_ref[PF_QS + j], i, 0))],
            out_specs=pl.BlockSpec((1, tr, cols), lambda j, i, pf_ref: (j, i, 0))),
        compiler_params=_cparams(2),
    )(pf, g, landed)


def _add_chips(g, landed, from_chips, pf, prev, layer, n_layer, name):
    _, r, cols = g.shape
    rh = r // 2
    tr = _row_tile(rh, cols)
    nr = rh // tr

    def body(pf_ref, g_ref, l_ref, f_ref, *rest):
        o_ref = rest[-1]
        acc = g_ref[0] + l_ref[0]
        for j in range(3):
            acc = acc + f_ref[j].astype(F32)
        o_ref[0] = acc

    in_specs = [pl.BlockSpec((1, tr, cols), lambda i, pf_ref: (pf_ref[PF_Q], pf_ref[PF_C] * nr + i, 0)),
                pl.BlockSpec((1, tr, cols), lambda i, pf_ref: (pf_ref[PF_Q], i, 0)),
                pl.BlockSpec((3, tr, cols), lambda i, pf_ref: (0, i, 0))]
    args = [pf, g, landed, from_chips]
    aliases = {}
    if prev is not None:
        in_specs.append(pl.BlockSpec(memory_space=pl.ANY))
        args.append(prev)
        aliases = {4: 0}
    return pl.pallas_call(
        body, name=name,
        out_shape=jax.ShapeDtypeStruct((n_layer, r, cols), F32),
        grid_spec=pltpu.PrefetchScalarGridSpec(
            num_scalar_prefetch=1, grid=(nr,), in_specs=in_specs,
            out_specs=pl.BlockSpec((1, tr, cols), lambda i, pf_ref: (layer, pf_ref[PF_C] * nr + i, 0))),
        input_output_aliases=aliases,
        compiler_params=_cparams(1),
    )(*args)


def _sum_devices(gathered, name):
    rows, n = gathered.shape
    m = rows // N_DEV
    tr = _row_tile(m, n, target_bytes=512 * 1024)
    g3 = gathered.reshape(N_DEV, m, n)

    def body(g_ref, o_ref):
        acc = g_ref[0]
        for d in range(1, N_DEV):
            acc = acc + g_ref[d]
        o_ref[...] = acc

    return pl.pallas_call(
        body, name=name, grid=(m // tr,),
        out_shape=jax.ShapeDtypeStruct((m, n), F32),
        in_specs=[pl.BlockSpec((N_DEV, tr, n), lambda i: (0, i, 0))],
        out_specs=pl.BlockSpec((tr, n), lambda i: (i, 0)),
        compiler_params=_cparams(1),
    )(g3)


def _mod_forward(c_all, w_mod, b_mod_shard, name):
    n_layer, d, mq = w_mod.shape

    def body(c_ref, w_ref, b_ref, o_ref):
        cv = c_ref[...]
        o_ref[...] = _dot(cv * _sigmoid(cv), w_ref[0]) + b_ref[0]

    return pl.pallas_call(
        body, name=name, grid=(n_layer,),
        out_shape=jax.ShapeDtypeStruct((n_layer * N_DEV, mq), F32),
        in_specs=[_full((N_DEV, d)), pl.BlockSpec((1, d, mq), lambda l: (l, 0, 0)),
                  pl.BlockSpec((1, 1, mq), lambda l: (l, 0, 0))],
        out_specs=pl.BlockSpec((N_DEV, mq), lambda l: (l, 0)),
        compiler_params=_cparams(1),
    )(c_all, w_mod, b_mod_shard.reshape(n_layer, 1, mq))


def _mod_backward(c_all_t, dmod_shard, name):
    n_layer, _, mq = dmod_shard.shape
    d = c_all_t.shape[0]

    def body(c_ref, dm_ref, o_ref):
        cv = c_ref[...]
        o_ref[0] = _dot(cv * _sigmoid(cv), dm_ref[0])

    return pl.pallas_call(
        body, name=name, grid=(n_layer,),
        out_shape=jax.ShapeDtypeStruct((n_layer, d, mq), F32),
        in_specs=[_full((d, N_DEV)), pl.BlockSpec((1, N_DEV, mq), lambda l: (l, 0, 0))],
        out_specs=pl.BlockSpec((1, d, mq), lambda l: (l, 0, 0)),
        compiler_params=_cparams(1),
    )(c_all_t, dmod_shard)


def _norm_proj(x, mod, vec, w_in, name):
    s, d = x.shape
    nq = w_in.shape[2]
    ts = min(TOKENS_MATMUL_TILE, s)

    def body(x_ref, mod_ref, vec_ref, w_ref, h_ref, p_ref):
        xn, _ = _rms(x_ref[...])
        gm = vec_ref[V_G_PRE_MIX:V_G_PRE_MIX + 1, :] * (1.0 + mod_ref[M_SC_M:M_SC_M + 1, :])
        h = (xn * gm + mod_ref[M_SH_M:M_SH_M + 1, :]).astype(BF16)
        h_ref[...] = h
        for qb in range(N_CHIP):
            p_ref[:, qb * nq:(qb + 1) * nq] = _dot(h, w_ref[qb]).astype(BF16)

    return pl.pallas_call(
        body, name=name, grid=(s // ts,),
        out_shape=[jax.ShapeDtypeStruct((s, d), BF16), jax.ShapeDtypeStruct((s, N_CHIP * nq), BF16)],
        in_specs=[pl.BlockSpec((ts, d), lambda i: (i, 0)), _full(mod.shape), _full(vec.shape), _full(w_in.shape)],
        out_specs=[pl.BlockSpec((ts, d), lambda i: (i, 0)), pl.BlockSpec((ts, N_CHIP * nq), lambda i: (i, 0))],
        compiler_params=_cparams(1),
    )(x, mod, vec, w_in)


def _gate_pre(xb2_b, wg_ref, n_head, bw):
    zr, zi = [], []
    for hd in range(n_head):
        z = _dot(xb2_b[:, hd * bw:(hd + 1) * bw], wg_ref[hd])
        zr.append(z[:, :bw])
        zi.append(z[:, bw:])
    return jnp.concatenate(zr, axis=1), jnp.concatenate(zi, axis=1)


def _lru_coeffs(xb2, wg_ref, vec_ref, n_head, bw):
    zr, zi = _gate_pre(xb2.astype(BF16), wg_ref, n_head, bw)
    r = _sigmoid(zr + vec_ref[V_B_GATE_R:V_B_GATE_R + 1, :])
    gi = _sigmoid(zi + vec_ref[V_B_GATE_I:V_B_GATE_I + 1, :])
    sp = _softplus(-vec_ref[V_LAMBDA:V_LAMBDA + 1, :])
    log_a = (-LRU_C) * r * sp
    a = jnp.exp(log_a)
    mult = jnp.sqrt(_neg_expm1(2.0 * log_a))
    return r, gi, sp, a, mult


def _mixer_forward(x, proj, mod, vec, wg, w_a_out, w_b_out, w_o, name):
    s, d = x.shape
    n_head, bw, _ = wg.shape
    ts = min(TOKENS_MIXER_TILE, s)

    def body(x_ref, p_ref, mod_ref, vec_ref, wg_ref, wa_ref, wb_ref, wo_ref,
             x1_ref, conva_ref, xb2_ref, hh_ref, ya_ref, yb_ref, pa_ref, pb_ref, m_ref, y_ref,
             cv_tail, xb_tail, h_last):
        i = pl.program_id(0)

        @pl.when(i == 0)
        def _():
            cv_tail[...] = jnp.zeros_like(cv_tail)
            xb_tail[...] = jnp.zeros_like(xb_tail)
            h_last[...] = jnp.zeros_like(h_last)

        def seg(k):
            return p_ref[:, k * d:(k + 1) * d].astype(F32)

        def vrow(k):
            return vec_ref[k:k + 1, :]

        b_a, c_a, v_a, x_b, g_b, u_a, u_b = (seg(k) for k in range(7))
        cv = c_a * v_a
        prev_cv = cv_tail[...]
        conv_a = (vrow(V_CONV_A_B) + vrow(V_CONV_A_W) * _shift_down(cv, 2, prev_cv)
                  + vrow(V_CONV_A_W + 1) * _shift_down(cv, 1, prev_cv) + vrow(V_CONV_A_W + 2) * cv)
        cv_tail[...] = cv[ts - SUBLANES:]
        y_a = b_a * conv_a
        prev_xb = xb_tail[...]
        xb2 = (vrow(V_CONV_B_B) + vrow(V_CONV_B_W) * _shift_down(x_b, 3, prev_xb)
               + vrow(V_CONV_B_W + 1) * _shift_down(x_b, 2, prev_xb)
               + vrow(V_CONV_B_W + 2) * _shift_down(x_b, 1, prev_xb) + vrow(V_CONV_B_W + 3) * x_b)
        xb_tail[...] = x_b[ts - SUBLANES:]
        _, gi, _, a, mult = _lru_coeffs(xb2, wg_ref, vec_ref, n_head, bw)
        h_loc, a_cum = _scan_fwd(a, mult * gi * xb2)
        hh = h_loc + a_cum * h_last[SUBLANES - 1:SUBLANES, :]
        h_last[...] = hh[ts - SUBLANES:]
        gel, _ = _gelu(g_b)
        y_b = hh * gel
        ya_b, yb_b = y_a.astype(BF16), y_b.astype(BF16)
        pa = _dot(ya_b, wa_ref[...])
        pb = _dot(yb_b, wb_ref[...])
        m = (_sigmoid(u_a) * pa + _sigmoid(u_b) * pb).astype(BF16)
        y = _dot(m, wo_ref[...])
        yn, _ = _rms(y)
        gg = mod_ref[M_GT_M:M_GT_M + 1, :] * vrow(V_G_POST_MIX)
        x1_ref[...] = x_ref[...] + yn * gg
        conva_ref[...] = conv_a.astype(BF16)
        xb2_ref[...] = xb2
        hh_ref[...] = hh
        ya_ref[...] = ya_b
        yb_ref[...] = yb_b
        pa_ref[...] = pa.astype(BF16)
        pb_ref[...] = pb.astype(BF16)
        m_ref[...] = m
        y_ref[...] = y.astype(BF16)

    tile = pl.BlockSpec((ts, d), lambda i: (i, 0))
    sd = lambda dt: jax.ShapeDtypeStruct((s, d), dt)
    return pl.pallas_call(
        body, name=name, grid=(s // ts,),
        out_shape=[sd(F32), sd(BF16), sd(F32), sd(F32), sd(BF16), sd(BF16), sd(BF16), sd(BF16), sd(BF16), sd(BF16)],
        in_specs=[tile, pl.BlockSpec((ts, 7 * d), lambda i: (i, 0)), _full(mod.shape), _full(vec.shape),
                  _full(wg.shape), _full(w_a_out.shape), _full(w_b_out.shape), _full(w_o.shape)],
        out_specs=[tile] * 10,
        scratch_shapes=[pltpu.VMEM((SUBLANES, d), F32), pltpu.VMEM((SUBLANES, d), F32), pltpu.VMEM((SUBLANES, d), F32)],
        compiler_params=_cparams(1),
    )(x, proj, mod, vec, wg, w_a_out, w_b_out, w_o)


def _mlp_forward(x1, mod, vec, w_up, w_down, name):
    s, d = x1.shape
    fq = w_up.shape[2]
    ts = min(TOKENS_MATMUL_TILE, s)

    def body(x_ref, mod_ref, vec_ref, wu_ref, wd_ref, x2_ref, h2_ref, up_ref, y2_ref):
        x = x_ref[...]
        xn, _ = _rms(x)
        gm = vec_ref[V_G_PRE_MLP:V_G_PRE_MLP + 1, :] * (1.0 + mod_ref[M_SC_F:M_SC_F + 1, :])
        h2 = (xn * gm + mod_ref[M_SH_F:M_SH_F + 1, :]).astype(BF16)
        h2_ref[...] = h2
        y2 = jnp.zeros((ts, d), F32)
        for qb in range(N_CHIP):
            up = _dot(h2, wu_ref[qb])
            up_ref[:, qb * fq:(qb + 1) * fq] = up.astype(BF16)
            ru = jnp.maximum(up, 0.0)
            y2 = y2 + _dot((ru * ru).astype(BF16), wd_ref[qb])
        y2_ref[...] = y2.astype(BF16)
        yn, _ = _rms(y2)
        gg = mod_ref[M_GT_F:M_GT_F + 1, :] * vec_ref[V_G_POST_MLP:V_G_POST_MLP + 1, :]
        x2_ref[...] = x + yn * gg

    tile = pl.BlockSpec((ts, d), lambda i: (i, 0))
    return pl.pallas_call(
        body, name=name, grid=(s // ts,),
        out_shape=[jax.ShapeDtypeStruct((s, d), F32), jax.ShapeDtypeStruct((s, d), BF16),
                   jax.ShapeDtypeStruct((s, N_CHIP * fq), BF16), jax.ShapeDtypeStruct((s, d), BF16)],
        in_specs=[tile, _full(mod.shape), _full(vec.shape), _full(w_up.shape), _full(w_down.shape)],
        out_specs=[tile, tile, pl.BlockSpec((ts, N_CHIP * fq), lambda i: (i, 0)), tile],
        compiler_params=_cparams(1),
    )(x1, mod, vec, w_up, w_down)


def _loss_head(xf, target, name):
    s, d = xf.shape
    ts = min(TOKENS_MATMUL_TILE, s)

    def body(x_ref, t_ref, dx_ref, loss_ref):
        @pl.when(pl.program_id(0) == 0)
        def _():
            loss_ref[...] = jnp.zeros_like(loss_ref)

        err = x_ref[...] - t_ref[...]
        dx_ref[...] = err * (1.0 / d)
        part = jnp.sum(jnp.sum(err * err, axis=1, keepdims=True), axis=0, keepdims=True) * (0.5 / d)
        loss_ref[...] = loss_ref[...] + part

    tile = pl.BlockSpec((ts, d), lambda i: (i, 0))
    return pl.pallas_call(
        body, name=name, grid=(s // ts,),
        out_shape=[jax.ShapeDtypeStruct((s, d), F32), jax.ShapeDtypeStruct((SUBLANES, 128), F32)],
        in_specs=[tile, tile], out_specs=[tile, _full((SUBLANES, 128))],
        compiler_params=_cparams(1),
    )(xf, target)


SB3_DSH, SB3_DSC, SB3_DGT, SB3_DG_PRE, SB3_DG_POST = range(5)
SB1_DSH, SB1_DSC, SB1_DG_PRE = range(3)
(SB2_DGT, SB2_DG_POST, SB2_DWA, SB2_DBA, SB2_DWB, SB2_DBB, SB2_DLAM, SB2_DBR, SB2_DBI) = (0, 1, 2, 5, 6, 10, 11, 12, 13)


def _mlp_backward(dx2, x1, y2, up, mod, vec, w_up, w_down, name):
    s, d = dx2.shape
    fq = w_up.shape[2]
    ts = min(TOKENS_MIXER_TILE, s)
    n_t = s // ts

    def body(dx2_ref, x_ref, y2_ref, up_ref, mod_ref, vec_ref, wu_ref, wd_ref,
             dx1_ref, dy2_ref, dup_ref, act_ref, small_ref):
        i = pl.program_id(0)

        @pl.when(i == 0)
        def _():
            small_ref[...] = jnp.zeros_like(small_ref)

        dout = dx2_ref[...]
        y2n, ry = _rms(y2_ref[...].astype(F32))
        g_post = vec_ref[V_G_POST_MLP:V_G_POST_MLP + 1, :]
        gt = mod_ref[M_GT_F:M_GT_F + 1, :]
        dgg = _colsum(dout * y2n)
        dy2 = _rms_bwd(dout * (gt * g_post), y2n, ry).astype(BF16)
        dy2_ref[...] = dy2
        dh2 = jnp.zeros((ts, d), F32)
        for qb in range(N_CHIP):
            cols = slice(qb * fq, (qb + 1) * fq)
            dact = _dot_tb(dy2, wd_ref[qb])
            ru = jnp.maximum(up_ref[:, cols].astype(F32), 0.0)
            dup = (dact * (2.0 * ru)).astype(BF16)
            dup_ref[:, cols] = dup
            act_ref[:, cols] = (ru * ru).astype(BF16)
            dh2 = dh2 + _dot_tb(dup, wu_ref[qb])
        xn, r = _rms(x_ref[...])
        g_pre = vec_ref[V_G_PRE_MLP:V_G_PRE_MLP + 1, :]
        sc1 = 1.0 + mod_ref[M_SC_F:M_SC_F + 1, :]
        dsh = _colsum(dh2)
        dgm = _colsum(dh2 * xn)
        dx1_ref[...] = dout + _rms_bwd(dh2 * (g_pre * sc1), xn, r)
        small_ref[SB3_DSH:SB3_DSH + 1, :] += dsh
        small_ref[SB3_DSC:SB3_DSC + 1, :] += dgm
        small_ref[SB3_DGT:SB3_DGT + 1, :] += dgg

        @pl.when(i == n_t - 1)
        def _():
            dgm_t = small_ref[SB3_DSC:SB3_DSC + 1, :]
            dgg_t = small_ref[SB3_DGT:SB3_DGT + 1, :]
            small_ref[SB3_DSC:SB3_DSC + 1, :] = dgm_t * g_pre
            small_ref[SB3_DG_PRE:SB3_DG_PRE + 1, :] = dgm_t * sc1
            small_ref[SB3_DGT:SB3_DGT + 1, :] = dgg_t * g_post
            small_ref[SB3_DG_POST:SB3_DG_POST + 1, :] = dgg_t * gt

    tile = pl.BlockSpec((ts, d), lambda i: (i, 0))
    wide = pl.BlockSpec((ts, N_CHIP * fq), lambda i: (i, 0))
    return pl.pallas_call(
        body, name=name, grid=(n_t,),
        out_shape=[jax.ShapeDtypeStruct((s, d), F32), jax.ShapeDtypeStruct((s, d), BF16),
                   jax.ShapeDtypeStruct((s, N_CHIP * fq), BF16), jax.ShapeDtypeStruct((s, N_CHIP * fq), BF16),
                   jax.ShapeDtypeStruct((SUBLANES, d), F32)],
        in_specs=[tile, tile, tile, wide, _full(mod.shape), _full(vec.shape), _full(w_up.shape), _full(w_down.shape)],
        out_specs=[tile, tile, wide, wide, _full((SUBLANES, d))],
        compiler_params=_cparams(1),
    )(dx2, x1, y2, up, mod, vec, w_up, w_down)


def _mixer_backward(dx1, proj, conva, xb2s, hhs, pas, pbs, ys, mod, vec, wg, w_a_out, w_b_out, w_o, name):
    s, d = dx1.shape
    n_head, bw, _ = wg.shape
    ts = min(TOKENS_MIXER_TILE, s)
    n_t = s // ts

    def body(dx1_ref, p_ref, conva_ref, xb2_ref, hh_ref, pa_ref, pb_ref, y_ref, mod_ref, vec_ref,
             wg_ref, wa_ref, wb_ref, wo_ref,
             dp_ref, dy_ref, dpa_ref, dpb_ref, small_ref, dwg_ref,
             dconv_head, dxb2_head, a_head, g_head):
        i = pl.program_id(0)

        @pl.when(i == 0)
        def _():
            small_ref[...] = jnp.zeros_like(small_ref)
            dwg_ref[...] = jnp.zeros_like(dwg_ref)
            dconv_head[...] = jnp.zeros_like(dconv_head)
            dxb2_head[...] = jnp.zeros_like(dxb2_head)
            a_head[...] = jnp.zeros_like(a_head)
            g_head[...] = jnp.zeros_like(g_head)

        def seg(k):
            return p_ref[:, k * d:(k + 1) * d].astype(F32)

        def vrow(k):
            return vec_ref[k:k + 1, :]

        def acc(row, val):
            small_ref[row:row + 1, :] += val

        dout = dx1_ref[...]
        yn, ry = _rms(y_ref[...].astype(F32))
        g_post = vrow(V_G_POST_MIX)
        gt = mod_ref[M_GT_M:M_GT_M + 1, :]
        acc(SB2_DGT, _colsum(dout * yn))
        dy = _rms_bwd(dout * (gt * g_post), yn, ry).astype(BF16)
        dy_ref[...] = dy
        dm = _dot_tb(dy, wo_ref[...])
        u_a, u_b = seg(5), seg(6)
        sa, sb = _sigmoid(u_a), _sigmoid(u_b)
        dpa = (dm * sa).astype(BF16)
        dpb = (dm * sb).astype(BF16)
        dpa_ref[...] = dpa
        dpb_ref[...] = dpb
        du_a = dm * pa_ref[...].astype(F32) * (sa * (1.0 - sa))
        du_b = dm * pb_ref[...].astype(F32) * (sb * (1.0 - sb))
        dp_ref[:, 5 * d:6 * d] = du_a.astype(BF16)
        dp_ref[:, 6 * d:7 * d] = du_b.astype(BF16)
        dy_a = _dot_tb(dpa, wa_ref[...])
        dy_b = _dot_tb(dpb, wb_ref[...])

        b_a, c_a, v_a = seg(0), seg(1), seg(2)
        dp_ref[:, 0:d] = (dy_a * conva_ref[...].astype(F32)).astype(BF16)
        dconv = dy_a * b_a
        nxt = dconv_head[...]
        d1 = _shift_up(dconv, 1, nxt)
        d2 = _shift_up(dconv, 2, nxt)
        dconv_head[...] = dconv[:SUBLANES]
        dcv = vrow(V_CONV_A_W + 2) * dconv + vrow(V_CONV_A_W + 1) * d1 + vrow(V_CONV_A_W) * d2
        cv = c_a * v_a
        acc(SB2_DWA + 2, _colsum(cv * dconv))
        acc(SB2_DWA + 1, _colsum(cv * d1))
        acc(SB2_DWA, _colsum(cv * d2))
        acc(SB2_DBA, _colsum(dconv))
        dp_ref[:, d:2 * d] = (dcv * v_a).astype(BF16)
        dp_ref[:, 2 * d:3 * d] = (dcv * c_a).astype(BF16)

        x_b, g_b = seg(3), seg(4)
        hh = hh_ref[...]
        gel, th = _gelu(g_b)
        dp_ref[:, 4 * d:5 * d] = (dy_b * hh * _gelu_grad(g_b, th)).astype(BF16)
        dhh = dy_b * gel
        xb2 = xb2_ref[...]
        r, gi, sp, a, mult = _lru_coeffs(xb2, wg_ref, vec_ref, n_head, bw)
        a_next = _shift_up(a, 1, a_head[...])
        g_loc, a_cum = _scan_bwd(a_next, dhh)
        g = g_loc + a_cum * g_head[0:1, :]
        a_head[...] = a[:SUBLANES]
        g_head[...] = g[:SUBLANES]
        bb = mult * gi * xb2
        dlog_a = g * (hh - bb) - g * gi * xb2 * (a * a / mult)
        dgi = g * mult * xb2
        dxb2 = g * mult * gi
        acc(SB2_DLAM, _colsum(dlog_a * r))
        dzr = dlog_a * ((-LRU_C) * sp) * (r * (1.0 - r))
        dzi = dgi * (gi * (1.0 - gi))
        acc(SB2_DBR, _colsum(dzr))
        acc(SB2_DBI, _colsum(dzi))
        xb2_b = xb2.astype(BF16)
        back = []
        for hd in range(n_head):
            cols = slice(hd * bw, (hd + 1) * bw)
            dz = jnp.concatenate([dzr[:, cols], dzi[:, cols]], axis=1).astype(BF16)
            back.append(_dot_tb(dz, wg_ref[hd]))
            dwg_ref[hd] += _dot_ta(xb2_b[:, cols], dz)
        dxb2 = dxb2 + jnp.concatenate(back, axis=1)
        nxt = dxb2_head[...]
        e1 = _shift_up(dxb2, 1, nxt)
        e2 = _shift_up(dxb2, 2, nxt)
        e3 = _shift_up(dxb2, 3, nxt)
        dxb2_head[...] = dxb2[:SUBLANES]
        dp_ref[:, 3 * d:4 * d] = (vrow(V_CONV_B_W + 3) * dxb2 + vrow(V_CONV_B_W + 2) * e1
                                  + vrow(V_CONV_B_W + 1) * e2 + vrow(V_CONV_B_W) * e3).astype(BF16)
        acc(SB2_DWB + 3, _colsum(x_b * dxb2))
        acc(SB2_DWB + 2, _colsum(x_b * e1))
        acc(SB2_DWB + 1, _colsum(x_b * e2))
        acc(SB2_DWB, _colsum(x_b * e3))
        acc(SB2_DBB, _colsum(dxb2))

        @pl.when(i == n_t - 1)
        def _():
            dgg_t = small_ref[SB2_DGT:SB2_DGT + 1, :]
            small_ref[SB2_DGT:SB2_DGT + 1, :] = dgg_t * g_post
            small_ref[SB2_DG_POST:SB2_DG_POST + 1, :] = dgg_t * gt
            lam = vrow(V_LAMBDA)
            small_ref[SB2_DLAM:SB2_DLAM + 1, :] = small_ref[SB2_DLAM:SB2_DLAM + 1, :] * (LRU_C * _sigmoid(-lam))

    rev = lambda i: (n_t - 1 - i, 0)
    tile = pl.BlockSpec((ts, d), rev)
    wide = pl.BlockSpec((ts, 7 * d), rev)
    sd = lambda dt: jax.ShapeDtypeStruct((s, d), dt)
    return pl.pallas_call(
        body, name=name, grid=(n_t,),
        out_shape=[jax.ShapeDtypeStruct((s, 7 * d), BF16), sd(BF16), sd(BF16), sd(BF16),
                   jax.ShapeDtypeStruct((2 * SUBLANES, d), F32), jax.ShapeDtypeStruct(wg.shape, F32)],
        in_specs=[tile, wide, tile, tile, tile, tile, tile, tile, _full(mod.shape), _full(vec.shape),
                  _full(wg.shape), _full(w_a_out.shape), _full(w_b_out.shape), _full(w_o.shape)],
        out_specs=[wide, tile, tile, tile, _full((2 * SUBLANES, d)), _full(wg.shape)],
        scratch_shapes=[pltpu.VMEM((SUBLANES, d), F32)] * 4,
        compiler_params=_cparams(1),
    )(dx1, proj, conva, xb2s, hhs, pas, pbs, ys, mod, vec, wg, w_a_out, w_b_out, w_o)


def _proj_backward(dproj, dx1, x, mod, vec, w_in, name):
    s, d = x.shape
    nq = w_in.shape[2]
    ts = min(TOKENS_MATMUL_TILE, s)
    n_t = s // ts

    def body(dp_ref, dx1_ref, x_ref, mod_ref, vec_ref, w_ref, dx_ref, small_ref):
        i = pl.program_id(0)

        @pl.when(i == 0)
        def _():
            small_ref[...] = jnp.zeros_like(small_ref)

        dh = jnp.zeros((ts, d), F32)
        for qb in range(N_CHIP):
            dh = dh + _dot_tb(dp_ref[:, qb * nq:(qb + 1) * nq], w_ref[qb])
        xn, r = _rms(x_ref[...])
        g_pre = vec_ref[V_G_PRE_MIX:V_G_PRE_MIX + 1, :]
        sc1 = 1.0 + mod_ref[M_SC_M:M_SC_M + 1, :]
        dx_ref[...] = dx1_ref[...] + _rms_bwd(dh * (g_pre * sc1), xn, r)
        small_ref[SB1_DSH:SB1_DSH + 1, :] += _colsum(dh)
        small_ref[SB1_DSC:SB1_DSC + 1, :] += _colsum(dh * xn)

        @pl.when(i == n_t - 1)
        def _():
            dgm_t = small_ref[SB1_DSC:SB1_DSC + 1, :]
            small_ref[SB1_DSC:SB1_DSC + 1, :] = dgm_t * g_pre
            small_ref[SB1_DG_PRE:SB1_DG_PRE + 1, :] = dgm_t * sc1

    tile = pl.BlockSpec((ts, d), lambda i: (i, 0))
    return pl.pallas_call(
        body, name=name, grid=(n_t,),
        out_shape=[jax.ShapeDtypeStruct((s, d), F32), jax.ShapeDtypeStruct((SUBLANES, d), F32)],
        in_specs=[pl.BlockSpec((ts, N_CHIP * nq), lambda i: (i, 0)), tile, tile, _full(mod.shape), _full(vec.shape),
                  _full(w_in.shape)],
        out_specs=[tile, _full((SUBLANES, d))],
        compiler_params=_cparams(1),
    )(dproj, dx1, x, mod, vec, w_in)


def _weight_grad(a, b, name, col_blocks=1, tk=512):
    s, k = a.shape
    n = b.shape[1]
    tn = n // col_blocks
    tk = min(tk, k)

    def body(a_ref, b_ref, o_ref):
        o_ref[0] = _dot_ta(a_ref[...], b_ref[...])

    return pl.pallas_call(
        body, name=name, grid=(col_blocks, k // tk),
        out_shape=jax.ShapeDtypeStruct((col_blocks, k, tn), F32),
        in_specs=[pl.BlockSpec((s, tk), lambda j, i: (0, i)), pl.BlockSpec((s, tn), lambda j, i: (0, j))],
        out_specs=pl.BlockSpec((1, tk, tn), lambda j, i: (j, i, 0)),
        compiler_params=_cparams(2),
    )(a, b)


def _adamw(w, g, m, v, name):
    shape = w.shape
    cols = shape[-1]
    rows = w.size // cols
    tr = _row_tile(rows, cols, target_bytes=1024 * 1024)
    c1 = 1.0 - ADAM_B1 ** ADAM_STEP
    c2 = 1.0 - ADAM_B2 ** ADAM_STEP

    def body(w_ref, g_ref, m_ref, v_ref, d_ref, nm_ref, nv_ref):
        gv = g_ref[...]
        nm = ADAM_B1 * m_ref[...] + (1.0 - ADAM_B1) * gv
        nv = ADAM_B2 * v_ref[...] + (1.0 - ADAM_B2) * (gv * gv)
        nm_ref[...] = nm
        nv_ref[...] = nv
        d_ref[...] = (-ADAM_LR) * ((nm / c1) / (jnp.sqrt(nv / c2) + ADAM_EPS) + ADAM_WD * w_ref[...])

    spec = pl.BlockSpec((tr, cols), lambda i: (i, 0))
    outs = pl.pallas_call(
        body, name=name, grid=(rows // tr,),
        out_shape=[jax.ShapeDtypeStruct((rows, cols), F32)] * 3,
        in_specs=[spec] * 4, out_specs=[spec] * 3,
        compiler_params=_cparams(1),
    )(*(t.reshape(rows, cols) for t in (w, g, m, v)))
    return tuple(o.reshape(shape) for o in outs)


def kernel(x, c, w_mod, b_mod, g_pre_mix, g_post_mix, w_in, conv_a_w, conv_a_b, w_a_out, conv_b_w, conv_b_b, w_gate_r, b_gate_r, w_gate_i, b_gate_i, lru_lambda, w_b_out, w_o, g_pre_mlp, g_post_mlp, w_mlp_up, w_mlp_down, loss_target, m_w_mod, m_b_mod, m_g_pre_mix, m_g_post_mix, m_w_in, m_conv_a_w, m_conv_a_b, m_w_a_out, m_conv_b_w, m_conv_b_b, m_w_gate_r, m_b_gate_r, m_w_gate_i, m_b_gate_i, m_lru_lambda, m_w_b_out, m_w_o, m_g_pre_mlp, m_g_post_mlp, m_w_mlp_up, m_w_mlp_down, v_w_mod, v_b_mod, v_g_pre_mix, v_g_post_mix, v_w_in, v_conv_a_w, v_conv_a_b, v_w_a_out, v_conv_b_w, v_conv_b_b, v_w_gate_r, v_b_gate_r, v_w_gate_i, v_b_gate_i, v_lru_lambda, v_w_b_out, v_w_o, v_g_pre_mlp, v_g_post_mlp, v_w_mlp_up, v_w_mlp_down):
    weights = dict(w_mod=w_mod, b_mod=b_mod, g_pre_mix=g_pre_mix, g_post_mix=g_post_mix, w_in=w_in, conv_a_w=conv_a_w,
                   conv_a_b=conv_a_b, w_a_out=w_a_out, conv_b_w=conv_b_w, conv_b_b=conv_b_b, w_gate_r=w_gate_r,
                   b_gate_r=b_gate_r, w_gate_i=w_gate_i, b_gate_i=b_gate_i, lru_lambda=lru_lambda, w_b_out=w_b_out,
                   w_o=w_o, g_pre_mlp=g_pre_mlp, g_post_mlp=g_post_mlp, w_mlp_up=w_mlp_up, w_mlp_down=w_mlp_down)
    mom1 = dict(w_mod=m_w_mod, b_mod=m_b_mod, g_pre_mix=m_g_pre_mix, g_post_mix=m_g_post_mix, w_in=m_w_in,
                conv_a_w=m_conv_a_w, conv_a_b=m_conv_a_b, w_a_out=m_w_a_out, conv_b_w=m_conv_b_w, conv_b_b=m_conv_b_b,
                w_gate_r=m_w_gate_r, b_gate_r=m_b_gate_r, w_gate_i=m_w_gate_i, b_gate_i=m_b_gate_i,
                lru_lambda=m_lru_lambda, w_b_out=m_w_b_out, w_o=m_w_o, g_pre_mlp=m_g_pre_mlp, g_post_mlp=m_g_post_mlp,
                w_mlp_up=m_w_mlp_up, w_mlp_down=m_w_mlp_down)
    mom2 = dict(w_mod=v_w_mod, b_mod=v_b_mod, g_pre_mix=v_g_pre_mix, g_post_mix=v_g_post_mix, w_in=v_w_in,
                conv_a_w=v_conv_a_w, conv_a_b=v_conv_a_b, w_a_out=v_w_a_out, conv_b_w=v_conv_b_w, conv_b_b=v_conv_b_b,
                w_gate_r=v_w_gate_r, b_gate_r=v_b_gate_r, w_gate_i=v_w_gate_i, b_gate_i=v_b_gate_i,
                lru_lambda=v_lru_lambda, w_b_out=v_w_b_out, w_o=v_w_o, g_pre_mlp=v_g_pre_mlp, g_post_mlp=v_g_post_mlp,
                w_mlp_up=v_w_mlp_up, w_mlp_down=v_w_mlp_down)
    names = list(weights)

    n_layer = w_in.shape[0]
    s, d = x.shape[1], x.shape[2]
    n_head, bw = w_gate_r.shape[1], w_gate_r.shape[2]
    dq = d // N_CHIP
    mq = w_mod.shape[2]
    n_mod = (N_CHIP * mq) // d
    ka, kb = conv_a_w.shape[1], conv_b_w.shape[1]

    mx, my, mc = _place()
    q_me = 2 * mx + my
    q_arr = jnp.reshape(q_me, (1,)).astype(jnp.int32)

    n_conv_rows = n_layer * (ka + kb)
    conv_blk = -(-n_conv_rows // SUBLANES) * SUBLANES
    blk_rows = SUBLANES + conv_blk
    conv_rows = jnp.concatenate([jnp.concatenate([conv_a_w[l], conv_b_w[l]], axis=0) for l in range(n_layer)], axis=0)
    conv_rows = jnp.pad(conv_rows, ((0, conv_blk - n_conv_rows), (0, d - dq)))
    c_blk = jnp.pad(c, ((0, SUBLANES - 1), (0, 0)))
    gathered1 = _all_gather_small(jnp.concatenate([c_blk, conv_rows], axis=0), "gather_c_conv").reshape(N_DEV, blk_rows, d)
    c_all = gathered1[:, 0, :]
    conv_full = jnp.concatenate([gathered1[2 * qb, SUBLANES:SUBLANES + n_conv_rows, :dq] for qb in range(N_CHIP)], axis=1)

    b_mod_shard = lax.dynamic_slice_in_dim(b_mod, q_me * mq, mq, axis=1)
    mod_part = _mod_forward(c_all, w_mod, b_mod_shard, "mod_forward")
    gathered2 = _all_gather_small(mod_part, "gather_mod").reshape(N_DEV, n_layer, N_DEV, mq)
    me = 4 * mx + 2 * my + mc
    mod_rows = jnp.concatenate(
        [lax.dynamic_index_in_dim(gathered2[2 * qb], me, axis=1, keepdims=False) for qb in range(N_CHIP)], axis=1)
    mods = [jnp.pad(mod_rows[l].reshape(n_mod, d), ((0, SUBLANES - n_mod), (0, 0))) for l in range(n_layer)]

    vecs = []
    for l in range(n_layer):
        base = l * (ka + kb)
        rows = [g_pre_mix[l], g_post_mix[l], conv_a_b[l], conv_b_b[l], b_gate_r[l], b_gate_i[l], lru_lambda[l],
                g_pre_mlp[l], g_post_mlp[l]]
        vecs.append(jnp.concatenate([jnp.stack(rows, axis=0), conv_full[base:base + ka + kb]], axis=0))

    big_names = ["w_in", "w_a_out", "w_b_out", "w_o", "w_mlp_up", "w_mlp_down"]
    gathered_w = _gather_weights([_cast_place(weights[nm], l, q_arr, f"cast_place_{nm}_{l}")
                                  for l in range(n_layer) for nm in big_names], "gather_weights")
    wfull = [dict(zip(big_names, gathered_w[l * len(big_names):(l + 1) * len(big_names)])) for l in range(n_layer)]
    for wl in wfull:
        for nm in ("w_a_out", "w_b_out", "w_o"):
            wl[nm] = wl[nm].reshape(d, d)
    wgs = [jnp.concatenate([w_gate_r[l], w_gate_i[l]], axis=-1).astype(BF16) for l in range(n_layer)]

    xs = x[0]
    saved = []
    for l in range(n_layer):
        wl = wfull[l]
        h, proj = _norm_proj(xs, mods[l], vecs[l], wl["w_in"], f"norm_proj_{l}")
        x1, conva, xb2, hh, ya, yb, pa, pb, mm, yy = _mixer_forward(
            xs, proj, mods[l], vecs[l], wgs[l], wl["w_a_out"], wl["w_b_out"], wl["w_o"], f"mixer_forward_{l}")
        x2, h2, up, y2 = _mlp_forward(x1, mods[l], vecs[l], wl["w_mlp_up"], wl["w_mlp_down"], f"mlp_forward_{l}")
        saved.append(dict(x=xs, h=h, proj=proj, x1=x1, conva=conva, xb2=xb2, hh=hh, ya=ya, yb=yb, pa=pa, pb=pb, m=mm,
                          y=yy, h2=h2, up=up, y2=y2))
        xs = x2
    dxs, loss_tile = _loss_head(xs, loss_target[0], "loss_head")
    loss = lax.psum(loss_tile[0, 0], ("x", "y", "c"))

    wgrads = [None] * n_layer
    smalls = [None] * n_layer
    for l in reversed(range(n_layer)):
        wl, sv = wfull[l], saved[l]
        dx1, dy2, dup, act, small3 = _mlp_backward(dxs, sv["x1"], sv["y2"], sv["up"], mods[l], vecs[l],
                                                   wl["w_mlp_up"], wl["w_mlp_down"], f"mlp_backward_{l}")
        dproj, dy, dpa, dpb, small2, dwg = _mixer_backward(
            dx1, sv["proj"], sv["conva"], sv["xb2"], sv["hh"], sv["pa"], sv["pb"], sv["y"], mods[l], vecs[l], wgs[l],
            wl["w_a_out"], wl["w_b_out"], wl["w_o"], f"mixer_backward_{l}")
        dxs, small1 = _proj_backward(dproj, dx1, sv["x"], mods[l], vecs[l], wl["w_in"], f"proj_backward_{l}")
        rowblk = lambda t: t.reshape(N_CHIP, t.shape[1] // N_CHIP, t.shape[2])
        wgrads[l] = dict(
            w_in=_weight_grad(sv["h"], dproj, f"grad_w_in_{l}", col_blocks=N_CHIP),
            w_a_out=rowblk(_weight_grad(sv["ya"], dpa, f"grad_w_a_out_{l}")),
            w_b_out=rowblk(_weight_grad(sv["yb"], dpb, f"grad_w_b_out_{l}")),
            w_o=rowblk(_weight_grad(sv["m"], dy, f"grad_w_o_{l}")),
            w_mlp_up=_weight_grad(sv["h2"], dup, f"grad_w_mlp_up_{l}", col_blocks=N_CHIP),
            w_mlp_down=rowblk(_weight_grad(act, dy2, f"grad_w_mlp_down_{l}")))
        smalls[l] = (small1, small2, small3, dwg)
    grad_x = dxs[None]

    o1, o2, o3, o4 = 0, SUBLANES, 3 * SUBLANES, 4 * SUBLANES
    n_small = o4 + 2 * bw
    packed = jnp.concatenate([part for l in range(n_layer)
                              for part in (smalls[l][0], smalls[l][1], smalls[l][2], smalls[l][3].reshape(2 * bw, d))], axis=0)
    small_all = _all_gather_small(packed, "gather_small_grads")
    small_sum = _sum_devices(small_all, "sum_small_grads").reshape(n_layer, n_small, d)
    mod_rows_of = [o1 + SB1_DSH, o1 + SB1_DSC, o2 + SB2_DGT, o3 + SB3_DSH, o3 + SB3_DSC, o3 + SB3_DGT]
    per_dev = small_all.reshape(N_DEV, n_layer, n_small, d)
    dmod_all = jnp.concatenate([per_dev[:, :, k, :] for k in mod_rows_of], axis=2)
    dmod_all = jnp.swapaxes(dmod_all, 0, 1)
    grads = {}
    grads["w_mod"] = _mod_backward(c_all.T, lax.dynamic_slice_in_dim(dmod_all, q_me * mq, mq, axis=2), "mod_backward")
    grads["b_mod"] = jnp.concatenate([small_sum[:, k, :] for k in mod_rows_of], axis=1)
    grads["g_pre_mix"] = small_sum[:, o1 + SB1_DG_PRE]
    grads["g_post_mix"] = small_sum[:, o2 + SB2_DG_POST]
    grads["conv_a_w"] = lax.dynamic_slice_in_dim(small_sum[:, o2 + SB2_DWA:o2 + SB2_DWA + ka], q_me * dq, dq, axis=2)
    grads["conv_a_b"] = small_sum[:, o2 + SB2_DBA]
    grads["conv_b_w"] = lax.dynamic_slice_in_dim(small_sum[:, o2 + SB2_DWB:o2 + SB2_DWB + kb], q_me * dq, dq, axis=2)
    grads["conv_b_b"] = small_sum[:, o2 + SB2_DBB]
    grads["lru_lambda"] = small_sum[:, o2 + SB2_DLAM]
    grads["b_gate_r"] = small_sum[:, o2 + SB2_DBR]
    grads["b_gate_i"] = small_sum[:, o2 + SB2_DBI]
    grads["g_pre_mlp"] = small_sum[:, o3 + SB3_DG_PRE]
    grads["g_post_mlp"] = small_sum[:, o3 + SB3_DG_POST]
    dwg_sum = small_sum[:, o4:].reshape(n_layer, n_head, bw, 2 * bw)
    grads["w_gate_r"] = dwg_sum[..., :bw]
    grads["w_gate_i"] = dwg_sum[..., bw:]

    chips_q = [q_me ^ 2, q_me ^ 1, q_me ^ 3]
    pf = jnp.stack([mc, q_me] + chips_q).astype(jnp.int32)
    flat = [wgrads[l][nm] for nm in big_names for l in range(n_layer)]
    landed = _exchange_halves(flat, "rs_exchange_halves")
    to_send = [_add_sibling_half(g, ld, pf, f"rs_add_sibling_{k}") for k, (g, ld) in enumerate(zip(flat, landed))]
    from_chips = _scatter_to_chips(to_send, "rs_scatter_to_chips")
    outs = []
    for k in range(len(big_names)):
        prev = None
        for l in range(n_layer):
            i = k * n_layer + l
            prev = _add_chips(flat[i], landed[i], from_chips[i], pf, prev, l, n_layer, f"rs_add_chips_{i}")
        outs.append(prev)
    for nm, gj in zip(big_names, _join_halves(outs, "rs_join_halves")):
        grads[nm] = gj.reshape(weights[nm].shape)

    deltas, new_m, new_v = {}, {}, {}
    for nm in names:
        deltas[nm], new_m[nm], new_v[nm] = _adamw(weights[nm], grads[nm], mom1[nm], mom2[nm], f"adamw_{nm}")
    return (loss, grad_x, *[grads[nm] for nm in names], *[deltas[nm] for nm in names],
            *[new_m[nm] for nm in names], *[new_v[nm] for nm in names])
```

```python
import jax
import jax.numpy as jnp
from jax import lax
from jax.experimental import pallas as pl
from jax.experimental.pallas import tpu as pltpu

F32 = jnp.float32
BF16 = jnp.bfloat16
MESH = pl.DeviceIdType.MESH

EPS = 1e-6
LRU_C = 8.0
N_CHIP = 4
N_DEV = 8
ADAM_LR = 0.001
ADAM_B1 = 0.9
ADAM_B2 = 0.999
ADAM_EPS = 1e-08
ADAM_WD = 0.01
ADAM_STEP = 10

VMEM_LIMIT_BYTES = 56 * 1024 * 1024
SUBLANES = 8
TOKENS_MATMUL_TILE = 512
TOKENS_MIXER_TILE = 256
GELU_K0 = 0.7978845608028654
GELU_K1 = 0.044715

V_G_PRE_MIX, V_G_POST_MIX, V_CONV_A_B, V_CONV_B_B, V_B_GATE_R, V_B_GATE_I, V_LAMBDA, V_G_PRE_MLP, V_G_POST_MLP = range(9)
V_CONV_A_W = 9
V_CONV_B_W = 12
M_SH_M, M_SC_M, M_GT_M, M_SH_F, M_SC_F, M_GT_F = range(6)


def _cparams(n_grid=0):
    sem = ("arbitrary",) * n_grid if n_grid else None
    return pltpu.CompilerParams(dimension_semantics=sem, vmem_limit_bytes=VMEM_LIMIT_BYTES)


def _full(shape):
    return pl.BlockSpec(shape, lambda *_: (0,) * len(shape))


def _dot(a, b):
    return jnp.dot(a, b, preferred_element_type=F32)


def _dot_tb(a, b):
    return lax.dot_general(a, b, (((1,), (1,)), ((), ())), preferred_element_type=F32)


def _dot_ta(a, b):
    return lax.dot_general(a, b, (((0,), (0,)), ((), ())), preferred_element_type=F32)


def _sigmoid(x):
    return 1.0 / (1.0 + jnp.exp(-x))


def _softplus(x):
    return jnp.maximum(x, 0.0) + jnp.log1p(jnp.exp(-jnp.abs(x)))


def _neg_expm1(x):
    series = -x * (1.0 + 0.5 * x * (1.0 + (x / 3.0) * (1.0 + 0.25 * x)))
    return jnp.where(x > -1e-2, series, 1.0 - jnp.exp(x))


def _gelu(x):
    t = jnp.tanh(GELU_K0 * (x + GELU_K1 * x * x * x))
    return 0.5 * x * (1.0 + t), t


def _gelu_grad(x, t):
    return 0.5 * (1.0 + t) + 0.5 * x * (1.0 - t * t) * GELU_K0 * (1.0 + 3.0 * GELU_K1 * x * x)


def _rms(x):
    r = lax.rsqrt(jnp.mean(x * x, axis=-1, keepdims=True) + EPS)
    return x * r, r


def _rms_bwd(dxn, xn, r):
    return r * (dxn - xn * jnp.mean(dxn * xn, axis=-1, keepdims=True))


def _colsum(x):
    return jnp.sum(x, axis=0, keepdims=True)


def _rows(t, w):
    return lax.broadcasted_iota(jnp.int32, (t, w), 0)


def _shift_down(x, k, prev8):
    t, w = x.shape
    rolled = pltpu.roll(x, k, 0)
    head = jnp.where(_rows(SUBLANES, w) < k, pltpu.roll(prev8, k, 0), rolled[:SUBLANES])
    return jnp.concatenate([head, rolled[SUBLANES:]], axis=0)


def _shift_up(x, k, next8):
    t, w = x.shape
    rolled = pltpu.roll(x, t - k, 0)
    tail = jnp.where(_rows(SUBLANES, w) >= SUBLANES - k, pltpu.roll(next8, SUBLANES - k, 0), rolled[t - SUBLANES:])
    return jnp.concatenate([rolled[:t - SUBLANES], tail], axis=0)


def _scan_fwd(a, b):
    t, w = a.shape
    rows = _rows(t, w)
    s = 1
    while s < t:
        keep = rows >= s
        b = b + a * jnp.where(keep, pltpu.roll(b, s, 0), 0.0)
        a = a * jnp.where(keep, pltpu.roll(a, s, 0), 1.0)
        s *= 2
    return b, a


def _scan_bwd(a, b):
    t, w = a.shape
    rows = _rows(t, w)
    s = 1
    while s < t:
        keep = rows < t - s
        b = b + a * jnp.where(keep, pltpu.roll(b, t - s, 0), 0.0)
        a = a * jnp.where(keep, pltpu.roll(a, t - s, 0), 1.0)
        s *= 2
    return b, a


def _row_tile(rows, cols, itemsize=4, target_bytes=2 * 1024 * 1024):
    if rows * cols * itemsize <= target_bytes or rows % SUBLANES:
        return rows
    t = max(SUBLANES, (target_bytes // (cols * itemsize)) // SUBLANES * SUBLANES)
    while rows % t:
        t -= SUBLANES
    return t


def _place():
    return lax.axis_index("x"), lax.axis_index("y"), lax.axis_index("c")


def _other_chips(x, y):
    chips = [(1 - x, y), (x, 1 - y), (1 - x, 1 - y)]
    return chips, [2 * cx + cy for cx, cy in chips]


def _all_gather_small(block, name):
    m_per, n = block.shape

    def body(x_ref, out_ref, send_sems, recv_sems, local_sem):
        x, y, c = _place()
        me, sibling = (x, y, c), (x, y, 1 - c)
        chips, _ = _other_chips(x, y)

        def rows(px, py, pc):
            return out_ref.at[pl.ds((4 * px + 2 * py + pc) * m_per, m_per), :]

        def copy(k, blk, to, src=None):
            return pltpu.make_async_remote_copy(
                src_ref=rows(*blk) if src is None else src, dst_ref=rows(*blk),
                send_sem=send_sems.at[k], recv_sem=recv_sems.at[k], device_id=to, device_id_type=MESH)

        mine = pltpu.make_async_copy(x_ref, rows(*me), local_sem)
        mine.start()
        first = [copy(0, me, sibling, src=x_ref)]
        first += [copy(1 + j, me, (*chip, c), src=x_ref) for j, chip in enumerate(chips)]
        for cp in first:
            cp.start()
        passed = [copy(4 + j, (*chip, c), sibling) for j, chip in enumerate(chips)]
        for j, chip in enumerate(chips):
            copy(1 + j, (*chip, c), me).wait_recv()
            passed[j].start()
        copy(0, sibling, me).wait_recv()
        for j, chip in enumerate(chips):
            copy(4 + j, (*chip, 1 - c), me).wait_recv()
        for cp in first + passed:
            cp.wait_send()
        mine.wait()

    return pl.pallas_call(
        body, name=name,
        out_shape=jax.ShapeDtypeStruct((N_DEV * m_per, n), block.dtype),
        in_specs=[pl.BlockSpec(memory_space=pltpu.VMEM)],
        out_specs=pl.BlockSpec(memory_space=pltpu.VMEM),
        scratch_shapes=[pltpu.SemaphoreType.DMA((7,)), pltpu.SemaphoreType.DMA((7,)), pltpu.SemaphoreType.DMA],
        compiler_params=pltpu.CompilerParams(vmem_limit_bytes=VMEM_LIMIT_BYTES),
    )(block)


def _cast_place(w, layer, q_arr, name):
    _, r, cols = w.shape
    tr = _row_tile(r, cols)

    def body(q_ref, w_ref, o_ref):
        o_ref[...] = w_ref[...].astype(BF16)

    return pl.pallas_call(
        body, name=name,
        out_shape=jax.ShapeDtypeStruct((N_CHIP, r, cols), BF16),
        grid_spec=pltpu.PrefetchScalarGridSpec(
            num_scalar_prefetch=1, grid=(r // tr,),
            in_specs=[pl.BlockSpec((1, tr, cols), lambda i, q_ref: (layer, i, 0))],
            out_specs=pl.BlockSpec((1, tr, cols), lambda i, q_ref: (q_ref[0], i, 0))),
        compiler_params=_cparams(1),
    )(q_arr, w)


class _Carry:
    def __init__(self, ins, out_shapes, aliases, sem_shapes, start, finish, mid=None):
        self.ins, self.out_shapes, self.aliases, self.sem_shapes = list(ins), list(out_shapes), dict(aliases), list(sem_shapes)
        self.start, self.mid, self.finish = start, mid, finish


def _pcall(body, *, name, grid, in_specs, out_specs, out_shape, args, scratch_shapes=(), carries=(), mid_frac=0.85):
    in_specs, out_specs, out_shape = list(in_specs), list(out_specs), list(out_shape)
    scratch_shapes, args = list(scratch_shapes), list(args)
    n_in, n_out, n_scr = len(in_specs), len(out_shape), len(scratch_shapes)
    steps = 1
    for g in grid:
        steps *= g
    mid_step = min(steps - 1, int(steps * mid_frac))
    any_spec = pl.BlockSpec(memory_space=pl.ANY)
    aliases = {}
    spans = []
    for cr in carries:
        spans.append((len(args), len(out_shape), len(scratch_shapes)))
        for a, b in cr.aliases.items():
            aliases[len(args) + a] = len(out_shape) + b
        args += cr.ins
        in_specs += [any_spec] * len(cr.ins)
        out_shape += cr.out_shapes
        out_specs += [any_spec] * len(cr.out_shapes)
        scratch_shapes += cr.sem_shapes
    n_all_in = len(args)
    n_all_out = len(out_shape)

    def wrapped(*refs):
        ins, outs, scr = refs[:n_all_in], refs[n_all_in:n_all_in + n_all_out], refs[n_all_in + n_all_out:]
        parts = [(cr, ins[a:a + len(cr.ins)], outs[b:b + len(cr.out_shapes)], scr[s:s + len(cr.sem_shapes)])
                 for cr, (a, b, s) in zip(carries, spans)]
        lin = 0
        for ax, g in enumerate(grid):
            lin = lin * g + pl.program_id(ax)

        def at(step, fn):
            if steps == 1:
                fn()
            else:
                pl.when(lin == step)(fn)

        def start_all():
            for cr, ci, co, cs in parts:
                cr.start(ci, co, cs)

        def mid_all():
            for cr, ci, co, cs in parts:
                if cr.mid is not None:
                    cr.mid(ci, co, cs)

        def finish_all():
            for cr, ci, co, cs in parts:
                cr.finish(ci, co, cs)

        if parts:
            at(0, start_all)
        body(*ins[:n_in], *outs[:n_out], *scr[:n_scr])
        if parts:
            at(mid_step, mid_all)
            at(steps - 1, finish_all)

    res = pl.pallas_call(
        wrapped, name=name, grid=tuple(grid), out_shape=out_shape, in_specs=in_specs, out_specs=out_specs,
        scratch_shapes=scratch_shapes, input_output_aliases=aliases, compiler_params=_cparams(len(grid)),
    )(*args)
    res = list(res)
    return res[:n_out], [res[b:b + len(cr.out_shapes)] for cr, (_, b, _) in zip(carries, spans)]


def _run_carries(carries, name):
    return _pcall(lambda: None, name=name, grid=(), in_specs=[], out_specs=[], out_shape=[], args=[], carries=carries)[1]


def _gather_carry(bufs):
    n = len(bufs)

    def copies(o_refs, sems):
        send_sems, recv_sems = sems
        x, y, c = _place()
        q = 2 * x + y
        sibling = (x, y, 1 - c)
        chips, qs = _other_chips(x, y)

        def half(w, shard, pc):
            rh = bufs[w].shape[1] // 2
            return o_refs[w].at[shard, pl.ds(pc * rh, rh), :]

        def over_ici(w, j, shard):
            blk = half(w, shard, c)
            return pltpu.make_async_remote_copy(
                src_ref=blk, dst_ref=blk, send_sem=send_sems.at[w, j], recv_sem=recv_sems.at[w, j],
                device_id=(*chips[j], c), device_id_type=MESH)

        def to_sibling(w, j, pc):
            blk = half(w, qs[j], pc)
            return pltpu.make_async_remote_copy(
                src_ref=blk, dst_ref=blk, send_sem=send_sems.at[w, 3 + j], recv_sem=recv_sems.at[w, 3 + j],
                device_id=sibling, device_id_type=MESH)

        return q, c, qs, over_ici, to_sibling

    pairs = [(w, j) for w in range(n) for j in range(3)]

    def start(i_refs, o_refs, sems):
        q, _, _, over_ici, _ = copies(o_refs, sems)
        for w, j in pairs:
            over_ici(w, j, q).start()

    def mid(i_refs, o_refs, sems):
        _, c, qs, over_ici, to_sibling = copies(o_refs, sems)
        for w, j in pairs:
            over_ici(w, j, qs[j]).wait_recv()
            to_sibling(w, j, c).start()

    def finish(i_refs, o_refs, sems):
        q, c, _, over_ici, to_sibling = copies(o_refs, sems)
        for w, j in pairs:
            to_sibling(w, j, 1 - c).wait_recv()
        for w, j in pairs:
            over_ici(w, j, q).wait_send()
            to_sibling(w, j, c).wait_send()

    return _Carry(bufs, [jax.ShapeDtypeStruct(b.shape, b.dtype) for b in bufs], {w: w for w in range(n)},
                  [pltpu.SemaphoreType.DMA((n, 6)), pltpu.SemaphoreType.DMA((n, 6))], start, finish, mid)


def _exchange_carry(grads):
    n = len(grads)

    def copies(g_refs, l_refs, sems):
        send_sems, recv_sems = sems
        x, y, c = _place()
        out = []
        for w in range(n):
            rh = grads[w].shape[1] // 2
            out.append(pltpu.make_async_remote_copy(
                src_ref=g_refs[w].at[:, pl.ds((1 - c) * rh, rh), :], dst_ref=l_refs[w],
                send_sem=send_sems.at[w], recv_sem=recv_sems.at[w], device_id=(x, y, 1 - c), device_id_type=MESH))
        return out

    def start(g_refs, l_refs, sems):
        for cp in copies(g_refs, l_refs, sems):
            cp.start()

    def finish(g_refs, l_refs, sems):
        for cp in copies(g_refs, l_refs, sems):
            cp.wait()

    return _Carry(grads, [jax.ShapeDtypeStruct((N_CHIP, g.shape[1] // 2, g.shape[2]), g.dtype) for g in grads], {},
                  [pltpu.SemaphoreType.DMA((n,)), pltpu.SemaphoreType.DMA((n,))], start, finish)


def _scatter_carry(sums):
    n = len(sums)

    def copies(s_refs, l_refs, sems):
        send_sems, recv_sems = sems
        x, y, c = _place()
        chips, _ = _other_chips(x, y)
        return [pltpu.make_async_remote_copy(
            src_ref=s_refs[w].at[j], dst_ref=l_refs[w].at[j], send_sem=send_sems.at[w, j], recv_sem=recv_sems.at[w, j],
            device_id=(*chips[j], c), device_id_type=MESH) for w in range(n) for j in range(3)]

    def start(s_refs, l_refs, sems):
        for cp in copies(s_refs, l_refs, sems):
            cp.start()

    def finish(s_refs, l_refs, sems):
        for cp in copies(s_refs, l_refs, sems):
            cp.wait()

    return _Carry(sums, [jax.ShapeDtypeStruct(s.shape, s.dtype) for s in sums], {},
                  [pltpu.SemaphoreType.DMA((n, 3)), pltpu.SemaphoreType.DMA((n, 3))], start, finish)


def _join_carry(outs, layers):
    n = len(outs)

    def copy(o_refs, sems, w, mine):
        send_sems, recv_sems = sems
        x, y, c = _place()
        r = outs[w].shape[1]
        rows = o_refs[w].at[layers[w], pl.ds((c if mine else 1 - c) * (r // 2), r // 2), :]
        return pltpu.make_async_remote_copy(
            src_ref=rows, dst_ref=rows, send_sem=send_sems.at[w], recv_sem=recv_sems.at[w],
            device_id=(x, y, 1 - c), device_id_type=MESH)

    def start(i_refs, o_refs, sems):
        for w in range(n):
            copy(o_refs, sems, w, True).start()

    def finish(i_refs, o_refs, sems):
        for w in range(n):
            copy(o_refs, sems, w, True).wait_send()
        for w in range(n):
            copy(o_refs, sems, w, False).wait_recv()

    return _Carry(outs, [jax.ShapeDtypeStruct(o.shape, o.dtype) for o in outs], {w: w for w in range(n)},
                  [pltpu.SemaphoreType.DMA((n,)), pltpu.SemaphoreType.DMA((n,))], start, finish)


PF_C, PF_Q, PF_QS = 0, 1, 2


def _add_sibling_half(g, landed, pf, name):
    _, r, cols = g.shape
    rh = r // 2
    tr = _row_tile(rh, cols)
    nr = rh // tr

    def body(pf_ref, g_ref, l_ref, o_ref):
        o_ref[...] = (g_ref[...] + l_ref[...]).astype(BF16)

    return pl.pallas_call(
        body, name=name,
        out_shape=jax.ShapeDtypeStruct((3, rh, cols), BF16),
        grid_spec=pltpu.PrefetchScalarGridSpec(
            num_scalar_prefetch=1, grid=(3, nr),
            in_specs=[pl.BlockSpec((1, tr, cols), lambda j, i, pf_ref: (pf_ref[PF_QS + j], pf_ref[PF_C] * nr + i, 0)),
                      pl.BlockSpec((1, tr, cols), lambda j, i, pf_ref: (pf_ref[PF_QS + j], i, 0))],
            out_specs=pl.BlockSpec((1, tr, cols), lambda j, i, pf_ref: (j, i, 0))),
        compiler_params=_cparams(2),
    )(pf, g, landed)


def _add_chips(g, landed, from_chips, pf, prev, layer, n_layer, name):
    _, r, cols = g.shape
    rh = r // 2
    tr = _row_tile(rh, cols)
    nr = rh // tr

    def body(pf_ref, g_ref, l_ref, f_ref, *rest):
        o_ref = rest[-1]
        acc = g_ref[0] + l_ref[0]
        for j in range(3):
            acc = acc + f_ref[j].astype(F32)
        o_ref[0] = acc

    in_specs = [pl.BlockSpec((1, tr, cols), lambda i, pf_ref: (pf_ref[PF_Q], pf_ref[PF_C] * nr + i, 0)),
                pl.BlockSpec((1, tr, cols), lambda i, pf_ref: (pf_ref[PF_Q], i, 0)),
                pl.BlockSpec((3, tr, cols), lambda i, pf_ref: (0, i, 0))]
    args = [pf, g, landed, from_chips]
    aliases = {}
    if prev is not None:
        in_specs.append(pl.BlockSpec(memory_space=pl.ANY))
        args.append(prev)
        aliases = {4: 0}
    return pl.pallas_call(
        body, name=name,
        out_shape=jax.ShapeDtypeStruct((n_layer, r, cols), F32),
        grid_spec=pltpu.PrefetchScalarGridSpec(
            num_scalar_prefetch=1, grid=(nr,), in_specs=in_specs,
            out_specs=pl.BlockSpec((1, tr, cols), lambda i, pf_ref: (layer, pf_ref[PF_C] * nr + i, 0))),
        input_output_aliases=aliases,
        compiler_params=_cparams(1),
    )(*args)


def _sum_devices(gathered, name):
    rows, n = gathered.shape
    m = rows // N_DEV
    tr = _row_tile(m, n, target_bytes=512 * 1024)
    g3 = gathered.reshape(N_DEV, m, n)

    def body(g_ref, o_ref):
        acc = g_ref[0]
        for d in range(1, N_DEV):
            acc = acc + g_ref[d]
        o_ref[...] = acc

    return pl.pallas_call(
        body, name=name, grid=(m // tr,),
        out_shape=jax.ShapeDtypeStruct((m, n), F32),
        in_specs=[pl.BlockSpec((N_DEV, tr, n), lambda i: (0, i, 0))],
        out_specs=pl.BlockSpec((tr, n), lambda i: (i, 0)),
        compiler_params=_cparams(1),
    )(g3)


def _mod_forward(c_all, w_mod, b_mod_shard, name):
    n_layer, d, mq = w_mod.shape

    def body(c_ref, w_ref, b_ref, o_ref):
        cv = c_ref[...]
        o_ref[...] = _dot(cv * _sigmoid(cv), w_ref[0]) + b_ref[0]

    return pl.pallas_call(
        body, name=name, grid=(n_layer,),
        out_shape=jax.ShapeDtypeStruct((n_layer * N_DEV, mq), F32),
        in_specs=[_full((N_DEV, d)), pl.BlockSpec((1, d, mq), lambda l: (l, 0, 0)),
                  pl.BlockSpec((1, 1, mq), lambda l: (l, 0, 0))],
        out_specs=pl.BlockSpec((N_DEV, mq), lambda l: (l, 0)),
        compiler_params=_cparams(1),
    )(c_all, w_mod, b_mod_shard.reshape(n_layer, 1, mq))


def _mod_backward(c_all_t, dmod_shard, name):
    n_layer, _, mq = dmod_shard.shape
    d = c_all_t.shape[0]

    def body(c_ref, dm_ref, o_ref):
        cv = c_ref[...]
        o_ref[0] = _dot(cv * _sigmoid(cv), dm_ref[0])

    return pl.pallas_call(
        body, name=name, grid=(n_layer,),
        out_shape=jax.ShapeDtypeStruct((n_layer, d, mq), F32),
        in_specs=[_full((d, N_DEV)), pl.BlockSpec((1, N_DEV, mq), lambda l: (l, 0, 0))],
        out_specs=pl.BlockSpec((1, d, mq), lambda l: (l, 0, 0)),
        compiler_params=_cparams(1),
    )(c_all_t, dmod_shard)


def _norm_proj(x, mod, vec, w_in, name, carries=()):
    s, d = x.shape
    nq = w_in.shape[2]
    ts = min(TOKENS_MATMUL_TILE, s)

    def body(x_ref, mod_ref, vec_ref, w_ref, h_ref, p_ref):
        xn, _ = _rms(x_ref[...])
        gm = vec_ref[V_G_PRE_MIX:V_G_PRE_MIX + 1, :] * (1.0 + mod_ref[M_SC_M:M_SC_M + 1, :])
        h = (xn * gm + mod_ref[M_SH_M:M_SH_M + 1, :]).astype(BF16)
        h_ref[...] = h
        for qb in range(N_CHIP):
            p_ref[:, qb * nq:(qb + 1) * nq] = _dot(h, w_ref[qb]).astype(BF16)

    return _pcall(
        body, name=name, grid=(s // ts,),
        out_shape=[jax.ShapeDtypeStruct((s, d), BF16), jax.ShapeDtypeStruct((s, N_CHIP * nq), BF16)],
        in_specs=[pl.BlockSpec((ts, d), lambda i: (i, 0)), _full(mod.shape), _full(vec.shape), _full(w_in.shape)],
        out_specs=[pl.BlockSpec((ts, d), lambda i: (i, 0)), pl.BlockSpec((ts, N_CHIP * nq), lambda i: (i, 0))],
        args=[x, mod, vec, w_in], carries=carries)


def _gate_pre(xb2_b, wg_ref, n_head, bw):
    zr, zi = [], []
    for hd in range(n_head):
        z = _dot(xb2_b[:, hd * bw:(hd + 1) * bw], wg_ref[hd])
        zr.append(z[:, :bw])
        zi.append(z[:, bw:])
    return jnp.concatenate(zr, axis=1), jnp.concatenate(zi, axis=1)


def _lru_coeffs(xb2, wg_ref, vec_ref, n_head, bw):
    zr, zi = _gate_pre(xb2.astype(BF16), wg_ref, n_head, bw)
    r = _sigmoid(zr + vec_ref[V_B_GATE_R:V_B_GATE_R + 1, :])
    gi = _sigmoid(zi + vec_ref[V_B_GATE_I:V_B_GATE_I + 1, :])
    sp = _softplus(-vec_ref[V_LAMBDA:V_LAMBDA + 1, :])
    log_a = (-LRU_C) * r * sp
    a = jnp.exp(log_a)
    mult = jnp.sqrt(_neg_expm1(2.0 * log_a))
    return r, gi, sp, a, mult


def _mixer_forward(x, proj, mod, vec, wg, w_a_out, w_b_out, w_o, name, carries=()):
    s, d = x.shape
    n_head, bw, _ = wg.shape
    ts = min(TOKENS_MIXER_TILE, s)

    def body(x_ref, p_ref, mod_ref, vec_ref, wg_ref, wa_ref, wb_ref, wo_ref,
             x1_ref, conva_ref, xb2_ref, hh_ref, ya_ref, yb_ref, pa_ref, pb_ref, m_ref, y_ref,
             cv_tail, xb_tail, h_last):
        i = pl.program_id(0)

        @pl.when(i == 0)
        def _():
            cv_tail[...] = jnp.zeros_like(cv_tail)
            xb_tail[...] = jnp.zeros_like(xb_tail)
            h_last[...] = jnp.zeros_like(h_last)

        def seg(k):
            return p_ref[:, k * d:(k + 1) * d].astype(F32)

        def vrow(k):
            return vec_ref[k:k + 1, :]

        b_a, c_a, v_a, x_b, g_b, u_a, u_b = (seg(k) for k in range(7))
        cv = c_a * v_a
        prev_cv = cv_tail[...]
        conv_a = (vrow(V_CONV_A_B) + vrow(V_CONV_A_W) * _shift_down(cv, 2, prev_cv)
                  + vrow(V_CONV_A_W + 1) * _shift_down(cv, 1, prev_cv) + vrow(V_CONV_A_W + 2) * cv)
        cv_tail[...] = cv[ts - SUBLANES:]
        y_a = b_a * conv_a
        prev_xb = xb_tail[...]
        xb2 = (vrow(V_CONV_B_B) + vrow(V_CONV_B_W) * _shift_down(x_b, 3, prev_xb)
               + vrow(V_CONV_B_W + 1) * _shift_down(x_b, 2, prev_xb)
               + vrow(V_CONV_B_W + 2) * _shift_down(x_b, 1, prev_xb) + vrow(V_CONV_B_W + 3) * x_b)
        xb_tail[...] = x_b[ts - SUBLANES:]
        _, gi, _, a, mult = _lru_coeffs(xb2, wg_ref, vec_ref, n_head, bw)
        h_loc, a_cum = _scan_fwd(a, mult * gi * xb2)
        hh = h_loc + a_cum * h_last[SUBLANES - 1:SUBLANES, :]
        h_last[...] = hh[ts - SUBLANES:]
        gel, _ = _gelu(g_b)
        y_b = hh * gel
        ya_b, yb_b = y_a.astype(BF16), y_b.astype(BF16)
        pa = _dot(ya_b, wa_ref[...])
        pb = _dot(yb_b, wb_ref[...])
        m = (_sigmoid(u_a) * pa + _sigmoid(u_b) * pb).astype(BF16)
        y = _dot(m, wo_ref[...])
        yn, _ = _rms(y)
        gg = mod_ref[M_GT_M:M_GT_M + 1, :] * vrow(V_G_POST_MIX)
        x1_ref[...] = x_ref[...] + yn * gg
        conva_ref[...] = conv_a.astype(BF16)
        xb2_ref[...] = xb2
        hh_ref[...] = hh
        ya_ref[...] = ya_b
        yb_ref[...] = yb_b
        pa_ref[...] = pa.astype(BF16)
        pb_ref[...] = pb.astype(BF16)
        m_ref[...] = m
        y_ref[...] = y.astype(BF16)

    tile = pl.BlockSpec((ts, d), lambda i: (i, 0))
    sd = lambda dt: jax.ShapeDtypeStruct((s, d), dt)
    return _pcall(
        body, name=name, grid=(s // ts,),
        out_shape=[sd(F32), sd(BF16), sd(F32), sd(F32), sd(BF16), sd(BF16), sd(BF16), sd(BF16), sd(BF16), sd(BF16)],
        in_specs=[tile, pl.BlockSpec((ts, 7 * d), lambda i: (i, 0)), _full(mod.shape), _full(vec.shape),
                  _full(wg.shape), _full(w_a_out.shape), _full(w_b_out.shape), _full(w_o.shape)],
        out_specs=[tile] * 10,
        scratch_shapes=[pltpu.VMEM((SUBLANES, d), F32), pltpu.VMEM((SUBLANES, d), F32), pltpu.VMEM((SUBLANES, d), F32)],
        args=[x, proj, mod, vec, wg, w_a_out, w_b_out, w_o], carries=carries)


def _mlp_forward(x1, mod, vec, w_up, w_down, name, carries=()):
    s, d = x1.shape
    fq = w_up.shape[2]
    ts = min(TOKENS_MATMUL_TILE, s)

    def body(x_ref, mod_ref, vec_ref, wu_ref, wd_ref, x2_ref, h2_ref, up_ref, y2_ref):
        x = x_ref[...]
        xn, _ = _rms(x)
        gm = vec_ref[V_G_PRE_MLP:V_G_PRE_MLP + 1, :] * (1.0 + mod_ref[M_SC_F:M_SC_F + 1, :])
        h2 = (xn * gm + mod_ref[M_SH_F:M_SH_F + 1, :]).astype(BF16)
        h2_ref[...] = h2
        y2 = jnp.zeros((ts, d), F32)
        for qb in range(N_CHIP):
            up = _dot(h2, wu_ref[qb])
            up_ref[:, qb * fq:(qb + 1) * fq] = up.astype(BF16)
            ru = jnp.maximum(up, 0.0)
            y2 = y2 + _dot((ru * ru).astype(BF16), wd_ref[qb])
        y2_ref[...] = y2.astype(BF16)
        yn, _ = _rms(y2)
        gg = mod_ref[M_GT_F:M_GT_F + 1, :] * vec_ref[V_G_POST_MLP:V_G_POST_MLP + 1, :]
        x2_ref[...] = x + yn * gg

    tile = pl.BlockSpec((ts, d), lambda i: (i, 0))
    return _pcall(
        body, name=name, grid=(s // ts,),
        out_shape=[jax.ShapeDtypeStruct((s, d), F32), jax.ShapeDtypeStruct((s, d), BF16),
                   jax.ShapeDtypeStruct((s, N_CHIP * fq), BF16), jax.ShapeDtypeStruct((s, d), BF16)],
        in_specs=[tile, _full(mod.shape), _full(vec.shape), _full(w_up.shape), _full(w_down.shape)],
        out_specs=[tile, tile, pl.BlockSpec((ts, N_CHIP * fq), lambda i: (i, 0)), tile],
        args=[x1, mod, vec, w_up, w_down], carries=carries)


def _loss_head(xf, target, name):
    s, d = xf.shape
    ts = min(TOKENS_MATMUL_TILE, s)

    def body(x_ref, t_ref, dx_ref, loss_ref):
        @pl.when(pl.program_id(0) == 0)
        def _():
            loss_ref[...] = jnp.zeros_like(loss_ref)

        err = x_ref[...] - t_ref[...]
        dx_ref[...] = err * (1.0 / d)
        part = jnp.sum(jnp.sum(err * err, axis=1, keepdims=True), axis=0, keepdims=True) * (0.5 / d)
        loss_ref[...] = loss_ref[...] + part

    tile = pl.BlockSpec((ts, d), lambda i: (i, 0))
    return pl.pallas_call(
        body, name=name, grid=(s // ts,),
        out_shape=[jax.ShapeDtypeStruct((s, d), F32), jax.ShapeDtypeStruct((SUBLANES, 128), F32)],
        in_specs=[tile, tile], out_specs=[tile, _full((SUBLANES, 128))],
        compiler_params=_cparams(1),
    )(xf, target)


SB3_DSH, SB3_DSC, SB3_DGT, SB3_DG_PRE, SB3_DG_POST = range(5)
SB1_DSH, SB1_DSC, SB1_DG_PRE = range(3)
(SB2_DGT, SB2_DG_POST, SB2_DWA, SB2_DBA, SB2_DWB, SB2_DBB, SB2_DLAM, SB2_DBR, SB2_DBI) = (0, 1, 2, 5, 6, 10, 11, 12, 13)


def _mlp_backward(dx2, x1, y2, up, mod, vec, w_up, w_down, name, carries=()):
    s, d = dx2.shape
    fq = w_up.shape[2]
    ts = min(TOKENS_MIXER_TILE, s)
    n_t = s // ts

    def body(dx2_ref, x_ref, y2_ref, up_ref, mod_ref, vec_ref, wu_ref, wd_ref,
             dx1_ref, dy2_ref, dup_ref, act_ref, small_ref):
        i = pl.program_id(0)

        @pl.when(i == 0)
        def _():
            small_ref[...] = jnp.zeros_like(small_ref)

        dout = dx2_ref[...]
        y2n, ry = _rms(y2_ref[...].astype(F32))
        g_post = vec_ref[V_G_POST_MLP:V_G_POST_MLP + 1, :]
        gt = mod_ref[M_GT_F:M_GT_F + 1, :]
        dgg = _colsum(dout * y2n)
        dy2 = _rms_bwd(dout * (gt * g_post), y2n, ry).astype(BF16)
        dy2_ref[...] = dy2
        dh2 = jnp.zeros((ts, d), F32)
        for qb in range(N_CHIP):
            cols = slice(qb * fq, (qb + 1) * fq)
            dact = _dot_tb(dy2, wd_ref[qb])
            ru = jnp.maximum(up_ref[:, cols].astype(F32), 0.0)
            dup = (dact * (2.0 * ru)).astype(BF16)
            dup_ref[:, cols] = dup
            act_ref[:, cols] = (ru * ru).astype(BF16)
            dh2 = dh2 + _dot_tb(dup, wu_ref[qb])
        xn, r = _rms(x_ref[...])
        g_pre = vec_ref[V_G_PRE_MLP:V_G_PRE_MLP + 1, :]
        sc1 = 1.0 + mod_ref[M_SC_F:M_SC_F + 1, :]
        dsh = _colsum(dh2)
        dgm = _colsum(dh2 * xn)
        dx1_ref[...] = dout + _rms_bwd(dh2 * (g_pre * sc1), xn, r)
        small_ref[SB3_DSH:SB3_DSH + 1, :] += dsh
        small_ref[SB3_DSC:SB3_DSC + 1, :] += dgm
        small_ref[SB3_DGT:SB3_DGT + 1, :] += dgg

        @pl.when(i == n_t - 1)
        def _():
            dgm_t = small_ref[SB3_DSC:SB3_DSC + 1, :]
            dgg_t = small_ref[SB3_DGT:SB3_DGT + 1, :]
            small_ref[SB3_DSC:SB3_DSC + 1, :] = dgm_t * g_pre
            small_ref[SB3_DG_PRE:SB3_DG_PRE + 1, :] = dgm_t * sc1
            small_ref[SB3_DGT:SB3_DGT + 1, :] = dgg_t * g_post
            small_ref[SB3_DG_POST:SB3_DG_POST + 1, :] = dgg_t * gt

    tile = pl.BlockSpec((ts, d), lambda i: (i, 0))
    wide = pl.BlockSpec((ts, N_CHIP * fq), lambda i: (i, 0))
    return _pcall(
        body, name=name, grid=(n_t,),
        out_shape=[jax.ShapeDtypeStruct((s, d), F32), jax.ShapeDtypeStruct((s, d), BF16),
                   jax.ShapeDtypeStruct((s, N_CHIP * fq), BF16), jax.ShapeDtypeStruct((s, N_CHIP * fq), BF16),
                   jax.ShapeDtypeStruct((SUBLANES, d), F32)],
        in_specs=[tile, tile, tile, wide, _full(mod.shape), _full(vec.shape), _full(w_up.shape), _full(w_down.shape)],
        out_specs=[tile, tile, wide, wide, _full((SUBLANES, d))],
        args=[dx2, x1, y2, up, mod, vec, w_up, w_down], carries=carries)


def _mixer_backward(dx1, proj, conva, xb2s, hhs, pas, pbs, ys, mod, vec, wg, w_a_out, w_b_out, w_o, name, carries=()):
    s, d = dx1.shape
    n_head, bw, _ = wg.shape
    ts = min(TOKENS_MIXER_TILE, s)
    n_t = s // ts

    def body(dx1_ref, p_ref, conva_ref, xb2_ref, hh_ref, pa_ref, pb_ref, y_ref, mod_ref, vec_ref,
             wg_ref, wa_ref, wb_ref, wo_ref,
             dp_ref, dy_ref, dpa_ref, dpb_ref, small_ref, dwg_ref,
             dconv_head, dxb2_head, a_head, g_head):
        i = pl.program_id(0)

        @pl.when(i == 0)
        def _():
            small_ref[...] = jnp.zeros_like(small_ref)
            dwg_ref[...] = jnp.zeros_like(dwg_ref)
            dconv_head[...] = jnp.zeros_like(dconv_head)
            dxb2_head[...] = jnp.zeros_like(dxb2_head)
            a_head[...] = jnp.zeros_like(a_head)
            g_head[...] = jnp.zeros_like(g_head)

        def seg(k):
            return p_ref[:, k * d:(k + 1) * d].astype(F32)

        def vrow(k):
            return vec_ref[k:k + 1, :]

        def acc(row, val):
            small_ref[row:row + 1, :] += val

        dout = dx1_ref[...]
        yn, ry = _rms(y_ref[...].astype(F32))
        g_post = vrow(V_G_POST_MIX)
        gt = mod_ref[M_GT_M:M_GT_M + 1, :]
        acc(SB2_DGT, _colsum(dout * yn))
        dy = _rms_bwd(dout * (gt * g_post), yn, ry).astype(BF16)
        dy_ref[...] = dy
        dm = _dot_tb(dy, wo_ref[...])
        u_a, u_b = seg(5), seg(6)
        sa, sb = _sigmoid(u_a), _sigmoid(u_b)
        dpa = (dm * sa).astype(BF16)
        dpb = (dm * sb).astype(BF16)
        dpa_ref[...] = dpa
        dpb_ref[...] = dpb
        du_a = dm * pa_ref[...].astype(F32) * (sa * (1.0 - sa))
        du_b = dm * pb_ref[...].astype(F32) * (sb * (1.0 - sb))
        dp_ref[:, 5 * d:6 * d] = du_a.astype(BF16)
        dp_ref[:, 6 * d:7 * d] = du_b.astype(BF16)
        dy_a = _dot_tb(dpa, wa_ref[...])
        dy_b = _dot_tb(dpb, wb_ref[...])

        b_a, c_a, v_a = seg(0), seg(1), seg(2)
        dp_ref[:, 0:d] = (dy_a * conva_ref[...].astype(F32)).astype(BF16)
        dconv = dy_a * b_a
        nxt = dconv_head[...]
        d1 = _shift_up(dconv, 1, nxt)
        d2 = _shift_up(dconv, 2, nxt)
        dconv_head[...] = dconv[:SUBLANES]
        dcv = vrow(V_CONV_A_W + 2) * dconv + vrow(V_CONV_A_W + 1) * d1 + vrow(V_CONV_A_W) * d2
        cv = c_a * v_a
        acc(SB2_DWA + 2, _colsum(cv * dconv))
        acc(SB2_DWA + 1, _colsum(cv * d1))
        acc(SB2_DWA, _colsum(cv * d2))
        acc(SB2_DBA, _colsum(dconv))
        dp_ref[:, d:2 * d] = (dcv * v_a).astype(BF16)
        dp_ref[:, 2 * d:3 * d] = (dcv * c_a).astype(BF16)

        x_b, g_b = seg(3), seg(4)
        hh = hh_ref[...]
        gel, th = _gelu(g_b)
        dp_ref[:, 4 * d:5 * d] = (dy_b * hh * _gelu_grad(g_b, th)).astype(BF16)
        dhh = dy_b * gel
        xb2 = xb2_ref[...]
        r, gi, sp, a, mult = _lru_coeffs(xb2, wg_ref, vec_ref, n_head, bw)
        a_next = _shift_up(a, 1, a_head[...])
        g_loc, a_cum = _scan_bwd(a_next, dhh)
        g = g_loc + a_cum * g_head[0:1, :]
        a_head[...] = a[:SUBLANES]
        g_head[...] = g[:SUBLANES]
        bb = mult * gi * xb2
        dlog_a = g * (hh - bb) - g * gi * xb2 * (a * a / mult)
        dgi = g * mult * xb2
        dxb2 = g * mult * gi
        acc(SB2_DLAM, _colsum(dlog_a * r))
        dzr = dlog_a * ((-LRU_C) * sp) * (r * (1.0 - r))
        dzi = dgi * (gi * (1.0 - gi))
        acc(SB2_DBR, _colsum(dzr))
        acc(SB2_DBI, _colsum(dzi))
        xb2_b = xb2.astype(BF16)
        back = []
        for hd in range(n_head):
            cols = slice(hd * bw, (hd + 1) * bw)
            dz = jnp.concatenate([dzr[:, cols], dzi[:, cols]], axis=1).astype(BF16)
            back.append(_dot_tb(dz, wg_ref[hd]))
            dwg_ref[hd] += _dot_ta(xb2_b[:, cols], dz)
        dxb2 = dxb2 + jnp.concatenate(back, axis=1)
        nxt = dxb2_head[...]
        e1 = _shift_up(dxb2, 1, nxt)
        e2 = _shift_up(dxb2, 2, nxt)
        e3 = _shift_up(dxb2, 3, nxt)
        dxb2_head[...] = dxb2[:SUBLANES]
        dp_ref[:, 3 * d:4 * d] = (vrow(V_CONV_B_W + 3) * dxb2 + vrow(V_CONV_B_W + 2) * e1
                                  + vrow(V_CONV_B_W + 1) * e2 + vrow(V_CONV_B_W) * e3).astype(BF16)
        acc(SB2_DWB + 3, _colsum(x_b * dxb2))
        acc(SB2_DWB + 2, _colsum(x_b * e1))
        acc(SB2_DWB + 1, _colsum(x_b * e2))
        acc(SB2_DWB, _colsum(x_b * e3))
        acc(SB2_DBB, _colsum(dxb2))

        @pl.when(i == n_t - 1)
        def _():
            dgg_t = small_ref[SB2_DGT:SB2_DGT + 1, :]
            small_ref[SB2_DGT:SB2_DGT + 1, :] = dgg_t * g_post
            small_ref[SB2_DG_POST:SB2_DG_POST + 1, :] = dgg_t * gt
            lam = vrow(V_LAMBDA)
            small_ref[SB2_DLAM:SB2_DLAM + 1, :] = small_ref[SB2_DLAM:SB2_DLAM + 1, :] * (LRU_C * _sigmoid(-lam))

    rev = lambda i: (n_t - 1 - i, 0)
    tile = pl.BlockSpec((ts, d), rev)
    wide = pl.BlockSpec((ts, 7 * d), rev)
    sd = lambda dt: jax.ShapeDtypeStruct((s, d), dt)
    return _pcall(
        body, name=name, grid=(n_t,),
        out_shape=[jax.ShapeDtypeStruct((s, 7 * d), BF16), sd(BF16), sd(BF16), sd(BF16),
                   jax.ShapeDtypeStruct((2 * SUBLANES, d), F32), jax.ShapeDtypeStruct(wg.shape, F32)],
        in_specs=[tile, wide, tile, tile, tile, tile, tile, tile, _full(mod.shape), _full(vec.shape),
                  _full(wg.shape), _full(w_a_out.shape), _full(w_b_out.shape), _full(w_o.shape)],
        out_specs=[wide, tile, tile, tile, _full((2 * SUBLANES, d)), _full(wg.shape)],
        scratch_shapes=[pltpu.VMEM((SUBLANES, d), F32)] * 4,
        args=[dx1, proj, conva, xb2s, hhs, pas, pbs, ys, mod, vec, wg, w_a_out, w_b_out, w_o], carries=carries)


def _proj_backward(dproj, dx1, x, mod, vec, w_in, name, carries=()):
    s, d = x.shape
    nq = w_in.shape[2]
    ts = min(TOKENS_MATMUL_TILE, s)
    n_t = s // ts

    def body(dp_ref, dx1_ref, x_ref, mod_ref, vec_ref, w_ref, dx_ref, small_ref):
        i = pl.program_id(0)

        @pl.when(i == 0)
        def _():
            small_ref[...] = jnp.zeros_like(small_ref)

        dh = jnp.zeros((ts, d), F32)
        for qb in range(N_CHIP):
            dh = dh + _dot_tb(dp_ref[:, qb * nq:(qb + 1) * nq], w_ref[qb])
        xn, r = _rms(x_ref[...])
        g_pre = vec_ref[V_G_PRE_MIX:V_G_PRE_MIX + 1, :]
        sc1 = 1.0 + mod_ref[M_SC_M:M_SC_M + 1, :]
        dx_ref[...] = dx1_ref[...] + _rms_bwd(dh * (g_pre * sc1), xn, r)
        small_ref[SB1_DSH:SB1_DSH + 1, :] += _colsum(dh)
        small_ref[SB1_DSC:SB1_DSC + 1, :] += _colsum(dh * xn)

        @pl.when(i == n_t - 1)
        def _():
            dgm_t = small_ref[SB1_DSC:SB1_DSC + 1, :]
            small_ref[SB1_DSC:SB1_DSC + 1, :] = dgm_t * g_pre
            small_ref[SB1_DG_PRE:SB1_DG_PRE + 1, :] = dgm_t * sc1

    tile = pl.BlockSpec((ts, d), lambda i: (i, 0))
    return _pcall(
        body, name=name, grid=(n_t,),
        out_shape=[jax.ShapeDtypeStruct((s, d), F32), jax.ShapeDtypeStruct((SUBLANES, d), F32)],
        in_specs=[pl.BlockSpec((ts, N_CHIP * nq), lambda i: (i, 0)), tile, tile, _full(mod.shape), _full(vec.shape),
                  _full(w_in.shape)],
        out_specs=[tile, _full((SUBLANES, d))],
        args=[dproj, dx1, x, mod, vec, w_in], carries=carries)


def _weight_grad(a, b, name, col_blocks=1, tk=512, carries=()):
    s, k = a.shape
    n = b.shape[1]
    tn = n // col_blocks
    tk = min(tk, k)

    def body(a_ref, b_ref, o_ref):
        o_ref[0] = _dot_ta(a_ref[...], b_ref[...])

    (out,), carried = _pcall(
        body, name=name, grid=(col_blocks, k // tk),
        out_shape=[jax.ShapeDtypeStruct((col_blocks, k, tn), F32)],
        in_specs=[pl.BlockSpec((s, tk), lambda j, i: (0, i)), pl.BlockSpec((s, tn), lambda j, i: (0, j))],
        out_specs=[pl.BlockSpec((1, tk, tn), lambda j, i: (j, i, 0))],
        args=[a, b], carries=carries)
    return out, carried


def _adamw(w, g, m, v, name):
    shape = w.shape
    cols = shape[-1]
    rows = w.size // cols
    tr = _row_tile(rows, cols, target_bytes=1024 * 1024)
    c1 = 1.0 - ADAM_B1 ** ADAM_STEP
    c2 = 1.0 - ADAM_B2 ** ADAM_STEP

    def body(w_ref, g_ref, m_ref, v_ref, d_ref, nm_ref, nv_ref):
        gv = g_ref[...]
        nm = ADAM_B1 * m_ref[...] + (1.0 - ADAM_B1) * gv
        nv = ADAM_B2 * v_ref[...] + (1.0 - ADAM_B2) * (gv * gv)
        nm_ref[...] = nm
        nv_ref[...] = nv
        d_ref[...] = (-ADAM_LR) * ((nm / c1) / (jnp.sqrt(nv / c2) + ADAM_EPS) + ADAM_WD * w_ref[...])

    spec = pl.BlockSpec((tr, cols), lambda i: (i, 0))
    outs = pl.pallas_call(
        body, name=name, grid=(rows // tr,),
        out_shape=[jax.ShapeDtypeStruct((rows, cols), F32)] * 3,
        in_specs=[spec] * 4, out_specs=[spec] * 3,
        compiler_params=_cparams(1),
    )(*(t.reshape(rows, cols) for t in (w, g, m, v)))
    return tuple(o.reshape(shape) for o in outs)


def kernel(x, c, w_mod, b_mod, g_pre_mix, g_post_mix, w_in, conv_a_w, conv_a_b, w_a_out, conv_b_w, conv_b_b, w_gate_r, b_gate_r, w_gate_i, b_gate_i, lru_lambda, w_b_out, w_o, g_pre_mlp, g_post_mlp, w_mlp_up, w_mlp_down, loss_target, m_w_mod, m_b_mod, m_g_pre_mix, m_g_post_mix, m_w_in, m_conv_a_w, m_conv_a_b, m_w_a_out, m_conv_b_w, m_conv_b_b, m_w_gate_r, m_b_gate_r, m_w_gate_i, m_b_gate_i, m_lru_lambda, m_w_b_out, m_w_o, m_g_pre_mlp, m_g_post_mlp, m_w_mlp_up, m_w_mlp_down, v_w_mod, v_b_mod, v_g_pre_mix, v_g_post_mix, v_w_in, v_conv_a_w, v_conv_a_b, v_w_a_out, v_conv_b_w, v_conv_b_b, v_w_gate_r, v_b_gate_r, v_w_gate_i, v_b_gate_i, v_lru_lambda, v_w_b_out, v_w_o, v_g_pre_mlp, v_g_post_mlp, v_w_mlp_up, v_w_mlp_down):
    weights = dict(w_mod=w_mod, b_mod=b_mod, g_pre_mix=g_pre_mix, g_post_mix=g_post_mix, w_in=w_in, conv_a_w=conv_a_w,
                   conv_a_b=conv_a_b, w_a_out=w_a_out, conv_b_w=conv_b_w, conv_b_b=conv_b_b, w_gate_r=w_gate_r,
                   b_gate_r=b_gate_r, w_gate_i=w_gate_i, b_gate_i=b_gate_i, lru_lambda=lru_lambda, w_b_out=w_b_out,
                   w_o=w_o, g_pre_mlp=g_pre_mlp, g_post_mlp=g_post_mlp, w_mlp_up=w_mlp_up, w_mlp_down=w_mlp_down)
    mom1 = dict(w_mod=m_w_mod, b_mod=m_b_mod, g_pre_mix=m_g_pre_mix, g_post_mix=m_g_post_mix, w_in=m_w_in,
                conv_a_w=m_conv_a_w, conv_a_b=m_conv_a_b, w_a_out=m_w_a_out, conv_b_w=m_conv_b_w, conv_b_b=m_conv_b_b,
                w_gate_r=m_w_gate_r, b_gate_r=m_b_gate_r, w_gate_i=m_w_gate_i, b_gate_i=m_b_gate_i,
                lru_lambda=m_lru_lambda, w_b_out=m_w_b_out, w_o=m_w_o, g_pre_mlp=m_g_pre_mlp, g_post_mlp=m_g_post_mlp,
                w_mlp_up=m_w_mlp_up, w_mlp_down=m_w_mlp_down)
    mom2 = dict(w_mod=v_w_mod, b_mod=v_b_mod, g_pre_mix=v_g_pre_mix, g_post_mix=v_g_post_mix, w_in=v_w_in,
                conv_a_w=v_conv_a_w, conv_a_b=v_conv_a_b, w_a_out=v_w_a_out, conv_b_w=v_conv_b_w, conv_b_b=v_conv_b_b,
                w_gate_r=v_w_gate_r, b_gate_r=v_b_gate_r, w_gate_i=v_w_gate_i, b_gate_i=v_b_gate_i,
                lru_lambda=v_lru_lambda, w_b_out=v_w_b_out, w_o=v_w_o, g_pre_mlp=v_g_pre_mlp, g_post_mlp=v_g_post_mlp,
                w_mlp_up=v_w_mlp_up, w_mlp_down=v_w_mlp_down)
    names = list(weights)

    n_layer = w_in.shape[0]
    s, d = x.shape[1], x.shape[2]
    n_head, bw = w_gate_r.shape[1], w_gate_r.shape[2]
    dq = d // N_CHIP
    mq = w_mod.shape[2]
    n_mod = (N_CHIP * mq) // d
    ka, kb = conv_a_w.shape[1], conv_b_w.shape[1]

    mx, my, mc = _place()
    q_me = 2 * mx + my
    q_arr = jnp.reshape(q_me, (1,)).astype(jnp.int32)

    n_conv_rows = n_layer * (ka + kb)
    conv_blk = -(-n_conv_rows // SUBLANES) * SUBLANES
    blk_rows = SUBLANES + conv_blk
    conv_rows = jnp.concatenate([jnp.concatenate([conv_a_w[l], conv_b_w[l]], axis=0) for l in range(n_layer)], axis=0)
    conv_rows = jnp.pad(conv_rows, ((0, conv_blk - n_conv_rows), (0, d - dq)))
    c_blk = jnp.pad(c, ((0, SUBLANES - 1), (0, 0)))
    gathered1 = _all_gather_small(jnp.concatenate([c_blk, conv_rows], axis=0), "gather_c_conv").reshape(N_DEV, blk_rows, d)
    c_all = gathered1[:, 0, :]
    conv_full = jnp.concatenate([gathered1[2 * qb, SUBLANES:SUBLANES + n_conv_rows, :dq] for qb in range(N_CHIP)], axis=1)

    b_mod_shard = lax.dynamic_slice_in_dim(b_mod, q_me * mq, mq, axis=1)
    mod_part = _mod_forward(c_all, w_mod, b_mod_shard, "mod_forward")
    gathered2 = _all_gather_small(mod_part, "gather_mod").reshape(N_DEV, n_layer, N_DEV, mq)
    me = 4 * mx + 2 * my + mc
    mod_rows = jnp.concatenate(
        [lax.dynamic_index_in_dim(gathered2[2 * qb], me, axis=1, keepdims=False) for qb in range(N_CHIP)], axis=1)
    mods = [jnp.pad(mod_rows[l].reshape(n_mod, d), ((0, SUBLANES - n_mod), (0, 0))) for l in range(n_layer)]

    vecs = []
    for l in range(n_layer):
        base = l * (ka + kb)
        rows = [g_pre_mix[l], g_post_mix[l], conv_a_b[l], conv_b_b[l], b_gate_r[l], b_gate_i[l], lru_lambda[l],
                g_pre_mlp[l], g_post_mlp[l]]
        vecs.append(jnp.concatenate([jnp.stack(rows, axis=0), conv_full[base:base + ka + kb]], axis=0))

    big_names = ["w_in", "w_a_out", "w_b_out", "w_o", "w_mlp_up", "w_mlp_down"]
    groups = [["w_in"], ["w_a_out", "w_b_out", "w_o"], ["w_mlp_up", "w_mlp_down"]]
    placed = {(nm, l): _cast_place(weights[nm], l, q_arr, f"cast_place_{nm}_{l}") for l in range(n_layer) for nm in big_names}
    wfull = [dict() for _ in range(n_layer)]
    stages = [(l, grp) for l in range(n_layer) for grp in groups]

    def gather_carry(stage):
        if stage >= len(stages):
            return []
        l, grp = stages[stage]
        return [_gather_carry([placed[(nm, l)] for nm in grp])]

    def gathered(stage, carried):
        if stage < len(stages):
            l, grp = stages[stage]
            for nm, w in zip(grp, carried[0]):
                wfull[l][nm] = w.reshape(d, d) if nm in groups[1] else w

    gathered(0, _run_carries(gather_carry(0), "gather_first"))
    wgs = [jnp.concatenate([w_gate_r[l], w_gate_i[l]], axis=-1).astype(BF16) for l in range(n_layer)]

    xs = x[0]
    saved = []
    for l in range(n_layer):
        wl = wfull[l]
        (h, proj), carried = _norm_proj(xs, mods[l], vecs[l], wl["w_in"], f"norm_proj_{l}", gather_carry(3 * l + 1))
        gathered(3 * l + 1, carried)
        (x1, conva, xb2, hh, ya, yb, pa, pb, mm, yy), carried = _mixer_forward(
            xs, proj, mods[l], vecs[l], wgs[l], wl["w_a_out"], wl["w_b_out"], wl["w_o"], f"mixer_forward_{l}",
            gather_carry(3 * l + 2))
        gathered(3 * l + 2, carried)
        (x2, h2, up, y2), carried = _mlp_forward(x1, mods[l], vecs[l], wl["w_mlp_up"], wl["w_mlp_down"],
                                                 f"mlp_forward_{l}", gather_carry(3 * l + 3))
        gathered(3 * l + 3, carried)
        saved.append(dict(x=xs, h=h, proj=proj, x1=x1, conva=conva, xb2=xb2, hh=hh, ya=ya, yb=yb, pa=pa, pb=pb, m=mm,
                          y=yy, h2=h2, up=up, y2=y2))
        xs = x2
    dxs, loss_tile = _loss_head(xs, loss_target[0], "loss_head")
    loss = lax.psum(loss_tile[0, 0], ("x", "y", "c"))

    chips_q = [q_me ^ 2, q_me ^ 1, q_me ^ 3]
    pf = jnp.stack([mc, q_me] + chips_q).astype(jnp.int32)
    rs = dict(grad={}, landed={}, to_send={}, from_chips={}, out={})
    to_exchange, to_scatter, to_join = [], [], []

    def ride(call, what, name=None):
        ex = list(to_exchange) if "x" in what else []
        sc = list(to_scatter) if "s" in what else []
        jn = []
        for key in (to_join if "j" in what else []):
            if key[0] not in [k[0] for k in jn]:
                jn.append(key)
        carries = []
        if ex:
            carries.append(_exchange_carry([rs["grad"][k] for k in ex]))
        if sc:
            carries.append(_scatter_carry([rs["to_send"][k] for k in sc]))
        if jn:
            carries.append(_join_carry([rs["out"][k[0]] for k in jn], [k[1] for k in jn]))
        if call is None:
            carried = _run_carries(carries, name) if carries else []
            res = None
        else:
            res, carried = call(carries)
        carried = list(carried)
        if ex:
            for k, ld in zip(ex, carried.pop(0)):
                to_exchange.remove(k)
                rs["landed"][k] = ld
                rs["to_send"][k] = _add_sibling_half(rs["grad"][k], ld, pf, f"rs_add_sibling_{k[0]}_{k[1]}")
                to_scatter.append(k)
        if sc:
            for k, fc in zip(sc, carried.pop(0)):
                to_scatter.remove(k)
                rs["out"][k[0]] = _add_chips(rs["grad"][k], rs["landed"][k], fc, pf, rs["out"].get(k[0]), k[1], n_layer,
                                             f"rs_add_chips_{k[0]}_{k[1]}")
                to_join.append(k)
        if jn:
            for k, o in zip(jn, carried.pop(0)):
                to_join.remove(k)
                rs["out"][k[0]] = o
        return res

    def ready(nm, l, g):
        rs["grad"][(nm, l)] = g
        to_exchange.append((nm, l))

    rowblk = lambda t: t.reshape(N_CHIP, t.shape[1] // N_CHIP, t.shape[2])
    smalls = [None] * n_layer
    for l in reversed(range(n_layer)):
        wl, sv = wfull[l], saved[l]
        dx1, dy2, dup, act, small3 = ride(lambda cr: _mlp_backward(
            dxs, sv["x1"], sv["y2"], sv["up"], mods[l], vecs[l], wl["w_mlp_up"], wl["w_mlp_down"], f"mlp_backward_{l}", cr), "xsj")
        ready("w_mlp_up", l, _weight_grad(sv["h2"], dup, f"grad_w_mlp_up_{l}", col_blocks=N_CHIP)[0])
        ready("w_mlp_down", l, rowblk(_weight_grad(act, dy2, f"grad_w_mlp_down_{l}")[0]))
        dproj, dy, dpa, dpb, small2, dwg = ride(lambda cr: _mixer_backward(
            dx1, sv["proj"], sv["conva"], sv["xb2"], sv["hh"], sv["pa"], sv["pb"], sv["y"], mods[l], vecs[l], wgs[l],
            wl["w_a_out"], wl["w_b_out"], wl["w_o"], f"mixer_backward_{l}", cr), "xsj")
        ready("w_a_out", l, rowblk(_weight_grad(sv["ya"], dpa, f"grad_w_a_out_{l}")[0]))
        ready("w_b_out", l, rowblk(_weight_grad(sv["yb"], dpb, f"grad_w_b_out_{l}")[0]))
        ready("w_o", l, rowblk(_weight_grad(sv["m"], dy, f"grad_w_o_{l}")[0]))
        g_in = ride(lambda cr: _weight_grad(sv["h"], dproj, f"grad_w_in_{l}", col_blocks=N_CHIP, carries=cr), "xs")
        ready("w_in", l, g_in)
        if l == 0:
            ride(None, "x", "rs_exchange_last")
        dxs, small1 = ride(lambda cr: _proj_backward(dproj, dx1, sv["x"], mods[l], vecs[l], wl["w_in"],
                                                     f"proj_backward_{l}", cr), "xsj")
        smalls[l] = (small1, small2, small3, dwg)
    grad_x = dxs[None]

    o1, o2, o3, o4 = 0, SUBLANES, 3 * SUBLANES, 4 * SUBLANES
    n_small = o4 + 2 * bw
    packed = jnp.concatenate([part for l in range(n_layer)
                              for part in (smalls[l][0], smalls[l][1], smalls[l][2], smalls[l][3].reshape(2 * bw, d))], axis=0)
    small_all = _all_gather_small(packed, "gather_small_grads")
    small_sum = _sum_devices(small_all, "sum_small_grads").reshape(n_layer, n_small, d)
    mod_rows_of = [o1 + SB1_DSH, o1 + SB1_DSC, o2 + SB2_DGT, o3 + SB3_DSH, o3 + SB3_DSC, o3 + SB3_DGT]
    per_dev = small_all.reshape(N_DEV, n_layer, n_small, d)
    dmod_all = jnp.concatenate([per_dev[:, :, k, :] for k in mod_rows_of], axis=2)
    dmod_all = jnp.swapaxes(dmod_all, 0, 1)
    grads = {}
    grads["w_mod"] = _mod_backward(c_all.T, lax.dynamic_slice_in_dim(dmod_all, q_me * mq, mq, axis=2), "mod_backward")
    grads["b_mod"] = jnp.concatenate([small_sum[:, k, :] for k in mod_rows_of], axis=1)
    grads["g_pre_mix"] = small_sum[:, o1 + SB1_DG_PRE]
    grads["g_post_mix"] = small_sum[:, o2 + SB2_DG_POST]
    grads["conv_a_w"] = lax.dynamic_slice_in_dim(small_sum[:, o2 + SB2_DWA:o2 + SB2_DWA + ka], q_me * dq, dq, axis=2)
    grads["conv_a_b"] = small_sum[:, o2 + SB2_DBA]
    grads["conv_b_w"] = lax.dynamic_slice_in_dim(small_sum[:, o2 + SB2_DWB:o2 + SB2_DWB + kb], q_me * dq, dq, axis=2)
    grads["conv_b_b"] = small_sum[:, o2 + SB2_DBB]
    grads["lru_lambda"] = small_sum[:, o2 + SB2_DLAM]
    grads["b_gate_r"] = small_sum[:, o2 + SB2_DBR]
    grads["b_gate_i"] = small_sum[:, o2 + SB2_DBI]
    grads["g_pre_mlp"] = small_sum[:, o3 + SB3_DG_PRE]
    grads["g_post_mlp"] = small_sum[:, o3 + SB3_DG_POST]
    dwg_sum = small_sum[:, o4:].reshape(n_layer, n_head, bw, 2 * bw)
    grads["w_gate_r"] = dwg_sum[..., :bw]
    grads["w_gate_i"] = dwg_sum[..., bw:]

    tail = 0
    while to_exchange or to_scatter or to_join:
        ride(None, "xsj", f"rs_tail_{tail}")
        tail += 1
    for nm in big_names:
        grads[nm] = rs["out"][nm].reshape(weights[nm].shape)

    deltas, new_m, new_v = {}, {}, {}
    for nm in names:
        deltas[nm], new_m[nm], new_v[nm] = _adamw(weights[nm], grads[nm], mom1[nm], mom2[nm], f"adamw_{nm}")
    return (loss, grad_x, *[grads[nm] for nm in names], *[deltas[nm] for nm in names],
            *[new_m[nm] for nm in names], *[new_v[nm] for nm in names])
```

```python
import jax
import jax.numpy as jnp
from jax import lax
from jax.experimental import pallas as pl
from jax.experimental.pallas import tpu as pltpu

F32 = jnp.float32
BF16 = jnp.bfloat16
MESH = pl.DeviceIdType.MESH

EPS = 1e-6
LRU_C = 8.0
N_CHIP = 4
N_DEV = 8
ADAM_LR = 0.001
ADAM_B1 = 0.9
ADAM_B2 = 0.999
ADAM_EPS = 1e-08
ADAM_WD = 0.01
ADAM_STEP = 10

VMEM_LIMIT_BYTES = 56 * 1024 * 1024
SUBLANES = 8
TOKENS_MATMUL_TILE = 512
TOKENS_MIXER_TILE = 256
GELU_K0 = 0.7978845608028654
GELU_K1 = 0.044715

V_G_PRE_MIX, V_G_POST_MIX, V_CONV_A_B, V_CONV_B_B, V_B_GATE_R, V_B_GATE_I, V_LAMBDA, V_G_PRE_MLP, V_G_POST_MLP = range(9)
V_CONV_A_W = 9
V_CONV_B_W = 12
M_SH_M, M_SC_M, M_GT_M, M_SH_F, M_SC_F, M_GT_F = range(6)


def _cparams(n_grid=0):
    sem = ("arbitrary",) * n_grid if n_grid else None
    return pltpu.CompilerParams(dimension_semantics=sem, vmem_limit_bytes=VMEM_LIMIT_BYTES)


def _full(shape):
    return pl.BlockSpec(shape, lambda *_: (0,) * len(shape))


def _dot(a, b):
    return jnp.dot(a, b, preferred_element_type=F32)


def _dot_tb(a, b):
    return lax.dot_general(a, b, (((1,), (1,)), ((), ())), preferred_element_type=F32)


def _dot_ta(a, b):
    return lax.dot_general(a, b, (((0,), (0,)), ((), ())), preferred_element_type=F32)


def _sigmoid(x):
    return 1.0 / (1.0 + jnp.exp(-x))


def _softplus(x):
    return jnp.maximum(x, 0.0) + jnp.log1p(jnp.exp(-jnp.abs(x)))


def _neg_expm1(x):
    series = -x * (1.0 + 0.5 * x * (1.0 + (x / 3.0) * (1.0 + 0.25 * x)))
    return jnp.where(x > -1e-2, series, 1.0 - jnp.exp(x))


def _gelu(x):
    t = jnp.tanh(GELU_K0 * (x + GELU_K1 * x * x * x))
    return 0.5 * x * (1.0 + t), t


def _gelu_grad(x, t):
    return 0.5 * (1.0 + t) + 0.5 * x * (1.0 - t * t) * GELU_K0 * (1.0 + 3.0 * GELU_K1 * x * x)


def _rms(x):
    r = lax.rsqrt(jnp.mean(x * x, axis=-1, keepdims=True) + EPS)
    return x * r, r


def _rms_bwd(dxn, xn, r):
    return r * (dxn - xn * jnp.mean(dxn * xn, axis=-1, keepdims=True))


def _colsum(x):
    return jnp.sum(x, axis=0, keepdims=True)


def _rows(t, w):
    return lax.broadcasted_iota(jnp.int32, (t, w), 0)


def _shift_down(x, k, prev8):
    t, w = x.shape
    rolled = pltpu.roll(x, k, 0)
    head = jnp.where(_rows(SUBLANES, w) < k, pltpu.roll(prev8, k, 0), rolled[:SUBLANES])
    return jnp.concatenate([head, rolled[SUBLANES:]], axis=0)


def _shift_up(x, k, next8):
    t, w = x.shape
    rolled = pltpu.roll(x, t - k, 0)
    tail = jnp.where(_rows(SUBLANES, w) >= SUBLANES - k, pltpu.roll(next8, SUBLANES - k, 0), rolled[t - SUBLANES:])
    return jnp.concatenate([rolled[:t - SUBLANES], tail], axis=0)


def _scan_fwd(a, b):
    t, w = a.shape
    rows = _rows(t, w)
    s = 1
    while s < t:
        keep = rows >= s
        b = b + a * jnp.where(keep, pltpu.roll(b, s, 0), 0.0)
        a = a * jnp.where(keep, pltpu.roll(a, s, 0), 1.0)
        s *= 2
    return b, a


def _scan_bwd(a, b):
    t, w = a.shape
    rows = _rows(t, w)
    s = 1
    while s < t:
        keep = rows < t - s
        b = b + a * jnp.where(keep, pltpu.roll(b, t - s, 0), 0.0)
        a = a * jnp.where(keep, pltpu.roll(a, t - s, 0), 1.0)
        s *= 2
    return b, a


def _row_tile(rows, cols, itemsize=4, target_bytes=2 * 1024 * 1024):
    if rows * cols * itemsize <= target_bytes or rows % SUBLANES:
        return rows
    t = max(SUBLANES, (target_bytes // (cols * itemsize)) // SUBLANES * SUBLANES)
    while rows % t:
        t -= SUBLANES
    return t


def _place():
    return lax.axis_index("x"), lax.axis_index("y"), lax.axis_index("c")


def _other_chips(x, y):
    chips = [(1 - x, y), (x, 1 - y), (1 - x, 1 - y)]
    return chips, [2 * cx + cy for cx, cy in chips]


def _all_gather_small(block, name):
    m_per, n = block.shape

    def body(x_ref, out_ref, send_sems, recv_sems, local_sem):
        x, y, c = _place()
        me, sibling = (x, y, c), (x, y, 1 - c)
        chips, _ = _other_chips(x, y)

        def rows(px, py, pc):
            return out_ref.at[pl.ds((4 * px + 2 * py + pc) * m_per, m_per), :]

        def copy(k, blk, to, src=None):
            return pltpu.make_async_remote_copy(
                src_ref=rows(*blk) if src is None else src, dst_ref=rows(*blk),
                send_sem=send_sems.at[k], recv_sem=recv_sems.at[k], device_id=to, device_id_type=MESH)

        mine = pltpu.make_async_copy(x_ref, rows(*me), local_sem)
        mine.start()
        first = [copy(0, me, sibling, src=x_ref)]
        first += [copy(1 + j, me, (*chip, c), src=x_ref) for j, chip in enumerate(chips)]
        for cp in first:
            cp.start()
        passed = [copy(4 + j, (*chip, c), sibling) for j, chip in enumerate(chips)]
        for j, chip in enumerate(chips):
            copy(1 + j, (*chip, c), me).wait_recv()
            passed[j].start()
        copy(0, sibling, me).wait_recv()
        for j, chip in enumerate(chips):
            copy(4 + j, (*chip, 1 - c), me).wait_recv()
        for cp in first + passed:
            cp.wait_send()
        mine.wait()

    return pl.pallas_call(
        body, name=name,
        out_shape=jax.ShapeDtypeStruct((N_DEV * m_per, n), block.dtype),
        in_specs=[pl.BlockSpec(memory_space=pltpu.VMEM)],
        out_specs=pl.BlockSpec(memory_space=pltpu.VMEM),
        scratch_shapes=[pltpu.SemaphoreType.DMA((7,)), pltpu.SemaphoreType.DMA((7,)), pltpu.SemaphoreType.DMA],
        compiler_params=pltpu.CompilerParams(vmem_limit_bytes=VMEM_LIMIT_BYTES),
    )(block)


def _cast_place(w, layer, q_arr, name):
    _, r, cols = w.shape
    tr = _row_tile(r, cols)

    def body(q_ref, w_ref, o_ref):
        o_ref[...] = w_ref[...].astype(BF16)

    return pl.pallas_call(
        body, name=name,
        out_shape=jax.ShapeDtypeStruct((N_CHIP, r, cols), BF16),
        grid_spec=pltpu.PrefetchScalarGridSpec(
            num_scalar_prefetch=1, grid=(r // tr,),
            in_specs=[pl.BlockSpec((1, tr, cols), lambda i, q_ref: (layer, i, 0))],
            out_specs=pl.BlockSpec((1, tr, cols), lambda i, q_ref: (q_ref[0], i, 0))),
        compiler_params=_cparams(1),
    )(q_arr, w)


class _Carry:
    def __init__(self, ins, out_shapes, aliases, sem_shapes, start, finish, mid=None):
        self.ins, self.out_shapes, self.aliases, self.sem_shapes = list(ins), list(out_shapes), dict(aliases), list(sem_shapes)
        self.start, self.mid, self.finish = start, mid, finish


def _pcall(body, *, name, grid, in_specs, out_specs, out_shape, args, scratch_shapes=(), carries=(), mid_frac=0.85):
    in_specs, out_specs, out_shape = list(in_specs), list(out_specs), list(out_shape)
    scratch_shapes, args = list(scratch_shapes), list(args)
    n_in, n_out, n_scr = len(in_specs), len(out_shape), len(scratch_shapes)
    steps = 1
    for g in grid:
        steps *= g
    mid_step = min(steps - 1, int(steps * mid_frac))
    any_spec = pl.BlockSpec(memory_space=pl.ANY)
    aliases = {}
    spans = []
    for cr in carries:
        spans.append((len(args), len(out_shape), len(scratch_shapes)))
        for a, b in cr.aliases.items():
            aliases[len(args) + a] = len(out_shape) + b
        args += cr.ins
        in_specs += [any_spec] * len(cr.ins)
        out_shape += cr.out_shapes
        out_specs += [any_spec] * len(cr.out_shapes)
        scratch_shapes += cr.sem_shapes
    n_all_in = len(args)
    n_all_out = len(out_shape)

    def wrapped(*refs):
        ins, outs, scr = refs[:n_all_in], refs[n_all_in:n_all_in + n_all_out], refs[n_all_in + n_all_out:]
        parts = [(cr, ins[a:a + len(cr.ins)], outs[b:b + len(cr.out_shapes)], scr[s:s + len(cr.sem_shapes)])
                 for cr, (a, b, s) in zip(carries, spans)]
        lin = 0
        for ax, g in enumerate(grid):
            lin = lin * g + pl.program_id(ax)

        def at(step, fn):
            if steps == 1:
                fn()
            else:
                pl.when(lin == step)(fn)

        def start_all():
            for cr, ci, co, cs in parts:
                cr.start(ci, co, cs)

        def mid_all():
            for cr, ci, co, cs in parts:
                if cr.mid is not None:
                    cr.mid(ci, co, cs)

        def finish_all():
            for cr, ci, co, cs in parts:
                cr.finish(ci, co, cs)

        if parts:
            at(0, start_all)
        body(*ins[:n_in], *outs[:n_out], *scr[:n_scr])
        if parts:
            at(mid_step, mid_all)
            at(steps - 1, finish_all)

    res = pl.pallas_call(
        wrapped, name=name, grid=tuple(grid), out_shape=out_shape, in_specs=in_specs, out_specs=out_specs,
        scratch_shapes=scratch_shapes, input_output_aliases=aliases, compiler_params=_cparams(len(grid)),
    )(*args)
    res = list(res)
    return res[:n_out], [res[b:b + len(cr.out_shapes)] for cr, (_, b, _) in zip(carries, spans)]


def _run_carries(carries, name):
    return _pcall(lambda: None, name=name, grid=(), in_specs=[], out_specs=[], out_shape=[], args=[], carries=carries)[1]


def _gather_carry(bufs):
    n = len(bufs)

    def copies(o_refs, sems):
        send_sems, recv_sems = sems
        x, y, c = _place()
        q = 2 * x + y
        sibling = (x, y, 1 - c)
        chips, qs = _other_chips(x, y)

        def half(w, shard, pc):
            rh = bufs[w].shape[1] // 2
            return o_refs[w].at[shard, pl.ds(pc * rh, rh), :]

        def over_ici(w, j, shard):
            blk = half(w, shard, c)
            return pltpu.make_async_remote_copy(
                src_ref=blk, dst_ref=blk, send_sem=send_sems.at[w, j], recv_sem=recv_sems.at[w, j],
                device_id=(*chips[j], c), device_id_type=MESH)

        def to_sibling(w, j, pc):
            blk = half(w, qs[j], pc)
            return pltpu.make_async_remote_copy(
                src_ref=blk, dst_ref=blk, send_sem=send_sems.at[w, 3 + j], recv_sem=recv_sems.at[w, 3 + j],
                device_id=sibling, device_id_type=MESH)

        return q, c, qs, over_ici, to_sibling

    pairs = [(w, j) for w in range(n) for j in range(3)]

    def start(i_refs, o_refs, sems):
        q, _, _, over_ici, _ = copies(o_refs, sems)
        for w, j in pairs:
            over_ici(w, j, q).start()

    def mid(i_refs, o_refs, sems):
        _, c, qs, over_ici, to_sibling = copies(o_refs, sems)
        for w, j in pairs:
            over_ici(w, j, qs[j]).wait_recv()
            to_sibling(w, j, c).start()

    def finish(i_refs, o_refs, sems):
        q, c, _, over_ici, to_sibling = copies(o_refs, sems)
        for w, j in pairs:
            to_sibling(w, j, 1 - c).wait_recv()
        for w, j in pairs:
            over_ici(w, j, q).wait_send()
            to_sibling(w, j, c).wait_send()

    return _Carry(bufs, [jax.ShapeDtypeStruct(b.shape, b.dtype) for b in bufs], {w: w for w in range(n)},
                  [pltpu.SemaphoreType.DMA((n, 6)), pltpu.SemaphoreType.DMA((n, 6))], start, finish, mid)


def _exchange_carry(grads):
    n = len(grads)

    def copies(g_refs, l_refs, sems):
        send_sems, recv_sems = sems
        x, y, c = _place()
        out = []
        for w in range(n):
            rh = grads[w].shape[1] // 2
            out.append(pltpu.make_async_remote_copy(
                src_ref=g_refs[w].at[:, pl.ds((1 - c) * rh, rh), :], dst_ref=l_refs[w],
                send_sem=send_sems.at[w], recv_sem=recv_sems.at[w], device_id=(x, y, 1 - c), device_id_type=MESH))
        return out

    def start(g_refs, l_refs, sems):
        for cp in copies(g_refs, l_refs, sems):
            cp.start()

    def finish(g_refs, l_refs, sems):
        for cp in copies(g_refs, l_refs, sems):
            cp.wait()

    return _Carry(grads, [jax.ShapeDtypeStruct((N_CHIP, g.shape[1] // 2, g.shape[2]), g.dtype) for g in grads], {},
                  [pltpu.SemaphoreType.DMA((n,)), pltpu.SemaphoreType.DMA((n,))], start, finish)


def _scatter_carry(sums):
    n = len(sums)

    def copies(s_refs, l_refs, sems):
        send_sems, recv_sems = sems
        x, y, c = _place()
        chips, _ = _other_chips(x, y)
        return [pltpu.make_async_remote_copy(
            src_ref=s_refs[w].at[j], dst_ref=l_refs[w].at[j], send_sem=send_sems.at[w, j], recv_sem=recv_sems.at[w, j],
            device_id=(*chips[j], c), device_id_type=MESH) for w in range(n) for j in range(3)]

    def start(s_refs, l_refs, sems):
        for cp in copies(s_refs, l_refs, sems):
            cp.start()

    def finish(s_refs, l_refs, sems):
        for cp in copies(s_refs, l_refs, sems):
            cp.wait()

    return _Carry(sums, [jax.ShapeDtypeStruct(s.shape, s.dtype) for s in sums], {},
                  [pltpu.SemaphoreType.DMA((n, 3)), pltpu.SemaphoreType.DMA((n, 3))], start, finish)


def _join_carry(outs, layers):
    n = len(outs)

    def copy(o_refs, sems, w, mine):
        send_sems, recv_sems = sems
        x, y, c = _place()
        r = outs[w].shape[1]
        rows = o_refs[w].at[layers[w], pl.ds((c if mine else 1 - c) * (r // 2), r // 2), :]
        return pltpu.make_async_remote_copy(
            src_ref=rows, dst_ref=rows, send_sem=send_sems.at[w], recv_sem=recv_sems.at[w],
            device_id=(x, y, 1 - c), device_id_type=MESH)

    def start(i_refs, o_refs, sems):
        for w in range(n):
            copy(o_refs, sems, w, True).start()

    def finish(i_refs, o_refs, sems):
        for w in range(n):
            copy(o_refs, sems, w, True).wait_send()
        for w in range(n):
            copy(o_refs, sems, w, False).wait_recv()

    return _Carry(outs, [jax.ShapeDtypeStruct(o.shape, o.dtype) for o in outs], {w: w for w in range(n)},
                  [pltpu.SemaphoreType.DMA((n,)), pltpu.SemaphoreType.DMA((n,))], start, finish)


PF_C, PF_Q, PF_QS = 0, 1, 2


def _add_sibling_half(g, landed, pf, name):
    _, r, cols = g.shape
    rh = r // 2
    tr = _row_tile(rh, cols)
    nr = rh // tr

    def body(pf_ref, g_ref, l_ref, o_ref):
        o_ref[...] = (g_ref[...] + l_ref[...]).astype(BF16)

    return pl.pallas_call(
        body, name=name,
        out_shape=jax.ShapeDtypeStruct((3, rh, cols), BF16),
        grid_spec=pltpu.PrefetchScalarGridSpec(
            num_scalar_prefetch=1, grid=(3, nr),
            in_specs=[pl.BlockSpec((1, tr, cols), lambda j, i, pf_ref: (pf_ref[PF_QS + j], pf_ref[PF_C] * nr + i, 0)),
                      pl.BlockSpec((1, tr, cols), lambda j, i, pf_ref: (pf_ref[PF_QS + j], i, 0))],
            out_specs=pl.BlockSpec((1, tr, cols), lambda j, i, pf_ref: (j, i, 0))),
        compiler_params=_cparams(2),
    )(pf, g, landed)


def _add_chips(g, landed, from_chips, pf, prev, layer, n_layer, name):
    _, r, cols = g.shape
    rh = r // 2
    tr = _row_tile(rh, cols)
    nr = rh // tr

    def body(pf_ref, g_ref, l_ref, f_ref, *rest):
        o_ref = rest[-1]
        acc = g_ref[0] + l_ref[0]
        for j in range(3):
            acc = acc + f_ref[j].astype(F32)
        o_ref[0] = acc

    in_specs = [pl.BlockSpec((1, tr, cols), lambda i, pf_ref: (pf_ref[PF_Q], pf_ref[PF_C] * nr + i, 0)),
                pl.BlockSpec((1, tr, cols), lambda i, pf_ref: (pf_ref[PF_Q], i, 0)),
                pl.BlockSpec((3, tr, cols), lambda i, pf_ref: (0, i, 0))]
    args = [pf, g, landed, from_chips]
    aliases = {}
    if prev is not None:
        in_specs.append(pl.BlockSpec(memory_space=pl.ANY))
        args.append(prev)
        aliases = {4: 0}
    return pl.pallas_call(
        body, name=name,
        out_shape=jax.ShapeDtypeStruct((n_layer, r, cols), F32),
        grid_spec=pltpu.PrefetchScalarGridSpec(
            num_scalar_prefetch=1, grid=(nr,), in_specs=in_specs,
            out_specs=pl.BlockSpec((1, tr, cols), lambda i, pf_ref: (layer, pf_ref[PF_C] * nr + i, 0))),
        input_output_aliases=aliases,
        compiler_params=_cparams(1),
    )(*args)


def _allgather_carry(blocks):
    n = len(blocks)

    def copies(b_refs, o_refs, sems):
        send_sems, recv_sems = sems
        x, y, c = _place()
        chips, _ = _other_chips(x, y)

        def place(w, px, py, pc):
            return o_refs[w].at[4 * px + 2 * py + pc]

        def own_to(w, k, to):
            dst = place(w, x, y, c)
            return pltpu.make_async_remote_copy(src_ref=b_refs[w], dst_ref=dst, send_sem=send_sems.at[w, k],
                                                recv_sem=recv_sems.at[w, k], device_id=to, device_id_type=MESH)

        def landed_from(w, k, px, py, pc):
            blk = place(w, px, py, pc)
            return pltpu.make_async_remote_copy(src_ref=blk, dst_ref=blk, send_sem=send_sems.at[w, k],
                                                recv_sem=recv_sems.at[w, k], device_id=(x, y, 1 - c), device_id_type=MESH)

        return x, y, c, chips, own_to, landed_from

    def start(b_refs, o_refs, sems):
        x, y, c, chips, own_to, _ = copies(b_refs, o_refs, sems)
        for w in range(n):
            own_to(w, 0, (x, y, 1 - c)).start()
            for j, chip in enumerate(chips):
                own_to(w, 1 + j, (*chip, c)).start()

    def mid(b_refs, o_refs, sems):
        x, y, c, chips, _, landed_from = copies(b_refs, o_refs, sems)
        for w in range(n):
            for j, chip in enumerate(chips):
                landed_from(w, 1 + j, *chip, c).wait_recv()
                landed_from(w, 4 + j, *chip, c).start()

    def finish(b_refs, o_refs, sems):
        x, y, c, chips, own_to, landed_from = copies(b_refs, o_refs, sems)
        for w in range(n):
            landed_from(w, 0, x, y, 1 - c).wait_recv()
            for j, chip in enumerate(chips):
                landed_from(w, 4 + j, *chip, 1 - c).wait_recv()
            own_to(w, 0, (x, y, 1 - c)).wait_send()
            for j, chip in enumerate(chips):
                own_to(w, 1 + j, (*chip, c)).wait_send()
                landed_from(w, 4 + j, *chip, c).wait_send()

    return _Carry(blocks, [jax.ShapeDtypeStruct((N_DEV,) + b.shape, b.dtype) for b in blocks], {},
                  [pltpu.SemaphoreType.DMA((n, 7)), pltpu.SemaphoreType.DMA((n, 7))], start, finish, mid)


def _sum_devices(gathered, own, me_arr, name):
    _, m, n = gathered.shape
    tr = _row_tile(m, n, target_bytes=512 * 1024)

    def body(me_ref, g_ref, own_ref, o_ref):
        me = me_ref[0]
        acc = None
        for dev in range(N_DEV):
            term = jnp.where(me == dev, own_ref[...], g_ref[dev])
            acc = term if acc is None else acc + term
        o_ref[...] = acc

    return pl.pallas_call(
        body, name=name,
        out_shape=jax.ShapeDtypeStruct((m, n), F32),
        grid_spec=pltpu.PrefetchScalarGridSpec(
            num_scalar_prefetch=1, grid=(m // tr,),
            in_specs=[pl.BlockSpec((N_DEV, tr, n), lambda i, me_ref: (0, i, 0)),
                      pl.BlockSpec((tr, n), lambda i, me_ref: (i, 0))],
            out_specs=pl.BlockSpec((tr, n), lambda i, me_ref: (i, 0))),
        compiler_params=_cparams(1),
    )(me_arr, gathered, own)


def _mod_forward(c_all, w_mod, b_mod_shard, name):
    n_layer, d, mq = w_mod.shape

    def body(c_ref, w_ref, b_ref, o_ref):
        cv = c_ref[...]
        o_ref[...] = _dot(cv * _sigmoid(cv), w_ref[0]) + b_ref[0]

    return pl.pallas_call(
        body, name=name, grid=(n_layer,),
        out_shape=jax.ShapeDtypeStruct((n_layer * N_DEV, mq), F32),
        in_specs=[_full((N_DEV, d)), pl.BlockSpec((1, d, mq), lambda l: (l, 0, 0)),
                  pl.BlockSpec((1, 1, mq), lambda l: (l, 0, 0))],
        out_specs=pl.BlockSpec((N_DEV, mq), lambda l: (l, 0)),
        compiler_params=_cparams(1),
    )(c_all, w_mod, b_mod_shard.reshape(n_layer, 1, mq))


def _mod_backward(c_all_t, dmod_shard, name):
    n_layer, _, mq = dmod_shard.shape
    d = c_all_t.shape[0]

    def body(c_ref, dm_ref, o_ref):
        cv = c_ref[...]
        o_ref[0] = _dot(cv * _sigmoid(cv), dm_ref[0])

    return pl.pallas_call(
        body, name=name, grid=(n_layer,),
        out_shape=jax.ShapeDtypeStruct((n_layer, d, mq), F32),
        in_specs=[_full((d, N_DEV)), pl.BlockSpec((1, N_DEV, mq), lambda l: (l, 0, 0))],
        out_specs=pl.BlockSpec((1, d, mq), lambda l: (l, 0, 0)),
        compiler_params=_cparams(1),
    )(c_all_t, dmod_shard)


def _norm_proj(x, mod, vec, w_in, name, carries=()):
    s, d = x.shape
    nq = w_in.shape[2]
    ts = min(TOKENS_MATMUL_TILE, s)

    def body(x_ref, mod_ref, vec_ref, w_ref, h_ref, p_ref):
        xn, _ = _rms(x_ref[...])
        gm = vec_ref[V_G_PRE_MIX:V_G_PRE_MIX + 1, :] * (1.0 + mod_ref[M_SC_M:M_SC_M + 1, :])
        h = (xn * gm + mod_ref[M_SH_M:M_SH_M + 1, :]).astype(BF16)
        h_ref[...] = h
        for qb in range(N_CHIP):
            p_ref[:, qb * nq:(qb + 1) * nq] = _dot(h, w_ref[qb]).astype(BF16)

    return _pcall(
        body, name=name, grid=(s // ts,),
        out_shape=[jax.ShapeDtypeStruct((s, d), BF16), jax.ShapeDtypeStruct((s, N_CHIP * nq), BF16)],
        in_specs=[pl.BlockSpec((ts, d), lambda i: (i, 0)), _full(mod.shape), _full(vec.shape), _full(w_in.shape)],
        out_specs=[pl.BlockSpec((ts, d), lambda i: (i, 0)), pl.BlockSpec((ts, N_CHIP * nq), lambda i: (i, 0))],
        args=[x, mod, vec, w_in], carries=carries)


def _gate_pre(xb2_b, wg_ref, n_head, bw):
    zr, zi = [], []
    for hd in range(n_head):
        z = _dot(xb2_b[:, hd * bw:(hd + 1) * bw], wg_ref[hd])
        zr.append(z[:, :bw])
        zi.append(z[:, bw:])
    return jnp.concatenate(zr, axis=1), jnp.concatenate(zi, axis=1)


def _lru_coeffs(xb2, wg_ref, vec_ref, n_head, bw):
    zr, zi = _gate_pre(xb2.astype(BF16), wg_ref, n_head, bw)
    r = _sigmoid(zr + vec_ref[V_B_GATE_R:V_B_GATE_R + 1, :])
    gi = _sigmoid(zi + vec_ref[V_B_GATE_I:V_B_GATE_I + 1, :])
    sp = _softplus(-vec_ref[V_LAMBDA:V_LAMBDA + 1, :])
    log_a = (-LRU_C) * r * sp
    a = jnp.exp(log_a)
    mult = jnp.sqrt(_neg_expm1(2.0 * log_a))
    return r, gi, sp, a, mult


def _mixer_forward(x, proj, mod, vec, wg, w_a_out, w_b_out, w_o, name, carries=()):
    s, d = x.shape
    n_head, bw, _ = wg.shape
    ts = min(TOKENS_MIXER_TILE, s)

    def body(x_ref, p_ref, mod_ref, vec_ref, wg_ref, wa_ref, wb_ref, wo_ref,
             x1_ref, conva_ref, xb2_ref, hh_ref, ya_ref, yb_ref, pa_ref, pb_ref, m_ref, y_ref,
             cv_tail, xb_tail, h_last):
        i = pl.program_id(0)

        @pl.when(i == 0)
        def _():
            cv_tail[...] = jnp.zeros_like(cv_tail)
            xb_tail[...] = jnp.zeros_like(xb_tail)
            h_last[...] = jnp.zeros_like(h_last)

        def seg(k):
            return p_ref[:, k * d:(k + 1) * d].astype(F32)

        def vrow(k):
            return vec_ref[k:k + 1, :]

        b_a, c_a, v_a, x_b, g_b, u_a, u_b = (seg(k) for k in range(7))
        cv = c_a * v_a
        prev_cv = cv_tail[...]
        conv_a = (vrow(V_CONV_A_B) + vrow(V_CONV_A_W) * _shift_down(cv, 2, prev_cv)
                  + vrow(V_CONV_A_W + 1) * _shift_down(cv, 1, prev_cv) + vrow(V_CONV_A_W + 2) * cv)
        cv_tail[...] = cv[ts - SUBLANES:]
        y_a = b_a * conv_a
        prev_xb = xb_tail[...]
        xb2 = (vrow(V_CONV_B_B) + vrow(V_CONV_B_W) * _shift_down(x_b, 3, prev_xb)
               + vrow(V_CONV_B_W + 1) * _shift_down(x_b, 2, prev_xb)
               + vrow(V_CONV_B_W + 2) * _shift_down(x_b, 1, prev_xb) + vrow(V_CONV_B_W + 3) * x_b)
        xb_tail[...] = x_b[ts - SUBLANES:]
        _, gi, _, a, mult = _lru_coeffs(xb2, wg_ref, vec_ref, n_head, bw)
        h_loc, a_cum = _scan_fwd(a, mult * gi * xb2)
        hh = h_loc + a_cum * h_last[SUBLANES - 1:SUBLANES, :]
        h_last[...] = hh[ts - SUBLANES:]
        gel, _ = _gelu(g_b)
        y_b = hh * gel
        ya_b, yb_b = y_a.astype(BF16), y_b.astype(BF16)
        pa = _dot(ya_b, wa_ref[...])
        pb = _dot(yb_b, wb_ref[...])
        m = (_sigmoid(u_a) * pa + _sigmoid(u_b) * pb).astype(BF16)
        y = _dot(m, wo_ref[...])
        yn, _ = _rms(y)
        gg = mod_ref[M_GT_M:M_GT_M + 1, :] * vrow(V_G_POST_MIX)
        x1_ref[...] = x_ref[...] + yn * gg
        conva_ref[...] = conv_a.astype(BF16)
        xb2_ref[...] = xb2
        hh_ref[...] = hh
        ya_ref[...] = ya_b
        yb_ref[...] = yb_b
        pa_ref[...] = pa.astype(BF16)
        pb_ref[...] = pb.astype(BF16)
        m_ref[...] = m
        y_ref[...] = y.astype(BF16)

    tile = pl.BlockSpec((ts, d), lambda i: (i, 0))
    sd = lambda dt: jax.ShapeDtypeStruct((s, d), dt)
    return _pcall(
        body, name=name, grid=(s // ts,),
        out_shape=[sd(F32), sd(BF16), sd(F32), sd(F32), sd(BF16), sd(BF16), sd(BF16), sd(BF16), sd(BF16), sd(BF16)],
        in_specs=[tile, pl.BlockSpec((ts, 7 * d), lambda i: (i, 0)), _full(mod.shape), _full(vec.shape),
                  _full(wg.shape), _full(w_a_out.shape), _full(w_b_out.shape), _full(w_o.shape)],
        out_specs=[tile] * 10,
        scratch_shapes=[pltpu.VMEM((SUBLANES, d), F32), pltpu.VMEM((SUBLANES, d), F32), pltpu.VMEM((SUBLANES, d), F32)],
        args=[x, proj, mod, vec, wg, w_a_out, w_b_out, w_o], carries=carries)


def _mlp_forward(x1, mod, vec, w_up, w_down, name, carries=()):
    s, d = x1.shape
    fq = w_up.shape[2]
    ts = min(TOKENS_MATMUL_TILE, s)

    def body(x_ref, mod_ref, vec_ref, wu_ref, wd_ref, x2_ref, h2_ref, up_ref, y2_ref):
        x = x_ref[...]
        xn, _ = _rms(x)
        gm = vec_ref[V_G_PRE_MLP:V_G_PRE_MLP + 1, :] * (1.0 + mod_ref[M_SC_F:M_SC_F + 1, :])
        h2 = (xn * gm + mod_ref[M_SH_F:M_SH_F + 1, :]).astype(BF16)
        h2_ref[...] = h2
        y2 = jnp.zeros((ts, d), F32)
        for qb in range(N_CHIP):
            up = _dot(h2, wu_ref[qb])
            up_ref[:, qb * fq:(qb + 1) * fq] = up.astype(BF16)
            ru = jnp.maximum(up, 0.0)
            y2 = y2 + _dot((ru * ru).astype(BF16), wd_ref[qb])
        y2_ref[...] = y2.astype(BF16)
        yn, _ = _rms(y2)
        gg = mod_ref[M_GT_F:M_GT_F + 1, :] * vec_ref[V_G_POST_MLP:V_G_POST_MLP + 1, :]
        x2_ref[...] = x + yn * gg

    tile = pl.BlockSpec((ts, d), lambda i: (i, 0))
    return _pcall(
        body, name=name, grid=(s // ts,),
        out_shape=[jax.ShapeDtypeStruct((s, d), F32), jax.ShapeDtypeStruct((s, d), BF16),
                   jax.ShapeDtypeStruct((s, N_CHIP * fq), BF16), jax.ShapeDtypeStruct((s, d), BF16)],
        in_specs=[tile, _full(mod.shape), _full(vec.shape), _full(w_up.shape), _full(w_down.shape)],
        out_specs=[tile, tile, pl.BlockSpec((ts, N_CHIP * fq), lambda i: (i, 0)), tile],
        args=[x1, mod, vec, w_up, w_down], carries=carries)


def _loss_head(xf, target, name):
    s, d = xf.shape
    ts = min(TOKENS_MATMUL_TILE, s)

    def body(x_ref, t_ref, dx_ref, loss_ref):
        @pl.when(pl.program_id(0) == 0)
        def _():
            loss_ref[...] = jnp.zeros_like(loss_ref)

        err = x_ref[...] - t_ref[...]
        dx_ref[...] = err * (1.0 / d)
        part = jnp.sum(jnp.sum(err * err, axis=1, keepdims=True), axis=0, keepdims=True) * (0.5 / d)
        loss_ref[...] = loss_ref[...] + part

    tile = pl.BlockSpec((ts, d), lambda i: (i, 0))
    return pl.pallas_call(
        body, name=name, grid=(s // ts,),
        out_shape=[jax.ShapeDtypeStruct((s, d), F32), jax.ShapeDtypeStruct((SUBLANES, 128), F32)],
        in_specs=[tile, tile], out_specs=[tile, _full((SUBLANES, 128))],
        compiler_params=_cparams(1),
    )(xf, target)


SB3_DSH, SB3_DSC, SB3_DGT, SB3_DG_PRE, SB3_DG_POST = range(5)
SB1_DSH, SB1_DSC, SB1_DG_PRE = range(3)
(SB2_DGT, SB2_DG_POST, SB2_DWA, SB2_DBA, SB2_DWB, SB2_DBB, SB2_DLAM, SB2_DBR, SB2_DBI) = (0, 1, 2, 5, 6, 10, 11, 12, 13)


def _mlp_backward(dx2, x1, y2, up, mod, vec, w_up, w_down, name, carries=()):
    s, d = dx2.shape
    fq = w_up.shape[2]
    ts = min(TOKENS_MIXER_TILE, s)
    n_t = s // ts

    def body(dx2_ref, x_ref, y2_ref, up_ref, mod_ref, vec_ref, wu_ref, wd_ref,
             dx1_ref, dy2_ref, dup_ref, act_ref, small_ref):
        i = pl.program_id(0)

        @pl.when(i == 0)
        def _():
            small_ref[...] = jnp.zeros_like(small_ref)

        dout = dx2_ref[...]
        y2n, ry = _rms(y2_ref[...].astype(F32))
        g_post = vec_ref[V_G_POST_MLP:V_G_POST_MLP + 1, :]
        gt = mod_ref[M_GT_F:M_GT_F + 1, :]
        dgg = _colsum(dout * y2n)
        dy2 = _rms_bwd(dout * (gt * g_post), y2n, ry).astype(BF16)
        dy2_ref[...] = dy2
        dh2 = jnp.zeros((ts, d), F32)
        for qb in range(N_CHIP):
            cols = slice(qb * fq, (qb + 1) * fq)
            dact = _dot_tb(dy2, wd_ref[qb])
            ru = jnp.maximum(up_ref[:, cols].astype(F32), 0.0)
            dup = (dact * (2.0 * ru)).astype(BF16)
            dup_ref[:, cols] = dup
            act_ref[:, cols] = (ru * ru).astype(BF16)
            dh2 = dh2 + _dot_tb(dup, wu_ref[qb])
        xn, r = _rms(x_ref[...])
        g_pre = vec_ref[V_G_PRE_MLP:V_G_PRE_MLP + 1, :]
        sc1 = 1.0 + mod_ref[M_SC_F:M_SC_F + 1, :]
        dsh = _colsum(dh2)
        dgm = _colsum(dh2 * xn)
        dx1_ref[...] = dout + _rms_bwd(dh2 * (g_pre * sc1), xn, r)
        small_ref[SB3_DSH:SB3_DSH + 1, :] += dsh
        small_ref[SB3_DSC:SB3_DSC + 1, :] += dgm
        small_ref[SB3_DGT:SB3_DGT + 1, :] += dgg

        @pl.when(i == n_t - 1)
        def _():
            dgm_t = small_ref[SB3_DSC:SB3_DSC + 1, :]
            dgg_t = small_ref[SB3_DGT:SB3_DGT + 1, :]
            small_ref[SB3_DSC:SB3_DSC + 1, :] = dgm_t * g_pre
            small_ref[SB3_DG_PRE:SB3_DG_PRE + 1, :] = dgm_t * sc1
            small_ref[SB3_DGT:SB3_DGT + 1, :] = dgg_t * g_post
            small_ref[SB3_DG_POST:SB3_DG_POST + 1, :] = dgg_t * gt

    tile = pl.BlockSpec((ts, d), lambda i: (i, 0))
    wide = pl.BlockSpec((ts, N_CHIP * fq), lambda i: (i, 0))
    return _pcall(
        body, name=name, grid=(n_t,),
        out_shape=[jax.ShapeDtypeStruct((s, d), F32), jax.ShapeDtypeStruct((s, d), BF16),
                   jax.ShapeDtypeStruct((s, N_CHIP * fq), BF16), jax.ShapeDtypeStruct((s, N_CHIP * fq), BF16),
                   jax.ShapeDtypeStruct((SUBLANES, d), F32)],
        in_specs=[tile, tile, tile, wide, _full(mod.shape), _full(vec.shape), _full(w_up.shape), _full(w_down.shape)],
        out_specs=[tile, tile, wide, wide, _full((SUBLANES, d))],
        args=[dx2, x1, y2, up, mod, vec, w_up, w_down], carries=carries)


def _mixer_backward(dx1, proj, conva, xb2s, hhs, pas, pbs, ys, mod, vec, wg, w_a_out, w_b_out, w_o, name, carries=()):
    s, d = dx1.shape
    n_head, bw, _ = wg.shape
    ts = min(TOKENS_MIXER_TILE, s)
    n_t = s // ts

    def body(dx1_ref, p_ref, conva_ref, xb2_ref, hh_ref, pa_ref, pb_ref, y_ref, mod_ref, vec_ref,
             wg_ref, wa_ref, wb_ref, wo_ref,
             dp_ref, dy_ref, dpa_ref, dpb_ref, small_ref, dwg_ref,
             dconv_head, dxb2_head, a_head, g_head):
        i = pl.program_id(0)

        @pl.when(i == 0)
        def _():
            small_ref[...] = jnp.zeros_like(small_ref)
            dwg_ref[...] = jnp.zeros_like(dwg_ref)
            dconv_head[...] = jnp.zeros_like(dconv_head)
            dxb2_head[...] = jnp.zeros_like(dxb2_head)
            a_head[...] = jnp.zeros_like(a_head)
            g_head[...] = jnp.zeros_like(g_head)

        def seg(k):
            return p_ref[:, k * d:(k + 1) * d].astype(F32)

        def vrow(k):
            return vec_ref[k:k + 1, :]

        def acc(row, val):
            small_ref[row:row + 1, :] += val

        dout = dx1_ref[...]
        yn, ry = _rms(y_ref[...].astype(F32))
        g_post = vrow(V_G_POST_MIX)
        gt = mod_ref[M_GT_M:M_GT_M + 1, :]
        acc(SB2_DGT, _colsum(dout * yn))
        dy = _rms_bwd(dout * (gt * g_post), yn, ry).astype(BF16)
        dy_ref[...] = dy
        dm = _dot_tb(dy, wo_ref[...])
        u_a, u_b = seg(5), seg(6)
        sa, sb = _sigmoid(u_a), _sigmoid(u_b)
        dpa = (dm * sa).astype(BF16)
        dpb = (dm * sb).astype(BF16)
        dpa_ref[...] = dpa
        dpb_ref[...] = dpb
        du_a = dm * pa_ref[...].astype(F32) * (sa * (1.0 - sa))
        du_b = dm * pb_ref[...].astype(F32) * (sb * (1.0 - sb))
        dp_ref[:, 5 * d:6 * d] = du_a.astype(BF16)
        dp_ref[:, 6 * d:7 * d] = du_b.astype(BF16)
        dy_a = _dot_tb(dpa, wa_ref[...])
        dy_b = _dot_tb(dpb, wb_ref[...])

        b_a, c_a, v_a = seg(0), seg(1), seg(2)
        dp_ref[:, 0:d] = (dy_a * conva_ref[...].astype(F32)).astype(BF16)
        dconv = dy_a * b_a
        nxt = dconv_head[...]
        d1 = _shift_up(dconv, 1, nxt)
        d2 = _shift_up(dconv, 2, nxt)
        dconv_head[...] = dconv[:SUBLANES]
        dcv = vrow(V_CONV_A_W + 2) * dconv + vrow(V_CONV_A_W + 1) * d1 + vrow(V_CONV_A_W) * d2
        cv = c_a * v_a
        acc(SB2_DWA + 2, _colsum(cv * dconv))
        acc(SB2_DWA + 1, _colsum(cv * d1))
        acc(SB2_DWA, _colsum(cv * d2))
        acc(SB2_DBA, _colsum(dconv))
        dp_ref[:, d:2 * d] = (dcv * v_a).astype(BF16)
        dp_ref[:, 2 * d:3 * d] = (dcv * c_a).astype(BF16)

        x_b, g_b = seg(3), seg(4)
        hh = hh_ref[...]
        gel, th = _gelu(g_b)
        dp_ref[:, 4 * d:5 * d] = (dy_b * hh * _gelu_grad(g_b, th)).astype(BF16)
        dhh = dy_b * gel
        xb2 = xb2_ref[...]
        r, gi, sp, a, mult = _lru_coeffs(xb2, wg_ref, vec_ref, n_head, bw)
        a_next = _shift_up(a, 1, a_head[...])
        g_loc, a_cum = _scan_bwd(a_next, dhh)
        g = g_loc + a_cum * g_head[0:1, :]
        a_head[...] = a[:SUBLANES]
        g_head[...] = g[:SUBLANES]
        bb = mult * gi * xb2
        dlog_a = g * (hh - bb) - g * gi * xb2 * (a * a / mult)
        dgi = g * mult * xb2
        dxb2 = g * mult * gi
        acc(SB2_DLAM, _colsum(dlog_a * r))
        dzr = dlog_a * ((-LRU_C) * sp) * (r * (1.0 - r))
        dzi = dgi * (gi * (1.0 - gi))
        acc(SB2_DBR, _colsum(dzr))
        acc(SB2_DBI, _colsum(dzi))
        xb2_b = xb2.astype(BF16)
        back = []
        for hd in range(n_head):
            cols = slice(hd * bw, (hd + 1) * bw)
            dz = jnp.concatenate([dzr[:, cols], dzi[:, cols]], axis=1).astype(BF16)
            back.append(_dot_tb(dz, wg_ref[hd]))
            dwg_ref[hd] += _dot_ta(xb2_b[:, cols], dz)
        dxb2 = dxb2 + jnp.concatenate(back, axis=1)
        nxt = dxb2_head[...]
        e1 = _shift_up(dxb2, 1, nxt)
        e2 = _shift_up(dxb2, 2, nxt)
        e3 = _shift_up(dxb2, 3, nxt)
        dxb2_head[...] = dxb2[:SUBLANES]
        dp_ref[:, 3 * d:4 * d] = (vrow(V_CONV_B_W + 3) * dxb2 + vrow(V_CONV_B_W + 2) * e1
                                  + vrow(V_CONV_B_W + 1) * e2 + vrow(V_CONV_B_W) * e3).astype(BF16)
        acc(SB2_DWB + 3, _colsum(x_b * dxb2))
        acc(SB2_DWB + 2, _colsum(x_b * e1))
        acc(SB2_DWB + 1, _colsum(x_b * e2))
        acc(SB2_DWB, _colsum(x_b * e3))
        acc(SB2_DBB, _colsum(dxb2))

        @pl.when(i == n_t - 1)
        def _():
            dgg_t = small_ref[SB2_DGT:SB2_DGT + 1, :]
            small_ref[SB2_DGT:SB2_DGT + 1, :] = dgg_t * g_post
            small_ref[SB2_DG_POST:SB2_DG_POST + 1, :] = dgg_t * gt
            lam = vrow(V_LAMBDA)
            small_ref[SB2_DLAM:SB2_DLAM + 1, :] = small_ref[SB2_DLAM:SB2_DLAM + 1, :] * (LRU_C * _sigmoid(-lam))

    rev = lambda i: (n_t - 1 - i, 0)
    tile = pl.BlockSpec((ts, d), rev)
    wide = pl.BlockSpec((ts, 7 * d), rev)
    sd = lambda dt: jax.ShapeDtypeStruct((s, d), dt)
    return _pcall(
        body, name=name, grid=(n_t,),
        out_shape=[jax.ShapeDtypeStruct((s, 7 * d), BF16), sd(BF16), sd(BF16), sd(BF16),
                   jax.ShapeDtypeStruct((2 * SUBLANES, d), F32), jax.ShapeDtypeStruct(wg.shape, F32)],
        in_specs=[tile, wide, tile, tile, tile, tile, tile, tile, _full(mod.shape), _full(vec.shape),
                  _full(wg.shape), _full(w_a_out.shape), _full(w_b_out.shape), _full(w_o.shape)],
        out_specs=[wide, tile, tile, tile, _full((2 * SUBLANES, d)), _full(wg.shape)],
        scratch_shapes=[pltpu.VMEM((SUBLANES, d), F32)] * 4,
        args=[dx1, proj, conva, xb2s, hhs, pas, pbs, ys, mod, vec, wg, w_a_out, w_b_out, w_o], carries=carries)


def _proj_backward(dproj, dx1, x, mod, vec, w_in, name, carries=()):
    s, d = x.shape
    nq = w_in.shape[2]
    ts = min(TOKENS_MATMUL_TILE, s)
    n_t = s // ts

    def body(dp_ref, dx1_ref, x_ref, mod_ref, vec_ref, w_ref, dx_ref, small_ref):
        i = pl.program_id(0)

        @pl.when(i == 0)
        def _():
            small_ref[...] = jnp.zeros_like(small_ref)

        dh = jnp.zeros((ts, d), F32)
        for qb in range(N_CHIP):
            dh = dh + _dot_tb(dp_ref[:, qb * nq:(qb + 1) * nq], w_ref[qb])
        xn, r = _rms(x_ref[...])
        g_pre = vec_ref[V_G_PRE_MIX:V_G_PRE_MIX + 1, :]
        sc1 = 1.0 + mod_ref[M_SC_M:M_SC_M + 1, :]
        dx_ref[...] = dx1_ref[...] + _rms_bwd(dh * (g_pre * sc1), xn, r)
        small_ref[SB1_DSH:SB1_DSH + 1, :] += _colsum(dh)
        small_ref[SB1_DSC:SB1_DSC + 1, :] += _colsum(dh * xn)

        @pl.when(i == n_t - 1)
        def _():
            dgm_t = small_ref[SB1_DSC:SB1_DSC + 1, :]
            small_ref[SB1_DSC:SB1_DSC + 1, :] = dgm_t * g_pre
            small_ref[SB1_DG_PRE:SB1_DG_PRE + 1, :] = dgm_t * sc1

    tile = pl.BlockSpec((ts, d), lambda i: (i, 0))
    return _pcall(
        body, name=name, grid=(n_t,),
        out_shape=[jax.ShapeDtypeStruct((s, d), F32), jax.ShapeDtypeStruct((SUBLANES, d), F32)],
        in_specs=[pl.BlockSpec((ts, N_CHIP * nq), lambda i: (i, 0)), tile, tile, _full(mod.shape), _full(vec.shape),
                  _full(w_in.shape)],
        out_specs=[tile, _full((SUBLANES, d))],
        args=[dproj, dx1, x, mod, vec, w_in], carries=carries)


def _weight_grad(a, b, name, col_blocks=1, tk=512, carries=()):
    s, k = a.shape
    n = b.shape[1]
    tn = n // col_blocks
    tk = min(tk, k)

    def body(a_ref, b_ref, o_ref):
        o_ref[0] = _dot_ta(a_ref[...], b_ref[...])

    (out,), carried = _pcall(
        body, name=name, grid=(col_blocks, k // tk),
        out_shape=[jax.ShapeDtypeStruct((col_blocks, k, tn), F32)],
        in_specs=[pl.BlockSpec((s, tk), lambda j, i: (0, i)), pl.BlockSpec((s, tn), lambda j, i: (0, j))],
        out_specs=[pl.BlockSpec((1, tk, tn), lambda j, i: (j, i, 0))],
        args=[a, b], carries=carries)
    return out, carried


def _adamw(w, g, m, v, name, copy_grad=False):
    shape = w.shape
    cols = shape[-1]
    rows = w.size // cols
    tr = _row_tile(rows, cols, target_bytes=1024 * 1024)
    c1 = 1.0 - ADAM_B1 ** ADAM_STEP
    c2 = 1.0 - ADAM_B2 ** ADAM_STEP
    n_out = 4 if copy_grad else 3

    def body(w_ref, g_ref, m_ref, v_ref, d_ref, nm_ref, nv_ref, *g_out):
        gv = g_ref[...]
        nm = ADAM_B1 * m_ref[...] + (1.0 - ADAM_B1) * gv
        nv = ADAM_B2 * v_ref[...] + (1.0 - ADAM_B2) * (gv * gv)
        nm_ref[...] = nm
        nv_ref[...] = nv
        d_ref[...] = (-ADAM_LR) * ((nm / c1) / (jnp.sqrt(nv / c2) + ADAM_EPS) + ADAM_WD * w_ref[...])
        if copy_grad:
            g_out[0][...] = gv

    spec = pl.BlockSpec((tr, cols), lambda i: (i, 0))
    outs = pl.pallas_call(
        body, name=name, grid=(rows // tr,),
        out_shape=[jax.ShapeDtypeStruct((rows, cols), F32)] * n_out,
        in_specs=[spec] * 4, out_specs=[spec] * n_out,
        compiler_params=_cparams(1),
    )(*(t.reshape(rows, cols) for t in (w, g, m, v)))
    return tuple(o.reshape(shape) for o in outs)


def kernel(x, c, w_mod, b_mod, g_pre_mix, g_post_mix, w_in, conv_a_w, conv_a_b, w_a_out, conv_b_w, conv_b_b, w_gate_r, b_gate_r, w_gate_i, b_gate_i, lru_lambda, w_b_out, w_o, g_pre_mlp, g_post_mlp, w_mlp_up, w_mlp_down, loss_target, m_w_mod, m_b_mod, m_g_pre_mix, m_g_post_mix, m_w_in, m_conv_a_w, m_conv_a_b, m_w_a_out, m_conv_b_w, m_conv_b_b, m_w_gate_r, m_b_gate_r, m_w_gate_i, m_b_gate_i, m_lru_lambda, m_w_b_out, m_w_o, m_g_pre_mlp, m_g_post_mlp, m_w_mlp_up, m_w_mlp_down, v_w_mod, v_b_mod, v_g_pre_mix, v_g_post_mix, v_w_in, v_conv_a_w, v_conv_a_b, v_w_a_out, v_conv_b_w, v_conv_b_b, v_w_gate_r, v_b_gate_r, v_w_gate_i, v_b_gate_i, v_lru_lambda, v_w_b_out, v_w_o, v_g_pre_mlp, v_g_post_mlp, v_w_mlp_up, v_w_mlp_down):
    weights = dict(w_mod=w_mod, b_mod=b_mod, g_pre_mix=g_pre_mix, g_post_mix=g_post_mix, w_in=w_in, conv_a_w=conv_a_w,
                   conv_a_b=conv_a_b, w_a_out=w_a_out, conv_b_w=conv_b_w, conv_b_b=conv_b_b, w_gate_r=w_gate_r,
                   b_gate_r=b_gate_r, w_gate_i=w_gate_i, b_gate_i=b_gate_i, lru_lambda=lru_lambda, w_b_out=w_b_out,
                   w_o=w_o, g_pre_mlp=g_pre_mlp, g_post_mlp=g_post_mlp, w_mlp_up=w_mlp_up, w_mlp_down=w_mlp_down)
    mom1 = dict(w_mod=m_w_mod, b_mod=m_b_mod, g_pre_mix=m_g_pre_mix, g_post_mix=m_g_post_mix, w_in=m_w_in,
                conv_a_w=m_conv_a_w, conv_a_b=m_conv_a_b, w_a_out=m_w_a_out, conv_b_w=m_conv_b_w, conv_b_b=m_conv_b_b,
                w_gate_r=m_w_gate_r, b_gate_r=m_b_gate_r, w_gate_i=m_w_gate_i, b_gate_i=m_b_gate_i,
                lru_lambda=m_lru_lambda, w_b_out=m_w_b_out, w_o=m_w_o, g_pre_mlp=m_g_pre_mlp, g_post_mlp=m_g_post_mlp,
                w_mlp_up=m_w_mlp_up, w_mlp_down=m_w_mlp_down)
    mom2 = dict(w_mod=v_w_mod, b_mod=v_b_mod, g_pre_mix=v_g_pre_mix, g_post_mix=v_g_post_mix, w_in=v_w_in,
                conv_a_w=v_conv_a_w, conv_a_b=v_conv_a_b, w_a_out=v_w_a_out, conv_b_w=v_conv_b_w, conv_b_b=v_conv_b_b,
                w_gate_r=v_w_gate_r, b_gate_r=v_b_gate_r, w_gate_i=v_w_gate_i, b_gate_i=v_b_gate_i,
                lru_lambda=v_lru_lambda, w_b_out=v_w_b_out, w_o=v_w_o, g_pre_mlp=v_g_pre_mlp, g_post_mlp=v_g_post_mlp,
                w_mlp_up=v_w_mlp_up, w_mlp_down=v_w_mlp_down)
    names = list(weights)

    n_layer = w_in.shape[0]
    s, d = x.shape[1], x.shape[2]
    n_head, bw = w_gate_r.shape[1], w_gate_r.shape[2]
    dq = d // N_CHIP
    mq = w_mod.shape[2]
    n_mod = (N_CHIP * mq) // d
    ka, kb = conv_a_w.shape[1], conv_b_w.shape[1]

    mx, my, mc = _place()
    q_me = 2 * mx + my
    q_arr = jnp.reshape(q_me, (1,)).astype(jnp.int32)

    n_conv_rows = n_layer * (ka + kb)
    conv_blk = -(-n_conv_rows // SUBLANES) * SUBLANES
    blk_rows = SUBLANES + conv_blk
    conv_rows = jnp.concatenate([jnp.concatenate([conv_a_w[l], conv_b_w[l]], axis=0) for l in range(n_layer)], axis=0)
    conv_rows = jnp.pad(conv_rows, ((0, conv_blk - n_conv_rows), (0, d - dq)))
    c_blk = jnp.pad(c, ((0, SUBLANES - 1), (0, 0)))
    gathered1 = _all_gather_small(jnp.concatenate([c_blk, conv_rows], axis=0), "gather_c_conv").reshape(N_DEV, blk_rows, d)
    c_all = gathered1[:, 0, :]
    conv_full = jnp.concatenate([gathered1[2 * qb, SUBLANES:SUBLANES + n_conv_rows, :dq] for qb in range(N_CHIP)], axis=1)

    b_mod_shard = lax.dynamic_slice_in_dim(b_mod, q_me * mq, mq, axis=1)
    mod_part = _mod_forward(c_all, w_mod, b_mod_shard, "mod_forward")
    gathered2 = _all_gather_small(mod_part, "gather_mod").reshape(N_DEV, n_layer, N_DEV, mq)
    me = 4 * mx + 2 * my + mc
    mod_rows = jnp.concatenate(
        [lax.dynamic_index_in_dim(gathered2[2 * qb], me, axis=1, keepdims=False) for qb in range(N_CHIP)], axis=1)
    mods = [jnp.pad(mod_rows[l].reshape(n_mod, d), ((0, SUBLANES - n_mod), (0, 0))) for l in range(n_layer)]

    vecs = []
    for l in range(n_layer):
        base = l * (ka + kb)
        rows = [g_pre_mix[l], g_post_mix[l], conv_a_b[l], conv_b_b[l], b_gate_r[l], b_gate_i[l], lru_lambda[l],
                g_pre_mlp[l], g_post_mlp[l]]
        vecs.append(jnp.concatenate([jnp.stack(rows, axis=0), conv_full[base:base + ka + kb]], axis=0))

    big_names = ["w_in", "w_a_out", "w_b_out", "w_o", "w_mlp_up", "w_mlp_down"]
    groups = [["w_in"], ["w_a_out", "w_b_out", "w_o"], ["w_mlp_up", "w_mlp_down"]]
    placed = {(nm, l): _cast_place(weights[nm], l, q_arr, f"cast_place_{nm}_{l}") for l in range(n_layer) for nm in big_names}
    wfull = [dict() for _ in range(n_layer)]
    stages = [(l, grp) for l in range(n_layer) for grp in groups]

    def gather_carry(stage):
        if stage >= len(stages):
            return []
        l, grp = stages[stage]
        return [_gather_carry([placed[(nm, l)] for nm in grp])]

    def gathered(stage, carried):
        if stage < len(stages):
            l, grp = stages[stage]
            for nm, w in zip(grp, carried[0]):
                wfull[l][nm] = w.reshape(d, d) if nm in groups[1] else w

    gathered(0, _run_carries(gather_carry(0), "gather_first"))
    wgs = [jnp.concatenate([w_gate_r[l], w_gate_i[l]], axis=-1).astype(BF16) for l in range(n_layer)]

    xs = x[0]
    saved = []
    for l in range(n_layer):
        wl = wfull[l]
        (h, proj), carried = _norm_proj(xs, mods[l], vecs[l], wl["w_in"], f"norm_proj_{l}", gather_carry(3 * l + 1))
        gathered(3 * l + 1, carried)
        (x1, conva, xb2, hh, ya, yb, pa, pb, mm, yy), carried = _mixer_forward(
            xs, proj, mods[l], vecs[l], wgs[l], wl["w_a_out"], wl["w_b_out"], wl["w_o"], f"mixer_forward_{l}",
            gather_carry(3 * l + 2))
        gathered(3 * l + 2, carried)
        (x2, h2, up, y2), carried = _mlp_forward(x1, mods[l], vecs[l], wl["w_mlp_up"], wl["w_mlp_down"],
                                                 f"mlp_forward_{l}", gather_carry(3 * l + 3))
        gathered(3 * l + 3, carried)
        saved.append(dict(x=xs, h=h, proj=proj, x1=x1, conva=conva, xb2=xb2, hh=hh, ya=ya, yb=yb, pa=pa, pb=pb, m=mm,
                          y=yy, h2=h2, up=up, y2=y2))
        xs = x2
    dxs, loss_tile = _loss_head(xs, loss_target[0], "loss_head")
    loss = lax.psum(loss_tile[0, 0], ("x", "y", "c"))

    chips_q = [q_me ^ 2, q_me ^ 1, q_me ^ 3]
    pf = jnp.stack([mc, q_me] + chips_q).astype(jnp.int32)
    rs = dict(grad={}, landed={}, to_send={}, from_chips={}, out={})
    to_exchange, to_scatter, to_join, to_gather = [], [], [], []
    small_own, small_all = {}, {}

    def ride(call, what, name=None):
        ex = list(to_exchange) if "x" in what else []
        sc = list(to_scatter) if "s" in what else []
        ga = list(to_gather) if "g" in what else []
        jn = []
        for key in (to_join if "j" in what else []):
            if key[0] not in [k[0] for k in jn]:
                jn.append(key)
        carries = []
        if ex:
            carries.append(_exchange_carry([rs["grad"][k] for k in ex]))
        if sc:
            carries.append(_scatter_carry([rs["to_send"][k] for k in sc]))
        if jn:
            carries.append(_join_carry([rs["out"][k[0]] for k in jn], [k[1] for k in jn]))
        if ga:
            carries.append(_allgather_carry([small_own[k] for k in ga]))
        if call is None:
            carried = _run_carries(carries, name) if carries else []
            res = None
        else:
            res, carried = call(carries)
        carried = list(carried)
        if ex:
            for k, ld in zip(ex, carried.pop(0)):
                to_exchange.remove(k)
                rs["landed"][k] = ld
                rs["to_send"][k] = _add_sibling_half(rs["grad"][k], ld, pf, f"rs_add_sibling_{k[0]}_{k[1]}")
                to_scatter.append(k)
        if sc:
            for k, fc in zip(sc, carried.pop(0)):
                to_scatter.remove(k)
                rs["out"][k[0]] = _add_chips(rs["grad"][k], rs["landed"][k], fc, pf, rs["out"].get(k[0]), k[1], n_layer,
                                             f"rs_add_chips_{k[0]}_{k[1]}")
                to_join.append(k)
        if jn:
            for k, o in zip(jn, carried.pop(0)):
                to_join.remove(k)
                rs["out"][k[0]] = o
        if ga:
            for k, o in zip(ga, carried.pop(0)):
                to_gather.remove(k)
                small_all[k] = o
        return res

    def gather_small(key, parts):
        small_own[key] = parts[0] if len(parts) == 1 else jnp.concatenate(parts, axis=0)
        to_gather.append(key)

    def ready(nm, l, g):
        rs["grad"][(nm, l)] = g
        to_exchange.append((nm, l))

    rowblk = lambda t: t.reshape(N_CHIP, t.shape[1] // N_CHIP, t.shape[2])
    small1_prev = None
    for l in reversed(range(n_layer)):
        wl, sv = wfull[l], saved[l]
        dx1, dy2, dup, act, small3 = ride(lambda cr: _mlp_backward(
            dxs, sv["x1"], sv["y2"], sv["up"], mods[l], vecs[l], wl["w_mlp_up"], wl["w_mlp_down"], f"mlp_backward_{l}", cr), "xsjg")
        ready("w_mlp_up", l, _weight_grad(sv["h2"], dup, f"grad_w_mlp_up_{l}", col_blocks=N_CHIP)[0])
        g_down = ride(lambda cr: _weight_grad(act, dy2, f"grad_w_mlp_down_{l}", carries=cr), "x")
        ready("w_mlp_down", l, rowblk(g_down))
        ride(None, "x", f"rs_exchange_down_{l}")
        dproj, dy, dpa, dpb, small2, dwg = ride(lambda cr: _mixer_backward(
            dx1, sv["proj"], sv["conva"], sv["xb2"], sv["hh"], sv["pa"], sv["pb"], sv["y"], mods[l], vecs[l], wgs[l],
            wl["w_a_out"], wl["w_b_out"], wl["w_o"], f"mixer_backward_{l}", cr), "xsjg")
        gather_small(("late", l), ([small1_prev] if small1_prev is not None else []) + [small2, small3, dwg.reshape(2 * bw, d)])
        ready("w_a_out", l, rowblk(_weight_grad(sv["ya"], dpa, f"grad_w_a_out_{l}")[0]))
        g_b = ride(lambda cr: _weight_grad(sv["yb"], dpb, f"grad_w_b_out_{l}", carries=cr), "x")
        ready("w_b_out", l, rowblk(g_b))
        g_o = ride(lambda cr: _weight_grad(sv["m"], dy, f"grad_w_o_{l}", carries=cr), "x")
        ready("w_o", l, rowblk(g_o))
        g_in = ride(lambda cr: _weight_grad(sv["h"], dproj, f"grad_w_in_{l}", col_blocks=N_CHIP, carries=cr), "xsg")
        ready("w_in", l, g_in)
        if l == 0:
            ride(None, "x", "rs_exchange_last")
        dxs, small1_prev = ride(lambda cr: _proj_backward(dproj, dx1, sv["x"], mods[l], vecs[l], wl["w_in"],
                                                          f"proj_backward_{l}", cr), "xsjg")
    grad_x = dxs[None]
    gather_small(("last", 0), [small1_prev])

    tail = 0
    while to_exchange or to_scatter or to_join or to_gather:
        ride(None, "xsjg", f"rs_tail_{tail}")
        tail += 1

    me_dev = 4 * mx + 2 * my + mc
    me_arr = jnp.reshape(me_dev, (1,)).astype(jnp.int32)
    is_me = (jnp.arange(N_DEV) == me_dev)[:, None, None]
    sums = {k: _sum_devices(small_all[k], small_own[k], me_arr, f"sum_small_{k[0]}_{k[1]}") for k in small_own}

    def rows_of(l, part):
        if part == 0:
            return (("late", l - 1), 0) if l >= 1 else (("last", 0), 0)
        base = SUBLANES if l < n_layer - 1 else 0
        return ("late", l), base + (0, 0, 2 * SUBLANES, 3 * SUBLANES)[part]

    def summed(l, part, row, n_rows=1):
        key, base = rows_of(l, part)
        return sums[key][base + row:base + row + n_rows]

    def per_device(l, part, row):
        key, base = rows_of(l, part)
        return jnp.where(is_me, small_own[key][None, base + row:base + row + 1], small_all[key][:, base + row:base + row + 1])

    mod_rows = [(0, SB1_DSH), (0, SB1_DSC), (1, SB2_DGT), (2, SB3_DSH), (2, SB3_DSC), (2, SB3_DGT)]
    dmod_all = jnp.stack([jnp.concatenate([per_device(l, p, r)[:, 0, :] for p, r in mod_rows], axis=1)
                          for l in range(n_layer)], axis=0)
    o1, o2, o3, o4 = 0, SUBLANES, 3 * SUBLANES, 4 * SUBLANES
    small_sum = jnp.stack([jnp.concatenate([summed(l, 0, 0, SUBLANES), summed(l, 1, 0, 2 * SUBLANES),
                                            summed(l, 2, 0, SUBLANES), summed(l, 3, 0, 2 * bw)], axis=0)
                           for l in range(n_layer)], axis=0)
    mod_rows_of = [o1 + SB1_DSH, o1 + SB1_DSC, o2 + SB2_DGT, o3 + SB3_DSH, o3 + SB3_DSC, o3 + SB3_DGT]
    grads = {}
    grads["w_mod"] = _mod_backward(c_all.T, lax.dynamic_slice_in_dim(dmod_all, q_me * mq, mq, axis=2), "mod_backward")
    grads["b_mod"] = jnp.concatenate([small_sum[:, k, :] for k in mod_rows_of], axis=1)
    grads["g_pre_mix"] = small_sum[:, o1 + SB1_DG_PRE]
    grads["g_post_mix"] = small_sum[:, o2 + SB2_DG_POST]
    grads["conv_a_w"] = lax.dynamic_slice_in_dim(small_sum[:, o2 + SB2_DWA:o2 + SB2_DWA + ka], q_me * dq, dq, axis=2)
    grads["conv_a_b"] = small_sum[:, o2 + SB2_DBA]
    grads["conv_b_w"] = lax.dynamic_slice_in_dim(small_sum[:, o2 + SB2_DWB:o2 + SB2_DWB + kb], q_me * dq, dq, axis=2)
    grads["conv_b_b"] = small_sum[:, o2 + SB2_DBB]
    grads["lru_lambda"] = small_sum[:, o2 + SB2_DLAM]
    grads["b_gate_r"] = small_sum[:, o2 + SB2_DBR]
    grads["b_gate_i"] = small_sum[:, o2 + SB2_DBI]
    grads["g_pre_mlp"] = small_sum[:, o3 + SB3_DG_PRE]
    grads["g_post_mlp"] = small_sum[:, o3 + SB3_DG_POST]
    dwg_sum = small_sum[:, o4:].reshape(n_layer, n_head, bw, 2 * bw)
    grads["w_gate_r"] = dwg_sum[..., :bw]
    grads["w_gate_i"] = dwg_sum[..., bw:]

    for nm in big_names:
        grads[nm] = rs["out"][nm].reshape(weights[nm].shape)

    deltas, new_m, new_v = {}, {}, {}
    for nm in names:
        res = _adamw(weights[nm], grads[nm], mom1[nm], mom2[nm], f"adamw_{nm}", copy_grad=nm in big_names)
        deltas[nm], new_m[nm], new_v[nm] = res[:3]
        if nm in big_names:
            grads[nm] = res[3]
    return (loss, grad_x, *[grads[nm] for nm in names], *[deltas[nm] for nm in names],
            *[new_m[nm] for nm in names], *[new_v[nm] for nm in names])
```

```python
import jax
import jax.numpy as jnp
from jax import lax
from jax.experimental import pallas as pl
from jax.experimental.pallas import tpu as pltpu

F32 = jnp.float32
BF16 = jnp.bfloat16
MESH = pl.DeviceIdType.MESH

EPS = 1e-6
LRU_C = 8.0
N_CHIP = 4
N_DEV = 8
ADAM_LR = 0.001
ADAM_B1 = 0.9
ADAM_B2 = 0.999
ADAM_EPS = 1e-08
ADAM_WD = 0.01
ADAM_STEP = 10

VMEM_LIMIT_BYTES = 56 * 1024 * 1024
SUBLANES = 8
LANES = 128
TOKENS_MATMUL_TILE = 512
TOKENS_MIXER_TILE = 256
GELU_K0 = 0.7978845608028654
GELU_K1 = 0.044715

V_G_PRE_MIX, V_G_POST_MIX, V_CONV_A_B, V_CONV_B_B, V_B_GATE_R, V_B_GATE_I, V_LAMBDA, V_G_PRE_MLP, V_G_POST_MLP = range(9)
V_CONV_A_W = 9
V_CONV_B_W = 12
M_SH_M, M_SC_M, M_GT_M, M_SH_F, M_SC_F, M_GT_F = range(6)


def _cparams(n_grid=0):
    sem = ("arbitrary",) * n_grid if n_grid else None
    return pltpu.CompilerParams(dimension_semantics=sem, vmem_limit_bytes=VMEM_LIMIT_BYTES)


def _full(shape):
    return pl.BlockSpec(shape, lambda *_: (0,) * len(shape))


def _dot(a, b):
    return jnp.dot(a, b, preferred_element_type=F32)


def _dot_tb(a, b):
    return lax.dot_general(a, b, (((1,), (1,)), ((), ())), preferred_element_type=F32)


def _dot_ta(a, b):
    return lax.dot_general(a, b, (((0,), (0,)), ((), ())), preferred_element_type=F32)


def _sigmoid(x):
    return 1.0 / (1.0 + jnp.exp(-x))


def _softplus(x):
    return jnp.maximum(x, 0.0) + jnp.log1p(jnp.exp(-jnp.abs(x)))


def _neg_expm1(x):
    series = -x * (1.0 + 0.5 * x * (1.0 + (x / 3.0) * (1.0 + 0.25 * x)))
    return jnp.where(x > -1e-2, series, 1.0 - jnp.exp(x))


def _gelu(x):
    t = jnp.tanh(GELU_K0 * (x + GELU_K1 * x * x * x))
    return 0.5 * x * (1.0 + t), t


def _gelu_grad(x, t):
    return 0.5 * (1.0 + t) + 0.5 * x * (1.0 - t * t) * GELU_K0 * (1.0 + 3.0 * GELU_K1 * x * x)


def _rms(x):
    r = lax.rsqrt(jnp.mean(x * x, axis=-1, keepdims=True) + EPS)
    return x * r, r


def _rms_bwd(dxn, xn, r):
    return r * (dxn - xn * jnp.mean(dxn * xn, axis=-1, keepdims=True))


def _colsum(x):
    return jnp.sum(x, axis=0, keepdims=True)


def _rows(t, w):
    return lax.broadcasted_iota(jnp.int32, (t, w), 0)


def _shift_down(x, k, prev8):
    t, w = x.shape
    rolled = pltpu.roll(x, k, 0)
    head = jnp.where(_rows(SUBLANES, w) < k, pltpu.roll(prev8, k, 0), rolled[:SUBLANES])
    return jnp.concatenate([head, rolled[SUBLANES:]], axis=0)


def _shift_up(x, k, next8):
    t, w = x.shape
    rolled = pltpu.roll(x, t - k, 0)
    tail = jnp.where(_rows(SUBLANES, w) >= SUBLANES - k, pltpu.roll(next8, SUBLANES - k, 0), rolled[t - SUBLANES:])
    return jnp.concatenate([rolled[:t - SUBLANES], tail], axis=0)


SCAN_GROUP = 16


def _scan_steps(a, b, group, reverse):
    t, w = a.shape
    pos = _rows(t, w) & (group - 1)
    s = 1
    while s < group:
        keep = (pos < group - s) if reverse else (pos >= s)
        shift = (t - s) if reverse else s
        b = b + a * jnp.where(keep, pltpu.roll(b, shift, 0), 0.0)
        a = a * jnp.where(keep, pltpu.roll(a, shift, 0), 1.0)
        s *= 2
    return b, a


def _scan_two_level(a, b, carry_row, a_buf, b_buf, c_buf, reverse):
    t, w = a.shape
    grp = SCAN_GROUP
    n_grp = t // grp
    h_loc, a_cum = _scan_steps(a, b, grp, reverse)
    end = 0 if reverse else grp - 1
    a_end, h_end = [], []
    for j in range(w // LANES):
        a_buf[j] = a_cum[:, j * LANES:(j + 1) * LANES]
        b_buf[j] = h_loc[:, j * LANES:(j + 1) * LANES]
        a_end.append(a_buf[j, pl.ds(end, n_grp, stride=grp), :])
        h_end.append(b_buf[j, pl.ds(end, n_grp, stride=grp), :])
    a_end = jnp.concatenate(a_end, axis=1)
    h_end = jnp.concatenate(h_end, axis=1)
    h_grp, a_grp = _scan_steps(a_end, h_end, n_grp, reverse)
    h_grp = h_grp + a_grp * carry_row
    rows = _rows(n_grp, w)
    if reverse:
        entering = jnp.where(rows == n_grp - 1, carry_row, pltpu.roll(h_grp, n_grp - 1, 0))
    else:
        entering = jnp.where(rows == 0, carry_row, pltpu.roll(h_grp, 1, 0))
    c_buf[...] = entering
    out = [h_loc[g * grp:(g + 1) * grp] + a_cum[g * grp:(g + 1) * grp] * c_buf[g:g + 1, :] for g in range(n_grp)]
    return jnp.concatenate(out, axis=0)


def _row_tile(rows, cols, itemsize=4, target_bytes=2 * 1024 * 1024):
    if rows * cols * itemsize <= target_bytes or rows % SUBLANES:
        return rows
    t = max(SUBLANES, (target_bytes // (cols * itemsize)) // SUBLANES * SUBLANES)
    while rows % t:
        t -= SUBLANES
    return t


def _place():
    return lax.axis_index("x"), lax.axis_index("y"), lax.axis_index("c")


def _other_chips(x, y):
    chips = [(1 - x, y), (x, 1 - y), (1 - x, 1 - y)]
    return chips, [2 * cx + cy for cx, cy in chips]


def _all_gather_small(block, name):
    m_per, n = block.shape

    def body(x_ref, out_ref, send_sems, recv_sems, local_sem):
        x, y, c = _place()
        me, sibling = (x, y, c), (x, y, 1 - c)
        chips, _ = _other_chips(x, y)

        def rows(px, py, pc):
            return out_ref.at[pl.ds((4 * px + 2 * py + pc) * m_per, m_per), :]

        def copy(k, blk, to, src=None):
            return pltpu.make_async_remote_copy(
                src_ref=rows(*blk) if src is None else src, dst_ref=rows(*blk),
                send_sem=send_sems.at[k], recv_sem=recv_sems.at[k], device_id=to, device_id_type=MESH)

        mine = pltpu.make_async_copy(x_ref, rows(*me), local_sem)
        mine.start()
        first = [copy(0, me, sibling, src=x_ref)]
        first += [copy(1 + j, me, (*chip, c), src=x_ref) for j, chip in enumerate(chips)]
        for cp in first:
            cp.start()
        passed = [copy(4 + j, (*chip, c), sibling) for j, chip in enumerate(chips)]
        for j, chip in enumerate(chips):
            copy(1 + j, (*chip, c), me).wait_recv()
            passed[j].start()
        copy(0, sibling, me).wait_recv()
        for j, chip in enumerate(chips):
            copy(4 + j, (*chip, 1 - c), me).wait_recv()
        for cp in first + passed:
            cp.wait_send()
        mine.wait()

    return pl.pallas_call(
        body, name=name,
        out_shape=jax.ShapeDtypeStruct((N_DEV * m_per, n), block.dtype),
        in_specs=[pl.BlockSpec(memory_space=pltpu.VMEM)],
        out_specs=pl.BlockSpec(memory_space=pltpu.VMEM),
        scratch_shapes=[pltpu.SemaphoreType.DMA((7,)), pltpu.SemaphoreType.DMA((7,)), pltpu.SemaphoreType.DMA],
        compiler_params=pltpu.CompilerParams(vmem_limit_bytes=VMEM_LIMIT_BYTES),
    )(block)


def _cast_place(w, layer, q_arr, name):
    _, r, cols = w.shape
    tr = _row_tile(r, cols)

    def body(q_ref, w_ref, o_ref):
        o_ref[...] = w_ref[...].astype(BF16)

    return pl.pallas_call(
        body, name=name,
        out_shape=jax.ShapeDtypeStruct((N_CHIP, r, cols), BF16),
        grid_spec=pltpu.PrefetchScalarGridSpec(
            num_scalar_prefetch=1, grid=(r // tr,),
            in_specs=[pl.BlockSpec((1, tr, cols), lambda i, q_ref: (layer, i, 0))],
            out_specs=pl.BlockSpec((1, tr, cols), lambda i, q_ref: (q_ref[0], i, 0))),
        compiler_params=_cparams(1),
    )(q_arr, w)


class _Carry:
    def __init__(self, ins, out_shapes, aliases, sem_shapes, start, finish, mid=None):
        self.ins, self.out_shapes, self.aliases, self.sem_shapes = list(ins), list(out_shapes), dict(aliases), list(sem_shapes)
        self.start, self.mid, self.finish = start, mid, finish


def _pcall(body, *, name, grid, in_specs, out_specs, out_shape, args, scratch_shapes=(), carries=(), mid_frac=0.85):
    in_specs, out_specs, out_shape = list(in_specs), list(out_specs), list(out_shape)
    scratch_shapes, args = list(scratch_shapes), list(args)
    n_in, n_out, n_scr = len(in_specs), len(out_shape), len(scratch_shapes)
    steps = 1
    for g in grid:
        steps *= g
    mid_step = min(steps - 1, int(steps * mid_frac))
    any_spec = pl.BlockSpec(memory_space=pl.ANY)
    aliases = {}
    spans = []
    for cr in carries:
        spans.append((len(args), len(out_shape), len(scratch_shapes)))
        for a, b in cr.aliases.items():
            aliases[len(args) + a] = len(out_shape) + b
        args += cr.ins
        in_specs += [any_spec] * len(cr.ins)
        out_shape += cr.out_shapes
        out_specs += [any_spec] * len(cr.out_shapes)
        scratch_shapes += cr.sem_shapes
    n_all_in = len(args)
    n_all_out = len(out_shape)

    def wrapped(*refs):
        ins, outs, scr = refs[:n_all_in], refs[n_all_in:n_all_in + n_all_out], refs[n_all_in + n_all_out:]
        parts = [(cr, ins[a:a + len(cr.ins)], outs[b:b + len(cr.out_shapes)], scr[s:s + len(cr.sem_shapes)])
                 for cr, (a, b, s) in zip(carries, spans)]
        lin = 0
        for ax, g in enumerate(grid):
            lin = lin * g + pl.program_id(ax)

        def at(step, fn):
            if steps == 1:
                fn()
            else:
                pl.when(lin == step)(fn)

        def start_all():
            for cr, ci, co, cs in parts:
                cr.start(ci, co, cs)

        def mid_all():
            for cr, ci, co, cs in parts:
                if cr.mid is not None:
                    cr.mid(ci, co, cs)

        def finish_all():
            for cr, ci, co, cs in parts:
                cr.finish(ci, co, cs)

        if parts:
            at(0, start_all)
        body(*ins[:n_in], *outs[:n_out], *scr[:n_scr])
        if parts:
            at(mid_step, mid_all)
            at(steps - 1, finish_all)

    res = pl.pallas_call(
        wrapped, name=name, grid=tuple(grid), out_shape=out_shape, in_specs=in_specs, out_specs=out_specs,
        scratch_shapes=scratch_shapes, input_output_aliases=aliases, compiler_params=_cparams(len(grid)),
    )(*args)
    res = list(res)
    return res[:n_out], [res[b:b + len(cr.out_shapes)] for cr, (_, b, _) in zip(carries, spans)]


def _run_carries(carries, name):
    return _pcall(lambda: None, name=name, grid=(), in_specs=[], out_specs=[], out_shape=[], args=[], carries=carries)[1]


def _gather_carry(bufs):
    n = len(bufs)

    def copies(o_refs, sems):
        send_sems, recv_sems = sems
        x, y, c = _place()
        q = 2 * x + y
        sibling = (x, y, 1 - c)
        chips, qs = _other_chips(x, y)

        def half(w, shard, pc):
            rh = bufs[w].shape[1] // 2
            return o_refs[w].at[shard, pl.ds(pc * rh, rh), :]

        def over_ici(w, j, shard):
            blk = half(w, shard, c)
            return pltpu.make_async_remote_copy(
                src_ref=blk, dst_ref=blk, send_sem=send_sems.at[w, j], recv_sem=recv_sems.at[w, j],
                device_id=(*chips[j], c), device_id_type=MESH)

        def to_sibling(w, j, pc):
            blk = half(w, qs[j], pc)
            return pltpu.make_async_remote_copy(
                src_ref=blk, dst_ref=blk, send_sem=send_sems.at[w, 3 + j], recv_sem=recv_sems.at[w, 3 + j],
                device_id=sibling, device_id_type=MESH)

        return q, c, qs, over_ici, to_sibling

    pairs = [(w, j) for w in range(n) for j in range(3)]

    def start(i_refs, o_refs, sems):
        q, _, _, over_ici, _ = copies(o_refs, sems)
        for w, j in pairs:
            over_ici(w, j, q).start()

    def mid(i_refs, o_refs, sems):
        _, c, qs, over_ici, to_sibling = copies(o_refs, sems)
        for w, j in pairs:
            over_ici(w, j, qs[j]).wait_recv()
            to_sibling(w, j, c).start()

    def finish(i_refs, o_refs, sems):
        q, c, _, over_ici, to_sibling = copies(o_refs, sems)
        for w, j in pairs:
            to_sibling(w, j, 1 - c).wait_recv()
        for w, j in pairs:
            over_ici(w, j, q).wait_send()
            to_sibling(w, j, c).wait_send()

    return _Carry(bufs, [jax.ShapeDtypeStruct(b.shape, b.dtype) for b in bufs], {w: w for w in range(n)},
                  [pltpu.SemaphoreType.DMA((n, 6)), pltpu.SemaphoreType.DMA((n, 6))], start, finish, mid)


def _exchange_carry(grads):
    n = len(grads)

    def copies(g_refs, l_refs, sems):
        send_sems, recv_sems = sems
        x, y, c = _place()
        out = []
        for w in range(n):
            rh = grads[w].shape[1] // 2
            out.append(pltpu.make_async_remote_copy(
                src_ref=g_refs[w].at[:, pl.ds((1 - c) * rh, rh), :], dst_ref=l_refs[w],
                send_sem=send_sems.at[w], recv_sem=recv_sems.at[w], device_id=(x, y, 1 - c), device_id_type=MESH))
        return out

    def start(g_refs, l_refs, sems):
        for cp in copies(g_refs, l_refs, sems):
            cp.start()

    def finish(g_refs, l_refs, sems):
        for cp in copies(g_refs, l_refs, sems):
            cp.wait()

    return _Carry(grads, [jax.ShapeDtypeStruct((N_CHIP, g.shape[1] // 2, g.shape[2]), g.dtype) for g in grads], {},
                  [pltpu.SemaphoreType.DMA((n,)), pltpu.SemaphoreType.DMA((n,))], start, finish)


def _scatter_carry(sums):
    n = len(sums)

    def copies(s_refs, l_refs, sems):
        send_sems, recv_sems = sems
        x, y, c = _place()
        chips, _ = _other_chips(x, y)
        return [pltpu.make_async_remote_copy(
            src_ref=s_refs[w].at[j], dst_ref=l_refs[w].at[j], send_sem=send_sems.at[w, j], recv_sem=recv_sems.at[w, j],
            device_id=(*chips[j], c), device_id_type=MESH) for w in range(n) for j in range(3)]

    def start(s_refs, l_refs, sems):
        for cp in copies(s_refs, l_refs, sems):
            cp.start()

    def finish(s_refs, l_refs, sems):
        for cp in copies(s_refs, l_refs, sems):
            cp.wait()

    return _Carry(sums, [jax.ShapeDtypeStruct(s.shape, s.dtype) for s in sums], {},
                  [pltpu.SemaphoreType.DMA((n, 3)), pltpu.SemaphoreType.DMA((n, 3))], start, finish)


def _join_carry(outs, layers):
    n = len(outs)

    def copy(o_refs, sems, w, mine):
        send_sems, recv_sems = sems
        x, y, c = _place()
        r = outs[w].shape[1]
        rows = o_refs[w].at[layers[w], pl.ds((c if mine else 1 - c) * (r // 2), r // 2), :]
        return pltpu.make_async_remote_copy(
            src_ref=rows, dst_ref=rows, send_sem=send_sems.at[w], recv_sem=recv_sems.at[w],
            device_id=(x, y, 1 - c), device_id_type=MESH)

    def start(i_refs, o_refs, sems):
        for w in range(n):
            copy(o_refs, sems, w, True).start()

    def finish(i_refs, o_refs, sems):
        for w in range(n):
            copy(o_refs, sems, w, True).wait_send()
        for w in range(n):
            copy(o_refs, sems, w, False).wait_recv()

    return _Carry(outs, [jax.ShapeDtypeStruct(o.shape, o.dtype) for o in outs], {w: w for w in range(n)},
                  [pltpu.SemaphoreType.DMA((n,)), pltpu.SemaphoreType.DMA((n,))], start, finish)


PF_C, PF_Q, PF_QS = 0, 1, 2


def _add_sibling_half(g, landed, pf, name):
    _, r, cols = g.shape
    rh = r // 2
    tr = _row_tile(rh, cols)
    nr = rh // tr

    def body(pf_ref, g_ref, l_ref, o_ref):
        o_ref[...] = (g_ref[...] + l_ref[...]).astype(BF16)

    return pl.pallas_call(
        body, name=name,
        out_shape=jax.ShapeDtypeStruct((3, rh, cols), BF16),
        grid_spec=pltpu.PrefetchScalarGridSpec(
            num_scalar_prefetch=1, grid=(3, nr),
            in_specs=[pl.BlockSpec((1, tr, cols), lambda j, i, pf_ref: (pf_ref[PF_QS + j], pf_ref[PF_C] * nr + i, 0)),
                      pl.BlockSpec((1, tr, cols), lambda j, i, pf_ref: (pf_ref[PF_QS + j], i, 0))],
            out_specs=pl.BlockSpec((1, tr, cols), lambda j, i, pf_ref: (j, i, 0))),
        compiler_params=_cparams(2),
    )(pf, g, landed)


def _add_chips(g, landed, from_chips, pf, prev, layer, n_layer, name):
    _, r, cols = g.shape
    rh = r // 2
    tr = _row_tile(rh, cols)
    nr = rh // tr

    def body(pf_ref, g_ref, l_ref, f_ref, *rest):
        o_ref = rest[-1]
        acc = g_ref[0] + l_ref[0]
        for j in range(3):
            acc = acc + f_ref[j].astype(F32)
        o_ref[0] = acc

    in_specs = [pl.BlockSpec((1, tr, cols), lambda i, pf_ref: (pf_ref[PF_Q], pf_ref[PF_C] * nr + i, 0)),
                pl.BlockSpec((1, tr, cols), lambda i, pf_ref: (pf_ref[PF_Q], i, 0)),
                pl.BlockSpec((3, tr, cols), lambda i, pf_ref: (0, i, 0))]
    args = [pf, g, landed, from_chips]
    aliases = {}
    if prev is not None:
        in_specs.append(pl.BlockSpec(memory_space=pl.ANY))
        args.append(prev)
        aliases = {4: 0}
    return pl.pallas_call(
        body, name=name,
        out_shape=jax.ShapeDtypeStruct((n_layer, r, cols), F32),
        grid_spec=pltpu.PrefetchScalarGridSpec(
            num_scalar_prefetch=1, grid=(nr,), in_specs=in_specs,
            out_specs=pl.BlockSpec((1, tr, cols), lambda i, pf_ref: (layer, pf_ref[PF_C] * nr + i, 0))),
        input_output_aliases=aliases,
        compiler_params=_cparams(1),
    )(*args)


def _allgather_carry(blocks):
    n = len(blocks)

    def copies(b_refs, o_refs, sems):
        send_sems, recv_sems = sems
        x, y, c = _place()
        chips, _ = _other_chips(x, y)

        def place(w, px, py, pc):
            m = blocks[w].shape[0]
            return o_refs[w].at[pl.ds((4 * px + 2 * py + pc) * m, m), :]

        def own_to(w, k, to):
            dst = place(w, x, y, c)
            return pltpu.make_async_remote_copy(src_ref=b_refs[w], dst_ref=dst, send_sem=send_sems.at[w, k],
                                                recv_sem=recv_sems.at[w, k], device_id=to, device_id_type=MESH)

        def landed_from(w, k, px, py, pc):
            blk = place(w, px, py, pc)
            return pltpu.make_async_remote_copy(src_ref=blk, dst_ref=blk, send_sem=send_sems.at[w, k],
                                                recv_sem=recv_sems.at[w, k], device_id=(x, y, 1 - c), device_id_type=MESH)

        return x, y, c, chips, own_to, landed_from

    def start(b_refs, o_refs, sems):
        x, y, c, chips, own_to, _ = copies(b_refs, o_refs, sems)
        for w in range(n):
            own_to(w, 0, (x, y, 1 - c)).start()
            for j, chip in enumerate(chips):
                own_to(w, 1 + j, (*chip, c)).start()

    def mid(b_refs, o_refs, sems):
        x, y, c, chips, _, landed_from = copies(b_refs, o_refs, sems)
        for w in range(n):
            for j, chip in enumerate(chips):
                landed_from(w, 1 + j, *chip, c).wait_recv()
                landed_from(w, 4 + j, *chip, c).start()

    def finish(b_refs, o_refs, sems):
        x, y, c, chips, own_to, landed_from = copies(b_refs, o_refs, sems)
        for w in range(n):
            landed_from(w, 0, x, y, 1 - c).wait_recv()
            for j, chip in enumerate(chips):
                landed_from(w, 4 + j, *chip, 1 - c).wait_recv()
            own_to(w, 0, (x, y, 1 - c)).wait_send()
            for j, chip in enumerate(chips):
                own_to(w, 1 + j, (*chip, c)).wait_send()
                landed_from(w, 4 + j, *chip, c).wait_send()

    return _Carry(blocks, [jax.ShapeDtypeStruct((N_DEV * b.shape[0], b.shape[1]), b.dtype) for b in blocks], {},
                  [pltpu.SemaphoreType.DMA((n, 7)), pltpu.SemaphoreType.DMA((n, 7))], start, finish, mid)


def _sum_devices(gathered, own, me_arr, name):
    m, n = own.shape
    tr = _row_tile(m, n, itemsize=own.dtype.itemsize, target_bytes=256 * 1024)
    nr = m // tr

    def body(me_ref, *refs):
        g_refs, own_ref, o_ref = refs[:N_DEV], refs[N_DEV], refs[N_DEV + 1]
        me = me_ref[0]
        acc = None
        for dev in range(N_DEV):
            term = jnp.where(me == dev, own_ref[...], g_refs[dev][...]).astype(F32)
            acc = term if acc is None else acc + term
        o_ref[...] = acc

    def dev_rows(dev):
        return pl.BlockSpec((tr, n), lambda i, me_ref: (dev * nr + i, 0))

    return pl.pallas_call(
        body, name=name,
        out_shape=jax.ShapeDtypeStruct((m, n), F32),
        grid_spec=pltpu.PrefetchScalarGridSpec(
            num_scalar_prefetch=1, grid=(nr,),
            in_specs=[dev_rows(dev) for dev in range(N_DEV)] + [pl.BlockSpec((tr, n), lambda i, me_ref: (i, 0))],
            out_specs=pl.BlockSpec((tr, n), lambda i, me_ref: (i, 0))),
        compiler_params=_cparams(1),
    )(me_arr, *([gathered] * N_DEV), own)


def _mod_forward(c_all, w_mod, b_mod_shard, name):
    n_layer, d, mq = w_mod.shape

    def body(c_ref, w_ref, b_ref, o_ref):
        cv = c_ref[...]
        o_ref[...] = _dot(cv * _sigmoid(cv), w_ref[0]) + b_ref[0]

    return pl.pallas_call(
        body, name=name, grid=(n_layer,),
        out_shape=jax.ShapeDtypeStruct((n_layer * N_DEV, mq), F32),
        in_specs=[_full((N_DEV, d)), pl.BlockSpec((1, d, mq), lambda l: (l, 0, 0)),
                  pl.BlockSpec((1, 1, mq), lambda l: (l, 0, 0))],
        out_specs=pl.BlockSpec((N_DEV, mq), lambda l: (l, 0)),
        compiler_params=_cparams(1),
    )(c_all, w_mod, b_mod_shard.reshape(n_layer, 1, mq))


def _mod_backward(c_all_t, dmod_shard, name):
    n_layer, _, mq = dmod_shard.shape
    d = c_all_t.shape[0]

    def body(c_ref, dm_ref, o_ref):
        cv = c_ref[...]
        o_ref[0] = _dot(cv * _sigmoid(cv), dm_ref[0])

    return pl.pallas_call(
        body, name=name, grid=(n_layer,),
        out_shape=jax.ShapeDtypeStruct((n_layer, d, mq), F32),
        in_specs=[_full((d, N_DEV)), pl.BlockSpec((1, N_DEV, mq), lambda l: (l, 0, 0))],
        out_specs=pl.BlockSpec((1, d, mq), lambda l: (l, 0, 0)),
        compiler_params=_cparams(1),
    )(c_all_t, dmod_shard)


def _norm_proj(x, mod, vec, w_in, name, carries=()):
    s, d = x.shape
    nq = w_in.shape[2]
    ts = min(TOKENS_MATMUL_TILE, s)

    def body(x_ref, mod_ref, vec_ref, w_ref, h_ref, p_ref):
        xn, _ = _rms(x_ref[...])
        gm = vec_ref[V_G_PRE_MIX:V_G_PRE_MIX + 1, :] * (1.0 + mod_ref[M_SC_M:M_SC_M + 1, :])
        h = (xn * gm + mod_ref[M_SH_M:M_SH_M + 1, :]).astype(BF16)
        h_ref[...] = h
        for qb in range(N_CHIP):
            p_ref[:, qb * nq:(qb + 1) * nq] = _dot(h, w_ref[qb]).astype(BF16)

    return _pcall(
        body, name=name, grid=(s // ts,),
        out_shape=[jax.ShapeDtypeStruct((s, d), BF16), jax.ShapeDtypeStruct((s, N_CHIP * nq), BF16)],
        in_specs=[pl.BlockSpec((ts, d), lambda i: (i, 0)), _full(mod.shape), _full(vec.shape), _full(w_in.shape)],
        out_specs=[pl.BlockSpec((ts, d), lambda i: (i, 0)), pl.BlockSpec((ts, N_CHIP * nq), lambda i: (i, 0))],
        args=[x, mod, vec, w_in], carries=carries)


def _gate_pre(xb2_b, wg_ref, n_head, bw):
    zr, zi = [], []
    for hd in range(n_head):
        z = _dot(xb2_b[:, hd * bw:(hd + 1) * bw], wg_ref[hd])
        zr.append(z[:, :bw])
        zi.append(z[:, bw:])
    return jnp.concatenate(zr, axis=1), jnp.concatenate(zi, axis=1)


def _lru_coeffs(xb2, wg_ref, vec_ref, n_head, bw):
    zr, zi = _gate_pre(xb2.astype(BF16), wg_ref, n_head, bw)
    r = _sigmoid(zr + vec_ref[V_B_GATE_R:V_B_GATE_R + 1, :])
    gi = _sigmoid(zi + vec_ref[V_B_GATE_I:V_B_GATE_I + 1, :])
    sp = _softplus(-vec_ref[V_LAMBDA:V_LAMBDA + 1, :])
    log_a = (-LRU_C) * r * sp
    a = jnp.exp(log_a)
    mult = jnp.sqrt(_neg_expm1(2.0 * log_a))
    return r, gi, sp, a, mult


def _mixer_forward(x, proj, mod, vec, wg, w_a_out, w_b_out, w_o, name, carries=()):
    s, d = x.shape
    n_head, bw, _ = wg.shape
    ts = min(TOKENS_MIXER_TILE, s)

    def body(x_ref, p_ref, mod_ref, vec_ref, wg_ref, wa_ref, wb_ref, wo_ref,
             x1_ref, conva_ref, xb2_ref, hh_ref, ya_ref, yb_ref, pa_ref, pb_ref, m_ref, y_ref,
             r_ref, gi_ref, a_ref, mult_ref,
             cv_tail, xb_tail, h_last, a_buf, b_buf, c_buf):
        i = pl.program_id(0)

        @pl.when(i == 0)
        def _():
            cv_tail[...] = jnp.zeros_like(cv_tail)
            xb_tail[...] = jnp.zeros_like(xb_tail)
            h_last[...] = jnp.zeros_like(h_last)

        def seg(k):
            return p_ref[:, k * d:(k + 1) * d].astype(F32)

        def vrow(k):
            return vec_ref[k:k + 1, :]

        b_a, c_a, v_a, x_b, g_b, u_a, u_b = (seg(k) for k in range(7))
        cv = c_a * v_a
        prev_cv = cv_tail[...]
        conv_a = (vrow(V_CONV_A_B) + vrow(V_CONV_A_W) * _shift_down(cv, 2, prev_cv)
                  + vrow(V_CONV_A_W + 1) * _shift_down(cv, 1, prev_cv) + vrow(V_CONV_A_W + 2) * cv)
        cv_tail[...] = cv[ts - SUBLANES:]
        y_a = b_a * conv_a
        prev_xb = xb_tail[...]
        xb2 = (vrow(V_CONV_B_B) + vrow(V_CONV_B_W) * _shift_down(x_b, 3, prev_xb)
               + vrow(V_CONV_B_W + 1) * _shift_down(x_b, 2, prev_xb)
               + vrow(V_CONV_B_W + 2) * _shift_down(x_b, 1, prev_xb) + vrow(V_CONV_B_W + 3) * x_b)
        xb_tail[...] = x_b[ts - SUBLANES:]
        r, gi, _, a, mult = _lru_coeffs(xb2, wg_ref, vec_ref, n_head, bw)
        r_ref[...] = r
        gi_ref[...] = gi
        a_ref[...] = a
        mult_ref[...] = mult
        hh = _scan_two_level(a, mult * gi * xb2, h_last[SUBLANES - 1:SUBLANES, :], a_buf, b_buf, c_buf, reverse=False)
        h_last[...] = hh[ts - SUBLANES:]
        gel, _ = _gelu(g_b)
        y_b = hh * gel
        ya_b, yb_b = y_a.astype(BF16), y_b.astype(BF16)
        pa = _dot(ya_b, wa_ref[...])
        pb = _dot(yb_b, wb_ref[...])
        m = (_sigmoid(u_a) * pa + _sigmoid(u_b) * pb).astype(BF16)
        y = _dot(m, wo_ref[...])
        yn, _ = _rms(y)
        gg = mod_ref[M_GT_M:M_GT_M + 1, :] * vrow(V_G_POST_MIX)
        x1_ref[...] = x_ref[...] + yn * gg
        conva_ref[...] = conv_a.astype(BF16)
        xb2_ref[...] = xb2
        hh_ref[...] = hh
        ya_ref[...] = ya_b
        yb_ref[...] = yb_b
        pa_ref[...] = pa.astype(BF16)
        pb_ref[...] = pb.astype(BF16)
        m_ref[...] = m
        y_ref[...] = y.astype(BF16)

    tile = pl.BlockSpec((ts, d), lambda i: (i, 0))
    sd = lambda dt: jax.ShapeDtypeStruct((s, d), dt)
    return _pcall(
        body, name=name, grid=(s // ts,),
        out_shape=[sd(F32), sd(BF16), sd(F32), sd(F32), sd(BF16), sd(BF16), sd(BF16), sd(BF16), sd(BF16), sd(BF16),
                   sd(F32), sd(F32), sd(F32), sd(F32)],
        in_specs=[tile, pl.BlockSpec((ts, 7 * d), lambda i: (i, 0)), _full(mod.shape), _full(vec.shape),
                  _full(wg.shape), _full(w_a_out.shape), _full(w_b_out.shape), _full(w_o.shape)],
        out_specs=[tile] * 14,
        scratch_shapes=[pltpu.VMEM((SUBLANES, d), F32)] * 3 + [pltpu.VMEM((d // LANES, ts, LANES), F32)] * 2
                       + [pltpu.VMEM((ts // SCAN_GROUP, d), F32)],
        args=[x, proj, mod, vec, wg, w_a_out, w_b_out, w_o], carries=carries)


def _mlp_forward(x1, mod, vec, w_up, w_down, name, carries=()):
    s, d = x1.shape
    fq = w_up.shape[2]
    ts = min(TOKENS_MATMUL_TILE, s)

    def body(x_ref, mod_ref, vec_ref, wu_ref, wd_ref, x2_ref, h2_ref, up_ref, y2_ref):
        x = x_ref[...]
        xn, _ = _rms(x)
        gm = vec_ref[V_G_PRE_MLP:V_G_PRE_MLP + 1, :] * (1.0 + mod_ref[M_SC_F:M_SC_F + 1, :])
        h2 = (xn * gm + mod_ref[M_SH_F:M_SH_F + 1, :]).astype(BF16)
        h2_ref[...] = h2
        y2 = jnp.zeros((ts, d), F32)
        for qb in range(N_CHIP):
            up = _dot(h2, wu_ref[qb])
            up_ref[:, qb * fq:(qb + 1) * fq] = up.astype(BF16)
            ru = jnp.maximum(up, 0.0)
            y2 = y2 + _dot((ru * ru).astype(BF16), wd_ref[qb])
        y2_ref[...] = y2.astype(BF16)
        yn, _ = _rms(y2)
        gg = mod_ref[M_GT_F:M_GT_F + 1, :] * vec_ref[V_G_POST_MLP:V_G_POST_MLP + 1, :]
        x2_ref[...] = x + yn * gg

    tile = pl.BlockSpec((ts, d), lambda i: (i, 0))
    return _pcall(
        body, name=name, grid=(s // ts,),
        out_shape=[jax.ShapeDtypeStruct((s, d), F32), jax.ShapeDtypeStruct((s, d), BF16),
                   jax.ShapeDtypeStruct((s, N_CHIP * fq), BF16), jax.ShapeDtypeStruct((s, d), BF16)],
        in_specs=[tile, _full(mod.shape), _full(vec.shape), _full(w_up.shape), _full(w_down.shape)],
        out_specs=[tile, tile, pl.BlockSpec((ts, N_CHIP * fq), lambda i: (i, 0)), tile],
        args=[x1, mod, vec, w_up, w_down], carries=carries)


def _loss_head(xf, target, name):
    s, d = xf.shape
    ts = min(TOKENS_MATMUL_TILE, s)

    def body(x_ref, t_ref, dx_ref, loss_ref):
        @pl.when(pl.program_id(0) == 0)
        def _():
            loss_ref[...] = jnp.zeros_like(loss_ref)

        err = x_ref[...] - t_ref[...]
        dx_ref[...] = err * (1.0 / d)
        part = jnp.sum(jnp.sum(err * err, axis=1, keepdims=True), axis=0, keepdims=True) * (0.5 / d)
        loss_ref[...] = loss_ref[...] + part

    tile = pl.BlockSpec((ts, d), lambda i: (i, 0))
    return pl.pallas_call(
        body, name=name, grid=(s // ts,),
        out_shape=[jax.ShapeDtypeStruct((s, d), F32), jax.ShapeDtypeStruct((SUBLANES, 128), F32)],
        in_specs=[tile, tile], out_specs=[tile, _full((SUBLANES, 128))],
        compiler_params=_cparams(1),
    )(xf, target)


SB3_DSH, SB3_DSC, SB3_DGT, SB3_DG_PRE, SB3_DG_POST = range(5)
SB1_DSH, SB1_DSC, SB1_DG_PRE = range(3)
(SB2_DGT, SB2_DG_POST, SB2_DWA, SB2_DBA, SB2_DWB, SB2_DBB, SB2_DLAM, SB2_DBR, SB2_DBI) = (0, 1, 2, 5, 6, 10, 11, 12, 13)


def _mlp_backward(dx2, x1, y2, up, mod, vec, w_up, w_down, name, carries=()):
    s, d = dx2.shape
    fq = w_up.shape[2]
    ts = min(TOKENS_MIXER_TILE, s)
    n_t = s // ts

    def body(dx2_ref, x_ref, y2_ref, up_ref, mod_ref, vec_ref, wu_ref, wd_ref,
             dx1_ref, dy2_ref, dup_ref, act_ref, small_ref):
        i = pl.program_id(0)

        @pl.when(i == 0)
        def _():
            small_ref[...] = jnp.zeros_like(small_ref)

        dout = dx2_ref[...]
        y2n, ry = _rms(y2_ref[...].astype(F32))
        g_post = vec_ref[V_G_POST_MLP:V_G_POST_MLP + 1, :]
        gt = mod_ref[M_GT_F:M_GT_F + 1, :]
        dgg = _colsum(dout * y2n)
        dy2 = _rms_bwd(dout * (gt * g_post), y2n, ry).astype(BF16)
        dy2_ref[...] = dy2
        dh2 = jnp.zeros((ts, d), F32)
        for qb in range(N_CHIP):
            cols = slice(qb * fq, (qb + 1) * fq)
            dact = _dot_tb(dy2, wd_ref[qb])
            ru = jnp.maximum(up_ref[:, cols].astype(F32), 0.0)
            dup = (dact * (2.0 * ru)).astype(BF16)
            dup_ref[:, cols] = dup
            act_ref[:, cols] = (ru * ru).astype(BF16)
            dh2 = dh2 + _dot_tb(dup, wu_ref[qb])
        xn, r = _rms(x_ref[...])
        g_pre = vec_ref[V_G_PRE_MLP:V_G_PRE_MLP + 1, :]
        sc1 = 1.0 + mod_ref[M_SC_F:M_SC_F + 1, :]
        dsh = _colsum(dh2)
        dgm = _colsum(dh2 * xn)
        dx1_ref[...] = dout + _rms_bwd(dh2 * (g_pre * sc1), xn, r)
        small_ref[SB3_DSH:SB3_DSH + 1, :] += dsh
        small_ref[SB3_DSC:SB3_DSC + 1, :] += dgm
        small_ref[SB3_DGT:SB3_DGT + 1, :] += dgg

        @pl.when(i == n_t - 1)
        def _():
            dgm_t = small_ref[SB3_DSC:SB3_DSC + 1, :]
            dgg_t = small_ref[SB3_DGT:SB3_DGT + 1, :]
            small_ref[SB3_DSC:SB3_DSC + 1, :] = dgm_t * g_pre
            small_ref[SB3_DG_PRE:SB3_DG_PRE + 1, :] = dgm_t * sc1
            small_ref[SB3_DGT:SB3_DGT + 1, :] = dgg_t * g_post
            small_ref[SB3_DG_POST:SB3_DG_POST + 1, :] = dgg_t * gt

    tile = pl.BlockSpec((ts, d), lambda i: (i, 0))
    wide = pl.BlockSpec((ts, N_CHIP * fq), lambda i: (i, 0))
    return _pcall(
        body, name=name, grid=(n_t,),
        out_shape=[jax.ShapeDtypeStruct((s, d), F32), jax.ShapeDtypeStruct((s, d), BF16),
                   jax.ShapeDtypeStruct((s, N_CHIP * fq), BF16), jax.ShapeDtypeStruct((s, N_CHIP * fq), BF16),
                   jax.ShapeDtypeStruct((SUBLANES, d), F32)],
        in_specs=[tile, tile, tile, wide, _full(mod.shape), _full(vec.shape), _full(w_up.shape), _full(w_down.shape)],
        out_specs=[tile, tile, wide, wide, _full((SUBLANES, d))],
        args=[dx2, x1, y2, up, mod, vec, w_up, w_down], carries=carries)


def _mixer_backward(dx1, proj, conva, xb2s, hhs, pas, pbs, ys, rs_, gis, as_, mults, mod, vec, wg, w_a_out, w_b_out, w_o,
                    name, carries=()):
    s, d = dx1.shape
    n_head, bw, _ = wg.shape
    ts = min(TOKENS_MIXER_TILE, s)
    n_t = s // ts

    def body(dx1_ref, p_ref, conva_ref, xb2_ref, hh_ref, pa_ref, pb_ref, y_ref, r_ref, gi_ref, a_ref, mult_ref,
             mod_ref, vec_ref, wg_ref, wa_ref, wb_ref, wo_ref,
             dp_ref, dy_ref, dpa_ref, dpb_ref, small_ref, dwg_ref,
             dconv_head, dxb2_head, a_head, g_head, a_buf, b_buf, c_buf):
        i = pl.program_id(0)

        @pl.when(i == 0)
        def _():
            small_ref[...] = jnp.zeros_like(small_ref)
            dwg_ref[...] = jnp.zeros_like(dwg_ref)
            dconv_head[...] = jnp.zeros_like(dconv_head)
            dxb2_head[...] = jnp.zeros_like(dxb2_head)
            a_head[...] = jnp.zeros_like(a_head)
            g_head[...] = jnp.zeros_like(g_head)

        def seg(k):
            return p_ref[:, k * d:(k + 1) * d].astype(F32)

        def vrow(k):
            return vec_ref[k:k + 1, :]

        def acc(row, val):
            small_ref[row:row + 1, :] += val

        dout = dx1_ref[...]
        yn, ry = _rms(y_ref[...].astype(F32))
        g_post = vrow(V_G_POST_MIX)
        gt = mod_ref[M_GT_M:M_GT_M + 1, :]
        acc(SB2_DGT, _colsum(dout * yn))
        dy = _rms_bwd(dout * (gt * g_post), yn, ry).astype(BF16)
        dy_ref[...] = dy
        dm = _dot_tb(dy, wo_ref[...])
        u_a, u_b = seg(5), seg(6)
        sa, sb = _sigmoid(u_a), _sigmoid(u_b)
        dpa = (dm * sa).astype(BF16)
        dpb = (dm * sb).astype(BF16)
        dpa_ref[...] = dpa
        dpb_ref[...] = dpb
        du_a = dm * pa_ref[...].astype(F32) * (sa * (1.0 - sa))
        du_b = dm * pb_ref[...].astype(F32) * (sb * (1.0 - sb))
        dp_ref[:, 5 * d:6 * d] = du_a.astype(BF16)
        dp_ref[:, 6 * d:7 * d] = du_b.astype(BF16)
        dy_a = _dot_tb(dpa, wa_ref[...])
        dy_b = _dot_tb(dpb, wb_ref[...])

        b_a, c_a, v_a = seg(0), seg(1), seg(2)
        dp_ref[:, 0:d] = (dy_a * conva_ref[...].astype(F32)).astype(BF16)
        dconv = dy_a * b_a
        nxt = dconv_head[...]
        d1 = _shift_up(dconv, 1, nxt)
        d2 = _shift_up(dconv, 2, nxt)
        dconv_head[...] = dconv[:SUBLANES]
        dcv = vrow(V_CONV_A_W + 2) * dconv + vrow(V_CONV_A_W + 1) * d1 + vrow(V_CONV_A_W) * d2
        cv = c_a * v_a
        acc(SB2_DWA + 2, _colsum(cv * dconv))
        acc(SB2_DWA + 1, _colsum(cv * d1))
        acc(SB2_DWA, _colsum(cv * d2))
        acc(SB2_DBA, _colsum(dconv))
        dp_ref[:, d:2 * d] = (dcv * v_a).astype(BF16)
        dp_ref[:, 2 * d:3 * d] = (dcv * c_a).astype(BF16)

        x_b, g_b = seg(3), seg(4)
        hh = hh_ref[...]
        gel, th = _gelu(g_b)
        dp_ref[:, 4 * d:5 * d] = (dy_b * hh * _gelu_grad(g_b, th)).astype(BF16)
        dhh = dy_b * gel
        xb2 = xb2_ref[...]
        r, gi, a, mult = r_ref[...], gi_ref[...], a_ref[...], mult_ref[...]
        sp = _softplus(-vrow(V_LAMBDA))
        a_next = _shift_up(a, 1, a_head[...])
        g = _scan_two_level(a_next, dhh, g_head[0:1, :], a_buf, b_buf, c_buf, reverse=True)
        a_head[...] = a[:SUBLANES]
        g_head[...] = g[:SUBLANES]
        bb = mult * gi * xb2
        dlog_a = g * (hh - bb) - g * gi * xb2 * (a * a / mult)
        dgi = g * mult * xb2
        dxb2 = g * mult * gi
        acc(SB2_DLAM, _colsum(dlog_a * r))
        dzr = dlog_a * ((-LRU_C) * sp) * (r * (1.0 - r))
        dzi = dgi * (gi * (1.0 - gi))
        acc(SB2_DBR, _colsum(dzr))
        acc(SB2_DBI, _colsum(dzi))
        xb2_b = xb2.astype(BF16)
        back = []
        for hd in range(n_head):
            cols = slice(hd * bw, (hd + 1) * bw)
            dz = jnp.concatenate([dzr[:, cols], dzi[:, cols]], axis=1).astype(BF16)
            back.append(_dot_tb(dz, wg_ref[hd]))
            dwg_ref[hd] += _dot_ta(xb2_b[:, cols], dz)
        dxb2 = dxb2 + jnp.concatenate(back, axis=1)
        nxt = dxb2_head[...]
        e1 = _shift_up(dxb2, 1, nxt)
        e2 = _shift_up(dxb2, 2, nxt)
        e3 = _shift_up(dxb2, 3, nxt)
        dxb2_head[...] = dxb2[:SUBLANES]
        dp_ref[:, 3 * d:4 * d] = (vrow(V_CONV_B_W + 3) * dxb2 + vrow(V_CONV_B_W + 2) * e1
                                  + vrow(V_CONV_B_W + 1) * e2 + vrow(V_CONV_B_W) * e3).astype(BF16)
        acc(SB2_DWB + 3, _colsum(x_b * dxb2))
        acc(SB2_DWB + 2, _colsum(x_b * e1))
        acc(SB2_DWB + 1, _colsum(x_b * e2))
        acc(SB2_DWB, _colsum(x_b * e3))
        acc(SB2_DBB, _colsum(dxb2))

        @pl.when(i == n_t - 1)
        def _():
            dgg_t = small_ref[SB2_DGT:SB2_DGT + 1, :]
            small_ref[SB2_DGT:SB2_DGT + 1, :] = dgg_t * g_post
            small_ref[SB2_DG_POST:SB2_DG_POST + 1, :] = dgg_t * gt
            lam = vrow(V_LAMBDA)
            small_ref[SB2_DLAM:SB2_DLAM + 1, :] = small_ref[SB2_DLAM:SB2_DLAM + 1, :] * (LRU_C * _sigmoid(-lam))

    rev = lambda i: (n_t - 1 - i, 0)
    tile = pl.BlockSpec((ts, d), rev)
    wide = pl.BlockSpec((ts, 7 * d), rev)
    sd = lambda dt: jax.ShapeDtypeStruct((s, d), dt)
    return _pcall(
        body, name=name, grid=(n_t,),
        out_shape=[jax.ShapeDtypeStruct((s, 7 * d), BF16), sd(BF16), sd(BF16), sd(BF16),
                   jax.ShapeDtypeStruct((2 * SUBLANES, d), F32), jax.ShapeDtypeStruct(wg.shape, F32)],
        in_specs=[tile, wide] + [tile] * 10 + [_full(mod.shape), _full(vec.shape),
                  _full(wg.shape), _full(w_a_out.shape), _full(w_b_out.shape), _full(w_o.shape)],
        out_specs=[wide, tile, tile, tile, _full((2 * SUBLANES, d)), _full(wg.shape)],
        scratch_shapes=[pltpu.VMEM((SUBLANES, d), F32)] * 4 + [pltpu.VMEM((d // LANES, ts, LANES), F32)] * 2
                       + [pltpu.VMEM((ts // SCAN_GROUP, d), F32)],
        args=[dx1, proj, conva, xb2s, hhs, pas, pbs, ys, rs_, gis, as_, mults, mod, vec, wg, w_a_out, w_b_out, w_o],
        carries=carries)


def _proj_backward(dproj, dx1, x, mod, vec, w_in, name, carries=()):
    s, d = x.shape
    nq = w_in.shape[2]
    ts = min(TOKENS_MATMUL_TILE, s)
    n_t = s // ts

    def body(dp_ref, dx1_ref, x_ref, mod_ref, vec_ref, w_ref, dx_ref, small_ref):
        i = pl.program_id(0)

        @pl.when(i == 0)
        def _():
            small_ref[...] = jnp.zeros_like(small_ref)

        dh = jnp.zeros((ts, d), F32)
        for qb in range(N_CHIP):
            dh = dh + _dot_tb(dp_ref[:, qb * nq:(qb + 1) * nq], w_ref[qb])
        xn, r = _rms(x_ref[...])
        g_pre = vec_ref[V_G_PRE_MIX:V_G_PRE_MIX + 1, :]
        sc1 = 1.0 + mod_ref[M_SC_M:M_SC_M + 1, :]
        dx_ref[...] = dx1_ref[...] + _rms_bwd(dh * (g_pre * sc1), xn, r)
        small_ref[SB1_DSH:SB1_DSH + 1, :] += _colsum(dh)
        small_ref[SB1_DSC:SB1_DSC + 1, :] += _colsum(dh * xn)

        @pl.when(i == n_t - 1)
        def _():
            dgm_t = small_ref[SB1_DSC:SB1_DSC + 1, :]
            small_ref[SB1_DSC:SB1_DSC + 1, :] = dgm_t * g_pre
            small_ref[SB1_DG_PRE:SB1_DG_PRE + 1, :] = dgm_t * sc1

    tile = pl.BlockSpec((ts, d), lambda i: (i, 0))
    return _pcall(
        body, name=name, grid=(n_t,),
        out_shape=[jax.ShapeDtypeStruct((s, d), F32), jax.ShapeDtypeStruct((SUBLANES, d), F32)],
        in_specs=[pl.BlockSpec((ts, N_CHIP * nq), lambda i: (i, 0)), tile, tile, _full(mod.shape), _full(vec.shape),
                  _full(w_in.shape)],
        out_specs=[tile, _full((SUBLANES, d))],
        args=[dproj, dx1, x, mod, vec, w_in], carries=carries)


def _weight_grad(a, b, name, col_blocks=1, tk=512, carries=()):
    s, k = a.shape
    n = b.shape[1]
    tn = n // col_blocks
    tk = min(tk, k)

    def body(a_ref, b_ref, o_ref):
        o_ref[0] = _dot_ta(a_ref[...], b_ref[...])

    (out,), carried = _pcall(
        body, name=name, grid=(col_blocks, k // tk),
        out_shape=[jax.ShapeDtypeStruct((col_blocks, k, tn), F32)],
        in_specs=[pl.BlockSpec((s, tk), lambda j, i: (0, i)), pl.BlockSpec((s, tn), lambda j, i: (0, j))],
        out_specs=[pl.BlockSpec((1, tk, tn), lambda j, i: (j, i, 0))],
        args=[a, b], carries=carries)
    return out, carried


def _adamw(w, g, m, v, name, copy_grad=False):
    shape = w.shape
    cols = shape[-1]
    rows = w.size // cols
    tr = _row_tile(rows, cols, target_bytes=1024 * 1024)
    c1 = 1.0 - ADAM_B1 ** ADAM_STEP
    c2 = 1.0 - ADAM_B2 ** ADAM_STEP
    n_out = 4 if copy_grad else 3

    def body(w_ref, g_ref, m_ref, v_ref, d_ref, nm_ref, nv_ref, *g_out):
        gv = g_ref[...]
        nm = ADAM_B1 * m_ref[...] + (1.0 - ADAM_B1) * gv
        nv = ADAM_B2 * v_ref[...] + (1.0 - ADAM_B2) * (gv * gv)
        nm_ref[...] = nm
        nv_ref[...] = nv
        d_ref[...] = (-ADAM_LR) * ((nm / c1) / (jnp.sqrt(nv / c2) + ADAM_EPS) + ADAM_WD * w_ref[...])
        if copy_grad:
            g_out[0][...] = gv

    spec = pl.BlockSpec((tr, cols), lambda i: (i, 0))
    outs = pl.pallas_call(
        body, name=name, grid=(rows // tr,),
        out_shape=[jax.ShapeDtypeStruct((rows, cols), F32)] * n_out,
        in_specs=[spec] * 4, out_specs=[spec] * n_out,
        compiler_params=_cparams(1),
    )(*(t.reshape(rows, cols) for t in (w, g, m, v)))
    return tuple(o.reshape(shape) for o in outs)


def kernel(x, c, w_mod, b_mod, g_pre_mix, g_post_mix, w_in, conv_a_w, conv_a_b, w_a_out, conv_b_w, conv_b_b, w_gate_r, b_gate_r, w_gate_i, b_gate_i, lru_lambda, w_b_out, w_o, g_pre_mlp, g_post_mlp, w_mlp_up, w_mlp_down, loss_target, m_w_mod, m_b_mod, m_g_pre_mix, m_g_post_mix, m_w_in, m_conv_a_w, m_conv_a_b, m_w_a_out, m_conv_b_w, m_conv_b_b, m_w_gate_r, m_b_gate_r, m_w_gate_i, m_b_gate_i, m_lru_lambda, m_w_b_out, m_w_o, m_g_pre_mlp, m_g_post_mlp, m_w_mlp_up, m_w_mlp_down, v_w_mod, v_b_mod, v_g_pre_mix, v_g_post_mix, v_w_in, v_conv_a_w, v_conv_a_b, v_w_a_out, v_conv_b_w, v_conv_b_b, v_w_gate_r, v_b_gate_r, v_w_gate_i, v_b_gate_i, v_lru_lambda, v_w_b_out, v_w_o, v_g_pre_mlp, v_g_post_mlp, v_w_mlp_up, v_w_mlp_down):
    weights = dict(w_mod=w_mod, b_mod=b_mod, g_pre_mix=g_pre_mix, g_post_mix=g_post_mix, w_in=w_in, conv_a_w=conv_a_w,
                   conv_a_b=conv_a_b, w_a_out=w_a_out, conv_b_w=conv_b_w, conv_b_b=conv_b_b, w_gate_r=w_gate_r,
                   b_gate_r=b_gate_r, w_gate_i=w_gate_i, b_gate_i=b_gate_i, lru_lambda=lru_lambda, w_b_out=w_b_out,
                   w_o=w_o, g_pre_mlp=g_pre_mlp, g_post_mlp=g_post_mlp, w_mlp_up=w_mlp_up, w_mlp_down=w_mlp_down)
    mom1 = dict(w_mod=m_w_mod, b_mod=m_b_mod, g_pre_mix=m_g_pre_mix, g_post_mix=m_g_post_mix, w_in=m_w_in,
                conv_a_w=m_conv_a_w, conv_a_b=m_conv_a_b, w_a_out=m_w_a_out, conv_b_w=m_conv_b_w, conv_b_b=m_conv_b_b,
                w_gate_r=m_w_gate_r, b_gate_r=m_b_gate_r, w_gate_i=m_w_gate_i, b_gate_i=m_b_gate_i,
                lru_lambda=m_lru_lambda, w_b_out=m_w_b_out, w_o=m_w_o, g_pre_mlp=m_g_pre_mlp, g_post_mlp=m_g_post_mlp,
                w_mlp_up=m_w_mlp_up, w_mlp_down=m_w_mlp_down)
    mom2 = dict(w_mod=v_w_mod, b_mod=v_b_mod, g_pre_mix=v_g_pre_mix, g_post_mix=v_g_post_mix, w_in=v_w_in,
                conv_a_w=v_conv_a_w, conv_a_b=v_conv_a_b, w_a_out=v_w_a_out, conv_b_w=v_conv_b_w, conv_b_b=v_conv_b_b,
                w_gate_r=v_w_gate_r, b_gate_r=v_b_gate_r, w_gate_i=v_w_gate_i, b_gate_i=v_b_gate_i,
                lru_lambda=v_lru_lambda, w_b_out=v_w_b_out, w_o=v_w_o, g_pre_mlp=v_g_pre_mlp, g_post_mlp=v_g_post_mlp,
                w_mlp_up=v_w_mlp_up, w_mlp_down=v_w_mlp_down)
    names = list(weights)

    n_layer = w_in.shape[0]
    s, d = x.shape[1], x.shape[2]
    n_head, bw = w_gate_r.shape[1], w_gate_r.shape[2]
    dq = d // N_CHIP
    mq = w_mod.shape[2]
    n_mod = (N_CHIP * mq) // d
    ka, kb = conv_a_w.shape[1], conv_b_w.shape[1]

    mx, my, mc = _place()
    q_me = 2 * mx + my
    q_arr = jnp.reshape(q_me, (1,)).astype(jnp.int32)

    n_conv_rows = n_layer * (ka + kb)
    conv_blk = -(-n_conv_rows // SUBLANES) * SUBLANES
    blk_rows = SUBLANES + conv_blk
    conv_rows = jnp.concatenate([jnp.concatenate([conv_a_w[l], conv_b_w[l]], axis=0) for l in range(n_layer)], axis=0)
    conv_rows = jnp.pad(conv_rows, ((0, conv_blk - n_conv_rows), (0, d - dq)))
    c_blk = jnp.pad(c, ((0, SUBLANES - 1), (0, 0)))
    gathered1 = _all_gather_small(jnp.concatenate([c_blk, conv_rows], axis=0), "gather_c_conv").reshape(N_DEV, blk_rows, d)
    c_all = gathered1[:, 0, :]
    conv_full = jnp.concatenate([gathered1[2 * qb, SUBLANES:SUBLANES + n_conv_rows, :dq] for qb in range(N_CHIP)], axis=1)

    b_mod_shard = lax.dynamic_slice_in_dim(b_mod, q_me * mq, mq, axis=1)
    mod_part = _mod_forward(c_all, w_mod, b_mod_shard, "mod_forward")
    gathered2 = _all_gather_small(mod_part, "gather_mod").reshape(N_DEV, n_layer, N_DEV, mq)
    me = 4 * mx + 2 * my + mc
    mod_rows = jnp.concatenate(
        [lax.dynamic_index_in_dim(gathered2[2 * qb], me, axis=1, keepdims=False) for qb in range(N_CHIP)], axis=1)
    mods = [jnp.pad(mod_rows[l].reshape(n_mod, d), ((0, SUBLANES - n_mod), (0, 0))) for l in range(n_layer)]

    vecs = []
    for l in range(n_layer):
        base = l * (ka + kb)
        rows = [g_pre_mix[l], g_post_mix[l], conv_a_b[l], conv_b_b[l], b_gate_r[l], b_gate_i[l], lru_lambda[l],
                g_pre_mlp[l], g_post_mlp[l]]
        vecs.append(jnp.concatenate([jnp.stack(rows, axis=0), conv_full[base:base + ka + kb]], axis=0))

    big_names = ["w_in", "w_a_out", "w_b_out", "w_o", "w_mlp_up", "w_mlp_down"]
    groups = [["w_in"], ["w_a_out", "w_b_out", "w_o"], ["w_mlp_up", "w_mlp_down"]]
    placed = {(nm, l): _cast_place(weights[nm], l, q_arr, f"cast_place_{nm}_{l}") for l in range(n_layer) for nm in big_names}
    wfull = [dict() for _ in range(n_layer)]
    stages = [(l, grp) for l in range(n_layer) for grp in groups]

    def gather_carry(stage):
        if stage >= len(stages):
            return []
        l, grp = stages[stage]
        return [_gather_carry([placed[(nm, l)] for nm in grp])]

    def gathered(stage, carried):
        if stage < len(stages):
            l, grp = stages[stage]
            for nm, w in zip(grp, carried[0]):
                wfull[l][nm] = w.reshape(d, d) if nm in groups[1] else w

    gathered(0, _run_carries(gather_carry(0), "gather_first"))
    wgs = [jnp.concatenate([w_gate_r[l], w_gate_i[l]], axis=-1).astype(BF16) for l in range(n_layer)]

    xs = x[0]
    saved = []
    for l in range(n_layer):
        wl = wfull[l]
        (h, proj), carried = _norm_proj(xs, mods[l], vecs[l], wl["w_in"], f"norm_proj_{l}", gather_carry(3 * l + 1))
        gathered(3 * l + 1, carried)
        (x1, conva, xb2, hh, ya, yb, pa, pb, mm, yy, gr, ggi, ga, gmult), carried = _mixer_forward(
            xs, proj, mods[l], vecs[l], wgs[l], wl["w_a_out"], wl["w_b_out"], wl["w_o"], f"mixer_forward_{l}",
            gather_carry(3 * l + 2))
        gathered(3 * l + 2, carried)
        (x2, h2, up, y2), carried = _mlp_forward(x1, mods[l], vecs[l], wl["w_mlp_up"], wl["w_mlp_down"],
                                                 f"mlp_forward_{l}", gather_carry(3 * l + 3))
        gathered(3 * l + 3, carried)
        saved.append(dict(x=xs, h=h, proj=proj, x1=x1, conva=conva, xb2=xb2, hh=hh, ya=ya, yb=yb, pa=pa, pb=pb, m=mm,
                          y=yy, r=gr, gi=ggi, a=ga, mult=gmult, h2=h2, up=up, y2=y2))
        xs = x2
    dxs, loss_tile = _loss_head(xs, loss_target[0], "loss_head")
    loss = lax.psum(loss_tile[0, 0], ("x", "y", "c"))

    chips_q = [q_me ^ 2, q_me ^ 1, q_me ^ 3]
    pf = jnp.stack([mc, q_me] + chips_q).astype(jnp.int32)
    rs = dict(grad={}, landed={}, to_send={}, from_chips={}, out={})
    to_exchange, to_scatter, to_join, to_gather = [], [], [], []
    small_own, small_all = {}, {}

    def ride(call, what, name=None):
        ex = list(to_exchange) if "x" in what else []
        sc = list(to_scatter) if "s" in what else []
        ga = list(to_gather) if "g" in what else []
        jn = []
        for key in (to_join if "j" in what else []):
            if key[0] not in [k[0] for k in jn]:
                jn.append(key)
        carries = []
        if ex:
            carries.append(_exchange_carry([rs["grad"][k] for k in ex]))
        if sc:
            carries.append(_scatter_carry([rs["to_send"][k] for k in sc]))
        if jn:
            carries.append(_join_carry([rs["out"][k[0]] for k in jn], [k[1] for k in jn]))
        if ga:
            carries.append(_allgather_carry([small_own[k] for k in ga]))
        if call is None:
            carried = _run_carries(carries, name) if carries else []
            res = None
        else:
            res, carried = call(carries)
        carried = list(carried)
        if ex:
            for k, ld in zip(ex, carried.pop(0)):
                to_exchange.remove(k)
                rs["landed"][k] = ld
                rs["to_send"][k] = _add_sibling_half(rs["grad"][k], ld, pf, f"rs_add_sibling_{k[0]}_{k[1]}")
                to_scatter.append(k)
        if sc:
            for k, fc in zip(sc, carried.pop(0)):
                to_scatter.remove(k)
                rs["out"][k[0]] = _add_chips(rs["grad"][k], rs["landed"][k], fc, pf, rs["out"].get(k[0]), k[1], n_layer,
                                             f"rs_add_chips_{k[0]}_{k[1]}")
                to_join.append(k)
        if jn:
            for k, o in zip(jn, carried.pop(0)):
                to_join.remove(k)
                rs["out"][k[0]] = o
        if ga:
            for k, o in zip(ga, carried.pop(0)):
                to_gather.remove(k)
                small_all[k] = o
        return res

    def gather_small(key, parts):
        small_own[key] = parts[0] if len(parts) == 1 else jnp.concatenate(parts, axis=0)
        to_gather.append(key)

    def ready(nm, l, g):
        rs["grad"][(nm, l)] = g
        to_exchange.append((nm, l))

    rowblk = lambda t: t.reshape(N_CHIP, t.shape[1] // N_CHIP, t.shape[2])
    small1_prev = None
    for l in reversed(range(n_layer)):
        wl, sv = wfull[l], saved[l]
        dx1, dy2, dup, act, small3 = ride(lambda cr: _mlp_backward(
            dxs, sv["x1"], sv["y2"], sv["up"], mods[l], vecs[l], wl["w_mlp_up"], wl["w_mlp_down"], f"mlp_backward_{l}", cr), "xsjg")
        ready("w_mlp_up", l, _weight_grad(sv["h2"], dup, f"grad_w_mlp_up_{l}", col_blocks=N_CHIP)[0])
        g_down = ride(lambda cr: _weight_grad(act, dy2, f"grad_w_mlp_down_{l}", carries=cr), "x")
        ready("w_mlp_down", l, rowblk(g_down))
        ride(None, "x", f"rs_exchange_down_{l}")
        dproj, dy, dpa, dpb, small2, dwg = ride(lambda cr: _mixer_backward(
            dx1, sv["proj"], sv["conva"], sv["xb2"], sv["hh"], sv["pa"], sv["pb"], sv["y"],
            sv["r"], sv["gi"], sv["a"], sv["mult"], mods[l], vecs[l], wgs[l],
            wl["w_a_out"], wl["w_b_out"], wl["w_o"], f"mixer_backward_{l}", cr), "xsjg")
        gather_small(("late", l, "s"), ([small1_prev] if small1_prev is not None else []) + [small2, small3])
        gather_small(("late", l, "w"), [dwg.reshape(2 * bw, d).astype(BF16)])
        ready("w_a_out", l, rowblk(_weight_grad(sv["ya"], dpa, f"grad_w_a_out_{l}")[0]))
        g_b = ride(lambda cr: _weight_grad(sv["yb"], dpb, f"grad_w_b_out_{l}", carries=cr), "x")
        ready("w_b_out", l, rowblk(g_b))
        g_o = ride(lambda cr: _weight_grad(sv["m"], dy, f"grad_w_o_{l}", carries=cr), "x")
        ready("w_o", l, rowblk(g_o))
        g_in = ride(lambda cr: _weight_grad(sv["h"], dproj, f"grad_w_in_{l}", col_blocks=N_CHIP, carries=cr), "xsg")
        ready("w_in", l, g_in)
        if l == 0:
            ride(None, "x", "rs_exchange_last")
        dxs, small1_prev = ride(lambda cr: _proj_backward(dproj, dx1, sv["x"], mods[l], vecs[l], wl["w_in"],
                                                          f"proj_backward_{l}", cr), "xsjg")
    grad_x = dxs[None]
    gather_small(("last", 0, "s"), [small1_prev])

    tail = 0
    while to_exchange or to_scatter or to_join or to_gather:
        ride(None, "xsjg", f"rs_tail_{tail}")
        tail += 1

    me_dev = 4 * mx + 2 * my + mc
    me_arr = jnp.reshape(me_dev, (1,)).astype(jnp.int32)
    is_me = (jnp.arange(N_DEV) == me_dev)[:, None, None]
    sums = {k: _sum_devices(small_all[k], small_own[k], me_arr, f"sum_small_{k[0]}_{k[1]}_{k[2]}") for k in small_own}

    def rows_of(l, part):
        if part == 0:
            return (("late", l - 1, "s"), 0) if l >= 1 else (("last", 0, "s"), 0)
        if part == 3:
            return ("late", l, "w"), 0
        base = SUBLANES if l < n_layer - 1 else 0
        return ("late", l, "s"), base + (0, 0, 2 * SUBLANES)[part]

    def summed(l, part, row, n_rows=1):
        key, base = rows_of(l, part)
        return sums[key][base + row:base + row + n_rows]

    def per_device(l, part, row):
        key, base = rows_of(l, part)
        own = small_own[key]
        others = small_all[key].reshape((N_DEV,) + own.shape)
        return jnp.where(is_me, own[None, base + row:base + row + 1], others[:, base + row:base + row + 1])

    mod_rows = [(0, SB1_DSH), (0, SB1_DSC), (1, SB2_DGT), (2, SB3_DSH), (2, SB3_DSC), (2, SB3_DGT)]
    dmod_all = jnp.stack([jnp.concatenate([per_device(l, p, r)[:, 0, :] for p, r in mod_rows], axis=1)
                          for l in range(n_layer)], axis=0)
    o1, o2, o3, o4 = 0, SUBLANES, 3 * SUBLANES, 4 * SUBLANES
    small_sum = jnp.stack([jnp.concatenate([summed(l, 0, 0, SUBLANES), summed(l, 1, 0, 2 * SUBLANES),
                                            summed(l, 2, 0, SUBLANES), summed(l, 3, 0, 2 * bw)], axis=0)
                           for l in range(n_layer)], axis=0)
    mod_rows_of = [o1 + SB1_DSH, o1 + SB1_DSC, o2 + SB2_DGT, o3 + SB3_DSH, o3 + SB3_DSC, o3 + SB3_DGT]
    grads = {}
    grads["w_mod"] = _mod_backward(c_all.T, lax.dynamic_slice_in_dim(dmod_all, q_me * mq, mq, axis=2), "mod_backward")
    grads["b_mod"] = jnp.concatenate([small_sum[:, k, :] for k in mod_rows_of], axis=1)
    grads["g_pre_mix"] = small_sum[:, o1 + SB1_DG_PRE]
    grads["g_post_mix"] = small_sum[:, o2 + SB2_DG_POST]
    grads["conv_a_w"] = lax.dynamic_slice_in_dim(small_sum[:, o2 + SB2_DWA:o2 + SB2_DWA + ka], q_me * dq, dq, axis=2)
    grads["conv_a_b"] = small_sum[:, o2 + SB2_DBA]
    grads["conv_b_w"] = lax.dynamic_slice_in_dim(small_sum[:, o2 + SB2_DWB:o2 + SB2_DWB + kb], q_me * dq, dq, axis=2)
    grads["conv_b_b"] = small_sum[:, o2 + SB2_DBB]
    grads["lru_lambda"] = small_sum[:, o2 + SB2_DLAM]
    grads["b_gate_r"] = small_sum[:, o2 + SB2_DBR]
    grads["b_gate_i"] = small_sum[:, o2 + SB2_DBI]
    grads["g_pre_mlp"] = small_sum[:, o3 + SB3_DG_PRE]
    grads["g_post_mlp"] = small_sum[:, o3 + SB3_DG_POST]
    dwg_sum = small_sum[:, o4:].reshape(n_layer, n_head, bw, 2 * bw)
    grads["w_gate_r"] = dwg_sum[..., :bw]
    grads["w_gate_i"] = dwg_sum[..., bw:]

    for nm in big_names:
        grads[nm] = rs["out"][nm].reshape(weights[nm].shape)

    deltas, new_m, new_v = {}, {}, {}
    for nm in names:
        res = _adamw(weights[nm], grads[nm], mom1[nm], mom2[nm], f"adamw_{nm}", copy_grad=nm in big_names)
        deltas[nm], new_m[nm], new_v[nm] = res[:3]
        if nm in big_names:
            grads[nm] = res[3]
    return (loss, grad_x, *[grads[nm] for nm in names], *[deltas[nm] for nm in names],
            *[new_m[nm] for nm in names], *[new_v[nm] for nm in names])
```

```python
import jax
import jax.numpy as jnp
from jax import lax
from jax.experimental import pallas as pl
from jax.experimental.pallas import tpu as pltpu

F32 = jnp.float32
BF16 = jnp.bfloat16
MESH = pl.DeviceIdType.MESH

EPS = 1e-6
LRU_C = 8.0
N_CHIP = 4
N_DEV = 8
ADAM_LR = 0.001
ADAM_B1 = 0.9
ADAM_B2 = 0.999
ADAM_EPS = 1e-08
ADAM_WD = 0.01
ADAM_STEP = 10

VMEM_LIMIT_BYTES = 56 * 1024 * 1024
SUBLANES = 8
LANES = 128
TOKENS_MATMUL_TILE = 512
TOKENS_MIXER_TILE = 256
GELU_K0 = 0.7978845608028654
GELU_K1 = 0.044715

V_G_PRE_MIX, V_G_POST_MIX, V_CONV_A_B, V_CONV_B_B, V_B_GATE_R, V_B_GATE_I, V_LAMBDA, V_G_PRE_MLP, V_G_POST_MLP = range(9)
V_CONV_A_W = 9
V_CONV_B_W = 12
M_SH_M, M_SC_M, M_GT_M, M_SH_F, M_SC_F, M_GT_F = range(6)


def _cparams(n_grid=0):
    sem = ("arbitrary",) * n_grid if n_grid else None
    return pltpu.CompilerParams(dimension_semantics=sem, vmem_limit_bytes=VMEM_LIMIT_BYTES)


def _full(shape):
    return pl.BlockSpec(shape, lambda *_: (0,) * len(shape))


def _dot(a, b):
    return jnp.dot(a, b, preferred_element_type=F32)


def _dot_tb(a, b):
    return lax.dot_general(a, b, (((1,), (1,)), ((), ())), preferred_element_type=F32)


def _dot_ta(a, b):
    return lax.dot_general(a, b, (((0,), (0,)), ((), ())), preferred_element_type=F32)


def _sigmoid(x):
    return 1.0 / (1.0 + jnp.exp(-x))


def _softplus(x):
    return jnp.maximum(x, 0.0) + jnp.log1p(jnp.exp(-jnp.abs(x)))


def _neg_expm1(x):
    series = -x * (1.0 + 0.5 * x * (1.0 + (x / 3.0) * (1.0 + 0.25 * x)))
    return jnp.where(x > -1e-2, series, 1.0 - jnp.exp(x))


def _gelu_and_grad(x):
    x2 = x * x
    s = _sigmoid(x * (2.0 * GELU_K0 + (2.0 * GELU_K0 * GELU_K1) * x2))
    gel = x * s
    return gel, s + gel * (1.0 - s) * (2.0 * GELU_K0 + (6.0 * GELU_K0 * GELU_K1) * x2)


def _rms(x):
    r = lax.rsqrt(jnp.mean(x * x, axis=-1, keepdims=True) + EPS)
    return x * r, r


def _rms_bwd(dxn, xn, r):
    return r * (dxn - xn * jnp.mean(dxn * xn, axis=-1, keepdims=True))


def _colsum(x):
    return jnp.sum(x, axis=0, keepdims=True)


def _rows(t, w):
    return lax.broadcasted_iota(jnp.int32, (t, w), 0)


def _shift_down(x, k, prev8):
    t, w = x.shape
    rolled = pltpu.roll(x, k, 0)
    head = jnp.where(_rows(SUBLANES, w) < k, pltpu.roll(prev8, k, 0), rolled[:SUBLANES])
    return jnp.concatenate([head, rolled[SUBLANES:]], axis=0)


def _shift_up(x, k, next8):
    t, w = x.shape
    rolled = pltpu.roll(x, t - k, 0)
    tail = jnp.where(_rows(SUBLANES, w) >= SUBLANES - k, pltpu.roll(next8, SUBLANES - k, 0), rolled[t - SUBLANES:])
    return jnp.concatenate([rolled[:t - SUBLANES], tail], axis=0)


SCAN_GROUP = 16


def _scan_steps(a, b, group, reverse):
    t, w = a.shape
    pos = _rows(t, w) & (group - 1)
    s = 1
    while s < group:
        keep = (pos < group - s) if reverse else (pos >= s)
        shift = (t - s) if reverse else s
        b = b + a * jnp.where(keep, pltpu.roll(b, shift, 0), 0.0)
        a = a * jnp.where(keep, pltpu.roll(a, shift, 0), 1.0)
        s *= 2
    return b, a


def _scan_two_level(a, b, carry_row, a_buf, b_buf, c_buf, reverse):
    t, w = a.shape
    grp = SCAN_GROUP
    n_grp = t // grp
    h_loc, a_cum = _scan_steps(a, b, grp, reverse)
    end = 0 if reverse else grp - 1
    a_end, h_end = [], []
    for j in range(w // LANES):
        a_buf[j] = a_cum[:, j * LANES:(j + 1) * LANES]
        b_buf[j] = h_loc[:, j * LANES:(j + 1) * LANES]
        a_end.append(a_buf[j, pl.ds(end, n_grp, stride=grp), :])
        h_end.append(b_buf[j, pl.ds(end, n_grp, stride=grp), :])
    a_end = jnp.concatenate(a_end, axis=1)
    h_end = jnp.concatenate(h_end, axis=1)
    h_grp, a_grp = _scan_steps(a_end, h_end, n_grp, reverse)
    h_grp = h_grp + a_grp * carry_row
    rows = _rows(n_grp, w)
    if reverse:
        entering = jnp.where(rows == n_grp - 1, carry_row, pltpu.roll(h_grp, n_grp - 1, 0))
    else:
        entering = jnp.where(rows == 0, carry_row, pltpu.roll(h_grp, 1, 0))
    c_buf[...] = entering
    out = [h_loc[g * grp:(g + 1) * grp] + a_cum[g * grp:(g + 1) * grp] * c_buf[g:g + 1, :] for g in range(n_grp)]
    return jnp.concatenate(out, axis=0)


def _row_tile(rows, cols, itemsize=4, target_bytes=2 * 1024 * 1024):
    if rows * cols * itemsize <= target_bytes or rows % SUBLANES:
        return rows
    t = max(SUBLANES, (target_bytes // (cols * itemsize)) // SUBLANES * SUBLANES)
    while rows % t:
        t -= SUBLANES
    return t


def _place():
    return lax.axis_index("x"), lax.axis_index("y"), lax.axis_index("c")


def _other_chips(x, y):
    chips = [(1 - x, y), (x, 1 - y), (1 - x, 1 - y)]
    return chips, [2 * cx + cy for cx, cy in chips]


def _all_gather_small(block, name):
    m_per, n = block.shape

    def body(x_ref, out_ref, send_sems, recv_sems, local_sem):
        x, y, c = _place()
        me, sibling = (x, y, c), (x, y, 1 - c)
        chips, _ = _other_chips(x, y)

        def rows(px, py, pc):
            return out_ref.at[pl.ds((4 * px + 2 * py + pc) * m_per, m_per), :]

        def copy(k, blk, to, src=None):
            return pltpu.make_async_remote_copy(
                src_ref=rows(*blk) if src is None else src, dst_ref=rows(*blk),
                send_sem=send_sems.at[k], recv_sem=recv_sems.at[k], device_id=to, device_id_type=MESH)

        mine = pltpu.make_async_copy(x_ref, rows(*me), local_sem)
        mine.start()
        first = [copy(0, me, sibling, src=x_ref)]
        first += [copy(1 + j, me, (*chip, c), src=x_ref) for j, chip in enumerate(chips)]
        for cp in first:
            cp.start()
        passed = [copy(4 + j, (*chip, c), sibling) for j, chip in enumerate(chips)]
        for j, chip in enumerate(chips):
            copy(1 + j, (*chip, c), me).wait_recv()
            passed[j].start()
        copy(0, sibling, me).wait_recv()
        for j, chip in enumerate(chips):
            copy(4 + j, (*chip, 1 - c), me).wait_recv()
        for cp in first + passed:
            cp.wait_send()
        mine.wait()

    return pl.pallas_call(
        body, name=name,
        out_shape=jax.ShapeDtypeStruct((N_DEV * m_per, n), block.dtype),
        in_specs=[pl.BlockSpec(memory_space=pltpu.VMEM)],
        out_specs=pl.BlockSpec(memory_space=pltpu.VMEM),
        scratch_shapes=[pltpu.SemaphoreType.DMA((7,)), pltpu.SemaphoreType.DMA((7,)), pltpu.SemaphoreType.DMA],
        compiler_params=pltpu.CompilerParams(vmem_limit_bytes=VMEM_LIMIT_BYTES),
    )(block)


def _cast_place(w, layer, q_arr, name):
    _, r, cols = w.shape
    tr = _row_tile(r, cols)

    def body(q_ref, w_ref, o_ref):
        o_ref[...] = w_ref[...].astype(BF16)

    return pl.pallas_call(
        body, name=name,
        out_shape=jax.ShapeDtypeStruct((N_CHIP, r, cols), BF16),
        grid_spec=pltpu.PrefetchScalarGridSpec(
            num_scalar_prefetch=1, grid=(r // tr,),
            in_specs=[pl.BlockSpec((1, tr, cols), lambda i, q_ref: (layer, i, 0))],
            out_specs=pl.BlockSpec((1, tr, cols), lambda i, q_ref: (q_ref[0], i, 0))),
        compiler_params=_cparams(1),
    )(q_arr, w)


class _Carry:
    def __init__(self, ins, out_shapes, aliases, sem_shapes, start, finish, mid=None):
        self.ins, self.out_shapes, self.aliases, self.sem_shapes = list(ins), list(out_shapes), dict(aliases), list(sem_shapes)
        self.start, self.mid, self.finish = start, mid, finish


def _pcall(body, *, name, grid, in_specs, out_specs, out_shape, args, scratch_shapes=(), carries=(), mid_frac=0.85):
    in_specs, out_specs, out_shape = list(in_specs), list(out_specs), list(out_shape)
    scratch_shapes, args = list(scratch_shapes), list(args)
    n_in, n_out, n_scr = len(in_specs), len(out_shape), len(scratch_shapes)
    steps = 1
    for g in grid:
        steps *= g
    mid_step = min(steps - 1, int(steps * mid_frac))
    any_spec = pl.BlockSpec(memory_space=pl.ANY)
    aliases = {}
    spans = []
    for cr in carries:
        spans.append((len(args), len(out_shape), len(scratch_shapes)))
        for a, b in cr.aliases.items():
            aliases[len(args) + a] = len(out_shape) + b
        args += cr.ins
        in_specs += [any_spec] * len(cr.ins)
        out_shape += cr.out_shapes
        out_specs += [any_spec] * len(cr.out_shapes)
        scratch_shapes += cr.sem_shapes
    n_all_in = len(args)
    n_all_out = len(out_shape)

    def wrapped(*refs):
        ins, outs, scr = refs[:n_all_in], refs[n_all_in:n_all_in + n_all_out], refs[n_all_in + n_all_out:]
        parts = [(cr, ins[a:a + len(cr.ins)], outs[b:b + len(cr.out_shapes)], scr[s:s + len(cr.sem_shapes)])
                 for cr, (a, b, s) in zip(carries, spans)]
        lin = 0
        for ax, g in enumerate(grid):
            lin = lin * g + pl.program_id(ax)

        def at(step, fn):
            if steps == 1:
                fn()
            else:
                pl.when(lin == step)(fn)

        def start_all():
            for cr, ci, co, cs in parts:
                cr.start(ci, co, cs)

        def mid_all():
            for cr, ci, co, cs in parts:
                if cr.mid is not None:
                    cr.mid(ci, co, cs)

        def finish_all():
            for cr, ci, co, cs in parts:
                cr.finish(ci, co, cs)

        if parts:
            at(0, start_all)
        body(*ins[:n_in], *outs[:n_out], *scr[:n_scr])
        if parts:
            at(mid_step, mid_all)
            at(steps - 1, finish_all)

    res = pl.pallas_call(
        wrapped, name=name, grid=tuple(grid), out_shape=out_shape, in_specs=in_specs, out_specs=out_specs,
        scratch_shapes=scratch_shapes, input_output_aliases=aliases, compiler_params=_cparams(len(grid)),
    )(*args)
    res = list(res)
    return res[:n_out], [res[b:b + len(cr.out_shapes)] for cr, (_, b, _) in zip(carries, spans)]


def _run_carries(carries, name):
    return _pcall(lambda: None, name=name, grid=(), in_specs=[], out_specs=[], out_shape=[], args=[], carries=carries)[1]


def _gather_carry(bufs):
    n = len(bufs)

    def copies(o_refs, sems):
        send_sems, recv_sems = sems
        x, y, c = _place()
        q = 2 * x + y
        sibling = (x, y, 1 - c)
        chips, qs = _other_chips(x, y)

        def half(w, shard, pc):
            rh = bufs[w].shape[1] // 2
            return o_refs[w].at[shard, pl.ds(pc * rh, rh), :]

        def over_ici(w, j, shard):
            blk = half(w, shard, c)
            return pltpu.make_async_remote_copy(
                src_ref=blk, dst_ref=blk, send_sem=send_sems.at[w, j], recv_sem=recv_sems.at[w, j],
                device_id=(*chips[j], c), device_id_type=MESH)

        def to_sibling(w, j, pc):
            blk = half(w, qs[j], pc)
            return pltpu.make_async_remote_copy(
                src_ref=blk, dst_ref=blk, send_sem=send_sems.at[w, 3 + j], recv_sem=recv_sems.at[w, 3 + j],
                device_id=sibling, device_id_type=MESH)

        return q, c, qs, over_ici, to_sibling

    pairs = [(w, j) for w in range(n) for j in range(3)]

    def start(i_refs, o_refs, sems):
        q, _, _, over_ici, _ = copies(o_refs, sems)
        for w, j in pairs:
            over_ici(w, j, q).start()

    def mid(i_refs, o_refs, sems):
        _, c, qs, over_ici, to_sibling = copies(o_refs, sems)
        for w, j in pairs:
            over_ici(w, j, qs[j]).wait_recv()
            to_sibling(w, j, c).start()

    def finish(i_refs, o_refs, sems):
        q, c, _, over_ici, to_sibling = copies(o_refs, sems)
        for w, j in pairs:
            to_sibling(w, j, 1 - c).wait_recv()
        for w, j in pairs:
            over_ici(w, j, q).wait_send()
            to_sibling(w, j, c).wait_send()

    return _Carry(bufs, [jax.ShapeDtypeStruct(b.shape, b.dtype) for b in bufs], {w: w for w in range(n)},
                  [pltpu.SemaphoreType.DMA((n, 6)), pltpu.SemaphoreType.DMA((n, 6))], start, finish, mid)


def _exchange_carry(grads):
    n = len(grads)

    def copies(g_refs, l_refs, sems):
        send_sems, recv_sems = sems
        x, y, c = _place()
        out = []
        for w in range(n):
            rh = grads[w].shape[1] // 2
            out.append(pltpu.make_async_remote_copy(
                src_ref=g_refs[w].at[:, pl.ds((1 - c) * rh, rh), :], dst_ref=l_refs[w],
                send_sem=send_sems.at[w], recv_sem=recv_sems.at[w], device_id=(x, y, 1 - c), device_id_type=MESH))
        return out

    def start(g_refs, l_refs, sems):
        for cp in copies(g_refs, l_refs, sems):
            cp.start()

    def finish(g_refs, l_refs, sems):
        for cp in copies(g_refs, l_refs, sems):
            cp.wait()

    return _Carry(grads, [jax.ShapeDtypeStruct((N_CHIP, g.shape[1] // 2, g.shape[2]), g.dtype) for g in grads], {},
                  [pltpu.SemaphoreType.DMA((n,)), pltpu.SemaphoreType.DMA((n,))], start, finish)


def _scatter_carry(sums):
    n = len(sums)

    def copies(s_refs, l_refs, sems):
        send_sems, recv_sems = sems
        x, y, c = _place()
        chips, _ = _other_chips(x, y)
        return [pltpu.make_async_remote_copy(
            src_ref=s_refs[w].at[j], dst_ref=l_refs[w].at[j], send_sem=send_sems.at[w, j], recv_sem=recv_sems.at[w, j],
            device_id=(*chips[j], c), device_id_type=MESH) for w in range(n) for j in range(3)]

    def start(s_refs, l_refs, sems):
        for cp in copies(s_refs, l_refs, sems):
            cp.start()

    def finish(s_refs, l_refs, sems):
        for cp in copies(s_refs, l_refs, sems):
            cp.wait()

    return _Carry(sums, [jax.ShapeDtypeStruct(s.shape, s.dtype) for s in sums], {},
                  [pltpu.SemaphoreType.DMA((n, 3)), pltpu.SemaphoreType.DMA((n, 3))], start, finish)


def _join_carry(outs, layers):
    n = len(outs)

    def copy(o_refs, sems, w, mine):
        send_sems, recv_sems = sems
        x, y, c = _place()
        r = outs[w].shape[1]
        rows = o_refs[w].at[layers[w], pl.ds((c if mine else 1 - c) * (r // 2), r // 2), :]
        return pltpu.make_async_remote_copy(
            src_ref=rows, dst_ref=rows, send_sem=send_sems.at[w], recv_sem=recv_sems.at[w],
            device_id=(x, y, 1 - c), device_id_type=MESH)

    def start(i_refs, o_refs, sems):
        for w in range(n):
            copy(o_refs, sems, w, True).start()

    def finish(i_refs, o_refs, sems):
        for w in range(n):
            copy(o_refs, sems, w, True).wait_send()
        for w in range(n):
            copy(o_refs, sems, w, False).wait_recv()

    return _Carry(outs, [jax.ShapeDtypeStruct(o.shape, o.dtype) for o in outs], {w: w for w in range(n)},
                  [pltpu.SemaphoreType.DMA((n,)), pltpu.SemaphoreType.DMA((n,))], start, finish)


PF_C, PF_Q, PF_QS = 0, 1, 2


def _add_sibling_half(g, landed, pf, name):
    _, r, cols = g.shape
    rh = r // 2
    tr = _row_tile(rh, cols)
    nr = rh // tr

    def body(pf_ref, g_ref, l_ref, o_ref):
        o_ref[...] = (g_ref[...] + l_ref[...]).astype(BF16)

    return pl.pallas_call(
        body, name=name,
        out_shape=jax.ShapeDtypeStruct((3, rh, cols), BF16),
        grid_spec=pltpu.PrefetchScalarGridSpec(
            num_scalar_prefetch=1, grid=(3, nr),
            in_specs=[pl.BlockSpec((1, tr, cols), lambda j, i, pf_ref: (pf_ref[PF_QS + j], pf_ref[PF_C] * nr + i, 0)),
                      pl.BlockSpec((1, tr, cols), lambda j, i, pf_ref: (pf_ref[PF_QS + j], i, 0))],
            out_specs=pl.BlockSpec((1, tr, cols), lambda j, i, pf_ref: (j, i, 0))),
        compiler_params=_cparams(2),
    )(pf, g, landed)


def _add_chips(g, landed, from_chips, pf, prev, layer, n_layer, name):
    _, r, cols = g.shape
    rh = r // 2
    tr = _row_tile(rh, cols)
    nr = rh // tr

    def body(pf_ref, g_ref, l_ref, f_ref, *rest):
        o_ref = rest[-1]
        acc = g_ref[0] + l_ref[0]
        for j in range(3):
            acc = acc + f_ref[j].astype(F32)
        o_ref[0] = acc

    in_specs = [pl.BlockSpec((1, tr, cols), lambda i, pf_ref: (pf_ref[PF_Q], pf_ref[PF_C] * nr + i, 0)),
                pl.BlockSpec((1, tr, cols), lambda i, pf_ref: (pf_ref[PF_Q], i, 0)),
                pl.BlockSpec((3, tr, cols), lambda i, pf_ref: (0, i, 0))]
    args = [pf, g, landed, from_chips]
    aliases = {}
    if prev is not None:
        in_specs.append(pl.BlockSpec(memory_space=pl.ANY))
        args.append(prev)
        aliases = {4: 0}
    return pl.pallas_call(
        body, name=name,
        out_shape=jax.ShapeDtypeStruct((n_layer, r, cols), F32),
        grid_spec=pltpu.PrefetchScalarGridSpec(
            num_scalar_prefetch=1, grid=(nr,), in_specs=in_specs,
            out_specs=pl.BlockSpec((1, tr, cols), lambda i, pf_ref: (layer, pf_ref[PF_C] * nr + i, 0))),
        input_output_aliases=aliases,
        compiler_params=_cparams(1),
    )(*args)


def _allgather_carry(blocks):
    n = len(blocks)

    def copies(b_refs, o_refs, sems):
        send_sems, recv_sems = sems
        x, y, c = _place()
        chips, _ = _other_chips(x, y)

        def place(w, px, py, pc):
            m = blocks[w].shape[0]
            return o_refs[w].at[pl.ds((4 * px + 2 * py + pc) * m, m), :]

        def own_to(w, k, to):
            dst = place(w, x, y, c)
            return pltpu.make_async_remote_copy(src_ref=b_refs[w], dst_ref=dst, send_sem=send_sems.at[w, k],
                                                recv_sem=recv_sems.at[w, k], device_id=to, device_id_type=MESH)

        def landed_from(w, k, px, py, pc):
            blk = place(w, px, py, pc)
            return pltpu.make_async_remote_copy(src_ref=blk, dst_ref=blk, send_sem=send_sems.at[w, k],
                                                recv_sem=recv_sems.at[w, k], device_id=(x, y, 1 - c), device_id_type=MESH)

        return x, y, c, chips, own_to, landed_from

    def start(b_refs, o_refs, sems):
        x, y, c, chips, own_to, _ = copies(b_refs, o_refs, sems)
        for w in range(n):
            own_to(w, 0, (x, y, 1 - c)).start()
            for j, chip in enumerate(chips):
                own_to(w, 1 + j, (*chip, c)).start()

    def mid(b_refs, o_refs, sems):
        x, y, c, chips, _, landed_from = copies(b_refs, o_refs, sems)
        for w in range(n):
            for j, chip in enumerate(chips):
                landed_from(w, 1 + j, *chip, c).wait_recv()
                landed_from(w, 4 + j, *chip, c).start()

    def finish(b_refs, o_refs, sems):
        x, y, c, chips, own_to, landed_from = copies(b_refs, o_refs, sems)
        for w in range(n):
            landed_from(w, 0, x, y, 1 - c).wait_recv()
            for j, chip in enumerate(chips):
                landed_from(w, 4 + j, *chip, 1 - c).wait_recv()
            own_to(w, 0, (x, y, 1 - c)).wait_send()
            for j, chip in enumerate(chips):
                own_to(w, 1 + j, (*chip, c)).wait_send()
                landed_from(w, 4 + j, *chip, c).wait_send()

    return _Carry(blocks, [jax.ShapeDtypeStruct((N_DEV * b.shape[0], b.shape[1]), b.dtype) for b in blocks], {},
                  [pltpu.SemaphoreType.DMA((n, 7)), pltpu.SemaphoreType.DMA((n, 7))], start, finish, mid)


def _sum_devices(gathered, own, me_arr, name):
    m, n = own.shape
    tr = _row_tile(m, n, itemsize=own.dtype.itemsize, target_bytes=256 * 1024)
    nr = m // tr

    def body(me_ref, *refs):
        g_refs, own_ref, o_ref = refs[:N_DEV], refs[N_DEV], refs[N_DEV + 1]
        me = me_ref[0]
        acc = None
        for dev in range(N_DEV):
            term = jnp.where(me == dev, own_ref[...], g_refs[dev][...]).astype(F32)
            acc = term if acc is None else acc + term
        o_ref[...] = acc

    def dev_rows(dev):
        return pl.BlockSpec((tr, n), lambda i, me_ref: (dev * nr + i, 0))

    return pl.pallas_call(
        body, name=name,
        out_shape=jax.ShapeDtypeStruct((m, n), F32),
        grid_spec=pltpu.PrefetchScalarGridSpec(
            num_scalar_prefetch=1, grid=(nr,),
            in_specs=[dev_rows(dev) for dev in range(N_DEV)] + [pl.BlockSpec((tr, n), lambda i, me_ref: (i, 0))],
            out_specs=pl.BlockSpec((tr, n), lambda i, me_ref: (i, 0))),
        compiler_params=_cparams(1),
    )(me_arr, *([gathered] * N_DEV), own)


def _mod_forward(c_all, w_mod, b_mod_shard, name):
    n_layer, d, mq = w_mod.shape

    def body(c_ref, w_ref, b_ref, o_ref):
        cv = c_ref[...]
        o_ref[...] = _dot(cv * _sigmoid(cv), w_ref[0]) + b_ref[0]

    return pl.pallas_call(
        body, name=name, grid=(n_layer,),
        out_shape=jax.ShapeDtypeStruct((n_layer * N_DEV, mq), F32),
        in_specs=[_full((N_DEV, d)), pl.BlockSpec((1, d, mq), lambda l: (l, 0, 0)),
                  pl.BlockSpec((1, 1, mq), lambda l: (l, 0, 0))],
        out_specs=pl.BlockSpec((N_DEV, mq), lambda l: (l, 0)),
        compiler_params=_cparams(1),
    )(c_all, w_mod, b_mod_shard.reshape(n_layer, 1, mq))


def _mod_backward(c_all_t, dmod_shard, name):
    n_layer, _, mq = dmod_shard.shape
    d = c_all_t.shape[0]

    def body(c_ref, dm_ref, o_ref):
        cv = c_ref[...]
        o_ref[0] = _dot(cv * _sigmoid(cv), dm_ref[0])

    return pl.pallas_call(
        body, name=name, grid=(n_layer,),
        out_shape=jax.ShapeDtypeStruct((n_layer, d, mq), F32),
        in_specs=[_full((d, N_DEV)), pl.BlockSpec((1, N_DEV, mq), lambda l: (l, 0, 0))],
        out_specs=pl.BlockSpec((1, d, mq), lambda l: (l, 0, 0)),
        compiler_params=_cparams(1),
    )(c_all_t, dmod_shard)


def _norm_proj(x, mod, vec, w_in, name, carries=()):
    s, d = x.shape
    nq = w_in.shape[2]
    ts = min(TOKENS_MATMUL_TILE, s)

    def body(x_ref, mod_ref, vec_ref, w_ref, h_ref, p_ref, dgel_ref):
        xn, _ = _rms(x_ref[...])
        gm = vec_ref[V_G_PRE_MIX:V_G_PRE_MIX + 1, :] * (1.0 + mod_ref[M_SC_M:M_SC_M + 1, :])
        h = (xn * gm + mod_ref[M_SH_M:M_SH_M + 1, :]).astype(BF16)
        h_ref[...] = h
        for qb in range(N_CHIP):
            pq = _dot(h, w_ref[qb])
            for k in range(N_CHIP * nq // d):
                lo, hi = max(qb * nq, k * d), min((qb + 1) * nq, (k + 1) * d)
                if lo >= hi:
                    continue
                piece = pq[:, lo - qb * nq:hi - qb * nq]
                if k == 4:
                    piece, dgel = _gelu_and_grad(piece)
                    dgel_ref[:, lo - 4 * d:hi - 4 * d] = dgel.astype(BF16)
                elif k >= 5:
                    piece = _sigmoid(piece)
                p_ref[:, lo:hi] = piece.astype(BF16)

    tile = pl.BlockSpec((ts, d), lambda i: (i, 0))
    return _pcall(
        body, name=name, grid=(s // ts,),
        out_shape=[jax.ShapeDtypeStruct((s, d), BF16), jax.ShapeDtypeStruct((s, N_CHIP * nq), BF16),
                   jax.ShapeDtypeStruct((s, d), BF16)],
        in_specs=[tile, _full(mod.shape), _full(vec.shape), _full(w_in.shape)],
        out_specs=[tile, pl.BlockSpec((ts, N_CHIP * nq), lambda i: (i, 0)), tile],
        args=[x, mod, vec, w_in], carries=carries)


def _gate_pre(xb2_b, wg_ref, n_head, bw):
    zr, zi = [], []
    for hd in range(n_head):
        z = _dot(xb2_b[:, hd * bw:(hd + 1) * bw], wg_ref[hd])
        zr.append(z[:, :bw])
        zi.append(z[:, bw:])
    return jnp.concatenate(zr, axis=1), jnp.concatenate(zi, axis=1)


def _lru_coeffs(xb2, wg_ref, vec_ref, n_head, bw):
    zr, zi = _gate_pre(xb2.astype(BF16), wg_ref, n_head, bw)
    r = _sigmoid(zr + vec_ref[V_B_GATE_R:V_B_GATE_R + 1, :])
    gi = _sigmoid(zi + vec_ref[V_B_GATE_I:V_B_GATE_I + 1, :])
    sp = _softplus(-vec_ref[V_LAMBDA:V_LAMBDA + 1, :])
    log_a = (-LRU_C) * r * sp
    a = jnp.exp(log_a)
    mult = jnp.sqrt(_neg_expm1(2.0 * log_a))
    return r, gi, sp, a, mult


def _mixer_forward(x, proj, mod, vec, wg, w_a_out, w_b_out, w_o, name, carries=()):
    s, d = x.shape
    n_head, bw, _ = wg.shape
    ts = min(TOKENS_MIXER_TILE, s)

    def body(x_ref, p_ref, mod_ref, vec_ref, wg_ref, wa_ref, wb_ref, wo_ref,
             x1_ref, conva_ref, xb2_ref, hh_ref, ya_ref, yb_ref, pa_ref, pb_ref, m_ref, y_ref,
             r_ref, gi_ref, a_ref, mult_ref,
             cv_tail, xb_tail, h_last, a_buf, b_buf, c_buf):
        i = pl.program_id(0)

        @pl.when(i == 0)
        def _():
            cv_tail[...] = jnp.zeros_like(cv_tail)
            xb_tail[...] = jnp.zeros_like(xb_tail)
            h_last[...] = jnp.zeros_like(h_last)

        def seg(k):
            return p_ref[:, k * d:(k + 1) * d].astype(F32)

        def vrow(k):
            return vec_ref[k:k + 1, :]

        b_a, c_a, v_a, x_b, gel, sa, sb = (seg(k) for k in range(7))
        cv = c_a * v_a
        prev_cv = cv_tail[...]
        conv_a = (vrow(V_CONV_A_B) + vrow(V_CONV_A_W) * _shift_down(cv, 2, prev_cv)
                  + vrow(V_CONV_A_W + 1) * _shift_down(cv, 1, prev_cv) + vrow(V_CONV_A_W + 2) * cv)
        cv_tail[...] = cv[ts - SUBLANES:]
        y_a = b_a * conv_a
        prev_xb = xb_tail[...]
        xb2 = (vrow(V_CONV_B_B) + vrow(V_CONV_B_W) * _shift_down(x_b, 3, prev_xb)
               + vrow(V_CONV_B_W + 1) * _shift_down(x_b, 2, prev_xb)
               + vrow(V_CONV_B_W + 2) * _shift_down(x_b, 1, prev_xb) + vrow(V_CONV_B_W + 3) * x_b)
        xb_tail[...] = x_b[ts - SUBLANES:]
        r, gi, _, a, mult = _lru_coeffs(xb2, wg_ref, vec_ref, n_head, bw)
        r_ref[...] = r
        gi_ref[...] = gi
        a_ref[...] = a
        mult_ref[...] = mult
        hh = _scan_two_level(a, mult * gi * xb2, h_last[SUBLANES - 1:SUBLANES, :], a_buf, b_buf, c_buf, reverse=False)
        h_last[...] = hh[ts - SUBLANES:]
        y_b = hh * gel
        ya_b, yb_b = y_a.astype(BF16), y_b.astype(BF16)
        pa = _dot(ya_b, wa_ref[...])
        pb = _dot(yb_b, wb_ref[...])
        m = (sa * pa + sb * pb).astype(BF16)
        y = _dot(m, wo_ref[...])
        yn, _ = _rms(y)
        gg = mod_ref[M_GT_M:M_GT_M + 1, :] * vrow(V_G_POST_MIX)
        x1_ref[...] = x_ref[...] + yn * gg
        conva_ref[...] = conv_a.astype(BF16)
        xb2_ref[...] = xb2
        hh_ref[...] = hh
        ya_ref[...] = ya_b
        yb_ref[...] = yb_b
        pa_ref[...] = pa.astype(BF16)
        pb_ref[...] = pb.astype(BF16)
        m_ref[...] = m
        y_ref[...] = y.astype(BF16)

    tile = pl.BlockSpec((ts, d), lambda i: (i, 0))
    sd = lambda dt: jax.ShapeDtypeStruct((s, d), dt)
    return _pcall(
        body, name=name, grid=(s // ts,),
        out_shape=[sd(F32), sd(BF16), sd(F32), sd(F32), sd(BF16), sd(BF16), sd(BF16), sd(BF16), sd(BF16), sd(BF16),
                   sd(F32), sd(F32), sd(F32), sd(F32)],
        in_specs=[tile, pl.BlockSpec((ts, 7 * d), lambda i: (i, 0)), _full(mod.shape), _full(vec.shape),
                  _full(wg.shape), _full(w_a_out.shape), _full(w_b_out.shape), _full(w_o.shape)],
        out_specs=[tile] * 14,
        scratch_shapes=[pltpu.VMEM((SUBLANES, d), F32)] * 3 + [pltpu.VMEM((d // LANES, ts, LANES), F32)] * 2
                       + [pltpu.VMEM((ts // SCAN_GROUP, d), F32)],
        args=[x, proj, mod, vec, wg, w_a_out, w_b_out, w_o], carries=carries)


def _mlp_forward(x1, mod, vec, w_up, w_down, name, carries=()):
    s, d = x1.shape
    fq = w_up.shape[2]
    ts = min(TOKENS_MATMUL_TILE, s)

    def body(x_ref, mod_ref, vec_ref, wu_ref, wd_ref, x2_ref, h2_ref, up_ref, y2_ref):
        x = x_ref[...]
        xn, _ = _rms(x)
        gm = vec_ref[V_G_PRE_MLP:V_G_PRE_MLP + 1, :] * (1.0 + mod_ref[M_SC_F:M_SC_F + 1, :])
        h2 = (xn * gm + mod_ref[M_SH_F:M_SH_F + 1, :]).astype(BF16)
        h2_ref[...] = h2
        y2 = jnp.zeros((ts, d), F32)
        for qb in range(N_CHIP):
            up = _dot(h2, wu_ref[qb])
            up_ref[:, qb * fq:(qb + 1) * fq] = up.astype(BF16)
            ru = jnp.maximum(up, 0.0)
            y2 = y2 + _dot((ru * ru).astype(BF16), wd_ref[qb])
        y2_ref[...] = y2.astype(BF16)
        yn, _ = _rms(y2)
        gg = mod_ref[M_GT_F:M_GT_F + 1, :] * vec_ref[V_G_POST_MLP:V_G_POST_MLP + 1, :]
        x2_ref[...] = x + yn * gg

    tile = pl.BlockSpec((ts, d), lambda i: (i, 0))
    return _pcall(
        body, name=name, grid=(s // ts,),
        out_shape=[jax.ShapeDtypeStruct((s, d), F32), jax.ShapeDtypeStruct((s, d), BF16),
                   jax.ShapeDtypeStruct((s, N_CHIP * fq), BF16), jax.ShapeDtypeStruct((s, d), BF16)],
        in_specs=[tile, _full(mod.shape), _full(vec.shape), _full(w_up.shape), _full(w_down.shape)],
        out_specs=[tile, tile, pl.BlockSpec((ts, N_CHIP * fq), lambda i: (i, 0)), tile],
        args=[x1, mod, vec, w_up, w_down], carries=carries)


def _loss_head(xf, target, name):
    s, d = xf.shape
    ts = min(TOKENS_MATMUL_TILE, s)

    def body(x_ref, t_ref, dx_ref, loss_ref):
        @pl.when(pl.program_id(0) == 0)
        def _():
            loss_ref[...] = jnp.zeros_like(loss_ref)

        err = x_ref[...] - t_ref[...]
        dx_ref[...] = err * (1.0 / d)
        part = jnp.sum(jnp.sum(err * err, axis=1, keepdims=True), axis=0, keepdims=True) * (0.5 / d)
        loss_ref[...] = loss_ref[...] + part

    tile = pl.BlockSpec((ts, d), lambda i: (i, 0))
    return pl.pallas_call(
        body, name=name, grid=(s // ts,),
        out_shape=[jax.ShapeDtypeStruct((s, d), F32), jax.ShapeDtypeStruct((SUBLANES, 128), F32)],
        in_specs=[tile, tile], out_specs=[tile, _full((SUBLANES, 128))],
        compiler_params=_cparams(1),
    )(xf, target)


SB3_DSH, SB3_DSC, SB3_DGT, SB3_DG_PRE, SB3_DG_POST = range(5)
SB1_DSH, SB1_DSC, SB1_DG_PRE = range(3)
(SB2_DGT, SB2_DG_POST, SB2_DWA, SB2_DBA, SB2_DWB, SB2_DBB, SB2_DLAM, SB2_DBR, SB2_DBI) = (0, 1, 2, 5, 6, 10, 11, 12, 13)


def _mlp_backward(dx2, x1, y2, up, mod, vec, w_up, w_down, name, carries=()):
    s, d = dx2.shape
    fq = w_up.shape[2]
    ts = min(TOKENS_MIXER_TILE, s)
    n_t = s // ts

    def body(dx2_ref, x_ref, y2_ref, up_ref, mod_ref, vec_ref, wu_ref, wd_ref,
             dx1_ref, dy2_ref, dup_ref, act_ref, small_ref):
        i = pl.program_id(0)

        @pl.when(i == 0)
        def _():
            small_ref[...] = jnp.zeros_like(small_ref)

        dout = dx2_ref[...]
        y2n, ry = _rms(y2_ref[...].astype(F32))
        g_post = vec_ref[V_G_POST_MLP:V_G_POST_MLP + 1, :]
        gt = mod_ref[M_GT_F:M_GT_F + 1, :]
        dgg = _colsum(dout * y2n)
        dy2 = _rms_bwd(dout * (gt * g_post), y2n, ry).astype(BF16)
        dy2_ref[...] = dy2
        dh2 = jnp.zeros((ts, d), F32)
        for qb in range(N_CHIP):
            cols = slice(qb * fq, (qb + 1) * fq)
            dact = _dot_tb(dy2, wd_ref[qb])
            ru = jnp.maximum(up_ref[:, cols].astype(F32), 0.0)
            dup = (dact * (2.0 * ru)).astype(BF16)
            dup_ref[:, cols] = dup
            act_ref[:, cols] = (ru * ru).astype(BF16)
            dh2 = dh2 + _dot_tb(dup, wu_ref[qb])
        xn, r = _rms(x_ref[...])
        g_pre = vec_ref[V_G_PRE_MLP:V_G_PRE_MLP + 1, :]
        sc1 = 1.0 + mod_ref[M_SC_F:M_SC_F + 1, :]
        dsh = _colsum(dh2)
        dgm = _colsum(dh2 * xn)
        dx1_ref[...] = dout + _rms_bwd(dh2 * (g_pre * sc1), xn, r)
        small_ref[SB3_DSH:SB3_DSH + 1, :] += dsh
        small_ref[SB3_DSC:SB3_DSC + 1, :] += dgm
        small_ref[SB3_DGT:SB3_DGT + 1, :] += dgg

        @pl.when(i == n_t - 1)
        def _():
            dgm_t = small_ref[SB3_DSC:SB3_DSC + 1, :]
            dgg_t = small_ref[SB3_DGT:SB3_DGT + 1, :]
            small_ref[SB3_DSC:SB3_DSC + 1, :] = dgm_t * g_pre
            small_ref[SB3_DG_PRE:SB3_DG_PRE + 1, :] = dgm_t * sc1
            small_ref[SB3_DGT:SB3_DGT + 1, :] = dgg_t * g_post
            small_ref[SB3_DG_POST:SB3_DG_POST + 1, :] = dgg_t * gt

    tile = pl.BlockSpec((ts, d), lambda i: (i, 0))
    wide = pl.BlockSpec((ts, N_CHIP * fq), lambda i: (i, 0))
    return _pcall(
        body, name=name, grid=(n_t,),
        out_shape=[jax.ShapeDtypeStruct((s, d), F32), jax.ShapeDtypeStruct((s, d), BF16),
                   jax.ShapeDtypeStruct((s, N_CHIP * fq), BF16), jax.ShapeDtypeStruct((s, N_CHIP * fq), BF16),
                   jax.ShapeDtypeStruct((SUBLANES, d), F32)],
        in_specs=[tile, tile, tile, wide, _full(mod.shape), _full(vec.shape), _full(w_up.shape), _full(w_down.shape)],
        out_specs=[tile, tile, wide, wide, _full((SUBLANES, d))],
        args=[dx2, x1, y2, up, mod, vec, w_up, w_down], carries=carries)


def _mixer_backward(dx1, proj, conva, xb2s, hhs, pas, pbs, ys, rs_, gis, as_, mults, dgels, mod, vec, wg, w_a_out, w_b_out,
                    w_o, name, carries=()):
    s, d = dx1.shape
    n_head, bw, _ = wg.shape
    ts = min(TOKENS_MIXER_TILE, s)
    n_t = s // ts

    def body(dx1_ref, p_ref, conva_ref, xb2_ref, hh_ref, pa_ref, pb_ref, y_ref, r_ref, gi_ref, a_ref, mult_ref, dgel_ref,
             mod_ref, vec_ref, wg_ref, wa_ref, wb_ref, wo_ref,
             dp_ref, dy_ref, dpa_ref, dpb_ref, small_ref, dwg_ref,
             dconv_head, dxb2_head, a_head, g_head, a_buf, b_buf, c_buf):
        i = pl.program_id(0)

        @pl.when(i == 0)
        def _():
            small_ref[...] = jnp.zeros_like(small_ref)
            dwg_ref[...] = jnp.zeros_like(dwg_ref)
            dconv_head[...] = jnp.zeros_like(dconv_head)
            dxb2_head[...] = jnp.zeros_like(dxb2_head)
            a_head[...] = jnp.zeros_like(a_head)
            g_head[...] = jnp.zeros_like(g_head)

        def seg(k):
            return p_ref[:, k * d:(k + 1) * d].astype(F32)

        def vrow(k):
            return vec_ref[k:k + 1, :]

        def acc(row, val):
            small_ref[row:row + 1, :] += val

        dout = dx1_ref[...]
        yn, ry = _rms(y_ref[...].astype(F32))
        g_post = vrow(V_G_POST_MIX)
        gt = mod_ref[M_GT_M:M_GT_M + 1, :]
        acc(SB2_DGT, _colsum(dout * yn))
        dy = _rms_bwd(dout * (gt * g_post), yn, ry).astype(BF16)
        dy_ref[...] = dy
        dm = _dot_tb(dy, wo_ref[...])
        sa, sb = seg(5), seg(6)
        dpa = (dm * sa).astype(BF16)
        dpb = (dm * sb).astype(BF16)
        dpa_ref[...] = dpa
        dpb_ref[...] = dpb
        du_a = dm * pa_ref[...].astype(F32) * (sa * (1.0 - sa))
        du_b = dm * pb_ref[...].astype(F32) * (sb * (1.0 - sb))
        dp_ref[:, 5 * d:6 * d] = du_a.astype(BF16)
        dp_ref[:, 6 * d:7 * d] = du_b.astype(BF16)
        dy_a = _dot_tb(dpa, wa_ref[...])
        dy_b = _dot_tb(dpb, wb_ref[...])

        b_a, c_a, v_a = seg(0), seg(1), seg(2)
        dp_ref[:, 0:d] = (dy_a * conva_ref[...].astype(F32)).astype(BF16)
        dconv = dy_a * b_a
        nxt = dconv_head[...]
        d1 = _shift_up(dconv, 1, nxt)
        d2 = _shift_up(dconv, 2, nxt)
        dconv_head[...] = dconv[:SUBLANES]
        dcv = vrow(V_CONV_A_W + 2) * dconv + vrow(V_CONV_A_W + 1) * d1 + vrow(V_CONV_A_W) * d2
        cv = c_a * v_a
        acc(SB2_DWA + 2, _colsum(cv * dconv))
        acc(SB2_DWA + 1, _colsum(cv * d1))
        acc(SB2_DWA, _colsum(cv * d2))
        acc(SB2_DBA, _colsum(dconv))
        dp_ref[:, d:2 * d] = (dcv * v_a).astype(BF16)
        dp_ref[:, 2 * d:3 * d] = (dcv * c_a).astype(BF16)

        x_b, gel = seg(3), seg(4)
        hh = hh_ref[...]
        dp_ref[:, 4 * d:5 * d] = (dy_b * hh * dgel_ref[...].astype(F32)).astype(BF16)
        dhh = dy_b * gel
        xb2 = xb2_ref[...]
        r, gi, a, mult = r_ref[...], gi_ref[...], a_ref[...], mult_ref[...]
        sp = _softplus(-vrow(V_LAMBDA))
        a_next = _shift_up(a, 1, a_head[...])
        g = _scan_two_level(a_next, dhh, g_head[0:1, :], a_buf, b_buf, c_buf, reverse=True)
        a_head[...] = a[:SUBLANES]
        g_head[...] = g[:SUBLANES]
        gix = gi * xb2
        gm = g * mult
        dlog_a = g * (hh - mult * gix) - (g * gix) * (a * a / mult)
        dgi = gm * xb2
        dxb2 = gm * gi
        acc(SB2_DLAM, _colsum(dlog_a * r))
        dzr = dlog_a * ((-LRU_C) * sp) * (r * (1.0 - r))
        dzi = dgi * (gi * (1.0 - gi))
        acc(SB2_DBR, _colsum(dzr))
        acc(SB2_DBI, _colsum(dzi))
        xb2_b = xb2.astype(BF16)
        back = []
        for hd in range(n_head):
            cols = slice(hd * bw, (hd + 1) * bw)
            dz = jnp.concatenate([dzr[:, cols], dzi[:, cols]], axis=1).astype(BF16)
            back.append(_dot_tb(dz, wg_ref[hd]))
            dwg_ref[hd] += _dot_ta(xb2_b[:, cols], dz)
        dxb2 = dxb2 + jnp.concatenate(back, axis=1)
        nxt = dxb2_head[...]
        e1 = _shift_up(dxb2, 1, nxt)
        e2 = _shift_up(dxb2, 2, nxt)
        e3 = _shift_up(dxb2, 3, nxt)
        dxb2_head[...] = dxb2[:SUBLANES]
        dp_ref[:, 3 * d:4 * d] = (vrow(V_CONV_B_W + 3) * dxb2 + vrow(V_CONV_B_W + 2) * e1
                                  + vrow(V_CONV_B_W + 1) * e2 + vrow(V_CONV_B_W) * e3).astype(BF16)
        acc(SB2_DWB + 3, _colsum(x_b * dxb2))
        acc(SB2_DWB + 2, _colsum(x_b * e1))
        acc(SB2_DWB + 1, _colsum(x_b * e2))
        acc(SB2_DWB, _colsum(x_b * e3))
        acc(SB2_DBB, _colsum(dxb2))

        @pl.when(i == n_t - 1)
        def _():
            dgg_t = small_ref[SB2_DGT:SB2_DGT + 1, :]
            small_ref[SB2_DGT:SB2_DGT + 1, :] = dgg_t * g_post
            small_ref[SB2_DG_POST:SB2_DG_POST + 1, :] = dgg_t * gt
            lam = vrow(V_LAMBDA)
            small_ref[SB2_DLAM:SB2_DLAM + 1, :] = small_ref[SB2_DLAM:SB2_DLAM + 1, :] * (LRU_C * _sigmoid(-lam))

    rev = lambda i: (n_t - 1 - i, 0)
    tile = pl.BlockSpec((ts, d), rev)
    wide = pl.BlockSpec((ts, 7 * d), rev)
    sd = lambda dt: jax.ShapeDtypeStruct((s, d), dt)
    return _pcall(
        body, name=name, grid=(n_t,),
        out_shape=[jax.ShapeDtypeStruct((s, 7 * d), BF16), sd(BF16), sd(BF16), sd(BF16),
                   jax.ShapeDtypeStruct((2 * SUBLANES, d), F32), jax.ShapeDtypeStruct(wg.shape, F32)],
        in_specs=[tile, wide] + [tile] * 11 + [_full(mod.shape), _full(vec.shape),
                  _full(wg.shape), _full(w_a_out.shape), _full(w_b_out.shape), _full(w_o.shape)],
        out_specs=[wide, tile, tile, tile, _full((2 * SUBLANES, d)), _full(wg.shape)],
        scratch_shapes=[pltpu.VMEM((SUBLANES, d), F32)] * 4 + [pltpu.VMEM((d // LANES, ts, LANES), F32)] * 2
                       + [pltpu.VMEM((ts // SCAN_GROUP, d), F32)],
        args=[dx1, proj, conva, xb2s, hhs, pas, pbs, ys, rs_, gis, as_, mults, dgels, mod, vec, wg, w_a_out, w_b_out, w_o],
        carries=carries)


def _proj_backward(dproj, dx1, x, mod, vec, w_in, name, carries=()):
    s, d = x.shape
    nq = w_in.shape[2]
    ts = min(TOKENS_MATMUL_TILE, s)
    n_t = s // ts

    def body(dp_ref, dx1_ref, x_ref, mod_ref, vec_ref, w_ref, dx_ref, small_ref):
        i = pl.program_id(0)

        @pl.when(i == 0)
        def _():
            small_ref[...] = jnp.zeros_like(small_ref)

        dh = jnp.zeros((ts, d), F32)
        for qb in range(N_CHIP):
            dh = dh + _dot_tb(dp_ref[:, qb * nq:(qb + 1) * nq], w_ref[qb])
        xn, r = _rms(x_ref[...])
        g_pre = vec_ref[V_G_PRE_MIX:V_G_PRE_MIX + 1, :]
        sc1 = 1.0 + mod_ref[M_SC_M:M_SC_M + 1, :]
        dx_ref[...] = dx1_ref[...] + _rms_bwd(dh * (g_pre * sc1), xn, r)
        small_ref[SB1_DSH:SB1_DSH + 1, :] += _colsum(dh)
        small_ref[SB1_DSC:SB1_DSC + 1, :] += _colsum(dh * xn)

        @pl.when(i == n_t - 1)
        def _():
            dgm_t = small_ref[SB1_DSC:SB1_DSC + 1, :]
            small_ref[SB1_DSC:SB1_DSC + 1, :] = dgm_t * g_pre
            small_ref[SB1_DG_PRE:SB1_DG_PRE + 1, :] = dgm_t * sc1

    tile = pl.BlockSpec((ts, d), lambda i: (i, 0))
    return _pcall(
        body, name=name, grid=(n_t,),
        out_shape=[jax.ShapeDtypeStruct((s, d), F32), jax.ShapeDtypeStruct((SUBLANES, d), F32)],
        in_specs=[pl.BlockSpec((ts, N_CHIP * nq), lambda i: (i, 0)), tile, tile, _full(mod.shape), _full(vec.shape),
                  _full(w_in.shape)],
        out_specs=[tile, _full((SUBLANES, d))],
        args=[dproj, dx1, x, mod, vec, w_in], carries=carries)


def _weight_grad(a, b, name, col_blocks=1, tk=512, carries=()):
    s, k = a.shape
    n = b.shape[1]
    tn = n // col_blocks
    tk = min(tk, k)

    def body(a_ref, b_ref, o_ref):
        o_ref[0] = _dot_ta(a_ref[...], b_ref[...])

    (out,), carried = _pcall(
        body, name=name, grid=(col_blocks, k // tk),
        out_shape=[jax.ShapeDtypeStruct((col_blocks, k, tn), F32)],
        in_specs=[pl.BlockSpec((s, tk), lambda j, i: (0, i)), pl.BlockSpec((s, tn), lambda j, i: (0, j))],
        out_specs=[pl.BlockSpec((1, tk, tn), lambda j, i: (j, i, 0))],
        args=[a, b], carries=carries)
    return out, carried


def _adamw(w, g, m, v, name, copy_grad=False):
    shape = w.shape
    cols = shape[-1]
    rows = w.size // cols
    tr = _row_tile(rows, cols, target_bytes=1024 * 1024)
    c1 = 1.0 - ADAM_B1 ** ADAM_STEP
    c2 = 1.0 - ADAM_B2 ** ADAM_STEP
    n_out = 4 if copy_grad else 3

    def body(w_ref, g_ref, m_ref, v_ref, d_ref, nm_ref, nv_ref, *g_out):
        gv = g_ref[...]
        nm = ADAM_B1 * m_ref[...] + (1.0 - ADAM_B1) * gv
        nv = ADAM_B2 * v_ref[...] + (1.0 - ADAM_B2) * (gv * gv)
        nm_ref[...] = nm
        nv_ref[...] = nv
        d_ref[...] = (-ADAM_LR) * ((nm / c1) / (jnp.sqrt(nv / c2) + ADAM_EPS) + ADAM_WD * w_ref[...])
        if copy_grad:
            g_out[0][...] = gv

    spec = pl.BlockSpec((tr, cols), lambda i: (i, 0))
    outs = pl.pallas_call(
        body, name=name, grid=(rows // tr,),
        out_shape=[jax.ShapeDtypeStruct((rows, cols), F32)] * n_out,
        in_specs=[spec] * 4, out_specs=[spec] * n_out,
        compiler_params=_cparams(1),
    )(*(t.reshape(rows, cols) for t in (w, g, m, v)))
    return tuple(o.reshape(shape) for o in outs)


def kernel(x, c, w_mod, b_mod, g_pre_mix, g_post_mix, w_in, conv_a_w, conv_a_b, w_a_out, conv_b_w, conv_b_b, w_gate_r, b_gate_r, w_gate_i, b_gate_i, lru_lambda, w_b_out, w_o, g_pre_mlp, g_post_mlp, w_mlp_up, w_mlp_down, loss_target, m_w_mod, m_b_mod, m_g_pre_mix, m_g_post_mix, m_w_in, m_conv_a_w, m_conv_a_b, m_w_a_out, m_conv_b_w, m_conv_b_b, m_w_gate_r, m_b_gate_r, m_w_gate_i, m_b_gate_i, m_lru_lambda, m_w_b_out, m_w_o, m_g_pre_mlp, m_g_post_mlp, m_w_mlp_up, m_w_mlp_down, v_w_mod, v_b_mod, v_g_pre_mix, v_g_post_mix, v_w_in, v_conv_a_w, v_conv_a_b, v_w_a_out, v_conv_b_w, v_conv_b_b, v_w_gate_r, v_b_gate_r, v_w_gate_i, v_b_gate_i, v_lru_lambda, v_w_b_out, v_w_o, v_g_pre_mlp, v_g_post_mlp, v_w_mlp_up, v_w_mlp_down):
    weights = dict(w_mod=w_mod, b_mod=b_mod, g_pre_mix=g_pre_mix, g_post_mix=g_post_mix, w_in=w_in, conv_a_w=conv_a_w,
                   conv_a_b=conv_a_b, w_a_out=w_a_out, conv_b_w=conv_b_w, conv_b_b=conv_b_b, w_gate_r=w_gate_r,
                   b_gate_r=b_gate_r, w_gate_i=w_gate_i, b_gate_i=b_gate_i, lru_lambda=lru_lambda, w_b_out=w_b_out,
                   w_o=w_o, g_pre_mlp=g_pre_mlp, g_post_mlp=g_post_mlp, w_mlp_up=w_mlp_up, w_mlp_down=w_mlp_down)
    mom1 = dict(w_mod=m_w_mod, b_mod=m_b_mod, g_pre_mix=m_g_pre_mix, g_post_mix=m_g_post_mix, w_in=m_w_in,
                conv_a_w=m_conv_a_w, conv_a_b=m_conv_a_b, w_a_out=m_w_a_out, conv_b_w=m_conv_b_w, conv_b_b=m_conv_b_b,
                w_gate_r=m_w_gate_r, b_gate_r=m_b_gate_r, w_gate_i=m_w_gate_i, b_gate_i=m_b_gate_i,
                lru_lambda=m_lru_lambda, w_b_out=m_w_b_out, w_o=m_w_o, g_pre_mlp=m_g_pre_mlp, g_post_mlp=m_g_post_mlp,
                w_mlp_up=m_w_mlp_up, w_mlp_down=m_w_mlp_down)
    mom2 = dict(w_mod=v_w_mod, b_mod=v_b_mod, g_pre_mix=v_g_pre_mix, g_post_mix=v_g_post_mix, w_in=v_w_in,
                conv_a_w=v_conv_a_w, conv_a_b=v_conv_a_b, w_a_out=v_w_a_out, conv_b_w=v_conv_b_w, conv_b_b=v_conv_b_b,
                w_gate_r=v_w_gate_r, b_gate_r=v_b_gate_r, w_gate_i=v_w_gate_i, b_gate_i=v_b_gate_i,
                lru_lambda=v_lru_lambda, w_b_out=v_w_b_out, w_o=v_w_o, g_pre_mlp=v_g_pre_mlp, g_post_mlp=v_g_post_mlp,
                w_mlp_up=v_w_mlp_up, w_mlp_down=v_w_mlp_down)
    names = list(weights)

    n_layer = w_in.shape[0]
    s, d = x.shape[1], x.shape[2]
    n_head, bw = w_gate_r.shape[1], w_gate_r.shape[2]
    dq = d // N_CHIP
    mq = w_mod.shape[2]
    n_mod = (N_CHIP * mq) // d
    ka, kb = conv_a_w.shape[1], conv_b_w.shape[1]

    mx, my, mc = _place()
    q_me = 2 * mx + my
    q_arr = jnp.reshape(q_me, (1,)).astype(jnp.int32)

    n_conv_rows = n_layer * (ka + kb)
    conv_blk = -(-n_conv_rows // SUBLANES) * SUBLANES
    blk_rows = SUBLANES + conv_blk
    conv_rows = jnp.concatenate([jnp.concatenate([conv_a_w[l], conv_b_w[l]], axis=0) for l in range(n_layer)], axis=0)
    conv_rows = jnp.pad(conv_rows, ((0, conv_blk - n_conv_rows), (0, d - dq)))
    c_blk = jnp.pad(c, ((0, SUBLANES - 1), (0, 0)))
    gathered1 = _all_gather_small(jnp.concatenate([c_blk, conv_rows], axis=0), "gather_c_conv").reshape(N_DEV, blk_rows, d)
    c_all = gathered1[:, 0, :]
    conv_full = jnp.concatenate([gathered1[2 * qb, SUBLANES:SUBLANES + n_conv_rows, :dq] for qb in range(N_CHIP)], axis=1)

    b_mod_shard = lax.dynamic_slice_in_dim(b_mod, q_me * mq, mq, axis=1)
    mod_part = _mod_forward(c_all, w_mod, b_mod_shard, "mod_forward")
    gathered2 = _all_gather_small(mod_part, "gather_mod").reshape(N_DEV, n_layer, N_DEV, mq)
    me = 4 * mx + 2 * my + mc
    mod_rows = jnp.concatenate(
        [lax.dynamic_index_in_dim(gathered2[2 * qb], me, axis=1, keepdims=False) for qb in range(N_CHIP)], axis=1)
    mods = [jnp.pad(mod_rows[l].reshape(n_mod, d), ((0, SUBLANES - n_mod), (0, 0))) for l in range(n_layer)]

    vecs = []
    for l in range(n_layer):
        base = l * (ka + kb)
        rows = [g_pre_mix[l], g_post_mix[l], conv_a_b[l], conv_b_b[l], b_gate_r[l], b_gate_i[l], lru_lambda[l],
                g_pre_mlp[l], g_post_mlp[l]]
        vecs.append(jnp.concatenate([jnp.stack(rows, axis=0), conv_full[base:base + ka + kb]], axis=0))

    big_names = ["w_in", "w_a_out", "w_b_out", "w_o", "w_mlp_up", "w_mlp_down"]
    groups = [["w_in"], ["w_a_out", "w_b_out", "w_o"], ["w_mlp_up", "w_mlp_down"]]
    placed = {(nm, l): _cast_place(weights[nm], l, q_arr, f"cast_place_{nm}_{l}") for l in range(n_layer) for nm in big_names}
    wfull = [dict() for _ in range(n_layer)]
    riders = {}
    for l in range(n_layer):
        riders.setdefault(3 * l - 2 if l else -1, []).append(("w_in", l))
        for nm in groups[1] + ["w_mlp_up"]:
            riders.setdefault(3 * l - 1 if l else 0, []).append((nm, l))
        riders.setdefault(3 * l if l else 1, []).append(("w_mlp_down", l))

    def gather_carry(call):
        keys = riders.get(call, [])
        return [_gather_carry([placed[k] for k in keys])] if keys else []

    def gathered(call, carried):
        for (nm, l), w in zip(riders.get(call, []), carried[0] if carried else []):
            wfull[l][nm] = w.reshape(d, d) if nm in groups[1] else w

    gathered(-1, _run_carries(gather_carry(-1), "gather_first"))
    wgs = [jnp.concatenate([w_gate_r[l], w_gate_i[l]], axis=-1).astype(BF16) for l in range(n_layer)]

    xs = x[0]
    saved = []
    for l in range(n_layer):
        wl = wfull[l]
        (h, proj, dgel), carried = _norm_proj(xs, mods[l], vecs[l], wl["w_in"], f"norm_proj_{l}", gather_carry(3 * l))
        gathered(3 * l, carried)
        (x1, conva, xb2, hh, ya, yb, pa, pb, mm, yy, gr, ggi, ga, gmult), carried = _mixer_forward(
            xs, proj, mods[l], vecs[l], wgs[l], wl["w_a_out"], wl["w_b_out"], wl["w_o"], f"mixer_forward_{l}",
            gather_carry(3 * l + 1))
        gathered(3 * l + 1, carried)
        (x2, h2, up, y2), carried = _mlp_forward(x1, mods[l], vecs[l], wl["w_mlp_up"], wl["w_mlp_down"],
                                                 f"mlp_forward_{l}", gather_carry(3 * l + 2))
        gathered(3 * l + 2, carried)
        saved.append(dict(x=xs, h=h, proj=proj, x1=x1, conva=conva, xb2=xb2, hh=hh, ya=ya, yb=yb, pa=pa, pb=pb, m=mm,
                          y=yy, r=gr, gi=ggi, a=ga, mult=gmult, dgel=dgel, h2=h2, up=up, y2=y2))
        xs = x2
    dxs, loss_tile = _loss_head(xs, loss_target[0], "loss_head")
    loss = lax.psum(loss_tile[0, 0], ("x", "y", "c"))

    chips_q = [q_me ^ 2, q_me ^ 1, q_me ^ 3]
    pf = jnp.stack([mc, q_me] + chips_q).astype(jnp.int32)
    rs = dict(grad={}, landed={}, to_send={}, from_chips={}, out={})
    to_exchange, to_scatter, to_join, to_gather = [], [], [], []
    small_own, small_all = {}, {}

    def ride(call, what, name=None):
        ex = list(to_exchange) if "x" in what else []
        sc = list(to_scatter) if "s" in what else []
        ga = list(to_gather) if "g" in what else []
        jn = []
        for key in (to_join if "j" in what else []):
            if key[0] not in [k[0] for k in jn]:
                jn.append(key)
        carries = []
        if ex:
            carries.append(_exchange_carry([rs["grad"][k] for k in ex]))
        if sc:
            carries.append(_scatter_carry([rs["to_send"][k] for k in sc]))
        if jn:
            carries.append(_join_carry([rs["out"][k[0]] for k in jn], [k[1] for k in jn]))
        if ga:
            carries.append(_allgather_carry([small_own[k] for k in ga]))
        if call is None:
            carried = _run_carries(carries, name) if carries else []
            res = None
        else:
            res, carried = call(carries)
        carried = list(carried)
        if ex:
            for k, ld in zip(ex, carried.pop(0)):
                to_exchange.remove(k)
                rs["landed"][k] = ld
                rs["to_send"][k] = _add_sibling_half(rs["grad"][k], ld, pf, f"rs_add_sibling_{k[0]}_{k[1]}")
                to_scatter.append(k)
        if sc:
            for k, fc in zip(sc, carried.pop(0)):
                to_scatter.remove(k)
                rs["out"][k[0]] = _add_chips(rs["grad"][k], rs["landed"][k], fc, pf, rs["out"].get(k[0]), k[1], n_layer,
                                             f"rs_add_chips_{k[0]}_{k[1]}")
                to_join.append(k)
        if jn:
            for k, o in zip(jn, carried.pop(0)):
                to_join.remove(k)
                rs["out"][k[0]] = o
        if ga:
            for k, o in zip(ga, carried.pop(0)):
                to_gather.remove(k)
                small_all[k] = o
        return res

    def gather_small(key, parts):
        small_own[key] = parts[0] if len(parts) == 1 else jnp.concatenate(parts, axis=0)
        to_gather.append(key)

    def ready(nm, l, g):
        rs["grad"][(nm, l)] = g
        to_exchange.append((nm, l))

    rowblk = lambda t: t.reshape(N_CHIP, t.shape[1] // N_CHIP, t.shape[2])
    small1_prev = None
    for l in reversed(range(n_layer)):
        wl, sv = wfull[l], saved[l]
        dx1, dy2, dup, act, small3 = ride(lambda cr: _mlp_backward(
            dxs, sv["x1"], sv["y2"], sv["up"], mods[l], vecs[l], wl["w_mlp_up"], wl["w_mlp_down"], f"mlp_backward_{l}", cr), "xsjg")
        ready("w_mlp_up", l, _weight_grad(sv["h2"], dup, f"grad_w_mlp_up_{l}", col_blocks=N_CHIP)[0])
        g_down = ride(lambda cr: _weight_grad(act, dy2, f"grad_w_mlp_down_{l}", carries=cr), "x")
        ready("w_mlp_down", l, rowblk(g_down))
        ride(None, "x", f"rs_exchange_down_{l}")
        dproj, dy, dpa, dpb, small2, dwg = ride(lambda cr: _mixer_backward(
            dx1, sv["proj"], sv["conva"], sv["xb2"], sv["hh"], sv["pa"], sv["pb"], sv["y"],
            sv["r"], sv["gi"], sv["a"], sv["mult"], sv["dgel"], mods[l], vecs[l], wgs[l],
            wl["w_a_out"], wl["w_b_out"], wl["w_o"], f"mixer_backward_{l}", cr), "xsjg")
        gather_small(("late", l, "s"), ([small1_prev] if small1_prev is not None else []) + [small2, small3])
        gather_small(("late", l, "w"), [dwg.reshape(2 * bw, d).astype(BF16)])
        ready("w_a_out", l, rowblk(_weight_grad(sv["ya"], dpa, f"grad_w_a_out_{l}")[0]))
        g_b = ride(lambda cr: _weight_grad(sv["yb"], dpb, f"grad_w_b_out_{l}", carries=cr), "x")
        ready("w_b_out", l, rowblk(g_b))
        g_o = ride(lambda cr: _weight_grad(sv["m"], dy, f"grad_w_o_{l}", carries=cr), "x")
        ready("w_o", l, rowblk(g_o))
        g_in = ride(lambda cr: _weight_grad(sv["h"], dproj, f"grad_w_in_{l}", col_blocks=N_CHIP, carries=cr), "xsg")
        ready("w_in", l, g_in)
        if l == 0:
            ride(None, "x", "rs_exchange_last")
        dxs, small1_prev = ride(lambda cr: _proj_backward(dproj, dx1, sv["x"], mods[l], vecs[l], wl["w_in"],
                                                          f"proj_backward_{l}", cr), "xsjg")
    grad_x = dxs[None]
    gather_small(("last", 0, "s"), [small1_prev])

    tail = 0
    while to_exchange or to_scatter or to_join or to_gather:
        ride(None, "xsjg", f"rs_tail_{tail}")
        tail += 1

    me_dev = 4 * mx + 2 * my + mc
    me_arr = jnp.reshape(me_dev, (1,)).astype(jnp.int32)
    is_me = (jnp.arange(N_DEV) == me_dev)[:, None, None]
    sums = {k: _sum_devices(small_all[k], small_own[k], me_arr, f"sum_small_{k[0]}_{k[1]}_{k[2]}") for k in small_own}

    def rows_of(l, part):
        if part == 0:
            return (("late", l - 1, "s"), 0) if l >= 1 else (("last", 0, "s"), 0)
        if part == 3:
            return ("late", l, "w"), 0
        base = SUBLANES if l < n_layer - 1 else 0
        return ("late", l, "s"), base + (0, 0, 2 * SUBLANES)[part]

    def summed(l, part, row, n_rows=1):
        key, base = rows_of(l, part)
        return sums[key][base + row:base + row + n_rows]

    def per_device(l, part, row):
        key, base = rows_of(l, part)
        own = small_own[key]
        others = small_all[key].reshape((N_DEV,) + own.shape)
        return jnp.where(is_me, own[None, base + row:base + row + 1], others[:, base + row:base + row + 1])

    mod_rows = [(0, SB1_DSH), (0, SB1_DSC), (1, SB2_DGT), (2, SB3_DSH), (2, SB3_DSC), (2, SB3_DGT)]
    dmod_all = jnp.stack([jnp.concatenate([per_device(l, p, r)[:, 0, :] for p, r in mod_rows], axis=1)
                          for l in range(n_layer)], axis=0)
    o1, o2, o3, o4 = 0, SUBLANES, 3 * SUBLANES, 4 * SUBLANES
    small_sum = jnp.stack([jnp.concatenate([summed(l, 0, 0, SUBLANES), summed(l, 1, 0, 2 * SUBLANES),
                                            summed(l, 2, 0, SUBLANES), summed(l, 3, 0, 2 * bw)], axis=0)
                           for l in range(n_layer)], axis=0)
    mod_rows_of = [o1 + SB1_DSH, o1 + SB1_DSC, o2 + SB2_DGT, o3 + SB3_DSH, o3 + SB3_DSC, o3 + SB3_DGT]
    grads = {}
    grads["w_mod"] = _mod_backward(c_all.T, lax.dynamic_slice_in_dim(dmod_all, q_me * mq, mq, axis=2), "mod_backward")
    grads["b_mod"] = jnp.concatenate([small_sum[:, k, :] for k in mod_rows_of], axis=1)
    grads["g_pre_mix"] = small_sum[:, o1 + SB1_DG_PRE]
    grads["g_post_mix"] = small_sum[:, o2 + SB2_DG_POST]
    grads["conv_a_w"] = lax.dynamic_slice_in_dim(small_sum[:, o2 + SB2_DWA:o2 + SB2_DWA + ka], q_me * dq, dq, axis=2)
    grads["conv_a_b"] = small_sum[:, o2 + SB2_DBA]
    grads["conv_b_w"] = lax.dynamic_slice_in_dim(small_sum[:, o2 + SB2_DWB:o2 + SB2_DWB + kb], q_me * dq, dq, axis=2)
    grads["conv_b_b"] = small_sum[:, o2 + SB2_DBB]
    grads["lru_lambda"] = small_sum[:, o2 + SB2_DLAM]
    grads["b_gate_r"] = small_sum[:, o2 + SB2_DBR]
    grads["b_gate_i"] = small_sum[:, o2 + SB2_DBI]
    grads["g_pre_mlp"] = small_sum[:, o3 + SB3_DG_PRE]
    grads["g_post_mlp"] = small_sum[:, o3 + SB3_DG_POST]
    dwg_sum = small_sum[:, o4:].reshape(n_layer, n_head, bw, 2 * bw)
    grads["w_gate_r"] = dwg_sum[..., :bw]
    grads["w_gate_i"] = dwg_sum[..., bw:]

    for nm in big_names:
        grads[nm] = rs["out"][nm].reshape(weights[nm].shape)

    deltas, new_m, new_v = {}, {}, {}
    for nm in names:
        res = _adamw(weights[nm], grads[nm], mom1[nm], mom2[nm], f"adamw_{nm}", copy_grad=nm in big_names)
        deltas[nm], new_m[nm], new_v[nm] = res[:3]
        if nm in big_names:
            grads[nm] = res[3]
    return (loss, grad_x, *[grads[nm] for nm in names], *[deltas[nm] for nm in names],
            *[new_m[nm] for nm in names], *[new_v[nm] for nm in names])
```

```python
import jax
import jax.numpy as jnp
from jax import lax
from jax.experimental import pallas as pl
from jax.experimental.pallas import tpu as pltpu

F32 = jnp.float32
BF16 = jnp.bfloat16
MESH = pl.DeviceIdType.MESH

EPS = 1e-6
LRU_C = 8.0
N_CHIP = 4
N_DEV = 8
ADAM_LR = 0.001
ADAM_B1 = 0.9
ADAM_B2 = 0.999
ADAM_EPS = 1e-08
ADAM_WD = 0.01
ADAM_STEP = 10

VMEM_LIMIT_BYTES = 56 * 1024 * 1024
SUBLANES = 8
LANES = 128
TOKENS_MATMUL_TILE = 512
TOKENS_MIXER_TILE = 256
GELU_K0 = 0.7978845608028654
GELU_K1 = 0.044715

V_G_PRE_MIX, V_G_POST_MIX, V_CONV_A_B, V_CONV_B_B, V_B_GATE_R, V_B_GATE_I, V_LAMBDA, V_G_PRE_MLP, V_G_POST_MLP = range(9)
V_CONV_A_W = 9
V_CONV_B_W = 12
M_SH_M, M_SC_M, M_GT_M, M_SH_F, M_SC_F, M_GT_F = range(6)


def _cparams(n_grid=0):
    sem = ("arbitrary",) * n_grid if n_grid else None
    return pltpu.CompilerParams(dimension_semantics=sem, vmem_limit_bytes=VMEM_LIMIT_BYTES)


def _full(shape):
    return pl.BlockSpec(shape, lambda *_: (0,) * len(shape))


def _dot(a, b):
    return jnp.dot(a, b, preferred_element_type=F32)


def _dot_tb(a, b):
    return lax.dot_general(a, b, (((1,), (1,)), ((), ())), preferred_element_type=F32)


def _dot_ta(a, b):
    return lax.dot_general(a, b, (((0,), (0,)), ((), ())), preferred_element_type=F32)


def _sigmoid(x):
    return 1.0 / (1.0 + jnp.exp(-x))


def _softplus(x):
    return jnp.maximum(x, 0.0) + jnp.log1p(jnp.exp(-jnp.abs(x)))


def _neg_expm1(x):
    series = -x * (1.0 + 0.5 * x * (1.0 + (x / 3.0) * (1.0 + 0.25 * x)))
    return jnp.where(x > -1e-2, series, 1.0 - jnp.exp(x))


def _gelu_and_grad(x):
    x2 = x * x
    s = _sigmoid(x * (2.0 * GELU_K0 + (2.0 * GELU_K0 * GELU_K1) * x2))
    gel = x * s
    return gel, s + gel * (1.0 - s) * (2.0 * GELU_K0 + (6.0 * GELU_K0 * GELU_K1) * x2)


def _rms(x):
    r = lax.rsqrt(jnp.mean(x * x, axis=-1, keepdims=True) + EPS)
    return x * r, r


def _rms_bwd(dxn, xn, r):
    return r * (dxn - xn * jnp.mean(dxn * xn, axis=-1, keepdims=True))


def _colsum(x):
    return jnp.sum(x, axis=0, keepdims=True)


def _rows(t, w):
    return lax.broadcasted_iota(jnp.int32, (t, w), 0)


def _shift_down(x, k, prev8):
    t, w = x.shape
    rolled = pltpu.roll(x, k, 0)
    head = jnp.where(_rows(SUBLANES, w) < k, pltpu.roll(prev8, k, 0), rolled[:SUBLANES])
    return jnp.concatenate([head, rolled[SUBLANES:]], axis=0)


def _shift_up(x, k, next8):
    t, w = x.shape
    rolled = pltpu.roll(x, t - k, 0)
    tail = jnp.where(_rows(SUBLANES, w) >= SUBLANES - k, pltpu.roll(next8, SUBLANES - k, 0), rolled[t - SUBLANES:])
    return jnp.concatenate([rolled[:t - SUBLANES], tail], axis=0)


SCAN_GROUP = 16


def _scan_steps(a, b, group, reverse):
    t, w = a.shape
    pos = _rows(t, w) & (group - 1)
    s = 1
    while s < group:
        keep = (pos < group - s) if reverse else (pos >= s)
        shift = (t - s) if reverse else s
        b = b + a * jnp.where(keep, pltpu.roll(b, shift, 0), 0.0)
        a = a * jnp.where(keep, pltpu.roll(a, shift, 0), 1.0)
        s *= 2
    return b, a


def _scan_two_level(a, b, carry_row, a_buf, b_buf, c_buf, reverse):
    t, w = a.shape
    grp = SCAN_GROUP
    n_grp = t // grp
    h_loc, a_cum = _scan_steps(a, b, grp, reverse)
    end = 0 if reverse else grp - 1
    a_end, h_end = [], []
    for j in range(w // LANES):
        a_buf[j] = a_cum[:, j * LANES:(j + 1) * LANES]
        b_buf[j] = h_loc[:, j * LANES:(j + 1) * LANES]
        a_end.append(a_buf[j, pl.ds(end, n_grp, stride=grp), :])
        h_end.append(b_buf[j, pl.ds(end, n_grp, stride=grp), :])
    a_end = jnp.concatenate(a_end, axis=1)
    h_end = jnp.concatenate(h_end, axis=1)
    h_grp, a_grp = _scan_steps(a_end, h_end, n_grp, reverse)
    h_grp = h_grp + a_grp * carry_row
    rows = _rows(n_grp, w)
    if reverse:
        entering = jnp.where(rows == n_grp - 1, carry_row, pltpu.roll(h_grp, n_grp - 1, 0))
    else:
        entering = jnp.where(rows == 0, carry_row, pltpu.roll(h_grp, 1, 0))
    c_buf[...] = entering
    out = [h_loc[g * grp:(g + 1) * grp] + a_cum[g * grp:(g + 1) * grp] * c_buf[g:g + 1, :] for g in range(n_grp)]
    return jnp.concatenate(out, axis=0)


def _row_tile(rows, cols, itemsize=4, target_bytes=2 * 1024 * 1024):
    if rows * cols * itemsize <= target_bytes or rows % SUBLANES:
        return rows
    t = max(SUBLANES, (target_bytes // (cols * itemsize)) // SUBLANES * SUBLANES)
    while rows % t:
        t -= SUBLANES
    return t


def _place():
    return lax.axis_index("x"), lax.axis_index("y"), lax.axis_index("c")


def _other_chips(x, y):
    chips = [(1 - x, y), (x, 1 - y), (1 - x, 1 - y)]
    return chips, [2 * cx + cy for cx, cy in chips]


def _all_gather_small(block, name):
    m_per, n = block.shape

    def body(x_ref, out_ref, send_sems, recv_sems, local_sem):
        x, y, c = _place()
        me, sibling = (x, y, c), (x, y, 1 - c)
        chips, _ = _other_chips(x, y)

        def rows(px, py, pc):
            return out_ref.at[pl.ds((4 * px + 2 * py + pc) * m_per, m_per), :]

        def copy(k, blk, to, src=None):
            return pltpu.make_async_remote_copy(
                src_ref=rows(*blk) if src is None else src, dst_ref=rows(*blk),
                send_sem=send_sems.at[k], recv_sem=recv_sems.at[k], device_id=to, device_id_type=MESH)

        mine = pltpu.make_async_copy(x_ref, rows(*me), local_sem)
        mine.start()
        first = [copy(0, me, sibling, src=x_ref)]
        first += [copy(1 + j, me, (*chip, c), src=x_ref) for j, chip in enumerate(chips)]
        for cp in first:
            cp.start()
        passed = [copy(4 + j, (*chip, c), sibling) for j, chip in enumerate(chips)]
        for j, chip in enumerate(chips):
            copy(1 + j, (*chip, c), me).wait_recv()
            passed[j].start()
        copy(0, sibling, me).wait_recv()
        for j, chip in enumerate(chips):
            copy(4 + j, (*chip, 1 - c), me).wait_recv()
        for cp in first + passed:
            cp.wait_send()
        mine.wait()

    return pl.pallas_call(
        body, name=name,
        out_shape=jax.ShapeDtypeStruct((N_DEV * m_per, n), block.dtype),
        in_specs=[pl.BlockSpec(memory_space=pltpu.VMEM)],
        out_specs=pl.BlockSpec(memory_space=pltpu.VMEM),
        scratch_shapes=[pltpu.SemaphoreType.DMA((7,)), pltpu.SemaphoreType.DMA((7,)), pltpu.SemaphoreType.DMA],
        compiler_params=pltpu.CompilerParams(vmem_limit_bytes=VMEM_LIMIT_BYTES),
    )(block)


def _cast_place(w, layer, q_arr, name):
    _, r, cols = w.shape
    tr = _row_tile(r, cols)

    def body(q_ref, w_ref, o_ref):
        o_ref[...] = w_ref[...].astype(BF16)

    return pl.pallas_call(
        body, name=name,
        out_shape=jax.ShapeDtypeStruct((N_CHIP, r, cols), BF16),
        grid_spec=pltpu.PrefetchScalarGridSpec(
            num_scalar_prefetch=1, grid=(r // tr,),
            in_specs=[pl.BlockSpec((1, tr, cols), lambda i, q_ref: (layer, i, 0))],
            out_specs=pl.BlockSpec((1, tr, cols), lambda i, q_ref: (q_ref[0], i, 0))),
        compiler_params=_cparams(1),
    )(q_arr, w)


class _Carry:
    def __init__(self, ins, out_shapes, aliases, sem_shapes, start, finish, mid=None):
        self.ins, self.out_shapes, self.aliases, self.sem_shapes = list(ins), list(out_shapes), dict(aliases), list(sem_shapes)
        self.start, self.mid, self.finish = start, mid, finish


def _pcall(body, *, name, grid, in_specs, out_specs, out_shape, args, scratch_shapes=(), carries=(), mid_frac=0.85,
           prefetch=()):
    in_specs, out_specs, out_shape = list(in_specs), list(out_specs), list(out_shape)
    scratch_shapes, args = list(scratch_shapes), list(args)
    n_in, n_out, n_scr, n_pre = len(in_specs), len(out_shape), len(scratch_shapes), len(prefetch)
    steps = 1
    for g in grid:
        steps *= g
    mid_step = min(steps - 1, int(steps * mid_frac))
    any_spec = pl.BlockSpec(memory_space=pl.ANY)
    aliases = {}
    spans = []
    for cr in carries:
        spans.append((len(args), len(out_shape), len(scratch_shapes)))
        for a, b in cr.aliases.items():
            aliases[n_pre + len(args) + a] = len(out_shape) + b
        args += cr.ins
        in_specs += [any_spec] * len(cr.ins)
        out_shape += cr.out_shapes
        out_specs += [any_spec] * len(cr.out_shapes)
        scratch_shapes += cr.sem_shapes
    n_all_in = len(args)
    n_all_out = len(out_shape)

    def wrapped(*refs):
        pre, refs = refs[:n_pre], refs[n_pre:]
        ins, outs, scr = refs[:n_all_in], refs[n_all_in:n_all_in + n_all_out], refs[n_all_in + n_all_out:]
        parts = [(cr, ins[a:a + len(cr.ins)], outs[b:b + len(cr.out_shapes)], scr[s:s + len(cr.sem_shapes)])
                 for cr, (a, b, s) in zip(carries, spans)]
        lin = 0
        for ax, g in enumerate(grid):
            lin = lin * g + pl.program_id(ax)

        def at(step, fn):
            if steps == 1:
                fn()
            else:
                pl.when(lin == step)(fn)

        def start_all():
            for cr, ci, co, cs in parts:
                cr.start(ci, co, cs)

        def mid_all():
            for cr, ci, co, cs in parts:
                if cr.mid is not None:
                    cr.mid(ci, co, cs)

        def finish_all():
            for cr, ci, co, cs in parts:
                cr.finish(ci, co, cs)

        if parts:
            at(0, start_all)
        body(*pre, *ins[:n_in], *outs[:n_out], *scr[:n_scr])
        if parts:
            at(mid_step, mid_all)
            at(steps - 1, finish_all)

    if n_pre:
        res = pl.pallas_call(
            wrapped, name=name, out_shape=out_shape,
            grid_spec=pltpu.PrefetchScalarGridSpec(num_scalar_prefetch=n_pre, grid=tuple(grid), in_specs=in_specs,
                                                   out_specs=out_specs, scratch_shapes=scratch_shapes),
            input_output_aliases=aliases, compiler_params=_cparams(len(grid)),
        )(*prefetch, *args)
    else:
        res = pl.pallas_call(
            wrapped, name=name, grid=tuple(grid), out_shape=out_shape, in_specs=in_specs, out_specs=out_specs,
            scratch_shapes=scratch_shapes, input_output_aliases=aliases, compiler_params=_cparams(len(grid)),
        )(*args)
    res = list(res)
    return res[:n_out], [res[b:b + len(cr.out_shapes)] for cr, (_, b, _) in zip(carries, spans)]


def _run_carries(carries, name):
    return _pcall(lambda: None, name=name, grid=(), in_specs=[], out_specs=[], out_shape=[], args=[], carries=carries)[1]


CAST_STEPS = 8


def _cast_place_all(shards, q_arr, name, carries=()):
    n = len(shards)

    def body(q_ref, *refs):
        for k in range(n):
            refs[n + k][...] = refs[k][...].astype(BF16)

    def spec_in(k):
        w, layer = shards[k]
        return pl.BlockSpec((1, w.shape[1] // CAST_STEPS, w.shape[2]), lambda i, q_ref: (layer, i, 0))

    def spec_out(k):
        w, _ = shards[k]
        return pl.BlockSpec((1, w.shape[1] // CAST_STEPS, w.shape[2]), lambda i, q_ref: (q_ref[0], i, 0))

    return _pcall(
        body, name=name, grid=(CAST_STEPS,),
        out_shape=[jax.ShapeDtypeStruct((N_CHIP,) + w.shape[1:], BF16) for w, _ in shards],
        in_specs=[spec_in(k) for k in range(n)], out_specs=[spec_out(k) for k in range(n)],
        args=[w for w, _ in shards], carries=carries, prefetch=[q_arr])


def _gather_carry(bufs):
    n = len(bufs)

    def copies(o_refs, sems):
        send_sems, recv_sems = sems
        x, y, c = _place()
        q = 2 * x + y
        sibling = (x, y, 1 - c)
        chips, qs = _other_chips(x, y)

        def half(w, shard, pc):
            rh = bufs[w].shape[1] // 2
            return o_refs[w].at[shard, pl.ds(pc * rh, rh), :]

        def over_ici(w, j, shard):
            blk = half(w, shard, c)
            return pltpu.make_async_remote_copy(
                src_ref=blk, dst_ref=blk, send_sem=send_sems.at[w, j], recv_sem=recv_sems.at[w, j],
                device_id=(*chips[j], c), device_id_type=MESH)

        def to_sibling(w, j, pc):
            blk = half(w, qs[j], pc)
            return pltpu.make_async_remote_copy(
                src_ref=blk, dst_ref=blk, send_sem=send_sems.at[w, 3 + j], recv_sem=recv_sems.at[w, 3 + j],
                device_id=sibling, device_id_type=MESH)

        return q, c, qs, over_ici, to_sibling

    pairs = [(w, j) for w in range(n) for j in range(3)]

    def start(i_refs, o_refs, sems):
        q, _, _, over_ici, _ = copies(o_refs, sems)
        for w, j in pairs:
            over_ici(w, j, q).start()

    def mid(i_refs, o_refs, sems):
        _, c, qs, over_ici, to_sibling = copies(o_refs, sems)
        for w, j in pairs:
            over_ici(w, j, qs[j]).wait_recv()
            to_sibling(w, j, c).start()

    def finish(i_refs, o_refs, sems):
        q, c, _, over_ici, to_sibling = copies(o_refs, sems)
        for w, j in pairs:
            to_sibling(w, j, 1 - c).wait_recv()
        for w, j in pairs:
            over_ici(w, j, q).wait_send()
            to_sibling(w, j, c).wait_send()

    return _Carry(bufs, [jax.ShapeDtypeStruct(b.shape, b.dtype) for b in bufs], {w: w for w in range(n)},
                  [pltpu.SemaphoreType.DMA((n, 6)), pltpu.SemaphoreType.DMA((n, 6))], start, finish, mid)


def _exchange_carry(grads):
    n = len(grads)

    def copies(g_refs, l_refs, sems):
        send_sems, recv_sems = sems
        x, y, c = _place()
        out = []
        for w in range(n):
            rh = grads[w].shape[1] // 2
            out.append(pltpu.make_async_remote_copy(
                src_ref=g_refs[w].at[:, pl.ds((1 - c) * rh, rh), :], dst_ref=l_refs[w],
                send_sem=send_sems.at[w], recv_sem=recv_sems.at[w], device_id=(x, y, 1 - c), device_id_type=MESH))
        return out

    def start(g_refs, l_refs, sems):
        for cp in copies(g_refs, l_refs, sems):
            cp.start()

    def finish(g_refs, l_refs, sems):
        for cp in copies(g_refs, l_refs, sems):
            cp.wait()

    return _Carry(grads, [jax.ShapeDtypeStruct((N_CHIP, g.shape[1] // 2, g.shape[2]), g.dtype) for g in grads], {},
                  [pltpu.SemaphoreType.DMA((n,)), pltpu.SemaphoreType.DMA((n,))], start, finish)


def _scatter_carry(sums):
    n = len(sums)

    def copies(s_refs, l_refs, sems):
        send_sems, recv_sems = sems
        x, y, c = _place()
        chips, _ = _other_chips(x, y)
        return [pltpu.make_async_remote_copy(
            src_ref=s_refs[w].at[j], dst_ref=l_refs[w].at[j], send_sem=send_sems.at[w, j], recv_sem=recv_sems.at[w, j],
            device_id=(*chips[j], c), device_id_type=MESH) for w in range(n) for j in range(3)]

    def start(s_refs, l_refs, sems):
        for cp in copies(s_refs, l_refs, sems):
            cp.start()

    def finish(s_refs, l_refs, sems):
        for cp in copies(s_refs, l_refs, sems):
            cp.wait()

    return _Carry(sums, [jax.ShapeDtypeStruct(s.shape, s.dtype) for s in sums], {},
                  [pltpu.SemaphoreType.DMA((n, 3)), pltpu.SemaphoreType.DMA((n, 3))], start, finish)


def _join_carry(outs, layers):
    n = len(outs)

    def copy(o_refs, sems, w, mine):
        send_sems, recv_sems = sems
        x, y, c = _place()
        r = outs[w].shape[1]
        rows = o_refs[w].at[layers[w], pl.ds((c if mine else 1 - c) * (r // 2), r // 2), :]
        return pltpu.make_async_remote_copy(
            src_ref=rows, dst_ref=rows, send_sem=send_sems.at[w], recv_sem=recv_sems.at[w],
            device_id=(x, y, 1 - c), device_id_type=MESH)

    def start(i_refs, o_refs, sems):
        for w in range(n):
            copy(o_refs, sems, w, True).start()

    def finish(i_refs, o_refs, sems):
        for w in range(n):
            copy(o_refs, sems, w, True).wait_send()
        for w in range(n):
            copy(o_refs, sems, w, False).wait_recv()

    return _Carry(outs, [jax.ShapeDtypeStruct(o.shape, o.dtype) for o in outs], {w: w for w in range(n)},
                  [pltpu.SemaphoreType.DMA((n,)), pltpu.SemaphoreType.DMA((n,))], start, finish)


PF_C, PF_Q, PF_QS = 0, 1, 2


def _add_sibling_half(g, landed, pf, name):
    _, r, cols = g.shape
    rh = r // 2
    tr = _row_tile(rh, cols)
    nr = rh // tr

    def body(pf_ref, g_ref, l_ref, o_ref):
        o_ref[...] = (g_ref[...] + l_ref[...]).astype(BF16)

    return pl.pallas_call(
        body, name=name,
        out_shape=jax.ShapeDtypeStruct((3, rh, cols), BF16),
        grid_spec=pltpu.PrefetchScalarGridSpec(
            num_scalar_prefetch=1, grid=(3, nr),
            in_specs=[pl.BlockSpec((1, tr, cols), lambda j, i, pf_ref: (pf_ref[PF_QS + j], pf_ref[PF_C] * nr + i, 0)),
                      pl.BlockSpec((1, tr, cols), lambda j, i, pf_ref: (pf_ref[PF_QS + j], i, 0))],
            out_specs=pl.BlockSpec((1, tr, cols), lambda j, i, pf_ref: (j, i, 0))),
        compiler_params=_cparams(2),
    )(pf, g, landed)


def _add_chips(g, landed, from_chips, pf, prev, layer, n_layer, name):
    _, r, cols = g.shape
    rh = r // 2
    tr = _row_tile(rh, cols)
    nr = rh // tr

    def body(pf_ref, g_ref, l_ref, f_ref, *rest):
        o_ref = rest[-1]
        acc = g_ref[0] + l_ref[0]
        for j in range(3):
            acc = acc + f_ref[j].astype(F32)
        o_ref[0] = acc

    in_specs = [pl.BlockSpec((1, tr, cols), lambda i, pf_ref: (pf_ref[PF_Q], pf_ref[PF_C] * nr + i, 0)),
                pl.BlockSpec((1, tr, cols), lambda i, pf_ref: (pf_ref[PF_Q], i, 0)),
                pl.BlockSpec((3, tr, cols), lambda i, pf_ref: (0, i, 0))]
    args = [pf, g, landed, from_chips]
    aliases = {}
    if prev is not None:
        in_specs.append(pl.BlockSpec(memory_space=pl.ANY))
        args.append(prev)
        aliases = {4: 0}
    return pl.pallas_call(
        body, name=name,
        out_shape=jax.ShapeDtypeStruct((n_layer, r, cols), F32),
        grid_spec=pltpu.PrefetchScalarGridSpec(
            num_scalar_prefetch=1, grid=(nr,), in_specs=in_specs,
            out_specs=pl.BlockSpec((1, tr, cols), lambda i, pf_ref: (layer, pf_ref[PF_C] * nr + i, 0))),
        input_output_aliases=aliases,
        compiler_params=_cparams(1),
    )(*args)


def _allgather_carry(blocks):
    n = len(blocks)

    def copies(b_refs, o_refs, sems):
        send_sems, recv_sems = sems
        x, y, c = _place()
        chips, _ = _other_chips(x, y)

        def place(w, px, py, pc):
            m = blocks[w].shape[0]
            return o_refs[w].at[pl.ds((4 * px + 2 * py + pc) * m, m), :]

        def own_to(w, k, to):
            dst = place(w, x, y, c)
            return pltpu.make_async_remote_copy(src_ref=b_refs[w], dst_ref=dst, send_sem=send_sems.at[w, k],
                                                recv_sem=recv_sems.at[w, k], device_id=to, device_id_type=MESH)

        def landed_from(w, k, px, py, pc):
            blk = place(w, px, py, pc)
            return pltpu.make_async_remote_copy(src_ref=blk, dst_ref=blk, send_sem=send_sems.at[w, k],
                                                recv_sem=recv_sems.at[w, k], device_id=(x, y, 1 - c), device_id_type=MESH)

        return x, y, c, chips, own_to, landed_from

    def start(b_refs, o_refs, sems):
        x, y, c, chips, own_to, _ = copies(b_refs, o_refs, sems)
        for w in range(n):
            own_to(w, 0, (x, y, 1 - c)).start()
            for j, chip in enumerate(chips):
                own_to(w, 1 + j, (*chip, c)).start()

    def mid(b_refs, o_refs, sems):
        x, y, c, chips, _, landed_from = copies(b_refs, o_refs, sems)
        for w in range(n):
            for j, chip in enumerate(chips):
                landed_from(w, 1 + j, *chip, c).wait_recv()
                landed_from(w, 4 + j, *chip, c).start()

    def finish(b_refs, o_refs, sems):
        x, y, c, chips, own_to, landed_from = copies(b_refs, o_refs, sems)
        for w in range(n):
            landed_from(w, 0, x, y, 1 - c).wait_recv()
            for j, chip in enumerate(chips):
                landed_from(w, 4 + j, *chip, 1 - c).wait_recv()
            own_to(w, 0, (x, y, 1 - c)).wait_send()
            for j, chip in enumerate(chips):
                own_to(w, 1 + j, (*chip, c)).wait_send()
                landed_from(w, 4 + j, *chip, c).wait_send()

    return _Carry(blocks, [jax.ShapeDtypeStruct((N_DEV * b.shape[0], b.shape[1]), b.dtype) for b in blocks], {},
                  [pltpu.SemaphoreType.DMA((n, 7)), pltpu.SemaphoreType.DMA((n, 7))], start, finish, mid)


def _sum_devices(gathered, own, me_arr, name):
    m, n = own.shape
    tr = _row_tile(m, n, itemsize=own.dtype.itemsize, target_bytes=256 * 1024)
    nr = m // tr

    def body(me_ref, *refs):
        g_refs, own_ref, o_ref = refs[:N_DEV], refs[N_DEV], refs[N_DEV + 1]
        me = me_ref[0]
        acc = None
        for dev in range(N_DEV):
            term = jnp.where(me == dev, own_ref[...], g_refs[dev][...]).astype(F32)
            acc = term if acc is None else acc + term
        o_ref[...] = acc

    def dev_rows(dev):
        return pl.BlockSpec((tr, n), lambda i, me_ref: (dev * nr + i, 0))

    return pl.pallas_call(
        body, name=name,
        out_shape=jax.ShapeDtypeStruct((m, n), F32),
        grid_spec=pltpu.PrefetchScalarGridSpec(
            num_scalar_prefetch=1, grid=(nr,),
            in_specs=[dev_rows(dev) for dev in range(N_DEV)] + [pl.BlockSpec((tr, n), lambda i, me_ref: (i, 0))],
            out_specs=pl.BlockSpec((tr, n), lambda i, me_ref: (i, 0))),
        compiler_params=_cparams(1),
    )(me_arr, *([gathered] * N_DEV), own)


def _mod_forward(c_all, w_mod, b_mod_shard, name):
    n_layer, d, mq = w_mod.shape

    def body(c_ref, w_ref, b_ref, o_ref):
        cv = c_ref[...]
        o_ref[...] = _dot(cv * _sigmoid(cv), w_ref[0]) + b_ref[0]

    return pl.pallas_call(
        body, name=name, grid=(n_layer,),
        out_shape=jax.ShapeDtypeStruct((n_layer * N_DEV, mq), F32),
        in_specs=[_full((N_DEV, d)), pl.BlockSpec((1, d, mq), lambda l: (l, 0, 0)),
                  pl.BlockSpec((1, 1, mq), lambda l: (l, 0, 0))],
        out_specs=pl.BlockSpec((N_DEV, mq), lambda l: (l, 0)),
        compiler_params=_cparams(1),
    )(c_all, w_mod, b_mod_shard.reshape(n_layer, 1, mq))


def _mod_backward(c_all_t, dmod_shard, name):
    n_layer, _, mq = dmod_shard.shape
    d = c_all_t.shape[0]

    def body(c_ref, dm_ref, o_ref):
        cv = c_ref[...]
        o_ref[0] = _dot(cv * _sigmoid(cv), dm_ref[0])

    return pl.pallas_call(
        body, name=name, grid=(n_layer,),
        out_shape=jax.ShapeDtypeStruct((n_layer, d, mq), F32),
        in_specs=[_full((d, N_DEV)), pl.BlockSpec((1, N_DEV, mq), lambda l: (l, 0, 0))],
        out_specs=pl.BlockSpec((1, d, mq), lambda l: (l, 0, 0)),
        compiler_params=_cparams(1),
    )(c_all_t, dmod_shard)


def _norm_proj(x, mod, vec, w_in, name, carries=()):
    s, d = x.shape
    nq = w_in.shape[2]
    ts = min(TOKENS_MATMUL_TILE, s)

    def body(x_ref, mod_ref, vec_ref, w_ref, h_ref, p_ref, dgel_ref):
        xn, _ = _rms(x_ref[...])
        gm = vec_ref[V_G_PRE_MIX:V_G_PRE_MIX + 1, :] * (1.0 + mod_ref[M_SC_M:M_SC_M + 1, :])
        h = (xn * gm + mod_ref[M_SH_M:M_SH_M + 1, :]).astype(BF16)
        h_ref[...] = h
        for qb in range(N_CHIP):
            pq = _dot(h, w_ref[qb])
            for k in range(N_CHIP * nq // d):
                lo, hi = max(qb * nq, k * d), min((qb + 1) * nq, (k + 1) * d)
                if lo >= hi:
                    continue
                piece = pq[:, lo - qb * nq:hi - qb * nq]
                if k == 4:
                    piece, dgel = _gelu_and_grad(piece)
                    dgel_ref[:, lo - 4 * d:hi - 4 * d] = dgel.astype(BF16)
                elif k >= 5:
                    piece = _sigmoid(piece)
                p_ref[:, lo:hi] = piece.astype(BF16)

    tile = pl.BlockSpec((ts, d), lambda i: (i, 0))
    return _pcall(
        body, name=name, grid=(s // ts,),
        out_shape=[jax.ShapeDtypeStruct((s, d), BF16), jax.ShapeDtypeStruct((s, N_CHIP * nq), BF16),
                   jax.ShapeDtypeStruct((s, d), BF16)],
        in_specs=[tile, _full(mod.shape), _full(vec.shape), _full(w_in.shape)],
        out_specs=[tile, pl.BlockSpec((ts, N_CHIP * nq), lambda i: (i, 0)), tile],
        args=[x, mod, vec, w_in], carries=carries)


def _gate_pre(xb2_b, wg_ref, n_head, bw):
    zr, zi = [], []
    for hd in range(n_head):
        z = _dot(xb2_b[:, hd * bw:(hd + 1) * bw], wg_ref[hd])
        zr.append(z[:, :bw])
        zi.append(z[:, bw:])
    return jnp.concatenate(zr, axis=1), jnp.concatenate(zi, axis=1)


def _lru_coeffs(xb2, wg_ref, vec_ref, n_head, bw):
    zr, zi = _gate_pre(xb2.astype(BF16), wg_ref, n_head, bw)
    r = _sigmoid(zr + vec_ref[V_B_GATE_R:V_B_GATE_R + 1, :])
    gi = _sigmoid(zi + vec_ref[V_B_GATE_I:V_B_GATE_I + 1, :])
    sp = _softplus(-vec_ref[V_LAMBDA:V_LAMBDA + 1, :])
    log_a = (-LRU_C) * r * sp
    a = jnp.exp(log_a)
    mult = jnp.sqrt(_neg_expm1(2.0 * log_a))
    return r, gi, sp, a, mult


def _mixer_forward(x, proj, mod, vec, wg, w_a_out, w_b_out, w_o, name, carries=()):
    s, d = x.shape
    n_head, bw, _ = wg.shape
    ts = min(TOKENS_MIXER_TILE, s)

    def body(x_ref, p_ref, mod_ref, vec_ref, wg_ref, wa_ref, wb_ref, wo_ref,
             x1_ref, conva_ref, xb2_ref, hh_ref, ya_ref, yb_ref, pa_ref, pb_ref, m_ref, y_ref,
             r_ref, gi_ref, a_ref, mult_ref,
             cv_tail, xb_tail, h_last, a_buf, b_buf, c_buf):
        i = pl.program_id(0)

        @pl.when(i == 0)
        def _():
            cv_tail[...] = jnp.zeros_like(cv_tail)
            xb_tail[...] = jnp.zeros_like(xb_tail)
            h_last[...] = jnp.zeros_like(h_last)

        def seg(k):
            return p_ref[:, k * d:(k + 1) * d].astype(F32)

        def vrow(k):
            return vec_ref[k:k + 1, :]

        b_a, c_a, v_a, x_b, gel, sa, sb = (seg(k) for k in range(7))
        cv = c_a * v_a
        prev_cv = cv_tail[...]
        conv_a = (vrow(V_CONV_A_B) + vrow(V_CONV_A_W) * _shift_down(cv, 2, prev_cv)
                  + vrow(V_CONV_A_W + 1) * _shift_down(cv, 1, prev_cv) + vrow(V_CONV_A_W + 2) * cv)
        cv_tail[...] = cv[ts - SUBLANES:]
        y_a = b_a * conv_a
        prev_xb = xb_tail[...]
        xb2 = (vrow(V_CONV_B_B) + vrow(V_CONV_B_W) * _shift_down(x_b, 3, prev_xb)
               + vrow(V_CONV_B_W + 1) * _shift_down(x_b, 2, prev_xb)
               + vrow(V_CONV_B_W + 2) * _shift_down(x_b, 1, prev_xb) + vrow(V_CONV_B_W + 3) * x_b)
        xb_tail[...] = x_b[ts - SUBLANES:]
        r, gi, _, a, mult = _lru_coeffs(xb2, wg_ref, vec_ref, n_head, bw)
        r_ref[...] = r
        gi_ref[...] = gi
        a_ref[...] = a
        mult_ref[...] = mult
        hh = _scan_two_level(a, mult * gi * xb2, h_last[SUBLANES - 1:SUBLANES, :], a_buf, b_buf, c_buf, reverse=False)
        h_last[...] = hh[ts - SUBLANES:]
        y_b = hh * gel
        ya_b, yb_b = y_a.astype(BF16), y_b.astype(BF16)
        pa = _dot(ya_b, wa_ref[...])
        pb = _dot(yb_b, wb_ref[...])
        m = (sa * pa + sb * pb).astype(BF16)
        y = _dot(m, wo_ref[...])
        yn, _ = _rms(y)
        gg = mod_ref[M_GT_M:M_GT_M + 1, :] * vrow(V_G_POST_MIX)
        x1_ref[...] = x_ref[...] + yn * gg
        conva_ref[...] = conv_a.astype(BF16)
        xb2_ref[...] = xb2
        hh_ref[...] = hh
        ya_ref[...] = ya_b
        yb_ref[...] = yb_b
        pa_ref[...] = pa.astype(BF16)
        pb_ref[...] = pb.astype(BF16)
        m_ref[...] = m
        y_ref[...] = y.astype(BF16)

    tile = pl.BlockSpec((ts, d), lambda i: (i, 0))
    sd = lambda dt: jax.ShapeDtypeStruct((s, d), dt)
    return _pcall(
        body, name=name, grid=(s // ts,),
        out_shape=[sd(F32), sd(BF16), sd(F32), sd(F32), sd(BF16), sd(BF16), sd(BF16), sd(BF16), sd(BF16), sd(BF16),
                   sd(F32), sd(F32), sd(F32), sd(F32)],
        in_specs=[tile, pl.BlockSpec((ts, 7 * d), lambda i: (i, 0)), _full(mod.shape), _full(vec.shape),
                  _full(wg.shape), _full(w_a_out.shape), _full(w_b_out.shape), _full(w_o.shape)],
        out_specs=[tile] * 14,
        scratch_shapes=[pltpu.VMEM((SUBLANES, d), F32)] * 3 + [pltpu.VMEM((d // LANES, ts, LANES), F32)] * 2
                       + [pltpu.VMEM((ts // SCAN_GROUP, d), F32)],
        args=[x, proj, mod, vec, wg, w_a_out, w_b_out, w_o], carries=carries)


def _mlp_forward(x1, mod, vec, w_up, w_down, name, carries=()):
    s, d = x1.shape
    fq = w_up.shape[2]
    ts = min(TOKENS_MATMUL_TILE, s)

    def body(x_ref, mod_ref, vec_ref, wu_ref, wd_ref, x2_ref, h2_ref, up_ref, y2_ref):
        x = x_ref[...]
        xn, _ = _rms(x)
        gm = vec_ref[V_G_PRE_MLP:V_G_PRE_MLP + 1, :] * (1.0 + mod_ref[M_SC_F:M_SC_F + 1, :])
        h2 = (xn * gm + mod_ref[M_SH_F:M_SH_F + 1, :]).astype(BF16)
        h2_ref[...] = h2
        y2 = jnp.zeros((ts, d), F32)
        for qb in range(N_CHIP):
            up = _dot(h2, wu_ref[qb])
            up_ref[:, qb * fq:(qb + 1) * fq] = up.astype(BF16)
            ru = jnp.maximum(up, 0.0)
            y2 = y2 + _dot((ru * ru).astype(BF16), wd_ref[qb])
        y2_ref[...] = y2.astype(BF16)
        yn, _ = _rms(y2)
        gg = mod_ref[M_GT_F:M_GT_F + 1, :] * vec_ref[V_G_POST_MLP:V_G_POST_MLP + 1, :]
        x2_ref[...] = x + yn * gg

    tile = pl.BlockSpec((ts, d), lambda i: (i, 0))
    return _pcall(
        body, name=name, grid=(s // ts,),
        out_shape=[jax.ShapeDtypeStruct((s, d), F32), jax.ShapeDtypeStruct((s, d), BF16),
                   jax.ShapeDtypeStruct((s, N_CHIP * fq), BF16), jax.ShapeDtypeStruct((s, d), BF16)],
        in_specs=[tile, _full(mod.shape), _full(vec.shape), _full(w_up.shape), _full(w_down.shape)],
        out_specs=[tile, tile, pl.BlockSpec((ts, N_CHIP * fq), lambda i: (i, 0)), tile],
        args=[x1, mod, vec, w_up, w_down], carries=carries)


def _loss_head(xf, target, name):
    s, d = xf.shape
    ts = min(TOKENS_MATMUL_TILE, s)

    def body(x_ref, t_ref, dx_ref, loss_ref):
        @pl.when(pl.program_id(0) == 0)
        def _():
            loss_ref[...] = jnp.zeros_like(loss_ref)

        err = x_ref[...] - t_ref[...]
        dx_ref[...] = err * (1.0 / d)
        part = jnp.sum(jnp.sum(err * err, axis=1, keepdims=True), axis=0, keepdims=True) * (0.5 / d)
        loss_ref[...] = loss_ref[...] + part

    tile = pl.BlockSpec((ts, d), lambda i: (i, 0))
    return pl.pallas_call(
        body, name=name, grid=(s // ts,),
        out_shape=[jax.ShapeDtypeStruct((s, d), F32), jax.ShapeDtypeStruct((SUBLANES, 128), F32)],
        in_specs=[tile, tile], out_specs=[tile, _full((SUBLANES, 128))],
        compiler_params=_cparams(1),
    )(xf, target)


SB3_DSH, SB3_DSC, SB3_DGT, SB3_DG_PRE, SB3_DG_POST = range(5)
SB1_DSH, SB1_DSC, SB1_DG_PRE = range(3)
(SB2_DGT, SB2_DG_POST, SB2_DWA, SB2_DBA, SB2_DWB, SB2_DBB, SB2_DLAM, SB2_DBR, SB2_DBI) = (0, 1, 2, 5, 6, 10, 11, 12, 13)


def _mlp_backward(dx2, x1, y2, up, mod, vec, w_up, w_down, name, carries=()):
    s, d = dx2.shape
    fq = w_up.shape[2]
    ts = min(TOKENS_MIXER_TILE, s)
    n_t = s // ts

    def body(dx2_ref, x_ref, y2_ref, up_ref, mod_ref, vec_ref, wu_ref, wd_ref,
             dx1_ref, dy2_ref, dup_ref, act_ref, small_ref):
        i = pl.program_id(0)

        @pl.when(i == 0)
        def _():
            small_ref[...] = jnp.zeros_like(small_ref)

        dout = dx2_ref[...]
        y2n, ry = _rms(y2_ref[...].astype(F32))
        g_post = vec_ref[V_G_POST_MLP:V_G_POST_MLP + 1, :]
        gt = mod_ref[M_GT_F:M_GT_F + 1, :]
        dgg = _colsum(dout * y2n)
        dy2 = _rms_bwd(dout * (gt * g_post), y2n, ry).astype(BF16)
        dy2_ref[...] = dy2
        dh2 = jnp.zeros((ts, d), F32)
        for qb in range(N_CHIP):
            cols = slice(qb * fq, (qb + 1) * fq)
            dact = _dot_tb(dy2, wd_ref[qb])
            ru = jnp.maximum(up_ref[:, cols].astype(F32), 0.0)
            dup = (dact * (2.0 * ru)).astype(BF16)
            dup_ref[:, cols] = dup
            act_ref[:, cols] = (ru * ru).astype(BF16)
            dh2 = dh2 + _dot_tb(dup, wu_ref[qb])
        xn, r = _rms(x_ref[...])
        g_pre = vec_ref[V_G_PRE_MLP:V_G_PRE_MLP + 1, :]
        sc1 = 1.0 + mod_ref[M_SC_F:M_SC_F + 1, :]
        dsh = _colsum(dh2)
        dgm = _colsum(dh2 * xn)
        dx1_ref[...] = dout + _rms_bwd(dh2 * (g_pre * sc1), xn, r)
        small_ref[SB3_DSH:SB3_DSH + 1, :] += dsh
        small_ref[SB3_DSC:SB3_DSC + 1, :] += dgm
        small_ref[SB3_DGT:SB3_DGT + 1, :] += dgg

        @pl.when(i == n_t - 1)
        def _():
            dgm_t = small_ref[SB3_DSC:SB3_DSC + 1, :]
            dgg_t = small_ref[SB3_DGT:SB3_DGT + 1, :]
            small_ref[SB3_DSC:SB3_DSC + 1, :] = dgm_t * g_pre
            small_ref[SB3_DG_PRE:SB3_DG_PRE + 1, :] = dgm_t * sc1
            small_ref[SB3_DGT:SB3_DGT + 1, :] = dgg_t * g_post
            small_ref[SB3_DG_POST:SB3_DG_POST + 1, :] = dgg_t * gt

    tile = pl.BlockSpec((ts, d), lambda i: (i, 0))
    wide = pl.BlockSpec((ts, N_CHIP * fq), lambda i: (i, 0))
    return _pcall(
        body, name=name, grid=(n_t,),
        out_shape=[jax.ShapeDtypeStruct((s, d), F32), jax.ShapeDtypeStruct((s, d), BF16),
                   jax.ShapeDtypeStruct((s, N_CHIP * fq), BF16), jax.ShapeDtypeStruct((s, N_CHIP * fq), BF16),
                   jax.ShapeDtypeStruct((SUBLANES, d), F32)],
        in_specs=[tile, tile, tile, wide, _full(mod.shape), _full(vec.shape), _full(w_up.shape), _full(w_down.shape)],
        out_specs=[tile, tile, wide, wide, _full((SUBLANES, d))],
        args=[dx2, x1, y2, up, mod, vec, w_up, w_down], carries=carries)


def _mixer_backward(dx1, proj, conva, xb2s, hhs, pas, pbs, ys, rs_, gis, as_, mults, dgels, mod, vec, wg, w_a_out, w_b_out,
                    w_o, name, carries=()):
    s, d = dx1.shape
    n_head, bw, _ = wg.shape
    ts = min(TOKENS_MIXER_TILE, s)
    n_t = s // ts

    def body(dx1_ref, p_ref, conva_ref, xb2_ref, hh_ref, pa_ref, pb_ref, y_ref, r_ref, gi_ref, a_ref, mult_ref, dgel_ref,
             mod_ref, vec_ref, wg_ref, wa_ref, wb_ref, wo_ref,
             dp_ref, dy_ref, dpa_ref, dpb_ref, small_ref, dwg_ref,
             dconv_head, dxb2_head, a_head, g_head, a_buf, b_buf, c_buf):
        i = pl.program_id(0)

        @pl.when(i == 0)
        def _():
            small_ref[...] = jnp.zeros_like(small_ref)
            dwg_ref[...] = jnp.zeros_like(dwg_ref)
            dconv_head[...] = jnp.zeros_like(dconv_head)
            dxb2_head[...] = jnp.zeros_like(dxb2_head)
            a_head[...] = jnp.zeros_like(a_head)
            g_head[...] = jnp.zeros_like(g_head)

        def seg(k):
            return p_ref[:, k * d:(k + 1) * d].astype(F32)

        def vrow(k):
            return vec_ref[k:k + 1, :]

        def acc(row, val):
            small_ref[row:row + 1, :] += val

        dout = dx1_ref[...]
        yn, ry = _rms(y_ref[...].astype(F32))
        g_post = vrow(V_G_POST_MIX)
        gt = mod_ref[M_GT_M:M_GT_M + 1, :]
        acc(SB2_DGT, _colsum(dout * yn))
        dy = _rms_bwd(dout * (gt * g_post), yn, ry).astype(BF16)
        dy_ref[...] = dy
        dm = _dot_tb(dy, wo_ref[...])
        sa, sb = seg(5), seg(6)
        dpa = (dm * sa).astype(BF16)
        dpb = (dm * sb).astype(BF16)
        dpa_ref[...] = dpa
        dpb_ref[...] = dpb
        du_a = dm * pa_ref[...].astype(F32) * (sa * (1.0 - sa))
        du_b = dm * pb_ref[...].astype(F32) * (sb * (1.0 - sb))
        dp_ref[:, 5 * d:6 * d] = du_a.astype(BF16)
        dp_ref[:, 6 * d:7 * d] = du_b.astype(BF16)
        dy_a = _dot_tb(dpa, wa_ref[...])
        dy_b = _dot_tb(dpb, wb_ref[...])

        b_a, c_a, v_a = seg(0), seg(1), seg(2)
        dp_ref[:, 0:d] = (dy_a * conva_ref[...].astype(F32)).astype(BF16)
        dconv = dy_a * b_a
        nxt = dconv_head[...]
        d1 = _shift_up(dconv, 1, nxt)
        d2 = _shift_up(dconv, 2, nxt)
        dconv_head[...] = dconv[:SUBLANES]
        dcv = vrow(V_CONV_A_W + 2) * dconv + vrow(V_CONV_A_W + 1) * d1 + vrow(V_CONV_A_W) * d2
        cv = c_a * v_a
        acc(SB2_DWA + 2, _colsum(cv * dconv))
        acc(SB2_DWA + 1, _colsum(cv * d1))
        acc(SB2_DWA, _colsum(cv * d2))
        acc(SB2_DBA, _colsum(dconv))
        dp_ref[:, d:2 * d] = (dcv * v_a).astype(BF16)
        dp_ref[:, 2 * d:3 * d] = (dcv * c_a).astype(BF16)

        x_b, gel = seg(3), seg(4)
        hh = hh_ref[...]
        dp_ref[:, 4 * d:5 * d] = (dy_b * hh * dgel_ref[...].astype(F32)).astype(BF16)
        dhh = dy_b * gel
        xb2 = xb2_ref[...]
        r, gi, a, mult = r_ref[...], gi_ref[...], a_ref[...], mult_ref[...]
        sp = _softplus(-vrow(V_LAMBDA))
        a_next = _shift_up(a, 1, a_head[...])
        g = _scan_two_level(a_next, dhh, g_head[0:1, :], a_buf, b_buf, c_buf, reverse=True)
        a_head[...] = a[:SUBLANES]
        g_head[...] = g[:SUBLANES]
        gix = gi * xb2
        gm = g * mult
        dlog_a = g * (hh - mult * gix) - (g * gix) * (a * a / mult)
        dgi = gm * xb2
        dxb2 = gm * gi
        acc(SB2_DLAM, _colsum(dlog_a * r))
        dzr = dlog_a * ((-LRU_C) * sp) * (r * (1.0 - r))
        dzi = dgi * (gi * (1.0 - gi))
        acc(SB2_DBR, _colsum(dzr))
        acc(SB2_DBI, _colsum(dzi))
        xb2_b = xb2.astype(BF16)
        back = []
        for hd in range(n_head):
            cols = slice(hd * bw, (hd + 1) * bw)
            dz = jnp.concatenate([dzr[:, cols], dzi[:, cols]], axis=1).astype(BF16)
            back.append(_dot_tb(dz, wg_ref[hd]))
            dwg_ref[hd] += _dot_ta(xb2_b[:, cols], dz)
        dxb2 = dxb2 + jnp.concatenate(back, axis=1)
        nxt = dxb2_head[...]
        e1 = _shift_up(dxb2, 1, nxt)
        e2 = _shift_up(dxb2, 2, nxt)
        e3 = _shift_up(dxb2, 3, nxt)
        dxb2_head[...] = dxb2[:SUBLANES]
        dp_ref[:, 3 * d:4 * d] = (vrow(V_CONV_B_W + 3) * dxb2 + vrow(V_CONV_B_W + 2) * e1
                                  + vrow(V_CONV_B_W + 1) * e2 + vrow(V_CONV_B_W) * e3).astype(BF16)
        acc(SB2_DWB + 3, _colsum(x_b * dxb2))
        acc(SB2_DWB + 2, _colsum(x_b * e1))
        acc(SB2_DWB + 1, _colsum(x_b * e2))
        acc(SB2_DWB, _colsum(x_b * e3))
        acc(SB2_DBB, _colsum(dxb2))

        @pl.when(i == n_t - 1)
        def _():
            dgg_t = small_ref[SB2_DGT:SB2_DGT + 1, :]
            small_ref[SB2_DGT:SB2_DGT + 1, :] = dgg_t * g_post
            small_ref[SB2_DG_POST:SB2_DG_POST + 1, :] = dgg_t * gt
            lam = vrow(V_LAMBDA)
            small_ref[SB2_DLAM:SB2_DLAM + 1, :] = small_ref[SB2_DLAM:SB2_DLAM + 1, :] * (LRU_C * _sigmoid(-lam))

    rev = lambda i: (n_t - 1 - i, 0)
    tile = pl.BlockSpec((ts, d), rev)
    wide = pl.BlockSpec((ts, 7 * d), rev)
    sd = lambda dt: jax.ShapeDtypeStruct((s, d), dt)
    return _pcall(
        body, name=name, grid=(n_t,),
        out_shape=[jax.ShapeDtypeStruct((s, 7 * d), BF16), sd(BF16), sd(BF16), sd(BF16),
                   jax.ShapeDtypeStruct((2 * SUBLANES, d), F32), jax.ShapeDtypeStruct(wg.shape, F32)],
        in_specs=[tile, wide] + [tile] * 11 + [_full(mod.shape), _full(vec.shape),
                  _full(wg.shape), _full(w_a_out.shape), _full(w_b_out.shape), _full(w_o.shape)],
        out_specs=[wide, tile, tile, tile, _full((2 * SUBLANES, d)), _full(wg.shape)],
        scratch_shapes=[pltpu.VMEM((SUBLANES, d), F32)] * 4 + [pltpu.VMEM((d // LANES, ts, LANES), F32)] * 2
                       + [pltpu.VMEM((ts // SCAN_GROUP, d), F32)],
        args=[dx1, proj, conva, xb2s, hhs, pas, pbs, ys, rs_, gis, as_, mults, dgels, mod, vec, wg, w_a_out, w_b_out, w_o],
        carries=carries)


def _proj_backward(dproj, dx1, x, mod, vec, w_in, name, carries=()):
    s, d = x.shape
    nq = w_in.shape[2]
    ts = min(TOKENS_MATMUL_TILE, s)
    n_t = s // ts

    def body(dp_ref, dx1_ref, x_ref, mod_ref, vec_ref, w_ref, dx_ref, small_ref):
        i = pl.program_id(0)

        @pl.when(i == 0)
        def _():
            small_ref[...] = jnp.zeros_like(small_ref)

        dh = jnp.zeros((ts, d), F32)
        for qb in range(N_CHIP):
            dh = dh + _dot_tb(dp_ref[:, qb * nq:(qb + 1) * nq], w_ref[qb])
        xn, r = _rms(x_ref[...])
        g_pre = vec_ref[V_G_PRE_MIX:V_G_PRE_MIX + 1, :]
        sc1 = 1.0 + mod_ref[M_SC_M:M_SC_M + 1, :]
        dx_ref[...] = dx1_ref[...] + _rms_bwd(dh * (g_pre * sc1), xn, r)
        small_ref[SB1_DSH:SB1_DSH + 1, :] += _colsum(dh)
        small_ref[SB1_DSC:SB1_DSC + 1, :] += _colsum(dh * xn)

        @pl.when(i == n_t - 1)
        def _():
            dgm_t = small_ref[SB1_DSC:SB1_DSC + 1, :]
            small_ref[SB1_DSC:SB1_DSC + 1, :] = dgm_t * g_pre
            small_ref[SB1_DG_PRE:SB1_DG_PRE + 1, :] = dgm_t * sc1

    tile = pl.BlockSpec((ts, d), lambda i: (i, 0))
    return _pcall(
        body, name=name, grid=(n_t,),
        out_shape=[jax.ShapeDtypeStruct((s, d), F32), jax.ShapeDtypeStruct((SUBLANES, d), F32)],
        in_specs=[pl.BlockSpec((ts, N_CHIP * nq), lambda i: (i, 0)), tile, tile, _full(mod.shape), _full(vec.shape),
                  _full(w_in.shape)],
        out_specs=[tile, _full((SUBLANES, d))],
        args=[dproj, dx1, x, mod, vec, w_in], carries=carries)


def _weight_grad(a, b, name, col_blocks=1, tk=512, carries=()):
    s, k = a.shape
    n = b.shape[1]
    tn = n // col_blocks
    tk = min(tk, k)

    def body(a_ref, b_ref, o_ref):
        o_ref[0] = _dot_ta(a_ref[...], b_ref[...])

    (out,), carried = _pcall(
        body, name=name, grid=(col_blocks, k // tk),
        out_shape=[jax.ShapeDtypeStruct((col_blocks, k, tn), F32)],
        in_specs=[pl.BlockSpec((s, tk), lambda j, i: (0, i)), pl.BlockSpec((s, tn), lambda j, i: (0, j))],
        out_specs=[pl.BlockSpec((1, tk, tn), lambda j, i: (j, i, 0))],
        args=[a, b], carries=carries)
    return out, carried


def _adamw(w, g, m, v, name, copy_grad=False):
    shape = w.shape
    cols = shape[-1]
    rows = w.size // cols
    tr = _row_tile(rows, cols, target_bytes=1024 * 1024)
    c1 = 1.0 - ADAM_B1 ** ADAM_STEP
    c2 = 1.0 - ADAM_B2 ** ADAM_STEP
    n_out = 4 if copy_grad else 3

    def body(w_ref, g_ref, m_ref, v_ref, d_ref, nm_ref, nv_ref, *g_out):
        gv = g_ref[...]
        nm = ADAM_B1 * m_ref[...] + (1.0 - ADAM_B1) * gv
        nv = ADAM_B2 * v_ref[...] + (1.0 - ADAM_B2) * (gv * gv)
        nm_ref[...] = nm
        nv_ref[...] = nv
        d_ref[...] = (-ADAM_LR) * ((nm / c1) / (jnp.sqrt(nv / c2) + ADAM_EPS) + ADAM_WD * w_ref[...])
        if copy_grad:
            g_out[0][...] = gv

    spec = pl.BlockSpec((tr, cols), lambda i: (i, 0))
    outs = pl.pallas_call(
        body, name=name, grid=(rows // tr,),
        out_shape=[jax.ShapeDtypeStruct((rows, cols), F32)] * n_out,
        in_specs=[spec] * 4, out_specs=[spec] * n_out,
        compiler_params=_cparams(1),
    )(*(t.reshape(rows, cols) for t in (w, g, m, v)))
    return tuple(o.reshape(shape) for o in outs)


def kernel(x, c, w_mod, b_mod, g_pre_mix, g_post_mix, w_in, conv_a_w, conv_a_b, w_a_out, conv_b_w, conv_b_b, w_gate_r, b_gate_r, w_gate_i, b_gate_i, lru_lambda, w_b_out, w_o, g_pre_mlp, g_post_mlp, w_mlp_up, w_mlp_down, loss_target, m_w_mod, m_b_mod, m_g_pre_mix, m_g_post_mix, m_w_in, m_conv_a_w, m_conv_a_b, m_w_a_out, m_conv_b_w, m_conv_b_b, m_w_gate_r, m_b_gate_r, m_w_gate_i, m_b_gate_i, m_lru_lambda, m_w_b_out, m_w_o, m_g_pre_mlp, m_g_post_mlp, m_w_mlp_up, m_w_mlp_down, v_w_mod, v_b_mod, v_g_pre_mix, v_g_post_mix, v_w_in, v_conv_a_w, v_conv_a_b, v_w_a_out, v_conv_b_w, v_conv_b_b, v_w_gate_r, v_b_gate_r, v_w_gate_i, v_b_gate_i, v_lru_lambda, v_w_b_out, v_w_o, v_g_pre_mlp, v_g_post_mlp, v_w_mlp_up, v_w_mlp_down):
    weights = dict(w_mod=w_mod, b_mod=b_mod, g_pre_mix=g_pre_mix, g_post_mix=g_post_mix, w_in=w_in, conv_a_w=conv_a_w,
                   conv_a_b=conv_a_b, w_a_out=w_a_out, conv_b_w=conv_b_w, conv_b_b=conv_b_b, w_gate_r=w_gate_r,
                   b_gate_r=b_gate_r, w_gate_i=w_gate_i, b_gate_i=b_gate_i, lru_lambda=lru_lambda, w_b_out=w_b_out,
                   w_o=w_o, g_pre_mlp=g_pre_mlp, g_post_mlp=g_post_mlp, w_mlp_up=w_mlp_up, w_mlp_down=w_mlp_down)
    mom1 = dict(w_mod=m_w_mod, b_mod=m_b_mod, g_pre_mix=m_g_pre_mix, g_post_mix=m_g_post_mix, w_in=m_w_in,
                conv_a_w=m_conv_a_w, conv_a_b=m_conv_a_b, w_a_out=m_w_a_out, conv_b_w=m_conv_b_w, conv_b_b=m_conv_b_b,
                w_gate_r=m_w_gate_r, b_gate_r=m_b_gate_r, w_gate_i=m_w_gate_i, b_gate_i=m_b_gate_i,
                lru_lambda=m_lru_lambda, w_b_out=m_w_b_out, w_o=m_w_o, g_pre_mlp=m_g_pre_mlp, g_post_mlp=m_g_post_mlp,
                w_mlp_up=m_w_mlp_up, w_mlp_down=m_w_mlp_down)
    mom2 = dict(w_mod=v_w_mod, b_mod=v_b_mod, g_pre_mix=v_g_pre_mix, g_post_mix=v_g_post_mix, w_in=v_w_in,
                conv_a_w=v_conv_a_w, conv_a_b=v_conv_a_b, w_a_out=v_w_a_out, conv_b_w=v_conv_b_w, conv_b_b=v_conv_b_b,
                w_gate_r=v_w_gate_r, b_gate_r=v_b_gate_r, w_gate_i=v_w_gate_i, b_gate_i=v_b_gate_i,
                lru_lambda=v_lru_lambda, w_b_out=v_w_b_out, w_o=v_w_o, g_pre_mlp=v_g_pre_mlp, g_post_mlp=v_g_post_mlp,
                w_mlp_up=v_w_mlp_up, w_mlp_down=v_w_mlp_down)
    names = list(weights)

    n_layer = w_in.shape[0]
    s, d = x.shape[1], x.shape[2]
    n_head, bw = w_gate_r.shape[1], w_gate_r.shape[2]
    dq = d // N_CHIP
    mq = w_mod.shape[2]
    n_mod = (N_CHIP * mq) // d
    ka, kb = conv_a_w.shape[1], conv_b_w.shape[1]

    mx, my, mc = _place()
    q_me = 2 * mx + my
    q_arr = jnp.reshape(q_me, (1,)).astype(jnp.int32)

    me_dev = 4 * mx + 2 * my + mc
    me_arr = jnp.reshape(me_dev, (1,)).astype(jnp.int32)
    is_me = (jnp.arange(N_DEV) == me_dev)[:, None, None]

    big_names = ["w_in", "w_a_out", "w_b_out", "w_o", "w_mlp_up", "w_mlp_down"]
    groups = [["w_in"], ["w_a_out", "w_b_out", "w_o"], ["w_mlp_up", "w_mlp_down"]]
    placed = {("w_in", 0): _cast_place(w_in, 0, q_arr, "cast_place_w_in_0")}
    wfull = [dict() for _ in range(n_layer)]
    riders = {}
    for l in range(n_layer):
        riders[3 * l - 1] = [("w_in", l)]
        riders[3 * l] = [(nm, l) for nm in groups[1]]
        riders[3 * l + 1] = [(nm, l) for nm in groups[2]]

    def gather_carry(call):
        keys = riders.get(call, [])
        return [_gather_carry([placed[k] for k in keys])] if keys else []

    def gathered(call, carried):
        for (nm, l), w in zip(riders.get(call, []), carried[0] if carried else []):
            wfull[l][nm] = w.reshape(d, d) if nm in groups[1] else w

    n_conv_rows = n_layer * (ka + kb)
    conv_blk = -(-n_conv_rows // SUBLANES) * SUBLANES
    blk_rows = SUBLANES + conv_blk
    conv_rows = jnp.concatenate([jnp.concatenate([conv_a_w[l], conv_b_w[l]], axis=0) for l in range(n_layer)], axis=0)
    conv_rows = jnp.pad(conv_rows, ((0, conv_blk - n_conv_rows), (0, d - dq)))
    c_conv = jnp.concatenate([jnp.pad(c, ((0, SUBLANES - 1), (0, 0))), conv_rows], axis=0)
    rest = [(nm, l) for l in range(n_layer) for nm in big_names if (nm, l) != ("w_in", 0)]
    rest_placed, carried = _cast_place_all([(weights[nm], l) for nm, l in rest], q_arr, "cast_place_rest",
                                           carries=gather_carry(-1) + [_allgather_carry([c_conv])])
    placed.update(zip(rest, rest_placed))
    gathered(-1, carried[:1])
    gathered1 = jnp.where(is_me, c_conv[None], carried[1][0].reshape(N_DEV, blk_rows, d))
    c_all = gathered1[:, 0, :]
    conv_full = jnp.concatenate([gathered1[2 * qb, SUBLANES:SUBLANES + n_conv_rows, :dq] for qb in range(N_CHIP)], axis=1)

    b_mod_shard = lax.dynamic_slice_in_dim(b_mod, q_me * mq, mq, axis=1)
    mod_part = _mod_forward(c_all, w_mod, b_mod_shard, "mod_forward")
    gathered2 = _all_gather_small(mod_part, "gather_mod").reshape(N_DEV, n_layer, N_DEV, mq)
    mod_rows = jnp.concatenate(
        [lax.dynamic_index_in_dim(gathered2[2 * qb], me_dev, axis=1, keepdims=False) for qb in range(N_CHIP)], axis=1)
    mods = [jnp.pad(mod_rows[l].reshape(n_mod, d), ((0, SUBLANES - n_mod), (0, 0))) for l in range(n_layer)]

    vecs = []
    for l in range(n_layer):
        base = l * (ka + kb)
        rows = [g_pre_mix[l], g_post_mix[l], conv_a_b[l], conv_b_b[l], b_gate_r[l], b_gate_i[l], lru_lambda[l],
                g_pre_mlp[l], g_post_mlp[l]]
        vecs.append(jnp.concatenate([jnp.stack(rows, axis=0), conv_full[base:base + ka + kb]], axis=0))

    wgs =[jnp.concatenate([w_gate_r[l], w_gate_i[l]], axis=-1).astype(BF16) for l in range(n_layer)]

    xs = x[0]
    saved = []
    for l in range(n_layer):
        wl = wfull[l]
        (h, proj, dgel), carried = _norm_proj(xs, mods[l], vecs[l], wl["w_in"], f"norm_proj_{l}", gather_carry(3 * l))
        gathered(3 * l, carried)
        (x1, conva, xb2, hh, ya, yb, pa, pb, mm, yy, gr, ggi, ga, gmult), carried = _mixer_forward(
            xs, proj, mods[l], vecs[l], wgs[l], wl["w_a_out"], wl["w_b_out"], wl["w_o"], f"mixer_forward_{l}",
            gather_carry(3 * l + 1))
        gathered(3 * l + 1, carried)
        (x2, h2, up, y2), carried = _mlp_forward(x1, mods[l], vecs[l], wl["w_mlp_up"], wl["w_mlp_down"],
                                                 f"mlp_forward_{l}", gather_carry(3 * l + 2))
        gathered(3 * l + 2, carried)
        saved.append(dict(x=xs, h=h, proj=proj, x1=x1, conva=conva, xb2=xb2, hh=hh, ya=ya, yb=yb, pa=pa, pb=pb, m=mm,
                          y=yy, r=gr, gi=ggi, a=ga, mult=gmult, dgel=dgel, h2=h2, up=up, y2=y2))
        xs = x2
    dxs, loss_tile = _loss_head(xs, loss_target[0], "loss_head")
    loss = lax.psum(loss_tile[0, 0], ("x", "y", "c"))

    chips_q = [q_me ^ 2, q_me ^ 1, q_me ^ 3]
    pf = jnp.stack([mc, q_me] + chips_q).astype(jnp.int32)
    rs = dict(grad={}, landed={}, to_send={}, from_chips={}, out={})
    to_exchange, to_scatter, to_join, to_gather = [], [], [], []
    small_own, small_all = {}, {}

    def ride(call, what, name=None):
        ex = list(to_exchange) if "x" in what else []
        sc = list(to_scatter) if "s" in what else []
        ga = list(to_gather) if "g" in what else []
        jn = []
        for key in (to_join if "j" in what else []):
            if key[0] not in [k[0] for k in jn]:
                jn.append(key)
        carries = []
        if ex:
            carries.append(_exchange_carry([rs["grad"][k] for k in ex]))
        if sc:
            carries.append(_scatter_carry([rs["to_send"][k] for k in sc]))
        if jn:
            carries.append(_join_carry([rs["out"][k[0]] for k in jn], [k[1] for k in jn]))
        if ga:
            carries.append(_allgather_carry([small_own[k] for k in ga]))
        if call is None:
            carried = _run_carries(carries, name) if carries else []
            res = None
        else:
            res, carried = call(carries)
        carried = list(carried)
        if ex:
            for k, ld in zip(ex, carried.pop(0)):
                to_exchange.remove(k)
                rs["landed"][k] = ld
                rs["to_send"][k] = _add_sibling_half(rs["grad"][k], ld, pf, f"rs_add_sibling_{k[0]}_{k[1]}")
                to_scatter.append(k)
        if sc:
            for k, fc in zip(sc, carried.pop(0)):
                to_scatter.remove(k)
                rs["out"][k[0]] = _add_chips(rs["grad"][k], rs["landed"][k], fc, pf, rs["out"].get(k[0]), k[1], n_layer,
                                             f"rs_add_chips_{k[0]}_{k[1]}")
                to_join.append(k)
        if jn:
            for k, o in zip(jn, carried.pop(0)):
                to_join.remove(k)
                rs["out"][k[0]] = o
        if ga:
            for k, o in zip(ga, carried.pop(0)):
                to_gather.remove(k)
                small_all[k] = o
        return res

    def gather_small(key, parts):
        small_own[key] = parts[0] if len(parts) == 1 else jnp.concatenate(parts, axis=0)
        to_gather.append(key)

    def ready(nm, l, g):
        rs["grad"][(nm, l)] = g
        to_exchange.append((nm, l))

    rowblk = lambda t: t.reshape(N_CHIP, t.shape[1] // N_CHIP, t.shape[2])
    small1_prev = None
    for l in reversed(range(n_layer)):
        wl, sv = wfull[l], saved[l]
        dx1, dy2, dup, act, small3 = ride(lambda cr: _mlp_backward(
            dxs, sv["x1"], sv["y2"], sv["up"], mods[l], vecs[l], wl["w_mlp_up"], wl["w_mlp_down"], f"mlp_backward_{l}", cr), "xsjg")
        ready("w_mlp_up", l, _weight_grad(sv["h2"], dup, f"grad_w_mlp_up_{l}", col_blocks=N_CHIP)[0])
        g_down = ride(lambda cr: _weight_grad(act, dy2, f"grad_w_mlp_down_{l}", carries=cr), "x")
        ready("w_mlp_down", l, rowblk(g_down))
        ride(None, "x", f"rs_exchange_down_{l}")
        dproj, dy, dpa, dpb, small2, dwg = ride(lambda cr: _mixer_backward(
            dx1, sv["proj"], sv["conva"], sv["xb2"], sv["hh"], sv["pa"], sv["pb"], sv["y"],
            sv["r"], sv["gi"], sv["a"], sv["mult"], sv["dgel"], mods[l], vecs[l], wgs[l],
            wl["w_a_out"], wl["w_b_out"], wl["w_o"], f"mixer_backward_{l}", cr), "xsjg")
        gather_small(("late", l, "s"), ([small1_prev] if small1_prev is not None else []) + [small2, small3])
        gather_small(("late", l, "w"), [dwg.reshape(2 * bw, d).astype(BF16)])
        ready("w_a_out", l, rowblk(_weight_grad(sv["ya"], dpa, f"grad_w_a_out_{l}")[0]))
        g_b = ride(lambda cr: _weight_grad(sv["yb"], dpb, f"grad_w_b_out_{l}", carries=cr), "x")
        ready("w_b_out", l, rowblk(g_b))
        g_o = ride(lambda cr: _weight_grad(sv["m"], dy, f"grad_w_o_{l}", carries=cr), "x")
        ready("w_o", l, rowblk(g_o))
        g_in = ride(lambda cr: _weight_grad(sv["h"], dproj, f"grad_w_in_{l}", col_blocks=N_CHIP, carries=cr), "xsg")
        ready("w_in", l, g_in)
        if l == 0:
            ride(None, "x", "rs_exchange_last")
        dxs, small1_prev = ride(lambda cr: _proj_backward(dproj, dx1, sv["x"], mods[l], vecs[l], wl["w_in"],
                                                          f"proj_backward_{l}", cr), "xsjg")
    grad_x = dxs[None]
    gather_small(("last", 0, "s"), [small1_prev])

    tail = 0
    while to_exchange or to_scatter or to_join or to_gather:
        ride(None, "xsjg", f"rs_tail_{tail}")
        tail += 1

    sums ={k: _sum_devices(small_all[k], small_own[k], me_arr, f"sum_small_{k[0]}_{k[1]}_{k[2]}") for k in small_own}

    def rows_of(l, part):
        if part == 0:
            return (("late", l - 1, "s"), 0) if l >= 1 else (("last", 0, "s"), 0)
        if part == 3:
            return ("late", l, "w"), 0
        base = SUBLANES if l < n_layer - 1 else 0
        return ("late", l, "s"), base + (0, 0, 2 * SUBLANES)[part]

    def summed(l, part, row, n_rows=1):
        key, base = rows_of(l, part)
        return sums[key][base + row:base + row + n_rows]

    def per_device(l, part, row):
        key, base = rows_of(l, part)
        own = small_own[key]
        others = small_all[key].reshape((N_DEV,) + own.shape)
        return jnp.where(is_me, own[None, base + row:base + row + 1], others[:, base + row:base + row + 1])

    mod_rows = [(0, SB1_DSH), (0, SB1_DSC), (1, SB2_DGT), (2, SB3_DSH), (2, SB3_DSC), (2, SB3_DGT)]
    dmod_all = jnp.stack([jnp.concatenate([per_device(l, p, r)[:, 0, :] for p, r in mod_rows], axis=1)
                          for l in range(n_layer)], axis=0)
    o1, o2, o3, o4 = 0, SUBLANES, 3 * SUBLANES, 4 * SUBLANES
    small_sum = jnp.stack([jnp.concatenate([summed(l, 0, 0, SUBLANES), summed(l, 1, 0, 2 * SUBLANES),
                                            summed(l, 2, 0, SUBLANES), summed(l, 3, 0, 2 * bw)], axis=0)
                           for l in range(n_layer)], axis=0)
    mod_rows_of = [o1 + SB1_DSH, o1 + SB1_DSC, o2 + SB2_DGT, o3 + SB3_DSH, o3 + SB3_DSC, o3 + SB3_DGT]
    grads = {}
    grads["w_mod"] = _mod_backward(c_all.T, lax.dynamic_slice_in_dim(dmod_all, q_me * mq, mq, axis=2), "mod_backward")
    grads["b_mod"] = jnp.concatenate([small_sum[:, k, :] for k in mod_rows_of], axis=1)
    grads["g_pre_mix"] = small_sum[:, o1 + SB1_DG_PRE]
    grads["g_post_mix"] = small_sum[:, o2 + SB2_DG_POST]
    grads["conv_a_w"] = lax.dynamic_slice_in_dim(small_sum[:, o2 + SB2_DWA:o2 + SB2_DWA + ka], q_me * dq, dq, axis=2)
    grads["conv_a_b"] = small_sum[:, o2 + SB2_DBA]
    grads["conv_b_w"] = lax.dynamic_slice_in_dim(small_sum[:, o2 + SB2_DWB:o2 + SB2_DWB + kb], q_me * dq, dq, axis=2)
    grads["conv_b_b"] = small_sum[:, o2 + SB2_DBB]
    grads["lru_lambda"] = small_sum[:, o2 + SB2_DLAM]
    grads["b_gate_r"] = small_sum[:, o2 + SB2_DBR]
    grads["b_gate_i"] = small_sum[:, o2 + SB2_DBI]
    grads["g_pre_mlp"] = small_sum[:, o3 + SB3_DG_PRE]
    grads["g_post_mlp"] = small_sum[:, o3 + SB3_DG_POST]
    dwg_sum = small_sum[:, o4:].reshape(n_layer, n_head, bw, 2 * bw)
    grads["w_gate_r"] = dwg_sum[..., :bw]
    grads["w_gate_i"] = dwg_sum[..., bw:]

    for nm in big_names:
        grads[nm] = rs["out"][nm].reshape(weights[nm].shape)

    deltas, new_m, new_v = {}, {}, {}
    for nm in names:
        res = _adamw(weights[nm], grads[nm], mom1[nm], mom2[nm], f"adamw_{nm}", copy_grad=nm in big_names)
        deltas[nm], new_m[nm], new_v[nm] = res[:3]
        if nm in big_names:
            grads[nm] = res[3]
    return (loss, grad_x, *[grads[nm] for nm in names], *[deltas[nm] for nm in names],
            *[new_m[nm] for nm in names], *[new_v[nm] for nm in names])
```

```python
import jax
import jax.numpy as jnp
from jax import lax
from jax.experimental import pallas as pl
from jax.experimental.pallas import tpu as pltpu

F32 = jnp.float32
BF16 = jnp.bfloat16
MESH = pl.DeviceIdType.MESH

EPS = 1e-6
LRU_C = 8.0
N_CHIP = 4
N_DEV = 8
ADAM_LR = 0.001
ADAM_B1 = 0.9
ADAM_B2 = 0.999
ADAM_EPS = 1e-08
ADAM_WD = 0.01
ADAM_STEP = 10

VMEM_LIMIT_BYTES = 56 * 1024 * 1024
SUBLANES = 8
LANES = 128
TOKENS_MATMUL_TILE = 512
TOKENS_MIXER_TILE = 256
GELU_K0 = 0.7978845608028654
GELU_K1 = 0.044715

V_G_PRE_MIX, V_G_POST_MIX, V_CONV_A_B, V_CONV_B_B, V_B_GATE_R, V_B_GATE_I, V_LAMBDA, V_G_PRE_MLP, V_G_POST_MLP = range(9)
V_CONV_A_W = 9
V_CONV_B_W = 12
M_SH_M, M_SC_M, M_GT_M, M_SH_F, M_SC_F, M_GT_F = range(6)


def _cparams(n_grid=0):
    sem = ("arbitrary",) * n_grid if n_grid else None
    return pltpu.CompilerParams(dimension_semantics=sem, vmem_limit_bytes=VMEM_LIMIT_BYTES)


def _full(shape):
    return pl.BlockSpec(shape, lambda *_: (0,) * len(shape))


def _dot(a, b):
    return jnp.dot(a, b, preferred_element_type=F32)


def _dot_tb(a, b):
    return lax.dot_general(a, b, (((1,), (1,)), ((), ())), preferred_element_type=F32)


def _dot_ta(a, b):
    return lax.dot_general(a, b, (((0,), (0,)), ((), ())), preferred_element_type=F32)


def _sigmoid(x):
    return 1.0 / (1.0 + jnp.exp(-x))


def _softplus(x):
    return jnp.maximum(x, 0.0) + jnp.log1p(jnp.exp(-jnp.abs(x)))


def _neg_expm1(x):
    series = -x * (1.0 + 0.5 * x * (1.0 + (x / 3.0) * (1.0 + 0.25 * x)))
    return jnp.where(x > -1e-2, series, 1.0 - jnp.exp(x))


def _gelu_and_grad(x):
    x2 = x * x
    s = _sigmoid(x * (2.0 * GELU_K0 + (2.0 * GELU_K0 * GELU_K1) * x2))
    gel = x * s
    return gel, s + gel * (1.0 - s) * (2.0 * GELU_K0 + (6.0 * GELU_K0 * GELU_K1) * x2)


def _rms(x):
    r = lax.rsqrt(jnp.mean(x * x, axis=-1, keepdims=True) + EPS)
    return x * r, r


def _rms_bwd(dxn, xn, r):
    return r * (dxn - xn * jnp.mean(dxn * xn, axis=-1, keepdims=True))


def _colsum(x):
    return jnp.sum(x, axis=0, keepdims=True)


def _rows(t, w):
    return lax.broadcasted_iota(jnp.int32, (t, w), 0)


def _shift_down(x, k, prev8):
    t, w = x.shape
    rolled = pltpu.roll(x, k, 0)
    head = jnp.where(_rows(SUBLANES, w) < k, pltpu.roll(prev8, k, 0), rolled[:SUBLANES])
    return jnp.concatenate([head, rolled[SUBLANES:]], axis=0)


def _shift_up(x, k, next8):
    t, w = x.shape
    rolled = pltpu.roll(x, t - k, 0)
    tail = jnp.where(_rows(SUBLANES, w) >= SUBLANES - k, pltpu.roll(next8, SUBLANES - k, 0), rolled[t - SUBLANES:])
    return jnp.concatenate([rolled[:t - SUBLANES], tail], axis=0)


SCAN_GROUP = 16


def _scan_steps(a, b, group, reverse):
    t, w = a.shape
    pos = _rows(t, w) & (group - 1)
    s = 1
    while s < group:
        keep = (pos < group - s) if reverse else (pos >= s)
        shift = (t - s) if reverse else s
        b = b + a * jnp.where(keep, pltpu.roll(b, shift, 0), 0.0)
        a = a * jnp.where(keep, pltpu.roll(a, shift, 0), 1.0)
        s *= 2
    return b, a


def _scan_two_level(a, b, carry_row, a_buf, b_buf, c_buf, reverse):
    t, w = a.shape
    grp = SCAN_GROUP
    n_grp = t // grp
    h_loc, a_cum = _scan_steps(a, b, grp, reverse)
    end = 0 if reverse else grp - 1
    a_end, h_end = [], []
    for j in range(w // LANES):
        a_buf[j] = a_cum[:, j * LANES:(j + 1) * LANES]
        b_buf[j] = h_loc[:, j * LANES:(j + 1) * LANES]
        a_end.append(a_buf[j, pl.ds(end, n_grp, stride=grp), :])
        h_end.append(b_buf[j, pl.ds(end, n_grp, stride=grp), :])
    a_end = jnp.concatenate(a_end, axis=1)
    h_end = jnp.concatenate(h_end, axis=1)
    h_grp, a_grp = _scan_steps(a_end, h_end, n_grp, reverse)
    h_grp = h_grp + a_grp * carry_row
    rows = _rows(n_grp, w)
    if reverse:
        entering = jnp.where(rows == n_grp - 1, carry_row, pltpu.roll(h_grp, n_grp - 1, 0))
    else:
        entering = jnp.where(rows == 0, carry_row, pltpu.roll(h_grp, 1, 0))
    c_buf[...] = entering
    out = [h_loc[g * grp:(g + 1) * grp] + a_cum[g * grp:(g + 1) * grp] * c_buf[g:g + 1, :] for g in range(n_grp)]
    return jnp.concatenate(out, axis=0)


def _row_tile(rows, cols, itemsize=4, target_bytes=2 * 1024 * 1024):
    if rows * cols * itemsize <= target_bytes or rows % SUBLANES:
        return rows
    t = max(SUBLANES, (target_bytes // (cols * itemsize)) // SUBLANES * SUBLANES)
    while rows % t:
        t -= SUBLANES
    return t


def _place():
    return lax.axis_index("x"), lax.axis_index("y"), lax.axis_index("c")


def _other_chips(x, y):
    chips = [(1 - x, y), (x, 1 - y), (1 - x, 1 - y)]
    return chips, [2 * cx + cy for cx, cy in chips]


def _all_gather_small(block, name):
    m_per, n = block.shape

    def body(x_ref, out_ref, send_sems, recv_sems, local_sem):
        x, y, c = _place()
        me, sibling = (x, y, c), (x, y, 1 - c)
        chips, _ = _other_chips(x, y)

        def rows(px, py, pc):
            return out_ref.at[pl.ds((4 * px + 2 * py + pc) * m_per, m_per), :]

        def copy(k, blk, to, src=None):
            return pltpu.make_async_remote_copy(
                src_ref=rows(*blk) if src is None else src, dst_ref=rows(*blk),
                send_sem=send_sems.at[k], recv_sem=recv_sems.at[k], device_id=to, device_id_type=MESH)

        mine = pltpu.make_async_copy(x_ref, rows(*me), local_sem)
        mine.start()
        first = [copy(0, me, sibling, src=x_ref)]
        first += [copy(1 + j, me, (*chip, c), src=x_ref) for j, chip in enumerate(chips)]
        for cp in first:
            cp.start()
        passed = [copy(4 + j, (*chip, c), sibling) for j, chip in enumerate(chips)]
        for j, chip in enumerate(chips):
            copy(1 + j, (*chip, c), me).wait_recv()
            passed[j].start()
        copy(0, sibling, me).wait_recv()
        for j, chip in enumerate(chips):
            copy(4 + j, (*chip, 1 - c), me).wait_recv()
        for cp in first + passed:
            cp.wait_send()
        mine.wait()

    return pl.pallas_call(
        body, name=name,
        out_shape=jax.ShapeDtypeStruct((N_DEV * m_per, n), block.dtype),
        in_specs=[pl.BlockSpec(memory_space=pltpu.VMEM)],
        out_specs=pl.BlockSpec(memory_space=pltpu.VMEM),
        scratch_shapes=[pltpu.SemaphoreType.DMA((7,)), pltpu.SemaphoreType.DMA((7,)), pltpu.SemaphoreType.DMA],
        compiler_params=pltpu.CompilerParams(vmem_limit_bytes=VMEM_LIMIT_BYTES),
    )(block)


def _cast_place(w, layer, q_arr, name):
    _, r, cols = w.shape
    tr = _row_tile(r, cols)

    def body(q_ref, w_ref, o_ref):
        o_ref[...] = w_ref[...].astype(BF16)

    return pl.pallas_call(
        body, name=name,
        out_shape=jax.ShapeDtypeStruct((N_CHIP, r, cols), BF16),
        grid_spec=pltpu.PrefetchScalarGridSpec(
            num_scalar_prefetch=1, grid=(r // tr,),
            in_specs=[pl.BlockSpec((1, tr, cols), lambda i, q_ref: (layer, i, 0))],
            out_specs=pl.BlockSpec((1, tr, cols), lambda i, q_ref: (q_ref[0], i, 0))),
        compiler_params=_cparams(1),
    )(q_arr, w)


class _Carry:
    def __init__(self, ins, out_shapes, aliases, sem_shapes, start, finish, mid=None, mid_frac=0.85):
        self.ins, self.out_shapes, self.aliases, self.sem_shapes = list(ins), list(out_shapes), dict(aliases), list(sem_shapes)
        self.start, self.mid, self.finish, self.mid_frac = start, mid, finish, mid_frac


def _pcall(body, *, name, grid, in_specs, out_specs, out_shape, args, scratch_shapes=(), carries=(), prefetch=()):
    in_specs, out_specs, out_shape = list(in_specs), list(out_specs), list(out_shape)
    scratch_shapes, args = list(scratch_shapes), list(args)
    n_in, n_out, n_scr, n_pre = len(in_specs), len(out_shape), len(scratch_shapes), len(prefetch)
    steps = 1
    for g in grid:
        steps *= g
    any_spec = pl.BlockSpec(memory_space=pl.ANY)
    aliases = {}
    spans = []
    for cr in carries:
        spans.append((len(args), len(out_shape), len(scratch_shapes)))
        for a, b in cr.aliases.items():
            aliases[n_pre + len(args) + a] = len(out_shape) + b
        args += cr.ins
        in_specs += [any_spec] * len(cr.ins)
        out_shape += cr.out_shapes
        out_specs += [any_spec] * len(cr.out_shapes)
        scratch_shapes += cr.sem_shapes
    n_all_in = len(args)
    n_all_out = len(out_shape)

    def wrapped(*refs):
        pre, refs = refs[:n_pre], refs[n_pre:]
        ins, outs, scr = refs[:n_all_in], refs[n_all_in:n_all_in + n_all_out], refs[n_all_in + n_all_out:]
        parts = [(cr, ins[a:a + len(cr.ins)], outs[b:b + len(cr.out_shapes)], scr[s:s + len(cr.sem_shapes)])
                 for cr, (a, b, s) in zip(carries, spans)]
        lin = 0
        for ax, g in enumerate(grid):
            lin = lin * g + pl.program_id(ax)

        def at(step, fn):
            if steps == 1:
                fn()
            else:
                pl.when(lin == step)(fn)

        def start_all():
            for cr, ci, co, cs in parts:
                cr.start(ci, co, cs)

        def finish_all():
            for cr, ci, co, cs in parts:
                cr.finish(ci, co, cs)

        if parts:
            at(0, start_all)
        body(*pre, *ins[:n_in], *outs[:n_out], *scr[:n_scr])
        for cr, ci, co, cs in parts:
            if cr.mid is not None:
                at(min(steps - 1, int(steps * cr.mid_frac)), lambda cr=cr, ci=ci, co=co, cs=cs: cr.mid(ci, co, cs))
        if parts:
            at(steps - 1, finish_all)

    if n_pre:
        res = pl.pallas_call(
            wrapped, name=name, out_shape=out_shape,
            grid_spec=pltpu.PrefetchScalarGridSpec(num_scalar_prefetch=n_pre, grid=tuple(grid), in_specs=in_specs,
                                                   out_specs=out_specs, scratch_shapes=scratch_shapes),
            input_output_aliases=aliases, compiler_params=_cparams(len(grid)),
        )(*prefetch, *args)
    else:
        res = pl.pallas_call(
            wrapped, name=name, grid=tuple(grid), out_shape=out_shape, in_specs=in_specs, out_specs=out_specs,
            scratch_shapes=scratch_shapes, input_output_aliases=aliases, compiler_params=_cparams(len(grid)),
        )(*args)
    res = list(res)
    return res[:n_out], [res[b:b + len(cr.out_shapes)] for cr, (_, b, _) in zip(carries, spans)]


def _run_carries(carries, name):
    return _pcall(lambda: None, name=name, grid=(), in_specs=[], out_specs=[], out_shape=[], args=[], carries=carries)[1]


CAST_STEPS = 8


def _cast_place_all(shards, q_arr, name, carries=()):
    n = len(shards)

    def body(q_ref, *refs):
        for k in range(n):
            refs[n + k][...] = refs[k][...].astype(BF16)

    def spec_in(k):
        w, layer = shards[k]
        return pl.BlockSpec((1, w.shape[1] // CAST_STEPS, w.shape[2]), lambda i, q_ref: (layer, i, 0))

    def spec_out(k):
        w, _ = shards[k]
        return pl.BlockSpec((1, w.shape[1] // CAST_STEPS, w.shape[2]), lambda i, q_ref: (q_ref[0], i, 0))

    return _pcall(
        body, name=name, grid=(CAST_STEPS,),
        out_shape=[jax.ShapeDtypeStruct((N_CHIP,) + w.shape[1:], BF16) for w, _ in shards],
        in_specs=[spec_in(k) for k in range(n)], out_specs=[spec_out(k) for k in range(n)],
        args=[w for w, _ in shards], carries=carries, prefetch=[q_arr])


def _gather_carry(bufs, mid_frac=0.85):
    n = len(bufs)

    def copies(o_refs, sems):
        send_sems, recv_sems = sems
        x, y, c = _place()
        q = 2 * x + y
        sibling = (x, y, 1 - c)
        chips, qs = _other_chips(x, y)

        def half(w, shard, pc):
            rh = bufs[w].shape[1] // 2
            return o_refs[w].at[shard, pl.ds(pc * rh, rh), :]

        def over_ici(w, j, shard):
            blk = half(w, shard, c)
            return pltpu.make_async_remote_copy(
                src_ref=blk, dst_ref=blk, send_sem=send_sems.at[w, j], recv_sem=recv_sems.at[w, j],
                device_id=(*chips[j], c), device_id_type=MESH)

        def to_sibling(w, j, pc):
            blk = half(w, qs[j], pc)
            return pltpu.make_async_remote_copy(
                src_ref=blk, dst_ref=blk, send_sem=send_sems.at[w, 3 + j], recv_sem=recv_sems.at[w, 3 + j],
                device_id=sibling, device_id_type=MESH)

        return q, c, qs, over_ici, to_sibling

    pairs = [(w, j) for w in range(n) for j in range(3)]

    def start(i_refs, o_refs, sems):
        q, _, _, over_ici, _ = copies(o_refs, sems)
        for w, j in pairs:
            over_ici(w, j, q).start()

    def mid(i_refs, o_refs, sems):
        _, c, qs, over_ici, to_sibling = copies(o_refs, sems)
        for w, j in pairs:
            over_ici(w, j, qs[j]).wait_recv()
            to_sibling(w, j, c).start()

    def finish(i_refs, o_refs, sems):
        q, c, _, over_ici, to_sibling = copies(o_refs, sems)
        for w, j in pairs:
            to_sibling(w, j, 1 - c).wait_recv()
        for w, j in pairs:
            over_ici(w, j, q).wait_send()
            to_sibling(w, j, c).wait_send()

    return _Carry(bufs, [jax.ShapeDtypeStruct(b.shape, b.dtype) for b in bufs], {w: w for w in range(n)},
                  [pltpu.SemaphoreType.DMA((n, 6)), pltpu.SemaphoreType.DMA((n, 6))], start, finish, mid, mid_frac)


def _exchange_carry(grads):
    n = len(grads)

    def copies(g_refs, l_refs, sems):
        send_sems, recv_sems = sems
        x, y, c = _place()
        out = []
        for w in range(n):
            rh = grads[w].shape[1] // 2
            out.append(pltpu.make_async_remote_copy(
                src_ref=g_refs[w].at[:, pl.ds((1 - c) * rh, rh), :], dst_ref=l_refs[w],
                send_sem=send_sems.at[w], recv_sem=recv_sems.at[w], device_id=(x, y, 1 - c), device_id_type=MESH))
        return out

    def start(g_refs, l_refs, sems):
        for cp in copies(g_refs, l_refs, sems):
            cp.start()

    def finish(g_refs, l_refs, sems):
        for cp in copies(g_refs, l_refs, sems):
            cp.wait()

    return _Carry(grads, [jax.ShapeDtypeStruct((N_CHIP, g.shape[1] // 2, g.shape[2]), g.dtype) for g in grads], {},
                  [pltpu.SemaphoreType.DMA((n,)), pltpu.SemaphoreType.DMA((n,))], start, finish)


def _scatter_carry(sums):
    n = len(sums)

    def copies(s_refs, l_refs, sems):
        send_sems, recv_sems = sems
        x, y, c = _place()
        chips, _ = _other_chips(x, y)
        return [pltpu.make_async_remote_copy(
            src_ref=s_refs[w].at[j], dst_ref=l_refs[w].at[j], send_sem=send_sems.at[w, j], recv_sem=recv_sems.at[w, j],
            device_id=(*chips[j], c), device_id_type=MESH) for w in range(n) for j in range(3)]

    def start(s_refs, l_refs, sems):
        for cp in copies(s_refs, l_refs, sems):
            cp.start()

    def finish(s_refs, l_refs, sems):
        for cp in copies(s_refs, l_refs, sems):
            cp.wait()

    return _Carry(sums, [jax.ShapeDtypeStruct(s.shape, s.dtype) for s in sums], {},
                  [pltpu.SemaphoreType.DMA((n, 3)), pltpu.SemaphoreType.DMA((n, 3))], start, finish)


def _join_carry(outs, layers):
    n = len(outs)

    def copy(o_refs, sems, w, mine):
        send_sems, recv_sems = sems
        x, y, c = _place()
        r = outs[w].shape[1]
        rows = o_refs[w].at[layers[w], pl.ds((c if mine else 1 - c) * (r // 2), r // 2), :]
        return pltpu.make_async_remote_copy(
            src_ref=rows, dst_ref=rows, send_sem=send_sems.at[w], recv_sem=recv_sems.at[w],
            device_id=(x, y, 1 - c), device_id_type=MESH)

    def start(i_refs, o_refs, sems):
        for w in range(n):
            copy(o_refs, sems, w, True).start()

    def finish(i_refs, o_refs, sems):
        for w in range(n):
            copy(o_refs, sems, w, True).wait_send()
        for w in range(n):
            copy(o_refs, sems, w, False).wait_recv()

    return _Carry(outs, [jax.ShapeDtypeStruct(o.shape, o.dtype) for o in outs], {w: w for w in range(n)},
                  [pltpu.SemaphoreType.DMA((n,)), pltpu.SemaphoreType.DMA((n,))], start, finish)


PF_C, PF_Q, PF_QS = 0, 1, 2


def _add_sibling_half(g, landed, pf, name):
    _, r, cols = g.shape
    rh = r // 2
    tr = _row_tile(rh, cols)
    nr = rh // tr

    def body(pf_ref, g_ref, l_ref, o_ref):
        o_ref[...] = (g_ref[...] + l_ref[...]).astype(BF16)

    return pl.pallas_call(
        body, name=name,
        out_shape=jax.ShapeDtypeStruct((3, rh, cols), BF16),
        grid_spec=pltpu.PrefetchScalarGridSpec(
            num_scalar_prefetch=1, grid=(3, nr),
            in_specs=[pl.BlockSpec((1, tr, cols), lambda j, i, pf_ref: (pf_ref[PF_QS + j], pf_ref[PF_C] * nr + i, 0)),
                      pl.BlockSpec((1, tr, cols), lambda j, i, pf_ref: (pf_ref[PF_QS + j], i, 0))],
            out_specs=pl.BlockSpec((1, tr, cols), lambda j, i, pf_ref: (j, i, 0))),
        compiler_params=_cparams(2),
    )(pf, g, landed)


def _add_chips(g, landed, from_chips, pf, prev, layer, n_layer, name):
    _, r, cols = g.shape
    rh = r // 2
    tr = _row_tile(rh, cols)
    nr = rh // tr

    def body(pf_ref, g_ref, l_ref, f_ref, *rest):
        o_ref = rest[-1]
        acc = g_ref[0] + l_ref[0]
        for j in range(3):
            acc = acc + f_ref[j].astype(F32)
        o_ref[0] = acc

    in_specs = [pl.BlockSpec((1, tr, cols), lambda i, pf_ref: (pf_ref[PF_Q], pf_ref[PF_C] * nr + i, 0)),
                pl.BlockSpec((1, tr, cols), lambda i, pf_ref: (pf_ref[PF_Q], i, 0)),
                pl.BlockSpec((3, tr, cols), lambda i, pf_ref: (0, i, 0))]
    args = [pf, g, landed, from_chips]
    aliases = {}
    if prev is not None:
        in_specs.append(pl.BlockSpec(memory_space=pl.ANY))
        args.append(prev)
        aliases = {4: 0}
    return pl.pallas_call(
        body, name=name,
        out_shape=jax.ShapeDtypeStruct((n_layer, r, cols), F32),
        grid_spec=pltpu.PrefetchScalarGridSpec(
            num_scalar_prefetch=1, grid=(nr,), in_specs=in_specs,
            out_specs=pl.BlockSpec((1, tr, cols), lambda i, pf_ref: (layer, pf_ref[PF_C] * nr + i, 0))),
        input_output_aliases=aliases,
        compiler_params=_cparams(1),
    )(*args)


def _allgather_carry(blocks):
    n = len(blocks)

    def copies(b_refs, o_refs, sems):
        send_sems, recv_sems = sems
        x, y, c = _place()
        chips, _ = _other_chips(x, y)

        def place(w, px, py, pc):
            m = blocks[w].shape[0]
            return o_refs[w].at[pl.ds((4 * px + 2 * py + pc) * m, m), :]

        def own_to(w, k, to):
            dst = place(w, x, y, c)
            return pltpu.make_async_remote_copy(src_ref=b_refs[w], dst_ref=dst, send_sem=send_sems.at[w, k],
                                                recv_sem=recv_sems.at[w, k], device_id=to, device_id_type=MESH)

        def landed_from(w, k, px, py, pc):
            blk = place(w, px, py, pc)
            return pltpu.make_async_remote_copy(src_ref=blk, dst_ref=blk, send_sem=send_sems.at[w, k],
                                                recv_sem=recv_sems.at[w, k], device_id=(x, y, 1 - c), device_id_type=MESH)

        return x, y, c, chips, own_to, landed_from

    def start(b_refs, o_refs, sems):
        x, y, c, chips, own_to, _ = copies(b_refs, o_refs, sems)
        for w in range(n):
            own_to(w, 0, (x, y, 1 - c)).start()
            for j, chip in enumerate(chips):
                own_to(w, 1 + j, (*chip, c)).start()

    def mid(b_refs, o_refs, sems):
        x, y, c, chips, _, landed_from = copies(b_refs, o_refs, sems)
        for w in range(n):
            for j, chip in enumerate(chips):
                landed_from(w, 1 + j, *chip, c).wait_recv()
                landed_from(w, 4 + j, *chip, c).start()

    def finish(b_refs, o_refs, sems):
        x, y, c, chips, own_to, landed_from = copies(b_refs, o_refs, sems)
        for w in range(n):
            landed_from(w, 0, x, y, 1 - c).wait_recv()
            for j, chip in enumerate(chips):
                landed_from(w, 4 + j, *chip, 1 - c).wait_recv()
            own_to(w, 0, (x, y, 1 - c)).wait_send()
            for j, chip in enumerate(chips):
                own_to(w, 1 + j, (*chip, c)).wait_send()
                landed_from(w, 4 + j, *chip, c).wait_send()

    return _Carry(blocks, [jax.ShapeDtypeStruct((N_DEV * b.shape[0], b.shape[1]), b.dtype) for b in blocks], {},
                  [pltpu.SemaphoreType.DMA((n, 7)), pltpu.SemaphoreType.DMA((n, 7))], start, finish, mid)


def _sum_devices(gathered, own, me_arr, name):
    m, n = own.shape
    tr = _row_tile(m, n, itemsize=own.dtype.itemsize, target_bytes=256 * 1024)
    nr = m // tr

    def body(me_ref, *refs):
        g_refs, own_ref, o_ref = refs[:N_DEV], refs[N_DEV], refs[N_DEV + 1]
        me = me_ref[0]
        acc = None
        for dev in range(N_DEV):
            term = jnp.where(me == dev, own_ref[...], g_refs[dev][...]).astype(F32)
            acc = term if acc is None else acc + term
        o_ref[...] = acc

    def dev_rows(dev):
        return pl.BlockSpec((tr, n), lambda i, me_ref: (dev * nr + i, 0))

    return pl.pallas_call(
        body, name=name,
        out_shape=jax.ShapeDtypeStruct((m, n), F32),
        grid_spec=pltpu.PrefetchScalarGridSpec(
            num_scalar_prefetch=1, grid=(nr,),
            in_specs=[dev_rows(dev) for dev in range(N_DEV)] + [pl.BlockSpec((tr, n), lambda i, me_ref: (i, 0))],
            out_specs=pl.BlockSpec((tr, n), lambda i, me_ref: (i, 0))),
        compiler_params=_cparams(1),
    )(me_arr, *([gathered] * N_DEV), own)


def _mod_forward(c_all, w_mod, b_mod_shard, name):
    n_layer, d, mq = w_mod.shape

    def body(c_ref, w_ref, b_ref, o_ref):
        cv = c_ref[...]
        o_ref[...] = _dot(cv * _sigmoid(cv), w_ref[0]) + b_ref[0]

    return pl.pallas_call(
        body, name=name, grid=(n_layer,),
        out_shape=jax.ShapeDtypeStruct((n_layer * N_DEV, mq), F32),
        in_specs=[_full((N_DEV, d)), pl.BlockSpec((1, d, mq), lambda l: (l, 0, 0)),
                  pl.BlockSpec((1, 1, mq), lambda l: (l, 0, 0))],
        out_specs=pl.BlockSpec((N_DEV, mq), lambda l: (l, 0)),
        compiler_params=_cparams(1),
    )(c_all, w_mod, b_mod_shard.reshape(n_layer, 1, mq))


def _mod_backward(c_all_t, dmod_shard, name):
    n_layer, _, mq = dmod_shard.shape
    d = c_all_t.shape[0]

    def body(c_ref, dm_ref, o_ref):
        cv = c_ref[...]
        o_ref[0] = _dot(cv * _sigmoid(cv), dm_ref[0])

    return pl.pallas_call(
        body, name=name, grid=(n_layer,),
        out_shape=jax.ShapeDtypeStruct((n_layer, d, mq), F32),
        in_specs=[_full((d, N_DEV)), pl.BlockSpec((1, N_DEV, mq), lambda l: (l, 0, 0))],
        out_specs=pl.BlockSpec((1, d, mq), lambda l: (l, 0, 0)),
        compiler_params=_cparams(1),
    )(c_all_t, dmod_shard)


def _norm_proj(x, mod, vec, w_in, name, carries=()):
    s, d = x.shape
    nq = w_in.shape[2]
    ts = min(TOKENS_MATMUL_TILE, s)

    def body(x_ref, mod_ref, vec_ref, w_ref, h_ref, p_ref, dgel_ref):
        xn, _ = _rms(x_ref[...])
        gm = vec_ref[V_G_PRE_MIX:V_G_PRE_MIX + 1, :] * (1.0 + mod_ref[M_SC_M:M_SC_M + 1, :])
        h = (xn * gm + mod_ref[M_SH_M:M_SH_M + 1, :]).astype(BF16)
        h_ref[...] = h
        for qb in range(N_CHIP):
            pq = _dot(h, w_ref[qb])
            for k in range(N_CHIP * nq // d):
                lo, hi = max(qb * nq, k * d), min((qb + 1) * nq, (k + 1) * d)
                if lo >= hi:
                    continue
                piece = pq[:, lo - qb * nq:hi - qb * nq]
                if k == 4:
                    piece, dgel = _gelu_and_grad(piece)
                    dgel_ref[:, lo - 4 * d:hi - 4 * d] = dgel.astype(BF16)
                elif k >= 5:
                    piece = _sigmoid(piece)
                p_ref[:, lo:hi] = piece.astype(BF16)

    tile = pl.BlockSpec((ts, d), lambda i: (i, 0))
    return _pcall(
        body, name=name, grid=(s // ts,),
        out_shape=[jax.ShapeDtypeStruct((s, d), BF16), jax.ShapeDtypeStruct((s, N_CHIP * nq), BF16),
                   jax.ShapeDtypeStruct((s, d), BF16)],
        in_specs=[tile, _full(mod.shape), _full(vec.shape), _full(w_in.shape)],
        out_specs=[tile, pl.BlockSpec((ts, N_CHIP * nq), lambda i: (i, 0)), tile],
        args=[x, mod, vec, w_in], carries=carries)


def _gate_pre(xb2_b, wg_ref, n_head, bw):
    zr, zi = [], []
    for hd in range(n_head):
        z = _dot(xb2_b[:, hd * bw:(hd + 1) * bw], wg_ref[hd])
        zr.append(z[:, :bw])
        zi.append(z[:, bw:])
    return jnp.concatenate(zr, axis=1), jnp.concatenate(zi, axis=1)


def _lru_coeffs(xb2, wg_ref, vec_ref, n_head, bw):
    zr, zi = _gate_pre(xb2.astype(BF16), wg_ref, n_head, bw)
    r = _sigmoid(zr + vec_ref[V_B_GATE_R:V_B_GATE_R + 1, :])
    gi = _sigmoid(zi + vec_ref[V_B_GATE_I:V_B_GATE_I + 1, :])
    sp = _softplus(-vec_ref[V_LAMBDA:V_LAMBDA + 1, :])
    log_a = (-LRU_C) * r * sp
    a = jnp.exp(log_a)
    mult = jnp.sqrt(_neg_expm1(2.0 * log_a))
    return r, gi, sp, a, mult


def _mixer_forward(x, proj, mod, vec, wg, w_a_out, w_b_out, w_o, name, carries=()):
    s, d = x.shape
    n_head, bw, _ = wg.shape
    ts = min(TOKENS_MIXER_TILE, s)

    def body(x_ref, p_ref, mod_ref, vec_ref, wg_ref, wa_ref, wb_ref, wo_ref,
             x1_ref, conva_ref, xb2_ref, hh_ref, ya_ref, yb_ref, pa_ref, pb_ref, m_ref, y_ref,
             r_ref, gi_ref, a_ref, mult_ref,
             cv_tail, xb_tail, h_last, a_buf, b_buf, c_buf):
        i = pl.program_id(0)

        @pl.when(i == 0)
        def _():
            cv_tail[...] = jnp.zeros_like(cv_tail)
            xb_tail[...] = jnp.zeros_like(xb_tail)
            h_last[...] = jnp.zeros_like(h_last)

        def seg(k):
            return p_ref[:, k * d:(k + 1) * d].astype(F32)

        def vrow(k):
            return vec_ref[k:k + 1, :]

        b_a, c_a, v_a, x_b, gel, sa, sb = (seg(k) for k in range(7))
        cv = c_a * v_a
        prev_cv = cv_tail[...]
        conv_a = (vrow(V_CONV_A_B) + vrow(V_CONV_A_W) * _shift_down(cv, 2, prev_cv)
                  + vrow(V_CONV_A_W + 1) * _shift_down(cv, 1, prev_cv) + vrow(V_CONV_A_W + 2) * cv)
        cv_tail[...] = cv[ts - SUBLANES:]
        y_a = b_a * conv_a
        prev_xb = xb_tail[...]
        xb2 = (vrow(V_CONV_B_B) + vrow(V_CONV_B_W) * _shift_down(x_b, 3, prev_xb)
               + vrow(V_CONV_B_W + 1) * _shift_down(x_b, 2, prev_xb)
               + vrow(V_CONV_B_W + 2) * _shift_down(x_b, 1, prev_xb) + vrow(V_CONV_B_W + 3) * x_b)
        xb_tail[...] = x_b[ts - SUBLANES:]
        r, gi, _, a, mult = _lru_coeffs(xb2, wg_ref, vec_ref, n_head, bw)
        r_ref[...] = r
        gi_ref[...] = gi
        a_ref[...] = a
        mult_ref[...] = mult
        hh = _scan_two_level(a, mult * gi * xb2, h_last[SUBLANES - 1:SUBLANES, :], a_buf, b_buf, c_buf, reverse=False)
        h_last[...] = hh[ts - SUBLANES:]
        y_b = hh * gel
        ya_b, yb_b = y_a.astype(BF16), y_b.astype(BF16)
        pa = _dot(ya_b, wa_ref[...])
        pb = _dot(yb_b, wb_ref[...])
        m = (sa * pa + sb * pb).astype(BF16)
        y = _dot(m, wo_ref[...])
        yn, _ = _rms(y)
        gg = mod_ref[M_GT_M:M_GT_M + 1, :] * vrow(V_G_POST_MIX)
        x1_ref[...] = x_ref[...] + yn * gg
        conva_ref[...] = conv_a.astype(BF16)
        xb2_ref[...] = xb2
        hh_ref[...] = hh
        ya_ref[...] = ya_b
        yb_ref[...] = yb_b
        pa_ref[...] = pa.astype(BF16)
        pb_ref[...] = pb.astype(BF16)
        m_ref[...] = m
        y_ref[...] = y.astype(BF16)

    tile = pl.BlockSpec((ts, d), lambda i: (i, 0))
    sd = lambda dt: jax.ShapeDtypeStruct((s, d), dt)
    return _pcall(
        body, name=name, grid=(s // ts,),
        out_shape=[sd(F32), sd(BF16), sd(F32), sd(F32), sd(BF16), sd(BF16), sd(BF16), sd(BF16), sd(BF16), sd(BF16),
                   sd(F32), sd(F32), sd(F32), sd(F32)],
        in_specs=[tile, pl.BlockSpec((ts, 7 * d), lambda i: (i, 0)), _full(mod.shape), _full(vec.shape),
                  _full(wg.shape), _full(w_a_out.shape), _full(w_b_out.shape), _full(w_o.shape)],
        out_specs=[tile] * 14,
        scratch_shapes=[pltpu.VMEM((SUBLANES, d), F32)] * 3 + [pltpu.VMEM((d // LANES, ts, LANES), F32)] * 2
                       + [pltpu.VMEM((ts // SCAN_GROUP, d), F32)],
        args=[x, proj, mod, vec, wg, w_a_out, w_b_out, w_o], carries=carries)


def _mlp_forward(x1, mod, vec, w_up, w_down, name, carries=(), target=None):
    s, d = x1.shape
    fq = w_up.shape[2]
    ts = min(TOKENS_MATMUL_TILE, s)

    def body(x_ref, *refs):
        if target is None:
            mod_ref, vec_ref, wu_ref, wd_ref, x2_ref, h2_ref, up_ref, y2_ref = refs
        else:
            t_ref, mod_ref, vec_ref, wu_ref, wd_ref, x2_ref, h2_ref, up_ref, y2_ref, loss_ref = refs
        x = x_ref[...]
        xn, _ = _rms(x)
        gm = vec_ref[V_G_PRE_MLP:V_G_PRE_MLP + 1, :] * (1.0 + mod_ref[M_SC_F:M_SC_F + 1, :])
        h2 = (xn * gm + mod_ref[M_SH_F:M_SH_F + 1, :]).astype(BF16)
        h2_ref[...] = h2
        y2 = jnp.zeros((ts, d), F32)
        for qb in range(N_CHIP):
            up = _dot(h2, wu_ref[qb])
            up_ref[:, qb * fq:(qb + 1) * fq] = up.astype(BF16)
            ru = jnp.maximum(up, 0.0)
            y2 = y2 + _dot((ru * ru).astype(BF16), wd_ref[qb])
        y2_ref[...] = y2.astype(BF16)
        yn, _ = _rms(y2)
        gg = mod_ref[M_GT_F:M_GT_F + 1, :] * vec_ref[V_G_POST_MLP:V_G_POST_MLP + 1, :]
        x2 = x + yn * gg
        if target is None:
            x2_ref[...] = x2
        else:
            @pl.when(pl.program_id(0) == 0)
            def _():
                loss_ref[...] = jnp.zeros_like(loss_ref)

            err = x2 - t_ref[...]
            x2_ref[...] = err * (1.0 / d)
            loss_ref[...] += jnp.sum(jnp.sum(err * err, axis=1, keepdims=True), axis=0, keepdims=True) * (0.5 / d)

    tile = pl.BlockSpec((ts, d), lambda i: (i, 0))
    last = target is not None
    return _pcall(
        body, name=name, grid=(s // ts,),
        out_shape=[jax.ShapeDtypeStruct((s, d), F32), jax.ShapeDtypeStruct((s, d), BF16),
                   jax.ShapeDtypeStruct((s, N_CHIP * fq), BF16), jax.ShapeDtypeStruct((s, d), BF16)]
                  + ([jax.ShapeDtypeStruct((SUBLANES, LANES), F32)] if last else []),
        in_specs=[tile] + ([tile] if last else []) + [_full(mod.shape), _full(vec.shape), _full(w_up.shape), _full(w_down.shape)],
        out_specs=[tile, tile, pl.BlockSpec((ts, N_CHIP * fq), lambda i: (i, 0)), tile]
                 + ([_full((SUBLANES, LANES))] if last else []),
        args=[x1] + ([target] if last else []) + [mod, vec, w_up, w_down], carries=carries)


SB3_DSH, SB3_DSC, SB3_DGT, SB3_DG_PRE, SB3_DG_POST = range(5)
SB1_DSH, SB1_DSC, SB1_DG_PRE = range(3)
(SB2_DGT, SB2_DG_POST, SB2_DWA, SB2_DBA, SB2_DWB, SB2_DBB, SB2_DLAM, SB2_DBR, SB2_DBI) = (0, 1, 2, 5, 6, 10, 11, 12, 13)


def _mlp_backward(dx2, x1, y2, up, mod, vec, w_up, w_down, name, carries=()):
    s, d = dx2.shape
    fq = w_up.shape[2]
    ts = min(TOKENS_MIXER_TILE, s)
    n_t = s // ts

    def body(dx2_ref, x_ref, y2_ref, up_ref, mod_ref, vec_ref, wu_ref, wd_ref,
             dx1_ref, dy2_ref, dup_ref, act_ref, small_ref):
        i = pl.program_id(0)

        @pl.when(i == 0)
        def _():
            small_ref[...] = jnp.zeros_like(small_ref)

        dout = dx2_ref[...]
        y2n, ry = _rms(y2_ref[...].astype(F32))
        g_post = vec_ref[V_G_POST_MLP:V_G_POST_MLP + 1, :]
        gt = mod_ref[M_GT_F:M_GT_F + 1, :]
        dgg = _colsum(dout * y2n)
        dy2 = _rms_bwd(dout * (gt * g_post), y2n, ry).astype(BF16)
        dy2_ref[...] = dy2
        dh2 = jnp.zeros((ts, d), F32)
        for qb in range(N_CHIP):
            cols = slice(qb * fq, (qb + 1) * fq)
            dact = _dot_tb(dy2, wd_ref[qb])
            ru = jnp.maximum(up_ref[:, cols].astype(F32), 0.0)
            dup = (dact * (2.0 * ru)).astype(BF16)
            dup_ref[:, cols] = dup
            act_ref[:, cols] = (ru * ru).astype(BF16)
            dh2 = dh2 + _dot_tb(dup, wu_ref[qb])
        xn, r = _rms(x_ref[...])
        g_pre = vec_ref[V_G_PRE_MLP:V_G_PRE_MLP + 1, :]
        sc1 = 1.0 + mod_ref[M_SC_F:M_SC_F + 1, :]
        dsh = _colsum(dh2)
        dgm = _colsum(dh2 * xn)
        dx1_ref[...] = dout + _rms_bwd(dh2 * (g_pre * sc1), xn, r)
        small_ref[SB3_DSH:SB3_DSH + 1, :] += dsh
        small_ref[SB3_DSC:SB3_DSC + 1, :] += dgm
        small_ref[SB3_DGT:SB3_DGT + 1, :] += dgg

        @pl.when(i == n_t - 1)
        def _():
            dgm_t = small_ref[SB3_DSC:SB3_DSC + 1, :]
            dgg_t = small_ref[SB3_DGT:SB3_DGT + 1, :]
            small_ref[SB3_DSC:SB3_DSC + 1, :] = dgm_t * g_pre
            small_ref[SB3_DG_PRE:SB3_DG_PRE + 1, :] = dgm_t * sc1
            small_ref[SB3_DGT:SB3_DGT + 1, :] = dgg_t * g_post
            small_ref[SB3_DG_POST:SB3_DG_POST + 1, :] = dgg_t * gt

    tile = pl.BlockSpec((ts, d), lambda i: (i, 0))
    wide = pl.BlockSpec((ts, N_CHIP * fq), lambda i: (i, 0))
    return _pcall(
        body, name=name, grid=(n_t,),
        out_shape=[jax.ShapeDtypeStruct((s, d), F32), jax.ShapeDtypeStruct((s, d), BF16),
                   jax.ShapeDtypeStruct((s, N_CHIP * fq), BF16), jax.ShapeDtypeStruct((s, N_CHIP * fq), BF16),
                   jax.ShapeDtypeStruct((SUBLANES, d), F32)],
        in_specs=[tile, tile, tile, wide, _full(mod.shape), _full(vec.shape), _full(w_up.shape), _full(w_down.shape)],
        out_specs=[tile, tile, wide, wide, _full((SUBLANES, d))],
        args=[dx2, x1, y2, up, mod, vec, w_up, w_down], carries=carries)


def _mixer_backward(dx1, proj, conva, xb2s, hhs, pas, pbs, ys, rs_, gis, as_, mults, dgels, mod, vec, wg, w_a_out, w_b_out,
                    w_o, name, carries=()):
    s, d = dx1.shape
    n_head, bw, _ = wg.shape
    ts = min(TOKENS_MIXER_TILE, s)
    n_t = s // ts

    def body(dx1_ref, p_ref, conva_ref, xb2_ref, hh_ref, pa_ref, pb_ref, y_ref, r_ref, gi_ref, a_ref, mult_ref, dgel_ref,
             mod_ref, vec_ref, wg_ref, wa_ref, wb_ref, wo_ref,
             dp_ref, dy_ref, dpa_ref, dpb_ref, small_ref, dwg_ref,
             dconv_head, dxb2_head, a_head, g_head, a_buf, b_buf, c_buf):
        i = pl.program_id(0)

        @pl.when(i == 0)
        def _():
            small_ref[...] = jnp.zeros_like(small_ref)
            dwg_ref[...] = jnp.zeros_like(dwg_ref)
            dconv_head[...] = jnp.zeros_like(dconv_head)
            dxb2_head[...] = jnp.zeros_like(dxb2_head)
            a_head[...] = jnp.zeros_like(a_head)
            g_head[...] = jnp.zeros_like(g_head)

        def seg(k):
            return p_ref[:, k * d:(k + 1) * d].astype(F32)

        def vrow(k):
            return vec_ref[k:k + 1, :]

        def acc(row, val):
            small_ref[row:row + 1, :] += val

        dout = dx1_ref[...]
        yn, ry = _rms(y_ref[...].astype(F32))
        g_post = vrow(V_G_POST_MIX)
        gt = mod_ref[M_GT_M:M_GT_M + 1, :]
        acc(SB2_DGT, _colsum(dout * yn))
        dy = _rms_bwd(dout * (gt * g_post), yn, ry).astype(BF16)
        dy_ref[...] = dy
        dm = _dot_tb(dy, wo_ref[...])
        sa, sb = seg(5), seg(6)
        dpa = (dm * sa).astype(BF16)
        dpb = (dm * sb).astype(BF16)
        dpa_ref[...] = dpa
        dpb_ref[...] = dpb
        du_a = dm * pa_ref[...].astype(F32) * (sa * (1.0 - sa))
        du_b = dm * pb_ref[...].astype(F32) * (sb * (1.0 - sb))
        dp_ref[:, 5 * d:6 * d] = du_a.astype(BF16)
        dp_ref[:, 6 * d:7 * d] = du_b.astype(BF16)
        dy_a = _dot_tb(dpa, wa_ref[...])
        dy_b = _dot_tb(dpb, wb_ref[...])

        b_a, c_a, v_a = seg(0), seg(1), seg(2)
        dp_ref[:, 0:d] = (dy_a * conva_ref[...].astype(F32)).astype(BF16)
        dconv = dy_a * b_a
        nxt = dconv_head[...]
        d1 = _shift_up(dconv, 1, nxt)
        d2 = _shift_up(dconv, 2, nxt)
        dconv_head[...] = dconv[:SUBLANES]
        dcv = vrow(V_CONV_A_W + 2) * dconv + vrow(V_CONV_A_W + 1) * d1 + vrow(V_CONV_A_W) * d2
        cv = c_a * v_a
        acc(SB2_DWA + 2, _colsum(cv * dconv))
        acc(SB2_DWA + 1, _colsum(cv * d1))
        acc(SB2_DWA, _colsum(cv * d2))
        acc(SB2_DBA, _colsum(dconv))
        dp_ref[:, d:2 * d] = (dcv * v_a).astype(BF16)
        dp_ref[:, 2 * d:3 * d] = (dcv * c_a).astype(BF16)

        x_b, gel = seg(3), seg(4)
        hh = hh_ref[...]
        dp_ref[:, 4 * d:5 * d] = (dy_b * hh * dgel_ref[...].astype(F32)).astype(BF16)
        dhh = dy_b * gel
        xb2 = xb2_ref[...]
        r, gi, a, mult = r_ref[...], gi_ref[...], a_ref[...], mult_ref[...]
        sp = _softplus(-vrow(V_LAMBDA))
        a_next = _shift_up(a, 1, a_head[...])
        g = _scan_two_level(a_next, dhh, g_head[0:1, :], a_buf, b_buf, c_buf, reverse=True)
        a_head[...] = a[:SUBLANES]
        g_head[...] = g[:SUBLANES]
        gix = gi * xb2
        gm = g * mult
        dlog_a = g * (hh - mult * gix) - (g * gix) * (a * a / mult)
        dgi = gm * xb2
        dxb2 = gm * gi
        acc(SB2_DLAM, _colsum(dlog_a * r))
        dzr = dlog_a * ((-LRU_C) * sp) * (r * (1.0 - r))
        dzi = dgi * (gi * (1.0 - gi))
        acc(SB2_DBR, _colsum(dzr))
        acc(SB2_DBI, _colsum(dzi))
        xb2_b = xb2.astype(BF16)
        back = []
        for hd in range(n_head):
            cols = slice(hd * bw, (hd + 1) * bw)
            dz = jnp.concatenate([dzr[:, cols], dzi[:, cols]], axis=1).astype(BF16)
            back.append(_dot_tb(dz, wg_ref[hd]))
            dwg_ref[hd] += _dot_ta(xb2_b[:, cols], dz)
        dxb2 = dxb2 + jnp.concatenate(back, axis=1)
        nxt = dxb2_head[...]
        e1 = _shift_up(dxb2, 1, nxt)
        e2 = _shift_up(dxb2, 2, nxt)
        e3 = _shift_up(dxb2, 3, nxt)
        dxb2_head[...] = dxb2[:SUBLANES]
        dp_ref[:, 3 * d:4 * d] = (vrow(V_CONV_B_W + 3) * dxb2 + vrow(V_CONV_B_W + 2) * e1
                                  + vrow(V_CONV_B_W + 1) * e2 + vrow(V_CONV_B_W) * e3).astype(BF16)
        acc(SB2_DWB + 3, _colsum(x_b * dxb2))
        acc(SB2_DWB + 2, _colsum(x_b * e1))
        acc(SB2_DWB + 1, _colsum(x_b * e2))
        acc(SB2_DWB, _colsum(x_b * e3))
        acc(SB2_DBB, _colsum(dxb2))

        @pl.when(i == n_t - 1)
        def _():
            dgg_t = small_ref[SB2_DGT:SB2_DGT + 1, :]
            small_ref[SB2_DGT:SB2_DGT + 1, :] = dgg_t * g_post
            small_ref[SB2_DG_POST:SB2_DG_POST + 1, :] = dgg_t * gt
            lam = vrow(V_LAMBDA)
            small_ref[SB2_DLAM:SB2_DLAM + 1, :] = small_ref[SB2_DLAM:SB2_DLAM + 1, :] * (LRU_C * _sigmoid(-lam))

    rev = lambda i: (n_t - 1 - i, 0)
    tile = pl.BlockSpec((ts, d), rev)
    wide = pl.BlockSpec((ts, 7 * d), rev)
    sd = lambda dt: jax.ShapeDtypeStruct((s, d), dt)
    return _pcall(
        body, name=name, grid=(n_t,),
        out_shape=[jax.ShapeDtypeStruct((s, 7 * d), BF16), sd(BF16), sd(BF16), sd(BF16),
                   jax.ShapeDtypeStruct((2 * SUBLANES, d), F32), jax.ShapeDtypeStruct(wg.shape, F32)],
        in_specs=[tile, wide] + [tile] * 11 + [_full(mod.shape), _full(vec.shape),
                  _full(wg.shape), _full(w_a_out.shape), _full(w_b_out.shape), _full(w_o.shape)],
        out_specs=[wide, tile, tile, tile, _full((2 * SUBLANES, d)), _full(wg.shape)],
        scratch_shapes=[pltpu.VMEM((SUBLANES, d), F32)] * 4 + [pltpu.VMEM((d // LANES, ts, LANES), F32)] * 2
                       + [pltpu.VMEM((ts // SCAN_GROUP, d), F32)],
        args=[dx1, proj, conva, xb2s, hhs, pas, pbs, ys, rs_, gis, as_, mults, dgels, mod, vec, wg, w_a_out, w_b_out, w_o],
        carries=carries)


def _proj_backward(dproj, dx1, x, mod, vec, w_in, name, carries=()):
    s, d = x.shape
    nq = w_in.shape[2]
    ts = min(TOKENS_MATMUL_TILE, s)
    n_t = s // ts

    def body(dp_ref, dx1_ref, x_ref, mod_ref, vec_ref, w_ref, dx_ref, small_ref):
        i = pl.program_id(0)

        @pl.when(i == 0)
        def _():
            small_ref[...] = jnp.zeros_like(small_ref)

        dh = jnp.zeros((ts, d), F32)
        for qb in range(N_CHIP):
            dh = dh + _dot_tb(dp_ref[:, qb * nq:(qb + 1) * nq], w_ref[qb])
        xn, r = _rms(x_ref[...])
        g_pre = vec_ref[V_G_PRE_MIX:V_G_PRE_MIX + 1, :]
        sc1 = 1.0 + mod_ref[M_SC_M:M_SC_M + 1, :]
        dx_ref[...] = dx1_ref[...] + _rms_bwd(dh * (g_pre * sc1), xn, r)
        small_ref[SB1_DSH:SB1_DSH + 1, :] += _colsum(dh)
        small_ref[SB1_DSC:SB1_DSC + 1, :] += _colsum(dh * xn)

        @pl.when(i == n_t - 1)
        def _():
            dgm_t = small_ref[SB1_DSC:SB1_DSC + 1, :]
            small_ref[SB1_DSC:SB1_DSC + 1, :] = dgm_t * g_pre
            small_ref[SB1_DG_PRE:SB1_DG_PRE + 1, :] = dgm_t * sc1

    tile = pl.BlockSpec((ts, d), lambda i: (i, 0))
    return _pcall(
        body, name=name, grid=(n_t,),
        out_shape=[jax.ShapeDtypeStruct((s, d), F32), jax.ShapeDtypeStruct((SUBLANES, d), F32)],
        in_specs=[pl.BlockSpec((ts, N_CHIP * nq), lambda i: (i, 0)), tile, tile, _full(mod.shape), _full(vec.shape),
                  _full(w_in.shape)],
        out_specs=[tile, _full((SUBLANES, d))],
        args=[dproj, dx1, x, mod, vec, w_in], carries=carries)


def _weight_grad(a, b, name, col_blocks=1, tk=512, carries=()):
    s, k = a.shape
    n = b.shape[1]
    tn = n // col_blocks
    tk = min(tk, k)

    def body(a_ref, b_ref, o_ref):
        o_ref[0] = _dot_ta(a_ref[...], b_ref[...])

    (out,), carried = _pcall(
        body, name=name, grid=(col_blocks, k // tk),
        out_shape=[jax.ShapeDtypeStruct((col_blocks, k, tn), F32)],
        in_specs=[pl.BlockSpec((s, tk), lambda j, i: (0, i)), pl.BlockSpec((s, tn), lambda j, i: (0, j))],
        out_specs=[pl.BlockSpec((1, tk, tn), lambda j, i: (j, i, 0))],
        args=[a, b], carries=carries)
    return out, carried


def _adamw(w, g, m, v, name, copy_grad=False):
    shape = w.shape
    cols = shape[-1]
    rows = w.size // cols
    tr = _row_tile(rows, cols, target_bytes=1024 * 1024)
    c1 = 1.0 - ADAM_B1 ** ADAM_STEP
    c2 = 1.0 - ADAM_B2 ** ADAM_STEP
    n_out = 4 if copy_grad else 3

    def body(w_ref, g_ref, m_ref, v_ref, d_ref, nm_ref, nv_ref, *g_out):
        gv = g_ref[...]
        nm = ADAM_B1 * m_ref[...] + (1.0 - ADAM_B1) * gv
        nv = ADAM_B2 * v_ref[...] + (1.0 - ADAM_B2) * (gv * gv)
        nm_ref[...] = nm
        nv_ref[...] = nv
        d_ref[...] = (-ADAM_LR) * ((nm / c1) / (jnp.sqrt(nv / c2) + ADAM_EPS) + ADAM_WD * w_ref[...])
        if copy_grad:
            g_out[0][...] = gv

    spec = pl.BlockSpec((tr, cols), lambda i: (i, 0))
    outs = pl.pallas_call(
        body, name=name, grid=(rows // tr,),
        out_shape=[jax.ShapeDtypeStruct((rows, cols), F32)] * n_out,
        in_specs=[spec] * 4, out_specs=[spec] * n_out,
        compiler_params=_cparams(1),
    )(*(t.reshape(rows, cols) for t in (w, g, m, v)))
    return tuple(o.reshape(shape) for o in outs)


def kernel(x, c, w_mod, b_mod, g_pre_mix, g_post_mix, w_in, conv_a_w, conv_a_b, w_a_out, conv_b_w, conv_b_b, w_gate_r, b_gate_r, w_gate_i, b_gate_i, lru_lambda, w_b_out, w_o, g_pre_mlp, g_post_mlp, w_mlp_up, w_mlp_down, loss_target, m_w_mod, m_b_mod, m_g_pre_mix, m_g_post_mix, m_w_in, m_conv_a_w, m_conv_a_b, m_w_a_out, m_conv_b_w, m_conv_b_b, m_w_gate_r, m_b_gate_r, m_w_gate_i, m_b_gate_i, m_lru_lambda, m_w_b_out, m_w_o, m_g_pre_mlp, m_g_post_mlp, m_w_mlp_up, m_w_mlp_down, v_w_mod, v_b_mod, v_g_pre_mix, v_g_post_mix, v_w_in, v_conv_a_w, v_conv_a_b, v_w_a_out, v_conv_b_w, v_conv_b_b, v_w_gate_r, v_b_gate_r, v_w_gate_i, v_b_gate_i, v_lru_lambda, v_w_b_out, v_w_o, v_g_pre_mlp, v_g_post_mlp, v_w_mlp_up, v_w_mlp_down):
    weights = dict(w_mod=w_mod, b_mod=b_mod, g_pre_mix=g_pre_mix, g_post_mix=g_post_mix, w_in=w_in, conv_a_w=conv_a_w,
                   conv_a_b=conv_a_b, w_a_out=w_a_out, conv_b_w=conv_b_w, conv_b_b=conv_b_b, w_gate_r=w_gate_r,
                   b_gate_r=b_gate_r, w_gate_i=w_gate_i, b_gate_i=b_gate_i, lru_lambda=lru_lambda, w_b_out=w_b_out,
                   w_o=w_o, g_pre_mlp=g_pre_mlp, g_post_mlp=g_post_mlp, w_mlp_up=w_mlp_up, w_mlp_down=w_mlp_down)
    mom1 = dict(w_mod=m_w_mod, b_mod=m_b_mod, g_pre_mix=m_g_pre_mix, g_post_mix=m_g_post_mix, w_in=m_w_in,
                conv_a_w=m_conv_a_w, conv_a_b=m_conv_a_b, w_a_out=m_w_a_out, conv_b_w=m_conv_b_w, conv_b_b=m_conv_b_b,
                w_gate_r=m_w_gate_r, b_gate_r=m_b_gate_r, w_gate_i=m_w_gate_i, b_gate_i=m_b_gate_i,
                lru_lambda=m_lru_lambda, w_b_out=m_w_b_out, w_o=m_w_o, g_pre_mlp=m_g_pre_mlp, g_post_mlp=m_g_post_mlp,
                w_mlp_up=m_w_mlp_up, w_mlp_down=m_w_mlp_down)
    mom2 = dict(w_mod=v_w_mod, b_mod=v_b_mod, g_pre_mix=v_g_pre_mix, g_post_mix=v_g_post_mix, w_in=v_w_in,
                conv_a_w=v_conv_a_w, conv_a_b=v_conv_a_b, w_a_out=v_w_a_out, conv_b_w=v_conv_b_w, conv_b_b=v_conv_b_b,
                w_gate_r=v_w_gate_r, b_gate_r=v_b_gate_r, w_gate_i=v_w_gate_i, b_gate_i=v_b_gate_i,
                lru_lambda=v_lru_lambda, w_b_out=v_w_b_out, w_o=v_w_o, g_pre_mlp=v_g_pre_mlp, g_post_mlp=v_g_post_mlp,
                w_mlp_up=v_w_mlp_up, w_mlp_down=v_w_mlp_down)
    names = list(weights)

    n_layer = w_in.shape[0]
    s, d = x.shape[1], x.shape[2]
    n_head, bw = w_gate_r.shape[1], w_gate_r.shape[2]
    dq = d // N_CHIP
    mq = w_mod.shape[2]
    n_mod = (N_CHIP * mq) // d
    ka, kb = conv_a_w.shape[1], conv_b_w.shape[1]

    mx, my, mc = _place()
    q_me = 2 * mx + my
    q_arr = jnp.reshape(q_me, (1,)).astype(jnp.int32)

    me_dev = 4 * mx + 2 * my + mc
    me_arr = jnp.reshape(me_dev, (1,)).astype(jnp.int32)

    big_names = ["w_in", "w_a_out", "w_b_out", "w_o", "w_mlp_up", "w_mlp_down"]
    groups = [["w_in"], ["w_a_out", "w_b_out", "w_o"], ["w_mlp_up", "w_mlp_down"]]
    placed = {("w_in", 0): _cast_place(w_in, 0, q_arr, "cast_place_w_in_0")}
    wfull = [dict() for _ in range(n_layer)]
    riders = {}
    for l in range(n_layer):
        riders.setdefault(3 * l - 1, []).append(([("w_in", l)], 1.0))
        riders.setdefault(3 * l - 2 if l else 0, []).append(([(nm, l) for nm in groups[1]], 0.9 if l else 0.5))
        riders.setdefault(3 * l, []).append(([("w_mlp_up", l)], 0.7 if l else 1.0))
        riders.setdefault(3 * l + 1, []).insert(0, ([("w_mlp_down", l)], 0.5))

    def gather_carry(call):
        return [_gather_carry([placed[k] for k in keys], frac) for keys, frac in riders.get(call, [])]

    def gathered(call, carried):
        for (keys, _), ws in zip(riders.get(call, []), carried):
            for (nm, l), w in zip(keys, ws):
                wfull[l][nm] = w.reshape(d, d) if nm in groups[1] else w

    n_conv_rows = n_layer * (ka + kb)
    conv_blk = -(-n_conv_rows // SUBLANES) * SUBLANES
    blk_rows = SUBLANES + conv_blk
    conv_rows = jnp.concatenate([jnp.concatenate([conv_a_w[l], conv_b_w[l]], axis=0) for l in range(n_layer)], axis=0)
    conv_rows = jnp.pad(conv_rows, ((0, conv_blk - n_conv_rows), (0, d - dq)))
    c_conv = jnp.concatenate([jnp.pad(c, ((0, SUBLANES - 1), (0, 0))), conv_rows], axis=0)
    rest = [(nm, l) for l in range(n_layer) for nm in big_names if (nm, l) != ("w_in", 0)]
    rest_placed, carried = _cast_place_all([(weights[nm], l) for nm, l in rest], q_arr, "cast_place_rest",
                                           carries=gather_carry(-1) + [_allgather_carry([c_conv])])
    placed.update(zip(rest, rest_placed))
    gathered(-1, carried[:1])
    gathered1 = lax.dynamic_update_slice(carried[1][0], c_conv, (me_dev * blk_rows, 0)).reshape(N_DEV, blk_rows, d)
    c_all = gathered1[:, 0, :]
    conv_full = jnp.concatenate([gathered1[2 * qb, SUBLANES:SUBLANES + n_conv_rows, :dq] for qb in range(N_CHIP)], axis=1)

    b_mod_shard = lax.dynamic_slice_in_dim(b_mod, q_me * mq, mq, axis=1)
    mod_part = _mod_forward(c_all, w_mod, b_mod_shard, "mod_forward")
    gathered2 = _all_gather_small(mod_part, "gather_mod").reshape(N_DEV, n_layer, N_DEV, mq)
    mod_rows = jnp.concatenate(
        [lax.dynamic_index_in_dim(gathered2[2 * qb], me_dev, axis=1, keepdims=False) for qb in range(N_CHIP)], axis=1)
    mods = [jnp.pad(mod_rows[l].reshape(n_mod, d), ((0, SUBLANES - n_mod), (0, 0))) for l in range(n_layer)]

    vecs = []
    for l in range(n_layer):
        base = l * (ka + kb)
        rows = [g_pre_mix[l], g_post_mix[l], conv_a_b[l], conv_b_b[l], b_gate_r[l], b_gate_i[l], lru_lambda[l],
                g_pre_mlp[l], g_post_mlp[l]]
        vecs.append(jnp.concatenate([jnp.stack(rows, axis=0), conv_full[base:base + ka + kb]], axis=0))

    wgs =[jnp.concatenate([w_gate_r[l], w_gate_i[l]], axis=-1).astype(BF16) for l in range(n_layer)]

    xs = x[0]
    saved = []
    for l in range(n_layer):
        wl = wfull[l]
        (h, proj, dgel), carried = _norm_proj(xs, mods[l], vecs[l], wl["w_in"], f"norm_proj_{l}", gather_carry(3 * l))
        gathered(3 * l, carried)
        (x1, conva, xb2, hh, ya, yb, pa, pb, mm, yy, gr, ggi, ga, gmult), carried = _mixer_forward(
            xs, proj, mods[l], vecs[l], wgs[l], wl["w_a_out"], wl["w_b_out"], wl["w_o"], f"mixer_forward_{l}",
            gather_carry(3 * l + 1))
        gathered(3 * l + 1, carried)
        (x2, h2, up, y2, *loss_tile), carried = _mlp_forward(
            x1, mods[l], vecs[l], wl["w_mlp_up"], wl["w_mlp_down"], f"mlp_forward_{l}", gather_carry(3 * l + 2),
            target=loss_target[0] if l == n_layer - 1 else None)
        gathered(3 * l + 2, carried)
        saved.append(dict(x=xs, h=h, proj=proj, x1=x1, conva=conva, xb2=xb2, hh=hh, ya=ya, yb=yb, pa=pa, pb=pb, m=mm,
                          y=yy, r=gr, gi=ggi, a=ga, mult=gmult, dgel=dgel, h2=h2, up=up, y2=y2))
        xs = x2
    dxs = xs
    loss = lax.psum(loss_tile[0][0, 0], ("x", "y", "c"))

    chips_q = [q_me ^ 2, q_me ^ 1, q_me ^ 3]
    pf = jnp.stack([mc, q_me] + chips_q).astype(jnp.int32)
    rs = dict(grad={}, landed={}, to_send={}, from_chips={}, out={})
    to_exchange, to_scatter, to_join, to_gather = [], [], [], []
    small_own, small_all = {}, {}

    def ride(call, what, name=None):
        ex = list(to_exchange) if "x" in what else []
        sc = list(to_scatter) if "s" in what else []
        ga = list(to_gather) if "g" in what else []
        jn = []
        for key in (to_join if "j" in what else []):
            if key[0] not in [k[0] for k in jn]:
                jn.append(key)
        carries = []
        if ex:
            carries.append(_exchange_carry([rs["grad"][k] for k in ex]))
        if sc:
            carries.append(_scatter_carry([rs["to_send"][k] for k in sc]))
        if jn:
            carries.append(_join_carry([rs["out"][k[0]] for k in jn], [k[1] for k in jn]))
        if ga:
            carries.append(_allgather_carry([small_own[k] for k in ga]))
        if call is None:
            carried = _run_carries(carries, name) if carries else []
            res = None
        else:
            res, carried = call(carries)
        carried = list(carried)
        if ex:
            for k, ld in zip(ex, carried.pop(0)):
                to_exchange.remove(k)
                rs["landed"][k] = ld
                rs["to_send"][k] = _add_sibling_half(rs["grad"][k], ld, pf, f"rs_add_sibling_{k[0]}_{k[1]}")
                to_scatter.append(k)
        if sc:
            for k, fc in zip(sc, carried.pop(0)):
                to_scatter.remove(k)
                rs["out"][k[0]] = _add_chips(rs["grad"][k], rs["landed"][k], fc, pf, rs["out"].get(k[0]), k[1], n_layer,
                                             f"rs_add_chips_{k[0]}_{k[1]}")
                to_join.append(k)
        if jn:
            for k, o in zip(jn, carried.pop(0)):
                to_join.remove(k)
                rs["out"][k[0]] = o
        if ga:
            for k, o in zip(ga, carried.pop(0)):
                to_gather.remove(k)
                small_all[k] = o
        return res

    def gather_small(key, parts):
        small_own[key] = parts[0] if len(parts) == 1 else jnp.concatenate(parts, axis=0)
        to_gather.append(key)

    def ready(nm, l, g):
        rs["grad"][(nm, l)] = g
        to_exchange.append((nm, l))

    rowblk = lambda t: t.reshape(N_CHIP, t.shape[1] // N_CHIP, t.shape[2])
    small1_prev = None
    for l in reversed(range(n_layer)):
        wl, sv = wfull[l], saved[l]
        dx1, dy2, dup, act, small3 = ride(lambda cr: _mlp_backward(
            dxs, sv["x1"], sv["y2"], sv["up"], mods[l], vecs[l], wl["w_mlp_up"], wl["w_mlp_down"], f"mlp_backward_{l}", cr), "xsjg")
        ready("w_mlp_up", l, _weight_grad(sv["h2"], dup, f"grad_w_mlp_up_{l}", col_blocks=N_CHIP)[0])
        g_down = ride(lambda cr: _weight_grad(act, dy2, f"grad_w_mlp_down_{l}", carries=cr), "x")
        ready("w_mlp_down", l, rowblk(g_down))
        ride(None, "x", f"rs_exchange_down_{l}")
        dproj, dy, dpa, dpb, small2, dwg = ride(lambda cr: _mixer_backward(
            dx1, sv["proj"], sv["conva"], sv["xb2"], sv["hh"], sv["pa"], sv["pb"], sv["y"],
            sv["r"], sv["gi"], sv["a"], sv["mult"], sv["dgel"], mods[l], vecs[l], wgs[l],
            wl["w_a_out"], wl["w_b_out"], wl["w_o"], f"mixer_backward_{l}", cr), "xsjg")
        gather_small(("late", l, "s"), ([small1_prev] if small1_prev is not None else []) + [small2, small3])
        gather_small(("late", l, "w"), [dwg.reshape(2 * bw, d).astype(BF16)])
        ready("w_a_out", l, rowblk(_weight_grad(sv["ya"], dpa, f"grad_w_a_out_{l}")[0]))
        g_b = ride(lambda cr: _weight_grad(sv["yb"], dpb, f"grad_w_b_out_{l}", carries=cr), "x")
        ready("w_b_out", l, rowblk(g_b))
        g_o = ride(lambda cr: _weight_grad(sv["m"], dy, f"grad_w_o_{l}", carries=cr), "x")
        ready("w_o", l, rowblk(g_o))
        g_in = ride(lambda cr: _weight_grad(sv["h"], dproj, f"grad_w_in_{l}", col_blocks=N_CHIP, carries=cr), "xsg")
        ready("w_in", l, g_in)
        if l == 0:
            ride(None, "x", "rs_exchange_last")
        dxs, small1_prev = ride(lambda cr: _proj_backward(dproj, dx1, sv["x"], mods[l], vecs[l], wl["w_in"],
                                                          f"proj_backward_{l}", cr), "xsjg")
    grad_x = dxs[None]
    gather_small(("last", 0, "s"), [small1_prev])

    tail = 0
    while to_exchange or to_scatter or to_join or to_gather:
        ride(None, "xsjg", f"rs_tail_{tail}")
        tail += 1

    sums ={k: _sum_devices(small_all[k], small_own[k], me_arr, f"sum_small_{k[0]}_{k[1]}_{k[2]}") for k in small_own}

    small_full = {}

    def rows_of(l, part):
        if part == 0:
            return (("late", l - 1, "s"), 0) if l >= 1 else (("last", 0, "s"), 0)
        if part == 3:
            return ("late", l, "w"), 0
        base = SUBLANES if l < n_layer - 1 else 0
        return ("late", l, "s"), base + (0, 0, 2 * SUBLANES)[part]

    def summed(l, part, row, n_rows=1):
        key, base = rows_of(l, part)
        return sums[key][base + row:base + row + n_rows]

    def per_device(l, part, row):
        key, base = rows_of(l, part)
        own = small_own[key]
        if key not in small_full:
            small_full[key] = lax.dynamic_update_slice(small_all[key], own, (me_dev * own.shape[0], 0)).reshape(
                (N_DEV,) + own.shape)
        return small_full[key][:, base + row:base + row + 1]

    mod_rows = [(0, SB1_DSH), (0, SB1_DSC), (1, SB2_DGT), (2, SB3_DSH), (2, SB3_DSC), (2, SB3_DGT)]
    dmod_all = jnp.stack([jnp.concatenate([per_device(l, p, r)[:, 0, :] for p, r in mod_rows], axis=1)
                          for l in range(n_layer)], axis=0)
    o1, o2, o3, o4 = 0, SUBLANES, 3 * SUBLANES, 4 * SUBLANES
    small_sum = jnp.stack([jnp.concatenate([summed(l, 0, 0, SUBLANES), summed(l, 1, 0, 2 * SUBLANES),
                                            summed(l, 2, 0, SUBLANES), summed(l, 3, 0, 2 * bw)], axis=0)
                           for l in range(n_layer)], axis=0)
    mod_rows_of = [o1 + SB1_DSH, o1 + SB1_DSC, o2 + SB2_DGT, o3 + SB3_DSH, o3 + SB3_DSC, o3 + SB3_DGT]
    grads = {}
    grads["w_mod"] = _mod_backward(c_all.T, lax.dynamic_slice_in_dim(dmod_all, q_me * mq, mq, axis=2), "mod_backward")
    grads["b_mod"] = jnp.concatenate([small_sum[:, k, :] for k in mod_rows_of], axis=1)
    grads["g_pre_mix"] = small_sum[:, o1 + SB1_DG_PRE]
    grads["g_post_mix"] = small_sum[:, o2 + SB2_DG_POST]
    grads["conv_a_w"] = lax.dynamic_slice_in_dim(small_sum[:, o2 + SB2_DWA:o2 + SB2_DWA + ka], q_me * dq, dq, axis=2)
    grads["conv_a_b"] = small_sum[:, o2 + SB2_DBA]
    grads["conv_b_w"] = lax.dynamic_slice_in_dim(small_sum[:, o2 + SB2_DWB:o2 + SB2_DWB + kb], q_me * dq, dq, axis=2)
    grads["conv_b_b"] = small_sum[:, o2 + SB2_DBB]
    grads["lru_lambda"] = small_sum[:, o2 + SB2_DLAM]
    grads["b_gate_r"] = small_sum[:, o2 + SB2_DBR]
    grads["b_gate_i"] = small_sum[:, o2 + SB2_DBI]
    grads["g_pre_mlp"] = small_sum[:, o3 + SB3_DG_PRE]
    grads["g_post_mlp"] = small_sum[:, o3 + SB3_DG_POST]
    dwg_sum = small_sum[:, o4:].reshape(n_layer, n_head, bw, 2 * bw)
    grads["w_gate_r"] = dwg_sum[..., :bw]
    grads["w_gate_i"] = dwg_sum[..., bw:]

    for nm in big_names:
        grads[nm] = rs["out"][nm].reshape(weights[nm].shape)

    deltas, new_m, new_v = {}, {}, {}
    for nm in names:
        res = _adamw(weights[nm], grads[nm], mom1[nm], mom2[nm], f"adamw_{nm}", copy_grad=nm in big_names)
        deltas[nm], new_m[nm], new_v[nm] = res[:3]
        if nm in big_names:
            grads[nm] = res[3]
    return (loss, grad_x, *[grads[nm] for nm in names], *[deltas[nm] for nm in names],
            *[new_m[nm] for nm in names], *[new_v[nm] for nm in names])
```

```python
import jax
import jax.numpy as jnp
from jax import lax
from jax.experimental import pallas as pl
from jax.experimental.pallas import tpu as pltpu

F32 = jnp.float32
BF16 = jnp.bfloat16
MESH = pl.DeviceIdType.MESH

EPS = 1e-6
LRU_C = 8.0
N_CHIP = 4
N_DEV = 8
ADAM_LR = 0.001
ADAM_B1 = 0.9
ADAM_B2 = 0.999
ADAM_EPS = 1e-08
ADAM_WD = 0.01
ADAM_STEP = 10

VMEM_LIMIT_BYTES = 56 * 1024 * 1024
SUBLANES = 8
LANES = 128
TOKENS_MATMUL_TILE = 512
TOKENS_MIXER_TILE = 256
GELU_K0 = 0.7978845608028654
GELU_K1 = 0.044715

V_G_PRE_MIX, V_G_POST_MIX, V_CONV_A_B, V_CONV_B_B, V_B_GATE_R, V_B_GATE_I, V_LAMBDA, V_G_PRE_MLP, V_G_POST_MLP = range(9)
V_CONV_A_W = 9
V_CONV_B_W = 12
M_SH_M, M_SC_M, M_GT_M, M_SH_F, M_SC_F, M_GT_F = range(6)


def _cparams(n_grid=0):
    sem = ("arbitrary",) * n_grid if n_grid else None
    return pltpu.CompilerParams(dimension_semantics=sem, vmem_limit_bytes=VMEM_LIMIT_BYTES)


def _full(shape):
    return pl.BlockSpec(shape, lambda *_: (0,) * len(shape))


def _dot(a, b):
    return jnp.dot(a, b, preferred_element_type=F32)


def _dot_tb(a, b):
    return lax.dot_general(a, b, (((1,), (1,)), ((), ())), preferred_element_type=F32)


def _dot_ta(a, b):
    return lax.dot_general(a, b, (((0,), (0,)), ((), ())), preferred_element_type=F32)


def _sigmoid(x):
    return 1.0 / (1.0 + jnp.exp(-x))


def _softplus(x):
    return jnp.maximum(x, 0.0) + jnp.log1p(jnp.exp(-jnp.abs(x)))


def _neg_expm1(x):
    series = -x * (1.0 + 0.5 * x * (1.0 + (x / 3.0) * (1.0 + 0.25 * x)))
    return jnp.where(x > -1e-2, series, 1.0 - jnp.exp(x))


def _gelu_and_grad(x):
    x2 = x * x
    s = _sigmoid(x * (2.0 * GELU_K0 + (2.0 * GELU_K0 * GELU_K1) * x2))
    gel = x * s
    return gel, s + gel * (1.0 - s) * (2.0 * GELU_K0 + (6.0 * GELU_K0 * GELU_K1) * x2)


def _rms(x):
    r = lax.rsqrt(jnp.mean(x * x, axis=-1, keepdims=True) + EPS)
    return x * r, r


def _rms_bwd(dxn, xn, r):
    return r * (dxn - xn * jnp.mean(dxn * xn, axis=-1, keepdims=True))


def _colsum(x):
    return jnp.sum(x, axis=0, keepdims=True)


def _rows(t, w):
    return lax.broadcasted_iota(jnp.int32, (t, w), 0)


def _shift_down(x, k, prev8):
    t, w = x.shape
    rolled = pltpu.roll(x, k, 0)
    head = jnp.where(_rows(SUBLANES, w) < k, pltpu.roll(prev8, k, 0), rolled[:SUBLANES])
    return jnp.concatenate([head, rolled[SUBLANES:]], axis=0)


def _shift_up(x, k, next8):
    t, w = x.shape
    rolled = pltpu.roll(x, t - k, 0)
    tail = jnp.where(_rows(SUBLANES, w) >= SUBLANES - k, pltpu.roll(next8, SUBLANES - k, 0), rolled[t - SUBLANES:])
    return jnp.concatenate([rolled[:t - SUBLANES], tail], axis=0)


SCAN_GROUP = 16


def _scan_steps(a, b, group, reverse):
    t, w = a.shape
    pos = _rows(t, w) & (group - 1)
    s = 1
    while s < group:
        keep = (pos < group - s) if reverse else (pos >= s)
        shift = (t - s) if reverse else s
        b = b + a * jnp.where(keep, pltpu.roll(b, shift, 0), 0.0)
        a = a * jnp.where(keep, pltpu.roll(a, shift, 0), 1.0)
        s *= 2
    return b, a


def _scan_two_level(a, b, carry_row, a_buf, b_buf, c_buf, reverse):
    t, w = a.shape
    grp = SCAN_GROUP
    n_grp = t // grp
    h_loc, a_cum = _scan_steps(a, b, grp, reverse)
    end = 0 if reverse else grp - 1
    a_end, h_end = [], []
    for j in range(w // LANES):
        a_buf[j] = a_cum[:, j * LANES:(j + 1) * LANES]
        b_buf[j] = h_loc[:, j * LANES:(j + 1) * LANES]
        a_end.append(a_buf[j, pl.ds(end, n_grp, stride=grp), :])
        h_end.append(b_buf[j, pl.ds(end, n_grp, stride=grp), :])
    a_end = jnp.concatenate(a_end, axis=1)
    h_end = jnp.concatenate(h_end, axis=1)
    h_grp, a_grp = _scan_steps(a_end, h_end, n_grp, reverse)
    h_grp = h_grp + a_grp * carry_row
    rows = _rows(n_grp, w)
    if reverse:
        entering = jnp.where(rows == n_grp - 1, carry_row, pltpu.roll(h_grp, n_grp - 1, 0))
    else:
        entering = jnp.where(rows == 0, carry_row, pltpu.roll(h_grp, 1, 0))
    c_buf[...] = entering
    out = [h_loc[g * grp:(g + 1) * grp] + a_cum[g * grp:(g + 1) * grp] * c_buf[g:g + 1, :] for g in range(n_grp)]
    return jnp.concatenate(out, axis=0)


def _row_tile(rows, cols, itemsize=4, target_bytes=2 * 1024 * 1024):
    if rows * cols * itemsize <= target_bytes or rows % SUBLANES:
        return rows
    t = max(SUBLANES, (target_bytes // (cols * itemsize)) // SUBLANES * SUBLANES)
    while rows % t:
        t -= SUBLANES
    return t


def _place():
    return lax.axis_index("x"), lax.axis_index("y"), lax.axis_index("c")


def _other_chips(x, y):
    chips = [(1 - x, y), (x, 1 - y), (1 - x, 1 - y)]
    return chips, [2 * cx + cy for cx, cy in chips]


def _all_gather_small(block, name):
    m_per, n = block.shape

    def body(x_ref, out_ref, send_sems, recv_sems, local_sem):
        x, y, c = _place()
        me, sibling = (x, y, c), (x, y, 1 - c)
        chips, _ = _other_chips(x, y)

        def rows(px, py, pc):
            return out_ref.at[pl.ds((4 * px + 2 * py + pc) * m_per, m_per), :]

        def copy(k, blk, to, src=None):
            return pltpu.make_async_remote_copy(
                src_ref=rows(*blk) if src is None else src, dst_ref=rows(*blk),
                send_sem=send_sems.at[k], recv_sem=recv_sems.at[k], device_id=to, device_id_type=MESH)

        mine = pltpu.make_async_copy(x_ref, rows(*me), local_sem)
        mine.start()
        first = [copy(0, me, sibling, src=x_ref)]
        first += [copy(1 + j, me, (*chip, c), src=x_ref) for j, chip in enumerate(chips)]
        for cp in first:
            cp.start()
        passed = [copy(4 + j, (*chip, c), sibling) for j, chip in enumerate(chips)]
        for j, chip in enumerate(chips):
            copy(1 + j, (*chip, c), me).wait_recv()
            passed[j].start()
        copy(0, sibling, me).wait_recv()
        for j, chip in enumerate(chips):
            copy(4 + j, (*chip, 1 - c), me).wait_recv()
        for cp in first + passed:
            cp.wait_send()
        mine.wait()

    return pl.pallas_call(
        body, name=name,
        out_shape=jax.ShapeDtypeStruct((N_DEV * m_per, n), block.dtype),
        in_specs=[pl.BlockSpec(memory_space=pltpu.VMEM)],
        out_specs=pl.BlockSpec(memory_space=pltpu.VMEM),
        scratch_shapes=[pltpu.SemaphoreType.DMA((7,)), pltpu.SemaphoreType.DMA((7,)), pltpu.SemaphoreType.DMA],
        compiler_params=pltpu.CompilerParams(vmem_limit_bytes=VMEM_LIMIT_BYTES),
    )(block)


def _cast_place(w, layer, q_arr, name):
    _, r, cols = w.shape
    tr = _row_tile(r, cols)

    def body(q_ref, w_ref, o_ref):
        o_ref[...] = w_ref[...].astype(BF16)

    return pl.pallas_call(
        body, name=name,
        out_shape=jax.ShapeDtypeStruct((N_CHIP, r, cols), BF16),
        grid_spec=pltpu.PrefetchScalarGridSpec(
            num_scalar_prefetch=1, grid=(r // tr,),
            in_specs=[pl.BlockSpec((1, tr, cols), lambda i, q_ref: (layer, i, 0))],
            out_specs=pl.BlockSpec((1, tr, cols), lambda i, q_ref: (q_ref[0], i, 0))),
        compiler_params=_cparams(1),
    )(q_arr, w)


class _Carry:
    def __init__(self, ins, out_shapes, aliases, sem_shapes, start, finish, mid=None, mid_frac=0.85):
        self.ins, self.out_shapes, self.aliases, self.sem_shapes = list(ins), list(out_shapes), dict(aliases), list(sem_shapes)
        self.start, self.mid, self.finish, self.mid_frac = start, mid, finish, mid_frac


def _pcall(body, *, name, grid, in_specs, out_specs, out_shape, args, scratch_shapes=(), carries=(), prefetch=()):
    in_specs, out_specs, out_shape = list(in_specs), list(out_specs), list(out_shape)
    scratch_shapes, args = list(scratch_shapes), list(args)
    n_in, n_out, n_scr, n_pre = len(in_specs), len(out_shape), len(scratch_shapes), len(prefetch)
    steps = 1
    for g in grid:
        steps *= g
    any_spec = pl.BlockSpec(memory_space=pl.ANY)
    aliases = {}
    spans = []
    for cr in carries:
        spans.append((len(args), len(out_shape), len(scratch_shapes)))
        for a, b in cr.aliases.items():
            aliases[n_pre + len(args) + a] = len(out_shape) + b
        args += cr.ins
        in_specs += [any_spec] * len(cr.ins)
        out_shape += cr.out_shapes
        out_specs += [any_spec] * len(cr.out_shapes)
        scratch_shapes += cr.sem_shapes
    n_all_in = len(args)
    n_all_out = len(out_shape)

    def wrapped(*refs):
        pre, refs = refs[:n_pre], refs[n_pre:]
        ins, outs, scr = refs[:n_all_in], refs[n_all_in:n_all_in + n_all_out], refs[n_all_in + n_all_out:]
        parts = [(cr, ins[a:a + len(cr.ins)], outs[b:b + len(cr.out_shapes)], scr[s:s + len(cr.sem_shapes)])
                 for cr, (a, b, s) in zip(carries, spans)]
        lin = 0
        for ax, g in enumerate(grid):
            lin = lin * g + pl.program_id(ax)

        def at(step, fn):
            if steps == 1:
                fn()
            else:
                pl.when(lin == step)(fn)

        def start_all():
            for cr, ci, co, cs in parts:
                cr.start(ci, co, cs)

        def finish_all():
            for cr, ci, co, cs in parts:
                cr.finish(ci, co, cs)

        if parts:
            at(0, start_all)
        body(*pre, *ins[:n_in], *outs[:n_out], *scr[:n_scr])
        for cr, ci, co, cs in parts:
            if cr.mid is not None:
                at(min(steps - 1, int(steps * cr.mid_frac)), lambda cr=cr, ci=ci, co=co, cs=cs: cr.mid(ci, co, cs))
        if parts:
            at(steps - 1, finish_all)

    if n_pre:
        res = pl.pallas_call(
            wrapped, name=name, out_shape=out_shape,
            grid_spec=pltpu.PrefetchScalarGridSpec(num_scalar_prefetch=n_pre, grid=tuple(grid), in_specs=in_specs,
                                                   out_specs=out_specs, scratch_shapes=scratch_shapes),
            input_output_aliases=aliases, compiler_params=_cparams(len(grid)),
        )(*prefetch, *args)
    else:
        res = pl.pallas_call(
            wrapped, name=name, grid=tuple(grid), out_shape=out_shape, in_specs=in_specs, out_specs=out_specs,
            scratch_shapes=scratch_shapes, input_output_aliases=aliases, compiler_params=_cparams(len(grid)),
        )(*args)
    res = list(res)
    return res[:n_out], [res[b:b + len(cr.out_shapes)] for cr, (_, b, _) in zip(carries, spans)]


def _run_carries(carries, name):
    return _pcall(lambda: None, name=name, grid=(), in_specs=[], out_specs=[], out_shape=[], args=[], carries=carries)[1]


CAST_STEPS = 8


def _cast_place_all(shards, q_arr, name, carries=()):
    n = len(shards)

    def body(q_ref, *refs):
        for k in range(n):
            refs[n + k][...] = refs[k][...].astype(BF16)

    def spec_in(k):
        w, layer = shards[k]
        return pl.BlockSpec((1, w.shape[1] // CAST_STEPS, w.shape[2]), lambda i, q_ref: (layer, i, 0))

    def spec_out(k):
        w, _ = shards[k]
        return pl.BlockSpec((1, w.shape[1] // CAST_STEPS, w.shape[2]), lambda i, q_ref: (q_ref[0], i, 0))

    return _pcall(
        body, name=name, grid=(CAST_STEPS,),
        out_shape=[jax.ShapeDtypeStruct((N_CHIP,) + w.shape[1:], BF16) for w, _ in shards],
        in_specs=[spec_in(k) for k in range(n)], out_specs=[spec_out(k) for k in range(n)],
        args=[w for w, _ in shards], carries=carries, prefetch=[q_arr])


def _gather_carry(bufs, mid_frac=0.85):
    n = len(bufs)

    def copies(o_refs, sems):
        send_sems, recv_sems = sems
        x, y, c = _place()
        q = 2 * x + y
        sibling = (x, y, 1 - c)
        chips, qs = _other_chips(x, y)

        def half(w, shard, pc):
            rh = bufs[w].shape[1] // 2
            return o_refs[w].at[shard, pl.ds(pc * rh, rh), :]

        def over_ici(w, j, shard):
            blk = half(w, shard, c)
            return pltpu.make_async_remote_copy(
                src_ref=blk, dst_ref=blk, send_sem=send_sems.at[w, j], recv_sem=recv_sems.at[w, j],
                device_id=(*chips[j], c), device_id_type=MESH)

        def to_sibling(w, j, pc):
            blk = half(w, qs[j], pc)
            return pltpu.make_async_remote_copy(
                src_ref=blk, dst_ref=blk, send_sem=send_sems.at[w, 3 + j], recv_sem=recv_sems.at[w, 3 + j],
                device_id=sibling, device_id_type=MESH)

        return q, c, qs, over_ici, to_sibling

    pairs = [(w, j) for w in range(n) for j in range(3)]

    def start(i_refs, o_refs, sems):
        q, _, _, over_ici, _ = copies(o_refs, sems)
        for w, j in pairs:
            over_ici(w, j, q).start()

    def mid(i_refs, o_refs, sems):
        _, c, qs, over_ici, to_sibling = copies(o_refs, sems)
        for w, j in pairs:
            over_ici(w, j, qs[j]).wait_recv()
            to_sibling(w, j, c).start()

    def finish(i_refs, o_refs, sems):
        q, c, _, over_ici, to_sibling = copies(o_refs, sems)
        for w, j in pairs:
            to_sibling(w, j, 1 - c).wait_recv()
        for w, j in pairs:
            over_ici(w, j, q).wait_send()
            to_sibling(w, j, c).wait_send()

    return _Carry(bufs, [jax.ShapeDtypeStruct(b.shape, b.dtype) for b in bufs], {w: w for w in range(n)},
                  [pltpu.SemaphoreType.DMA((n, 6)), pltpu.SemaphoreType.DMA((n, 6))], start, finish, mid, mid_frac)


def _exchange_carry(grads):
    n = len(grads)

    def copies(g_refs, l_refs, sems):
        send_sems, recv_sems = sems
        x, y, c = _place()
        out = []
        for w in range(n):
            rh = grads[w].shape[1] // 2
            out.append(pltpu.make_async_remote_copy(
                src_ref=g_refs[w].at[:, pl.ds((1 - c) * rh, rh), :], dst_ref=l_refs[w],
                send_sem=send_sems.at[w], recv_sem=recv_sems.at[w], device_id=(x, y, 1 - c), device_id_type=MESH))
        return out

    def start(g_refs, l_refs, sems):
        for cp in copies(g_refs, l_refs, sems):
            cp.start()

    def finish(g_refs, l_refs, sems):
        for cp in copies(g_refs, l_refs, sems):
            cp.wait()

    return _Carry(grads, [jax.ShapeDtypeStruct((N_CHIP, g.shape[1] // 2, g.shape[2]), g.dtype) for g in grads], {},
                  [pltpu.SemaphoreType.DMA((n,)), pltpu.SemaphoreType.DMA((n,))], start, finish)


def _scatter_carry(sums):
    n = len(sums)

    def copies(s_refs, l_refs, sems):
        send_sems, recv_sems = sems
        x, y, c = _place()
        chips, _ = _other_chips(x, y)
        return [pltpu.make_async_remote_copy(
            src_ref=s_refs[w].at[j], dst_ref=l_refs[w].at[j], send_sem=send_sems.at[w, j], recv_sem=recv_sems.at[w, j],
            device_id=(*chips[j], c), device_id_type=MESH) for w in range(n) for j in range(3)]

    def start(s_refs, l_refs, sems):
        for cp in copies(s_refs, l_refs, sems):
            cp.start()

    def finish(s_refs, l_refs, sems):
        for cp in copies(s_refs, l_refs, sems):
            cp.wait()

    return _Carry(sums, [jax.ShapeDtypeStruct(s.shape, s.dtype) for s in sums], {},
                  [pltpu.SemaphoreType.DMA((n, 3)), pltpu.SemaphoreType.DMA((n, 3))], start, finish)


def _join_carry(outs, layers):
    n = len(outs)

    def copy(o_refs, sems, w, mine):
        send_sems, recv_sems = sems
        x, y, c = _place()
        r = outs[w].shape[1]
        rows = o_refs[w].at[layers[w], pl.ds((c if mine else 1 - c) * (r // 2), r // 2), :]
        return pltpu.make_async_remote_copy(
            src_ref=rows, dst_ref=rows, send_sem=send_sems.at[w], recv_sem=recv_sems.at[w],
            device_id=(x, y, 1 - c), device_id_type=MESH)

    def start(i_refs, o_refs, sems):
        for w in range(n):
            copy(o_refs, sems, w, True).start()

    def finish(i_refs, o_refs, sems):
        for w in range(n):
            copy(o_refs, sems, w, True).wait_send()
        for w in range(n):
            copy(o_refs, sems, w, False).wait_recv()

    return _Carry(outs, [jax.ShapeDtypeStruct(o.shape, o.dtype) for o in outs], {w: w for w in range(n)},
                  [pltpu.SemaphoreType.DMA((n,)), pltpu.SemaphoreType.DMA((n,))], start, finish)


PF_C, PF_Q, PF_QS = 0, 1, 2


def _add_sibling_half(g, landed, pf, name):
    _, r, cols = g.shape
    rh = r // 2
    tr = _row_tile(rh, cols)
    nr = rh // tr

    def body(pf_ref, g_ref, l_ref, o_ref):
        o_ref[...] = (g_ref[...] + l_ref[...]).astype(BF16)

    return pl.pallas_call(
        body, name=name,
        out_shape=jax.ShapeDtypeStruct((3, rh, cols), BF16),
        grid_spec=pltpu.PrefetchScalarGridSpec(
            num_scalar_prefetch=1, grid=(3, nr),
            in_specs=[pl.BlockSpec((1, tr, cols), lambda j, i, pf_ref: (pf_ref[PF_QS + j], pf_ref[PF_C] * nr + i, 0)),
                      pl.BlockSpec((1, tr, cols), lambda j, i, pf_ref: (pf_ref[PF_QS + j], i, 0))],
            out_specs=pl.BlockSpec((1, tr, cols), lambda j, i, pf_ref: (j, i, 0))),
        compiler_params=_cparams(2),
    )(pf, g, landed)


def _add_chips(g, landed, from_chips, pf, prev, layer, n_layer, name):
    _, r, cols = g.shape
    rh = r // 2
    tr = _row_tile(rh, cols)
    nr = rh // tr

    def body(pf_ref, g_ref, l_ref, f_ref, *rest):
        o_ref = rest[-1]
        acc = g_ref[0] + l_ref[0]
        for j in range(3):
            acc = acc + f_ref[j].astype(F32)
        o_ref[0] = acc

    in_specs = [pl.BlockSpec((1, tr, cols), lambda i, pf_ref: (pf_ref[PF_Q], pf_ref[PF_C] * nr + i, 0)),
                pl.BlockSpec((1, tr, cols), lambda i, pf_ref: (pf_ref[PF_Q], i, 0)),
                pl.BlockSpec((3, tr, cols), lambda i, pf_ref: (0, i, 0))]
    args = [pf, g, landed, from_chips]
    aliases = {}
    if prev is not None:
        in_specs.append(pl.BlockSpec(memory_space=pl.ANY))
        args.append(prev)
        aliases = {4: 0}
    return pl.pallas_call(
        body, name=name,
        out_shape=jax.ShapeDtypeStruct((n_layer, r, cols), F32),
        grid_spec=pltpu.PrefetchScalarGridSpec(
            num_scalar_prefetch=1, grid=(nr,), in_specs=in_specs,
            out_specs=pl.BlockSpec((1, tr, cols), lambda i, pf_ref: (layer, pf_ref[PF_C] * nr + i, 0))),
        input_output_aliases=aliases,
        compiler_params=_cparams(1),
    )(*args)


def _allgather_carry(blocks):
    n = len(blocks)

    def copies(b_refs, o_refs, sems):
        send_sems, recv_sems = sems
        x, y, c = _place()
        chips, _ = _other_chips(x, y)

        def place(w, px, py, pc):
            m = blocks[w].shape[0]
            return o_refs[w].at[pl.ds((4 * px + 2 * py + pc) * m, m), :]

        def own_to(w, k, to):
            dst = place(w, x, y, c)
            return pltpu.make_async_remote_copy(src_ref=b_refs[w], dst_ref=dst, send_sem=send_sems.at[w, k],
                                                recv_sem=recv_sems.at[w, k], device_id=to, device_id_type=MESH)

        def landed_from(w, k, px, py, pc):
            blk = place(w, px, py, pc)
            return pltpu.make_async_remote_copy(src_ref=blk, dst_ref=blk, send_sem=send_sems.at[w, k],
                                                recv_sem=recv_sems.at[w, k], device_id=(x, y, 1 - c), device_id_type=MESH)

        return x, y, c, chips, own_to, landed_from

    def start(b_refs, o_refs, sems):
        x, y, c, chips, own_to, _ = copies(b_refs, o_refs, sems)
        for w in range(n):
            own_to(w, 0, (x, y, 1 - c)).start()
            for j, chip in enumerate(chips):
                own_to(w, 1 + j, (*chip, c)).start()

    def mid(b_refs, o_refs, sems):
        x, y, c, chips, _, landed_from = copies(b_refs, o_refs, sems)
        for w in range(n):
            for j, chip in enumerate(chips):
                landed_from(w, 1 + j, *chip, c).wait_recv()
                landed_from(w, 4 + j, *chip, c).start()

    def finish(b_refs, o_refs, sems):
        x, y, c, chips, own_to, landed_from = copies(b_refs, o_refs, sems)
        for w in range(n):
            landed_from(w, 0, x, y, 1 - c).wait_recv()
            for j, chip in enumerate(chips):
                landed_from(w, 4 + j, *chip, 1 - c).wait_recv()
            own_to(w, 0, (x, y, 1 - c)).wait_send()
            for j, chip in enumerate(chips):
                own_to(w, 1 + j, (*chip, c)).wait_send()
                landed_from(w, 4 + j, *chip, c).wait_send()

    return _Carry(blocks, [jax.ShapeDtypeStruct((N_DEV * b.shape[0], b.shape[1]), b.dtype) for b in blocks], {},
                  [pltpu.SemaphoreType.DMA((n, 7)), pltpu.SemaphoreType.DMA((n, 7))], start, finish, mid)


def _sum_devices(gathered, own, me_arr, name):
    m, n = own.shape
    tr = _row_tile(m, n, itemsize=own.dtype.itemsize, target_bytes=256 * 1024)
    nr = m // tr

    def body(me_ref, *refs):
        g_refs, own_ref, o_ref = refs[:N_DEV], refs[N_DEV], refs[N_DEV + 1]
        me = me_ref[0]
        acc = None
        for dev in range(N_DEV):
            term = jnp.where(me == dev, own_ref[...], g_refs[dev][...]).astype(F32)
            acc = term if acc is None else acc + term
        o_ref[...] = acc

    def dev_rows(dev):
        return pl.BlockSpec((tr, n), lambda i, me_ref: (dev * nr + i, 0))

    return pl.pallas_call(
        body, name=name,
        out_shape=jax.ShapeDtypeStruct((m, n), F32),
        grid_spec=pltpu.PrefetchScalarGridSpec(
            num_scalar_prefetch=1, grid=(nr,),
            in_specs=[dev_rows(dev) for dev in range(N_DEV)] + [pl.BlockSpec((tr, n), lambda i, me_ref: (i, 0))],
            out_specs=pl.BlockSpec((tr, n), lambda i, me_ref: (i, 0))),
        compiler_params=_cparams(1),
    )(me_arr, *([gathered] * N_DEV), own)


def _mod_forward(c_all, w_mod, b_mod_shard, name):
    n_layer, d, mq = w_mod.shape

    def body(c_ref, w_ref, b_ref, o_ref):
        cv = c_ref[...]
        o_ref[...] = _dot(cv * _sigmoid(cv), w_ref[0]) + b_ref[0]

    return pl.pallas_call(
        body, name=name, grid=(n_layer,),
        out_shape=jax.ShapeDtypeStruct((n_layer * N_DEV, mq), F32),
        in_specs=[_full((N_DEV, d)), pl.BlockSpec((1, d, mq), lambda l: (l, 0, 0)),
                  pl.BlockSpec((1, 1, mq), lambda l: (l, 0, 0))],
        out_specs=pl.BlockSpec((N_DEV, mq), lambda l: (l, 0)),
        compiler_params=_cparams(1),
    )(c_all, w_mod, b_mod_shard.reshape(n_layer, 1, mq))


def _mod_backward(c_all_t, dmod_shard, name):
    n_layer, _, mq = dmod_shard.shape
    d = c_all_t.shape[0]

    def body(c_ref, dm_ref, o_ref):
        cv = c_ref[...]
        o_ref[0] = _dot(cv * _sigmoid(cv), dm_ref[0])

    return pl.pallas_call(
        body, name=name, grid=(n_layer,),
        out_shape=jax.ShapeDtypeStruct((n_layer, d, mq), F32),
        in_specs=[_full((d, N_DEV)), pl.BlockSpec((1, N_DEV, mq), lambda l: (l, 0, 0))],
        out_specs=pl.BlockSpec((1, d, mq), lambda l: (l, 0, 0)),
        compiler_params=_cparams(1),
    )(c_all_t, dmod_shard)


def _norm_proj(x, mod, vec, w_in, name, carries=()):
    s, d = x.shape
    nq = w_in.shape[2]
    ts = min(TOKENS_MATMUL_TILE, s)

    def body(x_ref, mod_ref, vec_ref, w_ref, h_ref, p_ref, dgel_ref):
        xn, _ = _rms(x_ref[...])
        gm = vec_ref[V_G_PRE_MIX:V_G_PRE_MIX + 1, :] * (1.0 + mod_ref[M_SC_M:M_SC_M + 1, :])
        h = (xn * gm + mod_ref[M_SH_M:M_SH_M + 1, :]).astype(BF16)
        h_ref[...] = h
        for qb in range(N_CHIP):
            pq = _dot(h, w_ref[qb])
            for k in range(N_CHIP * nq // d):
                lo, hi = max(qb * nq, k * d), min((qb + 1) * nq, (k + 1) * d)
                if lo >= hi:
                    continue
                piece = pq[:, lo - qb * nq:hi - qb * nq]
                if k == 4:
                    piece, dgel = _gelu_and_grad(piece)
                    dgel_ref[:, lo - 4 * d:hi - 4 * d] = dgel.astype(BF16)
                elif k >= 5:
                    piece = _sigmoid(piece)
                p_ref[:, lo:hi] = piece.astype(BF16)

    tile = pl.BlockSpec((ts, d), lambda i: (i, 0))
    return _pcall(
        body, name=name, grid=(s // ts,),
        out_shape=[jax.ShapeDtypeStruct((s, d), BF16), jax.ShapeDtypeStruct((s, N_CHIP * nq), BF16),
                   jax.ShapeDtypeStruct((s, d), BF16)],
        in_specs=[tile, _full(mod.shape), _full(vec.shape), _full(w_in.shape)],
        out_specs=[tile, pl.BlockSpec((ts, N_CHIP * nq), lambda i: (i, 0)), tile],
        args=[x, mod, vec, w_in], carries=carries)


def _gate_pre(xb2_b, wg_ref, n_head, bw):
    zr, zi = [], []
    for hd in range(n_head):
        z = _dot(xb2_b[:, hd * bw:(hd + 1) * bw], wg_ref[hd])
        zr.append(z[:, :bw])
        zi.append(z[:, bw:])
    return jnp.concatenate(zr, axis=1), jnp.concatenate(zi, axis=1)


def _lru_coeffs(xb2, wg_ref, vec_ref, n_head, bw):
    zr, zi = _gate_pre(xb2.astype(BF16), wg_ref, n_head, bw)
    r = _sigmoid(zr + vec_ref[V_B_GATE_R:V_B_GATE_R + 1, :])
    gi = _sigmoid(zi + vec_ref[V_B_GATE_I:V_B_GATE_I + 1, :])
    sp = _softplus(-vec_ref[V_LAMBDA:V_LAMBDA + 1, :])
    log_a = (-LRU_C) * r * sp
    a = jnp.exp(log_a)
    mult = jnp.sqrt(_neg_expm1(2.0 * log_a))
    return r, gi, sp, a, mult


def _mixer_forward(x, proj, mod, vec, wg, w_a_out, w_b_out, w_o, name, carries=()):
    s, d = x.shape
    n_head, bw, _ = wg.shape
    ts = min(TOKENS_MIXER_TILE, s)

    def body(x_ref, p_ref, mod_ref, vec_ref, wg_ref, wa_ref, wb_ref, wo_ref,
             x1_ref, conva_ref, xb2_ref, hh_ref, abm_ref, pa_ref, pb_ref, y_ref,
             r_ref, gi_ref, a_ref, mult_ref,
             cv_tail, xb_tail, h_last, a_buf, b_buf, c_buf):
        i = pl.program_id(0)

        @pl.when(i == 0)
        def _():
            cv_tail[...] = jnp.zeros_like(cv_tail)
            xb_tail[...] = jnp.zeros_like(xb_tail)
            h_last[...] = jnp.zeros_like(h_last)

        def seg(k):
            return p_ref[:, k * d:(k + 1) * d].astype(F32)

        def vrow(k):
            return vec_ref[k:k + 1, :]

        b_a, c_a, v_a, x_b, gel, sa, sb = (seg(k) for k in range(7))
        cv = c_a * v_a
        prev_cv = cv_tail[...]
        conv_a = (vrow(V_CONV_A_B) + vrow(V_CONV_A_W) * _shift_down(cv, 2, prev_cv)
                  + vrow(V_CONV_A_W + 1) * _shift_down(cv, 1, prev_cv) + vrow(V_CONV_A_W + 2) * cv)
        cv_tail[...] = cv[ts - SUBLANES:]
        y_a = b_a * conv_a
        prev_xb = xb_tail[...]
        xb2 = (vrow(V_CONV_B_B) + vrow(V_CONV_B_W) * _shift_down(x_b, 3, prev_xb)
               + vrow(V_CONV_B_W + 1) * _shift_down(x_b, 2, prev_xb)
               + vrow(V_CONV_B_W + 2) * _shift_down(x_b, 1, prev_xb) + vrow(V_CONV_B_W + 3) * x_b)
        xb_tail[...] = x_b[ts - SUBLANES:]
        r, gi, _, a, mult = _lru_coeffs(xb2, wg_ref, vec_ref, n_head, bw)
        r_ref[...] = r
        gi_ref[...] = gi
        a_ref[...] = a
        mult_ref[...] = mult
        hh = _scan_two_level(a, mult * gi * xb2, h_last[SUBLANES - 1:SUBLANES, :], a_buf, b_buf, c_buf, reverse=False)
        h_last[...] = hh[ts - SUBLANES:]
        y_b = hh * gel
        ya_b, yb_b = y_a.astype(BF16), y_b.astype(BF16)
        pa = _dot(ya_b, wa_ref[...])
        pb = _dot(yb_b, wb_ref[...])
        m = (sa * pa + sb * pb).astype(BF16)
        y = _dot(m, wo_ref[...])
        yn, _ = _rms(y)
        gg = mod_ref[M_GT_M:M_GT_M + 1, :] * vrow(V_G_POST_MIX)
        x1_ref[...] = x_ref[...] + yn * gg
        conva_ref[...] = conv_a.astype(BF16)
        xb2_ref[...] = xb2
        hh_ref[...] = hh
        abm_ref[0] = ya_b
        abm_ref[1] = yb_b
        abm_ref[2] = m
        pa_ref[...] = pa.astype(BF16)
        pb_ref[...] = pb.astype(BF16)
        y_ref[...] = y.astype(BF16)

    tile = pl.BlockSpec((ts, d), lambda i: (i, 0))
    tile3 = pl.BlockSpec((3, ts, d), lambda i: (0, i, 0))
    sd = lambda dt: jax.ShapeDtypeStruct((s, d), dt)
    return _pcall(
        body, name=name, grid=(s // ts,),
        out_shape=[sd(F32), sd(BF16), sd(F32), sd(F32), jax.ShapeDtypeStruct((3, s, d), BF16), sd(BF16), sd(BF16), sd(BF16),
                   sd(F32), sd(F32), sd(F32), sd(F32)],
        in_specs=[tile, pl.BlockSpec((ts, 7 * d), lambda i: (i, 0)), _full(mod.shape), _full(vec.shape),
                  _full(wg.shape), _full(w_a_out.shape), _full(w_b_out.shape), _full(w_o.shape)],
        out_specs=[tile] * 4 + [tile3] + [tile] * 7,
        scratch_shapes=[pltpu.VMEM((SUBLANES, d), F32)] * 3 + [pltpu.VMEM((d // LANES, ts, LANES), F32)] * 2
                       + [pltpu.VMEM((ts // SCAN_GROUP, d), F32)],
        args=[x, proj, mod, vec, wg, w_a_out, w_b_out, w_o], carries=carries)


def _mlp_forward(x1, mod, vec, w_up, w_down, name, carries=(), target=None):
    s, d = x1.shape
    fq = w_up.shape[2]
    ts = min(TOKENS_MATMUL_TILE, s)

    def body(x_ref, *refs):
        if target is None:
            mod_ref, vec_ref, wu_ref, wd_ref, x2_ref, h2_ref, up_ref, y2_ref = refs
        else:
            t_ref, mod_ref, vec_ref, wu_ref, wd_ref, x2_ref, h2_ref, up_ref, y2_ref, loss_ref = refs
        x = x_ref[...]
        xn, _ = _rms(x)
        gm = vec_ref[V_G_PRE_MLP:V_G_PRE_MLP + 1, :] * (1.0 + mod_ref[M_SC_F:M_SC_F + 1, :])
        h2 = (xn * gm + mod_ref[M_SH_F:M_SH_F + 1, :]).astype(BF16)
        h2_ref[...] = h2
        y2 = jnp.zeros((ts, d), F32)
        for qb in range(N_CHIP):
            up = _dot(h2, wu_ref[qb])
            up_ref[:, qb * fq:(qb + 1) * fq] = up.astype(BF16)
            ru = jnp.maximum(up, 0.0)
            y2 = y2 + _dot((ru * ru).astype(BF16), wd_ref[qb])
        y2_ref[...] = y2.astype(BF16)
        yn, _ = _rms(y2)
        gg = mod_ref[M_GT_F:M_GT_F + 1, :] * vec_ref[V_G_POST_MLP:V_G_POST_MLP + 1, :]
        x2 = x + yn * gg
        if target is None:
            x2_ref[...] = x2
        else:
            @pl.when(pl.program_id(0) == 0)
            def _():
                loss_ref[...] = jnp.zeros_like(loss_ref)

            err = x2 - t_ref[...]
            x2_ref[...] = err * (1.0 / d)
            loss_ref[...] += jnp.sum(jnp.sum(err * err, axis=1, keepdims=True), axis=0, keepdims=True) * (0.5 / d)

    tile = pl.BlockSpec((ts, d), lambda i: (i, 0))
    last = target is not None
    return _pcall(
        body, name=name, grid=(s // ts,),
        out_shape=[jax.ShapeDtypeStruct((s, d), F32), jax.ShapeDtypeStruct((s, d), BF16),
                   jax.ShapeDtypeStruct((s, N_CHIP * fq), BF16), jax.ShapeDtypeStruct((s, d), BF16)]
                  + ([jax.ShapeDtypeStruct((SUBLANES, LANES), F32)] if last else []),
        in_specs=[tile] + ([tile] if last else []) + [_full(mod.shape), _full(vec.shape), _full(w_up.shape), _full(w_down.shape)],
        out_specs=[tile, tile, pl.BlockSpec((ts, N_CHIP * fq), lambda i: (i, 0)), tile]
                 + ([_full((SUBLANES, LANES))] if last else []),
        args=[x1] + ([target] if last else []) + [mod, vec, w_up, w_down], carries=carries)


SB3_DSH, SB3_DSC, SB3_DGT, SB3_DG_PRE, SB3_DG_POST = range(5)
SB1_DSH, SB1_DSC, SB1_DG_PRE = range(3)
(SB2_DGT, SB2_DG_POST, SB2_DWA, SB2_DBA, SB2_DWB, SB2_DBB, SB2_DLAM, SB2_DBR, SB2_DBI) = (0, 1, 2, 5, 6, 10, 11, 12, 13)


def _mlp_backward(dx2, x1, y2, up, mod, vec, w_up, w_down, name, carries=()):
    s, d = dx2.shape
    fq = w_up.shape[2]
    ts = min(TOKENS_MIXER_TILE, s)
    n_t = s // ts

    def body(dx2_ref, x_ref, y2_ref, up_ref, mod_ref, vec_ref, wu_ref, wd_ref,
             dx1_ref, dy2_ref, dup_ref, act_ref, small_ref):
        i = pl.program_id(0)

        @pl.when(i == 0)
        def _():
            small_ref[...] = jnp.zeros_like(small_ref)

        dout = dx2_ref[...]
        y2n, ry = _rms(y2_ref[...].astype(F32))
        g_post = vec_ref[V_G_POST_MLP:V_G_POST_MLP + 1, :]
        gt = mod_ref[M_GT_F:M_GT_F + 1, :]
        dgg = _colsum(dout * y2n)
        dy2 = _rms_bwd(dout * (gt * g_post), y2n, ry).astype(BF16)
        dy2_ref[...] = dy2
        dh2 = jnp.zeros((ts, d), F32)
        for qb in range(N_CHIP):
            cols = slice(qb * fq, (qb + 1) * fq)
            dact = _dot_tb(dy2, wd_ref[qb])
            ru = jnp.maximum(up_ref[:, cols].astype(F32), 0.0)
            dup = (dact * (2.0 * ru)).astype(BF16)
            dup_ref[:, cols] = dup
            act_ref[:, cols] = (ru * ru).astype(BF16)
            dh2 = dh2 + _dot_tb(dup, wu_ref[qb])
        xn, r = _rms(x_ref[...])
        g_pre = vec_ref[V_G_PRE_MLP:V_G_PRE_MLP + 1, :]
        sc1 = 1.0 + mod_ref[M_SC_F:M_SC_F + 1, :]
        dsh = _colsum(dh2)
        dgm = _colsum(dh2 * xn)
        dx1_ref[...] = dout + _rms_bwd(dh2 * (g_pre * sc1), xn, r)
        small_ref[SB3_DSH:SB3_DSH + 1, :] += dsh
        small_ref[SB3_DSC:SB3_DSC + 1, :] += dgm
        small_ref[SB3_DGT:SB3_DGT + 1, :] += dgg

        @pl.when(i == n_t - 1)
        def _():
            dgm_t = small_ref[SB3_DSC:SB3_DSC + 1, :]
            dgg_t = small_ref[SB3_DGT:SB3_DGT + 1, :]
            small_ref[SB3_DSC:SB3_DSC + 1, :] = dgm_t * g_pre
            small_ref[SB3_DG_PRE:SB3_DG_PRE + 1, :] = dgm_t * sc1
            small_ref[SB3_DGT:SB3_DGT + 1, :] = dgg_t * g_post
            small_ref[SB3_DG_POST:SB3_DG_POST + 1, :] = dgg_t * gt

    tile = pl.BlockSpec((ts, d), lambda i: (i, 0))
    wide = pl.BlockSpec((ts, N_CHIP * fq), lambda i: (i, 0))
    return _pcall(
        body, name=name, grid=(n_t,),
        out_shape=[jax.ShapeDtypeStruct((s, d), F32), jax.ShapeDtypeStruct((s, d), BF16),
                   jax.ShapeDtypeStruct((s, N_CHIP * fq), BF16), jax.ShapeDtypeStruct((s, N_CHIP * fq), BF16),
                   jax.ShapeDtypeStruct((SUBLANES, d), F32)],
        in_specs=[tile, tile, tile, wide, _full(mod.shape), _full(vec.shape), _full(w_up.shape), _full(w_down.shape)],
        out_specs=[tile, tile, wide, wide, _full((SUBLANES, d))],
        args=[dx2, x1, y2, up, mod, vec, w_up, w_down], carries=carries)


def _mixer_backward(dx1, proj, conva, xb2s, hhs, pas, pbs, ys, rs_, gis, as_, mults, dgels, mod, vec, wg, w_a_out, w_b_out,
                    w_o, name, carries=()):
    s, d = dx1.shape
    n_head, bw, _ = wg.shape
    ts = min(TOKENS_MIXER_TILE, s)
    n_t = s // ts

    def body(dx1_ref, p_ref, conva_ref, xb2_ref, hh_ref, pa_ref, pb_ref, y_ref, r_ref, gi_ref, a_ref, mult_ref, dgel_ref,
             mod_ref, vec_ref, wg_ref, wa_ref, wb_ref, wo_ref,
             dp_ref, dab_ref, small_ref, dwg_ref,
             dconv_head, dxb2_head, a_head, g_head, a_buf, b_buf, c_buf):
        i = pl.program_id(0)

        @pl.when(i == 0)
        def _():
            small_ref[...] = jnp.zeros_like(small_ref)
            dwg_ref[...] = jnp.zeros_like(dwg_ref)
            dconv_head[...] = jnp.zeros_like(dconv_head)
            dxb2_head[...] = jnp.zeros_like(dxb2_head)
            a_head[...] = jnp.zeros_like(a_head)
            g_head[...] = jnp.zeros_like(g_head)

        def seg(k):
            return p_ref[:, k * d:(k + 1) * d].astype(F32)

        def vrow(k):
            return vec_ref[k:k + 1, :]

        def acc(row, val):
            small_ref[row:row + 1, :] += val

        dout = dx1_ref[...]
        yn, ry = _rms(y_ref[...].astype(F32))
        g_post = vrow(V_G_POST_MIX)
        gt = mod_ref[M_GT_M:M_GT_M + 1, :]
        acc(SB2_DGT, _colsum(dout * yn))
        dy = _rms_bwd(dout * (gt * g_post), yn, ry).astype(BF16)
        dab_ref[2] = dy
        dm = _dot_tb(dy, wo_ref[...])
        sa, sb = seg(5), seg(6)
        dpa = (dm * sa).astype(BF16)
        dpb = (dm * sb).astype(BF16)
        dab_ref[0] = dpa
        dab_ref[1] = dpb
        du_a = dm * pa_ref[...].astype(F32) * (sa * (1.0 - sa))
        du_b = dm * pb_ref[...].astype(F32) * (sb * (1.0 - sb))
        dp_ref[:, 5 * d:6 * d] = du_a.astype(BF16)
        dp_ref[:, 6 * d:7 * d] = du_b.astype(BF16)
        dy_a = _dot_tb(dpa, wa_ref[...])
        dy_b = _dot_tb(dpb, wb_ref[...])

        b_a, c_a, v_a = seg(0), seg(1), seg(2)
        dp_ref[:, 0:d] = (dy_a * conva_ref[...].astype(F32)).astype(BF16)
        dconv = dy_a * b_a
        nxt = dconv_head[...]
        d1 = _shift_up(dconv, 1, nxt)
        d2 = _shift_up(dconv, 2, nxt)
        dconv_head[...] = dconv[:SUBLANES]
        dcv = vrow(V_CONV_A_W + 2) * dconv + vrow(V_CONV_A_W + 1) * d1 + vrow(V_CONV_A_W) * d2
        cv = c_a * v_a
        acc(SB2_DWA + 2, _colsum(cv * dconv))
        acc(SB2_DWA + 1, _colsum(cv * d1))
        acc(SB2_DWA, _colsum(cv * d2))
        acc(SB2_DBA, _colsum(dconv))
        dp_ref[:, d:2 * d] = (dcv * v_a).astype(BF16)
        dp_ref[:, 2 * d:3 * d] = (dcv * c_a).astype(BF16)

        x_b, gel = seg(3), seg(4)
        hh = hh_ref[...]
        dp_ref[:, 4 * d:5 * d] = (dy_b * hh * dgel_ref[...].astype(F32)).astype(BF16)
        dhh = dy_b * gel
        xb2 = xb2_ref[...]
        r, gi, a, mult = r_ref[...], gi_ref[...], a_ref[...], mult_ref[...]
        sp = _softplus(-vrow(V_LAMBDA))
        a_next = _shift_up(a, 1, a_head[...])
        g = _scan_two_level(a_next, dhh, g_head[0:1, :], a_buf, b_buf, c_buf, reverse=True)
        a_head[...] = a[:SUBLANES]
        g_head[...] = g[:SUBLANES]
        gix = gi * xb2
        gm = g * mult
        dlog_a = g * (hh - mult * gix) - (g * gix) * (a * a / mult)
        dgi = gm * xb2
        dxb2 = gm * gi
        acc(SB2_DLAM, _colsum(dlog_a * r))
        dzr = dlog_a * ((-LRU_C) * sp) * (r * (1.0 - r))
        dzi = dgi * (gi * (1.0 - gi))
        acc(SB2_DBR, _colsum(dzr))
        acc(SB2_DBI, _colsum(dzi))
        xb2_b = xb2.astype(BF16)
        back = []
        for hd in range(n_head):
            cols = slice(hd * bw, (hd + 1) * bw)
            dz = jnp.concatenate([dzr[:, cols], dzi[:, cols]], axis=1).astype(BF16)
            back.append(_dot_tb(dz, wg_ref[hd]))
            dwg_ref[hd] += _dot_ta(xb2_b[:, cols], dz)
        dxb2 = dxb2 + jnp.concatenate(back, axis=1)
        nxt = dxb2_head[...]
        e1 = _shift_up(dxb2, 1, nxt)
        e2 = _shift_up(dxb2, 2, nxt)
        e3 = _shift_up(dxb2, 3, nxt)
        dxb2_head[...] = dxb2[:SUBLANES]
        dp_ref[:, 3 * d:4 * d] = (vrow(V_CONV_B_W + 3) * dxb2 + vrow(V_CONV_B_W + 2) * e1
                                  + vrow(V_CONV_B_W + 1) * e2 + vrow(V_CONV_B_W) * e3).astype(BF16)
        acc(SB2_DWB + 3, _colsum(x_b * dxb2))
        acc(SB2_DWB + 2, _colsum(x_b * e1))
        acc(SB2_DWB + 1, _colsum(x_b * e2))
        acc(SB2_DWB, _colsum(x_b * e3))
        acc(SB2_DBB, _colsum(dxb2))

        @pl.when(i == n_t - 1)
        def _():
            dgg_t = small_ref[SB2_DGT:SB2_DGT + 1, :]
            small_ref[SB2_DGT:SB2_DGT + 1, :] = dgg_t * g_post
            small_ref[SB2_DG_POST:SB2_DG_POST + 1, :] = dgg_t * gt
            lam = vrow(V_LAMBDA)
            small_ref[SB2_DLAM:SB2_DLAM + 1, :] = small_ref[SB2_DLAM:SB2_DLAM + 1, :] * (LRU_C * _sigmoid(-lam))

    rev = lambda i: (n_t - 1 - i, 0)
    tile = pl.BlockSpec((ts, d), rev)
    wide = pl.BlockSpec((ts, 7 * d), rev)
    sd = lambda dt: jax.ShapeDtypeStruct((s, d), dt)
    return _pcall(
        body, name=name, grid=(n_t,),
        out_shape=[jax.ShapeDtypeStruct((s, 7 * d), BF16), jax.ShapeDtypeStruct((3, s, d), BF16),
                   jax.ShapeDtypeStruct((2 * SUBLANES, d), F32), jax.ShapeDtypeStruct(wg.shape, F32)],
        in_specs=[tile, wide] + [tile] * 11 + [_full(mod.shape), _full(vec.shape),
                  _full(wg.shape), _full(w_a_out.shape), _full(w_b_out.shape), _full(w_o.shape)],
        out_specs=[wide, pl.BlockSpec((3, ts, d), lambda i: (0, n_t - 1 - i, 0)), _full((2 * SUBLANES, d)), _full(wg.shape)],
        scratch_shapes=[pltpu.VMEM((SUBLANES, d), F32)] * 4 + [pltpu.VMEM((d // LANES, ts, LANES), F32)] * 2
                       + [pltpu.VMEM((ts // SCAN_GROUP, d), F32)],
        args=[dx1, proj, conva, xb2s, hhs, pas, pbs, ys, rs_, gis, as_, mults, dgels, mod, vec, wg, w_a_out, w_b_out, w_o],
        carries=carries)


def _proj_backward(dproj, dx1, x, mod, vec, w_in, name, carries=()):
    s, d = x.shape
    nq = w_in.shape[2]
    ts = min(TOKENS_MATMUL_TILE, s)
    n_t = s // ts

    def body(dp_ref, dx1_ref, x_ref, mod_ref, vec_ref, w_ref, dx_ref, small_ref):
        i = pl.program_id(0)

        @pl.when(i == 0)
        def _():
            small_ref[...] = jnp.zeros_like(small_ref)

        dh = jnp.zeros((ts, d), F32)
        for qb in range(N_CHIP):
            dh = dh + _dot_tb(dp_ref[:, qb * nq:(qb + 1) * nq], w_ref[qb])
        xn, r = _rms(x_ref[...])
        g_pre = vec_ref[V_G_PRE_MIX:V_G_PRE_MIX + 1, :]
        sc1 = 1.0 + mod_ref[M_SC_M:M_SC_M + 1, :]
        dx_ref[...] = dx1_ref[...] + _rms_bwd(dh * (g_pre * sc1), xn, r)
        small_ref[SB1_DSH:SB1_DSH + 1, :] += _colsum(dh)
        small_ref[SB1_DSC:SB1_DSC + 1, :] += _colsum(dh * xn)

        @pl.when(i == n_t - 1)
        def _():
            dgm_t = small_ref[SB1_DSC:SB1_DSC + 1, :]
            small_ref[SB1_DSC:SB1_DSC + 1, :] = dgm_t * g_pre
            small_ref[SB1_DG_PRE:SB1_DG_PRE + 1, :] = dgm_t * sc1

    tile = pl.BlockSpec((ts, d), lambda i: (i, 0))
    return _pcall(
        body, name=name, grid=(n_t,),
        out_shape=[jax.ShapeDtypeStruct((s, d), F32), jax.ShapeDtypeStruct((SUBLANES, d), F32)],
        in_specs=[pl.BlockSpec((ts, N_CHIP * nq), lambda i: (i, 0)), tile, tile, _full(mod.shape), _full(vec.shape),
                  _full(w_in.shape)],
        out_specs=[tile, _full((SUBLANES, d))],
        args=[dproj, dx1, x, mod, vec, w_in], carries=carries)


def _weight_grad(a, b, name, col_blocks=1, tk=512, carries=()):
    s, k = a.shape
    n = b.shape[1]
    tn = n // col_blocks
    tk = min(tk, k)

    def body(a_ref, b_ref, o_ref):
        o_ref[0] = _dot_ta(a_ref[...], b_ref[...])

    (out,), carried = _pcall(
        body, name=name, grid=(col_blocks, k // tk),
        out_shape=[jax.ShapeDtypeStruct((col_blocks, k, tn), F32)],
        in_specs=[pl.BlockSpec((s, tk), lambda j, i: (0, i)), pl.BlockSpec((s, tn), lambda j, i: (0, j))],
        out_specs=[pl.BlockSpec((1, tk, tn), lambda j, i: (j, i, 0))],
        args=[a, b], carries=carries)
    return out, carried


def _weight_grad_stacked(a3, b3, name, tk=512, carries=()):
    n_g, s, k = a3.shape
    n = b3.shape[2]
    kq = k // N_CHIP
    tk = min(tk, k)
    chips_per_tile = tk // kq

    def body(a_ref, b_ref, o_ref):
        o_ref[...] = _dot_ta(a_ref[...], b_ref[...]).reshape(chips_per_tile, kq, n)

    (out,), carried = _pcall(
        body, name=name, grid=(n_g, k // tk),
        out_shape=[jax.ShapeDtypeStruct((N_CHIP, n_g, kq, n), F32)],
        in_specs=[pl.BlockSpec((None, s, tk), lambda g, i: (g, 0, i)), pl.BlockSpec((None, s, n), lambda g, i: (g, 0, 0))],
        out_specs=[pl.BlockSpec((chips_per_tile, None, kq, n), lambda g, i: (i, g, 0, 0))],
        args=[a3, b3], carries=carries)
    return out.reshape(N_CHIP, n_g * kq, n), carried


def _adamw(w, g, m, v, name, copy_grad=False):
    shape = w.shape
    cols = shape[-1]
    rows = w.size // cols
    tr = _row_tile(rows, cols, target_bytes=1024 * 1024)
    c1 = 1.0 - ADAM_B1 ** ADAM_STEP
    c2 = 1.0 - ADAM_B2 ** ADAM_STEP
    n_out = 4 if copy_grad else 3

    def body(w_ref, g_ref, m_ref, v_ref, d_ref, nm_ref, nv_ref, *g_out):
        gv = g_ref[...]
        nm = ADAM_B1 * m_ref[...] + (1.0 - ADAM_B1) * gv
        nv = ADAM_B2 * v_ref[...] + (1.0 - ADAM_B2) * (gv * gv)
        nm_ref[...] = nm
        nv_ref[...] = nv
        d_ref[...] = (-ADAM_LR) * ((nm / c1) / (jnp.sqrt(nv / c2) + ADAM_EPS) + ADAM_WD * w_ref[...])
        if copy_grad:
            g_out[0][...] = gv

    spec = pl.BlockSpec((tr, cols), lambda i: (i, 0))
    outs = pl.pallas_call(
        body, name=name, grid=(rows // tr,),
        out_shape=[jax.ShapeDtypeStruct((rows, cols), F32)] * n_out,
        in_specs=[spec] * 4, out_specs=[spec] * n_out,
        compiler_params=_cparams(1),
    )(*(t.reshape(rows, cols) for t in (w, g, m, v)))
    return tuple(o.reshape(shape) for o in outs)


def kernel(x, c, w_mod, b_mod, g_pre_mix, g_post_mix, w_in, conv_a_w, conv_a_b, w_a_out, conv_b_w, conv_b_b, w_gate_r, b_gate_r, w_gate_i, b_gate_i, lru_lambda, w_b_out, w_o, g_pre_mlp, g_post_mlp, w_mlp_up, w_mlp_down, loss_target, m_w_mod, m_b_mod, m_g_pre_mix, m_g_post_mix, m_w_in, m_conv_a_w, m_conv_a_b, m_w_a_out, m_conv_b_w, m_conv_b_b, m_w_gate_r, m_b_gate_r, m_w_gate_i, m_b_gate_i, m_lru_lambda, m_w_b_out, m_w_o, m_g_pre_mlp, m_g_post_mlp, m_w_mlp_up, m_w_mlp_down, v_w_mod, v_b_mod, v_g_pre_mix, v_g_post_mix, v_w_in, v_conv_a_w, v_conv_a_b, v_w_a_out, v_conv_b_w, v_conv_b_b, v_w_gate_r, v_b_gate_r, v_w_gate_i, v_b_gate_i, v_lru_lambda, v_w_b_out, v_w_o, v_g_pre_mlp, v_g_post_mlp, v_w_mlp_up, v_w_mlp_down):
    weights = dict(w_mod=w_mod, b_mod=b_mod, g_pre_mix=g_pre_mix, g_post_mix=g_post_mix, w_in=w_in, conv_a_w=conv_a_w,
                   conv_a_b=conv_a_b, w_a_out=w_a_out, conv_b_w=conv_b_w, conv_b_b=conv_b_b, w_gate_r=w_gate_r,
                   b_gate_r=b_gate_r, w_gate_i=w_gate_i, b_gate_i=b_gate_i, lru_lambda=lru_lambda, w_b_out=w_b_out,
                   w_o=w_o, g_pre_mlp=g_pre_mlp, g_post_mlp=g_post_mlp, w_mlp_up=w_mlp_up, w_mlp_down=w_mlp_down)
    mom1 = dict(w_mod=m_w_mod, b_mod=m_b_mod, g_pre_mix=m_g_pre_mix, g_post_mix=m_g_post_mix, w_in=m_w_in,
                conv_a_w=m_conv_a_w, conv_a_b=m_conv_a_b, w_a_out=m_w_a_out, conv_b_w=m_conv_b_w, conv_b_b=m_conv_b_b,
                w_gate_r=m_w_gate_r, b_gate_r=m_b_gate_r, w_gate_i=m_w_gate_i, b_gate_i=m_b_gate_i,
                lru_lambda=m_lru_lambda, w_b_out=m_w_b_out, w_o=m_w_o, g_pre_mlp=m_g_pre_mlp, g_post_mlp=m_g_post_mlp,
                w_mlp_up=m_w_mlp_up, w_mlp_down=m_w_mlp_down)
    mom2 = dict(w_mod=v_w_mod, b_mod=v_b_mod, g_pre_mix=v_g_pre_mix, g_post_mix=v_g_post_mix, w_in=v_w_in,
                conv_a_w=v_conv_a_w, conv_a_b=v_conv_a_b, w_a_out=v_w_a_out, conv_b_w=v_conv_b_w, conv_b_b=v_conv_b_b,
                w_gate_r=v_w_gate_r, b_gate_r=v_b_gate_r, w_gate_i=v_w_gate_i, b_gate_i=v_b_gate_i,
                lru_lambda=v_lru_lambda, w_b_out=v_w_b_out, w_o=v_w_o, g_pre_mlp=v_g_pre_mlp, g_post_mlp=v_g_post_mlp,
                w_mlp_up=v_w_mlp_up, w_mlp_down=v_w_mlp_down)
    names = list(weights)

    n_layer = w_in.shape[0]
    s, d = x.shape[1], x.shape[2]
    n_head, bw = w_gate_r.shape[1], w_gate_r.shape[2]
    dq = d // N_CHIP
    mq = w_mod.shape[2]
    n_mod = (N_CHIP * mq) // d
    ka, kb = conv_a_w.shape[1], conv_b_w.shape[1]

    mx, my, mc = _place()
    q_me = 2 * mx + my
    q_arr = jnp.reshape(q_me, (1,)).astype(jnp.int32)

    me_dev = 4 * mx + 2 * my + mc
    me_arr = jnp.reshape(me_dev, (1,)).astype(jnp.int32)

    big_names = ["w_in", "w_a_out", "w_b_out", "w_o", "w_mlp_up", "w_mlp_down"]
    groups = [["w_in"], ["w_a_out", "w_b_out", "w_o"], ["w_mlp_up", "w_mlp_down"]]
    placed = {("w_in", 0): _cast_place(w_in, 0, q_arr, "cast_place_w_in_0")}
    wfull = [dict() for _ in range(n_layer)]
    riders = {}
    for l in range(n_layer):
        riders.setdefault(3 * l - 1, []).append(([("w_in", l)], 1.0))
        riders.setdefault(3 * l - 2 if l else 0, []).append(([(nm, l) for nm in groups[1]], 0.9 if l else 0.5))
        riders.setdefault(3 * l, []).append(([("w_mlp_up", l)], 0.7 if l else 1.0))
        riders.setdefault(3 * l + 1, []).insert(0, ([("w_mlp_down", l)], 0.5))

    def gather_carry(call):
        return [_gather_carry([placed[k] for k in keys], frac) for keys, frac in riders.get(call, [])]

    def gathered(call, carried):
        for (keys, _), ws in zip(riders.get(call, []), carried):
            for (nm, l), w in zip(keys, ws):
                wfull[l][nm] = w.reshape(d, d) if nm in groups[1] else w

    n_conv_rows = n_layer * (ka + kb)
    conv_blk = -(-n_conv_rows // SUBLANES) * SUBLANES
    blk_rows = SUBLANES + conv_blk
    conv_rows = jnp.concatenate([jnp.concatenate([conv_a_w[l], conv_b_w[l]], axis=0) for l in range(n_layer)], axis=0)
    conv_rows = jnp.pad(conv_rows, ((0, conv_blk - n_conv_rows), (0, d - dq)))
    c_conv = jnp.concatenate([jnp.pad(c, ((0, SUBLANES - 1), (0, 0))), conv_rows], axis=0)
    rest = [(nm, l) for l in range(n_layer) for nm in big_names if (nm, l) != ("w_in", 0)]
    rest_placed, carried = _cast_place_all([(weights[nm], l) for nm, l in rest], q_arr, "cast_place_rest",
                                           carries=gather_carry(-1) + [_allgather_carry([c_conv])])
    placed.update(zip(rest, rest_placed))
    gathered(-1, carried[:1])
    gathered1 = lax.dynamic_update_slice(carried[1][0], c_conv, (me_dev * blk_rows, 0)).reshape(N_DEV, blk_rows, d)
    c_all = gathered1[:, 0, :]
    conv_full = jnp.concatenate([gathered1[2 * qb, SUBLANES:SUBLANES + n_conv_rows, :dq] for qb in range(N_CHIP)], axis=1)

    b_mod_shard = lax.dynamic_slice_in_dim(b_mod, q_me * mq, mq, axis=1)
    mod_part = _mod_forward(c_all, w_mod, b_mod_shard, "mod_forward")
    gathered2 = _all_gather_small(mod_part, "gather_mod").reshape(N_DEV, n_layer, N_DEV, mq)
    mod_rows = jnp.concatenate(
        [lax.dynamic_index_in_dim(gathered2[2 * qb], me_dev, axis=1, keepdims=False) for qb in range(N_CHIP)], axis=1)
    mods = [jnp.pad(mod_rows[l].reshape(n_mod, d), ((0, SUBLANES - n_mod), (0, 0))) for l in range(n_layer)]

    vecs = []
    for l in range(n_layer):
        base = l * (ka + kb)
        rows = [g_pre_mix[l], g_post_mix[l], conv_a_b[l], conv_b_b[l], b_gate_r[l], b_gate_i[l], lru_lambda[l],
                g_pre_mlp[l], g_post_mlp[l]]
        vecs.append(jnp.concatenate([jnp.stack(rows, axis=0), conv_full[base:base + ka + kb]], axis=0))

    wgs =[jnp.concatenate([w_gate_r[l], w_gate_i[l]], axis=-1).astype(BF16) for l in range(n_layer)]

    xs = x[0]
    saved = []
    for l in range(n_layer):
        wl = wfull[l]
        (h, proj, dgel), carried = _norm_proj(xs, mods[l], vecs[l], wl["w_in"], f"norm_proj_{l}", gather_carry(3 * l))
        gathered(3 * l, carried)
        (x1, conva, xb2, hh, abm, pa, pb, yy, gr, ggi, ga, gmult), carried = _mixer_forward(
            xs, proj, mods[l], vecs[l], wgs[l], wl["w_a_out"], wl["w_b_out"], wl["w_o"], f"mixer_forward_{l}",
            gather_carry(3 * l + 1))
        gathered(3 * l + 1, carried)
        (x2, h2, up, y2, *loss_tile), carried = _mlp_forward(
            x1, mods[l], vecs[l], wl["w_mlp_up"], wl["w_mlp_down"], f"mlp_forward_{l}", gather_carry(3 * l + 2),
            target=loss_target[0] if l == n_layer - 1 else None)
        gathered(3 * l + 2, carried)
        saved.append(dict(x=xs, h=h, proj=proj, x1=x1, conva=conva, xb2=xb2, hh=hh, abm=abm, pa=pa, pb=pb,
                          y=yy, r=gr, gi=ggi, a=ga, mult=gmult, dgel=dgel, h2=h2, up=up, y2=y2))
        xs = x2
    dxs = xs
    loss = lax.psum(loss_tile[0][0, 0], ("x", "y", "c"))

    chips_q = [q_me ^ 2, q_me ^ 1, q_me ^ 3]
    pf = jnp.stack([mc, q_me] + chips_q).astype(jnp.int32)
    rs = dict(grad={}, landed={}, to_send={}, from_chips={}, out={})
    to_exchange, to_scatter, to_join, to_gather = [], [], [], []
    small_own, small_all = {}, {}

    def ride(call, what, name=None):
        ex = list(to_exchange) if "x" in what else []
        sc = list(to_scatter) if "s" in what else []
        ga = list(to_gather) if "g" in what else []
        jn = []
        for key in (to_join if "j" in what else []):
            if key[0] not in [k[0] for k in jn]:
                jn.append(key)
        carries = []
        if ex:
            carries.append(_exchange_carry([rs["grad"][k] for k in ex]))
        if sc:
            carries.append(_scatter_carry([rs["to_send"][k] for k in sc]))
        if jn:
            carries.append(_join_carry([rs["out"][k[0]] for k in jn], [k[1] for k in jn]))
        if ga:
            carries.append(_allgather_carry([small_own[k] for k in ga]))
        if call is None:
            carried = _run_carries(carries, name) if carries else []
            res = None
        else:
            res, carried = call(carries)
        carried = list(carried)
        if ex:
            for k, ld in zip(ex, carried.pop(0)):
                to_exchange.remove(k)
                rs["landed"][k] = ld
                rs["to_send"][k] = _add_sibling_half(rs["grad"][k], ld, pf, f"rs_add_sibling_{k[0]}_{k[1]}")
                to_scatter.append(k)
        if sc:
            for k, fc in zip(sc, carried.pop(0)):
                to_scatter.remove(k)
                rs["out"][k[0]] = _add_chips(rs["grad"][k], rs["landed"][k], fc, pf, rs["out"].get(k[0]), k[1], n_layer,
                                             f"rs_add_chips_{k[0]}_{k[1]}")
                to_join.append(k)
        if jn:
            for k, o in zip(jn, carried.pop(0)):
                to_join.remove(k)
                rs["out"][k[0]] = o
        if ga:
            for k, o in zip(ga, carried.pop(0)):
                to_gather.remove(k)
                small_all[k] = o
        return res

    def gather_small(key, parts):
        small_own[key] = parts[0] if len(parts) == 1 else jnp.concatenate(parts, axis=0)
        to_gather.append(key)

    def ready(nm, l, g):
        rs["grad"][(nm, l)] = g
        to_exchange.append((nm, l))

    rowblk = lambda t: t.reshape(N_CHIP, t.shape[1] // N_CHIP, t.shape[2])
    small1_prev = None
    for l in reversed(range(n_layer)):
        wl, sv = wfull[l], saved[l]
        dx1, dy2, dup, act, small3 = ride(lambda cr: _mlp_backward(
            dxs, sv["x1"], sv["y2"], sv["up"], mods[l], vecs[l], wl["w_mlp_up"], wl["w_mlp_down"], f"mlp_backward_{l}", cr), "xsjg")
        ready("w_mlp_up", l, _weight_grad(sv["h2"], dup, f"grad_w_mlp_up_{l}", col_blocks=N_CHIP)[0])
        g_down = ride(lambda cr: _weight_grad(act, dy2, f"grad_w_mlp_down_{l}", carries=cr), "x")
        ready("w_mlp_down", l, rowblk(g_down))
        dproj, dab, small2, dwg = ride(lambda cr: _mixer_backward(
            dx1, sv["proj"], sv["conva"], sv["xb2"], sv["hh"], sv["pa"], sv["pb"], sv["y"],
            sv["r"], sv["gi"], sv["a"], sv["mult"], sv["dgel"], mods[l], vecs[l], wgs[l],
            wl["w_a_out"], wl["w_b_out"], wl["w_o"], f"mixer_backward_{l}", cr), "xsjg")
        gather_small(("late", l, "s"), ([small1_prev] if small1_prev is not None else []) + [small2, small3])
        gather_small(("late", l, "w"), [dwg.reshape(2 * bw, d).astype(BF16)])
        g_in = ride(lambda cr: _weight_grad(sv["h"], dproj, f"grad_w_in_{l}", col_blocks=N_CHIP, carries=cr), "xsjg")
        ready("w_in", l, g_in)
        g_abo = ride(lambda cr: _weight_grad_stacked(sv["abm"], dab, f"grad_w_abo_{l}", carries=cr), "x")
        ready("w_abo", l, g_abo)
        dxs, small1_prev = ride(lambda cr: _proj_backward(dproj, dx1, sv["x"], mods[l], vecs[l], wl["w_in"],
                                                          f"proj_backward_{l}", cr), "xsjg")
    grad_x = dxs[None]
    gather_small(("last", 0, "s"), [small1_prev])

    tail = 0
    while to_exchange or to_scatter or to_join or to_gather:
        ride(None, "xsjg", f"rs_tail_{tail}")
        tail += 1

    sums ={k: _sum_devices(small_all[k], small_own[k], me_arr, f"sum_small_{k[0]}_{k[1]}_{k[2]}") for k in small_own}

    small_full = {}

    def rows_of(l, part):
        if part == 0:
            return (("late", l - 1, "s"), 0) if l >= 1 else (("last", 0, "s"), 0)
        if part == 3:
            return ("late", l, "w"), 0
        base = SUBLANES if l < n_layer - 1 else 0
        return ("late", l, "s"), base + (0, 0, 2 * SUBLANES)[part]

    def summed(l, part, row, n_rows=1):
        key, base = rows_of(l, part)
        return sums[key][base + row:base + row + n_rows]

    def per_device(l, part, row):
        key, base = rows_of(l, part)
        own = small_own[key]
        if key not in small_full:
            small_full[key] = lax.dynamic_update_slice(small_all[key], own, (me_dev * own.shape[0], 0)).reshape(
                (N_DEV,) + own.shape)
        return small_full[key][:, base + row:base + row + 1]

    mod_rows = [(0, SB1_DSH), (0, SB1_DSC), (1, SB2_DGT), (2, SB3_DSH), (2, SB3_DSC), (2, SB3_DGT)]
    dmod_all = jnp.stack([jnp.concatenate([per_device(l, p, r)[:, 0, :] for p, r in mod_rows], axis=1)
                          for l in range(n_layer)], axis=0)
    o1, o2, o3, o4 = 0, SUBLANES, 3 * SUBLANES, 4 * SUBLANES
    small_sum = jnp.stack([jnp.concatenate([summed(l, 0, 0, SUBLANES), summed(l, 1, 0, 2 * SUBLANES),
                                            summed(l, 2, 0, SUBLANES), summed(l, 3, 0, 2 * bw)], axis=0)
                           for l in range(n_layer)], axis=0)
    mod_rows_of = [o1 + SB1_DSH, o1 + SB1_DSC, o2 + SB2_DGT, o3 + SB3_DSH, o3 + SB3_DSC, o3 + SB3_DGT]
    grads = {}
    grads["w_mod"] = _mod_backward(c_all.T, lax.dynamic_slice_in_dim(dmod_all, q_me * mq, mq, axis=2), "mod_backward")
    grads["b_mod"] = jnp.concatenate([small_sum[:, k, :] for k in mod_rows_of], axis=1)
    grads["g_pre_mix"] = small_sum[:, o1 + SB1_DG_PRE]
    grads["g_post_mix"] = small_sum[:, o2 + SB2_DG_POST]
    grads["conv_a_w"] = lax.dynamic_slice_in_dim(small_sum[:, o2 + SB2_DWA:o2 + SB2_DWA + ka], q_me * dq, dq, axis=2)
    grads["conv_a_b"] = small_sum[:, o2 + SB2_DBA]
    grads["conv_b_w"] = lax.dynamic_slice_in_dim(small_sum[:, o2 + SB2_DWB:o2 + SB2_DWB + kb], q_me * dq, dq, axis=2)
    grads["conv_b_b"] = small_sum[:, o2 + SB2_DBB]
    grads["lru_lambda"] = small_sum[:, o2 + SB2_DLAM]
    grads["b_gate_r"] = small_sum[:, o2 + SB2_DBR]
    grads["b_gate_i"] = small_sum[:, o2 + SB2_DBI]
    grads["g_pre_mlp"] = small_sum[:, o3 + SB3_DG_PRE]
    grads["g_post_mlp"] = small_sum[:, o3 + SB3_DG_POST]
    dwg_sum = small_sum[:, o4:].reshape(n_layer, n_head, bw, 2 * bw)
    grads["w_gate_r"] = dwg_sum[..., :bw]
    grads["w_gate_i"] = dwg_sum[..., bw:]

    from_calls_that_communicate = ["w_in", "w_mlp_up", "w_mlp_down"]
    for nm in from_calls_that_communicate:
        grads[nm] = rs["out"][nm].reshape(weights[nm].shape)
    for k, nm in enumerate(groups[1]):
        grads[nm] = rs["out"]["w_abo"][:, k * dq:(k + 1) * dq]

    deltas, new_m, new_v = {}, {}, {}
    for nm in names:
        res = _adamw(weights[nm], grads[nm], mom1[nm], mom2[nm], f"adamw_{nm}", copy_grad=nm in from_calls_that_communicate)
        deltas[nm], new_m[nm], new_v[nm] = res[:3]
        if nm in from_calls_that_communicate:
            grads[nm] = res[3]
    return (loss, grad_x, *[grads[nm] for nm in names], *[deltas[nm] for nm in names],
            *[new_m[nm] for nm in names], *[new_v[nm] for nm in names])
```

```python
import jax
import jax.numpy as jnp
from jax import lax
from jax.experimental import pallas as pl
from jax.experimental.pallas import tpu as pltpu

F32 = jnp.float32
BF16 = jnp.bfloat16
MESH = pl.DeviceIdType.MESH

EPS = 1e-6
LRU_C = 8.0
N_CHIP = 4
N_DEV = 8
ADAM_LR = 0.001
ADAM_B1 = 0.9
ADAM_B2 = 0.999
ADAM_EPS = 1e-08
ADAM_WD = 0.01
ADAM_STEP = 10

VMEM_LIMIT_BYTES = 56 * 1024 * 1024
SUBLANES = 8
LANES = 128
TOKENS_MATMUL_TILE = 512
TOKENS_MIXER_TILE = 256
GELU_K0 = 0.7978845608028654
GELU_K1 = 0.044715

V_G_PRE_MIX, V_G_POST_MIX, V_CONV_A_B, V_CONV_B_B, V_B_GATE_R, V_B_GATE_I, V_LAMBDA, V_G_PRE_MLP, V_G_POST_MLP = range(9)
V_CONV_A_W = 9
V_CONV_B_W = 12
M_SH_M, M_SC_M, M_GT_M, M_SH_F, M_SC_F, M_GT_F = range(6)


def _cparams(n_grid=0):
    sem = ("arbitrary",) * n_grid if n_grid else None
    return pltpu.CompilerParams(dimension_semantics=sem, vmem_limit_bytes=VMEM_LIMIT_BYTES)


def _full(shape):
    return pl.BlockSpec(shape, lambda *_: (0,) * len(shape))


def _dot(a, b):
    return jnp.dot(a, b, preferred_element_type=F32)


def _dot_tb(a, b):
    return lax.dot_general(a, b, (((1,), (1,)), ((), ())), preferred_element_type=F32)


def _dot_ta(a, b):
    return lax.dot_general(a, b, (((0,), (0,)), ((), ())), preferred_element_type=F32)


def _sigmoid(x):
    return 1.0 / (1.0 + jnp.exp(-x))


def _softplus(x):
    return jnp.maximum(x, 0.0) + jnp.log1p(jnp.exp(-jnp.abs(x)))


def _neg_expm1(x):
    series = -x * (1.0 + 0.5 * x * (1.0 + (x / 3.0) * (1.0 + 0.25 * x)))
    return jnp.where(x > -1e-2, series, 1.0 - jnp.exp(x))


def _gelu_and_grad(x):
    x2 = x * x
    s = _sigmoid(x * (2.0 * GELU_K0 + (2.0 * GELU_K0 * GELU_K1) * x2))
    gel = x * s
    return gel, s + gel * (1.0 - s) * (2.0 * GELU_K0 + (6.0 * GELU_K0 * GELU_K1) * x2)


def _rms(x):
    r = lax.rsqrt(jnp.mean(x * x, axis=-1, keepdims=True) + EPS)
    return x * r, r


def _rms_bwd(dxn, xn, r):
    return r * (dxn - xn * jnp.mean(dxn * xn, axis=-1, keepdims=True))


def _colsum(x):
    return jnp.sum(x, axis=0, keepdims=True)


def _rows(t, w):
    return lax.broadcasted_iota(jnp.int32, (t, w), 0)


def _shift_down(x, k, prev8):
    t, w = x.shape
    rolled = pltpu.roll(x, k, 0)
    head = jnp.where(_rows(SUBLANES, w) < k, pltpu.roll(prev8, k, 0), rolled[:SUBLANES])
    return jnp.concatenate([head, rolled[SUBLANES:]], axis=0)


def _shift_up(x, k, next8):
    t, w = x.shape
    rolled = pltpu.roll(x, t - k, 0)
    tail = jnp.where(_rows(SUBLANES, w) >= SUBLANES - k, pltpu.roll(next8, SUBLANES - k, 0), rolled[t - SUBLANES:])
    return jnp.concatenate([rolled[:t - SUBLANES], tail], axis=0)


SCAN_GROUP = 16


def _scan_steps(a, b, group, reverse):
    t, w = a.shape
    pos = _rows(t, w) & (group - 1)
    s = 1
    while s < group:
        keep = (pos < group - s) if reverse else (pos >= s)
        shift = (t - s) if reverse else s
        b = b + a * jnp.where(keep, pltpu.roll(b, shift, 0), 0.0)
        a = a * jnp.where(keep, pltpu.roll(a, shift, 0), 1.0)
        s *= 2
    return b, a


def _scan_two_level(a, b, carry_row, a_buf, b_buf, c_buf, reverse):
    t, w = a.shape
    grp = SCAN_GROUP
    n_grp = t // grp
    h_loc, a_cum = _scan_steps(a, b, grp, reverse)
    end = 0 if reverse else grp - 1
    a_end, h_end = [], []
    for j in range(w // LANES):
        a_buf[j] = a_cum[:, j * LANES:(j + 1) * LANES]
        b_buf[j] = h_loc[:, j * LANES:(j + 1) * LANES]
        a_end.append(a_buf[j, pl.ds(end, n_grp, stride=grp), :])
        h_end.append(b_buf[j, pl.ds(end, n_grp, stride=grp), :])
    a_end = jnp.concatenate(a_end, axis=1)
    h_end = jnp.concatenate(h_end, axis=1)
    h_grp, a_grp = _scan_steps(a_end, h_end, n_grp, reverse)
    h_grp = h_grp + a_grp * carry_row
    rows = _rows(n_grp, w)
    if reverse:
        entering = jnp.where(rows == n_grp - 1, carry_row, pltpu.roll(h_grp, n_grp - 1, 0))
    else:
        entering = jnp.where(rows == 0, carry_row, pltpu.roll(h_grp, 1, 0))
    c_buf[...] = entering
    out = [h_loc[g * grp:(g + 1) * grp] + a_cum[g * grp:(g + 1) * grp] * c_buf[g:g + 1, :] for g in range(n_grp)]
    return jnp.concatenate(out, axis=0)


def _row_tile(rows, cols, itemsize=4, target_bytes=2 * 1024 * 1024):
    if rows * cols * itemsize <= target_bytes or rows % SUBLANES:
        return rows
    t = max(SUBLANES, (target_bytes // (cols * itemsize)) // SUBLANES * SUBLANES)
    while rows % t:
        t -= SUBLANES
    return t


def _place():
    return lax.axis_index("x"), lax.axis_index("y"), lax.axis_index("c")


def _other_chips(x, y):
    chips = [(1 - x, y), (x, 1 - y), (1 - x, 1 - y)]
    return chips, [2 * cx + cy for cx, cy in chips]


def _all_gather_small(block, name):
    m_per, n = block.shape

    def body(x_ref, out_ref, send_sems, recv_sems, local_sem):
        x, y, c = _place()
        me, sibling = (x, y, c), (x, y, 1 - c)
        chips, _ = _other_chips(x, y)

        def rows(px, py, pc):
            return out_ref.at[pl.ds((4 * px + 2 * py + pc) * m_per, m_per), :]

        def copy(k, blk, to, src=None):
            return pltpu.make_async_remote_copy(
                src_ref=rows(*blk) if src is None else src, dst_ref=rows(*blk),
                send_sem=send_sems.at[k], recv_sem=recv_sems.at[k], device_id=to, device_id_type=MESH)

        mine = pltpu.make_async_copy(x_ref, rows(*me), local_sem)
        mine.start()
        first = [copy(0, me, sibling, src=x_ref)]
        first += [copy(1 + j, me, (*chip, c), src=x_ref) for j, chip in enumerate(chips)]
        for cp in first:
            cp.start()
        passed = [copy(4 + j, (*chip, c), sibling) for j, chip in enumerate(chips)]
        for j, chip in enumerate(chips):
            copy(1 + j, (*chip, c), me).wait_recv()
            passed[j].start()
        copy(0, sibling, me).wait_recv()
        for j, chip in enumerate(chips):
            copy(4 + j, (*chip, 1 - c), me).wait_recv()
        for cp in first + passed:
            cp.wait_send()
        mine.wait()

    return pl.pallas_call(
        body, name=name,
        out_shape=jax.ShapeDtypeStruct((N_DEV * m_per, n), block.dtype),
        in_specs=[pl.BlockSpec(memory_space=pltpu.VMEM)],
        out_specs=pl.BlockSpec(memory_space=pltpu.VMEM),
        scratch_shapes=[pltpu.SemaphoreType.DMA((7,)), pltpu.SemaphoreType.DMA((7,)), pltpu.SemaphoreType.DMA],
        compiler_params=pltpu.CompilerParams(vmem_limit_bytes=VMEM_LIMIT_BYTES),
    )(block)


def _cast_place(w, layer, q_arr, name):
    _, r, cols = w.shape
    tr = _row_tile(r, cols)

    def body(q_ref, w_ref, o_ref):
        o_ref[...] = w_ref[...].astype(BF16)

    return pl.pallas_call(
        body, name=name,
        out_shape=jax.ShapeDtypeStruct((N_CHIP, r, cols), BF16),
        grid_spec=pltpu.PrefetchScalarGridSpec(
            num_scalar_prefetch=1, grid=(r // tr,),
            in_specs=[pl.BlockSpec((1, tr, cols), lambda i, q_ref: (layer, i, 0))],
            out_specs=pl.BlockSpec((1, tr, cols), lambda i, q_ref: (q_ref[0], i, 0))),
        compiler_params=_cparams(1),
    )(q_arr, w)


class _Carry:
    def __init__(self, ins, out_shapes, aliases, sem_shapes, start, finish, mid=None, mid_frac=0.85):
        self.ins, self.out_shapes, self.aliases, self.sem_shapes = list(ins), list(out_shapes), dict(aliases), list(sem_shapes)
        self.start, self.mid, self.finish, self.mid_frac = start, mid, finish, mid_frac


def _pcall(body, *, name, grid, in_specs, out_specs, out_shape, args, scratch_shapes=(), carries=(), prefetch=()):
    in_specs, out_specs, out_shape = list(in_specs), list(out_specs), list(out_shape)
    scratch_shapes, args = list(scratch_shapes), list(args)
    n_in, n_out, n_scr, n_pre = len(in_specs), len(out_shape), len(scratch_shapes), len(prefetch)
    steps = 1
    for g in grid:
        steps *= g
    any_spec = pl.BlockSpec(memory_space=pl.ANY)
    aliases = {}
    spans = []
    for cr in carries:
        spans.append((len(args), len(out_shape), len(scratch_shapes)))
        for a, b in cr.aliases.items():
            aliases[n_pre + len(args) + a] = len(out_shape) + b
        args += cr.ins
        in_specs += [any_spec] * len(cr.ins)
        out_shape += cr.out_shapes
        out_specs += [any_spec] * len(cr.out_shapes)
        scratch_shapes += cr.sem_shapes
    n_all_in = len(args)
    n_all_out = len(out_shape)

    def wrapped(*refs):
        pre, refs = refs[:n_pre], refs[n_pre:]
        ins, outs, scr = refs[:n_all_in], refs[n_all_in:n_all_in + n_all_out], refs[n_all_in + n_all_out:]
        parts = [(cr, ins[a:a + len(cr.ins)], outs[b:b + len(cr.out_shapes)], scr[s:s + len(cr.sem_shapes)])
                 for cr, (a, b, s) in zip(carries, spans)]
        lin = 0
        for ax, g in enumerate(grid):
            lin = lin * g + pl.program_id(ax)

        def at(step, fn):
            if steps == 1:
                fn()
            else:
                pl.when(lin == step)(fn)

        def start_all():
            for cr, ci, co, cs in parts:
                cr.start(ci, co, cs)

        def finish_all():
            for cr, ci, co, cs in parts:
                cr.finish(ci, co, cs)

        if parts:
            at(0, start_all)
        body(*pre, *ins[:n_in], *outs[:n_out], *scr[:n_scr])
        for cr, ci, co, cs in parts:
            if cr.mid is not None:
                at(min(steps - 1, int(steps * cr.mid_frac)), lambda cr=cr, ci=ci, co=co, cs=cs: cr.mid(ci, co, cs))
        if parts:
            at(steps - 1, finish_all)

    if n_pre:
        res = pl.pallas_call(
            wrapped, name=name, out_shape=out_shape,
            grid_spec=pltpu.PrefetchScalarGridSpec(num_scalar_prefetch=n_pre, grid=tuple(grid), in_specs=in_specs,
                                                   out_specs=out_specs, scratch_shapes=scratch_shapes),
            input_output_aliases=aliases, compiler_params=_cparams(len(grid)),
        )(*prefetch, *args)
    else:
        res = pl.pallas_call(
            wrapped, name=name, grid=tuple(grid), out_shape=out_shape, in_specs=in_specs, out_specs=out_specs,
            scratch_shapes=scratch_shapes, input_output_aliases=aliases, compiler_params=_cparams(len(grid)),
        )(*args)
    res = list(res)
    return res[:n_out], [res[b:b + len(cr.out_shapes)] for cr, (_, b, _) in zip(carries, spans)]


def _run_carries(carries, name):
    return _pcall(lambda: None, name=name, grid=(), in_specs=[], out_specs=[], out_shape=[], args=[], carries=carries)[1]


CAST_STEPS = 8


def _cast_place_all(shards, q_arr, name, carries=()):
    n = len(shards)

    def body(q_ref, *refs):
        for k in range(n):
            refs[n + k][...] = refs[k][...].astype(BF16)

    def spec_in(k):
        w, layer = shards[k]
        return pl.BlockSpec((1, w.shape[1] // CAST_STEPS, w.shape[2]), lambda i, q_ref: (layer, i, 0))

    def spec_out(k):
        w, _ = shards[k]
        return pl.BlockSpec((1, w.shape[1] // CAST_STEPS, w.shape[2]), lambda i, q_ref: (q_ref[0], i, 0))

    return _pcall(
        body, name=name, grid=(CAST_STEPS,),
        out_shape=[jax.ShapeDtypeStruct((N_CHIP,) + w.shape[1:], BF16) for w, _ in shards],
        in_specs=[spec_in(k) for k in range(n)], out_specs=[spec_out(k) for k in range(n)],
        args=[w for w, _ in shards], carries=carries, prefetch=[q_arr])


def _gather_carry(bufs, mid_frac=0.85):
    n = len(bufs)

    def copies(o_refs, sems):
        send_sems, recv_sems = sems
        x, y, c = _place()
        q = 2 * x + y
        sibling = (x, y, 1 - c)
        chips, qs = _other_chips(x, y)

        def half(w, shard, pc):
            rh = bufs[w].shape[1] // 2
            return o_refs[w].at[shard, pl.ds(pc * rh, rh), :]

        def over_ici(w, j, shard):
            blk = half(w, shard, c)
            return pltpu.make_async_remote_copy(
                src_ref=blk, dst_ref=blk, send_sem=send_sems.at[w, j], recv_sem=recv_sems.at[w, j],
                device_id=(*chips[j], c), device_id_type=MESH)

        def to_sibling(w, j, pc):
            blk = half(w, qs[j], pc)
            return pltpu.make_async_remote_copy(
                src_ref=blk, dst_ref=blk, send_sem=send_sems.at[w, 3 + j], recv_sem=recv_sems.at[w, 3 + j],
                device_id=sibling, device_id_type=MESH)

        return q, c, qs, over_ici, to_sibling

    pairs = [(w, j) for w in range(n) for j in range(3)]

    def start(i_refs, o_refs, sems):
        q, _, _, over_ici, _ = copies(o_refs, sems)
        for w, j in pairs:
            over_ici(w, j, q).start()

    def mid(i_refs, o_refs, sems):
        _, c, qs, over_ici, to_sibling = copies(o_refs, sems)
        for w, j in pairs:
            over_ici(w, j, qs[j]).wait_recv()
            to_sibling(w, j, c).start()

    def finish(i_refs, o_refs, sems):
        q, c, _, over_ici, to_sibling = copies(o_refs, sems)
        for w, j in pairs:
            to_sibling(w, j, 1 - c).wait_recv()
        for w, j in pairs:
            over_ici(w, j, q).wait_send()
            to_sibling(w, j, c).wait_send()

    return _Carry(bufs, [jax.ShapeDtypeStruct(b.shape, b.dtype) for b in bufs], {w: w for w in range(n)},
                  [pltpu.SemaphoreType.DMA((n, 6)), pltpu.SemaphoreType.DMA((n, 6))], start, finish, mid, mid_frac)


def _exchange_carry(grads):
    n = len(grads)

    def copies(g_refs, l_refs, sems):
        send_sems, recv_sems = sems
        x, y, c = _place()
        out = []
        for w in range(n):
            rh = grads[w].shape[1] // 2
            out.append(pltpu.make_async_remote_copy(
                src_ref=g_refs[w].at[:, pl.ds((1 - c) * rh, rh), :], dst_ref=l_refs[w],
                send_sem=send_sems.at[w], recv_sem=recv_sems.at[w], device_id=(x, y, 1 - c), device_id_type=MESH))
        return out

    def start(g_refs, l_refs, sems):
        for cp in copies(g_refs, l_refs, sems):
            cp.start()

    def finish(g_refs, l_refs, sems):
        for cp in copies(g_refs, l_refs, sems):
            cp.wait()

    return _Carry(grads, [jax.ShapeDtypeStruct((N_CHIP, g.shape[1] // 2, g.shape[2]), g.dtype) for g in grads], {},
                  [pltpu.SemaphoreType.DMA((n,)), pltpu.SemaphoreType.DMA((n,))], start, finish)


def _scatter_carry(sums):
    n = len(sums)

    def copies(s_refs, l_refs, sems):
        send_sems, recv_sems = sems
        x, y, c = _place()
        chips, _ = _other_chips(x, y)
        return [pltpu.make_async_remote_copy(
            src_ref=s_refs[w].at[j], dst_ref=l_refs[w].at[j], send_sem=send_sems.at[w, j], recv_sem=recv_sems.at[w, j],
            device_id=(*chips[j], c), device_id_type=MESH) for w in range(n) for j in range(3)]

    def start(s_refs, l_refs, sems):
        for cp in copies(s_refs, l_refs, sems):
            cp.start()

    def finish(s_refs, l_refs, sems):
        for cp in copies(s_refs, l_refs, sems):
            cp.wait()

    return _Carry(sums, [jax.ShapeDtypeStruct(s.shape, s.dtype) for s in sums], {},
                  [pltpu.SemaphoreType.DMA((n, 3)), pltpu.SemaphoreType.DMA((n, 3))], start, finish)


def _join_carry(outs, layers):
    n = len(outs)

    def copy(o_refs, sems, w, mine):
        send_sems, recv_sems = sems
        x, y, c = _place()
        r = outs[w].shape[1]
        rows = o_refs[w].at[layers[w], pl.ds((c if mine else 1 - c) * (r // 2), r // 2), :]
        return pltpu.make_async_remote_copy(
            src_ref=rows, dst_ref=rows, send_sem=send_sems.at[w], recv_sem=recv_sems.at[w],
            device_id=(x, y, 1 - c), device_id_type=MESH)

    def start(i_refs, o_refs, sems):
        for w in range(n):
            copy(o_refs, sems, w, True).start()

    def finish(i_refs, o_refs, sems):
        for w in range(n):
            copy(o_refs, sems, w, True).wait_send()
        for w in range(n):
            copy(o_refs, sems, w, False).wait_recv()

    return _Carry(outs, [jax.ShapeDtypeStruct(o.shape, o.dtype) for o in outs], {w: w for w in range(n)},
                  [pltpu.SemaphoreType.DMA((n,)), pltpu.SemaphoreType.DMA((n,))], start, finish)


PF_C, PF_Q, PF_QS = 0, 1, 2


def _add_sibling_half(g, landed, pf, name):
    _, r, cols = g.shape
    rh = r // 2
    tr = _row_tile(rh, cols)
    nr = rh // tr

    def body(pf_ref, g_ref, l_ref, o_ref):
        o_ref[...] = (g_ref[...] + l_ref[...]).astype(BF16)

    return pl.pallas_call(
        body, name=name,
        out_shape=jax.ShapeDtypeStruct((3, rh, cols), BF16),
        grid_spec=pltpu.PrefetchScalarGridSpec(
            num_scalar_prefetch=1, grid=(3, nr),
            in_specs=[pl.BlockSpec((1, tr, cols), lambda j, i, pf_ref: (pf_ref[PF_QS + j], pf_ref[PF_C] * nr + i, 0)),
                      pl.BlockSpec((1, tr, cols), lambda j, i, pf_ref: (pf_ref[PF_QS + j], i, 0))],
            out_specs=pl.BlockSpec((1, tr, cols), lambda j, i, pf_ref: (j, i, 0))),
        compiler_params=_cparams(2),
    )(pf, g, landed)


def _add_chips(g, landed, from_chips, pf, prev, layer, n_layer, name):
    _, r, cols = g.shape
    rh = r // 2
    tr = _row_tile(rh, cols)
    nr = rh // tr

    def body(pf_ref, g_ref, l_ref, f_ref, *rest):
        o_ref = rest[-1]
        acc = g_ref[0] + l_ref[0]
        for j in range(3):
            acc = acc + f_ref[j].astype(F32)
        o_ref[0] = acc

    in_specs = [pl.BlockSpec((1, tr, cols), lambda i, pf_ref: (pf_ref[PF_Q], pf_ref[PF_C] * nr + i, 0)),
                pl.BlockSpec((1, tr, cols), lambda i, pf_ref: (pf_ref[PF_Q], i, 0)),
                pl.BlockSpec((3, tr, cols), lambda i, pf_ref: (0, i, 0))]
    args = [pf, g, landed, from_chips]
    aliases = {}
    if prev is not None:
        in_specs.append(pl.BlockSpec(memory_space=pl.ANY))
        args.append(prev)
        aliases = {4: 0}
    return pl.pallas_call(
        body, name=name,
        out_shape=jax.ShapeDtypeStruct((n_layer, r, cols), F32),
        grid_spec=pltpu.PrefetchScalarGridSpec(
            num_scalar_prefetch=1, grid=(nr,), in_specs=in_specs,
            out_specs=pl.BlockSpec((1, tr, cols), lambda i, pf_ref: (layer, pf_ref[PF_C] * nr + i, 0))),
        input_output_aliases=aliases,
        compiler_params=_cparams(1),
    )(*args)


def _allgather_carry(blocks):
    n = len(blocks)

    def copies(b_refs, o_refs, sems):
        send_sems, recv_sems = sems
        x, y, c = _place()
        chips, _ = _other_chips(x, y)

        def place(w, px, py, pc):
            m = blocks[w].shape[0]
            return o_refs[w].at[pl.ds((4 * px + 2 * py + pc) * m, m), :]

        def own_to(w, k, to):
            dst = place(w, x, y, c)
            return pltpu.make_async_remote_copy(src_ref=b_refs[w], dst_ref=dst, send_sem=send_sems.at[w, k],
                                                recv_sem=recv_sems.at[w, k], device_id=to, device_id_type=MESH)

        def landed_from(w, k, px, py, pc):
            blk = place(w, px, py, pc)
            return pltpu.make_async_remote_copy(src_ref=blk, dst_ref=blk, send_sem=send_sems.at[w, k],
                                                recv_sem=recv_sems.at[w, k], device_id=(x, y, 1 - c), device_id_type=MESH)

        return x, y, c, chips, own_to, landed_from

    def start(b_refs, o_refs, sems):
        x, y, c, chips, own_to, _ = copies(b_refs, o_refs, sems)
        for w in range(n):
            own_to(w, 0, (x, y, 1 - c)).start()
            for j, chip in enumerate(chips):
                own_to(w, 1 + j, (*chip, c)).start()

    def mid(b_refs, o_refs, sems):
        x, y, c, chips, _, landed_from = copies(b_refs, o_refs, sems)
        for w in range(n):
            for j, chip in enumerate(chips):
                landed_from(w, 1 + j, *chip, c).wait_recv()
                landed_from(w, 4 + j, *chip, c).start()

    def finish(b_refs, o_refs, sems):
        x, y, c, chips, own_to, landed_from = copies(b_refs, o_refs, sems)
        for w in range(n):
            landed_from(w, 0, x, y, 1 - c).wait_recv()
            for j, chip in enumerate(chips):
                landed_from(w, 4 + j, *chip, 1 - c).wait_recv()
            own_to(w, 0, (x, y, 1 - c)).wait_send()
            for j, chip in enumerate(chips):
                own_to(w, 1 + j, (*chip, c)).wait_send()
                landed_from(w, 4 + j, *chip, c).wait_send()

    return _Carry(blocks, [jax.ShapeDtypeStruct((N_DEV * b.shape[0], b.shape[1]), b.dtype) for b in blocks], {},
                  [pltpu.SemaphoreType.DMA((n, 7)), pltpu.SemaphoreType.DMA((n, 7))], start, finish, mid)


def _sum_devices(gathered, own, me_arr, name):
    m, n = own.shape
    tr = _row_tile(m, n, itemsize=own.dtype.itemsize, target_bytes=256 * 1024)
    nr = m // tr

    def body(me_ref, *refs):
        g_refs, own_ref, o_ref = refs[:N_DEV], refs[N_DEV], refs[N_DEV + 1]
        me = me_ref[0]
        acc = None
        for dev in range(N_DEV):
            term = jnp.where(me == dev, own_ref[...], g_refs[dev][...]).astype(F32)
            acc = term if acc is None else acc + term
        o_ref[...] = acc

    def dev_rows(dev):
        return pl.BlockSpec((tr, n), lambda i, me_ref: (dev * nr + i, 0))

    return pl.pallas_call(
        body, name=name,
        out_shape=jax.ShapeDtypeStruct((m, n), F32),
        grid_spec=pltpu.PrefetchScalarGridSpec(
            num_scalar_prefetch=1, grid=(nr,),
            in_specs=[dev_rows(dev) for dev in range(N_DEV)] + [pl.BlockSpec((tr, n), lambda i, me_ref: (i, 0))],
            out_specs=pl.BlockSpec((tr, n), lambda i, me_ref: (i, 0))),
        compiler_params=_cparams(1),
    )(me_arr, *([gathered] * N_DEV), own)


def _mod_forward(c_all, w_mod, b_mod_shard, name):
    n_layer, d, mq = w_mod.shape

    def body(c_ref, w_ref, b_ref, o_ref):
        cv = c_ref[...]
        o_ref[...] = _dot(cv * _sigmoid(cv), w_ref[0]) + b_ref[0]

    return pl.pallas_call(
        body, name=name, grid=(n_layer,),
        out_shape=jax.ShapeDtypeStruct((n_layer * N_DEV, mq), F32),
        in_specs=[_full((N_DEV, d)), pl.BlockSpec((1, d, mq), lambda l: (l, 0, 0)),
                  pl.BlockSpec((1, 1, mq), lambda l: (l, 0, 0))],
        out_specs=pl.BlockSpec((N_DEV, mq), lambda l: (l, 0)),
        compiler_params=_cparams(1),
    )(c_all, w_mod, b_mod_shard.reshape(n_layer, 1, mq))


def _mod_backward(c_all_t, dmod_shard, name):
    n_layer, _, mq = dmod_shard.shape
    d = c_all_t.shape[0]

    def body(c_ref, dm_ref, o_ref):
        cv = c_ref[...]
        o_ref[0] = _dot(cv * _sigmoid(cv), dm_ref[0])

    return pl.pallas_call(
        body, name=name, grid=(n_layer,),
        out_shape=jax.ShapeDtypeStruct((n_layer, d, mq), F32),
        in_specs=[_full((d, N_DEV)), pl.BlockSpec((1, N_DEV, mq), lambda l: (l, 0, 0))],
        out_specs=pl.BlockSpec((1, d, mq), lambda l: (l, 0, 0)),
        compiler_params=_cparams(1),
    )(c_all_t, dmod_shard)


def _norm_proj(x, mod, vec, w_in, name, carries=()):
    s, d = x.shape
    nq = w_in.shape[2]
    ts = min(TOKENS_MATMUL_TILE, s)

    def body(x_ref, mod_ref, vec_ref, w_ref, h_ref, p_ref, dgel_ref):
        xn, _ = _rms(x_ref[...])
        gm = vec_ref[V_G_PRE_MIX:V_G_PRE_MIX + 1, :] * (1.0 + mod_ref[M_SC_M:M_SC_M + 1, :])
        h = (xn * gm + mod_ref[M_SH_M:M_SH_M + 1, :]).astype(BF16)
        h_ref[...] = h
        for qb in range(N_CHIP):
            pq = _dot(h, w_ref[qb])
            for k in range(N_CHIP * nq // d):
                lo, hi = max(qb * nq, k * d), min((qb + 1) * nq, (k + 1) * d)
                if lo >= hi:
                    continue
                piece = pq[:, lo - qb * nq:hi - qb * nq]
                if k == 4:
                    piece, dgel = _gelu_and_grad(piece)
                    dgel_ref[:, lo - 4 * d:hi - 4 * d] = dgel.astype(BF16)
                elif k >= 5:
                    piece = _sigmoid(piece)
                p_ref[:, lo:hi] = piece.astype(BF16)

    tile = pl.BlockSpec((ts, d), lambda i: (i, 0))
    return _pcall(
        body, name=name, grid=(s // ts,),
        out_shape=[jax.ShapeDtypeStruct((s, d), BF16), jax.ShapeDtypeStruct((s, N_CHIP * nq), BF16),
                   jax.ShapeDtypeStruct((s, d), BF16)],
        in_specs=[tile, _full(mod.shape), _full(vec.shape), _full(w_in.shape)],
        out_specs=[tile, pl.BlockSpec((ts, N_CHIP * nq), lambda i: (i, 0)), tile],
        args=[x, mod, vec, w_in], carries=carries)


def _gate_pre(xb2_b, wg_ref, n_head, bw):
    zr, zi = [], []
    for hd in range(n_head):
        z = _dot(xb2_b[:, hd * bw:(hd + 1) * bw], wg_ref[hd])
        zr.append(z[:, :bw])
        zi.append(z[:, bw:])
    return jnp.concatenate(zr, axis=1), jnp.concatenate(zi, axis=1)


def _lru_coeffs(xb2, wg_ref, vec_ref, n_head, bw):
    zr, zi = _gate_pre(xb2.astype(BF16), wg_ref, n_head, bw)
    r = _sigmoid(zr + vec_ref[V_B_GATE_R:V_B_GATE_R + 1, :])
    gi = _sigmoid(zi + vec_ref[V_B_GATE_I:V_B_GATE_I + 1, :])
    sp = _softplus(-vec_ref[V_LAMBDA:V_LAMBDA + 1, :])
    log_a = (-LRU_C) * r * sp
    a = jnp.exp(log_a)
    mult = jnp.sqrt(_neg_expm1(2.0 * log_a))
    return r, gi, sp, a, mult


def _mixer_forward(x, proj, mod, vec, wg, w_a_out, w_b_out, w_o, name, carries=()):
    s, d = x.shape
    n_head, bw, _ = wg.shape
    ts = min(TOKENS_MIXER_TILE, s)

    def body(x_ref, p_ref, mod_ref, vec_ref, wg_ref, wa_ref, wb_ref, wo_ref,
             x1_ref, conva_ref, xb2_ref, hh_ref, abm_ref, pa_ref, pb_ref, y_ref,
             r_ref, gi_ref, a_ref, mult_ref,
             cv_tail, xb_tail, h_last, a_buf, b_buf, c_buf):
        i = pl.program_id(0)

        @pl.when(i == 0)
        def _():
            cv_tail[...] = jnp.zeros_like(cv_tail)
            xb_tail[...] = jnp.zeros_like(xb_tail)
            h_last[...] = jnp.zeros_like(h_last)

        def seg(k):
            return p_ref[:, k * d:(k + 1) * d].astype(F32)

        def vrow(k):
            return vec_ref[k:k + 1, :]

        b_a, c_a, v_a, x_b, gel, sa, sb = (seg(k) for k in range(7))
        cv = c_a * v_a
        prev_cv = cv_tail[...]
        conv_a = (vrow(V_CONV_A_B) + vrow(V_CONV_A_W) * _shift_down(cv, 2, prev_cv)
                  + vrow(V_CONV_A_W + 1) * _shift_down(cv, 1, prev_cv) + vrow(V_CONV_A_W + 2) * cv)
        cv_tail[...] = cv[ts - SUBLANES:]
        y_a = b_a * conv_a
        prev_xb = xb_tail[...]
        xb2 = (vrow(V_CONV_B_B) + vrow(V_CONV_B_W) * _shift_down(x_b, 3, prev_xb)
               + vrow(V_CONV_B_W + 1) * _shift_down(x_b, 2, prev_xb)
               + vrow(V_CONV_B_W + 2) * _shift_down(x_b, 1, prev_xb) + vrow(V_CONV_B_W + 3) * x_b)
        xb_tail[...] = x_b[ts - SUBLANES:]
        r, gi, _, a, mult = _lru_coeffs(xb2, wg_ref, vec_ref, n_head, bw)
        r_ref[...] = r
        gi_ref[...] = gi
        a_ref[...] = a
        mult_ref[...] = mult
        hh = _scan_two_level(a, mult * gi * xb2, h_last[SUBLANES - 1:SUBLANES, :], a_buf, b_buf, c_buf, reverse=False)
        h_last[...] = hh[ts - SUBLANES:]
        y_b = hh * gel
        ya_b, yb_b = y_a.astype(BF16), y_b.astype(BF16)
        pa = _dot(ya_b, wa_ref[...])
        pb = _dot(yb_b, wb_ref[...])
        m = (sa * pa + sb * pb).astype(BF16)
        y = _dot(m, wo_ref[...])
        yn, _ = _rms(y)
        gg = mod_ref[M_GT_M:M_GT_M + 1, :] * vrow(V_G_POST_MIX)
        x1_ref[...] = x_ref[...] + yn * gg
        conva_ref[...] = conv_a.astype(BF16)
        xb2_ref[...] = xb2
        hh_ref[...] = hh
        abm_ref[0] = ya_b
        abm_ref[1] = yb_b
        abm_ref[2] = m
        pa_ref[...] = pa.astype(BF16)
        pb_ref[...] = pb.astype(BF16)
        y_ref[...] = y.astype(BF16)

    tile = pl.BlockSpec((ts, d), lambda i: (i, 0))
    tile3 = pl.BlockSpec((3, ts, d), lambda i: (0, i, 0))
    sd = lambda dt: jax.ShapeDtypeStruct((s, d), dt)
    return _pcall(
        body, name=name, grid=(s // ts,),
        out_shape=[sd(F32), sd(BF16), sd(F32), sd(F32), jax.ShapeDtypeStruct((3, s, d), BF16), sd(BF16), sd(BF16), sd(BF16),
                   sd(F32), sd(F32), sd(F32), sd(F32)],
        in_specs=[tile, pl.BlockSpec((ts, 7 * d), lambda i: (i, 0)), _full(mod.shape), _full(vec.shape),
                  _full(wg.shape), _full(w_a_out.shape), _full(w_b_out.shape), _full(w_o.shape)],
        out_specs=[tile] * 4 + [tile3] + [tile] * 7,
        scratch_shapes=[pltpu.VMEM((SUBLANES, d), F32)] * 3 + [pltpu.VMEM((d // LANES, ts, LANES), F32)] * 2
                       + [pltpu.VMEM((ts // SCAN_GROUP, d), F32)],
        args=[x, proj, mod, vec, wg, w_a_out, w_b_out, w_o], carries=carries)


def _mlp_forward(x1, mod, vec, w_up, w_down, name, carries=(), target=None):
    s, d = x1.shape
    fq = w_up.shape[2]
    ts = min(TOKENS_MATMUL_TILE, s)

    def body(x_ref, *refs):
        if target is None:
            mod_ref, vec_ref, wu_ref, wd_ref, x2_ref, h2_ref, up_ref, y2_ref = refs
        else:
            t_ref, mod_ref, vec_ref, wu_ref, wd_ref, x2_ref, h2_ref, up_ref, y2_ref, loss_ref = refs
        x = x_ref[...]
        xn, _ = _rms(x)
        gm = vec_ref[V_G_PRE_MLP:V_G_PRE_MLP + 1, :] * (1.0 + mod_ref[M_SC_F:M_SC_F + 1, :])
        h2 = (xn * gm + mod_ref[M_SH_F:M_SH_F + 1, :]).astype(BF16)
        h2_ref[...] = h2
        y2 = jnp.zeros((ts, d), F32)
        for qb in range(N_CHIP):
            up = _dot(h2, wu_ref[qb])
            up_ref[:, qb * fq:(qb + 1) * fq] = up.astype(BF16)
            ru = jnp.maximum(up, 0.0)
            y2 = y2 + _dot((ru * ru).astype(BF16), wd_ref[qb])
        y2_ref[...] = y2.astype(BF16)
        yn, _ = _rms(y2)
        gg = mod_ref[M_GT_F:M_GT_F + 1, :] * vec_ref[V_G_POST_MLP:V_G_POST_MLP + 1, :]
        x2 = x + yn * gg
        if target is None:
            x2_ref[...] = x2
        else:
            @pl.when(pl.program_id(0) == 0)
            def _():
                loss_ref[...] = jnp.zeros_like(loss_ref)

            err = x2 - t_ref[...]
            x2_ref[...] = err * (1.0 / d)
            loss_ref[...] += jnp.sum(jnp.sum(err * err, axis=1, keepdims=True), axis=0, keepdims=True) * (0.5 / d)

    tile = pl.BlockSpec((ts, d), lambda i: (i, 0))
    last = target is not None
    return _pcall(
        body, name=name, grid=(s // ts,),
        out_shape=[jax.ShapeDtypeStruct((s, d), F32), jax.ShapeDtypeStruct((s, d), BF16),
                   jax.ShapeDtypeStruct((s, N_CHIP * fq), BF16), jax.ShapeDtypeStruct((s, d), BF16)]
                  + ([jax.ShapeDtypeStruct((SUBLANES, LANES), F32)] if last else []),
        in_specs=[tile] + ([tile] if last else []) + [_full(mod.shape), _full(vec.shape), _full(w_up.shape), _full(w_down.shape)],
        out_specs=[tile, tile, pl.BlockSpec((ts, N_CHIP * fq), lambda i: (i, 0)), tile]
                 + ([_full((SUBLANES, LANES))] if last else []),
        args=[x1] + ([target] if last else []) + [mod, vec, w_up, w_down], carries=carries)


SB3_DSH, SB3_DSC, SB3_DGT, SB3_DG_PRE, SB3_DG_POST = range(5)
SB1_DSH, SB1_DSC, SB1_DG_PRE = range(3)
(SB2_DGT, SB2_DG_POST, SB2_DWA, SB2_DBA, SB2_DWB, SB2_DBB, SB2_DLAM, SB2_DBR, SB2_DBI) = (0, 1, 2, 5, 6, 10, 11, 12, 13)


def _mlp_backward(dx2, x1, y2, up, mod, vec, w_up, w_down, name, carries=()):
    s, d = dx2.shape
    fq = w_up.shape[2]
    ts = min(TOKENS_MIXER_TILE, s)
    n_t = s // ts

    def body(dx2_ref, x_ref, y2_ref, up_ref, mod_ref, vec_ref, wu_ref, wd_ref,
             dx1_ref, dy2_ref, dup_ref, act_ref, small_ref):
        i = pl.program_id(0)

        @pl.when(i == 0)
        def _():
            small_ref[...] = jnp.zeros_like(small_ref)

        dout = dx2_ref[...]
        y2n, ry = _rms(y2_ref[...].astype(F32))
        g_post = vec_ref[V_G_POST_MLP:V_G_POST_MLP + 1, :]
        gt = mod_ref[M_GT_F:M_GT_F + 1, :]
        dgg = _colsum(dout * y2n)
        dy2 = _rms_bwd(dout * (gt * g_post), y2n, ry).astype(BF16)
        dy2_ref[...] = dy2
        dh2 = jnp.zeros((ts, d), F32)
        for qb in range(N_CHIP):
            cols = slice(qb * fq, (qb + 1) * fq)
            dact = _dot_tb(dy2, wd_ref[qb])
            ru = jnp.maximum(up_ref[:, cols].astype(F32), 0.0)
            dup = (dact * (2.0 * ru)).astype(BF16)
            dup_ref[:, cols] = dup
            act_ref[:, cols] = (ru * ru).astype(BF16)
            dh2 = dh2 + _dot_tb(dup, wu_ref[qb])
        xn, r = _rms(x_ref[...])
        g_pre = vec_ref[V_G_PRE_MLP:V_G_PRE_MLP + 1, :]
        sc1 = 1.0 + mod_ref[M_SC_F:M_SC_F + 1, :]
        dsh = _colsum(dh2)
        dgm = _colsum(dh2 * xn)
        dx1_ref[...] = dout + _rms_bwd(dh2 * (g_pre * sc1), xn, r)
        small_ref[SB3_DSH:SB3_DSH + 1, :] += dsh
        small_ref[SB3_DSC:SB3_DSC + 1, :] += dgm
        small_ref[SB3_DGT:SB3_DGT + 1, :] += dgg

        @pl.when(i == n_t - 1)
        def _():
            dgm_t = small_ref[SB3_DSC:SB3_DSC + 1, :]
            dgg_t = small_ref[SB3_DGT:SB3_DGT + 1, :]
            small_ref[SB3_DSC:SB3_DSC + 1, :] = dgm_t * g_pre
            small_ref[SB3_DG_PRE:SB3_DG_PRE + 1, :] = dgm_t * sc1
            small_ref[SB3_DGT:SB3_DGT + 1, :] = dgg_t * g_post
            small_ref[SB3_DG_POST:SB3_DG_POST + 1, :] = dgg_t * gt

    tile = pl.BlockSpec((ts, d), lambda i: (i, 0))
    wide = pl.BlockSpec((ts, N_CHIP * fq), lambda i: (i, 0))
    return _pcall(
        body, name=name, grid=(n_t,),
        out_shape=[jax.ShapeDtypeStruct((s, d), F32), jax.ShapeDtypeStruct((s, d), BF16),
                   jax.ShapeDtypeStruct((s, N_CHIP * fq), BF16), jax.ShapeDtypeStruct((s, N_CHIP * fq), BF16),
                   jax.ShapeDtypeStruct((SUBLANES, d), F32)],
        in_specs=[tile, tile, tile, wide, _full(mod.shape), _full(vec.shape), _full(w_up.shape), _full(w_down.shape)],
        out_specs=[tile, tile, wide, wide, _full((SUBLANES, d))],
        args=[dx2, x1, y2, up, mod, vec, w_up, w_down], carries=carries)


def _mixer_backward(dx1, proj, conva, xb2s, hhs, pas, pbs, ys, rs_, gis, as_, mults, dgels, mod, vec, wg, w_a_out, w_b_out,
                    w_o, name, carries=()):
    s, d = dx1.shape
    n_head, bw, _ = wg.shape
    ts = min(TOKENS_MIXER_TILE, s)
    n_t = s // ts

    def body(dx1_ref, p_ref, conva_ref, xb2_ref, hh_ref, pa_ref, pb_ref, y_ref, r_ref, gi_ref, a_ref, mult_ref, dgel_ref,
             mod_ref, vec_ref, wg_ref, wa_ref, wb_ref, wo_ref,
             dp_ref, dab_ref, small_ref, dwg_ref,
             dconv_head, dxb2_head, a_head, g_head, a_buf, b_buf, c_buf):
        i = pl.program_id(0)

        @pl.when(i == 0)
        def _():
            small_ref[...] = jnp.zeros_like(small_ref)
            dwg_ref[...] = jnp.zeros_like(dwg_ref)
            dconv_head[...] = jnp.zeros_like(dconv_head)
            dxb2_head[...] = jnp.zeros_like(dxb2_head)
            a_head[...] = jnp.zeros_like(a_head)
            g_head[...] = jnp.zeros_like(g_head)

        def seg(k):
            return p_ref[:, k * d:(k + 1) * d].astype(F32)

        def vrow(k):
            return vec_ref[k:k + 1, :]

        def acc(row, val):
            small_ref[row:row + 1, :] += val

        dout = dx1_ref[...]
        yn, ry = _rms(y_ref[...].astype(F32))
        g_post = vrow(V_G_POST_MIX)
        gt = mod_ref[M_GT_M:M_GT_M + 1, :]
        acc(SB2_DGT, _colsum(dout * yn))
        dy = _rms_bwd(dout * (gt * g_post), yn, ry).astype(BF16)
        dab_ref[2] = dy
        dm = _dot_tb(dy, wo_ref[...])
        sa, sb = seg(5), seg(6)
        dpa = (dm * sa).astype(BF16)
        dpb = (dm * sb).astype(BF16)
        dab_ref[0] = dpa
        dab_ref[1] = dpb
        du_a = dm * pa_ref[...].astype(F32) * (sa * (1.0 - sa))
        du_b = dm * pb_ref[...].astype(F32) * (sb * (1.0 - sb))
        dp_ref[:, 5 * d:6 * d] = du_a.astype(BF16)
        dp_ref[:, 6 * d:7 * d] = du_b.astype(BF16)
        dy_a = _dot_tb(dpa, wa_ref[...])
        dy_b = _dot_tb(dpb, wb_ref[...])

        b_a, c_a, v_a = seg(0), seg(1), seg(2)
        dp_ref[:, 0:d] = (dy_a * conva_ref[...].astype(F32)).astype(BF16)
        dconv = dy_a * b_a
        nxt = dconv_head[...]
        d1 = _shift_up(dconv, 1, nxt)
        d2 = _shift_up(dconv, 2, nxt)
        dconv_head[...] = dconv[:SUBLANES]
        dcv = vrow(V_CONV_A_W + 2) * dconv + vrow(V_CONV_A_W + 1) * d1 + vrow(V_CONV_A_W) * d2
        cv = c_a * v_a
        acc(SB2_DWA + 2, _colsum(cv * dconv))
        acc(SB2_DWA + 1, _colsum(cv * d1))
        acc(SB2_DWA, _colsum(cv * d2))
        acc(SB2_DBA, _colsum(dconv))
        dp_ref[:, d:2 * d] = (dcv * v_a).astype(BF16)
        dp_ref[:, 2 * d:3 * d] = (dcv * c_a).astype(BF16)

        x_b, gel = seg(3), seg(4)
        hh = hh_ref[...]
        dp_ref[:, 4 * d:5 * d] = (dy_b * hh * dgel_ref[...].astype(F32)).astype(BF16)
        dhh = dy_b * gel
        xb2 = xb2_ref[...]
        r, gi, a, mult = r_ref[...], gi_ref[...], a_ref[...], mult_ref[...]
        sp = _softplus(-vrow(V_LAMBDA))
        a_next = _shift_up(a, 1, a_head[...])
        g = _scan_two_level(a_next, dhh, g_head[0:1, :], a_buf, b_buf, c_buf, reverse=True)
        a_head[...] = a[:SUBLANES]
        g_head[...] = g[:SUBLANES]
        gix = gi * xb2
        gm = g * mult
        dlog_a = g * (hh - mult * gix) - (g * gix) * (a * a / mult)
        dgi = gm * xb2
        dxb2 = gm * gi
        acc(SB2_DLAM, _colsum(dlog_a * r))
        dzr = dlog_a * ((-LRU_C) * sp) * (r * (1.0 - r))
        dzi = dgi * (gi * (1.0 - gi))
        acc(SB2_DBR, _colsum(dzr))
        acc(SB2_DBI, _colsum(dzi))
        xb2_b = xb2.astype(BF16)
        back = []
        for hd in range(n_head):
            cols = slice(hd * bw, (hd + 1) * bw)
            dz = jnp.concatenate([dzr[:, cols], dzi[:, cols]], axis=1).astype(BF16)
            back.append(_dot_tb(dz, wg_ref[hd]))
            dwg_ref[hd] += _dot_ta(xb2_b[:, cols], dz)
        dxb2 = dxb2 + jnp.concatenate(back, axis=1)
        nxt = dxb2_head[...]
        e1 = _shift_up(dxb2, 1, nxt)
        e2 = _shift_up(dxb2, 2, nxt)
        e3 = _shift_up(dxb2, 3, nxt)
        dxb2_head[...] = dxb2[:SUBLANES]
        dp_ref[:, 3 * d:4 * d] = (vrow(V_CONV_B_W + 3) * dxb2 + vrow(V_CONV_B_W + 2) * e1
                                  + vrow(V_CONV_B_W + 1) * e2 + vrow(V_CONV_B_W) * e3).astype(BF16)
        acc(SB2_DWB + 3, _colsum(x_b * dxb2))
        acc(SB2_DWB + 2, _colsum(x_b * e1))
        acc(SB2_DWB + 1, _colsum(x_b * e2))
        acc(SB2_DWB, _colsum(x_b * e3))
        acc(SB2_DBB, _colsum(dxb2))

        @pl.when(i == n_t - 1)
        def _():
            dgg_t = small_ref[SB2_DGT:SB2_DGT + 1, :]
            small_ref[SB2_DGT:SB2_DGT + 1, :] = dgg_t * g_post
            small_ref[SB2_DG_POST:SB2_DG_POST + 1, :] = dgg_t * gt
            lam = vrow(V_LAMBDA)
            small_ref[SB2_DLAM:SB2_DLAM + 1, :] = small_ref[SB2_DLAM:SB2_DLAM + 1, :] * (LRU_C * _sigmoid(-lam))

    rev = lambda i: (n_t - 1 - i, 0)
    tile = pl.BlockSpec((ts, d), rev)
    wide = pl.BlockSpec((ts, 7 * d), rev)
    sd = lambda dt: jax.ShapeDtypeStruct((s, d), dt)
    return _pcall(
        body, name=name, grid=(n_t,),
        out_shape=[jax.ShapeDtypeStruct((s, 7 * d), BF16), jax.ShapeDtypeStruct((3, s, d), BF16),
                   jax.ShapeDtypeStruct((2 * SUBLANES, d), F32), jax.ShapeDtypeStruct(wg.shape, F32)],
        in_specs=[tile, wide] + [tile] * 11 + [_full(mod.shape), _full(vec.shape),
                  _full(wg.shape), _full(w_a_out.shape), _full(w_b_out.shape), _full(w_o.shape)],
        out_specs=[wide, pl.BlockSpec((3, ts, d), lambda i: (0, n_t - 1 - i, 0)), _full((2 * SUBLANES, d)), _full(wg.shape)],
        scratch_shapes=[pltpu.VMEM((SUBLANES, d), F32)] * 4 + [pltpu.VMEM((d // LANES, ts, LANES), F32)] * 2
                       + [pltpu.VMEM((ts // SCAN_GROUP, d), F32)],
        args=[dx1, proj, conva, xb2s, hhs, pas, pbs, ys, rs_, gis, as_, mults, dgels, mod, vec, wg, w_a_out, w_b_out, w_o],
        carries=carries)


def _proj_backward(dproj, dx1, x, mod, vec, w_in, name, carries=()):
    s, d = x.shape
    nq = w_in.shape[2]
    ts = min(TOKENS_MATMUL_TILE, s)
    n_t = s // ts

    def body(dp_ref, dx1_ref, x_ref, mod_ref, vec_ref, w_ref, dx_ref, small_ref):
        i = pl.program_id(0)

        @pl.when(i == 0)
        def _():
            small_ref[...] = jnp.zeros_like(small_ref)

        dh = jnp.zeros((ts, d), F32)
        for qb in range(N_CHIP):
            dh = dh + _dot_tb(dp_ref[:, qb * nq:(qb + 1) * nq], w_ref[qb])
        xn, r = _rms(x_ref[...])
        g_pre = vec_ref[V_G_PRE_MIX:V_G_PRE_MIX + 1, :]
        sc1 = 1.0 + mod_ref[M_SC_M:M_SC_M + 1, :]
        dx_ref[...] = dx1_ref[...] + _rms_bwd(dh * (g_pre * sc1), xn, r)
        small_ref[SB1_DSH:SB1_DSH + 1, :] += _colsum(dh)
        small_ref[SB1_DSC:SB1_DSC + 1, :] += _colsum(dh * xn)

        @pl.when(i == n_t - 1)
        def _():
            dgm_t = small_ref[SB1_DSC:SB1_DSC + 1, :]
            small_ref[SB1_DSC:SB1_DSC + 1, :] = dgm_t * g_pre
            small_ref[SB1_DG_PRE:SB1_DG_PRE + 1, :] = dgm_t * sc1

    tile = pl.BlockSpec((ts, d), lambda i: (i, 0))
    return _pcall(
        body, name=name, grid=(n_t,),
        out_shape=[jax.ShapeDtypeStruct((s, d), F32), jax.ShapeDtypeStruct((SUBLANES, d), F32)],
        in_specs=[pl.BlockSpec((ts, N_CHIP * nq), lambda i: (i, 0)), tile, tile, _full(mod.shape), _full(vec.shape),
                  _full(w_in.shape)],
        out_specs=[tile, _full((SUBLANES, d))],
        args=[dproj, dx1, x, mod, vec, w_in], carries=carries)


def _weight_grad(a, b, name, col_blocks=1, tk=512, carries=()):
    s, k = a.shape
    n = b.shape[1]
    tn = n // col_blocks
    tk = min(tk, k)

    def body(a_ref, b_ref, o_ref):
        o_ref[0] = _dot_ta(a_ref[...], b_ref[...])

    (out,), carried = _pcall(
        body, name=name, grid=(col_blocks, k // tk),
        out_shape=[jax.ShapeDtypeStruct((col_blocks, k, tn), F32)],
        in_specs=[pl.BlockSpec((s, tk), lambda j, i: (0, i)), pl.BlockSpec((s, tn), lambda j, i: (0, j))],
        out_specs=[pl.BlockSpec((1, tk, tn), lambda j, i: (j, i, 0))],
        args=[a, b], carries=carries)
    return out, carried


def _weight_grad_stacked(a3, b3, name, tk=512, carries=()):
    n_g, s, k = a3.shape
    n = b3.shape[2]
    kq = k // N_CHIP
    tk = min(tk, k)
    chips_per_tile = tk // kq

    def body(a_ref, b_ref, o_ref):
        o_ref[...] = _dot_ta(a_ref[...], b_ref[...]).reshape(chips_per_tile, kq, n)

    (out,), carried = _pcall(
        body, name=name, grid=(n_g, k // tk),
        out_shape=[jax.ShapeDtypeStruct((N_CHIP, n_g, kq, n), F32)],
        in_specs=[pl.BlockSpec((None, s, tk), lambda g, i: (g, 0, i)), pl.BlockSpec((None, s, n), lambda g, i: (g, 0, 0))],
        out_specs=[pl.BlockSpec((chips_per_tile, None, kq, n), lambda g, i: (i, g, 0, 0))],
        args=[a3, b3], carries=carries)
    return out.reshape(N_CHIP, n_g * kq, n), carried


def _adamw(items, name, copy_grad=False, carries=()):
    shape = items[0][0].shape
    cols = shape[-1]
    rows = items[0][0].size // cols
    tr = _row_tile(rows, cols, target_bytes=1024 * 1024 // len(items))
    c1 = 1.0 - ADAM_B1 ** ADAM_STEP
    c2 = 1.0 - ADAM_B2 ** ADAM_STEP
    n_out = 4 if copy_grad else 3
    n = len(items)

    def body(*refs):
        for k in range(n):
            w_ref, g_ref, m_ref, v_ref = refs[4 * k:4 * k + 4]
            outs = refs[4 * n + n_out * k:4 * n + n_out * (k + 1)]
            gv = g_ref[...]
            nm = ADAM_B1 * m_ref[...] + (1.0 - ADAM_B1) * gv
            nv = ADAM_B2 * v_ref[...] + (1.0 - ADAM_B2) * (gv * gv)
            outs[0][...] = (-ADAM_LR) * ((nm / c1) / (jnp.sqrt(nv / c2) + ADAM_EPS) + ADAM_WD * w_ref[...])
            outs[1][...] = nm
            outs[2][...] = nv
            if copy_grad:
                outs[3][...] = gv

    spec = pl.BlockSpec((tr, cols), lambda i: (i, 0))
    outs, carried = _pcall(
        body, name=name, grid=(rows // tr,),
        out_shape=[jax.ShapeDtypeStruct((rows, cols), F32)] * (n_out * n),
        in_specs=[spec] * (4 * n), out_specs=[spec] * (n_out * n),
        args=[t.reshape(rows, cols) for item in items for t in item], carries=carries)
    return [tuple(o.reshape(shape) for o in outs[n_out * k:n_out * (k + 1)]) for k in range(n)], carried


def kernel(x, c, w_mod, b_mod, g_pre_mix, g_post_mix, w_in, conv_a_w, conv_a_b, w_a_out, conv_b_w, conv_b_b, w_gate_r, b_gate_r, w_gate_i, b_gate_i, lru_lambda, w_b_out, w_o, g_pre_mlp, g_post_mlp, w_mlp_up, w_mlp_down, loss_target, m_w_mod, m_b_mod, m_g_pre_mix, m_g_post_mix, m_w_in, m_conv_a_w, m_conv_a_b, m_w_a_out, m_conv_b_w, m_conv_b_b, m_w_gate_r, m_b_gate_r, m_w_gate_i, m_b_gate_i, m_lru_lambda, m_w_b_out, m_w_o, m_g_pre_mlp, m_g_post_mlp, m_w_mlp_up, m_w_mlp_down, v_w_mod, v_b_mod, v_g_pre_mix, v_g_post_mix, v_w_in, v_conv_a_w, v_conv_a_b, v_w_a_out, v_conv_b_w, v_conv_b_b, v_w_gate_r, v_b_gate_r, v_w_gate_i, v_b_gate_i, v_lru_lambda, v_w_b_out, v_w_o, v_g_pre_mlp, v_g_post_mlp, v_w_mlp_up, v_w_mlp_down):
    weights = dict(w_mod=w_mod, b_mod=b_mod, g_pre_mix=g_pre_mix, g_post_mix=g_post_mix, w_in=w_in, conv_a_w=conv_a_w,
                   conv_a_b=conv_a_b, w_a_out=w_a_out, conv_b_w=conv_b_w, conv_b_b=conv_b_b, w_gate_r=w_gate_r,
                   b_gate_r=b_gate_r, w_gate_i=w_gate_i, b_gate_i=b_gate_i, lru_lambda=lru_lambda, w_b_out=w_b_out,
                   w_o=w_o, g_pre_mlp=g_pre_mlp, g_post_mlp=g_post_mlp, w_mlp_up=w_mlp_up, w_mlp_down=w_mlp_down)
    mom1 = dict(w_mod=m_w_mod, b_mod=m_b_mod, g_pre_mix=m_g_pre_mix, g_post_mix=m_g_post_mix, w_in=m_w_in,
                conv_a_w=m_conv_a_w, conv_a_b=m_conv_a_b, w_a_out=m_w_a_out, conv_b_w=m_conv_b_w, conv_b_b=m_conv_b_b,
                w_gate_r=m_w_gate_r, b_gate_r=m_b_gate_r, w_gate_i=m_w_gate_i, b_gate_i=m_b_gate_i,
                lru_lambda=m_lru_lambda, w_b_out=m_w_b_out, w_o=m_w_o, g_pre_mlp=m_g_pre_mlp, g_post_mlp=m_g_post_mlp,
                w_mlp_up=m_w_mlp_up, w_mlp_down=m_w_mlp_down)
    mom2 = dict(w_mod=v_w_mod, b_mod=v_b_mod, g_pre_mix=v_g_pre_mix, g_post_mix=v_g_post_mix, w_in=v_w_in,
                conv_a_w=v_conv_a_w, conv_a_b=v_conv_a_b, w_a_out=v_w_a_out, conv_b_w=v_conv_b_w, conv_b_b=v_conv_b_b,
                w_gate_r=v_w_gate_r, b_gate_r=v_b_gate_r, w_gate_i=v_w_gate_i, b_gate_i=v_b_gate_i,
                lru_lambda=v_lru_lambda, w_b_out=v_w_b_out, w_o=v_w_o, g_pre_mlp=v_g_pre_mlp, g_post_mlp=v_g_post_mlp,
                w_mlp_up=v_w_mlp_up, w_mlp_down=v_w_mlp_down)
    names = list(weights)

    n_layer = w_in.shape[0]
    s, d = x.shape[1], x.shape[2]
    n_head, bw = w_gate_r.shape[1], w_gate_r.shape[2]
    dq = d // N_CHIP
    mq = w_mod.shape[2]
    n_mod = (N_CHIP * mq) // d
    ka, kb = conv_a_w.shape[1], conv_b_w.shape[1]

    mx, my, mc = _place()
    q_me = 2 * mx + my
    q_arr = jnp.reshape(q_me, (1,)).astype(jnp.int32)

    me_dev = 4 * mx + 2 * my + mc
    me_arr = jnp.reshape(me_dev, (1,)).astype(jnp.int32)

    big_names = ["w_in", "w_a_out", "w_b_out", "w_o", "w_mlp_up", "w_mlp_down"]
    groups = [["w_in"], ["w_a_out", "w_b_out", "w_o"], ["w_mlp_up", "w_mlp_down"]]
    placed = {("w_in", 0): _cast_place(w_in, 0, q_arr, "cast_place_w_in_0")}
    wfull = [dict() for _ in range(n_layer)]
    riders = {}
    for l in range(n_layer):
        riders.setdefault(3 * l - 1, []).append(([("w_in", l)], 1.0))
        riders.setdefault(3 * l - 2 if l else 0, []).append(([(nm, l) for nm in groups[1]], 0.9 if l else 0.5))
        riders.setdefault(3 * l, []).append(([("w_mlp_up", l)], 0.7 if l else 1.0))
        riders.setdefault(3 * l + 1, []).insert(0, ([("w_mlp_down", l)], 0.5))

    def gather_carry(call):
        return [_gather_carry([placed[k] for k in keys], frac) for keys, frac in riders.get(call, [])]

    def gathered(call, carried):
        for (keys, _), ws in zip(riders.get(call, []), carried):
            for (nm, l), w in zip(keys, ws):
                wfull[l][nm] = w.reshape(d, d) if nm in groups[1] else w

    n_conv_rows = n_layer * (ka + kb)
    conv_blk = -(-n_conv_rows // SUBLANES) * SUBLANES
    blk_rows = SUBLANES + conv_blk
    conv_rows = jnp.concatenate([jnp.concatenate([conv_a_w[l], conv_b_w[l]], axis=0) for l in range(n_layer)], axis=0)
    conv_rows = jnp.pad(conv_rows, ((0, conv_blk - n_conv_rows), (0, d - dq)))
    c_conv = jnp.concatenate([jnp.pad(c, ((0, SUBLANES - 1), (0, 0))), conv_rows], axis=0)
    rest = [(nm, l) for l in range(n_layer) for nm in big_names if (nm, l) != ("w_in", 0)]
    rest_placed, carried = _cast_place_all([(weights[nm], l) for nm, l in rest], q_arr, "cast_place_rest",
                                           carries=gather_carry(-1) + [_allgather_carry([c_conv])])
    placed.update(zip(rest, rest_placed))
    gathered(-1, carried[:1])
    gathered1 = lax.dynamic_update_slice(carried[1][0], c_conv, (me_dev * blk_rows, 0)).reshape(N_DEV, blk_rows, d)
    c_all = gathered1[:, 0, :]
    conv_full = jnp.concatenate([gathered1[2 * qb, SUBLANES:SUBLANES + n_conv_rows, :dq] for qb in range(N_CHIP)], axis=1)

    b_mod_shard = lax.dynamic_slice_in_dim(b_mod, q_me * mq, mq, axis=1)
    mod_part = _mod_forward(c_all, w_mod, b_mod_shard, "mod_forward")
    gathered2 = _all_gather_small(mod_part, "gather_mod").reshape(N_DEV, n_layer, N_DEV, mq)
    mod_rows = jnp.concatenate(
        [lax.dynamic_index_in_dim(gathered2[2 * qb], me_dev, axis=1, keepdims=False) for qb in range(N_CHIP)], axis=1)
    mods = [jnp.pad(mod_rows[l].reshape(n_mod, d), ((0, SUBLANES - n_mod), (0, 0))) for l in range(n_layer)]

    vecs = []
    for l in range(n_layer):
        base = l * (ka + kb)
        rows = [g_pre_mix[l], g_post_mix[l], conv_a_b[l], conv_b_b[l], b_gate_r[l], b_gate_i[l], lru_lambda[l],
                g_pre_mlp[l], g_post_mlp[l]]
        vecs.append(jnp.concatenate([jnp.stack(rows, axis=0), conv_full[base:base + ka + kb]], axis=0))

    wgs =[jnp.concatenate([w_gate_r[l], w_gate_i[l]], axis=-1).astype(BF16) for l in range(n_layer)]

    xs = x[0]
    saved = []
    for l in range(n_layer):
        wl = wfull[l]
        (h, proj, dgel), carried = _norm_proj(xs, mods[l], vecs[l], wl["w_in"], f"norm_proj_{l}", gather_carry(3 * l))
        gathered(3 * l, carried)
        (x1, conva, xb2, hh, abm, pa, pb, yy, gr, ggi, ga, gmult), carried = _mixer_forward(
            xs, proj, mods[l], vecs[l], wgs[l], wl["w_a_out"], wl["w_b_out"], wl["w_o"], f"mixer_forward_{l}",
            gather_carry(3 * l + 1))
        gathered(3 * l + 1, carried)
        (x2, h2, up, y2, *loss_tile), carried = _mlp_forward(
            x1, mods[l], vecs[l], wl["w_mlp_up"], wl["w_mlp_down"], f"mlp_forward_{l}", gather_carry(3 * l + 2),
            target=loss_target[0] if l == n_layer - 1 else None)
        gathered(3 * l + 2, carried)
        saved.append(dict(x=xs, h=h, proj=proj, x1=x1, conva=conva, xb2=xb2, hh=hh, abm=abm, pa=pa, pb=pb,
                          y=yy, r=gr, gi=ggi, a=ga, mult=gmult, dgel=dgel, h2=h2, up=up, y2=y2))
        xs = x2
    dxs = xs
    loss = lax.psum(loss_tile[0][0, 0], ("x", "y", "c"))

    chips_q = [q_me ^ 2, q_me ^ 1, q_me ^ 3]
    pf = jnp.stack([mc, q_me] + chips_q).astype(jnp.int32)
    rs = dict(grad={}, landed={}, to_send={}, from_chips={}, out={})
    to_exchange, to_scatter, to_join, to_gather = [], [], [], []
    small_own, small_all = {}, {}

    def ride(call, what, name=None):
        ex = list(to_exchange) if "x" in what else []
        sc = list(to_scatter) if "s" in what else []
        ga = list(to_gather) if "g" in what else []
        jn = []
        for key in (to_join if "j" in what else []):
            if key[0] not in [k[0] for k in jn]:
                jn.append(key)
        carries = []
        if ex:
            carries.append(_exchange_carry([rs["grad"][k] for k in ex]))
        if sc:
            carries.append(_scatter_carry([rs["to_send"][k] for k in sc]))
        if jn:
            carries.append(_join_carry([rs["out"][k[0]] for k in jn], [k[1] for k in jn]))
        if ga:
            carries.append(_allgather_carry([small_own[k] for k in ga]))
        if call is None:
            carried = _run_carries(carries, name) if carries else []
            res = None
        else:
            res, carried = call(carries)
        carried = list(carried)
        if ex:
            for k, ld in zip(ex, carried.pop(0)):
                to_exchange.remove(k)
                rs["landed"][k] = ld
                rs["to_send"][k] = _add_sibling_half(rs["grad"][k], ld, pf, f"rs_add_sibling_{k[0]}_{k[1]}")
                to_scatter.append(k)
        if sc:
            for k, fc in zip(sc, carried.pop(0)):
                to_scatter.remove(k)
                rs["out"][k[0]] = _add_chips(rs["grad"][k], rs["landed"][k], fc, pf, rs["out"].get(k[0]), k[1], n_layer,
                                             f"rs_add_chips_{k[0]}_{k[1]}")
                to_join.append(k)
        if jn:
            for k, o in zip(jn, carried.pop(0)):
                to_join.remove(k)
                rs["out"][k[0]] = o
        if ga:
            for k, o in zip(ga, carried.pop(0)):
                to_gather.remove(k)
                small_all[k] = o
        return res

    def gather_small(key, parts):
        small_own[key] = parts[0] if len(parts) == 1 else jnp.concatenate(parts, axis=0)
        to_gather.append(key)

    def ready(nm, l, g):
        rs["grad"][(nm, l)] = g
        to_exchange.append((nm, l))

    rowblk = lambda t: t.reshape(N_CHIP, t.shape[1] // N_CHIP, t.shape[2])
    small1_prev = None
    for l in reversed(range(n_layer)):
        wl, sv = wfull[l], saved[l]
        dx1, dy2, dup, act, small3 = ride(lambda cr: _mlp_backward(
            dxs, sv["x1"], sv["y2"], sv["up"], mods[l], vecs[l], wl["w_mlp_up"], wl["w_mlp_down"], f"mlp_backward_{l}", cr), "xsjg")
        ready("w_mlp_up", l, _weight_grad(sv["h2"], dup, f"grad_w_mlp_up_{l}", col_blocks=N_CHIP)[0])
        g_down = ride(lambda cr: _weight_grad(act, dy2, f"grad_w_mlp_down_{l}", carries=cr), "x")
        ready("w_mlp_down", l, rowblk(g_down))
        dproj, dab, small2, dwg = ride(lambda cr: _mixer_backward(
            dx1, sv["proj"], sv["conva"], sv["xb2"], sv["hh"], sv["pa"], sv["pb"], sv["y"],
            sv["r"], sv["gi"], sv["a"], sv["mult"], sv["dgel"], mods[l], vecs[l], wgs[l],
            wl["w_a_out"], wl["w_b_out"], wl["w_o"], f"mixer_backward_{l}", cr), "xsjg")
        gather_small(("late", l, "s"), ([small1_prev] if small1_prev is not None else []) + [small2, small3])
        gather_small(("late", l, "w"), [dwg.reshape(2 * bw, d).astype(BF16)])
        g_in = ride(lambda cr: _weight_grad(sv["h"], dproj, f"grad_w_in_{l}", col_blocks=N_CHIP, carries=cr), "xsj")
        ready("w_in", l, g_in)
        g_abo = ride(lambda cr: _weight_grad_stacked(sv["abm"], dab, f"grad_w_abo_{l}", carries=cr), "xg")
        ready("w_abo", l, g_abo)
        dxs, small1_prev = ride(lambda cr: _proj_backward(dproj, dx1, sv["x"], mods[l], vecs[l], wl["w_in"],
                                                          f"proj_backward_{l}", cr), "xsjg")
    grad_x = dxs[None]
    gather_small(("last", 0, "s"), [small1_prev])

    grads, deltas, new_m, new_v = {}, {}, {}, {}

    def adam(nms, copy_grad=False, what=""):
        items = [(weights[nm], grads[nm], mom1[nm], mom2[nm]) for nm in nms]
        res = ride(lambda cr: _adamw(items, "adamw_" + "_".join(nms), copy_grad, cr), what)
        for nm, r in zip(nms, res):
            deltas[nm], new_m[nm], new_v[nm] = r[:3]
            if copy_grad:
                grads[nm] = r[3]

    for nms in (["w_mlp_up", "w_mlp_down"], ["w_in"]):
        for nm in nms:
            grads[nm] = rs["out"][nm].reshape(weights[nm].shape)
        adam(nms, True, "xsjg")
    tail = 0
    while to_exchange or to_scatter or to_join or to_gather:
        ride(None, "xsjg", f"rs_tail_{tail}")
        tail += 1

    sums ={k: _sum_devices(small_all[k], small_own[k], me_arr, f"sum_small_{k[0]}_{k[1]}_{k[2]}") for k in small_own}

    small_full = {}

    def rows_of(l, part):
        if part == 0:
            return (("late", l - 1, "s"), 0) if l >= 1 else (("last", 0, "s"), 0)
        if part == 3:
            return ("late", l, "w"), 0
        base = SUBLANES if l < n_layer - 1 else 0
        return ("late", l, "s"), base + (0, 0, 2 * SUBLANES)[part]

    def summed(l, part, row, n_rows=1):
        key, base = rows_of(l, part)
        return sums[key][base + row:base + row + n_rows]

    def per_device(l, part, row):
        key, base = rows_of(l, part)
        own = small_own[key]
        if key not in small_full:
            small_full[key] = lax.dynamic_update_slice(small_all[key], own, (me_dev * own.shape[0], 0)).reshape(
                (N_DEV,) + own.shape)
        return small_full[key][:, base + row:base + row + 1]

    mod_rows = [(0, SB1_DSH), (0, SB1_DSC), (1, SB2_DGT), (2, SB3_DSH), (2, SB3_DSC), (2, SB3_DGT)]
    dmod_all = jnp.stack([jnp.concatenate([per_device(l, p, r)[:, 0, :] for p, r in mod_rows], axis=1)
                          for l in range(n_layer)], axis=0)
    o1, o2, o3, o4 = 0, SUBLANES, 3 * SUBLANES, 4 * SUBLANES
    small_sum = jnp.stack([jnp.concatenate([summed(l, 0, 0, SUBLANES), summed(l, 1, 0, 2 * SUBLANES),
                                            summed(l, 2, 0, SUBLANES), summed(l, 3, 0, 2 * bw)], axis=0)
                           for l in range(n_layer)], axis=0)
    mod_rows_of = [o1 + SB1_DSH, o1 + SB1_DSC, o2 + SB2_DGT, o3 + SB3_DSH, o3 + SB3_DSC, o3 + SB3_DGT]
    grads["w_mod"] = _mod_backward(c_all.T, lax.dynamic_slice_in_dim(dmod_all, q_me * mq, mq, axis=2), "mod_backward")
    grads["b_mod"] = jnp.concatenate([small_sum[:, k, :] for k in mod_rows_of], axis=1)
    grads["g_pre_mix"] = small_sum[:, o1 + SB1_DG_PRE]
    grads["g_post_mix"] = small_sum[:, o2 + SB2_DG_POST]
    grads["conv_a_w"] = lax.dynamic_slice_in_dim(small_sum[:, o2 + SB2_DWA:o2 + SB2_DWA + ka], q_me * dq, dq, axis=2)
    grads["conv_a_b"] = small_sum[:, o2 + SB2_DBA]
    grads["conv_b_w"] = lax.dynamic_slice_in_dim(small_sum[:, o2 + SB2_DWB:o2 + SB2_DWB + kb], q_me * dq, dq, axis=2)
    grads["conv_b_b"] = small_sum[:, o2 + SB2_DBB]
    grads["lru_lambda"] = small_sum[:, o2 + SB2_DLAM]
    grads["b_gate_r"] = small_sum[:, o2 + SB2_DBR]
    grads["b_gate_i"] = small_sum[:, o2 + SB2_DBI]
    grads["g_pre_mlp"] = small_sum[:, o3 + SB3_DG_PRE]
    grads["g_post_mlp"] = small_sum[:, o3 + SB3_DG_POST]
    dwg_sum = small_sum[:, o4:].reshape(n_layer, n_head, bw, 2 * bw)
    grads["w_gate_r"] = dwg_sum[..., :bw]
    grads["w_gate_i"] = dwg_sum[..., bw:]

    for k, nm in enumerate(groups[1]):
        grads[nm] = rs["out"]["w_abo"][:, k * dq:(k + 1) * dq]

    by_shape = {}
    for nm in names:
        if nm not in deltas:
            by_shape.setdefault(weights[nm].shape, []).append(nm)
    for nms in by_shape.values():
        adam(nms)
    return (loss, grad_x, *[grads[nm] for nm in names], *[deltas[nm] for nm in names],
            *[new_m[nm] for nm in names], *[new_v[nm] for nm in names])
```

```python
import jax
import jax.numpy as jnp
from jax import lax
from jax.experimental import pallas as pl
from jax.experimental.pallas import tpu as pltpu

F32 = jnp.float32
BF16 = jnp.bfloat16
MESH = pl.DeviceIdType.MESH

EPS = 1e-6
LRU_C = 8.0
N_CHIP = 4
N_DEV = 8
ADAM_LR = 0.001
ADAM_B1 = 0.9
ADAM_B2 = 0.999
ADAM_EPS = 1e-08
ADAM_WD = 0.01
ADAM_STEP = 10

VMEM_LIMIT_BYTES = 56 * 1024 * 1024
SUBLANES = 8
LANES = 128
TOKENS_MATMUL_TILE = 512
TOKENS_MIXER_TILE = 256
GELU_K0 = 0.7978845608028654
GELU_K1 = 0.044715

V_G_PRE_MIX, V_G_POST_MIX, V_CONV_A_B, V_CONV_B_B, V_B_GATE_R, V_B_GATE_I, V_LAMBDA, V_G_PRE_MLP, V_G_POST_MLP = range(9)
V_CONV_A_W = 9
V_CONV_B_W = 12
M_SH_M, M_SC_M, M_GT_M, M_SH_F, M_SC_F, M_GT_F = range(6)


def _cparams(n_grid=0):
    sem = ("arbitrary",) * n_grid if n_grid else None
    return pltpu.CompilerParams(dimension_semantics=sem, vmem_limit_bytes=VMEM_LIMIT_BYTES)


def _full(shape):
    return pl.BlockSpec(shape, lambda *_: (0,) * len(shape))


def _dot(a, b):
    return jnp.dot(a, b, preferred_element_type=F32)


def _dot_tb(a, b):
    return lax.dot_general(a, b, (((1,), (1,)), ((), ())), preferred_element_type=F32)


def _dot_ta(a, b):
    return lax.dot_general(a, b, (((0,), (0,)), ((), ())), preferred_element_type=F32)


def _sigmoid(x):
    return 1.0 / (1.0 + jnp.exp(-x))


def _softplus(x):
    return jnp.maximum(x, 0.0) + jnp.log1p(jnp.exp(-jnp.abs(x)))


def _neg_expm1(x):
    series = -x * (1.0 + 0.5 * x * (1.0 + (x / 3.0) * (1.0 + 0.25 * x)))
    return jnp.where(x > -1e-2, series, 1.0 - jnp.exp(x))


def _gelu_and_grad(x):
    x2 = x * x
    s = _sigmoid(x * (2.0 * GELU_K0 + (2.0 * GELU_K0 * GELU_K1) * x2))
    gel = x * s
    return gel, s + gel * (1.0 - s) * (2.0 * GELU_K0 + (6.0 * GELU_K0 * GELU_K1) * x2)


def _rms(x):
    r = lax.rsqrt(jnp.mean(x * x, axis=-1, keepdims=True) + EPS)
    return x * r, r


def _rms_bwd(dxn, xn, r):
    return r * (dxn - xn * jnp.mean(dxn * xn, axis=-1, keepdims=True))


def _colsum(x):
    return jnp.sum(x, axis=0, keepdims=True)


def _rows(t, w):
    return lax.broadcasted_iota(jnp.int32, (t, w), 0)


def _shift_down(x, k, prev8):
    t, w = x.shape
    rolled = pltpu.roll(x, k, 0)
    head = jnp.where(_rows(SUBLANES, w) < k, pltpu.roll(prev8, k, 0), rolled[:SUBLANES])
    return jnp.concatenate([head, rolled[SUBLANES:]], axis=0)


def _shift_up(x, k, next8):
    t, w = x.shape
    rolled = pltpu.roll(x, t - k, 0)
    tail = jnp.where(_rows(SUBLANES, w) >= SUBLANES - k, pltpu.roll(next8, SUBLANES - k, 0), rolled[t - SUBLANES:])
    return jnp.concatenate([rolled[:t - SUBLANES], tail], axis=0)


SCAN_GROUP = 16


def _scan_steps(a, b, group, reverse):
    t, w = a.shape
    pos = _rows(t, w) & (group - 1)
    s = 1
    while s < group:
        keep = (pos < group - s) if reverse else (pos >= s)
        shift = (t - s) if reverse else s
        b = b + a * jnp.where(keep, pltpu.roll(b, shift, 0), 0.0)
        a = a * jnp.where(keep, pltpu.roll(a, shift, 0), 1.0)
        s *= 2
    return b, a


def _scan_two_level(a, b, carry_row, a_buf, b_buf, c_buf, reverse):
    t, w = a.shape
    grp = SCAN_GROUP
    n_grp = t // grp
    h_loc, a_cum = _scan_steps(a, b, grp, reverse)
    end = 0 if reverse else grp - 1
    a_end, h_end = [], []
    for j in range(w // LANES):
        a_buf[j] = a_cum[:, j * LANES:(j + 1) * LANES]
        b_buf[j] = h_loc[:, j * LANES:(j + 1) * LANES]
        a_end.append(a_buf[j, pl.ds(end, n_grp, stride=grp), :])
        h_end.append(b_buf[j, pl.ds(end, n_grp, stride=grp), :])
    a_end = jnp.concatenate(a_end, axis=1)
    h_end = jnp.concatenate(h_end, axis=1)
    h_grp, a_grp = _scan_steps(a_end, h_end, n_grp, reverse)
    h_grp = h_grp + a_grp * carry_row
    rows = _rows(n_grp, w)
    if reverse:
        entering = jnp.where(rows == n_grp - 1, carry_row, pltpu.roll(h_grp, n_grp - 1, 0))
    else:
        entering = jnp.where(rows == 0, carry_row, pltpu.roll(h_grp, 1, 0))
    c_buf[...] = entering
    out = [h_loc[g * grp:(g + 1) * grp] + a_cum[g * grp:(g + 1) * grp] * c_buf[g:g + 1, :] for g in range(n_grp)]
    return jnp.concatenate(out, axis=0)


def _row_tile(rows, cols, itemsize=4, target_bytes=2 * 1024 * 1024):
    if rows * cols * itemsize <= target_bytes or rows % SUBLANES:
        return rows
    t = max(SUBLANES, (target_bytes // (cols * itemsize)) // SUBLANES * SUBLANES)
    while rows % t:
        t -= SUBLANES
    return t


def _place():
    return lax.axis_index("x"), lax.axis_index("y"), lax.axis_index("c")


def _other_chips(x, y):
    chips = [(1 - x, y), (x, 1 - y), (1 - x, 1 - y)]
    return chips, [2 * cx + cy for cx, cy in chips]


def _all_gather_small(block, name):
    m_per, n = block.shape

    def body(x_ref, out_ref, send_sems, recv_sems, local_sem):
        x, y, c = _place()
        me, sibling = (x, y, c), (x, y, 1 - c)
        chips, _ = _other_chips(x, y)

        def rows(px, py, pc):
            return out_ref.at[pl.ds((4 * px + 2 * py + pc) * m_per, m_per), :]

        def copy(k, blk, to, src=None):
            return pltpu.make_async_remote_copy(
                src_ref=rows(*blk) if src is None else src, dst_ref=rows(*blk),
                send_sem=send_sems.at[k], recv_sem=recv_sems.at[k], device_id=to, device_id_type=MESH)

        mine = pltpu.make_async_copy(x_ref, rows(*me), local_sem)
        mine.start()
        first = [copy(0, me, sibling, src=x_ref)]
        first += [copy(1 + j, me, (*chip, c), src=x_ref) for j, chip in enumerate(chips)]
        for cp in first:
            cp.start()
        passed = [copy(4 + j, (*chip, c), sibling) for j, chip in enumerate(chips)]
        for j, chip in enumerate(chips):
            copy(1 + j, (*chip, c), me).wait_recv()
            passed[j].start()
        copy(0, sibling, me).wait_recv()
        for j, chip in enumerate(chips):
            copy(4 + j, (*chip, 1 - c), me).wait_recv()
        for cp in first + passed:
            cp.wait_send()
        mine.wait()

    return pl.pallas_call(
        body, name=name,
        out_shape=jax.ShapeDtypeStruct((N_DEV * m_per, n), block.dtype),
        in_specs=[pl.BlockSpec(memory_space=pltpu.VMEM)],
        out_specs=pl.BlockSpec(memory_space=pltpu.VMEM),
        scratch_shapes=[pltpu.SemaphoreType.DMA((7,)), pltpu.SemaphoreType.DMA((7,)), pltpu.SemaphoreType.DMA],
        compiler_params=pltpu.CompilerParams(vmem_limit_bytes=VMEM_LIMIT_BYTES),
    )(block)


def _cast_place(w, layer, q_arr, name):
    _, r, cols = w.shape
    tr = _row_tile(r, cols)

    def body(q_ref, w_ref, o_ref):
        o_ref[...] = w_ref[...].astype(BF16)

    return pl.pallas_call(
        body, name=name,
        out_shape=jax.ShapeDtypeStruct((N_CHIP, r, cols), BF16),
        grid_spec=pltpu.PrefetchScalarGridSpec(
            num_scalar_prefetch=1, grid=(r // tr,),
            in_specs=[pl.BlockSpec((1, tr, cols), lambda i, q_ref: (layer, i, 0))],
            out_specs=pl.BlockSpec((1, tr, cols), lambda i, q_ref: (q_ref[0], i, 0))),
        compiler_params=_cparams(1),
    )(q_arr, w)


class _Carry:
    def __init__(self, ins, out_shapes, aliases, sem_shapes, start, finish, mid=None, mid_frac=0.85):
        self.ins, self.out_shapes, self.aliases, self.sem_shapes = list(ins), list(out_shapes), dict(aliases), list(sem_shapes)
        self.start, self.mid, self.finish, self.mid_frac = start, mid, finish, mid_frac


def _pcall(body, *, name, grid, in_specs, out_specs, out_shape, args, scratch_shapes=(), carries=(), prefetch=()):
    in_specs, out_specs, out_shape = list(in_specs), list(out_specs), list(out_shape)
    scratch_shapes, args = list(scratch_shapes), list(args)
    n_in, n_out, n_scr, n_pre = len(in_specs), len(out_shape), len(scratch_shapes), len(prefetch)
    steps = 1
    for g in grid:
        steps *= g
    any_spec = pl.BlockSpec(memory_space=pl.ANY)
    aliases = {}
    spans = []
    for cr in carries:
        spans.append((len(args), len(out_shape), len(scratch_shapes)))
        for a, b in cr.aliases.items():
            aliases[n_pre + len(args) + a] = len(out_shape) + b
        args += cr.ins
        in_specs += [any_spec] * len(cr.ins)
        out_shape += cr.out_shapes
        out_specs += [any_spec] * len(cr.out_shapes)
        scratch_shapes += cr.sem_shapes
    n_all_in = len(args)
    n_all_out = len(out_shape)

    def wrapped(*refs):
        pre, refs = refs[:n_pre], refs[n_pre:]
        ins, outs, scr = refs[:n_all_in], refs[n_all_in:n_all_in + n_all_out], refs[n_all_in + n_all_out:]
        parts = [(cr, ins[a:a + len(cr.ins)], outs[b:b + len(cr.out_shapes)], scr[s:s + len(cr.sem_shapes)])
                 for cr, (a, b, s) in zip(carries, spans)]
        lin = 0
        for ax, g in enumerate(grid):
            lin = lin * g + pl.program_id(ax)

        def at(step, fn):
            if steps == 1:
                fn()
            else:
                pl.when(lin == step)(fn)

        def start_all():
            for cr, ci, co, cs in parts:
                cr.start(ci, co, cs)

        def finish_all():
            for cr, ci, co, cs in parts:
                cr.finish(ci, co, cs)

        if parts:
            at(0, start_all)
        body(*pre, *ins[:n_in], *outs[:n_out], *scr[:n_scr])
        for cr, ci, co, cs in parts:
            if cr.mid is not None:
                at(min(steps - 1, int(steps * cr.mid_frac)), lambda cr=cr, ci=ci, co=co, cs=cs: cr.mid(ci, co, cs))
        if parts:
            at(steps - 1, finish_all)

    if n_pre:
        res = pl.pallas_call(
            wrapped, name=name, out_shape=out_shape,
            grid_spec=pltpu.PrefetchScalarGridSpec(num_scalar_prefetch=n_pre, grid=tuple(grid), in_specs=in_specs,
                                                   out_specs=out_specs, scratch_shapes=scratch_shapes),
            input_output_aliases=aliases, compiler_params=_cparams(len(grid)),
        )(*prefetch, *args)
    else:
        res = pl.pallas_call(
            wrapped, name=name, grid=tuple(grid), out_shape=out_shape, in_specs=in_specs, out_specs=out_specs,
            scratch_shapes=scratch_shapes, input_output_aliases=aliases, compiler_params=_cparams(len(grid)),
        )(*args)
    res = list(res)
    return res[:n_out], [res[b:b + len(cr.out_shapes)] for cr, (_, b, _) in zip(carries, spans)]


def _run_carries(carries, name):
    return _pcall(lambda: None, name=name, grid=(), in_specs=[], out_specs=[], out_shape=[], args=[], carries=carries)[1]


CAST_STEPS = 8


def _cast_place_all(shards, q_arr, name, carries=()):
    n = len(shards)

    def body(q_ref, *refs):
        for k in range(n):
            refs[n + k][...] = refs[k][...].astype(BF16)

    def spec_in(k):
        w, layer = shards[k]
        return pl.BlockSpec((1, w.shape[1] // CAST_STEPS, w.shape[2]), lambda i, q_ref: (layer, i, 0))

    def spec_out(k):
        w, _ = shards[k]
        return pl.BlockSpec((1, w.shape[1] // CAST_STEPS, w.shape[2]), lambda i, q_ref: (q_ref[0], i, 0))

    return _pcall(
        body, name=name, grid=(CAST_STEPS,),
        out_shape=[jax.ShapeDtypeStruct((N_CHIP,) + w.shape[1:], BF16) for w, _ in shards],
        in_specs=[spec_in(k) for k in range(n)], out_specs=[spec_out(k) for k in range(n)],
        args=[w for w, _ in shards], carries=carries, prefetch=[q_arr])


def _gather_carry(bufs, mid_frac=0.85):
    n = len(bufs)

    def copies(o_refs, sems):
        send_sems, recv_sems = sems
        x, y, c = _place()
        q = 2 * x + y
        sibling = (x, y, 1 - c)
        chips, qs = _other_chips(x, y)

        def half(w, shard, pc):
            rh = bufs[w].shape[1] // 2
            return o_refs[w].at[shard, pl.ds(pc * rh, rh), :]

        def over_ici(w, j, shard):
            blk = half(w, shard, c)
            return pltpu.make_async_remote_copy(
                src_ref=blk, dst_ref=blk, send_sem=send_sems.at[w, j], recv_sem=recv_sems.at[w, j],
                device_id=(*chips[j], c), device_id_type=MESH)

        def to_sibling(w, j, pc):
            blk = half(w, qs[j], pc)
            return pltpu.make_async_remote_copy(
                src_ref=blk, dst_ref=blk, send_sem=send_sems.at[w, 3 + j], recv_sem=recv_sems.at[w, 3 + j],
                device_id=sibling, device_id_type=MESH)

        return q, c, qs, over_ici, to_sibling

    pairs = [(w, j) for w in range(n) for j in range(3)]

    def start(i_refs, o_refs, sems):
        q, _, _, over_ici, _ = copies(o_refs, sems)
        for w, j in pairs:
            over_ici(w, j, q).start()

    def mid(i_refs, o_refs, sems):
        _, c, qs, over_ici, to_sibling = copies(o_refs, sems)
        for w, j in pairs:
            over_ici(w, j, qs[j]).wait_recv()
            to_sibling(w, j, c).start()

    def finish(i_refs, o_refs, sems):
        q, c, _, over_ici, to_sibling = copies(o_refs, sems)
        for w, j in pairs:
            to_sibling(w, j, 1 - c).wait_recv()
        for w, j in pairs:
            over_ici(w, j, q).wait_send()
            to_sibling(w, j, c).wait_send()

    return _Carry(bufs, [jax.ShapeDtypeStruct(b.shape, b.dtype) for b in bufs], {w: w for w in range(n)},
                  [pltpu.SemaphoreType.DMA((n, 6)), pltpu.SemaphoreType.DMA((n, 6))], start, finish, mid, mid_frac)


def _exchange_carry(grads):
    n = len(grads)

    def copies(g_refs, l_refs, sems):
        send_sems, recv_sems = sems
        x, y, c = _place()
        out = []
        for w in range(n):
            rh = grads[w].shape[1] // 2
            out.append(pltpu.make_async_remote_copy(
                src_ref=g_refs[w].at[:, pl.ds((1 - c) * rh, rh), :], dst_ref=l_refs[w],
                send_sem=send_sems.at[w], recv_sem=recv_sems.at[w], device_id=(x, y, 1 - c), device_id_type=MESH))
        return out

    def start(g_refs, l_refs, sems):
        for cp in copies(g_refs, l_refs, sems):
            cp.start()

    def finish(g_refs, l_refs, sems):
        for cp in copies(g_refs, l_refs, sems):
            cp.wait()

    return _Carry(grads, [jax.ShapeDtypeStruct((N_CHIP, g.shape[1] // 2, g.shape[2]), g.dtype) for g in grads], {},
                  [pltpu.SemaphoreType.DMA((n,)), pltpu.SemaphoreType.DMA((n,))], start, finish)


def _scatter_carry(sums):
    n = len(sums)

    def copies(s_refs, l_refs, sems):
        send_sems, recv_sems = sems
        x, y, c = _place()
        chips, _ = _other_chips(x, y)
        return [pltpu.make_async_remote_copy(
            src_ref=s_refs[w].at[j], dst_ref=l_refs[w].at[j], send_sem=send_sems.at[w, j], recv_sem=recv_sems.at[w, j],
            device_id=(*chips[j], c), device_id_type=MESH) for w in range(n) for j in range(3)]

    def start(s_refs, l_refs, sems):
        for cp in copies(s_refs, l_refs, sems):
            cp.start()

    def finish(s_refs, l_refs, sems):
        for cp in copies(s_refs, l_refs, sems):
            cp.wait()

    return _Carry(sums, [jax.ShapeDtypeStruct(s.shape, s.dtype) for s in sums], {},
                  [pltpu.SemaphoreType.DMA((n, 3)), pltpu.SemaphoreType.DMA((n, 3))], start, finish)


def _join_carry(outs, layers):
    n = len(outs)

    def copy(o_refs, sems, w, mine):
        send_sems, recv_sems = sems
        x, y, c = _place()
        r = outs[w].shape[1]
        rows = o_refs[w].at[layers[w], pl.ds((c if mine else 1 - c) * (r // 2), r // 2), :]
        return pltpu.make_async_remote_copy(
            src_ref=rows, dst_ref=rows, send_sem=send_sems.at[w], recv_sem=recv_sems.at[w],
            device_id=(x, y, 1 - c), device_id_type=MESH)

    def start(i_refs, o_refs, sems):
        for w in range(n):
            copy(o_refs, sems, w, True).start()

    def finish(i_refs, o_refs, sems):
        for w in range(n):
            copy(o_refs, sems, w, True).wait_send()
        for w in range(n):
            copy(o_refs, sems, w, False).wait_recv()

    return _Carry(outs, [jax.ShapeDtypeStruct(o.shape, o.dtype) for o in outs], {w: w for w in range(n)},
                  [pltpu.SemaphoreType.DMA((n,)), pltpu.SemaphoreType.DMA((n,))], start, finish)


PF_C, PF_Q, PF_QS = 0, 1, 2


def _add_sibling_half(g, landed, pf, name):
    _, r, cols = g.shape
    rh = r // 2
    tr = _row_tile(rh, cols)
    nr = rh // tr

    def body(pf_ref, g_ref, l_ref, o_ref):
        o_ref[...] = (g_ref[...] + l_ref[...]).astype(BF16)

    return pl.pallas_call(
        body, name=name,
        out_shape=jax.ShapeDtypeStruct((3, rh, cols), BF16),
        grid_spec=pltpu.PrefetchScalarGridSpec(
            num_scalar_prefetch=1, grid=(3, nr),
            in_specs=[pl.BlockSpec((1, tr, cols), lambda j, i, pf_ref: (pf_ref[PF_QS + j], pf_ref[PF_C] * nr + i, 0)),
                      pl.BlockSpec((1, tr, cols), lambda j, i, pf_ref: (pf_ref[PF_QS + j], i, 0))],
            out_specs=pl.BlockSpec((1, tr, cols), lambda j, i, pf_ref: (j, i, 0))),
        compiler_params=_cparams(2),
    )(pf, g, landed)


def _add_chips(g, landed, from_chips, pf, prev, layer, n_layer, name):
    _, r, cols = g.shape
    rh = r // 2
    tr = _row_tile(rh, cols)
    nr = rh // tr

    def body(pf_ref, g_ref, l_ref, f_ref, *rest):
        o_ref = rest[-1]
        acc = g_ref[0] + l_ref[0]
        for j in range(3):
            acc = acc + f_ref[j].astype(F32)
        o_ref[0] = acc

    in_specs = [pl.BlockSpec((1, tr, cols), lambda i, pf_ref: (pf_ref[PF_Q], pf_ref[PF_C] * nr + i, 0)),
                pl.BlockSpec((1, tr, cols), lambda i, pf_ref: (pf_ref[PF_Q], i, 0)),
                pl.BlockSpec((3, tr, cols), lambda i, pf_ref: (0, i, 0))]
    args = [pf, g, landed, from_chips]
    aliases = {}
    if prev is not None:
        in_specs.append(pl.BlockSpec(memory_space=pl.ANY))
        args.append(prev)
        aliases = {4: 0}
    return pl.pallas_call(
        body, name=name,
        out_shape=jax.ShapeDtypeStruct((n_layer, r, cols), F32),
        grid_spec=pltpu.PrefetchScalarGridSpec(
            num_scalar_prefetch=1, grid=(nr,), in_specs=in_specs,
            out_specs=pl.BlockSpec((1, tr, cols), lambda i, pf_ref: (layer, pf_ref[PF_C] * nr + i, 0))),
        input_output_aliases=aliases,
        compiler_params=_cparams(1),
    )(*args)


def _allgather_carry(blocks):
    n = len(blocks)

    def copies(b_refs, o_refs, sems):
        send_sems, recv_sems = sems
        x, y, c = _place()
        chips, _ = _other_chips(x, y)

        def place(w, px, py, pc):
            m = blocks[w].shape[0]
            return o_refs[w].at[pl.ds((4 * px + 2 * py + pc) * m, m), :]

        def own_to(w, k, to):
            dst = place(w, x, y, c)
            return pltpu.make_async_remote_copy(src_ref=b_refs[w], dst_ref=dst, send_sem=send_sems.at[w, k],
                                                recv_sem=recv_sems.at[w, k], device_id=to, device_id_type=MESH)

        def landed_from(w, k, px, py, pc):
            blk = place(w, px, py, pc)
            return pltpu.make_async_remote_copy(src_ref=blk, dst_ref=blk, send_sem=send_sems.at[w, k],
                                                recv_sem=recv_sems.at[w, k], device_id=(x, y, 1 - c), device_id_type=MESH)

        return x, y, c, chips, own_to, landed_from

    def start(b_refs, o_refs, sems):
        x, y, c, chips, own_to, _ = copies(b_refs, o_refs, sems)
        for w in range(n):
            own_to(w, 0, (x, y, 1 - c)).start()
            for j, chip in enumerate(chips):
                own_to(w, 1 + j, (*chip, c)).start()

    def mid(b_refs, o_refs, sems):
        x, y, c, chips, _, landed_from = copies(b_refs, o_refs, sems)
        for w in range(n):
            for j, chip in enumerate(chips):
                landed_from(w, 1 + j, *chip, c).wait_recv()
                landed_from(w, 4 + j, *chip, c).start()

    def finish(b_refs, o_refs, sems):
        x, y, c, chips, own_to, landed_from = copies(b_refs, o_refs, sems)
        for w in range(n):
            landed_from(w, 0, x, y, 1 - c).wait_recv()
            for j, chip in enumerate(chips):
                landed_from(w, 4 + j, *chip, 1 - c).wait_recv()
            own_to(w, 0, (x, y, 1 - c)).wait_send()
            for j, chip in enumerate(chips):
                own_to(w, 1 + j, (*chip, c)).wait_send()
                landed_from(w, 4 + j, *chip, c).wait_send()

    return _Carry(blocks, [jax.ShapeDtypeStruct((N_DEV * b.shape[0], b.shape[1]), b.dtype) for b in blocks], {},
                  [pltpu.SemaphoreType.DMA((n, 7)), pltpu.SemaphoreType.DMA((n, 7))], start, finish, mid)


def _sum_devices(gathered, own, me_arr, name):
    m, n = own.shape
    tr = _row_tile(m, n, itemsize=own.dtype.itemsize, target_bytes=256 * 1024)
    nr = m // tr

    def body(me_ref, *refs):
        g_refs, own_ref, o_ref = refs[:N_DEV], refs[N_DEV], refs[N_DEV + 1]
        me = me_ref[0]
        acc = None
        for dev in range(N_DEV):
            term = jnp.where(me == dev, own_ref[...], g_refs[dev][...]).astype(F32)
            acc = term if acc is None else acc + term
        o_ref[...] = acc

    def dev_rows(dev):
        return pl.BlockSpec((tr, n), lambda i, me_ref: (dev * nr + i, 0))

    return pl.pallas_call(
        body, name=name,
        out_shape=jax.ShapeDtypeStruct((m, n), F32),
        grid_spec=pltpu.PrefetchScalarGridSpec(
            num_scalar_prefetch=1, grid=(nr,),
            in_specs=[dev_rows(dev) for dev in range(N_DEV)] + [pl.BlockSpec((tr, n), lambda i, me_ref: (i, 0))],
            out_specs=pl.BlockSpec((tr, n), lambda i, me_ref: (i, 0))),
        compiler_params=_cparams(1),
    )(me_arr, *([gathered] * N_DEV), own)


def _mod_forward(c_all, w_mod, b_mod_shard, name):
    n_layer, d, mq = w_mod.shape

    def body(c_ref, w_ref, b_ref, o_ref):
        cv = c_ref[...]
        o_ref[...] = _dot(cv * _sigmoid(cv), w_ref[0]) + b_ref[0]

    return pl.pallas_call(
        body, name=name, grid=(n_layer,),
        out_shape=jax.ShapeDtypeStruct((n_layer * N_DEV, mq), F32),
        in_specs=[_full((N_DEV, d)), pl.BlockSpec((1, d, mq), lambda l: (l, 0, 0)),
                  pl.BlockSpec((1, 1, mq), lambda l: (l, 0, 0))],
        out_specs=pl.BlockSpec((N_DEV, mq), lambda l: (l, 0)),
        compiler_params=_cparams(1),
    )(c_all, w_mod, b_mod_shard.reshape(n_layer, 1, mq))


def _mod_backward(c_all_t, dmod_shard, name):
    n_layer, _, mq = dmod_shard.shape
    d = c_all_t.shape[0]

    def body(c_ref, dm_ref, o_ref):
        cv = c_ref[...]
        o_ref[0] = _dot(cv * _sigmoid(cv), dm_ref[0])

    return pl.pallas_call(
        body, name=name, grid=(n_layer,),
        out_shape=jax.ShapeDtypeStruct((n_layer, d, mq), F32),
        in_specs=[_full((d, N_DEV)), pl.BlockSpec((1, N_DEV, mq), lambda l: (l, 0, 0))],
        out_specs=pl.BlockSpec((1, d, mq), lambda l: (l, 0, 0)),
        compiler_params=_cparams(1),
    )(c_all_t, dmod_shard)


def _norm_proj(x, mod, vec, w_in, name, carries=()):
    s, d = x.shape
    nq = w_in.shape[2]
    ts = min(TOKENS_MATMUL_TILE, s)

    def body(x_ref, mod_ref, vec_ref, w_ref, h_ref, p_ref, dgel_ref):
        xn, _ = _rms(x_ref[...])
        gm = vec_ref[V_G_PRE_MIX:V_G_PRE_MIX + 1, :] * (1.0 + mod_ref[M_SC_M:M_SC_M + 1, :])
        h = (xn * gm + mod_ref[M_SH_M:M_SH_M + 1, :]).astype(BF16)
        h_ref[...] = h
        for qb in range(N_CHIP):
            pq = _dot(h, w_ref[qb])
            for k in range(N_CHIP * nq // d):
                lo, hi = max(qb * nq, k * d), min((qb + 1) * nq, (k + 1) * d)
                if lo >= hi:
                    continue
                piece = pq[:, lo - qb * nq:hi - qb * nq]
                if k == 4:
                    piece, dgel = _gelu_and_grad(piece)
                    dgel_ref[:, lo - 4 * d:hi - 4 * d] = dgel.astype(BF16)
                elif k >= 5:
                    piece = _sigmoid(piece)
                p_ref[:, lo:hi] = piece.astype(BF16)

    tile = pl.BlockSpec((ts, d), lambda i: (i, 0))
    return _pcall(
        body, name=name, grid=(s // ts,),
        out_shape=[jax.ShapeDtypeStruct((s, d), BF16), jax.ShapeDtypeStruct((s, N_CHIP * nq), BF16),
                   jax.ShapeDtypeStruct((s, d), BF16)],
        in_specs=[tile, _full(mod.shape), _full(vec.shape), _full(w_in.shape)],
        out_specs=[tile, pl.BlockSpec((ts, N_CHIP * nq), lambda i: (i, 0)), tile],
        args=[x, mod, vec, w_in], carries=carries)


def _gate_pre(xb2_b, wg_ref, n_head, bw):
    zr, zi = [], []
    for hd in range(n_head):
        z = _dot(xb2_b[:, hd * bw:(hd + 1) * bw], wg_ref[hd])
        zr.append(z[:, :bw])
        zi.append(z[:, bw:])
    return jnp.concatenate(zr, axis=1), jnp.concatenate(zi, axis=1)


def _lru_coeffs(xb2, wg_ref, vec_ref, n_head, bw):
    zr, zi = _gate_pre(xb2.astype(BF16), wg_ref, n_head, bw)
    r = _sigmoid(zr + vec_ref[V_B_GATE_R:V_B_GATE_R + 1, :])
    gi = _sigmoid(zi + vec_ref[V_B_GATE_I:V_B_GATE_I + 1, :])
    sp = _softplus(-vec_ref[V_LAMBDA:V_LAMBDA + 1, :])
    log_a = (-LRU_C) * r * sp
    a = jnp.exp(log_a)
    mult = jnp.sqrt(_neg_expm1(2.0 * log_a))
    return r, gi, sp, a, mult


def _mixer_forward(x, proj, mod, vec, wg, w_a_out, w_b_out, w_o, name, carries=()):
    s, d = x.shape
    n_head, bw, _ = wg.shape
    ts = min(TOKENS_MIXER_TILE, s)

    def body(x_ref, p_ref, mod_ref, vec_ref, wg_ref, wa_ref, wb_ref, wo_ref,
             x1_ref, conva_ref, xb2_ref, hh_ref, abm_ref, pa_ref, pb_ref, y_ref,
             r_ref, gi_ref, a_ref, mult_ref,
             cv_tail, xb_tail, h_last, a_buf, b_buf, c_buf):
        i = pl.program_id(0)

        @pl.when(i == 0)
        def _():
            cv_tail[...] = jnp.zeros_like(cv_tail)
            xb_tail[...] = jnp.zeros_like(xb_tail)
            h_last[...] = jnp.zeros_like(h_last)

        def seg(k):
            return p_ref[:, k * d:(k + 1) * d].astype(F32)

        def vrow(k):
            return vec_ref[k:k + 1, :]

        b_a, c_a, v_a, x_b, gel, sa, sb = (seg(k) for k in range(7))
        cv = c_a * v_a
        prev_cv = cv_tail[...]
        conv_a = (vrow(V_CONV_A_B) + vrow(V_CONV_A_W) * _shift_down(cv, 2, prev_cv)
                  + vrow(V_CONV_A_W + 1) * _shift_down(cv, 1, prev_cv) + vrow(V_CONV_A_W + 2) * cv)
        cv_tail[...] = cv[ts - SUBLANES:]
        y_a = b_a * conv_a
        prev_xb = xb_tail[...]
        xb2 = (vrow(V_CONV_B_B) + vrow(V_CONV_B_W) * _shift_down(x_b, 3, prev_xb)
               + vrow(V_CONV_B_W + 1) * _shift_down(x_b, 2, prev_xb)
               + vrow(V_CONV_B_W + 2) * _shift_down(x_b, 1, prev_xb) + vrow(V_CONV_B_W + 3) * x_b)
        xb_tail[...] = x_b[ts - SUBLANES:]
        r, gi, _, a, mult = _lru_coeffs(xb2, wg_ref, vec_ref, n_head, bw)
        r_ref[...] = r
        gi_ref[...] = gi
        a_ref[...] = a
        mult_ref[...] = mult
        hh = _scan_two_level(a, mult * gi * xb2, h_last[SUBLANES - 1:SUBLANES, :], a_buf, b_buf, c_buf, reverse=False)
        h_last[...] = hh[ts - SUBLANES:]
        y_b = hh * gel
        ya_b, yb_b = y_a.astype(BF16), y_b.astype(BF16)
        pa = _dot(ya_b, wa_ref[...])
        pb = _dot(yb_b, wb_ref[...])
        m = (sa * pa + sb * pb).astype(BF16)
        y = _dot(m, wo_ref[...])
        yn, _ = _rms(y)
        gg = mod_ref[M_GT_M:M_GT_M + 1, :] * vrow(V_G_POST_MIX)
        x1_ref[...] = x_ref[...] + yn * gg
        conva_ref[...] = conv_a.astype(BF16)
        xb2_ref[...] = xb2
        hh_ref[...] = hh
        abm_ref[0] = ya_b
        abm_ref[1] = yb_b
        abm_ref[2] = m
        pa_ref[...] = pa.astype(BF16)
        pb_ref[...] = pb.astype(BF16)
        y_ref[...] = y.astype(BF16)

    tile = pl.BlockSpec((ts, d), lambda i: (i, 0))
    tile3 = pl.BlockSpec((3, ts, d), lambda i: (0, i, 0))
    sd = lambda dt: jax.ShapeDtypeStruct((s, d), dt)
    return _pcall(
        body, name=name, grid=(s // ts,),
        out_shape=[sd(F32), sd(BF16), sd(F32), sd(F32), jax.ShapeDtypeStruct((3, s, d), BF16), sd(BF16), sd(BF16), sd(BF16),
                   sd(F32), sd(F32), sd(F32), sd(F32)],
        in_specs=[tile, pl.BlockSpec((ts, 7 * d), lambda i: (i, 0)), _full(mod.shape), _full(vec.shape),
                  _full(wg.shape), _full(w_a_out.shape), _full(w_b_out.shape), _full(w_o.shape)],
        out_specs=[tile] * 4 + [tile3] + [tile] * 7,
        scratch_shapes=[pltpu.VMEM((SUBLANES, d), F32)] * 3 + [pltpu.VMEM((d // LANES, ts, LANES), F32)] * 2
                       + [pltpu.VMEM((ts // SCAN_GROUP, d), F32)],
        args=[x, proj, mod, vec, wg, w_a_out, w_b_out, w_o], carries=carries)


def _mlp_forward(x1, mod, vec, w_up, w_down, name, carries=(), target=None):
    s, d = x1.shape
    fq = w_up.shape[2]
    ts = min(TOKENS_MATMUL_TILE, s)

    def body(x_ref, *refs):
        if target is None:
            mod_ref, vec_ref, wu_ref, wd_ref, x2_ref, h2_ref, up_ref, y2_ref = refs
        else:
            t_ref, mod_ref, vec_ref, wu_ref, wd_ref, x2_ref, h2_ref, up_ref, y2_ref, loss_ref = refs
        x = x_ref[...]
        xn, _ = _rms(x)
        gm = vec_ref[V_G_PRE_MLP:V_G_PRE_MLP + 1, :] * (1.0 + mod_ref[M_SC_F:M_SC_F + 1, :])
        h2 = (xn * gm + mod_ref[M_SH_F:M_SH_F + 1, :]).astype(BF16)
        h2_ref[...] = h2
        y2 = jnp.zeros((ts, d), F32)
        for qb in range(N_CHIP):
            up = _dot(h2, wu_ref[qb])
            up_ref[:, qb * fq:(qb + 1) * fq] = up.astype(BF16)
            ru = jnp.maximum(up, 0.0)
            y2 = y2 + _dot((ru * ru).astype(BF16), wd_ref[qb])
        y2_ref[...] = y2.astype(BF16)
        yn, _ = _rms(y2)
        gg = mod_ref[M_GT_F:M_GT_F + 1, :] * vec_ref[V_G_POST_MLP:V_G_POST_MLP + 1, :]
        x2 = x + yn * gg
        if target is None:
            x2_ref[...] = x2
        else:
            @pl.when(pl.program_id(0) == 0)
            def _():
                loss_ref[...] = jnp.zeros_like(loss_ref)

            err = x2 - t_ref[...]
            x2_ref[...] = err * (1.0 / d)
            loss_ref[...] += jnp.sum(jnp.sum(err * err, axis=1, keepdims=True), axis=0, keepdims=True) * (0.5 / d)

    tile = pl.BlockSpec((ts, d), lambda i: (i, 0))
    last = target is not None
    return _pcall(
        body, name=name, grid=(s // ts,),
        out_shape=[jax.ShapeDtypeStruct((s, d), F32), jax.ShapeDtypeStruct((s, d), BF16),
                   jax.ShapeDtypeStruct((s, N_CHIP * fq), BF16), jax.ShapeDtypeStruct((s, d), BF16)]
                  + ([jax.ShapeDtypeStruct((SUBLANES, LANES), F32)] if last else []),
        in_specs=[tile] + ([tile] if last else []) + [_full(mod.shape), _full(vec.shape), _full(w_up.shape), _full(w_down.shape)],
        out_specs=[tile, tile, pl.BlockSpec((ts, N_CHIP * fq), lambda i: (i, 0)), tile]
                 + ([_full((SUBLANES, LANES))] if last else []),
        args=[x1] + ([target] if last else []) + [mod, vec, w_up, w_down], carries=carries)


SB3_DSH, SB3_DSC, SB3_DGT, SB3_DG_PRE, SB3_DG_POST = range(5)
SB1_DSH, SB1_DSC, SB1_DG_PRE = range(3)
(SB2_DGT, SB2_DG_POST, SB2_DWA, SB2_DBA, SB2_DWB, SB2_DBB, SB2_DLAM, SB2_DBR, SB2_DBI) = (0, 1, 2, 5, 6, 10, 11, 12, 13)


def _mlp_backward(dx2, x1, y2, up, mod, vec, w_up, w_down, name, carries=()):
    s, d = dx2.shape
    fq = w_up.shape[2]
    ts = min(TOKENS_MIXER_TILE, s)
    n_t = s // ts

    def body(dx2_ref, x_ref, y2_ref, up_ref, mod_ref, vec_ref, wu_ref, wd_ref,
             dx1_ref, dy2_ref, dup_ref, act_ref, small_ref):
        i = pl.program_id(0)

        @pl.when(i == 0)
        def _():
            small_ref[...] = jnp.zeros_like(small_ref)

        dout = dx2_ref[...]
        y2n, ry = _rms(y2_ref[...].astype(F32))
        g_post = vec_ref[V_G_POST_MLP:V_G_POST_MLP + 1, :]
        gt = mod_ref[M_GT_F:M_GT_F + 1, :]
        dgg = _colsum(dout * y2n)
        dy2 = _rms_bwd(dout * (gt * g_post), y2n, ry).astype(BF16)
        dy2_ref[...] = dy2
        dh2 = jnp.zeros((ts, d), F32)
        for qb in range(N_CHIP):
            cols = slice(qb * fq, (qb + 1) * fq)
            dact = _dot_tb(dy2, wd_ref[qb])
            ru = jnp.maximum(up_ref[:, cols].astype(F32), 0.0)
            dup = (dact * (2.0 * ru)).astype(BF16)
            dup_ref[:, cols] = dup
            act_ref[:, cols] = (ru * ru).astype(BF16)
            dh2 = dh2 + _dot_tb(dup, wu_ref[qb])
        xn, r = _rms(x_ref[...])
        g_pre = vec_ref[V_G_PRE_MLP:V_G_PRE_MLP + 1, :]
        sc1 = 1.0 + mod_ref[M_SC_F:M_SC_F + 1, :]
        dsh = _colsum(dh2)
        dgm = _colsum(dh2 * xn)
        dx1_ref[...] = dout + _rms_bwd(dh2 * (g_pre * sc1), xn, r)
        small_ref[SB3_DSH:SB3_DSH + 1, :] += dsh
        small_ref[SB3_DSC:SB3_DSC + 1, :] += dgm
        small_ref[SB3_DGT:SB3_DGT + 1, :] += dgg

        @pl.when(i == n_t - 1)
        def _():
            dgm_t = small_ref[SB3_DSC:SB3_DSC + 1, :]
            dgg_t = small_ref[SB3_DGT:SB3_DGT + 1, :]
            small_ref[SB3_DSC:SB3_DSC + 1, :] = dgm_t * g_pre
            small_ref[SB3_DG_PRE:SB3_DG_PRE + 1, :] = dgm_t * sc1
            small_ref[SB3_DGT:SB3_DGT + 1, :] = dgg_t * g_post
            small_ref[SB3_DG_POST:SB3_DG_POST + 1, :] = dgg_t * gt

    tile = pl.BlockSpec((ts, d), lambda i: (i, 0))
    wide = pl.BlockSpec((ts, N_CHIP * fq), lambda i: (i, 0))
    return _pcall(
        body, name=name, grid=(n_t,),
        out_shape=[jax.ShapeDtypeStruct((s, d), F32), jax.ShapeDtypeStruct((s, d), BF16),
                   jax.ShapeDtypeStruct((s, N_CHIP * fq), BF16), jax.ShapeDtypeStruct((s, N_CHIP * fq), BF16),
                   jax.ShapeDtypeStruct((SUBLANES, d), F32)],
        in_specs=[tile, tile, tile, wide, _full(mod.shape), _full(vec.shape), _full(w_up.shape), _full(w_down.shape)],
        out_specs=[tile, tile, wide, wide, _full((SUBLANES, d))],
        args=[dx2, x1, y2, up, mod, vec, w_up, w_down], carries=carries)


def _mixer_backward(dx1, proj, conva, xb2s, hhs, pas, pbs, ys, rs_, gis, as_, mults, dgels, mod, vec, wg, w_a_out, w_b_out,
                    w_o, name, carries=()):
    s, d = dx1.shape
    n_head, bw, _ = wg.shape
    ts = min(TOKENS_MIXER_TILE, s)
    n_t = s // ts

    def body(dx1_ref, p_ref, conva_ref, xb2_ref, hh_ref, pa_ref, pb_ref, y_ref, r_ref, gi_ref, a_ref, mult_ref, dgel_ref,
             mod_ref, vec_ref, wg_ref, wa_ref, wb_ref, wo_ref,
             dp_ref, dab_ref, small_ref, dwg_ref,
             dconv_head, dxb2_head, a_head, g_head, a_buf, b_buf, c_buf):
        i = pl.program_id(0)

        @pl.when(i == 0)
        def _():
            small_ref[...] = jnp.zeros_like(small_ref)
            dwg_ref[...] = jnp.zeros_like(dwg_ref)
            dconv_head[...] = jnp.zeros_like(dconv_head)
            dxb2_head[...] = jnp.zeros_like(dxb2_head)
            a_head[...] = jnp.zeros_like(a_head)
            g_head[...] = jnp.zeros_like(g_head)

        def seg(k):
            return p_ref[:, k * d:(k + 1) * d].astype(F32)

        def vrow(k):
            return vec_ref[k:k + 1, :]

        def acc(row, val):
            small_ref[row:row + 1, :] += val

        dout = dx1_ref[...]
        yn, ry = _rms(y_ref[...].astype(F32))
        g_post = vrow(V_G_POST_MIX)
        gt = mod_ref[M_GT_M:M_GT_M + 1, :]
        acc(SB2_DGT, _colsum(dout * yn))
        dy = _rms_bwd(dout * (gt * g_post), yn, ry).astype(BF16)
        dab_ref[2] = dy
        dm = _dot_tb(dy, wo_ref[...])
        sa, sb = seg(5), seg(6)
        dpa = (dm * sa).astype(BF16)
        dpb = (dm * sb).astype(BF16)
        dab_ref[0] = dpa
        dab_ref[1] = dpb
        du_a = dm * pa_ref[...].astype(F32) * (sa * (1.0 - sa))
        du_b = dm * pb_ref[...].astype(F32) * (sb * (1.0 - sb))
        dp_ref[:, 5 * d:6 * d] = du_a.astype(BF16)
        dp_ref[:, 6 * d:7 * d] = du_b.astype(BF16)
        dy_a = _dot_tb(dpa, wa_ref[...])
        dy_b = _dot_tb(dpb, wb_ref[...])

        b_a, c_a, v_a = seg(0), seg(1), seg(2)
        dp_ref[:, 0:d] = (dy_a * conva_ref[...].astype(F32)).astype(BF16)
        dconv = dy_a * b_a
        nxt = dconv_head[...]
        d1 = _shift_up(dconv, 1, nxt)
        d2 = _shift_up(dconv, 2, nxt)
        dconv_head[...] = dconv[:SUBLANES]
        dcv = vrow(V_CONV_A_W + 2) * dconv + vrow(V_CONV_A_W + 1) * d1 + vrow(V_CONV_A_W) * d2
        cv = c_a * v_a
        acc(SB2_DWA + 2, _colsum(cv * dconv))
        acc(SB2_DWA + 1, _colsum(cv * d1))
        acc(SB2_DWA, _colsum(cv * d2))
        acc(SB2_DBA, _colsum(dconv))
        dp_ref[:, d:2 * d] = (dcv * v_a).astype(BF16)
        dp_ref[:, 2 * d:3 * d] = (dcv * c_a).astype(BF16)

        x_b, gel = seg(3), seg(4)
        hh = hh_ref[...]
        dp_ref[:, 4 * d:5 * d] = (dy_b * hh * dgel_ref[...].astype(F32)).astype(BF16)
        dhh = dy_b * gel
        xb2 = xb2_ref[...]
        r, gi, a, mult = r_ref[...], gi_ref[...], a_ref[...], mult_ref[...]
        sp = _softplus(-vrow(V_LAMBDA))
        a_next = _shift_up(a, 1, a_head[...])
        g = _scan_two_level(a_next, dhh, g_head[0:1, :], a_buf, b_buf, c_buf, reverse=True)
        a_head[...] = a[:SUBLANES]
        g_head[...] = g[:SUBLANES]
        gix = gi * xb2
        gm = g * mult
        dlog_a = g * (hh - mult * gix) - (g * gix) * (a * a / mult)
        dgi = gm * xb2
        dxb2 = gm * gi
        acc(SB2_DLAM, _colsum(dlog_a * r))
        dzr = dlog_a * ((-LRU_C) * sp) * (r * (1.0 - r))
        dzi = dgi * (gi * (1.0 - gi))
        acc(SB2_DBR, _colsum(dzr))
        acc(SB2_DBI, _colsum(dzi))
        xb2_b = xb2.astype(BF16)
        back = []
        for hd in range(n_head):
            cols = slice(hd * bw, (hd + 1) * bw)
            dz = jnp.concatenate([dzr[:, cols], dzi[:, cols]], axis=1).astype(BF16)
            back.append(_dot_tb(dz, wg_ref[hd]))
            dwg_ref[hd] += _dot_ta(xb2_b[:, cols], dz)
        dxb2 = dxb2 + jnp.concatenate(back, axis=1)
        nxt = dxb2_head[...]
        e1 = _shift_up(dxb2, 1, nxt)
        e2 = _shift_up(dxb2, 2, nxt)
        e3 = _shift_up(dxb2, 3, nxt)
        dxb2_head[...] = dxb2[:SUBLANES]
        dp_ref[:, 3 * d:4 * d] = (vrow(V_CONV_B_W + 3) * dxb2 + vrow(V_CONV_B_W + 2) * e1
                                  + vrow(V_CONV_B_W + 1) * e2 + vrow(V_CONV_B_W) * e3).astype(BF16)
        acc(SB2_DWB + 3, _colsum(x_b * dxb2))
        acc(SB2_DWB + 2, _colsum(x_b * e1))
        acc(SB2_DWB + 1, _colsum(x_b * e2))
        acc(SB2_DWB, _colsum(x_b * e3))
        acc(SB2_DBB, _colsum(dxb2))

        @pl.when(i == n_t - 1)
        def _():
            dgg_t = small_ref[SB2_DGT:SB2_DGT + 1, :]
            small_ref[SB2_DGT:SB2_DGT + 1, :] = dgg_t * g_post
            small_ref[SB2_DG_POST:SB2_DG_POST + 1, :] = dgg_t * gt
            lam = vrow(V_LAMBDA)
            small_ref[SB2_DLAM:SB2_DLAM + 1, :] = small_ref[SB2_DLAM:SB2_DLAM + 1, :] * (LRU_C * _sigmoid(-lam))

    rev = lambda i: (n_t - 1 - i, 0)
    tile = pl.BlockSpec((ts, d), rev)
    wide = pl.BlockSpec((ts, 7 * d), rev)
    sd = lambda dt: jax.ShapeDtypeStruct((s, d), dt)
    return _pcall(
        body, name=name, grid=(n_t,),
        out_shape=[jax.ShapeDtypeStruct((s, 7 * d), BF16), jax.ShapeDtypeStruct((3, s, d), BF16),
                   jax.ShapeDtypeStruct((2 * SUBLANES, d), F32), jax.ShapeDtypeStruct(wg.shape, F32)],
        in_specs=[tile, wide] + [tile] * 11 + [_full(mod.shape), _full(vec.shape),
                  _full(wg.shape), _full(w_a_out.shape), _full(w_b_out.shape), _full(w_o.shape)],
        out_specs=[wide, pl.BlockSpec((3, ts, d), lambda i: (0, n_t - 1 - i, 0)), _full((2 * SUBLANES, d)), _full(wg.shape)],
        scratch_shapes=[pltpu.VMEM((SUBLANES, d), F32)] * 4 + [pltpu.VMEM((d // LANES, ts, LANES), F32)] * 2
                       + [pltpu.VMEM((ts // SCAN_GROUP, d), F32)],
        args=[dx1, proj, conva, xb2s, hhs, pas, pbs, ys, rs_, gis, as_, mults, dgels, mod, vec, wg, w_a_out, w_b_out, w_o],
        carries=carries)


def _proj_backward(dproj, dx1, x, mod, vec, w_in, name, carries=()):
    s, d = x.shape
    nq = w_in.shape[2]
    ts = min(TOKENS_MATMUL_TILE, s)
    n_t = s // ts

    def body(dp_ref, dx1_ref, x_ref, mod_ref, vec_ref, w_ref, dx_ref, small_ref):
        i = pl.program_id(0)

        @pl.when(i == 0)
        def _():
            small_ref[...] = jnp.zeros_like(small_ref)

        dh = jnp.zeros((ts, d), F32)
        for qb in range(N_CHIP):
            dh = dh + _dot_tb(dp_ref[:, qb * nq:(qb + 1) * nq], w_ref[qb])
        xn, r = _rms(x_ref[...])
        g_pre = vec_ref[V_G_PRE_MIX:V_G_PRE_MIX + 1, :]
        sc1 = 1.0 + mod_ref[M_SC_M:M_SC_M + 1, :]
        dx_ref[...] = dx1_ref[...] + _rms_bwd(dh * (g_pre * sc1), xn, r)
        small_ref[SB1_DSH:SB1_DSH + 1, :] += _colsum(dh)
        small_ref[SB1_DSC:SB1_DSC + 1, :] += _colsum(dh * xn)

        @pl.when(i == n_t - 1)
        def _():
            dgm_t = small_ref[SB1_DSC:SB1_DSC + 1, :]
            small_ref[SB1_DSC:SB1_DSC + 1, :] = dgm_t * g_pre
            small_ref[SB1_DG_PRE:SB1_DG_PRE + 1, :] = dgm_t * sc1

    tile = pl.BlockSpec((ts, d), lambda i: (i, 0))
    return _pcall(
        body, name=name, grid=(n_t,),
        out_shape=[jax.ShapeDtypeStruct((s, d), F32), jax.ShapeDtypeStruct((SUBLANES, d), F32)],
        in_specs=[pl.BlockSpec((ts, N_CHIP * nq), lambda i: (i, 0)), tile, tile, _full(mod.shape), _full(vec.shape),
                  _full(w_in.shape)],
        out_specs=[tile, _full((SUBLANES, d))],
        args=[dproj, dx1, x, mod, vec, w_in], carries=carries)


def _weight_grad(a, b, name, col_blocks=1, tk=512, carries=()):
    s, k = a.shape
    n = b.shape[1]
    tn = n // col_blocks
    tk = min(tk, k)

    def body(a_ref, b_ref, o_ref):
        o_ref[0] = _dot_ta(a_ref[...], b_ref[...])

    (out,), carried = _pcall(
        body, name=name, grid=(col_blocks, k // tk),
        out_shape=[jax.ShapeDtypeStruct((col_blocks, k, tn), F32)],
        in_specs=[pl.BlockSpec((s, tk), lambda j, i: (0, i)), pl.BlockSpec((s, tn), lambda j, i: (0, j))],
        out_specs=[pl.BlockSpec((1, tk, tn), lambda j, i: (j, i, 0))],
        args=[a, b], carries=carries)
    return out, carried


def _weight_grad_stacked(a3, b3, name, tk=512, carries=()):
    n_g, s, k = a3.shape
    n = b3.shape[2]
    kq = k // N_CHIP
    tk = min(tk, k)
    chips_per_tile = tk // kq

    def body(a_ref, b_ref, o_ref):
        o_ref[...] = _dot_ta(a_ref[...], b_ref[...]).reshape(chips_per_tile, kq, n)

    (out,), carried = _pcall(
        body, name=name, grid=(n_g, k // tk),
        out_shape=[jax.ShapeDtypeStruct((N_CHIP, n_g, kq, n), F32)],
        in_specs=[pl.BlockSpec((None, s, tk), lambda g, i: (g, 0, i)), pl.BlockSpec((None, s, n), lambda g, i: (g, 0, 0))],
        out_specs=[pl.BlockSpec((chips_per_tile, None, kq, n), lambda g, i: (i, g, 0, 0))],
        args=[a3, b3], carries=carries)
    return out.reshape(N_CHIP, n_g * kq, n), carried


def _adamw(items, name, copy_grad=False, carries=()):
    shape = items[0][0].shape
    cols = shape[-1]
    rows = items[0][0].size // cols
    tr = _row_tile(rows, cols, target_bytes=1024 * 1024 // len(items))
    c1 = 1.0 - ADAM_B1 ** ADAM_STEP
    c2 = 1.0 - ADAM_B2 ** ADAM_STEP
    n_out = 4 if copy_grad else 3
    n = len(items)

    def body(*refs):
        for k in range(n):
            w_ref, g_ref, m_ref, v_ref = refs[4 * k:4 * k + 4]
            outs = refs[4 * n + n_out * k:4 * n + n_out * (k + 1)]
            gv = g_ref[...]
            nm = ADAM_B1 * m_ref[...] + (1.0 - ADAM_B1) * gv
            nv = ADAM_B2 * v_ref[...] + (1.0 - ADAM_B2) * (gv * gv)
            outs[0][...] = (-ADAM_LR) * ((nm / c1) / (jnp.sqrt(nv / c2) + ADAM_EPS) + ADAM_WD * w_ref[...])
            outs[1][...] = nm
            outs[2][...] = nv
            if copy_grad:
                outs[3][...] = gv

    spec = pl.BlockSpec((tr, cols), lambda i: (i, 0))
    outs, carried = _pcall(
        body, name=name, grid=(rows // tr,),
        out_shape=[jax.ShapeDtypeStruct((rows, cols), F32)] * (n_out * n),
        in_specs=[spec] * (4 * n), out_specs=[spec] * (n_out * n),
        args=[t.reshape(rows, cols) for item in items for t in item], carries=carries)
    return [tuple(o.reshape(shape) for o in outs[n_out * k:n_out * (k + 1)]) for k in range(n)], carried


def kernel(x, c, w_mod, b_mod, g_pre_mix, g_post_mix, w_in, conv_a_w, conv_a_b, w_a_out, conv_b_w, conv_b_b, w_gate_r, b_gate_r, w_gate_i, b_gate_i, lru_lambda, w_b_out, w_o, g_pre_mlp, g_post_mlp, w_mlp_up, w_mlp_down, loss_target, m_w_mod, m_b_mod, m_g_pre_mix, m_g_post_mix, m_w_in, m_conv_a_w, m_conv_a_b, m_w_a_out, m_conv_b_w, m_conv_b_b, m_w_gate_r, m_b_gate_r, m_w_gate_i, m_b_gate_i, m_lru_lambda, m_w_b_out, m_w_o, m_g_pre_mlp, m_g_post_mlp, m_w_mlp_up, m_w_mlp_down, v_w_mod, v_b_mod, v_g_pre_mix, v_g_post_mix, v_w_in, v_conv_a_w, v_conv_a_b, v_w_a_out, v_conv_b_w, v_conv_b_b, v_w_gate_r, v_b_gate_r, v_w_gate_i, v_b_gate_i, v_lru_lambda, v_w_b_out, v_w_o, v_g_pre_mlp, v_g_post_mlp, v_w_mlp_up, v_w_mlp_down):
    weights = dict(w_mod=w_mod, b_mod=b_mod, g_pre_mix=g_pre_mix, g_post_mix=g_post_mix, w_in=w_in, conv_a_w=conv_a_w,
                   conv_a_b=conv_a_b, w_a_out=w_a_out, conv_b_w=conv_b_w, conv_b_b=conv_b_b, w_gate_r=w_gate_r,
                   b_gate_r=b_gate_r, w_gate_i=w_gate_i, b_gate_i=b_gate_i, lru_lambda=lru_lambda, w_b_out=w_b_out,
                   w_o=w_o, g_pre_mlp=g_pre_mlp, g_post_mlp=g_post_mlp, w_mlp_up=w_mlp_up, w_mlp_down=w_mlp_down)
    mom1 = dict(w_mod=m_w_mod, b_mod=m_b_mod, g_pre_mix=m_g_pre_mix, g_post_mix=m_g_post_mix, w_in=m_w_in,
                conv_a_w=m_conv_a_w, conv_a_b=m_conv_a_b, w_a_out=m_w_a_out, conv_b_w=m_conv_b_w, conv_b_b=m_conv_b_b,
                w_gate_r=m_w_gate_r, b_gate_r=m_b_gate_r, w_gate_i=m_w_gate_i, b_gate_i=m_b_gate_i,
                lru_lambda=m_lru_lambda, w_b_out=m_w_b_out, w_o=m_w_o, g_pre_mlp=m_g_pre_mlp, g_post_mlp=m_g_post_mlp,
                w_mlp_up=m_w_mlp_up, w_mlp_down=m_w_mlp_down)
    mom2 = dict(w_mod=v_w_mod, b_mod=v_b_mod, g_pre_mix=v_g_pre_mix, g_post_mix=v_g_post_mix, w_in=v_w_in,
                conv_a_w=v_conv_a_w, conv_a_b=v_conv_a_b, w_a_out=v_w_a_out, conv_b_w=v_conv_b_w, conv_b_b=v_conv_b_b,
                w_gate_r=v_w_gate_r, b_gate_r=v_b_gate_r, w_gate_i=v_w_gate_i, b_gate_i=v_b_gate_i,
                lru_lambda=v_lru_lambda, w_b_out=v_w_b_out, w_o=v_w_o, g_pre_mlp=v_g_pre_mlp, g_post_mlp=v_g_post_mlp,
                w_mlp_up=v_w_mlp_up, w_mlp_down=v_w_mlp_down)
    names = list(weights)

    n_layer = w_in.shape[0]
    s, d = x.shape[1], x.shape[2]
    n_head, bw = w_gate_r.shape[1], w_gate_r.shape[2]
    dq = d // N_CHIP
    mq = w_mod.shape[2]
    n_mod = (N_CHIP * mq) // d
    ka, kb = conv_a_w.shape[1], conv_b_w.shape[1]

    mx, my, mc = _place()
    q_me = 2 * mx + my
    q_arr = jnp.reshape(q_me, (1,)).astype(jnp.int32)

    me_dev = 4 * mx + 2 * my + mc
    me_arr = jnp.reshape(me_dev, (1,)).astype(jnp.int32)

    big_names = ["w_in", "w_a_out", "w_b_out", "w_o", "w_mlp_up", "w_mlp_down"]
    groups = [["w_in"], ["w_a_out", "w_b_out", "w_o"], ["w_mlp_up", "w_mlp_down"]]
    placed = {("w_in", 0): _cast_place(w_in, 0, q_arr, "cast_place_w_in_0")}
    wfull = [dict() for _ in range(n_layer)]
    riders = {}
    for l in range(n_layer):
        riders.setdefault(3 * l - 1, []).append(([("w_in", l)], 1.0))
        riders.setdefault(3 * l - 2 if l else 0, []).append(([(nm, l) for nm in groups[1]], 0.9 if l else 0.5))
        riders.setdefault(3 * l, []).append(([("w_mlp_up", l)], 0.7 if l else 1.0))
        riders.setdefault(3 * l + 1, []).insert(0, ([("w_mlp_down", l)], 0.5))

    def gather_carry(call):
        return [_gather_carry([placed[k] for k in keys], frac) for keys, frac in riders.get(call, [])]

    def gathered(call, carried):
        for (keys, _), ws in zip(riders.get(call, []), carried):
            for (nm, l), w in zip(keys, ws):
                wfull[l][nm] = w.reshape(d, d) if nm in groups[1] else w

    n_conv_rows = n_layer * (ka + kb)
    conv_blk = -(-n_conv_rows // SUBLANES) * SUBLANES
    blk_rows = SUBLANES + conv_blk
    conv_rows = jnp.concatenate([jnp.concatenate([conv_a_w[l], conv_b_w[l]], axis=0) for l in range(n_layer)], axis=0)
    conv_rows = jnp.pad(conv_rows, ((0, conv_blk - n_conv_rows), (0, d - dq)))
    c_conv = jnp.concatenate([jnp.pad(c, ((0, SUBLANES - 1), (0, 0))), conv_rows], axis=0)
    rest = [(nm, l) for l in range(n_layer) for nm in big_names if (nm, l) != ("w_in", 0)]
    rest_placed, carried = _cast_place_all([(weights[nm], l) for nm, l in rest], q_arr, "cast_place_rest",
                                           carries=gather_carry(-1) + [_allgather_carry([c_conv])])
    placed.update(zip(rest, rest_placed))
    gathered(-1, carried[:1])
    gathered1 = lax.dynamic_update_slice(carried[1][0], c_conv, (me_dev * blk_rows, 0)).reshape(N_DEV, blk_rows, d)
    c_all = gathered1[:, 0, :]
    conv_full = jnp.concatenate([gathered1[2 * qb, SUBLANES:SUBLANES + n_conv_rows, :dq] for qb in range(N_CHIP)], axis=1)

    b_mod_shard = lax.dynamic_slice_in_dim(b_mod, q_me * mq, mq, axis=1)
    mod_part = _mod_forward(c_all, w_mod, b_mod_shard, "mod_forward")
    gathered2 = _all_gather_small(mod_part, "gather_mod").reshape(N_DEV, n_layer, N_DEV, mq)
    mod_rows = jnp.concatenate(
        [lax.dynamic_index_in_dim(gathered2[2 * qb], me_dev, axis=1, keepdims=False) for qb in range(N_CHIP)], axis=1)
    mods = [jnp.pad(mod_rows[l].reshape(n_mod, d), ((0, SUBLANES - n_mod), (0, 0))) for l in range(n_layer)]

    vecs = []
    for l in range(n_layer):
        base = l * (ka + kb)
        rows = [g_pre_mix[l], g_post_mix[l], conv_a_b[l], conv_b_b[l], b_gate_r[l], b_gate_i[l], lru_lambda[l],
                g_pre_mlp[l], g_post_mlp[l]]
        vecs.append(jnp.concatenate([jnp.stack(rows, axis=0), conv_full[base:base + ka + kb]], axis=0))

    wgs =[jnp.concatenate([w_gate_r[l], w_gate_i[l]], axis=-1).astype(BF16) for l in range(n_layer)]

    xs = x[0]
    saved = []
    for l in range(n_layer):
        wl = wfull[l]
        (h, proj, dgel), carried = _norm_proj(xs, mods[l], vecs[l], wl["w_in"], f"norm_proj_{l}", gather_carry(3 * l))
        gathered(3 * l, carried)
        (x1, conva, xb2, hh, abm, pa, pb, yy, gr, ggi, ga, gmult), carried = _mixer_forward(
            xs, proj, mods[l], vecs[l], wgs[l], wl["w_a_out"], wl["w_b_out"], wl["w_o"], f"mixer_forward_{l}",
            gather_carry(3 * l + 1))
        gathered(3 * l + 1, carried)
        (x2, h2, up, y2, *loss_tile), carried = _mlp_forward(
            x1, mods[l], vecs[l], wl["w_mlp_up"], wl["w_mlp_down"], f"mlp_forward_{l}", gather_carry(3 * l + 2),
            target=loss_target[0] if l == n_layer - 1 else None)
        gathered(3 * l + 2, carried)
        saved.append(dict(x=xs, h=h, proj=proj, x1=x1, conva=conva, xb2=xb2, hh=hh, abm=abm, pa=pa, pb=pb,
                          y=yy, r=gr, gi=ggi, a=ga, mult=gmult, dgel=dgel, h2=h2, up=up, y2=y2))
        xs = x2
    dxs = xs
    loss = lax.psum(loss_tile[0][0, 0], ("x", "y", "c"))

    chips_q = [q_me ^ 2, q_me ^ 1, q_me ^ 3]
    pf = jnp.stack([mc, q_me] + chips_q).astype(jnp.int32)
    rs = dict(grad={}, landed={}, to_send={}, from_chips={}, out={})
    to_exchange, to_scatter, to_join, to_gather = [], [], [], []
    small_own, small_all = {}, {}

    def ride(call, what, name=None):
        ex = list(to_exchange) if "x" in what else []
        sc = list(to_scatter) if "s" in what else []
        ga = list(to_gather) if "g" in what else []
        jn = []
        for key in (to_join if "j" in what else []):
            if key[0] not in [k[0] for k in jn]:
                jn.append(key)
        carries = []
        if ex:
            carries.append(_exchange_carry([rs["grad"][k] for k in ex]))
        if sc:
            carries.append(_scatter_carry([rs["to_send"][k] for k in sc]))
        if jn:
            carries.append(_join_carry([rs["out"][k[0]] for k in jn], [k[1] for k in jn]))
        if ga:
            carries.append(_allgather_carry([small_own[k] for k in ga]))
        if call is None:
            carried = _run_carries(carries, name) if carries else []
            res = None
        else:
            res, carried = call(carries)
        carried = list(carried)
        if ex:
            for k, ld in zip(ex, carried.pop(0)):
                to_exchange.remove(k)
                rs["landed"][k] = ld
                rs["to_send"][k] = _add_sibling_half(rs["grad"][k], ld, pf, f"rs_add_sibling_{k[0]}_{k[1]}")
                to_scatter.append(k)
        if sc:
            for k, fc in zip(sc, carried.pop(0)):
                to_scatter.remove(k)
                rs["out"][k[0]] = _add_chips(rs["grad"][k], rs["landed"][k], fc, pf, rs["out"].get(k[0]), k[1], n_layer,
                                             f"rs_add_chips_{k[0]}_{k[1]}")
                to_join.append(k)
        if jn:
            for k, o in zip(jn, carried.pop(0)):
                to_join.remove(k)
                rs["out"][k[0]] = o
        if ga:
            for k, o in zip(ga, carried.pop(0)):
                to_gather.remove(k)
                small_all[k] = o
        return res

    def gather_small(key, parts):
        small_own[key] = parts[0] if len(parts) == 1 else jnp.concatenate(parts, axis=0)
        to_gather.append(key)

    def ready(nm, l, g):
        rs["grad"][(nm, l)] = g
        to_exchange.append((nm, l))

    rowblk = lambda t: t.reshape(N_CHIP, t.shape[1] // N_CHIP, t.shape[2])
    small1_prev = None
    for l in reversed(range(n_layer)):
        wl, sv = wfull[l], saved[l]
        dx1, dy2, dup, act, small3 = ride(lambda cr: _mlp_backward(
            dxs, sv["x1"], sv["y2"], sv["up"], mods[l], vecs[l], wl["w_mlp_up"], wl["w_mlp_down"], f"mlp_backward_{l}", cr), "xsjg")
        ready("w_mlp_up", l, _weight_grad(sv["h2"], dup, f"grad_w_mlp_up_{l}", col_blocks=N_CHIP)[0])
        g_down = ride(lambda cr: _weight_grad(act, dy2, f"grad_w_mlp_down_{l}", carries=cr), "x")
        ready("w_mlp_down", l, rowblk(g_down))
        dproj, dab, small2, dwg = ride(lambda cr: _mixer_backward(
            dx1, sv["proj"], sv["conva"], sv["xb2"], sv["hh"], sv["pa"], sv["pb"], sv["y"],
            sv["r"], sv["gi"], sv["a"], sv["mult"], sv["dgel"], mods[l], vecs[l], wgs[l],
            wl["w_a_out"], wl["w_b_out"], wl["w_o"], f"mixer_backward_{l}", cr), "xsjg")
        gather_small(("late", l, "s"), ([small1_prev] if small1_prev is not None else []) + [small2, small3])
        gather_small(("late", l, "w"), [dwg.reshape(2 * bw, d).astype(BF16)])
        g_in = ride(lambda cr: _weight_grad(sv["h"], dproj, f"grad_w_in_{l}", col_blocks=N_CHIP, carries=cr), "xsj")
        ready("w_in", l, g_in)
        g_abo = ride(lambda cr: _weight_grad_stacked(sv["abm"], dab, f"grad_w_abo_{l}", carries=cr), "xg")
        ready("w_abo", l, g_abo)
        dxs, small1_prev = ride(lambda cr: _proj_backward(dproj, dx1, sv["x"], mods[l], vecs[l], wl["w_in"],
                                                          f"proj_backward_{l}", cr), "xsjg")
    grad_x = dxs[None]
    gather_small(("last", 0, "s"), [small1_prev])

    tail = 0
    while to_exchange or to_scatter or to_join or to_gather:
        ride(None, "xsjg", f"rs_tail_{tail}")
        tail += 1
    grads, deltas, new_m, new_v = {}, {}, {}, {}

    def adam(nms, copy_grad=False):
        items = [(weights[nm], grads[nm], mom1[nm], mom2[nm]) for nm in nms]
        res, _ = _adamw(items, "adamw_" + "_".join(nms), copy_grad)
        for nm, r in zip(nms, res):
            deltas[nm], new_m[nm], new_v[nm] = r[:3]
            if copy_grad:
                grads[nm] = r[3]

    for nms in (["w_mlp_up", "w_mlp_down"], ["w_in"]):
        for nm in nms:
            grads[nm] = rs["out"][nm].reshape(weights[nm].shape)
        adam(nms, True)

    sums ={k: _sum_devices(small_all[k], small_own[k], me_arr, f"sum_small_{k[0]}_{k[1]}_{k[2]}") for k in small_own}

    small_full = {}

    def rows_of(l, part):
        if part == 0:
            return (("late", l - 1, "s"), 0) if l >= 1 else (("last", 0, "s"), 0)
        if part == 3:
            return ("late", l, "w"), 0
        base = SUBLANES if l < n_layer - 1 else 0
        return ("late", l, "s"), base + (0, 0, 2 * SUBLANES)[part]

    def summed(l, part, row, n_rows=1):
        key, base = rows_of(l, part)
        return sums[key][base + row:base + row + n_rows]

    def per_device(l, part, row):
        key, base = rows_of(l, part)
        own = small_own[key]
        if key not in small_full:
            small_full[key] = lax.dynamic_update_slice(small_all[key], own, (me_dev * own.shape[0], 0)).reshape(
                (N_DEV,) + own.shape)
        return small_full[key][:, base + row:base + row + 1]

    mod_rows = [(0, SB1_DSH), (0, SB1_DSC), (1, SB2_DGT), (2, SB3_DSH), (2, SB3_DSC), (2, SB3_DGT)]
    dmod_all = jnp.stack([jnp.concatenate([per_device(l, p, r)[:, 0, :] for p, r in mod_rows], axis=1)
                          for l in range(n_layer)], axis=0)
    o1, o2, o3, o4 = 0, SUBLANES, 3 * SUBLANES, 4 * SUBLANES
    small_sum = jnp.stack([jnp.concatenate([summed(l, 0, 0, SUBLANES), summed(l, 1, 0, 2 * SUBLANES),
                                            summed(l, 2, 0, SUBLANES), summed(l, 3, 0, 2 * bw)], axis=0)
                           for l in range(n_layer)], axis=0)
    mod_rows_of = [o1 + SB1_DSH, o1 + SB1_DSC, o2 + SB2_DGT, o3 + SB3_DSH, o3 + SB3_DSC, o3 + SB3_DGT]
    grads["w_mod"] = _mod_backward(c_all.T, lax.dynamic_slice_in_dim(dmod_all, q_me * mq, mq, axis=2), "mod_backward")
    grads["b_mod"] = jnp.concatenate([small_sum[:, k, :] for k in mod_rows_of], axis=1)
    grads["g_pre_mix"] = small_sum[:, o1 + SB1_DG_PRE]
    grads["g_post_mix"] = small_sum[:, o2 + SB2_DG_POST]
    grads["conv_a_w"] = lax.dynamic_slice_in_dim(small_sum[:, o2 + SB2_DWA:o2 + SB2_DWA + ka], q_me * dq, dq, axis=2)
    grads["conv_a_b"] = small_sum[:, o2 + SB2_DBA]
    grads["conv_b_w"] = lax.dynamic_slice_in_dim(small_sum[:, o2 + SB2_DWB:o2 + SB2_DWB + kb], q_me * dq, dq, axis=2)
    grads["conv_b_b"] = small_sum[:, o2 + SB2_DBB]
    grads["lru_lambda"] = small_sum[:, o2 + SB2_DLAM]
    grads["b_gate_r"] = small_sum[:, o2 + SB2_DBR]
    grads["b_gate_i"] = small_sum[:, o2 + SB2_DBI]
    grads["g_pre_mlp"] = small_sum[:, o3 + SB3_DG_PRE]
    grads["g_post_mlp"] = small_sum[:, o3 + SB3_DG_POST]
    dwg_sum = small_sum[:, o4:].reshape(n_layer, n_head, bw, 2 * bw)
    grads["w_gate_r"] = dwg_sum[..., :bw]
    grads["w_gate_i"] = dwg_sum[..., bw:]

    for k, nm in enumerate(groups[1]):
        grads[nm] = rs["out"]["w_abo"][:, k * dq:(k + 1) * dq]

    by_shape = {}
    for nm in names:
        if nm not in deltas:
            by_shape.setdefault(weights[nm].shape, []).append(nm)
    for nms in by_shape.values():
        adam(nms)
    return (loss, grad_x, *[grads[nm] for nm in names], *[deltas[nm] for nm in names],
            *[new_m[nm] for nm in names], *[new_v[nm] for nm in names])
```

```python
import jax
import jax.numpy as jnp
from jax import lax
from jax.experimental import pallas as pl
from jax.experimental.pallas import tpu as pltpu

F32 = jnp.float32
BF16 = jnp.bfloat16
MESH = pl.DeviceIdType.MESH

EPS = 1e-6
LRU_C = 8.0
N_CHIP = 4
N_DEV = 8
ADAM_LR = 0.001
ADAM_B1 = 0.9
ADAM_B2 = 0.999
ADAM_EPS = 1e-08
ADAM_WD = 0.01
ADAM_STEP = 10

VMEM_LIMIT_BYTES = 56 * 1024 * 1024
SUBLANES = 8
LANES = 128
TOKENS_MATMUL_TILE = 512
TOKENS_MIXER_TILE = 256
GELU_K0 = 0.7978845608028654
GELU_K1 = 0.044715

V_G_PRE_MIX, V_G_POST_MIX, V_CONV_A_B, V_CONV_B_B, V_B_GATE_R, V_B_GATE_I, V_LAMBDA, V_G_PRE_MLP, V_G_POST_MLP = range(9)
V_CONV_A_W = 9
V_CONV_B_W = 12
M_SH_M, M_SC_M, M_GT_M, M_SH_F, M_SC_F, M_GT_F = range(6)


def _cparams(n_grid=0):
    sem = ("arbitrary",) * n_grid if n_grid else None
    return pltpu.CompilerParams(dimension_semantics=sem, vmem_limit_bytes=VMEM_LIMIT_BYTES)


def _full(shape):
    return pl.BlockSpec(shape, lambda *_: (0,) * len(shape))


def _dot(a, b):
    return jnp.dot(a, b, preferred_element_type=F32)


def _dot_tb(a, b):
    return lax.dot_general(a, b, (((1,), (1,)), ((), ())), preferred_element_type=F32)


def _dot_ta(a, b):
    return lax.dot_general(a, b, (((0,), (0,)), ((), ())), preferred_element_type=F32)


def _sigmoid(x):
    return 1.0 / (1.0 + jnp.exp(-x))


def _softplus(x):
    return jnp.maximum(x, 0.0) + jnp.log1p(jnp.exp(-jnp.abs(x)))


def _neg_expm1(x):
    series = -x * (1.0 + 0.5 * x * (1.0 + (x / 3.0) * (1.0 + 0.25 * x)))
    return jnp.where(x > -1e-2, series, 1.0 - jnp.exp(x))


def _gelu_and_grad(x):
    x2 = x * x
    s = _sigmoid(x * (2.0 * GELU_K0 + (2.0 * GELU_K0 * GELU_K1) * x2))
    gel = x * s
    return gel, s + gel * (1.0 - s) * (2.0 * GELU_K0 + (6.0 * GELU_K0 * GELU_K1) * x2)


def _rms(x):
    r = lax.rsqrt(jnp.mean(x * x, axis=-1, keepdims=True) + EPS)
    return x * r, r


def _rms_bwd(dxn, xn, r):
    return r * (dxn - xn * jnp.mean(dxn * xn, axis=-1, keepdims=True))


def _colsum(x):
    return jnp.sum(x, axis=0, keepdims=True)


def _rows(t, w):
    return lax.broadcasted_iota(jnp.int32, (t, w), 0)


def _shift_down(x, k, prev8):
    t, w = x.shape
    rolled = pltpu.roll(x, k, 0)
    head = jnp.where(_rows(SUBLANES, w) < k, pltpu.roll(prev8, k, 0), rolled[:SUBLANES])
    return jnp.concatenate([head, rolled[SUBLANES:]], axis=0)


def _shift_up(x, k, next8):
    t, w = x.shape
    rolled = pltpu.roll(x, t - k, 0)
    tail = jnp.where(_rows(SUBLANES, w) >= SUBLANES - k, pltpu.roll(next8, SUBLANES - k, 0), rolled[t - SUBLANES:])
    return jnp.concatenate([rolled[:t - SUBLANES], tail], axis=0)


SCAN_GROUP = 16


def _scan_steps(a, b, group, reverse):
    t, w = a.shape
    pos = _rows(t, w) & (group - 1)
    s = 1
    while s < group:
        keep = (pos < group - s) if reverse else (pos >= s)
        shift = (t - s) if reverse else s
        b = b + a * jnp.where(keep, pltpu.roll(b, shift, 0), 0.0)
        a = a * jnp.where(keep, pltpu.roll(a, shift, 0), 1.0)
        s *= 2
    return b, a


def _scan_two_level(a, b, carry_row, a_buf, b_buf, c_buf, reverse):
    t, w = a.shape
    grp = SCAN_GROUP
    n_grp = t // grp
    h_loc, a_cum = _scan_steps(a, b, grp, reverse)
    end = 0 if reverse else grp - 1
    a_end, h_end = [], []
    for j in range(w // LANES):
        a_buf[j] = a_cum[:, j * LANES:(j + 1) * LANES]
        b_buf[j] = h_loc[:, j * LANES:(j + 1) * LANES]
        a_end.append(a_buf[j, pl.ds(end, n_grp, stride=grp), :])
        h_end.append(b_buf[j, pl.ds(end, n_grp, stride=grp), :])
    a_end = jnp.concatenate(a_end, axis=1)
    h_end = jnp.concatenate(h_end, axis=1)
    h_grp, a_grp = _scan_steps(a_end, h_end, n_grp, reverse)
    h_grp = h_grp + a_grp * carry_row
    rows = _rows(n_grp, w)
    if reverse:
        entering = jnp.where(rows == n_grp - 1, carry_row, pltpu.roll(h_grp, n_grp - 1, 0))
    else:
        entering = jnp.where(rows == 0, carry_row, pltpu.roll(h_grp, 1, 0))
    c_buf[...] = entering
    out = [h_loc[g * grp:(g + 1) * grp] + a_cum[g * grp:(g + 1) * grp] * c_buf[g:g + 1, :] for g in range(n_grp)]
    return jnp.concatenate(out, axis=0)


def _row_tile(rows, cols, itemsize=4, target_bytes=2 * 1024 * 1024):
    if rows * cols * itemsize <= target_bytes or rows % SUBLANES:
        return rows
    t = max(SUBLANES, (target_bytes // (cols * itemsize)) // SUBLANES * SUBLANES)
    while rows % t:
        t -= SUBLANES
    return t


def _place():
    return lax.axis_index("x"), lax.axis_index("y"), lax.axis_index("c")


def _other_chips(x, y):
    chips = [(1 - x, y), (x, 1 - y), (1 - x, 1 - y)]
    return chips, [2 * cx + cy for cx, cy in chips]


def _all_gather_small(block, name):
    m_per, n = block.shape

    def body(x_ref, out_ref, send_sems, recv_sems, local_sem):
        x, y, c = _place()
        me, sibling = (x, y, c), (x, y, 1 - c)
        chips, _ = _other_chips(x, y)

        def rows(px, py, pc):
            return out_ref.at[pl.ds((4 * px + 2 * py + pc) * m_per, m_per), :]

        def copy(k, blk, to, src=None):
            return pltpu.make_async_remote_copy(
                src_ref=rows(*blk) if src is None else src, dst_ref=rows(*blk),
                send_sem=send_sems.at[k], recv_sem=recv_sems.at[k], device_id=to, device_id_type=MESH)

        mine = pltpu.make_async_copy(x_ref, rows(*me), local_sem)
        mine.start()
        first = [copy(0, me, sibling, src=x_ref)]
        first += [copy(1 + j, me, (*chip, c), src=x_ref) for j, chip in enumerate(chips)]
        for cp in first:
            cp.start()
        passed = [copy(4 + j, (*chip, c), sibling) for j, chip in enumerate(chips)]
        for j, chip in enumerate(chips):
            copy(1 + j, (*chip, c), me).wait_recv()
            passed[j].start()
        copy(0, sibling, me).wait_recv()
        for j, chip in enumerate(chips):
            copy(4 + j, (*chip, 1 - c), me).wait_recv()
        for cp in first + passed:
            cp.wait_send()
        mine.wait()

    return pl.pallas_call(
        body, name=name,
        out_shape=jax.ShapeDtypeStruct((N_DEV * m_per, n), block.dtype),
        in_specs=[pl.BlockSpec(memory_space=pltpu.VMEM)],
        out_specs=pl.BlockSpec(memory_space=pltpu.VMEM),
        scratch_shapes=[pltpu.SemaphoreType.DMA((7,)), pltpu.SemaphoreType.DMA((7,)), pltpu.SemaphoreType.DMA],
        compiler_params=pltpu.CompilerParams(vmem_limit_bytes=VMEM_LIMIT_BYTES),
    )(block)


def _cast_place(w, layer, q_arr, name):
    _, r, cols = w.shape
    tr = _row_tile(r, cols)

    def body(q_ref, w_ref, o_ref):
        o_ref[...] = w_ref[...].astype(BF16)

    return pl.pallas_call(
        body, name=name,
        out_shape=jax.ShapeDtypeStruct((N_CHIP, r, cols), BF16),
        grid_spec=pltpu.PrefetchScalarGridSpec(
            num_scalar_prefetch=1, grid=(r // tr,),
            in_specs=[pl.BlockSpec((1, tr, cols), lambda i, q_ref: (layer, i, 0))],
            out_specs=pl.BlockSpec((1, tr, cols), lambda i, q_ref: (q_ref[0], i, 0))),
        compiler_params=_cparams(1),
    )(q_arr, w)


class _Carry:
    def __init__(self, ins, out_shapes, aliases, sem_shapes, start, finish, mid=None, mid_frac=0.85):
        self.ins, self.out_shapes, self.aliases, self.sem_shapes = list(ins), list(out_shapes), dict(aliases), list(sem_shapes)
        self.start, self.mid, self.finish, self.mid_frac = start, mid, finish, mid_frac


def _pcall(body, *, name, grid, in_specs, out_specs, out_shape, args, scratch_shapes=(), carries=(), prefetch=()):
    in_specs, out_specs, out_shape = list(in_specs), list(out_specs), list(out_shape)
    scratch_shapes, args = list(scratch_shapes), list(args)
    n_in, n_out, n_scr, n_pre = len(in_specs), len(out_shape), len(scratch_shapes), len(prefetch)
    steps = 1
    for g in grid:
        steps *= g
    any_spec = pl.BlockSpec(memory_space=pl.ANY)
    aliases = {}
    spans = []
    for cr in carries:
        spans.append((len(args), len(out_shape), len(scratch_shapes)))
        for a, b in cr.aliases.items():
            aliases[n_pre + len(args) + a] = len(out_shape) + b
        args += cr.ins
        in_specs += [any_spec] * len(cr.ins)
        out_shape += cr.out_shapes
        out_specs += [any_spec] * len(cr.out_shapes)
        scratch_shapes += cr.sem_shapes
    n_all_in = len(args)
    n_all_out = len(out_shape)

    def wrapped(*refs):
        pre, refs = refs[:n_pre], refs[n_pre:]
        ins, outs, scr = refs[:n_all_in], refs[n_all_in:n_all_in + n_all_out], refs[n_all_in + n_all_out:]
        parts = [(cr, ins[a:a + len(cr.ins)], outs[b:b + len(cr.out_shapes)], scr[s:s + len(cr.sem_shapes)])
                 for cr, (a, b, s) in zip(carries, spans)]
        lin = 0
        for ax, g in enumerate(grid):
            lin = lin * g + pl.program_id(ax)

        def at(step, fn):
            if steps == 1:
                fn()
            else:
                pl.when(lin == step)(fn)

        def start_all():
            for cr, ci, co, cs in parts:
                cr.start(ci, co, cs)

        def finish_all():
            for cr, ci, co, cs in parts:
                cr.finish(ci, co, cs)

        if parts:
            at(0, start_all)
        body(*pre, *ins[:n_in], *outs[:n_out], *scr[:n_scr])
        for cr, ci, co, cs in parts:
            if cr.mid is not None:
                at(min(steps - 1, int(steps * cr.mid_frac)), lambda cr=cr, ci=ci, co=co, cs=cs: cr.mid(ci, co, cs))
        if parts:
            at(steps - 1, finish_all)

    if n_pre:
        res = pl.pallas_call(
            wrapped, name=name, out_shape=out_shape,
            grid_spec=pltpu.PrefetchScalarGridSpec(num_scalar_prefetch=n_pre, grid=tuple(grid), in_specs=in_specs,
                                                   out_specs=out_specs, scratch_shapes=scratch_shapes),
            input_output_aliases=aliases, compiler_params=_cparams(len(grid)),
        )(*prefetch, *args)
    else:
        res = pl.pallas_call(
            wrapped, name=name, grid=tuple(grid), out_shape=out_shape, in_specs=in_specs, out_specs=out_specs,
            scratch_shapes=scratch_shapes, input_output_aliases=aliases, compiler_params=_cparams(len(grid)),
        )(*args)
    res = list(res)
    return res[:n_out], [res[b:b + len(cr.out_shapes)] for cr, (_, b, _) in zip(carries, spans)]


def _run_carries(carries, name):
    return _pcall(lambda: None, name=name, grid=(), in_specs=[], out_specs=[], out_shape=[], args=[], carries=carries)[1]


CAST_STEPS = 8


def _cast_place_all(shards, q_arr, name, carries=()):
    n = len(shards)

    def body(q_ref, *refs):
        for k in range(n):
            refs[n + k][...] = refs[k][...].astype(BF16)

    def spec_in(k):
        w, layer = shards[k]
        return pl.BlockSpec((1, w.shape[1] // CAST_STEPS, w.shape[2]), lambda i, q_ref: (layer, i, 0))

    def spec_out(k):
        w, _ = shards[k]
        return pl.BlockSpec((1, w.shape[1] // CAST_STEPS, w.shape[2]), lambda i, q_ref: (q_ref[0], i, 0))

    return _pcall(
        body, name=name, grid=(CAST_STEPS,),
        out_shape=[jax.ShapeDtypeStruct((N_CHIP,) + w.shape[1:], BF16) for w, _ in shards],
        in_specs=[spec_in(k) for k in range(n)], out_specs=[spec_out(k) for k in range(n)],
        args=[w for w, _ in shards], carries=carries, prefetch=[q_arr])


def _gather_carry(bufs, mid_frac=0.85):
    n = len(bufs)

    def copies(o_refs, sems):
        send_sems, recv_sems = sems
        x, y, c = _place()
        q = 2 * x + y
        sibling = (x, y, 1 - c)
        chips, qs = _other_chips(x, y)

        def half(w, shard, pc):
            rh = bufs[w].shape[1] // 2
            return o_refs[w].at[shard, pl.ds(pc * rh, rh), :]

        def over_ici(w, j, shard):
            blk = half(w, shard, c)
            return pltpu.make_async_remote_copy(
                src_ref=blk, dst_ref=blk, send_sem=send_sems.at[w, j], recv_sem=recv_sems.at[w, j],
                device_id=(*chips[j], c), device_id_type=MESH)

        def to_sibling(w, j, pc):
            blk = half(w, qs[j], pc)
            return pltpu.make_async_remote_copy(
                src_ref=blk, dst_ref=blk, send_sem=send_sems.at[w, 3 + j], recv_sem=recv_sems.at[w, 3 + j],
                device_id=sibling, device_id_type=MESH)

        return q, c, qs, over_ici, to_sibling

    pairs = [(w, j) for w in range(n) for j in range(3)]

    def start(i_refs, o_refs, sems):
        q, _, _, over_ici, _ = copies(o_refs, sems)
        for w, j in pairs:
            over_ici(w, j, q).start()

    def mid(i_refs, o_refs, sems):
        _, c, qs, over_ici, to_sibling = copies(o_refs, sems)
        for w, j in pairs:
            over_ici(w, j, qs[j]).wait_recv()
            to_sibling(w, j, c).start()

    def finish(i_refs, o_refs, sems):
        q, c, _, over_ici, to_sibling = copies(o_refs, sems)
        for w, j in pairs:
            to_sibling(w, j, 1 - c).wait_recv()
        for w, j in pairs:
            over_ici(w, j, q).wait_send()
            to_sibling(w, j, c).wait_send()

    return _Carry(bufs, [jax.ShapeDtypeStruct(b.shape, b.dtype) for b in bufs], {w: w for w in range(n)},
                  [pltpu.SemaphoreType.DMA((n, 6)), pltpu.SemaphoreType.DMA((n, 6))], start, finish, mid, mid_frac)


def _exchange_carry(grads):
    n = len(grads)

    def copies(g_refs, l_refs, sems):
        send_sems, recv_sems = sems
        x, y, c = _place()
        out = []
        for w in range(n):
            rh = grads[w].shape[1] // 2
            out.append(pltpu.make_async_remote_copy(
                src_ref=g_refs[w].at[:, pl.ds((1 - c) * rh, rh), :], dst_ref=l_refs[w],
                send_sem=send_sems.at[w], recv_sem=recv_sems.at[w], device_id=(x, y, 1 - c), device_id_type=MESH))
        return out

    def start(g_refs, l_refs, sems):
        for cp in copies(g_refs, l_refs, sems):
            cp.start()

    def finish(g_refs, l_refs, sems):
        for cp in copies(g_refs, l_refs, sems):
            cp.wait()

    return _Carry(grads, [jax.ShapeDtypeStruct((N_CHIP, g.shape[1] // 2, g.shape[2]), g.dtype) for g in grads], {},
                  [pltpu.SemaphoreType.DMA((n,)), pltpu.SemaphoreType.DMA((n,))], start, finish)


def _scatter_carry(sums):
    n = len(sums)

    def copies(s_refs, l_refs, sems):
        send_sems, recv_sems = sems
        x, y, c = _place()
        chips, _ = _other_chips(x, y)
        return [pltpu.make_async_remote_copy(
            src_ref=s_refs[w].at[j], dst_ref=l_refs[w].at[j], send_sem=send_sems.at[w, j], recv_sem=recv_sems.at[w, j],
            device_id=(*chips[j], c), device_id_type=MESH) for w in range(n) for j in range(3)]

    def start(s_refs, l_refs, sems):
        for cp in copies(s_refs, l_refs, sems):
            cp.start()

    def finish(s_refs, l_refs, sems):
        for cp in copies(s_refs, l_refs, sems):
            cp.wait()

    return _Carry(sums, [jax.ShapeDtypeStruct(s.shape, s.dtype) for s in sums], {},
                  [pltpu.SemaphoreType.DMA((n, 3)), pltpu.SemaphoreType.DMA((n, 3))], start, finish)


def _join_carry(outs, layers):
    n = len(outs)

    def copy(o_refs, sems, w, mine):
        send_sems, recv_sems = sems
        x, y, c = _place()
        r = outs[w].shape[1]
        rows = o_refs[w].at[layers[w], pl.ds((c if mine else 1 - c) * (r // 2), r // 2), :]
        return pltpu.make_async_remote_copy(
            src_ref=rows, dst_ref=rows, send_sem=send_sems.at[w], recv_sem=recv_sems.at[w],
            device_id=(x, y, 1 - c), device_id_type=MESH)

    def start(i_refs, o_refs, sems):
        for w in range(n):
            copy(o_refs, sems, w, True).start()

    def finish(i_refs, o_refs, sems):
        for w in range(n):
            copy(o_refs, sems, w, True).wait_send()
        for w in range(n):
            copy(o_refs, sems, w, False).wait_recv()

    return _Carry(outs, [jax.ShapeDtypeStruct(o.shape, o.dtype) for o in outs], {w: w for w in range(n)},
                  [pltpu.SemaphoreType.DMA((n,)), pltpu.SemaphoreType.DMA((n,))], start, finish)


PF_C, PF_Q, PF_QS = 0, 1, 2


def _add_sibling_half(g, landed, pf, name):
    _, r, cols = g.shape
    rh = r // 2
    tr = _row_tile(rh, cols)
    nr = rh // tr

    def body(pf_ref, g_ref, l_ref, o_ref):
        o_ref[...] = (g_ref[...] + l_ref[...]).astype(BF16)

    return pl.pallas_call(
        body, name=name,
        out_shape=jax.ShapeDtypeStruct((3, rh, cols), BF16),
        grid_spec=pltpu.PrefetchScalarGridSpec(
            num_scalar_prefetch=1, grid=(3, nr),
            in_specs=[pl.BlockSpec((1, tr, cols), lambda j, i, pf_ref: (pf_ref[PF_QS + j], pf_ref[PF_C] * nr + i, 0)),
                      pl.BlockSpec((1, tr, cols), lambda j, i, pf_ref: (pf_ref[PF_QS + j], i, 0))],
            out_specs=pl.BlockSpec((1, tr, cols), lambda j, i, pf_ref: (j, i, 0))),
        compiler_params=_cparams(2),
    )(pf, g, landed)


def _add_chips(g, landed, from_chips, pf, prev, layer, n_layer, name):
    _, r, cols = g.shape
    rh = r // 2
    tr = _row_tile(rh, cols)
    nr = rh // tr

    def body(pf_ref, g_ref, l_ref, f_ref, *rest):
        o_ref = rest[-1]
        acc = g_ref[0] + l_ref[0]
        for j in range(3):
            acc = acc + f_ref[j].astype(F32)
        o_ref[0] = acc

    in_specs = [pl.BlockSpec((1, tr, cols), lambda i, pf_ref: (pf_ref[PF_Q], pf_ref[PF_C] * nr + i, 0)),
                pl.BlockSpec((1, tr, cols), lambda i, pf_ref: (pf_ref[PF_Q], i, 0)),
                pl.BlockSpec((3, tr, cols), lambda i, pf_ref: (0, i, 0))]
    args = [pf, g, landed, from_chips]
    aliases = {}
    if prev is not None:
        in_specs.append(pl.BlockSpec(memory_space=pl.ANY))
        args.append(prev)
        aliases = {4: 0}
    return pl.pallas_call(
        body, name=name,
        out_shape=jax.ShapeDtypeStruct((n_layer, r, cols), F32),
        grid_spec=pltpu.PrefetchScalarGridSpec(
            num_scalar_prefetch=1, grid=(nr,), in_specs=in_specs,
            out_specs=pl.BlockSpec((1, tr, cols), lambda i, pf_ref: (layer, pf_ref[PF_C] * nr + i, 0))),
        input_output_aliases=aliases,
        compiler_params=_cparams(1),
    )(*args)


def _allgather_carry(blocks):
    n = len(blocks)

    def copies(b_refs, o_refs, sems):
        send_sems, recv_sems = sems
        x, y, c = _place()
        chips, _ = _other_chips(x, y)

        def place(w, px, py, pc):
            m = blocks[w].shape[0]
            return o_refs[w].at[pl.ds((4 * px + 2 * py + pc) * m, m), :]

        def own_to(w, k, to):
            dst = place(w, x, y, c)
            return pltpu.make_async_remote_copy(src_ref=b_refs[w], dst_ref=dst, send_sem=send_sems.at[w, k],
                                                recv_sem=recv_sems.at[w, k], device_id=to, device_id_type=MESH)

        def landed_from(w, k, px, py, pc):
            blk = place(w, px, py, pc)
            return pltpu.make_async_remote_copy(src_ref=blk, dst_ref=blk, send_sem=send_sems.at[w, k],
                                                recv_sem=recv_sems.at[w, k], device_id=(x, y, 1 - c), device_id_type=MESH)

        return x, y, c, chips, own_to, landed_from

    def start(b_refs, o_refs, sems):
        x, y, c, chips, own_to, _ = copies(b_refs, o_refs, sems)
        for w in range(n):
            own_to(w, 0, (x, y, 1 - c)).start()
            for j, chip in enumerate(chips):
                own_to(w, 1 + j, (*chip, c)).start()

    def mid(b_refs, o_refs, sems):
        x, y, c, chips, _, landed_from = copies(b_refs, o_refs, sems)
        for w in range(n):
            for j, chip in enumerate(chips):
                landed_from(w, 1 + j, *chip, c).wait_recv()
                landed_from(w, 4 + j, *chip, c).start()

    def finish(b_refs, o_refs, sems):
        x, y, c, chips, own_to, landed_from = copies(b_refs, o_refs, sems)
        for w in range(n):
            landed_from(w, 0, x, y, 1 - c).wait_recv()
            for j, chip in enumerate(chips):
                landed_from(w, 4 + j, *chip, 1 - c).wait_recv()
            own_to(w, 0, (x, y, 1 - c)).wait_send()
            for j, chip in enumerate(chips):
                own_to(w, 1 + j, (*chip, c)).wait_send()
                landed_from(w, 4 + j, *chip, c).wait_send()

    return _Carry(blocks, [jax.ShapeDtypeStruct((N_DEV * b.shape[0], b.shape[1]), b.dtype) for b in blocks], {},
                  [pltpu.SemaphoreType.DMA((n, 7)), pltpu.SemaphoreType.DMA((n, 7))], start, finish, mid)


def _sum_devices(gathered, own, me_arr, name):
    m, n = own.shape
    tr = _row_tile(m, n, itemsize=own.dtype.itemsize, target_bytes=256 * 1024)
    nr = m // tr

    def body(me_ref, *refs):
        g_refs, own_ref, o_ref = refs[:N_DEV], refs[N_DEV], refs[N_DEV + 1]
        me = me_ref[0]
        acc = None
        for dev in range(N_DEV):
            term = jnp.where(me == dev, own_ref[...], g_refs[dev][...]).astype(F32)
            acc = term if acc is None else acc + term
        o_ref[...] = acc

    def dev_rows(dev):
        return pl.BlockSpec((tr, n), lambda i, me_ref: (dev * nr + i, 0))

    return pl.pallas_call(
        body, name=name,
        out_shape=jax.ShapeDtypeStruct((m, n), F32),
        grid_spec=pltpu.PrefetchScalarGridSpec(
            num_scalar_prefetch=1, grid=(nr,),
            in_specs=[dev_rows(dev) for dev in range(N_DEV)] + [pl.BlockSpec((tr, n), lambda i, me_ref: (i, 0))],
            out_specs=pl.BlockSpec((tr, n), lambda i, me_ref: (i, 0))),
        compiler_params=_cparams(1),
    )(me_arr, *([gathered] * N_DEV), own)


def _mod_forward(c_all, w_mod, b_mod_shard, name):
    n_layer, d, mq = w_mod.shape

    def body(c_ref, w_ref, b_ref, o_ref):
        cv = c_ref[...]
        o_ref[...] = _dot(cv * _sigmoid(cv), w_ref[0]) + b_ref[0]

    return pl.pallas_call(
        body, name=name, grid=(n_layer,),
        out_shape=jax.ShapeDtypeStruct((n_layer * N_DEV, mq), F32),
        in_specs=[_full((N_DEV, d)), pl.BlockSpec((1, d, mq), lambda l: (l, 0, 0)),
                  pl.BlockSpec((1, 1, mq), lambda l: (l, 0, 0))],
        out_specs=pl.BlockSpec((N_DEV, mq), lambda l: (l, 0)),
        compiler_params=_cparams(1),
    )(c_all, w_mod, b_mod_shard.reshape(n_layer, 1, mq))


def _mod_backward(c_all_t, dmod_shard, name):
    n_layer, _, mq = dmod_shard.shape
    d = c_all_t.shape[0]

    def body(c_ref, dm_ref, o_ref):
        cv = c_ref[...]
        o_ref[0] = _dot(cv * _sigmoid(cv), dm_ref[0])

    return pl.pallas_call(
        body, name=name, grid=(n_layer,),
        out_shape=jax.ShapeDtypeStruct((n_layer, d, mq), F32),
        in_specs=[_full((d, N_DEV)), pl.BlockSpec((1, N_DEV, mq), lambda l: (l, 0, 0))],
        out_specs=pl.BlockSpec((1, d, mq), lambda l: (l, 0, 0)),
        compiler_params=_cparams(1),
    )(c_all_t, dmod_shard)


def _norm_proj(x, mod, vec, w_in, name, carries=()):
    s, d = x.shape
    nq = w_in.shape[2]
    ts = min(TOKENS_MATMUL_TILE, s)

    def body(x_ref, mod_ref, vec_ref, w_ref, h_ref, p_ref, dgel_ref):
        xn, _ = _rms(x_ref[...])
        gm = vec_ref[V_G_PRE_MIX:V_G_PRE_MIX + 1, :] * (1.0 + mod_ref[M_SC_M:M_SC_M + 1, :])
        h = (xn * gm + mod_ref[M_SH_M:M_SH_M + 1, :]).astype(BF16)
        h_ref[...] = h
        for qb in range(N_CHIP):
            pq = _dot(h, w_ref[qb])
            for k in range(N_CHIP * nq // d):
                lo, hi = max(qb * nq, k * d), min((qb + 1) * nq, (k + 1) * d)
                if lo >= hi:
                    continue
                piece = pq[:, lo - qb * nq:hi - qb * nq]
                if k == 4:
                    piece, dgel = _gelu_and_grad(piece)
                    dgel_ref[:, lo - 4 * d:hi - 4 * d] = dgel.astype(BF16)
                elif k >= 5:
                    piece = _sigmoid(piece)
                p_ref[:, lo:hi] = piece.astype(BF16)

    tile = pl.BlockSpec((ts, d), lambda i: (i, 0))
    return _pcall(
        body, name=name, grid=(s // ts,),
        out_shape=[jax.ShapeDtypeStruct((s, d), BF16), jax.ShapeDtypeStruct((s, N_CHIP * nq), BF16),
                   jax.ShapeDtypeStruct((s, d), BF16)],
        in_specs=[tile, _full(mod.shape), _full(vec.shape), _full(w_in.shape)],
        out_specs=[tile, pl.BlockSpec((ts, N_CHIP * nq), lambda i: (i, 0)), tile],
        args=[x, mod, vec, w_in], carries=carries)


def _gate_pre(xb2_b, wg_ref, n_head, bw):
    zr, zi = [], []
    for hd in range(n_head):
        z = _dot(xb2_b[:, hd * bw:(hd + 1) * bw], wg_ref[hd])
        zr.append(z[:, :bw])
        zi.append(z[:, bw:])
    return jnp.concatenate(zr, axis=1), jnp.concatenate(zi, axis=1)


def _lru_coeffs(xb2, wg_ref, vec_ref, n_head, bw):
    zr, zi = _gate_pre(xb2.astype(BF16), wg_ref, n_head, bw)
    r = _sigmoid(zr + vec_ref[V_B_GATE_R:V_B_GATE_R + 1, :])
    gi = _sigmoid(zi + vec_ref[V_B_GATE_I:V_B_GATE_I + 1, :])
    sp = _softplus(-vec_ref[V_LAMBDA:V_LAMBDA + 1, :])
    log_a = (-LRU_C) * r * sp
    a = jnp.exp(log_a)
    mult = jnp.sqrt(_neg_expm1(2.0 * log_a))
    return r, gi, sp, a, mult


def _mixer_forward(x, proj, mod, vec, wg, w_a_out, w_b_out, w_o, name, carries=()):
    s, d = x.shape
    n_head, bw, _ = wg.shape
    ts = min(TOKENS_MIXER_TILE, s)

    def body(x_ref, p_ref, mod_ref, vec_ref, wg_ref, wa_ref, wb_ref, wo_ref,
             x1_ref, conva_ref, xb2_ref, hh_ref, abm_ref, pa_ref, pb_ref, y_ref,
             r_ref, gi_ref, a_ref, mult_ref,
             cv_tail, xb_tail, h_last, a_buf, b_buf, c_buf):
        i = pl.program_id(0)

        @pl.when(i == 0)
        def _():
            cv_tail[...] = jnp.zeros_like(cv_tail)
            xb_tail[...] = jnp.zeros_like(xb_tail)
            h_last[...] = jnp.zeros_like(h_last)

        def seg(k):
            return p_ref[:, k * d:(k + 1) * d].astype(F32)

        def vrow(k):
            return vec_ref[k:k + 1, :]

        b_a, c_a, v_a, x_b, gel, sa, sb = (seg(k) for k in range(7))
        cv = c_a * v_a
        prev_cv = cv_tail[...]
        conv_a = (vrow(V_CONV_A_B) + vrow(V_CONV_A_W) * _shift_down(cv, 2, prev_cv)
                  + vrow(V_CONV_A_W + 1) * _shift_down(cv, 1, prev_cv) + vrow(V_CONV_A_W + 2) * cv)
        cv_tail[...] = cv[ts - SUBLANES:]
        y_a = b_a * conv_a
        prev_xb = xb_tail[...]
        xb2 = (vrow(V_CONV_B_B) + vrow(V_CONV_B_W) * _shift_down(x_b, 3, prev_xb)
               + vrow(V_CONV_B_W + 1) * _shift_down(x_b, 2, prev_xb)
               + vrow(V_CONV_B_W + 2) * _shift_down(x_b, 1, prev_xb) + vrow(V_CONV_B_W + 3) * x_b)
        xb_tail[...] = x_b[ts - SUBLANES:]
        r, gi, _, a, mult = _lru_coeffs(xb2, wg_ref, vec_ref, n_head, bw)
        r_ref[...] = r
        gi_ref[...] = gi
        a_ref[...] = a
        mult_ref[...] = mult
        hh = _scan_two_level(a, mult * gi * xb2, h_last[SUBLANES - 1:SUBLANES, :], a_buf, b_buf, c_buf, reverse=False)
        h_last[...] = hh[ts - SUBLANES:]
        y_b = hh * gel
        ya_b, yb_b = y_a.astype(BF16), y_b.astype(BF16)
        pa = _dot(ya_b, wa_ref[...])
        pb = _dot(yb_b, wb_ref[...])
        m = (sa * pa + sb * pb).astype(BF16)
        y = _dot(m, wo_ref[...])
        yn, _ = _rms(y)
        gg = mod_ref[M_GT_M:M_GT_M + 1, :] * vrow(V_G_POST_MIX)
        x1_ref[...] = x_ref[...] + yn * gg
        conva_ref[...] = conv_a.astype(BF16)
        xb2_ref[...] = xb2
        hh_ref[...] = hh
        abm_ref[0] = ya_b
        abm_ref[1] = yb_b
        abm_ref[2] = m
        pa_ref[...] = pa.astype(BF16)
        pb_ref[...] = pb.astype(BF16)
        y_ref[...] = y.astype(BF16)

    tile = pl.BlockSpec((ts, d), lambda i: (i, 0))
    tile3 = pl.BlockSpec((3, ts, d), lambda i: (0, i, 0))
    sd = lambda dt: jax.ShapeDtypeStruct((s, d), dt)
    return _pcall(
        body, name=name, grid=(s // ts,),
        out_shape=[sd(F32), sd(BF16), sd(F32), sd(F32), jax.ShapeDtypeStruct((3, s, d), BF16), sd(BF16), sd(BF16), sd(BF16),
                   sd(F32), sd(F32), sd(F32), sd(F32)],
        in_specs=[tile, pl.BlockSpec((ts, 7 * d), lambda i: (i, 0)), _full(mod.shape), _full(vec.shape),
                  _full(wg.shape), _full(w_a_out.shape), _full(w_b_out.shape), _full(w_o.shape)],
        out_specs=[tile] * 4 + [tile3] + [tile] * 7,
        scratch_shapes=[pltpu.VMEM((SUBLANES, d), F32)] * 3 + [pltpu.VMEM((d // LANES, ts, LANES), F32)] * 2
                       + [pltpu.VMEM((ts // SCAN_GROUP, d), F32)],
        args=[x, proj, mod, vec, wg, w_a_out, w_b_out, w_o], carries=carries)


def _mlp_forward(x1, mod, vec, w_up, w_down, name, carries=(), target=None):
    s, d = x1.shape
    fq = w_up.shape[2]
    ts = min(TOKENS_MATMUL_TILE, s)

    def body(x_ref, *refs):
        if target is None:
            mod_ref, vec_ref, wu_ref, wd_ref, x2_ref, h2_ref, up_ref, y2_ref = refs
        else:
            t_ref, mod_ref, vec_ref, wu_ref, wd_ref, x2_ref, h2_ref, up_ref, y2_ref, loss_ref = refs
        x = x_ref[...]
        xn, _ = _rms(x)
        gm = vec_ref[V_G_PRE_MLP:V_G_PRE_MLP + 1, :] * (1.0 + mod_ref[M_SC_F:M_SC_F + 1, :])
        h2 = (xn * gm + mod_ref[M_SH_F:M_SH_F + 1, :]).astype(BF16)
        h2_ref[...] = h2
        y2 = jnp.zeros((ts, d), F32)
        for qb in range(N_CHIP):
            up = _dot(h2, wu_ref[qb])
            up_ref[:, qb * fq:(qb + 1) * fq] = up.astype(BF16)
            ru = jnp.maximum(up, 0.0)
            y2 = y2 + _dot((ru * ru).astype(BF16), wd_ref[qb])
        y2_ref[...] = y2.astype(BF16)
        yn, _ = _rms(y2)
        gg = mod_ref[M_GT_F:M_GT_F + 1, :] * vec_ref[V_G_POST_MLP:V_G_POST_MLP + 1, :]
        x2 = x + yn * gg
        if target is None:
            x2_ref[...] = x2
        else:
            @pl.when(pl.program_id(0) == 0)
            def _():
                loss_ref[...] = jnp.zeros_like(loss_ref)

            err = x2 - t_ref[...]
            x2_ref[...] = err * (1.0 / d)
            loss_ref[...] += jnp.sum(jnp.sum(err * err, axis=1, keepdims=True), axis=0, keepdims=True) * (0.5 / d)

    tile = pl.BlockSpec((ts, d), lambda i: (i, 0))
    last = target is not None
    return _pcall(
        body, name=name, grid=(s // ts,),
        out_shape=[jax.ShapeDtypeStruct((s, d), F32), jax.ShapeDtypeStruct((s, d), BF16),
                   jax.ShapeDtypeStruct((s, N_CHIP * fq), BF16), jax.ShapeDtypeStruct((s, d), BF16)]
                  + ([jax.ShapeDtypeStruct((SUBLANES, LANES), F32)] if last else []),
        in_specs=[tile] + ([tile] if last else []) + [_full(mod.shape), _full(vec.shape), _full(w_up.shape), _full(w_down.shape)],
        out_specs=[tile, tile, pl.BlockSpec((ts, N_CHIP * fq), lambda i: (i, 0)), tile]
                 + ([_full((SUBLANES, LANES))] if last else []),
        args=[x1] + ([target] if last else []) + [mod, vec, w_up, w_down], carries=carries)


SB3_DSH, SB3_DSC, SB3_DGT, SB3_DG_PRE, SB3_DG_POST = range(5)
SB1_DSH, SB1_DSC, SB1_DG_PRE = range(3)
(SB2_DGT, SB2_DG_POST, SB2_DWA, SB2_DBA, SB2_DWB, SB2_DBB, SB2_DLAM, SB2_DBR, SB2_DBI) = (0, 1, 2, 5, 6, 10, 11, 12, 13)


def _mlp_backward(dx2, x1, y2, up, mod, vec, w_up, w_down, name, carries=()):
    s, d = dx2.shape
    fq = w_up.shape[2]
    ts = min(TOKENS_MIXER_TILE, s)
    n_t = s // ts

    def body(dx2_ref, x_ref, y2_ref, up_ref, mod_ref, vec_ref, wu_ref, wd_ref,
             dx1_ref, dy2_ref, dup_ref, act_ref, small_ref):
        i = pl.program_id(0)

        @pl.when(i == 0)
        def _():
            small_ref[...] = jnp.zeros_like(small_ref)

        dout = dx2_ref[...]
        y2n, ry = _rms(y2_ref[...].astype(F32))
        g_post = vec_ref[V_G_POST_MLP:V_G_POST_MLP + 1, :]
        gt = mod_ref[M_GT_F:M_GT_F + 1, :]
        dgg = _colsum(dout * y2n)
        dy2 = _rms_bwd(dout * (gt * g_post), y2n, ry).astype(BF16)
        dy2_ref[...] = dy2
        dh2 = jnp.zeros((ts, d), F32)
        for qb in range(N_CHIP):
            cols = slice(qb * fq, (qb + 1) * fq)
            dact = _dot_tb(dy2, wd_ref[qb])
            ru = jnp.maximum(up_ref[:, cols].astype(F32), 0.0)
            dup = (dact * (2.0 * ru)).astype(BF16)
            dup_ref[:, cols] = dup
            act_ref[:, cols] = (ru * ru).astype(BF16)
            dh2 = dh2 + _dot_tb(dup, wu_ref[qb])
        xn, r = _rms(x_ref[...])
        g_pre = vec_ref[V_G_PRE_MLP:V_G_PRE_MLP + 1, :]
        sc1 = 1.0 + mod_ref[M_SC_F:M_SC_F + 1, :]
        dsh = _colsum(dh2)
        dgm = _colsum(dh2 * xn)
        dx1_ref[...] = dout + _rms_bwd(dh2 * (g_pre * sc1), xn, r)
        small_ref[SB3_DSH:SB3_DSH + 1, :] += dsh
        small_ref[SB3_DSC:SB3_DSC + 1, :] += dgm
        small_ref[SB3_DGT:SB3_DGT + 1, :] += dgg

        @pl.when(i == n_t - 1)
        def _():
            dgm_t = small_ref[SB3_DSC:SB3_DSC + 1, :]
            dgg_t = small_ref[SB3_DGT:SB3_DGT + 1, :]
            small_ref[SB3_DSC:SB3_DSC + 1, :] = dgm_t * g_pre
            small_ref[SB3_DG_PRE:SB3_DG_PRE + 1, :] = dgm_t * sc1
            small_ref[SB3_DGT:SB3_DGT + 1, :] = dgg_t * g_post
            small_ref[SB3_DG_POST:SB3_DG_POST + 1, :] = dgg_t * gt

    tile = pl.BlockSpec((ts, d), lambda i: (i, 0))
    wide = pl.BlockSpec((ts, N_CHIP * fq), lambda i: (i, 0))
    return _pcall(
        body, name=name, grid=(n_t,),
        out_shape=[jax.ShapeDtypeStruct((s, d), F32), jax.ShapeDtypeStruct((s, d), BF16),
                   jax.ShapeDtypeStruct((s, N_CHIP * fq), BF16), jax.ShapeDtypeStruct((s, N_CHIP * fq), BF16),
                   jax.ShapeDtypeStruct((SUBLANES, d), F32)],
        in_specs=[tile, tile, tile, wide, _full(mod.shape), _full(vec.shape), _full(w_up.shape), _full(w_down.shape)],
        out_specs=[tile, tile, wide, wide, _full((SUBLANES, d))],
        args=[dx2, x1, y2, up, mod, vec, w_up, w_down], carries=carries)


def _mixer_backward(dx1, proj, conva, xb2s, hhs, pas, pbs, ys, rs_, gis, as_, mults, dgels, mod, vec, wg, w_a_out, w_b_out,
                    w_o, name, carries=()):
    s, d = dx1.shape
    n_head, bw, _ = wg.shape
    ts = min(TOKENS_MIXER_TILE, s)
    n_t = s // ts

    def body(dx1_ref, p_ref, conva_ref, xb2_ref, hh_ref, pa_ref, pb_ref, y_ref, r_ref, gi_ref, a_ref, mult_ref, dgel_ref,
             mod_ref, vec_ref, wg_ref, wa_ref, wb_ref, wo_ref,
             dp_ref, dab_ref, small_ref, dwg_ref,
             dconv_head, dxb2_head, a_head, g_head, a_buf, b_buf, c_buf):
        i = pl.program_id(0)

        @pl.when(i == 0)
        def _():
            small_ref[...] = jnp.zeros_like(small_ref)
            dwg_ref[...] = jnp.zeros_like(dwg_ref)
            dconv_head[...] = jnp.zeros_like(dconv_head)
            dxb2_head[...] = jnp.zeros_like(dxb2_head)
            a_head[...] = jnp.zeros_like(a_head)
            g_head[...] = jnp.zeros_like(g_head)

        def seg(k):
            return p_ref[:, k * d:(k + 1) * d].astype(F32)

        def vrow(k):
            return vec_ref[k:k + 1, :]

        def acc(row, val):
            small_ref[row:row + 1, :] += val

        dout = dx1_ref[...]
        yn, ry = _rms(y_ref[...].astype(F32))
        g_post = vrow(V_G_POST_MIX)
        gt = mod_ref[M_GT_M:M_GT_M + 1, :]
        acc(SB2_DGT, _colsum(dout * yn))
        dy = _rms_bwd(dout * (gt * g_post), yn, ry).astype(BF16)
        dab_ref[2] = dy
        dm = _dot_tb(dy, wo_ref[...])
        sa, sb = seg(5), seg(6)
        dpa = (dm * sa).astype(BF16)
        dpb = (dm * sb).astype(BF16)
        dab_ref[0] = dpa
        dab_ref[1] = dpb
        du_a = dm * pa_ref[...].astype(F32) * (sa * (1.0 - sa))
        du_b = dm * pb_ref[...].astype(F32) * (sb * (1.0 - sb))
        dp_ref[:, 5 * d:6 * d] = du_a.astype(BF16)
        dp_ref[:, 6 * d:7 * d] = du_b.astype(BF16)
        dy_a = _dot_tb(dpa, wa_ref[...])
        dy_b = _dot_tb(dpb, wb_ref[...])

        b_a, c_a, v_a = seg(0), seg(1), seg(2)
        dp_ref[:, 0:d] = (dy_a * conva_ref[...].astype(F32)).astype(BF16)
        dconv = dy_a * b_a
        nxt = dconv_head[...]
        d1 = _shift_up(dconv, 1, nxt)
        d2 = _shift_up(dconv, 2, nxt)
        dconv_head[...] = dconv[:SUBLANES]
        dcv = vrow(V_CONV_A_W + 2) * dconv + vrow(V_CONV_A_W + 1) * d1 + vrow(V_CONV_A_W) * d2
        cv = c_a * v_a
        acc(SB2_DWA + 2, _colsum(cv * dconv))
        acc(SB2_DWA + 1, _colsum(cv * d1))
        acc(SB2_DWA, _colsum(cv * d2))
        acc(SB2_DBA, _colsum(dconv))
        dp_ref[:, d:2 * d] = (dcv * v_a).astype(BF16)
        dp_ref[:, 2 * d:3 * d] = (dcv * c_a).astype(BF16)

        x_b, gel = seg(3), seg(4)
        hh = hh_ref[...]
        dp_ref[:, 4 * d:5 * d] = (dy_b * hh * dgel_ref[...].astype(F32)).astype(BF16)
        dhh = dy_b * gel
        xb2 = xb2_ref[...]
        r, gi, a, mult = r_ref[...], gi_ref[...], a_ref[...], mult_ref[...]
        sp = _softplus(-vrow(V_LAMBDA))
        a_next = _shift_up(a, 1, a_head[...])
        g = _scan_two_level(a_next, dhh, g_head[0:1, :], a_buf, b_buf, c_buf, reverse=True)
        a_head[...] = a[:SUBLANES]
        g_head[...] = g[:SUBLANES]
        gix = gi * xb2
        gm = g * mult
        dlog_a = g * (hh - mult * gix) - (g * gix) * (a * a / mult)
        dgi = gm * xb2
        dxb2 = gm * gi
        acc(SB2_DLAM, _colsum(dlog_a * r))
        dzr = dlog_a * ((-LRU_C) * sp) * (r * (1.0 - r))
        dzi = dgi * (gi * (1.0 - gi))
        acc(SB2_DBR, _colsum(dzr))
        acc(SB2_DBI, _colsum(dzi))
        xb2_b = xb2.astype(BF16)
        back = []
        for hd in range(n_head):
            cols = slice(hd * bw, (hd + 1) * bw)
            dz = jnp.concatenate([dzr[:, cols], dzi[:, cols]], axis=1).astype(BF16)
            back.append(_dot_tb(dz, wg_ref[hd]))
            dwg_ref[hd] += _dot_ta(xb2_b[:, cols], dz)
        dxb2 = dxb2 + jnp.concatenate(back, axis=1)
        nxt = dxb2_head[...]
        e1 = _shift_up(dxb2, 1, nxt)
        e2 = _shift_up(dxb2, 2, nxt)
        e3 = _shift_up(dxb2, 3, nxt)
        dxb2_head[...] = dxb2[:SUBLANES]
        dp_ref[:, 3 * d:4 * d] = (vrow(V_CONV_B_W + 3) * dxb2 + vrow(V_CONV_B_W + 2) * e1
                                  + vrow(V_CONV_B_W + 1) * e2 + vrow(V_CONV_B_W) * e3).astype(BF16)
        acc(SB2_DWB + 3, _colsum(x_b * dxb2))
        acc(SB2_DWB + 2, _colsum(x_b * e1))
        acc(SB2_DWB + 1, _colsum(x_b * e2))
        acc(SB2_DWB, _colsum(x_b * e3))
        acc(SB2_DBB, _colsum(dxb2))

        @pl.when(i == n_t - 1)
        def _():
            dgg_t = small_ref[SB2_DGT:SB2_DGT + 1, :]
            small_ref[SB2_DGT:SB2_DGT + 1, :] = dgg_t * g_post
            small_ref[SB2_DG_POST:SB2_DG_POST + 1, :] = dgg_t * gt
            lam = vrow(V_LAMBDA)
            small_ref[SB2_DLAM:SB2_DLAM + 1, :] = small_ref[SB2_DLAM:SB2_DLAM + 1, :] * (LRU_C * _sigmoid(-lam))

    rev = lambda i: (n_t - 1 - i, 0)
    tile = pl.BlockSpec((ts, d), rev)
    wide = pl.BlockSpec((ts, 7 * d), rev)
    sd = lambda dt: jax.ShapeDtypeStruct((s, d), dt)
    return _pcall(
        body, name=name, grid=(n_t,),
        out_shape=[jax.ShapeDtypeStruct((s, 7 * d), BF16), jax.ShapeDtypeStruct((3, s, d), BF16),
                   jax.ShapeDtypeStruct((2 * SUBLANES, d), F32), jax.ShapeDtypeStruct(wg.shape, F32)],
        in_specs=[tile, wide] + [tile] * 11 + [_full(mod.shape), _full(vec.shape),
                  _full(wg.shape), _full(w_a_out.shape), _full(w_b_out.shape), _full(w_o.shape)],
        out_specs=[wide, pl.BlockSpec((3, ts, d), lambda i: (0, n_t - 1 - i, 0)), _full((2 * SUBLANES, d)), _full(wg.shape)],
        scratch_shapes=[pltpu.VMEM((SUBLANES, d), F32)] * 4 + [pltpu.VMEM((d // LANES, ts, LANES), F32)] * 2
                       + [pltpu.VMEM((ts // SCAN_GROUP, d), F32)],
        args=[dx1, proj, conva, xb2s, hhs, pas, pbs, ys, rs_, gis, as_, mults, dgels, mod, vec, wg, w_a_out, w_b_out, w_o],
        carries=carries)


def _proj_backward(dproj, dx1, x, mod, vec, w_in, name, carries=()):
    s, d = x.shape
    nq = w_in.shape[2]
    ts = min(TOKENS_MATMUL_TILE, s)
    n_t = s // ts

    def body(dp_ref, dx1_ref, x_ref, mod_ref, vec_ref, w_ref, dx_ref, small_ref):
        i = pl.program_id(0)

        @pl.when(i == 0)
        def _():
            small_ref[...] = jnp.zeros_like(small_ref)

        dh = jnp.zeros((ts, d), F32)
        for qb in range(N_CHIP):
            dh = dh + _dot_tb(dp_ref[:, qb * nq:(qb + 1) * nq], w_ref[qb])
        xn, r = _rms(x_ref[...])
        g_pre = vec_ref[V_G_PRE_MIX:V_G_PRE_MIX + 1, :]
        sc1 = 1.0 + mod_ref[M_SC_M:M_SC_M + 1, :]
        dx_ref[...] = dx1_ref[...] + _rms_bwd(dh * (g_pre * sc1), xn, r)
        small_ref[SB1_DSH:SB1_DSH + 1, :] += _colsum(dh)
        small_ref[SB1_DSC:SB1_DSC + 1, :] += _colsum(dh * xn)

        @pl.when(i == n_t - 1)
        def _():
            dgm_t = small_ref[SB1_DSC:SB1_DSC + 1, :]
            small_ref[SB1_DSC:SB1_DSC + 1, :] = dgm_t * g_pre
            small_ref[SB1_DG_PRE:SB1_DG_PRE + 1, :] = dgm_t * sc1

    tile = pl.BlockSpec((ts, d), lambda i: (i, 0))
    return _pcall(
        body, name=name, grid=(n_t,),
        out_shape=[jax.ShapeDtypeStruct((s, d), F32), jax.ShapeDtypeStruct((SUBLANES, d), F32)],
        in_specs=[pl.BlockSpec((ts, N_CHIP * nq), lambda i: (i, 0)), tile, tile, _full(mod.shape), _full(vec.shape),
                  _full(w_in.shape)],
        out_specs=[tile, _full((SUBLANES, d))],
        args=[dproj, dx1, x, mod, vec, w_in], carries=carries)


def _weight_grad(a, b, name, col_blocks=1, tk=512, carries=()):
    s, k = a.shape
    n = b.shape[1]
    tn = n // col_blocks
    tk = min(tk, k)

    def body(a_ref, b_ref, o_ref):
        o_ref[0] = _dot_ta(a_ref[...], b_ref[...])

    (out,), carried = _pcall(
        body, name=name, grid=(col_blocks, k // tk),
        out_shape=[jax.ShapeDtypeStruct((col_blocks, k, tn), F32)],
        in_specs=[pl.BlockSpec((s, tk), lambda j, i: (0, i)), pl.BlockSpec((s, tn), lambda j, i: (0, j))],
        out_specs=[pl.BlockSpec((1, tk, tn), lambda j, i: (j, i, 0))],
        args=[a, b], carries=carries)
    return out, carried


def _weight_grad_stacked(a3, b3, name, tk=512, carries=()):
    n_g, s, k = a3.shape
    n = b3.shape[2]
    kq = k // N_CHIP
    tk = min(tk, k)
    chips_per_tile = tk // kq

    def body(a_ref, b_ref, o_ref):
        o_ref[...] = _dot_ta(a_ref[...], b_ref[...]).reshape(chips_per_tile, kq, n)

    (out,), carried = _pcall(
        body, name=name, grid=(n_g, k // tk),
        out_shape=[jax.ShapeDtypeStruct((N_CHIP, n_g, kq, n), F32)],
        in_specs=[pl.BlockSpec((None, s, tk), lambda g, i: (g, 0, i)), pl.BlockSpec((None, s, n), lambda g, i: (g, 0, 0))],
        out_specs=[pl.BlockSpec((chips_per_tile, None, kq, n), lambda g, i: (i, g, 0, 0))],
        args=[a3, b3], carries=carries)
    return out.reshape(N_CHIP, n_g * kq, n), carried


def _adamw(items, name, copy_grad=False, carries=()):
    shape = items[0][0].shape
    cols = shape[-1]
    rows = items[0][0].size // cols
    tr = _row_tile(rows, cols, target_bytes=1024 * 1024 // len(items))
    c1 = 1.0 - ADAM_B1 ** ADAM_STEP
    c2 = 1.0 - ADAM_B2 ** ADAM_STEP
    n_out = 4 if copy_grad else 3
    n = len(items)

    def body(*refs):
        for k in range(n):
            w_ref, g_ref, m_ref, v_ref = refs[4 * k:4 * k + 4]
            outs = refs[4 * n + n_out * k:4 * n + n_out * (k + 1)]
            gv = g_ref[...]
            nm = ADAM_B1 * m_ref[...] + (1.0 - ADAM_B1) * gv
            nv = ADAM_B2 * v_ref[...] + (1.0 - ADAM_B2) * (gv * gv)
            outs[0][...] = (-ADAM_LR) * ((nm / c1) / (jnp.sqrt(nv / c2) + ADAM_EPS) + ADAM_WD * w_ref[...])
            outs[1][...] = nm
            outs[2][...] = nv
            if copy_grad:
                outs[3][...] = gv

    spec = pl.BlockSpec((tr, cols), lambda i: (i, 0))
    outs, carried = _pcall(
        body, name=name, grid=(rows // tr,),
        out_shape=[jax.ShapeDtypeStruct((rows, cols), F32)] * (n_out * n),
        in_specs=[spec] * (4 * n), out_specs=[spec] * (n_out * n),
        args=[t.reshape(rows, cols) for item in items for t in item], carries=carries)
    return [tuple(o.reshape(shape) for o in outs[n_out * k:n_out * (k + 1)]) for k in range(n)], carried


def kernel(x, c, w_mod, b_mod, g_pre_mix, g_post_mix, w_in, conv_a_w, conv_a_b, w_a_out, conv_b_w, conv_b_b, w_gate_r, b_gate_r, w_gate_i, b_gate_i, lru_lambda, w_b_out, w_o, g_pre_mlp, g_post_mlp, w_mlp_up, w_mlp_down, loss_target, m_w_mod, m_b_mod, m_g_pre_mix, m_g_post_mix, m_w_in, m_conv_a_w, m_conv_a_b, m_w_a_out, m_conv_b_w, m_conv_b_b, m_w_gate_r, m_b_gate_r, m_w_gate_i, m_b_gate_i, m_lru_lambda, m_w_b_out, m_w_o, m_g_pre_mlp, m_g_post_mlp, m_w_mlp_up, m_w_mlp_down, v_w_mod, v_b_mod, v_g_pre_mix, v_g_post_mix, v_w_in, v_conv_a_w, v_conv_a_b, v_w_a_out, v_conv_b_w, v_conv_b_b, v_w_gate_r, v_b_gate_r, v_w_gate_i, v_b_gate_i, v_lru_lambda, v_w_b_out, v_w_o, v_g_pre_mlp, v_g_post_mlp, v_w_mlp_up, v_w_mlp_down):
    weights = dict(w_mod=w_mod, b_mod=b_mod, g_pre_mix=g_pre_mix, g_post_mix=g_post_mix, w_in=w_in, conv_a_w=conv_a_w,
                   conv_a_b=conv_a_b, w_a_out=w_a_out, conv_b_w=conv_b_w, conv_b_b=conv_b_b, w_gate_r=w_gate_r,
                   b_gate_r=b_gate_r, w_gate_i=w_gate_i, b_gate_i=b_gate_i, lru_lambda=lru_lambda, w_b_out=w_b_out,
                   w_o=w_o, g_pre_mlp=g_pre_mlp, g_post_mlp=g_post_mlp, w_mlp_up=w_mlp_up, w_mlp_down=w_mlp_down)
    mom1 = dict(w_mod=m_w_mod, b_mod=m_b_mod, g_pre_mix=m_g_pre_mix, g_post_mix=m_g_post_mix, w_in=m_w_in,
                conv_a_w=m_conv_a_w, conv_a_b=m_conv_a_b, w_a_out=m_w_a_out, conv_b_w=m_conv_b_w, conv_b_b=m_conv_b_b,
                w_gate_r=m_w_gate_r, b_gate_r=m_b_gate_r, w_gate_i=m_w_gate_i, b_gate_i=m_b_gate_i,
                lru_lambda=m_lru_lambda, w_b_out=m_w_b_out, w_o=m_w_o, g_pre_mlp=m_g_pre_mlp, g_post_mlp=m_g_post_mlp,
                w_mlp_up=m_w_mlp_up, w_mlp_down=m_w_mlp_down)
    mom2 = dict(w_mod=v_w_mod, b_mod=v_b_mod, g_pre_mix=v_g_pre_mix, g_post_mix=v_g_post_mix, w_in=v_w_in,
                conv_a_w=v_conv_a_w, conv_a_b=v_conv_a_b, w_a_out=v_w_a_out, conv_b_w=v_conv_b_w, conv_b_b=v_conv_b_b,
                w_gate_r=v_w_gate_r, b_gate_r=v_b_gate_r, w_gate_i=v_w_gate_i, b_gate_i=v_b_gate_i,
                lru_lambda=v_lru_lambda, w_b_out=v_w_b_out, w_o=v_w_o, g_pre_mlp=v_g_pre_mlp, g_post_mlp=v_g_post_mlp,
                w_mlp_up=v_w_mlp_up, w_mlp_down=v_w_mlp_down)
    names = list(weights)

    n_layer = w_in.shape[0]
    s, d = x.shape[1], x.shape[2]
    n_head, bw = w_gate_r.shape[1], w_gate_r.shape[2]
    dq = d // N_CHIP
    mq = w_mod.shape[2]
    n_mod = (N_CHIP * mq) // d
    ka, kb = conv_a_w.shape[1], conv_b_w.shape[1]

    mx, my, mc = _place()
    q_me = 2 * mx + my
    q_arr = jnp.reshape(q_me, (1,)).astype(jnp.int32)

    me_dev = 4 * mx + 2 * my + mc
    me_arr = jnp.reshape(me_dev, (1,)).astype(jnp.int32)

    big_names = ["w_in", "w_a_out", "w_b_out", "w_o", "w_mlp_up", "w_mlp_down"]
    groups = [["w_in"], ["w_a_out", "w_b_out", "w_o"], ["w_mlp_up", "w_mlp_down"]]
    placed = {("w_in", 0): _cast_place(w_in, 0, q_arr, "cast_place_w_in_0")}
    wfull = [dict() for _ in range(n_layer)]
    riders = {}
    for l in range(n_layer):
        riders.setdefault(3 * l - 1, []).append(([("w_in", l)], 0.9 if l else 1.0))
        riders.setdefault(3 * l - 2 if l else 0, []).append(([(nm, l) for nm in groups[1]], 0.9 if l else 0.5))
        riders.setdefault(3 * l, []).append(([("w_mlp_up", l)], 0.7 if l else 0.9))
        riders.setdefault(3 * l + 1, []).insert(0, ([("w_mlp_down", l)], 0.5))

    def gather_carry(call):
        return [_gather_carry([placed[k] for k in keys], frac) for keys, frac in riders.get(call, [])]

    def gathered(call, carried):
        for (keys, _), ws in zip(riders.get(call, []), carried):
            for (nm, l), w in zip(keys, ws):
                wfull[l][nm] = w.reshape(d, d) if nm in groups[1] else w

    n_conv_rows = n_layer * (ka + kb)
    conv_blk = -(-n_conv_rows // SUBLANES) * SUBLANES
    blk_rows = SUBLANES + conv_blk
    conv_rows = jnp.concatenate([jnp.concatenate([conv_a_w[l], conv_b_w[l]], axis=0) for l in range(n_layer)], axis=0)
    conv_rows = jnp.pad(conv_rows, ((0, conv_blk - n_conv_rows), (0, d - dq)))
    c_conv = jnp.concatenate([jnp.pad(c, ((0, SUBLANES - 1), (0, 0))), conv_rows], axis=0)
    rest = [(nm, l) for l in range(n_layer) for nm in big_names if (nm, l) != ("w_in", 0)]
    rest_placed, carried = _cast_place_all([(weights[nm], l) for nm, l in rest], q_arr, "cast_place_rest",
                                           carries=gather_carry(-1) + [_allgather_carry([c_conv])])
    placed.update(zip(rest, rest_placed))
    gathered(-1, carried[:1])
    gathered1 = lax.dynamic_update_slice(carried[1][0], c_conv, (me_dev * blk_rows, 0)).reshape(N_DEV, blk_rows, d)
    c_all = gathered1[:, 0, :]
    conv_full = jnp.concatenate([gathered1[2 * qb, SUBLANES:SUBLANES + n_conv_rows, :dq] for qb in range(N_CHIP)], axis=1)

    b_mod_shard = lax.dynamic_slice_in_dim(b_mod, q_me * mq, mq, axis=1)
    mod_part = _mod_forward(c_all, w_mod, b_mod_shard, "mod_forward")
    gathered2 = _all_gather_small(mod_part, "gather_mod").reshape(N_DEV, n_layer, N_DEV, mq)
    mod_rows = jnp.concatenate(
        [lax.dynamic_index_in_dim(gathered2[2 * qb], me_dev, axis=1, keepdims=False) for qb in range(N_CHIP)], axis=1)
    mods = [jnp.pad(mod_rows[l].reshape(n_mod, d), ((0, SUBLANES - n_mod), (0, 0))) for l in range(n_layer)]

    vecs = []
    for l in range(n_layer):
        base = l * (ka + kb)
        rows = [g_pre_mix[l], g_post_mix[l], conv_a_b[l], conv_b_b[l], b_gate_r[l], b_gate_i[l], lru_lambda[l],
                g_pre_mlp[l], g_post_mlp[l]]
        vecs.append(jnp.concatenate([jnp.stack(rows, axis=0), conv_full[base:base + ka + kb]], axis=0))

    wgs =[jnp.concatenate([w_gate_r[l], w_gate_i[l]], axis=-1).astype(BF16) for l in range(n_layer)]

    xs = x[0]
    saved = []
    for l in range(n_layer):
        wl = wfull[l]
        (h, proj, dgel), carried = _norm_proj(xs, mods[l], vecs[l], wl["w_in"], f"norm_proj_{l}", gather_carry(3 * l))
        gathered(3 * l, carried)
        (x1, conva, xb2, hh, abm, pa, pb, yy, gr, ggi, ga, gmult), carried = _mixer_forward(
            xs, proj, mods[l], vecs[l], wgs[l], wl["w_a_out"], wl["w_b_out"], wl["w_o"], f"mixer_forward_{l}",
            gather_carry(3 * l + 1))
        gathered(3 * l + 1, carried)
        (x2, h2, up, y2, *loss_tile), carried = _mlp_forward(
            x1, mods[l], vecs[l], wl["w_mlp_up"], wl["w_mlp_down"], f"mlp_forward_{l}", gather_carry(3 * l + 2),
            target=loss_target[0] if l == n_layer - 1 else None)
        gathered(3 * l + 2, carried)
        saved.append(dict(x=xs, h=h, proj=proj, x1=x1, conva=conva, xb2=xb2, hh=hh, abm=abm, pa=pa, pb=pb,
                          y=yy, r=gr, gi=ggi, a=ga, mult=gmult, dgel=dgel, h2=h2, up=up, y2=y2))
        xs = x2
    dxs = xs
    loss_block = jnp.pad(loss_tile[0], ((0, 0), (0, d - LANES)))

    chips_q = [q_me ^ 2, q_me ^ 1, q_me ^ 3]
    pf = jnp.stack([mc, q_me] + chips_q).astype(jnp.int32)
    rs = dict(grad={}, landed={}, to_send={}, from_chips={}, out={})
    to_exchange, to_scatter, to_join, to_gather = [], [], [], []
    small_own, small_all = {}, {}

    def ride(call, what, name=None):
        ex = list(to_exchange) if "x" in what else []
        sc = list(to_scatter) if "s" in what else []
        ga = list(to_gather) if "g" in what else []
        jn = []
        for key in (to_join if "j" in what else []):
            if key[0] not in [k[0] for k in jn]:
                jn.append(key)
        carries = []
        if ex:
            carries.append(_exchange_carry([rs["grad"][k] for k in ex]))
        if sc:
            carries.append(_scatter_carry([rs["to_send"][k] for k in sc]))
        if jn:
            carries.append(_join_carry([rs["out"][k[0]] for k in jn], [k[1] for k in jn]))
        if ga:
            carries.append(_allgather_carry([small_own[k] for k in ga]))
        if call is None:
            carried = _run_carries(carries, name) if carries else []
            res = None
        else:
            res, carried = call(carries)
        carried = list(carried)
        if ex:
            for k, ld in zip(ex, carried.pop(0)):
                to_exchange.remove(k)
                rs["landed"][k] = ld
                rs["to_send"][k] = _add_sibling_half(rs["grad"][k], ld, pf, f"rs_add_sibling_{k[0]}_{k[1]}")
                to_scatter.append(k)
        if sc:
            for k, fc in zip(sc, carried.pop(0)):
                to_scatter.remove(k)
                rs["out"][k[0]] = _add_chips(rs["grad"][k], rs["landed"][k], fc, pf, rs["out"].get(k[0]), k[1], n_layer,
                                             f"rs_add_chips_{k[0]}_{k[1]}")
                to_join.append(k)
        if jn:
            for k, o in zip(jn, carried.pop(0)):
                to_join.remove(k)
                rs["out"][k[0]] = o
        if ga:
            for k, o in zip(ga, carried.pop(0)):
                to_gather.remove(k)
                small_all[k] = o
        return res

    def gather_small(key, parts):
        small_own[key] = parts[0] if len(parts) == 1 else jnp.concatenate(parts, axis=0)
        to_gather.append(key)

    def ready(nm, l, g):
        rs["grad"][(nm, l)] = g
        to_exchange.append((nm, l))

    rowblk = lambda t: t.reshape(N_CHIP, t.shape[1] // N_CHIP, t.shape[2])
    small1_prev = None
    for l in reversed(range(n_layer)):
        wl, sv = wfull[l], saved[l]
        dx1, dy2, dup, act, small3 = ride(lambda cr: _mlp_backward(
            dxs, sv["x1"], sv["y2"], sv["up"], mods[l], vecs[l], wl["w_mlp_up"], wl["w_mlp_down"], f"mlp_backward_{l}", cr), "xsjg")
        ready("w_mlp_up", l, _weight_grad(sv["h2"], dup, f"grad_w_mlp_up_{l}", col_blocks=N_CHIP)[0])
        g_down = ride(lambda cr: _weight_grad(act, dy2, f"grad_w_mlp_down_{l}", carries=cr), "x")
        ready("w_mlp_down", l, rowblk(g_down))
        dproj, dab, small2, dwg = ride(lambda cr: _mixer_backward(
            dx1, sv["proj"], sv["conva"], sv["xb2"], sv["hh"], sv["pa"], sv["pb"], sv["y"],
            sv["r"], sv["gi"], sv["a"], sv["mult"], sv["dgel"], mods[l], vecs[l], wgs[l],
            wl["w_a_out"], wl["w_b_out"], wl["w_o"], f"mixer_backward_{l}", cr), "xsjg")
        gather_small(("late", l, "s"), ([small1_prev] if small1_prev is not None else []) + [small2, small3])
        gather_small(("late", l, "w"), [dwg.reshape(2 * bw, d).astype(BF16)])
        g_in = ride(lambda cr: _weight_grad(sv["h"], dproj, f"grad_w_in_{l}", col_blocks=N_CHIP, carries=cr), "xsj")
        ready("w_in", l, g_in)
        g_abo = ride(lambda cr: _weight_grad_stacked(sv["abm"], dab, f"grad_w_abo_{l}", carries=cr), "xg")
        ready("w_abo", l, g_abo)
        dxs, small1_prev = ride(lambda cr: _proj_backward(dproj, dx1, sv["x"], mods[l], vecs[l], wl["w_in"],
                                                          f"proj_backward_{l}", cr), "xsjg")
    grad_x = dxs[None]
    gather_small(("last", 0, "s"), [small1_prev, loss_block])

    tail = 0
    while to_exchange or to_scatter or to_join or to_gather:
        ride(None, "xsjg", f"rs_tail_{tail}")
        tail += 1
    grads, deltas, new_m, new_v = {}, {}, {}, {}

    def adam(nms, copy_grad=False):
        items = [(weights[nm], grads[nm], mom1[nm], mom2[nm]) for nm in nms]
        res, _ = _adamw(items, "adamw_" + "_".join(nms), copy_grad)
        for nm, r in zip(nms, res):
            deltas[nm], new_m[nm], new_v[nm] = r[:3]
            if copy_grad:
                grads[nm] = r[3]

    for nms in (["w_mlp_up", "w_mlp_down"], ["w_in"]):
        for nm in nms:
            grads[nm] = rs["out"][nm].reshape(weights[nm].shape)
        adam(nms, True)

    sums ={k: _sum_devices(small_all[k], small_own[k], me_arr, f"sum_small_{k[0]}_{k[1]}_{k[2]}") for k in small_own}

    loss = sums[("last", 0, "s")][SUBLANES, 0]
    small_full = {}

    def rows_of(l, part):
        if part == 0:
            return (("late", l - 1, "s"), 0) if l >= 1 else (("last", 0, "s"), 0)
        if part == 3:
            return ("late", l, "w"), 0
        base = SUBLANES if l < n_layer - 1 else 0
        return ("late", l, "s"), base + (0, 0, 2 * SUBLANES)[part]

    def summed(l, part, row, n_rows=1):
        key, base = rows_of(l, part)
        return sums[key][base + row:base + row + n_rows]

    def per_device(l, part, row):
        key, base = rows_of(l, part)
        own = small_own[key]
        if key not in small_full:
            small_full[key] = lax.dynamic_update_slice(small_all[key], own, (me_dev * own.shape[0], 0)).reshape(
                (N_DEV,) + own.shape)
        return small_full[key][:, base + row:base + row + 1]

    mod_rows = [(0, SB1_DSH), (0, SB1_DSC), (1, SB2_DGT), (2, SB3_DSH), (2, SB3_DSC), (2, SB3_DGT)]
    dmod_all = jnp.stack([jnp.concatenate([per_device(l, p, r)[:, 0, :] for p, r in mod_rows], axis=1)
                          for l in range(n_layer)], axis=0)
    o1, o2, o3, o4 = 0, SUBLANES, 3 * SUBLANES, 4 * SUBLANES
    small_sum = jnp.stack([jnp.concatenate([summed(l, 0, 0, SUBLANES), summed(l, 1, 0, 2 * SUBLANES),
                                            summed(l, 2, 0, SUBLANES), summed(l, 3, 0, 2 * bw)], axis=0)
                           for l in range(n_layer)], axis=0)
    mod_rows_of = [o1 + SB1_DSH, o1 + SB1_DSC, o2 + SB2_DGT, o3 + SB3_DSH, o3 + SB3_DSC, o3 + SB3_DGT]
    grads["w_mod"] = _mod_backward(c_all.T, lax.dynamic_slice_in_dim(dmod_all, q_me * mq, mq, axis=2), "mod_backward")
    grads["b_mod"] = jnp.concatenate([small_sum[:, k, :] for k in mod_rows_of], axis=1)
    grads["g_pre_mix"] = small_sum[:, o1 + SB1_DG_PRE]
    grads["g_post_mix"] = small_sum[:, o2 + SB2_DG_POST]
    grads["conv_a_w"] = lax.dynamic_slice_in_dim(small_sum[:, o2 + SB2_DWA:o2 + SB2_DWA + ka], q_me * dq, dq, axis=2)
    grads["conv_a_b"] = small_sum[:, o2 + SB2_DBA]
    grads["conv_b_w"] = lax.dynamic_slice_in_dim(small_sum[:, o2 + SB2_DWB:o2 + SB2_DWB + kb], q_me * dq, dq, axis=2)
    grads["conv_b_b"] = small_sum[:, o2 + SB2_DBB]
    grads["lru_lambda"] = small_sum[:, o2 + SB2_DLAM]
    grads["b_gate_r"] = small_sum[:, o2 + SB2_DBR]
    grads["b_gate_i"] = small_sum[:, o2 + SB2_DBI]
    grads["g_pre_mlp"] = small_sum[:, o3 + SB3_DG_PRE]
    grads["g_post_mlp"] = small_sum[:, o3 + SB3_DG_POST]
    dwg_sum = small_sum[:, o4:].reshape(n_layer, n_head, bw, 2 * bw)
    grads["w_gate_r"] = dwg_sum[..., :bw]
    grads["w_gate_i"] = dwg_sum[..., bw:]

    for k, nm in enumerate(groups[1]):
        grads[nm] = rs["out"]["w_abo"][:, k * dq:(k + 1) * dq]

    by_shape = {}
    for nm in names:
        if nm not in deltas:
            by_shape.setdefault(weights[nm].shape, []).append(nm)
    for nms in by_shape.values():
        adam(nms)
    return (loss, grad_x, *[grads[nm] for nm in names], *[deltas[nm] for nm in names],
            *[new_m[nm] for nm in names], *[new_v[nm] for nm in names])
```

```python
import jax
import jax.numpy as jnp
from jax import lax
from jax.experimental import pallas as pl
from jax.experimental.pallas import tpu as pltpu

F32 = jnp.float32
BF16 = jnp.bfloat16
MESH = pl.DeviceIdType.MESH

EPS = 1e-6
LRU_C = 8.0
N_CHIP = 4
N_DEV = 8
ADAM_LR = 0.001
ADAM_B1 = 0.9
ADAM_B2 = 0.999
ADAM_EPS = 1e-08
ADAM_WD = 0.01
ADAM_STEP = 10

VMEM_LIMIT_BYTES = 56 * 1024 * 1024
SUBLANES = 8
LANES = 128
TOKENS_MATMUL_TILE = 512
TOKENS_MIXER_TILE = 256
GELU_K0 = 0.7978845608028654
GELU_K1 = 0.044715

V_G_PRE_MIX, V_G_POST_MIX, V_CONV_A_B, V_CONV_B_B, V_B_GATE_R, V_B_GATE_I, V_LAMBDA, V_G_PRE_MLP, V_G_POST_MLP = range(9)
V_CONV_A_W = 9
V_CONV_B_W = 12
M_SH_M, M_SC_M, M_GT_M, M_SH_F, M_SC_F, M_GT_F = range(6)


def _cparams(n_grid=0):
    sem = ("arbitrary",) * n_grid if n_grid else None
    return pltpu.CompilerParams(dimension_semantics=sem, vmem_limit_bytes=VMEM_LIMIT_BYTES)


def _full(shape):
    return pl.BlockSpec(shape, lambda *_: (0,) * len(shape))


def _dot(a, b):
    return jnp.dot(a, b, preferred_element_type=F32)


def _dot_tb(a, b):
    return lax.dot_general(a, b, (((1,), (1,)), ((), ())), preferred_element_type=F32)


def _dot_ta(a, b):
    return lax.dot_general(a, b, (((0,), (0,)), ((), ())), preferred_element_type=F32)


def _sigmoid(x):
    return 1.0 / (1.0 + jnp.exp(-x))


def _softplus(x):
    return jnp.maximum(x, 0.0) + jnp.log1p(jnp.exp(-jnp.abs(x)))


def _neg_expm1(x):
    series = -x * (1.0 + 0.5 * x * (1.0 + (x / 3.0) * (1.0 + 0.25 * x)))
    return jnp.where(x > -1e-2, series, 1.0 - jnp.exp(x))


def _gelu_and_grad(x):
    x2 = x * x
    s = _sigmoid(x * (2.0 * GELU_K0 + (2.0 * GELU_K0 * GELU_K1) * x2))
    gel = x * s
    return gel, s + gel * (1.0 - s) * (2.0 * GELU_K0 + (6.0 * GELU_K0 * GELU_K1) * x2)


def _rms(x):
    r = lax.rsqrt(jnp.mean(x * x, axis=-1, keepdims=True) + EPS)
    return x * r, r


def _rms_bwd(dxn, xn, r):
    return r * (dxn - xn * jnp.mean(dxn * xn, axis=-1, keepdims=True))


def _colsum(x):
    return jnp.sum(x, axis=0, keepdims=True)


def _rows(t, w):
    return lax.broadcasted_iota(jnp.int32, (t, w), 0)


def _shift_down(x, k, prev8):
    t, w = x.shape
    rolled = pltpu.roll(x, k, 0)
    head = jnp.where(_rows(SUBLANES, w) < k, pltpu.roll(prev8, k, 0), rolled[:SUBLANES])
    return jnp.concatenate([head, rolled[SUBLANES:]], axis=0)


def _shift_up(x, k, next8):
    t, w = x.shape
    rolled = pltpu.roll(x, t - k, 0)
    tail = jnp.where(_rows(SUBLANES, w) >= SUBLANES - k, pltpu.roll(next8, SUBLANES - k, 0), rolled[t - SUBLANES:])
    return jnp.concatenate([rolled[:t - SUBLANES], tail], axis=0)


SCAN_GROUP = 16


def _scan_steps(a, b, group, reverse):
    t, w = a.shape
    pos = _rows(t, w) & (group - 1)
    s = 1
    while s < group:
        keep = (pos < group - s) if reverse else (pos >= s)
        shift = (t - s) if reverse else s
        b = b + a * jnp.where(keep, pltpu.roll(b, shift, 0), 0.0)
        a = a * jnp.where(keep, pltpu.roll(a, shift, 0), 1.0)
        s *= 2
    return b, a


def _scan_two_level(a, b, carry_row, a_buf, b_buf, c_buf, reverse):
    t, w = a.shape
    grp = SCAN_GROUP
    n_grp = t // grp
    h_loc, a_cum = _scan_steps(a, b, grp, reverse)
    end = 0 if reverse else grp - 1
    a_end, h_end = [], []
    for j in range(w // LANES):
        a_buf[j] = a_cum[:, j * LANES:(j + 1) * LANES]
        b_buf[j] = h_loc[:, j * LANES:(j + 1) * LANES]
        a_end.append(a_buf[j, pl.ds(end, n_grp, stride=grp), :])
        h_end.append(b_buf[j, pl.ds(end, n_grp, stride=grp), :])
    a_end = jnp.concatenate(a_end, axis=1)
    h_end = jnp.concatenate(h_end, axis=1)
    h_grp, a_grp = _scan_steps(a_end, h_end, n_grp, reverse)
    h_grp = h_grp + a_grp * carry_row
    rows = _rows(n_grp, w)
    if reverse:
        entering = jnp.where(rows == n_grp - 1, carry_row, pltpu.roll(h_grp, n_grp - 1, 0))
    else:
        entering = jnp.where(rows == 0, carry_row, pltpu.roll(h_grp, 1, 0))
    c_buf[...] = entering
    out = [h_loc[g * grp:(g + 1) * grp] + a_cum[g * grp:(g + 1) * grp] * c_buf[g:g + 1, :] for g in range(n_grp)]
    return jnp.concatenate(out, axis=0)


def _row_tile(rows, cols, itemsize=4, target_bytes=2 * 1024 * 1024):
    if rows * cols * itemsize <= target_bytes or rows % SUBLANES:
        return rows
    t = max(SUBLANES, (target_bytes // (cols * itemsize)) // SUBLANES * SUBLANES)
    while rows % t:
        t -= SUBLANES
    return t


def _place():
    return lax.axis_index("x"), lax.axis_index("y"), lax.axis_index("c")


def _other_chips(x, y):
    chips = [(1 - x, y), (x, 1 - y), (1 - x, 1 - y)]
    return chips, [2 * cx + cy for cx, cy in chips]


def _all_gather_small(block, name):
    m_per, n = block.shape

    def body(x_ref, out_ref, send_sems, recv_sems, local_sem):
        x, y, c = _place()
        me, sibling = (x, y, c), (x, y, 1 - c)
        chips, _ = _other_chips(x, y)

        def rows(px, py, pc):
            return out_ref.at[pl.ds((4 * px + 2 * py + pc) * m_per, m_per), :]

        def copy(k, blk, to, src=None):
            return pltpu.make_async_remote_copy(
                src_ref=rows(*blk) if src is None else src, dst_ref=rows(*blk),
                send_sem=send_sems.at[k], recv_sem=recv_sems.at[k], device_id=to, device_id_type=MESH)

        mine = pltpu.make_async_copy(x_ref, rows(*me), local_sem)
        mine.start()
        first = [copy(0, me, sibling, src=x_ref)]
        first += [copy(1 + j, me, (*chip, c), src=x_ref) for j, chip in enumerate(chips)]
        for cp in first:
            cp.start()
        passed = [copy(4 + j, (*chip, c), sibling) for j, chip in enumerate(chips)]
        for j, chip in enumerate(chips):
            copy(1 + j, (*chip, c), me).wait_recv()
            passed[j].start()
        copy(0, sibling, me).wait_recv()
        for j, chip in enumerate(chips):
            copy(4 + j, (*chip, 1 - c), me).wait_recv()
        for cp in first + passed:
            cp.wait_send()
        mine.wait()

    return pl.pallas_call(
        body, name=name,
        out_shape=jax.ShapeDtypeStruct((N_DEV * m_per, n), block.dtype),
        in_specs=[pl.BlockSpec(memory_space=pltpu.VMEM)],
        out_specs=pl.BlockSpec(memory_space=pltpu.VMEM),
        scratch_shapes=[pltpu.SemaphoreType.DMA((7,)), pltpu.SemaphoreType.DMA((7,)), pltpu.SemaphoreType.DMA],
        compiler_params=pltpu.CompilerParams(vmem_limit_bytes=VMEM_LIMIT_BYTES),
    )(block)


class _Carry:
    def __init__(self, ins, out_shapes, aliases, sem_shapes, start, finish, mid=None, mid_frac=0.85):
        self.ins, self.out_shapes, self.aliases, self.sem_shapes = list(ins), list(out_shapes), dict(aliases), list(sem_shapes)
        self.start, self.mid, self.finish, self.mid_frac = start, mid, finish, mid_frac


def _pcall(body, *, name, grid, in_specs, out_specs, out_shape, args, scratch_shapes=(), carries=(), prefetch=()):
    in_specs, out_specs, out_shape = list(in_specs), list(out_specs), list(out_shape)
    scratch_shapes, args = list(scratch_shapes), list(args)
    n_in, n_out, n_scr, n_pre = len(in_specs), len(out_shape), len(scratch_shapes), len(prefetch)
    steps = 1
    for g in grid:
        steps *= g
    any_spec = pl.BlockSpec(memory_space=pl.ANY)
    aliases = {}
    spans = []
    for cr in carries:
        spans.append((len(args), len(out_shape), len(scratch_shapes)))
        for a, b in cr.aliases.items():
            aliases[n_pre + len(args) + a] = len(out_shape) + b
        args += cr.ins
        in_specs += [any_spec] * len(cr.ins)
        out_shape += cr.out_shapes
        out_specs += [any_spec] * len(cr.out_shapes)
        scratch_shapes += cr.sem_shapes
    n_all_in = len(args)
    n_all_out = len(out_shape)

    def wrapped(*refs):
        pre, refs = refs[:n_pre], refs[n_pre:]
        ins, outs, scr = refs[:n_all_in], refs[n_all_in:n_all_in + n_all_out], refs[n_all_in + n_all_out:]
        parts = [(cr, ins[a:a + len(cr.ins)], outs[b:b + len(cr.out_shapes)], scr[s:s + len(cr.sem_shapes)])
                 for cr, (a, b, s) in zip(carries, spans)]
        lin = 0
        for ax, g in enumerate(grid):
            lin = lin * g + pl.program_id(ax)

        def at(step, fn):
            if steps == 1:
                fn()
            else:
                pl.when(lin == step)(fn)

        def start_all():
            for cr, ci, co, cs in parts:
                cr.start(ci, co, cs)

        def finish_all():
            for cr, ci, co, cs in parts:
                cr.finish(ci, co, cs)

        if parts:
            at(0, start_all)
        body(*pre, *ins[:n_in], *outs[:n_out], *scr[:n_scr])
        for cr, ci, co, cs in parts:
            if cr.mid is not None:
                at(min(steps - 1, int(steps * cr.mid_frac)), lambda cr=cr, ci=ci, co=co, cs=cs: cr.mid(ci, co, cs))
        if parts:
            at(steps - 1, finish_all)

    if n_pre:
        res = pl.pallas_call(
            wrapped, name=name, out_shape=out_shape,
            grid_spec=pltpu.PrefetchScalarGridSpec(num_scalar_prefetch=n_pre, grid=tuple(grid), in_specs=in_specs,
                                                   out_specs=out_specs, scratch_shapes=scratch_shapes),
            input_output_aliases=aliases, compiler_params=_cparams(len(grid)),
        )(*prefetch, *args)
    else:
        res = pl.pallas_call(
            wrapped, name=name, grid=tuple(grid), out_shape=out_shape, in_specs=in_specs, out_specs=out_specs,
            scratch_shapes=scratch_shapes, input_output_aliases=aliases, compiler_params=_cparams(len(grid)),
        )(*args)
    res = list(res)
    return res[:n_out], [res[b:b + len(cr.out_shapes)] for cr, (_, b, _) in zip(carries, spans)]


def _run_carries(carries, name):
    return _pcall(lambda: None, name=name, grid=(), in_specs=[], out_specs=[], out_shape=[], args=[], carries=carries)[1]


CAST_STEPS = 8


def _cast_place_all(shards, q_arr, name, carries=()):
    n = len(shards)

    def body(q_ref, *refs):
        for k in range(n):
            refs[n + k][...] = refs[k][...].astype(BF16)

    def spec_in(k):
        w, layer = shards[k]
        return pl.BlockSpec((1, w.shape[1] // CAST_STEPS, w.shape[2]), lambda i, q_ref: (layer, i, 0))

    def spec_out(k):
        w, _ = shards[k]
        return pl.BlockSpec((1, w.shape[1] // CAST_STEPS, w.shape[2]), lambda i, q_ref: (q_ref[0], i, 0))

    return _pcall(
        body, name=name, grid=(CAST_STEPS,),
        out_shape=[jax.ShapeDtypeStruct((N_CHIP,) + w.shape[1:], BF16) for w, _ in shards],
        in_specs=[spec_in(k) for k in range(n)], out_specs=[spec_out(k) for k in range(n)],
        args=[w for w, _ in shards], carries=carries, prefetch=[q_arr])


def _gather_carry(bufs, mid_frac=0.85):
    n = len(bufs)

    def copies(o_refs, sems):
        send_sems, recv_sems = sems
        x, y, c = _place()
        q = 2 * x + y
        sibling = (x, y, 1 - c)
        chips, qs = _other_chips(x, y)

        def half(w, shard, pc):
            rh = bufs[w].shape[1] // 2
            return o_refs[w].at[shard, pl.ds(pc * rh, rh), :]

        def over_ici(w, j, shard):
            blk = half(w, shard, c)
            return pltpu.make_async_remote_copy(
                src_ref=blk, dst_ref=blk, send_sem=send_sems.at[w, j], recv_sem=recv_sems.at[w, j],
                device_id=(*chips[j], c), device_id_type=MESH)

        def to_sibling(w, j, pc):
            blk = half(w, qs[j], pc)
            return pltpu.make_async_remote_copy(
                src_ref=blk, dst_ref=blk, send_sem=send_sems.at[w, 3 + j], recv_sem=recv_sems.at[w, 3 + j],
                device_id=sibling, device_id_type=MESH)

        return q, c, qs, over_ici, to_sibling

    pairs = [(w, j) for w in range(n) for j in range(3)]

    def start(i_refs, o_refs, sems):
        q, _, _, over_ici, _ = copies(o_refs, sems)
        for w, j in pairs:
            over_ici(w, j, q).start()

    def mid(i_refs, o_refs, sems):
        _, c, qs, over_ici, to_sibling = copies(o_refs, sems)
        for w, j in pairs:
            over_ici(w, j, qs[j]).wait_recv()
            to_sibling(w, j, c).start()

    def finish(i_refs, o_refs, sems):
        q, c, _, over_ici, to_sibling = copies(o_refs, sems)
        for w, j in pairs:
            to_sibling(w, j, 1 - c).wait_recv()
        for w, j in pairs:
            over_ici(w, j, q).wait_send()
            to_sibling(w, j, c).wait_send()

    return _Carry(bufs, [jax.ShapeDtypeStruct(b.shape, b.dtype) for b in bufs], {w: w for w in range(n)},
                  [pltpu.SemaphoreType.DMA((n, 6)), pltpu.SemaphoreType.DMA((n, 6))], start, finish, mid, mid_frac)


def _exchange_carry(grads):
    n = len(grads)

    def copies(g_refs, l_refs, sems):
        send_sems, recv_sems = sems
        x, y, c = _place()
        out = []
        for w in range(n):
            rh = grads[w].shape[1] // 2
            out.append(pltpu.make_async_remote_copy(
                src_ref=g_refs[w].at[:, pl.ds((1 - c) * rh, rh), :], dst_ref=l_refs[w],
                send_sem=send_sems.at[w], recv_sem=recv_sems.at[w], device_id=(x, y, 1 - c), device_id_type=MESH))
        return out

    def start(g_refs, l_refs, sems):
        for cp in copies(g_refs, l_refs, sems):
            cp.start()

    def finish(g_refs, l_refs, sems):
        for cp in copies(g_refs, l_refs, sems):
            cp.wait()

    return _Carry(grads, [jax.ShapeDtypeStruct((N_CHIP, g.shape[1] // 2, g.shape[2]), g.dtype) for g in grads], {},
                  [pltpu.SemaphoreType.DMA((n,)), pltpu.SemaphoreType.DMA((n,))], start, finish)


def _scatter_carry(sums):
    n = len(sums)

    def copies(s_refs, l_refs, sems):
        send_sems, recv_sems = sems
        x, y, c = _place()
        chips, _ = _other_chips(x, y)
        return [pltpu.make_async_remote_copy(
            src_ref=s_refs[w].at[j], dst_ref=l_refs[w].at[j], send_sem=send_sems.at[w, j], recv_sem=recv_sems.at[w, j],
            device_id=(*chips[j], c), device_id_type=MESH) for w in range(n) for j in range(3)]

    def start(s_refs, l_refs, sems):
        for cp in copies(s_refs, l_refs, sems):
            cp.start()

    def finish(s_refs, l_refs, sems):
        for cp in copies(s_refs, l_refs, sems):
            cp.wait()

    return _Carry(sums, [jax.ShapeDtypeStruct(s.shape, s.dtype) for s in sums], {},
                  [pltpu.SemaphoreType.DMA((n, 3)), pltpu.SemaphoreType.DMA((n, 3))], start, finish)


def _join_carry(outs, layers):
    n = len(outs)

    def copy(o_refs, sems, w, mine):
        send_sems, recv_sems = sems
        x, y, c = _place()
        r = outs[w].shape[1]
        rows = o_refs[w].at[layers[w], pl.ds((c if mine else 1 - c) * (r // 2), r // 2), :]
        return pltpu.make_async_remote_copy(
            src_ref=rows, dst_ref=rows, send_sem=send_sems.at[w], recv_sem=recv_sems.at[w],
            device_id=(x, y, 1 - c), device_id_type=MESH)

    def start(i_refs, o_refs, sems):
        for w in range(n):
            copy(o_refs, sems, w, True).start()

    def finish(i_refs, o_refs, sems):
        for w in range(n):
            copy(o_refs, sems, w, True).wait_send()
        for w in range(n):
            copy(o_refs, sems, w, False).wait_recv()

    return _Carry(outs, [jax.ShapeDtypeStruct(o.shape, o.dtype) for o in outs], {w: w for w in range(n)},
                  [pltpu.SemaphoreType.DMA((n,)), pltpu.SemaphoreType.DMA((n,))], start, finish)


PF_C, PF_Q, PF_QS = 0, 1, 2


def _add_sibling_half(g, landed, pf, name):
    _, r, cols = g.shape
    rh = r // 2
    tr = _row_tile(rh, cols)
    nr = rh // tr

    def body(pf_ref, g_ref, l_ref, o_ref):
        o_ref[...] = (g_ref[...] + l_ref[...]).astype(BF16)

    return pl.pallas_call(
        body, name=name,
        out_shape=jax.ShapeDtypeStruct((3, rh, cols), BF16),
        grid_spec=pltpu.PrefetchScalarGridSpec(
            num_scalar_prefetch=1, grid=(3, nr),
            in_specs=[pl.BlockSpec((1, tr, cols), lambda j, i, pf_ref: (pf_ref[PF_QS + j], pf_ref[PF_C] * nr + i, 0)),
                      pl.BlockSpec((1, tr, cols), lambda j, i, pf_ref: (pf_ref[PF_QS + j], i, 0))],
            out_specs=pl.BlockSpec((1, tr, cols), lambda j, i, pf_ref: (j, i, 0))),
        compiler_params=_cparams(2),
    )(pf, g, landed)


def _add_chips(g, landed, from_chips, pf, prev, layer, n_layer, name):
    _, r, cols = g.shape
    rh = r // 2
    tr = _row_tile(rh, cols)
    nr = rh // tr

    def body(pf_ref, g_ref, l_ref, f_ref, *rest):
        o_ref = rest[-1]
        acc = g_ref[0] + l_ref[0]
        for j in range(3):
            acc = acc + f_ref[j].astype(F32)
        o_ref[0] = acc

    in_specs = [pl.BlockSpec((1, tr, cols), lambda i, pf_ref: (pf_ref[PF_Q], pf_ref[PF_C] * nr + i, 0)),
                pl.BlockSpec((1, tr, cols), lambda i, pf_ref: (pf_ref[PF_Q], i, 0)),
                pl.BlockSpec((3, tr, cols), lambda i, pf_ref: (0, i, 0))]
    args = [pf, g, landed, from_chips]
    aliases = {}
    if prev is not None:
        in_specs.append(pl.BlockSpec(memory_space=pl.ANY))
        args.append(prev)
        aliases = {4: 0}
    return pl.pallas_call(
        body, name=name,
        out_shape=jax.ShapeDtypeStruct((n_layer, r, cols), F32),
        grid_spec=pltpu.PrefetchScalarGridSpec(
            num_scalar_prefetch=1, grid=(nr,), in_specs=in_specs,
            out_specs=pl.BlockSpec((1, tr, cols), lambda i, pf_ref: (layer, pf_ref[PF_C] * nr + i, 0))),
        input_output_aliases=aliases,
        compiler_params=_cparams(1),
    )(*args)


def _allgather_carry(blocks):
    n = len(blocks)

    def copies(b_refs, o_refs, sems):
        send_sems, recv_sems = sems
        x, y, c = _place()
        chips, _ = _other_chips(x, y)

        def place(w, px, py, pc):
            m = blocks[w].shape[0]
            return o_refs[w].at[pl.ds((4 * px + 2 * py + pc) * m, m), :]

        def own_to(w, k, to):
            dst = place(w, x, y, c)
            return pltpu.make_async_remote_copy(src_ref=b_refs[w], dst_ref=dst, send_sem=send_sems.at[w, k],
                                                recv_sem=recv_sems.at[w, k], device_id=to, device_id_type=MESH)

        def landed_from(w, k, px, py, pc):
            blk = place(w, px, py, pc)
            return pltpu.make_async_remote_copy(src_ref=blk, dst_ref=blk, send_sem=send_sems.at[w, k],
                                                recv_sem=recv_sems.at[w, k], device_id=(x, y, 1 - c), device_id_type=MESH)

        return x, y, c, chips, own_to, landed_from

    def start(b_refs, o_refs, sems):
        x, y, c, chips, own_to, _ = copies(b_refs, o_refs, sems)
        for w in range(n):
            own_to(w, 0, (x, y, 1 - c)).start()
            for j, chip in enumerate(chips):
                own_to(w, 1 + j, (*chip, c)).start()

    def mid(b_refs, o_refs, sems):
        x, y, c, chips, _, landed_from = copies(b_refs, o_refs, sems)
        for w in range(n):
            for j, chip in enumerate(chips):
                landed_from(w, 1 + j, *chip, c).wait_recv()
                landed_from(w, 4 + j, *chip, c).start()

    def finish(b_refs, o_refs, sems):
        x, y, c, chips, own_to, landed_from = copies(b_refs, o_refs, sems)
        for w in range(n):
            landed_from(w, 0, x, y, 1 - c).wait_recv()
            for j, chip in enumerate(chips):
                landed_from(w, 4 + j, *chip, 1 - c).wait_recv()
            own_to(w, 0, (x, y, 1 - c)).wait_send()
            for j, chip in enumerate(chips):
                own_to(w, 1 + j, (*chip, c)).wait_send()
                landed_from(w, 4 + j, *chip, c).wait_send()

    return _Carry(blocks, [jax.ShapeDtypeStruct((N_DEV * b.shape[0], b.shape[1]), b.dtype) for b in blocks], {},
                  [pltpu.SemaphoreType.DMA((n, 7)), pltpu.SemaphoreType.DMA((n, 7))], start, finish, mid)


def _sum_devices(gathered, own, me_arr, name):
    m, n = own.shape
    tr = _row_tile(m, n, itemsize=own.dtype.itemsize, target_bytes=256 * 1024)
    nr = m // tr

    def body(me_ref, *refs):
        g_refs, own_ref, o_ref = refs[:N_DEV], refs[N_DEV], refs[N_DEV + 1]
        me = me_ref[0]
        acc = None
        for dev in range(N_DEV):
            term = jnp.where(me == dev, own_ref[...], g_refs[dev][...]).astype(F32)
            acc = term if acc is None else acc + term
        o_ref[...] = acc

    def dev_rows(dev):
        return pl.BlockSpec((tr, n), lambda i, me_ref: (dev * nr + i, 0))

    return pl.pallas_call(
        body, name=name,
        out_shape=jax.ShapeDtypeStruct((m, n), F32),
        grid_spec=pltpu.PrefetchScalarGridSpec(
            num_scalar_prefetch=1, grid=(nr,),
            in_specs=[dev_rows(dev) for dev in range(N_DEV)] + [pl.BlockSpec((tr, n), lambda i, me_ref: (i, 0))],
            out_specs=pl.BlockSpec((tr, n), lambda i, me_ref: (i, 0))),
        compiler_params=_cparams(1),
    )(me_arr, *([gathered] * N_DEV), own)


def _mod_forward(c_all, w_mod, b_mod_shard, name):
    n_layer, d, mq = w_mod.shape

    def body(c_ref, w_ref, b_ref, o_ref):
        cv = c_ref[...]
        o_ref[...] = _dot(cv * _sigmoid(cv), w_ref[0]) + b_ref[0]

    return pl.pallas_call(
        body, name=name, grid=(n_layer,),
        out_shape=jax.ShapeDtypeStruct((n_layer * N_DEV, mq), F32),
        in_specs=[_full((N_DEV, d)), pl.BlockSpec((1, d, mq), lambda l: (l, 0, 0)),
                  pl.BlockSpec((1, 1, mq), lambda l: (l, 0, 0))],
        out_specs=pl.BlockSpec((N_DEV, mq), lambda l: (l, 0)),
        compiler_params=_cparams(1),
    )(c_all, w_mod, b_mod_shard.reshape(n_layer, 1, mq))


def _mod_backward(c_all_t, dmod_shard, name):
    n_layer, _, mq = dmod_shard.shape
    d = c_all_t.shape[0]

    def body(c_ref, dm_ref, o_ref):
        cv = c_ref[...]
        o_ref[0] = _dot(cv * _sigmoid(cv), dm_ref[0])

    return pl.pallas_call(
        body, name=name, grid=(n_layer,),
        out_shape=jax.ShapeDtypeStruct((n_layer, d, mq), F32),
        in_specs=[_full((d, N_DEV)), pl.BlockSpec((1, N_DEV, mq), lambda l: (l, 0, 0))],
        out_specs=pl.BlockSpec((1, d, mq), lambda l: (l, 0, 0)),
        compiler_params=_cparams(1),
    )(c_all_t, dmod_shard)


def _norm_proj(x, mod, vec, w_in, name, carries=()):
    s, d = x.shape
    nq = w_in.shape[2]
    ts = min(TOKENS_MATMUL_TILE, s)

    def body(x_ref, mod_ref, vec_ref, w_ref, h_ref, p_ref, dgel_ref):
        xn, _ = _rms(x_ref[...])
        gm = vec_ref[V_G_PRE_MIX:V_G_PRE_MIX + 1, :] * (1.0 + mod_ref[M_SC_M:M_SC_M + 1, :])
        h = (xn * gm + mod_ref[M_SH_M:M_SH_M + 1, :]).astype(BF16)
        h_ref[...] = h
        for qb in range(N_CHIP):
            pq = _dot(h, w_ref[qb])
            for k in range(N_CHIP * nq // d):
                lo, hi = max(qb * nq, k * d), min((qb + 1) * nq, (k + 1) * d)
                if lo >= hi:
                    continue
                piece = pq[:, lo - qb * nq:hi - qb * nq]
                if k == 4:
                    piece, dgel = _gelu_and_grad(piece)
                    dgel_ref[:, lo - 4 * d:hi - 4 * d] = dgel.astype(BF16)
                elif k >= 5:
                    piece = _sigmoid(piece)
                p_ref[:, lo:hi] = piece.astype(BF16)

    tile = pl.BlockSpec((ts, d), lambda i: (i, 0))
    return _pcall(
        body, name=name, grid=(s // ts,),
        out_shape=[jax.ShapeDtypeStruct((s, d), BF16), jax.ShapeDtypeStruct((s, N_CHIP * nq), BF16),
                   jax.ShapeDtypeStruct((s, d), BF16)],
        in_specs=[tile, _full(mod.shape), _full(vec.shape), _full(w_in.shape)],
        out_specs=[tile, pl.BlockSpec((ts, N_CHIP * nq), lambda i: (i, 0)), tile],
        args=[x, mod, vec, w_in], carries=carries)


def _gate_pre(xb2_b, wg_ref, n_head, bw):
    zr, zi = [], []
    for hd in range(n_head):
        z = _dot(xb2_b[:, hd * bw:(hd + 1) * bw], wg_ref[hd])
        zr.append(z[:, :bw])
        zi.append(z[:, bw:])
    return jnp.concatenate(zr, axis=1), jnp.concatenate(zi, axis=1)


def _lru_coeffs(xb2, wg_ref, vec_ref, n_head, bw):
    zr, zi = _gate_pre(xb2.astype(BF16), wg_ref, n_head, bw)
    r = _sigmoid(zr + vec_ref[V_B_GATE_R:V_B_GATE_R + 1, :])
    gi = _sigmoid(zi + vec_ref[V_B_GATE_I:V_B_GATE_I + 1, :])
    sp = _softplus(-vec_ref[V_LAMBDA:V_LAMBDA + 1, :])
    log_a = (-LRU_C) * r * sp
    a = jnp.exp(log_a)
    mult = jnp.sqrt(_neg_expm1(2.0 * log_a))
    return r, gi, sp, a, mult


def _mixer_forward(x, proj, mod, vec, wg, w_a_out, w_b_out, w_o, name, carries=()):
    s, d = x.shape
    n_head, bw, _ = wg.shape
    ts = min(TOKENS_MIXER_TILE, s)

    def body(x_ref, p_ref, mod_ref, vec_ref, wg_ref, wa_ref, wb_ref, wo_ref,
             x1_ref, conva_ref, xb2_ref, hh_ref, abm_ref, pa_ref, pb_ref, y_ref,
             r_ref, gi_ref, a_ref, mult_ref,
             cv_tail, xb_tail, h_last, a_buf, b_buf, c_buf):
        i = pl.program_id(0)

        @pl.when(i == 0)
        def _():
            cv_tail[...] = jnp.zeros_like(cv_tail)
            xb_tail[...] = jnp.zeros_like(xb_tail)
            h_last[...] = jnp.zeros_like(h_last)

        def seg(k):
            return p_ref[:, k * d:(k + 1) * d].astype(F32)

        def vrow(k):
            return vec_ref[k:k + 1, :]

        b_a, c_a, v_a, x_b, gel, sa, sb = (seg(k) for k in range(7))
        cv = c_a * v_a
        prev_cv = cv_tail[...]
        conv_a = (vrow(V_CONV_A_B) + vrow(V_CONV_A_W) * _shift_down(cv, 2, prev_cv)
                  + vrow(V_CONV_A_W + 1) * _shift_down(cv, 1, prev_cv) + vrow(V_CONV_A_W + 2) * cv)
        cv_tail[...] = cv[ts - SUBLANES:]
        y_a = b_a * conv_a
        prev_xb = xb_tail[...]
        xb2 = (vrow(V_CONV_B_B) + vrow(V_CONV_B_W) * _shift_down(x_b, 3, prev_xb)
               + vrow(V_CONV_B_W + 1) * _shift_down(x_b, 2, prev_xb)
               + vrow(V_CONV_B_W + 2) * _shift_down(x_b, 1, prev_xb) + vrow(V_CONV_B_W + 3) * x_b)
        xb_tail[...] = x_b[ts - SUBLANES:]
        r, gi, _, a, mult = _lru_coeffs(xb2, wg_ref, vec_ref, n_head, bw)
        r_ref[...] = r
        gi_ref[...] = gi
        a_ref[...] = a
        mult_ref[...] = mult
        hh = _scan_two_level(a, mult * gi * xb2, h_last[SUBLANES - 1:SUBLANES, :], a_buf, b_buf, c_buf, reverse=False)
        h_last[...] = hh[ts - SUBLANES:]
        y_b = hh * gel
        ya_b, yb_b = y_a.astype(BF16), y_b.astype(BF16)
        pa = _dot(ya_b, wa_ref[...])
        pb = _dot(yb_b, wb_ref[...])
        m = (sa * pa + sb * pb).astype(BF16)
        y = _dot(m, wo_ref[...])
        yn, _ = _rms(y)
        gg = mod_ref[M_GT_M:M_GT_M + 1, :] * vrow(V_G_POST_MIX)
        x1_ref[...] = x_ref[...] + yn * gg
        conva_ref[...] = conv_a.astype(BF16)
        xb2_ref[...] = xb2
        hh_ref[...] = hh
        abm_ref[0] = ya_b
        abm_ref[1] = yb_b
        abm_ref[2] = m
        pa_ref[...] = pa.astype(BF16)
        pb_ref[...] = pb.astype(BF16)
        y_ref[...] = y.astype(BF16)

    tile = pl.BlockSpec((ts, d), lambda i: (i, 0))
    tile3 = pl.BlockSpec((3, ts, d), lambda i: (0, i, 0))
    sd = lambda dt: jax.ShapeDtypeStruct((s, d), dt)
    return _pcall(
        body, name=name, grid=(s // ts,),
        out_shape=[sd(F32), sd(BF16), sd(F32), sd(F32), jax.ShapeDtypeStruct((3, s, d), BF16), sd(BF16), sd(BF16), sd(BF16),
                   sd(F32), sd(F32), sd(F32), sd(F32)],
        in_specs=[tile, pl.BlockSpec((ts, 7 * d), lambda i: (i, 0)), _full(mod.shape), _full(vec.shape),
                  _full(wg.shape), _full(w_a_out.shape), _full(w_b_out.shape), _full(w_o.shape)],
        out_specs=[tile] * 4 + [tile3] + [tile] * 7,
        scratch_shapes=[pltpu.VMEM((SUBLANES, d), F32)] * 3 + [pltpu.VMEM((d // LANES, ts, LANES), F32)] * 2
                       + [pltpu.VMEM((ts // SCAN_GROUP, d), F32)],
        args=[x, proj, mod, vec, wg, w_a_out, w_b_out, w_o], carries=carries)


def _mlp_forward(x1, mod, vec, w_up, w_down, name, carries=(), target=None):
    s, d = x1.shape
    fq = w_up.shape[2]
    ts = min(TOKENS_MATMUL_TILE, s)

    def body(x_ref, *refs):
        if target is None:
            mod_ref, vec_ref, wu_ref, wd_ref, x2_ref, h2_ref, up_ref, y2_ref = refs
        else:
            t_ref, mod_ref, vec_ref, wu_ref, wd_ref, x2_ref, h2_ref, up_ref, y2_ref, loss_ref = refs
        x = x_ref[...]
        xn, _ = _rms(x)
        gm = vec_ref[V_G_PRE_MLP:V_G_PRE_MLP + 1, :] * (1.0 + mod_ref[M_SC_F:M_SC_F + 1, :])
        h2 = (xn * gm + mod_ref[M_SH_F:M_SH_F + 1, :]).astype(BF16)
        h2_ref[...] = h2
        y2 = jnp.zeros((ts, d), F32)
        for qb in range(N_CHIP):
            up = _dot(h2, wu_ref[qb])
            up_ref[:, qb * fq:(qb + 1) * fq] = up.astype(BF16)
            ru = jnp.maximum(up, 0.0)
            y2 = y2 + _dot((ru * ru).astype(BF16), wd_ref[qb])
        y2_ref[...] = y2.astype(BF16)
        yn, _ = _rms(y2)
        gg = mod_ref[M_GT_F:M_GT_F + 1, :] * vec_ref[V_G_POST_MLP:V_G_POST_MLP + 1, :]
        x2 = x + yn * gg
        if target is None:
            x2_ref[...] = x2
        else:
            @pl.when(pl.program_id(0) == 0)
            def _():
                loss_ref[...] = jnp.zeros_like(loss_ref)

            err = x2 - t_ref[...]
            x2_ref[...] = err * (1.0 / d)
            loss_ref[...] += jnp.sum(jnp.sum(err * err, axis=1, keepdims=True), axis=0, keepdims=True) * (0.5 / d)

    tile = pl.BlockSpec((ts, d), lambda i: (i, 0))
    last = target is not None
    return _pcall(
        body, name=name, grid=(s // ts,),
        out_shape=[jax.ShapeDtypeStruct((s, d), F32), jax.ShapeDtypeStruct((s, d), BF16),
                   jax.ShapeDtypeStruct((s, N_CHIP * fq), BF16), jax.ShapeDtypeStruct((s, d), BF16)]
                  + ([jax.ShapeDtypeStruct((SUBLANES, LANES), F32)] if last else []),
        in_specs=[tile] + ([tile] if last else []) + [_full(mod.shape), _full(vec.shape), _full(w_up.shape), _full(w_down.shape)],
        out_specs=[tile, tile, pl.BlockSpec((ts, N_CHIP * fq), lambda i: (i, 0)), tile]
                 + ([_full((SUBLANES, LANES))] if last else []),
        args=[x1] + ([target] if last else []) + [mod, vec, w_up, w_down], carries=carries)


SB3_DSH, SB3_DSC, SB3_DGT, SB3_DG_PRE, SB3_DG_POST = range(5)
SB1_DSH, SB1_DSC, SB1_DG_PRE = range(3)
(SB2_DGT, SB2_DG_POST, SB2_DWA, SB2_DBA, SB2_DWB, SB2_DBB, SB2_DLAM, SB2_DBR, SB2_DBI) = (0, 1, 2, 5, 6, 10, 11, 12, 13)


def _mlp_backward(dx2, x1, y2, up, mod, vec, w_up, w_down, name, carries=()):
    s, d = dx2.shape
    fq = w_up.shape[2]
    ts = min(TOKENS_MIXER_TILE, s)
    n_t = s // ts

    def body(dx2_ref, x_ref, y2_ref, up_ref, mod_ref, vec_ref, wu_ref, wd_ref,
             dx1_ref, dy2_ref, dup_ref, act_ref, small_ref):
        i = pl.program_id(0)

        @pl.when(i == 0)
        def _():
            small_ref[...] = jnp.zeros_like(small_ref)

        dout = dx2_ref[...]
        y2n, ry = _rms(y2_ref[...].astype(F32))
        g_post = vec_ref[V_G_POST_MLP:V_G_POST_MLP + 1, :]
        gt = mod_ref[M_GT_F:M_GT_F + 1, :]
        dgg = _colsum(dout * y2n)
        dy2 = _rms_bwd(dout * (gt * g_post), y2n, ry).astype(BF16)
        dy2_ref[...] = dy2
        dh2 = jnp.zeros((ts, d), F32)
        for qb in range(N_CHIP):
            cols = slice(qb * fq, (qb + 1) * fq)
            dact = _dot_tb(dy2, wd_ref[qb])
            ru = jnp.maximum(up_ref[:, cols].astype(F32), 0.0)
            dup = (dact * (2.0 * ru)).astype(BF16)
            dup_ref[:, cols] = dup
            act_ref[:, cols] = (ru * ru).astype(BF16)
            dh2 = dh2 + _dot_tb(dup, wu_ref[qb])
        xn, r = _rms(x_ref[...])
        g_pre = vec_ref[V_G_PRE_MLP:V_G_PRE_MLP + 1, :]
        sc1 = 1.0 + mod_ref[M_SC_F:M_SC_F + 1, :]
        dsh = _colsum(dh2)
        dgm = _colsum(dh2 * xn)
        dx1_ref[...] = dout + _rms_bwd(dh2 * (g_pre * sc1), xn, r)
        small_ref[SB3_DSH:SB3_DSH + 1, :] += dsh
        small_ref[SB3_DSC:SB3_DSC + 1, :] += dgm
        small_ref[SB3_DGT:SB3_DGT + 1, :] += dgg

        @pl.when(i == n_t - 1)
        def _():
            dgm_t = small_ref[SB3_DSC:SB3_DSC + 1, :]
            dgg_t = small_ref[SB3_DGT:SB3_DGT + 1, :]
            small_ref[SB3_DSC:SB3_DSC + 1, :] = dgm_t * g_pre
            small_ref[SB3_DG_PRE:SB3_DG_PRE + 1, :] = dgm_t * sc1
            small_ref[SB3_DGT:SB3_DGT + 1, :] = dgg_t * g_post
            small_ref[SB3_DG_POST:SB3_DG_POST + 1, :] = dgg_t * gt

    tile = pl.BlockSpec((ts, d), lambda i: (i, 0))
    wide = pl.BlockSpec((ts, N_CHIP * fq), lambda i: (i, 0))
    return _pcall(
        body, name=name, grid=(n_t,),
        out_shape=[jax.ShapeDtypeStruct((s, d), F32), jax.ShapeDtypeStruct((s, d), BF16),
                   jax.ShapeDtypeStruct((s, N_CHIP * fq), BF16), jax.ShapeDtypeStruct((s, N_CHIP * fq), BF16),
                   jax.ShapeDtypeStruct((SUBLANES, d), F32)],
        in_specs=[tile, tile, tile, wide, _full(mod.shape), _full(vec.shape), _full(w_up.shape), _full(w_down.shape)],
        out_specs=[tile, tile, wide, wide, _full((SUBLANES, d))],
        args=[dx2, x1, y2, up, mod, vec, w_up, w_down], carries=carries)


def _mixer_backward(dx1, proj, conva, xb2s, hhs, pas, pbs, ys, rs_, gis, as_, mults, dgels, mod, vec, wg, w_a_out, w_b_out,
                    w_o, name, carries=()):
    s, d = dx1.shape
    n_head, bw, _ = wg.shape
    ts = min(TOKENS_MIXER_TILE, s)
    n_t = s // ts

    def body(dx1_ref, p_ref, conva_ref, xb2_ref, hh_ref, pa_ref, pb_ref, y_ref, r_ref, gi_ref, a_ref, mult_ref, dgel_ref,
             mod_ref, vec_ref, wg_ref, wa_ref, wb_ref, wo_ref,
             dp_ref, dab_ref, small_ref, dwg_ref,
             dconv_head, dxb2_head, a_head, g_head, a_buf, b_buf, c_buf):
        i = pl.program_id(0)

        @pl.when(i == 0)
        def _():
            small_ref[...] = jnp.zeros_like(small_ref)
            dwg_ref[...] = jnp.zeros_like(dwg_ref)
            dconv_head[...] = jnp.zeros_like(dconv_head)
            dxb2_head[...] = jnp.zeros_like(dxb2_head)
            a_head[...] = jnp.zeros_like(a_head)
            g_head[...] = jnp.zeros_like(g_head)

        def seg(k):
            return p_ref[:, k * d:(k + 1) * d].astype(F32)

        def vrow(k):
            return vec_ref[k:k + 1, :]

        def acc(row, val):
            small_ref[row:row + 1, :] += val

        dout = dx1_ref[...]
        yn, ry = _rms(y_ref[...].astype(F32))
        g_post = vrow(V_G_POST_MIX)
        gt = mod_ref[M_GT_M:M_GT_M + 1, :]
        acc(SB2_DGT, _colsum(dout * yn))
        dy = _rms_bwd(dout * (gt * g_post), yn, ry).astype(BF16)
        dab_ref[2] = dy
        dm = _dot_tb(dy, wo_ref[...])
        sa, sb = seg(5), seg(6)
        dpa = (dm * sa).astype(BF16)
        dpb = (dm * sb).astype(BF16)
        dab_ref[0] = dpa
        dab_ref[1] = dpb
        du_a = dm * pa_ref[...].astype(F32) * (sa * (1.0 - sa))
        du_b = dm * pb_ref[...].astype(F32) * (sb * (1.0 - sb))
        dp_ref[:, 5 * d:6 * d] = du_a.astype(BF16)
        dp_ref[:, 6 * d:7 * d] = du_b.astype(BF16)
        dy_a = _dot_tb(dpa, wa_ref[...])
        dy_b = _dot_tb(dpb, wb_ref[...])

        b_a, c_a, v_a = seg(0), seg(1), seg(2)
        dp_ref[:, 0:d] = (dy_a * conva_ref[...].astype(F32)).astype(BF16)
        dconv = dy_a * b_a
        nxt = dconv_head[...]
        d1 = _shift_up(dconv, 1, nxt)
        d2 = _shift_up(dconv, 2, nxt)
        dconv_head[...] = dconv[:SUBLANES]
        dcv = vrow(V_CONV_A_W + 2) * dconv + vrow(V_CONV_A_W + 1) * d1 + vrow(V_CONV_A_W) * d2
        cv = c_a * v_a
        acc(SB2_DWA + 2, _colsum(cv * dconv))
        acc(SB2_DWA + 1, _colsum(cv * d1))
        acc(SB2_DWA, _colsum(cv * d2))
        acc(SB2_DBA, _colsum(dconv))
        dp_ref[:, d:2 * d] = (dcv * v_a).astype(BF16)
        dp_ref[:, 2 * d:3 * d] = (dcv * c_a).astype(BF16)

        x_b, gel = seg(3), seg(4)
        hh = hh_ref[...]
        dp_ref[:, 4 * d:5 * d] = (dy_b * hh * dgel_ref[...].astype(F32)).astype(BF16)
        dhh = dy_b * gel
        xb2 = xb2_ref[...]
        r, gi, a, mult = r_ref[...], gi_ref[...], a_ref[...], mult_ref[...]
        sp = _softplus(-vrow(V_LAMBDA))
        a_next = _shift_up(a, 1, a_head[...])
        g = _scan_two_level(a_next, dhh, g_head[0:1, :], a_buf, b_buf, c_buf, reverse=True)
        a_head[...] = a[:SUBLANES]
        g_head[...] = g[:SUBLANES]
        gix = gi * xb2
        gm = g * mult
        dlog_a = g * (hh - mult * gix) - (g * gix) * (a * a / mult)
        dgi = gm * xb2
        dxb2 = gm * gi
        acc(SB2_DLAM, _colsum(dlog_a * r))
        dzr = dlog_a * ((-LRU_C) * sp) * (r * (1.0 - r))
        dzi = dgi * (gi * (1.0 - gi))
        acc(SB2_DBR, _colsum(dzr))
        acc(SB2_DBI, _colsum(dzi))
        xb2_b = xb2.astype(BF16)
        back = []
        for hd in range(n_head):
            cols = slice(hd * bw, (hd + 1) * bw)
            dz = jnp.concatenate([dzr[:, cols], dzi[:, cols]], axis=1).astype(BF16)
            back.append(_dot_tb(dz, wg_ref[hd]))
            dwg_ref[hd] += _dot_ta(xb2_b[:, cols], dz)
        dxb2 = dxb2 + jnp.concatenate(back, axis=1)
        nxt = dxb2_head[...]
        e1 = _shift_up(dxb2, 1, nxt)
        e2 = _shift_up(dxb2, 2, nxt)
        e3 = _shift_up(dxb2, 3, nxt)
        dxb2_head[...] = dxb2[:SUBLANES]
        dp_ref[:, 3 * d:4 * d] = (vrow(V_CONV_B_W + 3) * dxb2 + vrow(V_CONV_B_W + 2) * e1
                                  + vrow(V_CONV_B_W + 1) * e2 + vrow(V_CONV_B_W) * e3).astype(BF16)
        acc(SB2_DWB + 3, _colsum(x_b * dxb2))
        acc(SB2_DWB + 2, _colsum(x_b * e1))
        acc(SB2_DWB + 1, _colsum(x_b * e2))
        acc(SB2_DWB, _colsum(x_b * e3))
        acc(SB2_DBB, _colsum(dxb2))

        @pl.when(i == n_t - 1)
        def _():
            dgg_t = small_ref[SB2_DGT:SB2_DGT + 1, :]
            small_ref[SB2_DGT:SB2_DGT + 1, :] = dgg_t * g_post
            small_ref[SB2_DG_POST:SB2_DG_POST + 1, :] = dgg_t * gt
            lam = vrow(V_LAMBDA)
            small_ref[SB2_DLAM:SB2_DLAM + 1, :] = small_ref[SB2_DLAM:SB2_DLAM + 1, :] * (LRU_C * _sigmoid(-lam))

    rev = lambda i: (n_t - 1 - i, 0)
    tile = pl.BlockSpec((ts, d), rev)
    wide = pl.BlockSpec((ts, 7 * d), rev)
    sd = lambda dt: jax.ShapeDtypeStruct((s, d), dt)
    return _pcall(
        body, name=name, grid=(n_t,),
        out_shape=[jax.ShapeDtypeStruct((s, 7 * d), BF16), jax.ShapeDtypeStruct((3, s, d), BF16),
                   jax.ShapeDtypeStruct((2 * SUBLANES, d), F32), jax.ShapeDtypeStruct(wg.shape, F32)],
        in_specs=[tile, wide] + [tile] * 11 + [_full(mod.shape), _full(vec.shape),
                  _full(wg.shape), _full(w_a_out.shape), _full(w_b_out.shape), _full(w_o.shape)],
        out_specs=[wide, pl.BlockSpec((3, ts, d), lambda i: (0, n_t - 1 - i, 0)), _full((2 * SUBLANES, d)), _full(wg.shape)],
        scratch_shapes=[pltpu.VMEM((SUBLANES, d), F32)] * 4 + [pltpu.VMEM((d // LANES, ts, LANES), F32)] * 2
                       + [pltpu.VMEM((ts // SCAN_GROUP, d), F32)],
        args=[dx1, proj, conva, xb2s, hhs, pas, pbs, ys, rs_, gis, as_, mults, dgels, mod, vec, wg, w_a_out, w_b_out, w_o],
        carries=carries)


def _proj_backward(dproj, dx1, x, mod, vec, w_in, name, carries=()):
    s, d = x.shape
    nq = w_in.shape[2]
    ts = min(TOKENS_MATMUL_TILE, s)
    n_t = s // ts

    def body(dp_ref, dx1_ref, x_ref, mod_ref, vec_ref, w_ref, dx_ref, small_ref):
        i = pl.program_id(0)

        @pl.when(i == 0)
        def _():
            small_ref[...] = jnp.zeros_like(small_ref)

        dh = jnp.zeros((ts, d), F32)
        for qb in range(N_CHIP):
            dh = dh + _dot_tb(dp_ref[:, qb * nq:(qb + 1) * nq], w_ref[qb])
        xn, r = _rms(x_ref[...])
        g_pre = vec_ref[V_G_PRE_MIX:V_G_PRE_MIX + 1, :]
        sc1 = 1.0 + mod_ref[M_SC_M:M_SC_M + 1, :]
        dx_ref[...] = dx1_ref[...] + _rms_bwd(dh * (g_pre * sc1), xn, r)
        small_ref[SB1_DSH:SB1_DSH + 1, :] += _colsum(dh)
        small_ref[SB1_DSC:SB1_DSC + 1, :] += _colsum(dh * xn)

        @pl.when(i == n_t - 1)
        def _():
            dgm_t = small_ref[SB1_DSC:SB1_DSC + 1, :]
            small_ref[SB1_DSC:SB1_DSC + 1, :] = dgm_t * g_pre
            small_ref[SB1_DG_PRE:SB1_DG_PRE + 1, :] = dgm_t * sc1

    tile = pl.BlockSpec((ts, d), lambda i: (i, 0))
    return _pcall(
        body, name=name, grid=(n_t,),
        out_shape=[jax.ShapeDtypeStruct((s, d), F32), jax.ShapeDtypeStruct((SUBLANES, d), F32)],
        in_specs=[pl.BlockSpec((ts, N_CHIP * nq), lambda i: (i, 0)), tile, tile, _full(mod.shape), _full(vec.shape),
                  _full(w_in.shape)],
        out_specs=[tile, _full((SUBLANES, d))],
        args=[dproj, dx1, x, mod, vec, w_in], carries=carries)


def _weight_grad(a, b, name, col_blocks=1, tk=512, carries=()):
    s, k = a.shape
    n = b.shape[1]
    tn = n // col_blocks
    tk = min(tk, k)

    def body(a_ref, b_ref, o_ref):
        o_ref[0] = _dot_ta(a_ref[...], b_ref[...])

    (out,), carried = _pcall(
        body, name=name, grid=(col_blocks, k // tk),
        out_shape=[jax.ShapeDtypeStruct((col_blocks, k, tn), F32)],
        in_specs=[pl.BlockSpec((s, tk), lambda j, i: (0, i)), pl.BlockSpec((s, tn), lambda j, i: (0, j))],
        out_specs=[pl.BlockSpec((1, tk, tn), lambda j, i: (j, i, 0))],
        args=[a, b], carries=carries)
    return out, carried


def _weight_grad_stacked(a3, b3, name, tk=512, carries=()):
    n_g, s, k = a3.shape
    n = b3.shape[2]
    kq = k // N_CHIP
    tk = min(tk, k)
    chips_per_tile = tk // kq

    def body(a_ref, b_ref, o_ref):
        o_ref[...] = _dot_ta(a_ref[...], b_ref[...]).reshape(chips_per_tile, kq, n)

    (out,), carried = _pcall(
        body, name=name, grid=(n_g, k // tk),
        out_shape=[jax.ShapeDtypeStruct((N_CHIP, n_g, kq, n), F32)],
        in_specs=[pl.BlockSpec((None, s, tk), lambda g, i: (g, 0, i)), pl.BlockSpec((None, s, n), lambda g, i: (g, 0, 0))],
        out_specs=[pl.BlockSpec((chips_per_tile, None, kq, n), lambda g, i: (i, g, 0, 0))],
        args=[a3, b3], carries=carries)
    return out.reshape(N_CHIP, n_g * kq, n), carried


def _adamw(items, name, copy_grad=False, carries=()):
    shape = items[0][0].shape
    cols = shape[-1]
    rows = items[0][0].size // cols
    tr = _row_tile(rows, cols, target_bytes=1024 * 1024 // len(items))
    c1 = 1.0 - ADAM_B1 ** ADAM_STEP
    c2 = 1.0 - ADAM_B2 ** ADAM_STEP
    n_out = 4 if copy_grad else 3
    n = len(items)

    def body(*refs):
        for k in range(n):
            w_ref, g_ref, m_ref, v_ref = refs[4 * k:4 * k + 4]
            outs = refs[4 * n + n_out * k:4 * n + n_out * (k + 1)]
            gv = g_ref[...]
            nm = ADAM_B1 * m_ref[...] + (1.0 - ADAM_B1) * gv
            nv = ADAM_B2 * v_ref[...] + (1.0 - ADAM_B2) * (gv * gv)
            outs[0][...] = (-ADAM_LR) * ((nm / c1) / (jnp.sqrt(nv / c2) + ADAM_EPS) + ADAM_WD * w_ref[...])
            outs[1][...] = nm
            outs[2][...] = nv
            if copy_grad:
                outs[3][...] = gv

    spec = pl.BlockSpec((tr, cols), lambda i: (i, 0))
    outs, carried = _pcall(
        body, name=name, grid=(rows // tr,),
        out_shape=[jax.ShapeDtypeStruct((rows, cols), F32)] * (n_out * n),
        in_specs=[spec] * (4 * n), out_specs=[spec] * (n_out * n),
        args=[t.reshape(rows, cols) for item in items for t in item], carries=carries)
    return [tuple(o.reshape(shape) for o in outs[n_out * k:n_out * (k + 1)]) for k in range(n)], carried


def kernel(x, c, w_mod, b_mod, g_pre_mix, g_post_mix, w_in, conv_a_w, conv_a_b, w_a_out, conv_b_w, conv_b_b, w_gate_r, b_gate_r, w_gate_i, b_gate_i, lru_lambda, w_b_out, w_o, g_pre_mlp, g_post_mlp, w_mlp_up, w_mlp_down, loss_target, m_w_mod, m_b_mod, m_g_pre_mix, m_g_post_mix, m_w_in, m_conv_a_w, m_conv_a_b, m_w_a_out, m_conv_b_w, m_conv_b_b, m_w_gate_r, m_b_gate_r, m_w_gate_i, m_b_gate_i, m_lru_lambda, m_w_b_out, m_w_o, m_g_pre_mlp, m_g_post_mlp, m_w_mlp_up, m_w_mlp_down, v_w_mod, v_b_mod, v_g_pre_mix, v_g_post_mix, v_w_in, v_conv_a_w, v_conv_a_b, v_w_a_out, v_conv_b_w, v_conv_b_b, v_w_gate_r, v_b_gate_r, v_w_gate_i, v_b_gate_i, v_lru_lambda, v_w_b_out, v_w_o, v_g_pre_mlp, v_g_post_mlp, v_w_mlp_up, v_w_mlp_down):
    weights = dict(w_mod=w_mod, b_mod=b_mod, g_pre_mix=g_pre_mix, g_post_mix=g_post_mix, w_in=w_in, conv_a_w=conv_a_w,
                   conv_a_b=conv_a_b, w_a_out=w_a_out, conv_b_w=conv_b_w, conv_b_b=conv_b_b, w_gate_r=w_gate_r,
                   b_gate_r=b_gate_r, w_gate_i=w_gate_i, b_gate_i=b_gate_i, lru_lambda=lru_lambda, w_b_out=w_b_out,
                   w_o=w_o, g_pre_mlp=g_pre_mlp, g_post_mlp=g_post_mlp, w_mlp_up=w_mlp_up, w_mlp_down=w_mlp_down)
    mom1 = dict(w_mod=m_w_mod, b_mod=m_b_mod, g_pre_mix=m_g_pre_mix, g_post_mix=m_g_post_mix, w_in=m_w_in,
                conv_a_w=m_conv_a_w, conv_a_b=m_conv_a_b, w_a_out=m_w_a_out, conv_b_w=m_conv_b_w, conv_b_b=m_conv_b_b,
                w_gate_r=m_w_gate_r, b_gate_r=m_b_gate_r, w_gate_i=m_w_gate_i, b_gate_i=m_b_gate_i,
                lru_lambda=m_lru_lambda, w_b_out=m_w_b_out, w_o=m_w_o, g_pre_mlp=m_g_pre_mlp, g_post_mlp=m_g_post_mlp,
                w_mlp_up=m_w_mlp_up, w_mlp_down=m_w_mlp_down)
    mom2 = dict(w_mod=v_w_mod, b_mod=v_b_mod, g_pre_mix=v_g_pre_mix, g_post_mix=v_g_post_mix, w_in=v_w_in,
                conv_a_w=v_conv_a_w, conv_a_b=v_conv_a_b, w_a_out=v_w_a_out, conv_b_w=v_conv_b_w, conv_b_b=v_conv_b_b,
                w_gate_r=v_w_gate_r, b_gate_r=v_b_gate_r, w_gate_i=v_w_gate_i, b_gate_i=v_b_gate_i,
                lru_lambda=v_lru_lambda, w_b_out=v_w_b_out, w_o=v_w_o, g_pre_mlp=v_g_pre_mlp, g_post_mlp=v_g_post_mlp,
                w_mlp_up=v_w_mlp_up, w_mlp_down=v_w_mlp_down)
    names = list(weights)

    n_layer = w_in.shape[0]
    s, d = x.shape[1], x.shape[2]
    n_head, bw = w_gate_r.shape[1], w_gate_r.shape[2]
    dq = d // N_CHIP
    mq = w_mod.shape[2]
    n_mod = (N_CHIP * mq) // d
    ka, kb = conv_a_w.shape[1], conv_b_w.shape[1]

    mx, my, mc = _place()
    q_me = 2 * mx + my
    q_arr = jnp.reshape(q_me, (1,)).astype(jnp.int32)

    me_dev = 4 * mx + 2 * my + mc
    me_arr = jnp.reshape(me_dev, (1,)).astype(jnp.int32)

    big_names = ["w_in", "w_a_out", "w_b_out", "w_o", "w_mlp_up", "w_mlp_down"]
    groups = [["w_in"], ["w_a_out", "w_b_out", "w_o"], ["w_mlp_up", "w_mlp_down"]]
    placed = {("w_in", 0): _cast_place_all([(w_in, 0)], q_arr, "cast_place_first")[0][0]}
    wfull = [dict() for _ in range(n_layer)]
    riders = {}
    for l in range(n_layer):
        riders.setdefault(3 * l - 1, []).append(([("w_in", l)], 0.9 if l else 1.0))
        riders.setdefault(3 * l - 2 if l else 0, []).append(([(nm, l) for nm in groups[1]], 0.9 if l else 0.5))
        riders.setdefault(3 * l, []).append(([("w_mlp_up", l)], 0.7 if l else 0.9))
        riders.setdefault(3 * l + 1, []).insert(0, ([("w_mlp_down", l)], 0.5))

    def gather_carry(call):
        return [_gather_carry([placed[k] for k in keys], frac) for keys, frac in riders.get(call, [])]

    def gathered(call, carried):
        for (keys, _), ws in zip(riders.get(call, []), carried):
            for (nm, l), w in zip(keys, ws):
                wfull[l][nm] = w.reshape(d, d) if nm in groups[1] else w

    n_conv_rows = n_layer * (ka + kb)
    conv_blk = -(-n_conv_rows // SUBLANES) * SUBLANES
    blk_rows = SUBLANES + conv_blk
    conv_rows = jnp.concatenate([jnp.concatenate([conv_a_w[l], conv_b_w[l]], axis=0) for l in range(n_layer)], axis=0)
    conv_rows = jnp.pad(conv_rows, ((0, conv_blk - n_conv_rows), (0, d - dq)))
    c_conv = jnp.concatenate([jnp.pad(c, ((0, SUBLANES - 1), (0, 0))), conv_rows], axis=0)
    rest = [(nm, l) for l in range(n_layer) for nm in big_names if (nm, l) != ("w_in", 0)]
    rest_placed, carried = _cast_place_all([(weights[nm], l) for nm, l in rest], q_arr, "cast_place_rest",
                                           carries=gather_carry(-1) + [_allgather_carry([c_conv])])
    placed.update(zip(rest, rest_placed))
    gathered(-1, carried[:1])
    gathered1 = lax.dynamic_update_slice(carried[1][0], c_conv, (me_dev * blk_rows, 0)).reshape(N_DEV, blk_rows, d)
    c_all = gathered1[:, 0, :]
    conv_full = jnp.concatenate([gathered1[2 * qb, SUBLANES:SUBLANES + n_conv_rows, :dq] for qb in range(N_CHIP)], axis=1)

    b_mod_shard = lax.dynamic_slice_in_dim(b_mod, q_me * mq, mq, axis=1)
    mod_part = _mod_forward(c_all, w_mod, b_mod_shard, "mod_forward")
    gathered2 = _all_gather_small(mod_part, "gather_mod").reshape(N_DEV, n_layer, N_DEV, mq)
    mod_rows = jnp.concatenate(
        [lax.dynamic_index_in_dim(gathered2[2 * qb], me_dev, axis=1, keepdims=False) for qb in range(N_CHIP)], axis=1)
    mods = [jnp.pad(mod_rows[l].reshape(n_mod, d), ((0, SUBLANES - n_mod), (0, 0))) for l in range(n_layer)]

    vecs = []
    for l in range(n_layer):
        base = l * (ka + kb)
        rows = [g_pre_mix[l], g_post_mix[l], conv_a_b[l], conv_b_b[l], b_gate_r[l], b_gate_i[l], lru_lambda[l],
                g_pre_mlp[l], g_post_mlp[l]]
        vecs.append(jnp.concatenate([jnp.stack(rows, axis=0), conv_full[base:base + ka + kb]], axis=0))

    wgs =[jnp.concatenate([w_gate_r[l], w_gate_i[l]], axis=-1).astype(BF16) for l in range(n_layer)]

    xs = x[0]
    saved = []
    for l in range(n_layer):
        wl = wfull[l]
        (h, proj, dgel), carried = _norm_proj(xs, mods[l], vecs[l], wl["w_in"], f"norm_proj_{l}", gather_carry(3 * l))
        gathered(3 * l, carried)
        (x1, conva, xb2, hh, abm, pa, pb, yy, gr, ggi, ga, gmult), carried = _mixer_forward(
            xs, proj, mods[l], vecs[l], wgs[l], wl["w_a_out"], wl["w_b_out"], wl["w_o"], f"mixer_forward_{l}",
            gather_carry(3 * l + 1))
        gathered(3 * l + 1, carried)
        (x2, h2, up, y2, *loss_tile), carried = _mlp_forward(
            x1, mods[l], vecs[l], wl["w_mlp_up"], wl["w_mlp_down"], f"mlp_forward_{l}", gather_carry(3 * l + 2),
            target=loss_target[0] if l == n_layer - 1 else None)
        gathered(3 * l + 2, carried)
        saved.append(dict(x=xs, h=h, proj=proj, x1=x1, conva=conva, xb2=xb2, hh=hh, abm=abm, pa=pa, pb=pb,
                          y=yy, r=gr, gi=ggi, a=ga, mult=gmult, dgel=dgel, h2=h2, up=up, y2=y2))
        xs = x2
    dxs = xs
    loss_block = jnp.pad(loss_tile[0], ((0, 0), (0, d - LANES)))

    chips_q = [q_me ^ 2, q_me ^ 1, q_me ^ 3]
    pf = jnp.stack([mc, q_me] + chips_q).astype(jnp.int32)
    rs = dict(grad={}, landed={}, to_send={}, from_chips={}, out={})
    to_exchange, to_scatter, to_join, to_gather = [], [], [], []
    small_own, small_all = {}, {}

    def ride(call, what, name=None):
        ex = list(to_exchange) if "x" in what else []
        sc = list(to_scatter) if "s" in what else []
        ga = list(to_gather) if "g" in what else []
        jn = []
        for key in (to_join if "j" in what else []):
            if key[0] not in [k[0] for k in jn]:
                jn.append(key)
        carries = []
        if ex:
            carries.append(_exchange_carry([rs["grad"][k] for k in ex]))
        if sc:
            carries.append(_scatter_carry([rs["to_send"][k] for k in sc]))
        if jn:
            carries.append(_join_carry([rs["out"][k[0]] for k in jn], [k[1] for k in jn]))
        if ga:
            carries.append(_allgather_carry([small_own[k] for k in ga]))
        if call is None:
            carried = _run_carries(carries, name) if carries else []
            res = None
        else:
            res, carried = call(carries)
        carried = list(carried)
        if ex:
            for k, ld in zip(ex, carried.pop(0)):
                to_exchange.remove(k)
                rs["landed"][k] = ld
                rs["to_send"][k] = _add_sibling_half(rs["grad"][k], ld, pf, f"rs_add_sibling_{k[0]}_{k[1]}")
                to_scatter.append(k)
        if sc:
            for k, fc in zip(sc, carried.pop(0)):
                to_scatter.remove(k)
                rs["out"][k[0]] = _add_chips(rs["grad"][k], rs["landed"][k], fc, pf, rs["out"].get(k[0]), k[1], n_layer,
                                             f"rs_add_chips_{k[0]}_{k[1]}")
                to_join.append(k)
        if jn:
            for k, o in zip(jn, carried.pop(0)):
                to_join.remove(k)
                rs["out"][k[0]] = o
        if ga:
            for k, o in zip(ga, carried.pop(0)):
                to_gather.remove(k)
                small_all[k] = o
        return res

    def gather_small(key, parts):
        small_own[key] = parts[0] if len(parts) == 1 else jnp.concatenate(parts, axis=0)
        to_gather.append(key)

    def ready(nm, l, g):
        rs["grad"][(nm, l)] = g
        to_exchange.append((nm, l))

    rowblk = lambda t: t.reshape(N_CHIP, t.shape[1] // N_CHIP, t.shape[2])
    small1_prev = None
    for l in reversed(range(n_layer)):
        wl, sv = wfull[l], saved[l]
        dx1, dy2, dup, act, small3 = ride(lambda cr: _mlp_backward(
            dxs, sv["x1"], sv["y2"], sv["up"], mods[l], vecs[l], wl["w_mlp_up"], wl["w_mlp_down"], f"mlp_backward_{l}", cr), "xsjg")
        ready("w_mlp_up", l, _weight_grad(sv["h2"], dup, f"grad_w_mlp_up_{l}", col_blocks=N_CHIP)[0])
        g_down = ride(lambda cr: _weight_grad(act, dy2, f"grad_w_mlp_down_{l}", carries=cr), "x")
        ready("w_mlp_down", l, rowblk(g_down))
        dproj, dab, small2, dwg = ride(lambda cr: _mixer_backward(
            dx1, sv["proj"], sv["conva"], sv["xb2"], sv["hh"], sv["pa"], sv["pb"], sv["y"],
            sv["r"], sv["gi"], sv["a"], sv["mult"], sv["dgel"], mods[l], vecs[l], wgs[l],
            wl["w_a_out"], wl["w_b_out"], wl["w_o"], f"mixer_backward_{l}", cr), "xsjg")
        gather_small(("late", l, "s"), ([small1_prev] if small1_prev is not None else []) + [small2, small3])
        gather_small(("late", l, "w"), [dwg.reshape(2 * bw, d).astype(BF16)])
        g_in = ride(lambda cr: _weight_grad(sv["h"], dproj, f"grad_w_in_{l}", col_blocks=N_CHIP, carries=cr), "xsj")
        ready("w_in", l, g_in)
        g_abo = ride(lambda cr: _weight_grad_stacked(sv["abm"], dab, f"grad_w_abo_{l}", carries=cr), "xg")
        ready("w_abo", l, g_abo)
        dxs, small1_prev = ride(lambda cr: _proj_backward(dproj, dx1, sv["x"], mods[l], vecs[l], wl["w_in"],
                                                          f"proj_backward_{l}", cr), "xsjg")
    grad_x = dxs[None]
    gather_small(("last", 0, "s"), [small1_prev, loss_block])

    tail = 0
    while to_exchange or to_scatter or to_join or to_gather:
        ride(None, "xsjg", f"rs_tail_{tail}")
        tail += 1
    grads, deltas, new_m, new_v = {}, {}, {}, {}

    def adam(nms, copy_grad=False):
        items = [(weights[nm], grads[nm], mom1[nm], mom2[nm]) for nm in nms]
        res, _ = _adamw(items, "adamw_" + "_".join(nms), copy_grad)
        for nm, r in zip(nms, res):
            deltas[nm], new_m[nm], new_v[nm] = r[:3]
            if copy_grad:
                grads[nm] = r[3]

    for nms in (["w_mlp_up", "w_mlp_down"], ["w_in"]):
        for nm in nms:
            grads[nm] = rs["out"][nm].reshape(weights[nm].shape)
        adam(nms, True)

    sums ={k: _sum_devices(small_all[k], small_own[k], me_arr, f"sum_small_{k[0]}_{k[1]}_{k[2]}") for k in small_own}

    loss = sums[("last", 0, "s")][SUBLANES, 0]
    small_full = {}

    def rows_of(l, part):
        if part == 0:
            return (("late", l - 1, "s"), 0) if l >= 1 else (("last", 0, "s"), 0)
        if part == 3:
            return ("late", l, "w"), 0
        base = SUBLANES if l < n_layer - 1 else 0
        return ("late", l, "s"), base + (0, 0, 2 * SUBLANES)[part]

    def summed(l, part, row, n_rows=1):
        key, base = rows_of(l, part)
        return sums[key][base + row:base + row + n_rows]

    def per_device(l, part, row):
        key, base = rows_of(l, part)
        own = small_own[key]
        if key not in small_full:
            small_full[key] = lax.dynamic_update_slice(small_all[key], own, (me_dev * own.shape[0], 0)).reshape(
                (N_DEV,) + own.shape)
        return small_full[key][:, base + row:base + row + 1]

    mod_rows = [(0, SB1_DSH), (0, SB1_DSC), (1, SB2_DGT), (2, SB3_DSH), (2, SB3_DSC), (2, SB3_DGT)]
    dmod_all = jnp.stack([jnp.concatenate([per_device(l, p, r)[:, 0, :] for p, r in mod_rows], axis=1)
                          for l in range(n_layer)], axis=0)
    o1, o2, o3, o4 = 0, SUBLANES, 3 * SUBLANES, 4 * SUBLANES
    small_sum = jnp.stack([jnp.concatenate([summed(l, 0, 0, SUBLANES), summed(l, 1, 0, 2 * SUBLANES),
                                            summed(l, 2, 0, SUBLANES), summed(l, 3, 0, 2 * bw)], axis=0)
                           for l in range(n_layer)], axis=0)
    mod_rows_of = [o1 + SB1_DSH, o1 + SB1_DSC, o2 + SB2_DGT, o3 + SB3_DSH, o3 + SB3_DSC, o3 + SB3_DGT]
    grads["w_mod"] = _mod_backward(c_all.T, lax.dynamic_slice_in_dim(dmod_all, q_me * mq, mq, axis=2), "mod_backward")
    grads["b_mod"] = jnp.concatenate([small_sum[:, k, :] for k in mod_rows_of], axis=1)
    grads["g_pre_mix"] = small_sum[:, o1 + SB1_DG_PRE]
    grads["g_post_mix"] = small_sum[:, o2 + SB2_DG_POST]
    grads["conv_a_w"] = lax.dynamic_slice_in_dim(small_sum[:, o2 + SB2_DWA:o2 + SB2_DWA + ka], q_me * dq, dq, axis=2)
    grads["conv_a_b"] = small_sum[:, o2 + SB2_DBA]
    grads["conv_b_w"] = lax.dynamic_slice_in_dim(small_sum[:, o2 + SB2_DWB:o2 + SB2_DWB + kb], q_me * dq, dq, axis=2)
    grads["conv_b_b"] = small_sum[:, o2 + SB2_DBB]
    grads["lru_lambda"] = small_sum[:, o2 + SB2_DLAM]
    grads["b_gate_r"] = small_sum[:, o2 + SB2_DBR]
    grads["b_gate_i"] = small_sum[:, o2 + SB2_DBI]
    grads["g_pre_mlp"] = small_sum[:, o3 + SB3_DG_PRE]
    grads["g_post_mlp"] = small_sum[:, o3 + SB3_DG_POST]
    dwg_sum = small_sum[:, o4:].reshape(n_layer, n_head, bw, 2 * bw)
    grads["w_gate_r"] = dwg_sum[..., :bw]
    grads["w_gate_i"] = dwg_sum[..., bw:]

    for k, nm in enumerate(groups[1]):
        grads[nm] = rs["out"]["w_abo"][:, k * dq:(k + 1) * dq]

    by_shape = {}
    for nm in names:
        if nm not in deltas:
            by_shape.setdefault(weights[nm].shape, []).append(nm)
    for nms in by_shape.values():
        adam(nms)
    return (loss, grad_x, *[grads[nm] for nm in names], *[deltas[nm] for nm in names],
            *[new_m[nm] for nm in names], *[new_v[nm] for nm in names])
```

```python
import jax
import jax.numpy as jnp
from jax import lax
from jax.experimental import pallas as pl
from jax.experimental.pallas import tpu as pltpu

F32 = jnp.float32
BF16 = jnp.bfloat16
MESH = pl.DeviceIdType.MESH

EPS = 1e-6
LRU_C = 8.0
N_CHIP = 4
N_DEV = 8
ADAM_LR = 0.001
ADAM_B1 = 0.9
ADAM_B2 = 0.999
ADAM_EPS = 1e-08
ADAM_WD = 0.01
ADAM_STEP = 10

VMEM_LIMIT_BYTES = 56 * 1024 * 1024
SUBLANES = 8
LANES = 128
TOKENS_MATMUL_TILE = 512
TOKENS_MIXER_TILE = 256
GELU_K0 = 0.7978845608028654
GELU_K1 = 0.044715

V_G_PRE_MIX, V_G_POST_MIX, V_CONV_A_B, V_CONV_B_B, V_B_GATE_R, V_B_GATE_I, V_LAMBDA, V_G_PRE_MLP, V_G_POST_MLP = range(9)
V_CONV_A_W = 9
V_CONV_B_W = 12
M_SH_M, M_SC_M, M_GT_M, M_SH_F, M_SC_F, M_GT_F = range(6)


def _cparams(n_grid=0):
    sem = ("arbitrary",) * n_grid if n_grid else None
    return pltpu.CompilerParams(dimension_semantics=sem, vmem_limit_bytes=VMEM_LIMIT_BYTES)


def _full(shape):
    return pl.BlockSpec(shape, lambda *_: (0,) * len(shape))


def _dot(a, b):
    return jnp.dot(a, b, preferred_element_type=F32)


def _dot_tb(a, b):
    return lax.dot_general(a, b, (((1,), (1,)), ((), ())), preferred_element_type=F32)


def _dot_ta(a, b):
    return lax.dot_general(a, b, (((0,), (0,)), ((), ())), preferred_element_type=F32)


def _sigmoid(x):
    return 1.0 / (1.0 + jnp.exp(-x))


def _softplus(x):
    return jnp.maximum(x, 0.0) + jnp.log1p(jnp.exp(-jnp.abs(x)))


def _neg_expm1(x):
    series = -x * (1.0 + 0.5 * x * (1.0 + (x / 3.0) * (1.0 + 0.25 * x)))
    return jnp.where(x > -1e-2, series, 1.0 - jnp.exp(x))


def _gelu_and_grad(x):
    x2 = x * x
    s = _sigmoid(x * (2.0 * GELU_K0 + (2.0 * GELU_K0 * GELU_K1) * x2))
    gel = x * s
    return gel, s + gel * (1.0 - s) * (2.0 * GELU_K0 + (6.0 * GELU_K0 * GELU_K1) * x2)


def _rms(x):
    r = lax.rsqrt(jnp.mean(x * x, axis=-1, keepdims=True) + EPS)
    return x * r, r


def _rms_bwd(dxn, xn, r):
    return r * (dxn - xn * jnp.mean(dxn * xn, axis=-1, keepdims=True))


def _colsum(x):
    return jnp.sum(x, axis=0, keepdims=True)


def _rows(t, w):
    return lax.broadcasted_iota(jnp.int32, (t, w), 0)


def _shift_down(x, k, prev8):
    t, w = x.shape
    rolled = pltpu.roll(x, k, 0)
    head = jnp.where(_rows(SUBLANES, w) < k, pltpu.roll(prev8, k, 0), rolled[:SUBLANES])
    return jnp.concatenate([head, rolled[SUBLANES:]], axis=0)


def _shift_up(x, k, next8):
    t, w = x.shape
    rolled = pltpu.roll(x, t - k, 0)
    tail = jnp.where(_rows(SUBLANES, w) >= SUBLANES - k, pltpu.roll(next8, SUBLANES - k, 0), rolled[t - SUBLANES:])
    return jnp.concatenate([rolled[:t - SUBLANES], tail], axis=0)


SCAN_GROUP = 16


def _scan_steps(a, b, group, reverse):
    t, w = a.shape
    pos = _rows(t, w) & (group - 1)
    s = 1
    while s < group:
        keep = (pos < group - s) if reverse else (pos >= s)
        shift = (t - s) if reverse else s
        b = b + a * jnp.where(keep, pltpu.roll(b, shift, 0), 0.0)
        a = a * jnp.where(keep, pltpu.roll(a, shift, 0), 1.0)
        s *= 2
    return b, a


def _scan_two_level(a, b, carry_row, a_buf, b_buf, c_buf, reverse):
    t, w = a.shape
    grp = SCAN_GROUP
    n_grp = t // grp
    h_loc, a_cum = _scan_steps(a, b, grp, reverse)
    end = 0 if reverse else grp - 1
    a_end, h_end = [], []
    for j in range(w // LANES):
        a_buf[j] = a_cum[:, j * LANES:(j + 1) * LANES]
        b_buf[j] = h_loc[:, j * LANES:(j + 1) * LANES]
        a_end.append(a_buf[j, pl.ds(end, n_grp, stride=grp), :])
        h_end.append(b_buf[j, pl.ds(end, n_grp, stride=grp), :])
    a_end = jnp.concatenate(a_end, axis=1)
    h_end = jnp.concatenate(h_end, axis=1)
    h_grp, a_grp = _scan_steps(a_end, h_end, n_grp, reverse)
    h_grp = h_grp + a_grp * carry_row
    rows = _rows(n_grp, w)
    if reverse:
        entering = jnp.where(rows == n_grp - 1, carry_row, pltpu.roll(h_grp, n_grp - 1, 0))
    else:
        entering = jnp.where(rows == 0, carry_row, pltpu.roll(h_grp, 1, 0))
    c_buf[...] = entering
    out = [h_loc[g * grp:(g + 1) * grp] + a_cum[g * grp:(g + 1) * grp] * c_buf[g:g + 1, :] for g in range(n_grp)]
    return jnp.concatenate(out, axis=0)


def _row_tile(rows, cols, itemsize=4, target_bytes=2 * 1024 * 1024):
    if rows * cols * itemsize <= target_bytes or rows % SUBLANES:
        return rows
    t = max(SUBLANES, (target_bytes // (cols * itemsize)) // SUBLANES * SUBLANES)
    while rows % t:
        t -= SUBLANES
    return t


def _place():
    return lax.axis_index("x"), lax.axis_index("y"), lax.axis_index("c")


def _other_chips(x, y):
    chips = [(1 - x, y), (x, 1 - y), (1 - x, 1 - y)]
    return chips, [2 * cx + cy for cx, cy in chips]


def _all_gather_small(block, name):
    m_per, n = block.shape

    def body(x_ref, out_ref, send_sems, recv_sems, local_sem):
        x, y, c = _place()
        me, sibling = (x, y, c), (x, y, 1 - c)
        chips, _ = _other_chips(x, y)

        def rows(px, py, pc):
            return out_ref.at[pl.ds((4 * px + 2 * py + pc) * m_per, m_per), :]

        def copy(k, blk, to, src=None):
            return pltpu.make_async_remote_copy(
                src_ref=rows(*blk) if src is None else src, dst_ref=rows(*blk),
                send_sem=send_sems.at[k], recv_sem=recv_sems.at[k], device_id=to, device_id_type=MESH)

        mine = pltpu.make_async_copy(x_ref, rows(*me), local_sem)
        mine.start()
        first = [copy(0, me, sibling, src=x_ref)]
        first += [copy(1 + j, me, (*chip, c), src=x_ref) for j, chip in enumerate(chips)]
        for cp in first:
            cp.start()
        passed = [copy(4 + j, (*chip, c), sibling) for j, chip in enumerate(chips)]
        for j, chip in enumerate(chips):
            copy(1 + j, (*chip, c), me).wait_recv()
            passed[j].start()
        copy(0, sibling, me).wait_recv()
        for j, chip in enumerate(chips):
            copy(4 + j, (*chip, 1 - c), me).wait_recv()
        for cp in first + passed:
            cp.wait_send()
        mine.wait()

    return pl.pallas_call(
        body, name=name,
        out_shape=jax.ShapeDtypeStruct((N_DEV * m_per, n), block.dtype),
        in_specs=[pl.BlockSpec(memory_space=pltpu.VMEM)],
        out_specs=pl.BlockSpec(memory_space=pltpu.VMEM),
        scratch_shapes=[pltpu.SemaphoreType.DMA((7,)), pltpu.SemaphoreType.DMA((7,)), pltpu.SemaphoreType.DMA],
        compiler_params=pltpu.CompilerParams(vmem_limit_bytes=VMEM_LIMIT_BYTES),
    )(block)


class _Carry:
    def __init__(self, ins, out_shapes, aliases, sem_shapes, start, finish, mids=()):
        self.ins, self.out_shapes, self.aliases, self.sem_shapes = list(ins), list(out_shapes), dict(aliases), list(sem_shapes)
        self.start, self.mids, self.finish = start, list(mids), finish


def _pcall(body, *, name, grid, in_specs, out_specs, out_shape, args, scratch_shapes=(), carries=(), prefetch=()):
    in_specs, out_specs, out_shape = list(in_specs), list(out_specs), list(out_shape)
    scratch_shapes, args = list(scratch_shapes), list(args)
    n_in, n_out, n_scr, n_pre = len(in_specs), len(out_shape), len(scratch_shapes), len(prefetch)
    steps = 1
    for g in grid:
        steps *= g
    any_spec = pl.BlockSpec(memory_space=pl.ANY)
    aliases = {}
    spans = []
    for cr in carries:
        spans.append((len(args), len(out_shape), len(scratch_shapes)))
        for a, b in cr.aliases.items():
            aliases[n_pre + len(args) + a] = len(out_shape) + b
        args += cr.ins
        in_specs += [any_spec] * len(cr.ins)
        out_shape += cr.out_shapes
        out_specs += [any_spec] * len(cr.out_shapes)
        scratch_shapes += cr.sem_shapes
    n_all_in = len(args)
    n_all_out = len(out_shape)

    def wrapped(*refs):
        pre, refs = refs[:n_pre], refs[n_pre:]
        ins, outs, scr = refs[:n_all_in], refs[n_all_in:n_all_in + n_all_out], refs[n_all_in + n_all_out:]
        parts = [(cr, ins[a:a + len(cr.ins)], outs[b:b + len(cr.out_shapes)], scr[s:s + len(cr.sem_shapes)])
                 for cr, (a, b, s) in zip(carries, spans)]
        lin = 0
        for ax, g in enumerate(grid):
            lin = lin * g + pl.program_id(ax)

        def at(step, fn):
            if steps == 1:
                fn()
            else:
                pl.when(lin == step)(fn)

        def start_all():
            for cr, ci, co, cs in parts:
                cr.start(ci, co, cs)

        def finish_all():
            for cr, ci, co, cs in parts:
                cr.finish(ci, co, cs)

        if parts:
            at(0, start_all)
        body(*pre, *ins[:n_in], *outs[:n_out], *scr[:n_scr])
        for cr, ci, co, cs in parts:
            for frac, mid in cr.mids:
                at(min(steps - 1, int(steps * frac)), lambda mid=mid, ci=ci, co=co, cs=cs: mid(ci, co, cs))
        if parts:
            at(steps - 1, finish_all)

    if n_pre:
        res = pl.pallas_call(
            wrapped, name=name, out_shape=out_shape,
            grid_spec=pltpu.PrefetchScalarGridSpec(num_scalar_prefetch=n_pre, grid=tuple(grid), in_specs=in_specs,
                                                   out_specs=out_specs, scratch_shapes=scratch_shapes),
            input_output_aliases=aliases, compiler_params=_cparams(len(grid)),
        )(*prefetch, *args)
    else:
        res = pl.pallas_call(
            wrapped, name=name, grid=tuple(grid), out_shape=out_shape, in_specs=in_specs, out_specs=out_specs,
            scratch_shapes=scratch_shapes, input_output_aliases=aliases, compiler_params=_cparams(len(grid)),
        )(*args)
    res = list(res)
    return res[:n_out], [res[b:b + len(cr.out_shapes)] for cr, (_, b, _) in zip(carries, spans)]


def _run_carries(carries, name):
    return _pcall(lambda: None, name=name, grid=(), in_specs=[], out_specs=[], out_shape=[], args=[], carries=carries)[1]


CAST_STEPS = 8


def _cast_place_all(shards, q_arr, name, carries=()):
    n = len(shards)

    def body(q_ref, *refs):
        for k in range(n):
            refs[n + k][...] = refs[k][...].astype(BF16)

    def spec_in(k):
        w, layer = shards[k]
        return pl.BlockSpec((1, w.shape[1] // CAST_STEPS, w.shape[2]), lambda i, q_ref: (layer, i, 0))

    def spec_out(k):
        w, _ = shards[k]
        return pl.BlockSpec((1, w.shape[1] // CAST_STEPS, w.shape[2]), lambda i, q_ref: (q_ref[0], i, 0))

    return _pcall(
        body, name=name, grid=(CAST_STEPS,),
        out_shape=[jax.ShapeDtypeStruct((N_CHIP,) + w.shape[1:], BF16) for w, _ in shards],
        in_specs=[spec_in(k) for k in range(n)], out_specs=[spec_out(k) for k in range(n)],
        args=[w for w, _ in shards], carries=carries, prefetch=[q_arr])


def _gather_carry(bufs, mid_frac=0.85, early_frac=None):
    n = len(bufs)

    def copies(o_refs, sems):
        send_sems, recv_sems = sems
        x, y, c = _place()
        q = 2 * x + y
        sibling = (x, y, 1 - c)
        chips, qs = _other_chips(x, y)

        def half(w, shard, pc):
            rh = bufs[w].shape[1] // 2
            return o_refs[w].at[shard, pl.ds(pc * rh, rh), :]

        def over_ici(w, j, shard):
            blk = half(w, shard, c)
            return pltpu.make_async_remote_copy(
                src_ref=blk, dst_ref=blk, send_sem=send_sems.at[w, j], recv_sem=recv_sems.at[w, j],
                device_id=(*chips[j], c), device_id_type=MESH)

        def to_sibling(w, j, pc):
            blk = half(w, qs[j], pc)
            return pltpu.make_async_remote_copy(
                src_ref=blk, dst_ref=blk, send_sem=send_sems.at[w, 3 + j], recv_sem=recv_sems.at[w, 3 + j],
                device_id=sibling, device_id_type=MESH)

        return q, c, qs, over_ici, to_sibling

    pairs = [(w, j) for w in range(n) for j in range(3)]

    def start(i_refs, o_refs, sems):
        q, _, _, over_ici, _ = copies(o_refs, sems)
        for w, j in pairs:
            over_ici(w, j, q).start()

    def hand_on(which):
        def mid(i_refs, o_refs, sems):
            _, c, qs, over_ici, to_sibling = copies(o_refs, sems)
            for w, j in pairs:
                if j in which:
                    over_ici(w, j, qs[j]).wait_recv()
                    to_sibling(w, j, c).start()
        return mid

    def finish(i_refs, o_refs, sems):
        q, c, _, over_ici, to_sibling = copies(o_refs, sems)
        for w, j in pairs:
            to_sibling(w, j, 1 - c).wait_recv()
        for w, j in pairs:
            over_ici(w, j, q).wait_send()
            to_sibling(w, j, c).wait_send()

    mids = [(mid_frac, hand_on((0, 1, 2)))] if early_frac is None else [(early_frac, hand_on((0, 1))), (mid_frac, hand_on((2,)))]
    return _Carry(bufs, [jax.ShapeDtypeStruct(b.shape, b.dtype) for b in bufs], {w: w for w in range(n)},
                  [pltpu.SemaphoreType.DMA((n, 6)), pltpu.SemaphoreType.DMA((n, 6))], start, finish, mids)


def _exchange_carry(grads):
    n = len(grads)

    def copies(g_refs, l_refs, sems):
        send_sems, recv_sems = sems
        x, y, c = _place()
        out = []
        for w in range(n):
            rh = grads[w].shape[1] // 2
            out.append(pltpu.make_async_remote_copy(
                src_ref=g_refs[w].at[:, pl.ds((1 - c) * rh, rh), :], dst_ref=l_refs[w],
                send_sem=send_sems.at[w], recv_sem=recv_sems.at[w], device_id=(x, y, 1 - c), device_id_type=MESH))
        return out

    def start(g_refs, l_refs, sems):
        for cp in copies(g_refs, l_refs, sems):
            cp.start()

    def finish(g_refs, l_refs, sems):
        for cp in copies(g_refs, l_refs, sems):
            cp.wait()

    return _Carry(grads, [jax.ShapeDtypeStruct((N_CHIP, g.shape[1] // 2, g.shape[2]), g.dtype) for g in grads], {},
                  [pltpu.SemaphoreType.DMA((n,)), pltpu.SemaphoreType.DMA((n,))], start, finish)


def _scatter_carry(sums):
    n = len(sums)

    def copies(s_refs, l_refs, sems):
        send_sems, recv_sems = sems
        x, y, c = _place()
        chips, _ = _other_chips(x, y)
        return [pltpu.make_async_remote_copy(
            src_ref=s_refs[w].at[j], dst_ref=l_refs[w].at[j], send_sem=send_sems.at[w, j], recv_sem=recv_sems.at[w, j],
            device_id=(*chips[j], c), device_id_type=MESH) for w in range(n) for j in range(3)]

    def start(s_refs, l_refs, sems):
        for cp in copies(s_refs, l_refs, sems):
            cp.start()

    def finish(s_refs, l_refs, sems):
        for cp in copies(s_refs, l_refs, sems):
            cp.wait()

    return _Carry(sums, [jax.ShapeDtypeStruct(s.shape, s.dtype) for s in sums], {},
                  [pltpu.SemaphoreType.DMA((n, 3)), pltpu.SemaphoreType.DMA((n, 3))], start, finish)


def _join_carry(outs, layers):
    n = len(outs)

    def copy(o_refs, sems, w, mine):
        send_sems, recv_sems = sems
        x, y, c = _place()
        r = outs[w].shape[1]
        rows = o_refs[w].at[layers[w], pl.ds((c if mine else 1 - c) * (r // 2), r // 2), :]
        return pltpu.make_async_remote_copy(
            src_ref=rows, dst_ref=rows, send_sem=send_sems.at[w], recv_sem=recv_sems.at[w],
            device_id=(x, y, 1 - c), device_id_type=MESH)

    def start(i_refs, o_refs, sems):
        for w in range(n):
            copy(o_refs, sems, w, True).start()

    def finish(i_refs, o_refs, sems):
        for w in range(n):
            copy(o_refs, sems, w, True).wait_send()
        for w in range(n):
            copy(o_refs, sems, w, False).wait_recv()

    return _Carry(outs, [jax.ShapeDtypeStruct(o.shape, o.dtype) for o in outs], {w: w for w in range(n)},
                  [pltpu.SemaphoreType.DMA((n,)), pltpu.SemaphoreType.DMA((n,))], start, finish)


PF_C, PF_Q, PF_QS = 0, 1, 2


def _add_sibling_half(g, landed, pf, name):
    _, r, cols = g.shape
    rh = r // 2
    tr = _row_tile(rh, cols)
    nr = rh // tr

    def body(pf_ref, g_ref, l_ref, o_ref):
        o_ref[...] = (g_ref[...] + l_ref[...]).astype(BF16)

    return pl.pallas_call(
        body, name=name,
        out_shape=jax.ShapeDtypeStruct((3, rh, cols), BF16),
        grid_spec=pltpu.PrefetchScalarGridSpec(
            num_scalar_prefetch=1, grid=(3, nr),
            in_specs=[pl.BlockSpec((1, tr, cols), lambda j, i, pf_ref: (pf_ref[PF_QS + j], pf_ref[PF_C] * nr + i, 0)),
                      pl.BlockSpec((1, tr, cols), lambda j, i, pf_ref: (pf_ref[PF_QS + j], i, 0))],
            out_specs=pl.BlockSpec((1, tr, cols), lambda j, i, pf_ref: (j, i, 0))),
        compiler_params=_cparams(2),
    )(pf, g, landed)


def _add_chips(g, landed, from_chips, pf, prev, layer, n_layer, name):
    _, r, cols = g.shape
    rh = r // 2
    tr = _row_tile(rh, cols)
    nr = rh // tr

    def body(pf_ref, g_ref, l_ref, f_ref, *rest):
        o_ref = rest[-1]
        acc = g_ref[0] + l_ref[0]
        for j in range(3):
            acc = acc + f_ref[j].astype(F32)
        o_ref[0] = acc

    in_specs = [pl.BlockSpec((1, tr, cols), lambda i, pf_ref: (pf_ref[PF_Q], pf_ref[PF_C] * nr + i, 0)),
                pl.BlockSpec((1, tr, cols), lambda i, pf_ref: (pf_ref[PF_Q], i, 0)),
                pl.BlockSpec((3, tr, cols), lambda i, pf_ref: (0, i, 0))]
    args = [pf, g, landed, from_chips]
    aliases = {}
    if prev is not None:
        in_specs.append(pl.BlockSpec(memory_space=pl.ANY))
        args.append(prev)
        aliases = {4: 0}
    return pl.pallas_call(
        body, name=name,
        out_shape=jax.ShapeDtypeStruct((n_layer, r, cols), F32),
        grid_spec=pltpu.PrefetchScalarGridSpec(
            num_scalar_prefetch=1, grid=(nr,), in_specs=in_specs,
            out_specs=pl.BlockSpec((1, tr, cols), lambda i, pf_ref: (layer, pf_ref[PF_C] * nr + i, 0))),
        input_output_aliases=aliases,
        compiler_params=_cparams(1),
    )(*args)


def _allgather_carry(blocks):
    n = len(blocks)

    def copies(b_refs, o_refs, sems):
        send_sems, recv_sems = sems
        x, y, c = _place()
        chips, _ = _other_chips(x, y)

        def place(w, px, py, pc):
            m = blocks[w].shape[0]
            return o_refs[w].at[pl.ds((4 * px + 2 * py + pc) * m, m), :]

        def own_to(w, k, to):
            dst = place(w, x, y, c)
            return pltpu.make_async_remote_copy(src_ref=b_refs[w], dst_ref=dst, send_sem=send_sems.at[w, k],
                                                recv_sem=recv_sems.at[w, k], device_id=to, device_id_type=MESH)

        def landed_from(w, k, px, py, pc):
            blk = place(w, px, py, pc)
            return pltpu.make_async_remote_copy(src_ref=blk, dst_ref=blk, send_sem=send_sems.at[w, k],
                                                recv_sem=recv_sems.at[w, k], device_id=(x, y, 1 - c), device_id_type=MESH)

        return x, y, c, chips, own_to, landed_from

    def start(b_refs, o_refs, sems):
        x, y, c, chips, own_to, _ = copies(b_refs, o_refs, sems)
        for w in range(n):
            own_to(w, 0, (x, y, 1 - c)).start()
            for j, chip in enumerate(chips):
                own_to(w, 1 + j, (*chip, c)).start()

    def mid(b_refs, o_refs, sems):
        x, y, c, chips, _, landed_from = copies(b_refs, o_refs, sems)
        for w in range(n):
            for j, chip in enumerate(chips):
                landed_from(w, 1 + j, *chip, c).wait_recv()
                landed_from(w, 4 + j, *chip, c).start()

    def finish(b_refs, o_refs, sems):
        x, y, c, chips, own_to, landed_from = copies(b_refs, o_refs, sems)
        for w in range(n):
            landed_from(w, 0, x, y, 1 - c).wait_recv()
            for j, chip in enumerate(chips):
                landed_from(w, 4 + j, *chip, 1 - c).wait_recv()
            own_to(w, 0, (x, y, 1 - c)).wait_send()
            for j, chip in enumerate(chips):
                own_to(w, 1 + j, (*chip, c)).wait_send()
                landed_from(w, 4 + j, *chip, c).wait_send()

    return _Carry(blocks, [jax.ShapeDtypeStruct((N_DEV * b.shape[0], b.shape[1]), b.dtype) for b in blocks], {},
                  [pltpu.SemaphoreType.DMA((n, 7)), pltpu.SemaphoreType.DMA((n, 7))], start, finish, [(0.85, mid)])


def _sum_devices(gathered, own, me_arr, name):
    m, n = own.shape
    tr = _row_tile(m, n, itemsize=own.dtype.itemsize, target_bytes=256 * 1024)
    nr = m // tr

    def body(me_ref, *refs):
        g_refs, own_ref, o_ref = refs[:N_DEV], refs[N_DEV], refs[N_DEV + 1]
        me = me_ref[0]
        acc = None
        for dev in range(N_DEV):
            term = jnp.where(me == dev, own_ref[...], g_refs[dev][...]).astype(F32)
            acc = term if acc is None else acc + term
        o_ref[...] = acc

    def dev_rows(dev):
        return pl.BlockSpec((tr, n), lambda i, me_ref: (dev * nr + i, 0))

    return pl.pallas_call(
        body, name=name,
        out_shape=jax.ShapeDtypeStruct((m, n), F32),
        grid_spec=pltpu.PrefetchScalarGridSpec(
            num_scalar_prefetch=1, grid=(nr,),
            in_specs=[dev_rows(dev) for dev in range(N_DEV)] + [pl.BlockSpec((tr, n), lambda i, me_ref: (i, 0))],
            out_specs=pl.BlockSpec((tr, n), lambda i, me_ref: (i, 0))),
        compiler_params=_cparams(1),
    )(me_arr, *([gathered] * N_DEV), own)


def _mod_forward(c_all, w_mod, b_mod_shard, name):
    n_layer, d, mq = w_mod.shape

    def body(c_ref, w_ref, b_ref, o_ref):
        cv = c_ref[...]
        o_ref[...] = _dot(cv * _sigmoid(cv), w_ref[0]) + b_ref[0]

    return pl.pallas_call(
        body, name=name, grid=(n_layer,),
        out_shape=jax.ShapeDtypeStruct((n_layer * N_DEV, mq), F32),
        in_specs=[_full((N_DEV, d)), pl.BlockSpec((1, d, mq), lambda l: (l, 0, 0)),
                  pl.BlockSpec((1, 1, mq), lambda l: (l, 0, 0))],
        out_specs=pl.BlockSpec((N_DEV, mq), lambda l: (l, 0)),
        compiler_params=_cparams(1),
    )(c_all, w_mod, b_mod_shard.reshape(n_layer, 1, mq))


def _mod_backward(c_all_t, dmod_shard, name):
    n_layer, _, mq = dmod_shard.shape
    d = c_all_t.shape[0]

    def body(c_ref, dm_ref, o_ref):
        cv = c_ref[...]
        o_ref[0] = _dot(cv * _sigmoid(cv), dm_ref[0])

    return pl.pallas_call(
        body, name=name, grid=(n_layer,),
        out_shape=jax.ShapeDtypeStruct((n_layer, d, mq), F32),
        in_specs=[_full((d, N_DEV)), pl.BlockSpec((1, N_DEV, mq), lambda l: (l, 0, 0))],
        out_specs=pl.BlockSpec((1, d, mq), lambda l: (l, 0, 0)),
        compiler_params=_cparams(1),
    )(c_all_t, dmod_shard)


def _norm_proj(x, mod, vec, w_in, name, carries=()):
    s, d = x.shape
    nq = w_in.shape[2]
    ts = min(TOKENS_MATMUL_TILE, s)

    def body(x_ref, mod_ref, vec_ref, w_ref, h_ref, p_ref, dgel_ref):
        xn, _ = _rms(x_ref[...])
        gm = vec_ref[V_G_PRE_MIX:V_G_PRE_MIX + 1, :] * (1.0 + mod_ref[M_SC_M:M_SC_M + 1, :])
        h = (xn * gm + mod_ref[M_SH_M:M_SH_M + 1, :]).astype(BF16)
        h_ref[...] = h
        for qb in range(N_CHIP):
            pq = _dot(h, w_ref[qb])
            for k in range(N_CHIP * nq // d):
                lo, hi = max(qb * nq, k * d), min((qb + 1) * nq, (k + 1) * d)
                if lo >= hi:
                    continue
                piece = pq[:, lo - qb * nq:hi - qb * nq]
                if k == 4:
                    piece, dgel = _gelu_and_grad(piece)
                    dgel_ref[:, lo - 4 * d:hi - 4 * d] = dgel.astype(BF16)
                elif k >= 5:
                    piece = _sigmoid(piece)
                p_ref[:, lo:hi] = piece.astype(BF16)

    tile = pl.BlockSpec((ts, d), lambda i: (i, 0))
    return _pcall(
        body, name=name, grid=(s // ts,),
        out_shape=[jax.ShapeDtypeStruct((s, d), BF16), jax.ShapeDtypeStruct((s, N_CHIP * nq), BF16),
                   jax.ShapeDtypeStruct((s, d), BF16)],
        in_specs=[tile, _full(mod.shape), _full(vec.shape), _full(w_in.shape)],
        out_specs=[tile, pl.BlockSpec((ts, N_CHIP * nq), lambda i: (i, 0)), tile],
        args=[x, mod, vec, w_in], carries=carries)


def _gate_pre(xb2_b, wg_ref, n_head, bw):
    zr, zi = [], []
    for hd in range(n_head):
        z = _dot(xb2_b[:, hd * bw:(hd + 1) * bw], wg_ref[hd])
        zr.append(z[:, :bw])
        zi.append(z[:, bw:])
    return jnp.concatenate(zr, axis=1), jnp.concatenate(zi, axis=1)


def _lru_coeffs(xb2, wg_ref, vec_ref, n_head, bw):
    zr, zi = _gate_pre(xb2.astype(BF16), wg_ref, n_head, bw)
    r = _sigmoid(zr + vec_ref[V_B_GATE_R:V_B_GATE_R + 1, :])
    gi = _sigmoid(zi + vec_ref[V_B_GATE_I:V_B_GATE_I + 1, :])
    sp = _softplus(-vec_ref[V_LAMBDA:V_LAMBDA + 1, :])
    log_a = (-LRU_C) * r * sp
    a = jnp.exp(log_a)
    mult = jnp.sqrt(_neg_expm1(2.0 * log_a))
    return r, gi, sp, a, mult


def _mixer_forward(x, proj, mod, vec, wg, w_a_out, w_b_out, w_o, name, carries=()):
    s, d = x.shape
    n_head, bw, _ = wg.shape
    ts = min(TOKENS_MIXER_TILE, s)

    def body(x_ref, p_ref, mod_ref, vec_ref, wg_ref, wa_ref, wb_ref, wo_ref,
             x1_ref, conva_ref, xb2_ref, hh_ref, abm_ref, pa_ref, pb_ref, y_ref,
             r_ref, gi_ref, a_ref, mult_ref,
             cv_tail, xb_tail, h_last, a_buf, b_buf, c_buf):
        i = pl.program_id(0)

        @pl.when(i == 0)
        def _():
            cv_tail[...] = jnp.zeros_like(cv_tail)
            xb_tail[...] = jnp.zeros_like(xb_tail)
            h_last[...] = jnp.zeros_like(h_last)

        def seg(k):
            return p_ref[:, k * d:(k + 1) * d].astype(F32)

        def vrow(k):
            return vec_ref[k:k + 1, :]

        b_a, c_a, v_a, x_b, gel, sa, sb = (seg(k) for k in range(7))
        cv = c_a * v_a
        prev_cv = cv_tail[...]
        conv_a = (vrow(V_CONV_A_B) + vrow(V_CONV_A_W) * _shift_down(cv, 2, prev_cv)
                  + vrow(V_CONV_A_W + 1) * _shift_down(cv, 1, prev_cv) + vrow(V_CONV_A_W + 2) * cv)
        cv_tail[...] = cv[ts - SUBLANES:]
        y_a = b_a * conv_a
        prev_xb = xb_tail[...]
        xb2 = (vrow(V_CONV_B_B) + vrow(V_CONV_B_W) * _shift_down(x_b, 3, prev_xb)
               + vrow(V_CONV_B_W + 1) * _shift_down(x_b, 2, prev_xb)
               + vrow(V_CONV_B_W + 2) * _shift_down(x_b, 1, prev_xb) + vrow(V_CONV_B_W + 3) * x_b)
        xb_tail[...] = x_b[ts - SUBLANES:]
        r, gi, _, a, mult = _lru_coeffs(xb2, wg_ref, vec_ref, n_head, bw)
        r_ref[...] = r
        gi_ref[...] = gi
        a_ref[...] = a
        mult_ref[...] = mult
        hh = _scan_two_level(a, mult * gi * xb2, h_last[SUBLANES - 1:SUBLANES, :], a_buf, b_buf, c_buf, reverse=False)
        h_last[...] = hh[ts - SUBLANES:]
        y_b = hh * gel
        ya_b, yb_b = y_a.astype(BF16), y_b.astype(BF16)
        pa = _dot(ya_b, wa_ref[...])
        pb = _dot(yb_b, wb_ref[...])
        m = (sa * pa + sb * pb).astype(BF16)
        y = _dot(m, wo_ref[...])
        yn, _ = _rms(y)
        gg = mod_ref[M_GT_M:M_GT_M + 1, :] * vrow(V_G_POST_MIX)
        x1_ref[...] = x_ref[...] + yn * gg
        conva_ref[...] = conv_a.astype(BF16)
        xb2_ref[...] = xb2
        hh_ref[...] = hh
        abm_ref[0] = ya_b
        abm_ref[1] = yb_b
        abm_ref[2] = m
        pa_ref[...] = pa.astype(BF16)
        pb_ref[...] = pb.astype(BF16)
        y_ref[...] = y.astype(BF16)

    tile = pl.BlockSpec((ts, d), lambda i: (i, 0))
    tile3 = pl.BlockSpec((3, ts, d), lambda i: (0, i, 0))
    sd = lambda dt: jax.ShapeDtypeStruct((s, d), dt)
    return _pcall(
        body, name=name, grid=(s // ts,),
        out_shape=[sd(F32), sd(BF16), sd(F32), sd(F32), jax.ShapeDtypeStruct((3, s, d), BF16), sd(BF16), sd(BF16), sd(BF16),
                   sd(F32), sd(F32), sd(F32), sd(F32)],
        in_specs=[tile, pl.BlockSpec((ts, 7 * d), lambda i: (i, 0)), _full(mod.shape), _full(vec.shape),
                  _full(wg.shape), _full(w_a_out.shape), _full(w_b_out.shape), _full(w_o.shape)],
        out_specs=[tile] * 4 + [tile3] + [tile] * 7,
        scratch_shapes=[pltpu.VMEM((SUBLANES, d), F32)] * 3 + [pltpu.VMEM((d // LANES, ts, LANES), F32)] * 2
                       + [pltpu.VMEM((ts // SCAN_GROUP, d), F32)],
        args=[x, proj, mod, vec, wg, w_a_out, w_b_out, w_o], carries=carries)


def _mlp_forward(x1, mod, vec, w_up, w_down, name, carries=(), target=None):
    s, d = x1.shape
    fq = w_up.shape[2]
    ts = min(TOKENS_MATMUL_TILE, s)

    def body(x_ref, *refs):
        if target is None:
            mod_ref, vec_ref, wu_ref, wd_ref, x2_ref, h2_ref, up_ref, y2_ref = refs
        else:
            t_ref, mod_ref, vec_ref, wu_ref, wd_ref, x2_ref, h2_ref, up_ref, y2_ref, loss_ref = refs
        x = x_ref[...]
        xn, _ = _rms(x)
        gm = vec_ref[V_G_PRE_MLP:V_G_PRE_MLP + 1, :] * (1.0 + mod_ref[M_SC_F:M_SC_F + 1, :])
        h2 = (xn * gm + mod_ref[M_SH_F:M_SH_F + 1, :]).astype(BF16)
        h2_ref[...] = h2
        y2 = jnp.zeros((ts, d), F32)
        for qb in range(N_CHIP):
            up = _dot(h2, wu_ref[qb])
            up_ref[:, qb * fq:(qb + 1) * fq] = up.astype(BF16)
            ru = jnp.maximum(up, 0.0)
            y2 = y2 + _dot((ru * ru).astype(BF16), wd_ref[qb])
        y2_ref[...] = y2.astype(BF16)
        yn, _ = _rms(y2)
        gg = mod_ref[M_GT_F:M_GT_F + 1, :] * vec_ref[V_G_POST_MLP:V_G_POST_MLP + 1, :]
        x2 = x + yn * gg
        if target is None:
            x2_ref[...] = x2
        else:
            @pl.when(pl.program_id(0) == 0)
            def _():
                loss_ref[...] = jnp.zeros_like(loss_ref)

            err = x2 - t_ref[...]
            x2_ref[...] = err * (1.0 / d)
            loss_ref[...] += jnp.sum(jnp.sum(err * err, axis=1, keepdims=True), axis=0, keepdims=True) * (0.5 / d)

    tile = pl.BlockSpec((ts, d), lambda i: (i, 0))
    last = target is not None
    return _pcall(
        body, name=name, grid=(s // ts,),
        out_shape=[jax.ShapeDtypeStruct((s, d), F32), jax.ShapeDtypeStruct((s, d), BF16),
                   jax.ShapeDtypeStruct((s, N_CHIP * fq), BF16), jax.ShapeDtypeStruct((s, d), BF16)]
                  + ([jax.ShapeDtypeStruct((SUBLANES, LANES), F32)] if last else []),
        in_specs=[tile] + ([tile] if last else []) + [_full(mod.shape), _full(vec.shape), _full(w_up.shape), _full(w_down.shape)],
        out_specs=[tile, tile, pl.BlockSpec((ts, N_CHIP * fq), lambda i: (i, 0)), tile]
                 + ([_full((SUBLANES, LANES))] if last else []),
        args=[x1] + ([target] if last else []) + [mod, vec, w_up, w_down], carries=carries)


SB3_DSH, SB3_DSC, SB3_DGT, SB3_DG_PRE, SB3_DG_POST = range(5)
SB1_DSH, SB1_DSC, SB1_DG_PRE = range(3)
(SB2_DGT, SB2_DG_POST, SB2_DWA, SB2_DBA, SB2_DWB, SB2_DBB, SB2_DLAM, SB2_DBR, SB2_DBI) = (0, 1, 2, 5, 6, 10, 11, 12, 13)


def _mlp_backward(dx2, x1, y2, up, mod, vec, w_up, w_down, name, carries=()):
    s, d = dx2.shape
    fq = w_up.shape[2]
    ts = min(TOKENS_MIXER_TILE, s)
    n_t = s // ts

    def body(dx2_ref, x_ref, y2_ref, up_ref, mod_ref, vec_ref, wu_ref, wd_ref,
             dx1_ref, dy2_ref, dup_ref, act_ref, small_ref):
        i = pl.program_id(0)

        @pl.when(i == 0)
        def _():
            small_ref[...] = jnp.zeros_like(small_ref)

        dout = dx2_ref[...]
        y2n, ry = _rms(y2_ref[...].astype(F32))
        g_post = vec_ref[V_G_POST_MLP:V_G_POST_MLP + 1, :]
        gt = mod_ref[M_GT_F:M_GT_F + 1, :]
        dgg = _colsum(dout * y2n)
        dy2 = _rms_bwd(dout * (gt * g_post), y2n, ry).astype(BF16)
        dy2_ref[...] = dy2
        dh2 = jnp.zeros((ts, d), F32)
        for qb in range(N_CHIP):
            cols = slice(qb * fq, (qb + 1) * fq)
            dact = _dot_tb(dy2, wd_ref[qb])
            ru = jnp.maximum(up_ref[:, cols].astype(F32), 0.0)
            dup = (dact * (2.0 * ru)).astype(BF16)
            dup_ref[:, cols] = dup
            act_ref[:, cols] = (ru * ru).astype(BF16)
            dh2 = dh2 + _dot_tb(dup, wu_ref[qb])
        xn, r = _rms(x_ref[...])
        g_pre = vec_ref[V_G_PRE_MLP:V_G_PRE_MLP + 1, :]
        sc1 = 1.0 + mod_ref[M_SC_F:M_SC_F + 1, :]
        dsh = _colsum(dh2)
        dgm = _colsum(dh2 * xn)
        dx1_ref[...] = dout + _rms_bwd(dh2 * (g_pre * sc1), xn, r)
        small_ref[SB3_DSH:SB3_DSH + 1, :] += dsh
        small_ref[SB3_DSC:SB3_DSC + 1, :] += dgm
        small_ref[SB3_DGT:SB3_DGT + 1, :] += dgg

        @pl.when(i == n_t - 1)
        def _():
            dgm_t = small_ref[SB3_DSC:SB3_DSC + 1, :]
            dgg_t = small_ref[SB3_DGT:SB3_DGT + 1, :]
            small_ref[SB3_DSC:SB3_DSC + 1, :] = dgm_t * g_pre
            small_ref[SB3_DG_PRE:SB3_DG_PRE + 1, :] = dgm_t * sc1
            small_ref[SB3_DGT:SB3_DGT + 1, :] = dgg_t * g_post
            small_ref[SB3_DG_POST:SB3_DG_POST + 1, :] = dgg_t * gt

    tile = pl.BlockSpec((ts, d), lambda i: (i, 0))
    wide = pl.BlockSpec((ts, N_CHIP * fq), lambda i: (i, 0))
    return _pcall(
        body, name=name, grid=(n_t,),
        out_shape=[jax.ShapeDtypeStruct((s, d), F32), jax.ShapeDtypeStruct((s, d), BF16),
                   jax.ShapeDtypeStruct((s, N_CHIP * fq), BF16), jax.ShapeDtypeStruct((s, N_CHIP * fq), BF16),
                   jax.ShapeDtypeStruct((SUBLANES, d), F32)],
        in_specs=[tile, tile, tile, wide, _full(mod.shape), _full(vec.shape), _full(w_up.shape), _full(w_down.shape)],
        out_specs=[tile, tile, wide, wide, _full((SUBLANES, d))],
        args=[dx2, x1, y2, up, mod, vec, w_up, w_down], carries=carries)


def _mixer_backward(dx1, proj, conva, xb2s, hhs, pas, pbs, ys, rs_, gis, as_, mults, dgels, mod, vec, wg, w_a_out, w_b_out,
                    w_o, name, carries=()):
    s, d = dx1.shape
    n_head, bw, _ = wg.shape
    ts = min(TOKENS_MIXER_TILE, s)
    n_t = s // ts

    def body(dx1_ref, p_ref, conva_ref, xb2_ref, hh_ref, pa_ref, pb_ref, y_ref, r_ref, gi_ref, a_ref, mult_ref, dgel_ref,
             mod_ref, vec_ref, wg_ref, wa_ref, wb_ref, wo_ref,
             dp_ref, dab_ref, small_ref, dwg_ref,
             dconv_head, dxb2_head, a_head, g_head, a_buf, b_buf, c_buf):
        i = pl.program_id(0)

        @pl.when(i == 0)
        def _():
            small_ref[...] = jnp.zeros_like(small_ref)
            dwg_ref[...] = jnp.zeros_like(dwg_ref)
            dconv_head[...] = jnp.zeros_like(dconv_head)
            dxb2_head[...] = jnp.zeros_like(dxb2_head)
            a_head[...] = jnp.zeros_like(a_head)
            g_head[...] = jnp.zeros_like(g_head)

        def seg(k):
            return p_ref[:, k * d:(k + 1) * d].astype(F32)

        def vrow(k):
            return vec_ref[k:k + 1, :]

        def acc(row, val):
            small_ref[row:row + 1, :] += val

        dout = dx1_ref[...]
        yn, ry = _rms(y_ref[...].astype(F32))
        g_post = vrow(V_G_POST_MIX)
        gt = mod_ref[M_GT_M:M_GT_M + 1, :]
        acc(SB2_DGT, _colsum(dout * yn))
        dy = _rms_bwd(dout * (gt * g_post), yn, ry).astype(BF16)
        dab_ref[2] = dy
        dm = _dot_tb(dy, wo_ref[...])
        sa, sb = seg(5), seg(6)
        dpa = (dm * sa).astype(BF16)
        dpb = (dm * sb).astype(BF16)
        dab_ref[0] = dpa
        dab_ref[1] = dpb
        du_a = dm * pa_ref[...].astype(F32) * (sa * (1.0 - sa))
        du_b = dm * pb_ref[...].astype(F32) * (sb * (1.0 - sb))
        dp_ref[:, 5 * d:6 * d] = du_a.astype(BF16)
        dp_ref[:, 6 * d:7 * d] = du_b.astype(BF16)
        dy_a = _dot_tb(dpa, wa_ref[...])
        dy_b = _dot_tb(dpb, wb_ref[...])

        b_a, c_a, v_a = seg(0), seg(1), seg(2)
        dp_ref[:, 0:d] = (dy_a * conva_ref[...].astype(F32)).astype(BF16)
        dconv = dy_a * b_a
        nxt = dconv_head[...]
        d1 = _shift_up(dconv, 1, nxt)
        d2 = _shift_up(dconv, 2, nxt)
        dconv_head[...] = dconv[:SUBLANES]
        dcv = vrow(V_CONV_A_W + 2) * dconv + vrow(V_CONV_A_W + 1) * d1 + vrow(V_CONV_A_W) * d2
        cv = c_a * v_a
        acc(SB2_DWA + 2, _colsum(cv * dconv))
        acc(SB2_DWA + 1, _colsum(cv * d1))
        acc(SB2_DWA, _colsum(cv * d2))
        acc(SB2_DBA, _colsum(dconv))
        dp_ref[:, d:2 * d] = (dcv * v_a).astype(BF16)
        dp_ref[:, 2 * d:3 * d] = (dcv * c_a).astype(BF16)

        x_b, gel = seg(3), seg(4)
        hh = hh_ref[...]
        dp_ref[:, 4 * d:5 * d] = (dy_b * hh * dgel_ref[...].astype(F32)).astype(BF16)
        dhh = dy_b * gel
        xb2 = xb2_ref[...]
        r, gi, a, mult = r_ref[...], gi_ref[...], a_ref[...], mult_ref[...]
        sp = _softplus(-vrow(V_LAMBDA))
        a_next = _shift_up(a, 1, a_head[...])
        g = _scan_two_level(a_next, dhh, g_head[0:1, :], a_buf, b_buf, c_buf, reverse=True)
        a_head[...] = a[:SUBLANES]
        g_head[...] = g[:SUBLANES]
        gix = gi * xb2
        gm = g * mult
        dlog_a = g * (hh - mult * gix) - (g * gix) * (a * a / mult)
        dgi = gm * xb2
        dxb2 = gm * gi
        acc(SB2_DLAM, _colsum(dlog_a * r))
        dzr = dlog_a * ((-LRU_C) * sp) * (r * (1.0 - r))
        dzi = dgi * (gi * (1.0 - gi))
        acc(SB2_DBR, _colsum(dzr))
        acc(SB2_DBI, _colsum(dzi))
        xb2_b = xb2.astype(BF16)
        back = []
        for hd in range(n_head):
            cols = slice(hd * bw, (hd + 1) * bw)
            dz = jnp.concatenate([dzr[:, cols], dzi[:, cols]], axis=1).astype(BF16)
            back.append(_dot_tb(dz, wg_ref[hd]))
            dwg_ref[hd] += _dot_ta(xb2_b[:, cols], dz)
        dxb2 = dxb2 + jnp.concatenate(back, axis=1)
        nxt = dxb2_head[...]
        e1 = _shift_up(dxb2, 1, nxt)
        e2 = _shift_up(dxb2, 2, nxt)
        e3 = _shift_up(dxb2, 3, nxt)
        dxb2_head[...] = dxb2[:SUBLANES]
        dp_ref[:, 3 * d:4 * d] = (vrow(V_CONV_B_W + 3) * dxb2 + vrow(V_CONV_B_W + 2) * e1
                                  + vrow(V_CONV_B_W + 1) * e2 + vrow(V_CONV_B_W) * e3).astype(BF16)
        acc(SB2_DWB + 3, _colsum(x_b * dxb2))
        acc(SB2_DWB + 2, _colsum(x_b * e1))
        acc(SB2_DWB + 1, _colsum(x_b * e2))
        acc(SB2_DWB, _colsum(x_b * e3))
        acc(SB2_DBB, _colsum(dxb2))

        @pl.when(i == n_t - 1)
        def _():
            dgg_t = small_ref[SB2_DGT:SB2_DGT + 1, :]
            small_ref[SB2_DGT:SB2_DGT + 1, :] = dgg_t * g_post
            small_ref[SB2_DG_POST:SB2_DG_POST + 1, :] = dgg_t * gt
            lam = vrow(V_LAMBDA)
            small_ref[SB2_DLAM:SB2_DLAM + 1, :] = small_ref[SB2_DLAM:SB2_DLAM + 1, :] * (LRU_C * _sigmoid(-lam))

    rev = lambda i: (n_t - 1 - i, 0)
    tile = pl.BlockSpec((ts, d), rev)
    wide = pl.BlockSpec((ts, 7 * d), rev)
    sd = lambda dt: jax.ShapeDtypeStruct((s, d), dt)
    return _pcall(
        body, name=name, grid=(n_t,),
        out_shape=[jax.ShapeDtypeStruct((s, 7 * d), BF16), jax.ShapeDtypeStruct((3, s, d), BF16),
                   jax.ShapeDtypeStruct((2 * SUBLANES, d), F32), jax.ShapeDtypeStruct(wg.shape, F32)],
        in_specs=[tile, wide] + [tile] * 11 + [_full(mod.shape), _full(vec.shape),
                  _full(wg.shape), _full(w_a_out.shape), _full(w_b_out.shape), _full(w_o.shape)],
        out_specs=[wide, pl.BlockSpec((3, ts, d), lambda i: (0, n_t - 1 - i, 0)), _full((2 * SUBLANES, d)), _full(wg.shape)],
        scratch_shapes=[pltpu.VMEM((SUBLANES, d), F32)] * 4 + [pltpu.VMEM((d // LANES, ts, LANES), F32)] * 2
                       + [pltpu.VMEM((ts // SCAN_GROUP, d), F32)],
        args=[dx1, proj, conva, xb2s, hhs, pas, pbs, ys, rs_, gis, as_, mults, dgels, mod, vec, wg, w_a_out, w_b_out, w_o],
        carries=carries)


def _proj_backward(dproj, dx1, x, mod, vec, w_in, name, carries=()):
    s, d = x.shape
    nq = w_in.shape[2]
    ts = min(TOKENS_MATMUL_TILE, s)
    n_t = s // ts

    def body(dp_ref, dx1_ref, x_ref, mod_ref, vec_ref, w_ref, dx_ref, small_ref):
        i = pl.program_id(0)

        @pl.when(i == 0)
        def _():
            small_ref[...] = jnp.zeros_like(small_ref)

        dh = jnp.zeros((ts, d), F32)
        for qb in range(N_CHIP):
            dh = dh + _dot_tb(dp_ref[:, qb * nq:(qb + 1) * nq], w_ref[qb])
        xn, r = _rms(x_ref[...])
        g_pre = vec_ref[V_G_PRE_MIX:V_G_PRE_MIX + 1, :]
        sc1 = 1.0 + mod_ref[M_SC_M:M_SC_M + 1, :]
        dx_ref[...] = dx1_ref[...] + _rms_bwd(dh * (g_pre * sc1), xn, r)
        small_ref[SB1_DSH:SB1_DSH + 1, :] += _colsum(dh)
        small_ref[SB1_DSC:SB1_DSC + 1, :] += _colsum(dh * xn)

        @pl.when(i == n_t - 1)
        def _():
            dgm_t = small_ref[SB1_DSC:SB1_DSC + 1, :]
            small_ref[SB1_DSC:SB1_DSC + 1, :] = dgm_t * g_pre
            small_ref[SB1_DG_PRE:SB1_DG_PRE + 1, :] = dgm_t * sc1

    tile = pl.BlockSpec((ts, d), lambda i: (i, 0))
    return _pcall(
        body, name=name, grid=(n_t,),
        out_shape=[jax.ShapeDtypeStruct((s, d), F32), jax.ShapeDtypeStruct((SUBLANES, d), F32)],
        in_specs=[pl.BlockSpec((ts, N_CHIP * nq), lambda i: (i, 0)), tile, tile, _full(mod.shape), _full(vec.shape),
                  _full(w_in.shape)],
        out_specs=[tile, _full((SUBLANES, d))],
        args=[dproj, dx1, x, mod, vec, w_in], carries=carries)


def _weight_grad(a, b, name, col_blocks=1, tk=512, carries=()):
    s, k = a.shape
    n = b.shape[1]
    tn = n // col_blocks
    tk = min(tk, k)

    def body(a_ref, b_ref, o_ref):
        o_ref[0] = _dot_ta(a_ref[...], b_ref[...])

    (out,), carried = _pcall(
        body, name=name, grid=(col_blocks, k // tk),
        out_shape=[jax.ShapeDtypeStruct((col_blocks, k, tn), F32)],
        in_specs=[pl.BlockSpec((s, tk), lambda j, i: (0, i)), pl.BlockSpec((s, tn), lambda j, i: (0, j))],
        out_specs=[pl.BlockSpec((1, tk, tn), lambda j, i: (j, i, 0))],
        args=[a, b], carries=carries)
    return out, carried


def _weight_grad_stacked(a3, b3, name, tk=512, carries=()):
    n_g, s, k = a3.shape
    n = b3.shape[2]
    kq = k // N_CHIP
    tk = min(tk, k)
    chips_per_tile = tk // kq

    def body(a_ref, b_ref, o_ref):
        o_ref[...] = _dot_ta(a_ref[...], b_ref[...]).reshape(chips_per_tile, kq, n)

    (out,), carried = _pcall(
        body, name=name, grid=(n_g, k // tk),
        out_shape=[jax.ShapeDtypeStruct((N_CHIP, n_g, kq, n), F32)],
        in_specs=[pl.BlockSpec((None, s, tk), lambda g, i: (g, 0, i)), pl.BlockSpec((None, s, n), lambda g, i: (g, 0, 0))],
        out_specs=[pl.BlockSpec((chips_per_tile, None, kq, n), lambda g, i: (i, g, 0, 0))],
        args=[a3, b3], carries=carries)
    return out.reshape(N_CHIP, n_g * kq, n), carried


def _adamw(items, name, copy_grad=False, carries=()):
    shape = items[0][0].shape
    cols = shape[-1]
    rows = items[0][0].size // cols
    tr = _row_tile(rows, cols, target_bytes=1024 * 1024 // len(items))
    c1 = 1.0 - ADAM_B1 ** ADAM_STEP
    c2 = 1.0 - ADAM_B2 ** ADAM_STEP
    n_out = 4 if copy_grad else 3
    n = len(items)

    def body(*refs):
        for k in range(n):
            w_ref, g_ref, m_ref, v_ref = refs[4 * k:4 * k + 4]
            outs = refs[4 * n + n_out * k:4 * n + n_out * (k + 1)]
            gv = g_ref[...]
            nm = ADAM_B1 * m_ref[...] + (1.0 - ADAM_B1) * gv
            nv = ADAM_B2 * v_ref[...] + (1.0 - ADAM_B2) * (gv * gv)
            outs[0][...] = (-ADAM_LR) * ((nm / c1) / (jnp.sqrt(nv / c2) + ADAM_EPS) + ADAM_WD * w_ref[...])
            outs[1][...] = nm
            outs[2][...] = nv
            if copy_grad:
                outs[3][...] = gv

    spec = pl.BlockSpec((tr, cols), lambda i: (i, 0))
    outs, carried = _pcall(
        body, name=name, grid=(rows // tr,),
        out_shape=[jax.ShapeDtypeStruct((rows, cols), F32)] * (n_out * n),
        in_specs=[spec] * (4 * n), out_specs=[spec] * (n_out * n),
        args=[t.reshape(rows, cols) for item in items for t in item], carries=carries)
    return [tuple(o.reshape(shape) for o in outs[n_out * k:n_out * (k + 1)]) for k in range(n)], carried


def kernel(x, c, w_mod, b_mod, g_pre_mix, g_post_mix, w_in, conv_a_w, conv_a_b, w_a_out, conv_b_w, conv_b_b, w_gate_r, b_gate_r, w_gate_i, b_gate_i, lru_lambda, w_b_out, w_o, g_pre_mlp, g_post_mlp, w_mlp_up, w_mlp_down, loss_target, m_w_mod, m_b_mod, m_g_pre_mix, m_g_post_mix, m_w_in, m_conv_a_w, m_conv_a_b, m_w_a_out, m_conv_b_w, m_conv_b_b, m_w_gate_r, m_b_gate_r, m_w_gate_i, m_b_gate_i, m_lru_lambda, m_w_b_out, m_w_o, m_g_pre_mlp, m_g_post_mlp, m_w_mlp_up, m_w_mlp_down, v_w_mod, v_b_mod, v_g_pre_mix, v_g_post_mix, v_w_in, v_conv_a_w, v_conv_a_b, v_w_a_out, v_conv_b_w, v_conv_b_b, v_w_gate_r, v_b_gate_r, v_w_gate_i, v_b_gate_i, v_lru_lambda, v_w_b_out, v_w_o, v_g_pre_mlp, v_g_post_mlp, v_w_mlp_up, v_w_mlp_down):
    weights = dict(w_mod=w_mod, b_mod=b_mod, g_pre_mix=g_pre_mix, g_post_mix=g_post_mix, w_in=w_in, conv_a_w=conv_a_w,
                   conv_a_b=conv_a_b, w_a_out=w_a_out, conv_b_w=conv_b_w, conv_b_b=conv_b_b, w_gate_r=w_gate_r,
                   b_gate_r=b_gate_r, w_gate_i=w_gate_i, b_gate_i=b_gate_i, lru_lambda=lru_lambda, w_b_out=w_b_out,
                   w_o=w_o, g_pre_mlp=g_pre_mlp, g_post_mlp=g_post_mlp, w_mlp_up=w_mlp_up, w_mlp_down=w_mlp_down)
    mom1 = dict(w_mod=m_w_mod, b_mod=m_b_mod, g_pre_mix=m_g_pre_mix, g_post_mix=m_g_post_mix, w_in=m_w_in,
                conv_a_w=m_conv_a_w, conv_a_b=m_conv_a_b, w_a_out=m_w_a_out, conv_b_w=m_conv_b_w, conv_b_b=m_conv_b_b,
                w_gate_r=m_w_gate_r, b_gate_r=m_b_gate_r, w_gate_i=m_w_gate_i, b_gate_i=m_b_gate_i,
                lru_lambda=m_lru_lambda, w_b_out=m_w_b_out, w_o=m_w_o, g_pre_mlp=m_g_pre_mlp, g_post_mlp=m_g_post_mlp,
                w_mlp_up=m_w_mlp_up, w_mlp_down=m_w_mlp_down)
    mom2 = dict(w_mod=v_w_mod, b_mod=v_b_mod, g_pre_mix=v_g_pre_mix, g_post_mix=v_g_post_mix, w_in=v_w_in,
                conv_a_w=v_conv_a_w, conv_a_b=v_conv_a_b, w_a_out=v_w_a_out, conv_b_w=v_conv_b_w, conv_b_b=v_conv_b_b,
                w_gate_r=v_w_gate_r, b_gate_r=v_b_gate_r, w_gate_i=v_w_gate_i, b_gate_i=v_b_gate_i,
                lru_lambda=v_lru_lambda, w_b_out=v_w_b_out, w_o=v_w_o, g_pre_mlp=v_g_pre_mlp, g_post_mlp=v_g_post_mlp,
                w_mlp_up=v_w_mlp_up, w_mlp_down=v_w_mlp_down)
    names = list(weights)

    n_layer = w_in.shape[0]
    s, d = x.shape[1], x.shape[2]
    n_head, bw = w_gate_r.shape[1], w_gate_r.shape[2]
    dq = d // N_CHIP
    mq = w_mod.shape[2]
    n_mod = (N_CHIP * mq) // d
    ka, kb = conv_a_w.shape[1], conv_b_w.shape[1]

    mx, my, mc = _place()
    q_me = 2 * mx + my
    q_arr = jnp.reshape(q_me, (1,)).astype(jnp.int32)

    me_dev = 4 * mx + 2 * my + mc
    me_arr = jnp.reshape(me_dev, (1,)).astype(jnp.int32)

    big_names = ["w_in", "w_a_out", "w_b_out", "w_o", "w_mlp_up", "w_mlp_down"]
    groups = [["w_in"], ["w_a_out", "w_b_out", "w_o"], ["w_mlp_up", "w_mlp_down"]]
    placed = {("w_in", 0): _cast_place_all([(w_in, 0)], q_arr, "cast_place_first")[0][0]}
    wfull = [dict() for _ in range(n_layer)]
    riders = {}
    for l in range(n_layer):
        riders.setdefault(3 * l - 1, []).append(([("w_in", l)], 1.0, 0.75 if l else 1.0))
        riders.setdefault(3 * l - 2 if l else 0, []).append(([(nm, l) for nm in groups[1]], 0.9 if l else 0.5, None))
        riders.setdefault(3 * l, []).append(([("w_mlp_up", l)], 0.7 if l else 1.0, 0.4 if l else 0.75))
        riders.setdefault(3 * l + 1, []).insert(0, ([("w_mlp_down", l)], 0.6, 0.4))

    def gather_carry(call):
        return [_gather_carry([placed[k] for k in keys], frac, early) for keys, frac, early in riders.get(call, [])]

    def gathered(call, carried):
        for (keys, _, _), ws in zip(riders.get(call, []), carried):
            for (nm, l), w in zip(keys, ws):
                wfull[l][nm] = w.reshape(d, d) if nm in groups[1] else w

    n_conv_rows = n_layer * (ka + kb)
    conv_blk = -(-n_conv_rows // SUBLANES) * SUBLANES
    blk_rows = SUBLANES + conv_blk
    conv_rows = jnp.concatenate([jnp.concatenate([conv_a_w[l], conv_b_w[l]], axis=0) for l in range(n_layer)], axis=0)
    conv_rows = jnp.pad(conv_rows, ((0, conv_blk - n_conv_rows), (0, d - dq)))
    c_conv = jnp.concatenate([jnp.pad(c, ((0, SUBLANES - 1), (0, 0))), conv_rows], axis=0)
    rest = [(nm, l) for l in range(n_layer) for nm in big_names if (nm, l) != ("w_in", 0)]
    rest_placed, carried = _cast_place_all([(weights[nm], l) for nm, l in rest], q_arr, "cast_place_rest",
                                           carries=gather_carry(-1) + [_allgather_carry([c_conv])])
    placed.update(zip(rest, rest_placed))
    gathered(-1, carried[:1])
    gathered1 = lax.dynamic_update_slice(carried[1][0], c_conv, (me_dev * blk_rows, 0)).reshape(N_DEV, blk_rows, d)
    c_all = gathered1[:, 0, :]
    conv_full = jnp.concatenate([gathered1[2 * qb, SUBLANES:SUBLANES + n_conv_rows, :dq] for qb in range(N_CHIP)], axis=1)

    b_mod_shard = lax.dynamic_slice_in_dim(b_mod, q_me * mq, mq, axis=1)
    mod_part = _mod_forward(c_all, w_mod, b_mod_shard, "mod_forward")
    gathered2 = _all_gather_small(mod_part, "gather_mod").reshape(N_DEV, n_layer, N_DEV, mq)
    mod_rows = jnp.concatenate(
        [lax.dynamic_index_in_dim(gathered2[2 * qb], me_dev, axis=1, keepdims=False) for qb in range(N_CHIP)], axis=1)
    mods = [jnp.pad(mod_rows[l].reshape(n_mod, d), ((0, SUBLANES - n_mod), (0, 0))) for l in range(n_layer)]

    vecs = []
    for l in range(n_layer):
        base = l * (ka + kb)
        rows = [g_pre_mix[l], g_post_mix[l], conv_a_b[l], conv_b_b[l], b_gate_r[l], b_gate_i[l], lru_lambda[l],
                g_pre_mlp[l], g_post_mlp[l]]
        vecs.append(jnp.concatenate([jnp.stack(rows, axis=0), conv_full[base:base + ka + kb]], axis=0))

    wgs =[jnp.concatenate([w_gate_r[l], w_gate_i[l]], axis=-1).astype(BF16) for l in range(n_layer)]

    xs = x[0]
    saved = []
    for l in range(n_layer):
        wl = wfull[l]
        (h, proj, dgel), carried = _norm_proj(xs, mods[l], vecs[l], wl["w_in"], f"norm_proj_{l}", gather_carry(3 * l))
        gathered(3 * l, carried)
        (x1, conva, xb2, hh, abm, pa, pb, yy, gr, ggi, ga, gmult), carried = _mixer_forward(
            xs, proj, mods[l], vecs[l], wgs[l], wl["w_a_out"], wl["w_b_out"], wl["w_o"], f"mixer_forward_{l}",
            gather_carry(3 * l + 1))
        gathered(3 * l + 1, carried)
        (x2, h2, up, y2, *loss_tile), carried = _mlp_forward(
            x1, mods[l], vecs[l], wl["w_mlp_up"], wl["w_mlp_down"], f"mlp_forward_{l}", gather_carry(3 * l + 2),
            target=loss_target[0] if l == n_layer - 1 else None)
        gathered(3 * l + 2, carried)
        saved.append(dict(x=xs, h=h, proj=proj, x1=x1, conva=conva, xb2=xb2, hh=hh, abm=abm, pa=pa, pb=pb,
                          y=yy, r=gr, gi=ggi, a=ga, mult=gmult, dgel=dgel, h2=h2, up=up, y2=y2))
        xs = x2
    dxs = xs
    loss_block = jnp.pad(loss_tile[0], ((0, 0), (0, d - LANES)))

    chips_q = [q_me ^ 2, q_me ^ 1, q_me ^ 3]
    pf = jnp.stack([mc, q_me] + chips_q).astype(jnp.int32)
    rs = dict(grad={}, landed={}, to_send={}, from_chips={}, out={})
    to_exchange, to_scatter, to_join, to_gather = [], [], [], []
    small_own, small_all = {}, {}

    def ride(call, what, name=None):
        ex = list(to_exchange) if "x" in what else []
        sc = list(to_scatter) if "s" in what else []
        ga = list(to_gather) if "g" in what else []
        jn = []
        for key in (to_join if "j" in what else []):
            if key[0] not in [k[0] for k in jn]:
                jn.append(key)
        carries = []
        if ex:
            carries.append(_exchange_carry([rs["grad"][k] for k in ex]))
        if sc:
            carries.append(_scatter_carry([rs["to_send"][k] for k in sc]))
        if jn:
            carries.append(_join_carry([rs["out"][k[0]] for k in jn], [k[1] for k in jn]))
        if ga:
            carries.append(_allgather_carry([small_own[k] for k in ga]))
        if call is None:
            carried = _run_carries(carries, name) if carries else []
            res = None
        else:
            res, carried = call(carries)
        carried = list(carried)
        if ex:
            for k, ld in zip(ex, carried.pop(0)):
                to_exchange.remove(k)
                rs["landed"][k] = ld
                rs["to_send"][k] = _add_sibling_half(rs["grad"][k], ld, pf, f"rs_add_sibling_{k[0]}_{k[1]}")
                to_scatter.append(k)
        if sc:
            for k, fc in zip(sc, carried.pop(0)):
                to_scatter.remove(k)
                rs["out"][k[0]] = _add_chips(rs["grad"][k], rs["landed"][k], fc, pf, rs["out"].get(k[0]), k[1], n_layer,
                                             f"rs_add_chips_{k[0]}_{k[1]}")
                to_join.append(k)
        if jn:
            for k, o in zip(jn, carried.pop(0)):
                to_join.remove(k)
                rs["out"][k[0]] = o
        if ga:
            for k, o in zip(ga, carried.pop(0)):
                to_gather.remove(k)
                small_all[k] = o
        return res

    def gather_small(key, parts):
        small_own[key] = parts[0] if len(parts) == 1 else jnp.concatenate(parts, axis=0)
        to_gather.append(key)

    def ready(nm, l, g):
        rs["grad"][(nm, l)] = g
        to_exchange.append((nm, l))

    rowblk = lambda t: t.reshape(N_CHIP, t.shape[1] // N_CHIP, t.shape[2])
    small1_prev = None
    for l in reversed(range(n_layer)):
        wl, sv = wfull[l], saved[l]
        dx1, dy2, dup, act, small3 = ride(lambda cr: _mlp_backward(
            dxs, sv["x1"], sv["y2"], sv["up"], mods[l], vecs[l], wl["w_mlp_up"], wl["w_mlp_down"], f"mlp_backward_{l}", cr), "xsjg")
        ready("w_mlp_up", l, _weight_grad(sv["h2"], dup, f"grad_w_mlp_up_{l}", col_blocks=N_CHIP)[0])
        g_down = ride(lambda cr: _weight_grad(act, dy2, f"grad_w_mlp_down_{l}", carries=cr), "x")
        ready("w_mlp_down", l, rowblk(g_down))
        dproj, dab, small2, dwg = ride(lambda cr: _mixer_backward(
            dx1, sv["proj"], sv["conva"], sv["xb2"], sv["hh"], sv["pa"], sv["pb"], sv["y"],
            sv["r"], sv["gi"], sv["a"], sv["mult"], sv["dgel"], mods[l], vecs[l], wgs[l],
            wl["w_a_out"], wl["w_b_out"], wl["w_o"], f"mixer_backward_{l}", cr), "xsjg")
        gather_small(("late", l, "s"), ([small1_prev] if small1_prev is not None else []) + [small2, small3])
        gather_small(("late", l, "w"), [dwg.reshape(2 * bw, d).astype(BF16)])
        g_in = ride(lambda cr: _weight_grad(sv["h"], dproj, f"grad_w_in_{l}", col_blocks=N_CHIP, carries=cr), "xsj")
        ready("w_in", l, g_in)
        g_abo = ride(lambda cr: _weight_grad_stacked(sv["abm"], dab, f"grad_w_abo_{l}", carries=cr), "xg")
        ready("w_abo", l, g_abo)
        dxs, small1_prev = ride(lambda cr: _proj_backward(dproj, dx1, sv["x"], mods[l], vecs[l], wl["w_in"],
                                                          f"proj_backward_{l}", cr), "xsjg")
    grad_x = dxs[None]
    gather_small(("last", 0, "s"), [small1_prev, loss_block])

    tail = 0
    while to_exchange or to_scatter or to_join or to_gather:
        ride(None, "xsjg", f"rs_tail_{tail}")
        tail += 1
    grads, deltas, new_m, new_v = {}, {}, {}, {}

    def adam(nms, copy_grad=False):
        items = [(weights[nm], grads[nm], mom1[nm], mom2[nm]) for nm in nms]
        res, _ = _adamw(items, "adamw_" + "_".join(nms), copy_grad)
        for nm, r in zip(nms, res):
            deltas[nm], new_m[nm], new_v[nm] = r[:3]
            if copy_grad:
                grads[nm] = r[3]

    for nms in (["w_mlp_up", "w_mlp_down"], ["w_in"]):
        for nm in nms:
            grads[nm] = rs["out"][nm].reshape(weights[nm].shape)
        adam(nms, True)

    sums ={k: _sum_devices(small_all[k], small_own[k], me_arr, f"sum_small_{k[0]}_{k[1]}_{k[2]}") for k in small_own}

    loss = sums[("last", 0, "s")][SUBLANES, 0]
    small_full = {}

    def rows_of(l, part):
        if part == 0:
            return (("late", l - 1, "s"), 0) if l >= 1 else (("last", 0, "s"), 0)
        if part == 3:
            return ("late", l, "w"), 0
        base = SUBLANES if l < n_layer - 1 else 0
        return ("late", l, "s"), base + (0, 0, 2 * SUBLANES)[part]

    def summed(l, part, row, n_rows=1):
        key, base = rows_of(l, part)
        return sums[key][base + row:base + row + n_rows]

    def per_device(l, part, row):
        key, base = rows_of(l, part)
        own = small_own[key]
        if key not in small_full:
            small_full[key] = lax.dynamic_update_slice(small_all[key], own, (me_dev * own.shape[0], 0)).reshape(
                (N_DEV,) + own.shape)
        return small_full[key][:, base + row:base + row + 1]

    mod_rows = [(0, SB1_DSH), (0, SB1_DSC), (1, SB2_DGT), (2, SB3_DSH), (2, SB3_DSC), (2, SB3_DGT)]
    dmod_all = jnp.stack([jnp.concatenate([per_device(l, p, r)[:, 0, :] for p, r in mod_rows], axis=1)
                          for l in range(n_layer)], axis=0)
    o1, o2, o3, o4 = 0, SUBLANES, 3 * SUBLANES, 4 * SUBLANES
    small_sum = jnp.stack([jnp.concatenate([summed(l, 0, 0, SUBLANES), summed(l, 1, 0, 2 * SUBLANES),
                                            summed(l, 2, 0, SUBLANES), summed(l, 3, 0, 2 * bw)], axis=0)
                           for l in range(n_layer)], axis=0)
    mod_rows_of = [o1 + SB1_DSH, o1 + SB1_DSC, o2 + SB2_DGT, o3 + SB3_DSH, o3 + SB3_DSC, o3 + SB3_DGT]
    grads["w_mod"] = _mod_backward(c_all.T, lax.dynamic_slice_in_dim(dmod_all, q_me * mq, mq, axis=2), "mod_backward")
    grads["b_mod"] = jnp.concatenate([small_sum[:, k, :] for k in mod_rows_of], axis=1)
    grads["g_pre_mix"] = small_sum[:, o1 + SB1_DG_PRE]
    grads["g_post_mix"] = small_sum[:, o2 + SB2_DG_POST]
    grads["conv_a_w"] = lax.dynamic_slice_in_dim(small_sum[:, o2 + SB2_DWA:o2 + SB2_DWA + ka], q_me * dq, dq, axis=2)
    grads["conv_a_b"] = small_sum[:, o2 + SB2_DBA]
    grads["conv_b_w"] = lax.dynamic_slice_in_dim(small_sum[:, o2 + SB2_DWB:o2 + SB2_DWB + kb], q_me * dq, dq, axis=2)
    grads["conv_b_b"] = small_sum[:, o2 + SB2_DBB]
    grads["lru_lambda"] = small_sum[:, o2 + SB2_DLAM]
    grads["b_gate_r"] = small_sum[:, o2 + SB2_DBR]
    grads["b_gate_i"] = small_sum[:, o2 + SB2_DBI]
    grads["g_pre_mlp"] = small_sum[:, o3 + SB3_DG_PRE]
    grads["g_post_mlp"] = small_sum[:, o3 + SB3_DG_POST]
    dwg_sum = small_sum[:, o4:].reshape(n_layer, n_head, bw, 2 * bw)
    grads["w_gate_r"] = dwg_sum[..., :bw]
    grads["w_gate_i"] = dwg_sum[..., bw:]

    for k, nm in enumerate(groups[1]):
        grads[nm] = rs["out"]["w_abo"][:, k * dq:(k + 1) * dq]

    by_shape = {}
    for nm in names:
        if nm not in deltas:
            by_shape.setdefault(weights[nm].shape, []).append(nm)
    for nms in by_shape.values():
        adam(nms)
    return (loss, grad_x, *[grads[nm] for nm in names], *[deltas[nm] for nm in names],
            *[new_m[nm] for nm in names], *[new_v[nm] for nm in names])
```

```python
import jax
import jax.numpy as jnp
from jax import lax
from jax.experimental import pallas as pl
from jax.experimental.pallas import tpu as pltpu

F32 = jnp.float32
BF16 = jnp.bfloat16
MESH = pl.DeviceIdType.MESH

EPS = 1e-6
LRU_C = 8.0
N_CHIP = 4
N_DEV = 8
ADAM_LR = 0.001
ADAM_B1 = 0.9
ADAM_B2 = 0.999
ADAM_EPS = 1e-08
ADAM_WD = 0.01
ADAM_STEP = 10

VMEM_LIMIT_BYTES = 56 * 1024 * 1024
SUBLANES = 8
LANES = 128
TOKENS_MATMUL_TILE = 512
TOKENS_MIXER_TILE = 256
GELU_K0 = 0.7978845608028654
GELU_K1 = 0.044715

V_G_PRE_MIX, V_G_POST_MIX, V_CONV_A_B, V_CONV_B_B, V_B_GATE_R, V_B_GATE_I, V_LAMBDA, V_G_PRE_MLP, V_G_POST_MLP = range(9)
V_CONV_A_W = 9
V_CONV_B_W = 12
M_SH_M, M_SC_M, M_GT_M, M_SH_F, M_SC_F, M_GT_F = range(6)


def _cparams(n_grid=0):
    sem = ("arbitrary",) * n_grid if n_grid else None
    return pltpu.CompilerParams(dimension_semantics=sem, vmem_limit_bytes=VMEM_LIMIT_BYTES)


def _full(shape):
    return pl.BlockSpec(shape, lambda *_: (0,) * len(shape))


def _dot(a, b):
    return jnp.dot(a, b, preferred_element_type=F32)


def _dot_tb(a, b):
    return lax.dot_general(a, b, (((1,), (1,)), ((), ())), preferred_element_type=F32)


def _dot_ta(a, b):
    return lax.dot_general(a, b, (((0,), (0,)), ((), ())), preferred_element_type=F32)


def _sigmoid(x):
    return 1.0 / (1.0 + jnp.exp(-x))


def _softplus(x):
    return jnp.maximum(x, 0.0) + jnp.log1p(jnp.exp(-jnp.abs(x)))


def _neg_expm1(x):
    series = -x * (1.0 + 0.5 * x * (1.0 + (x / 3.0) * (1.0 + 0.25 * x)))
    return jnp.where(x > -1e-2, series, 1.0 - jnp.exp(x))


def _gelu_and_grad(x):
    x2 = x * x
    s = _sigmoid(x * (2.0 * GELU_K0 + (2.0 * GELU_K0 * GELU_K1) * x2))
    gel = x * s
    return gel, s + gel * (1.0 - s) * (2.0 * GELU_K0 + (6.0 * GELU_K0 * GELU_K1) * x2)


def _rms(x):
    r = lax.rsqrt(jnp.mean(x * x, axis=-1, keepdims=True) + EPS)
    return x * r, r


def _rms_bwd(dxn, xn, r):
    return r * (dxn - xn * jnp.mean(dxn * xn, axis=-1, keepdims=True))


def _colsum(x):
    return jnp.sum(x, axis=0, keepdims=True)


def _rows(t, w):
    return lax.broadcasted_iota(jnp.int32, (t, w), 0)


def _shift_down(x, k, prev8):
    t, w = x.shape
    rolled = pltpu.roll(x, k, 0)
    head = jnp.where(_rows(SUBLANES, w) < k, pltpu.roll(prev8, k, 0), rolled[:SUBLANES])
    return jnp.concatenate([head, rolled[SUBLANES:]], axis=0)


def _shift_up(x, k, next8):
    t, w = x.shape
    rolled = pltpu.roll(x, t - k, 0)
    tail = jnp.where(_rows(SUBLANES, w) >= SUBLANES - k, pltpu.roll(next8, SUBLANES - k, 0), rolled[t - SUBLANES:])
    return jnp.concatenate([rolled[:t - SUBLANES], tail], axis=0)


SCAN_GROUP = 16


def _scan_steps(a, b, group, reverse):
    t, w = a.shape
    pos = _rows(t, w) & (group - 1)
    s = 1
    while s < group:
        keep = (pos < group - s) if reverse else (pos >= s)
        shift = (t - s) if reverse else s
        b = b + a * jnp.where(keep, pltpu.roll(b, shift, 0), 0.0)
        a = a * jnp.where(keep, pltpu.roll(a, shift, 0), 1.0)
        s *= 2
    return b, a


def _scan_two_level(a, b, carry_row, a_buf, b_buf, c_buf, reverse):
    t, w = a.shape
    grp = SCAN_GROUP
    n_grp = t // grp
    h_loc, a_cum = _scan_steps(a, b, grp, reverse)
    end = 0 if reverse else grp - 1
    a_end, h_end = [], []
    for j in range(w // LANES):
        a_buf[j] = a_cum[:, j * LANES:(j + 1) * LANES]
        b_buf[j] = h_loc[:, j * LANES:(j + 1) * LANES]
        a_end.append(a_buf[j, pl.ds(end, n_grp, stride=grp), :])
        h_end.append(b_buf[j, pl.ds(end, n_grp, stride=grp), :])
    a_end = jnp.concatenate(a_end, axis=1)
    h_end = jnp.concatenate(h_end, axis=1)
    h_grp, a_grp = _scan_steps(a_end, h_end, n_grp, reverse)
    h_grp = h_grp + a_grp * carry_row
    rows = _rows(n_grp, w)
    if reverse:
        entering = jnp.where(rows == n_grp - 1, carry_row, pltpu.roll(h_grp, n_grp - 1, 0))
    else:
        entering = jnp.where(rows == 0, carry_row, pltpu.roll(h_grp, 1, 0))
    c_buf[...] = entering
    out = [h_loc[g * grp:(g + 1) * grp] + a_cum[g * grp:(g + 1) * grp] * c_buf[g:g + 1, :] for g in range(n_grp)]
    return jnp.concatenate(out, axis=0)


def _row_tile(rows, cols, itemsize=4, target_bytes=2 * 1024 * 1024):
    if rows * cols * itemsize <= target_bytes or rows % SUBLANES:
        return rows
    t = max(SUBLANES, (target_bytes // (cols * itemsize)) // SUBLANES * SUBLANES)
    while rows % t:
        t -= SUBLANES
    return t


def _place():
    return lax.axis_index("x"), lax.axis_index("y"), lax.axis_index("c")


def _other_chips(x, y):
    chips = [(1 - x, y), (x, 1 - y), (1 - x, 1 - y)]
    return chips, [2 * cx + cy for cx, cy in chips]


def _all_gather_small(block, name):
    m_per, n = block.shape

    def body(x_ref, out_ref, send_sems, recv_sems, local_sem):
        x, y, c = _place()
        me, sibling = (x, y, c), (x, y, 1 - c)
        chips, _ = _other_chips(x, y)

        def rows(px, py, pc):
            return out_ref.at[pl.ds((4 * px + 2 * py + pc) * m_per, m_per), :]

        def copy(k, blk, to, src=None):
            return pltpu.make_async_remote_copy(
                src_ref=rows(*blk) if src is None else src, dst_ref=rows(*blk),
                send_sem=send_sems.at[k], recv_sem=recv_sems.at[k], device_id=to, device_id_type=MESH)

        mine = pltpu.make_async_copy(x_ref, rows(*me), local_sem)
        mine.start()
        first = [copy(0, me, sibling, src=x_ref)]
        first += [copy(1 + j, me, (*chip, c), src=x_ref) for j, chip in enumerate(chips)]
        for cp in first:
            cp.start()
        passed = [copy(4 + j, (*chip, c), sibling) for j, chip in enumerate(chips)]
        for j, chip in enumerate(chips):
            copy(1 + j, (*chip, c), me).wait_recv()
            passed[j].start()
        copy(0, sibling, me).wait_recv()
        for j, chip in enumerate(chips):
            copy(4 + j, (*chip, 1 - c), me).wait_recv()
        for cp in first + passed:
            cp.wait_send()
        mine.wait()

    return pl.pallas_call(
        body, name=name,
        out_shape=jax.ShapeDtypeStruct((N_DEV * m_per, n), block.dtype),
        in_specs=[pl.BlockSpec(memory_space=pltpu.VMEM)],
        out_specs=pl.BlockSpec(memory_space=pltpu.VMEM),
        scratch_shapes=[pltpu.SemaphoreType.DMA((7,)), pltpu.SemaphoreType.DMA((7,)), pltpu.SemaphoreType.DMA],
        compiler_params=pltpu.CompilerParams(vmem_limit_bytes=VMEM_LIMIT_BYTES),
    )(block)


class _Carry:
    def __init__(self, ins, out_shapes, aliases, sem_shapes, start, finish, mid=None, mid_frac=0.85):
        self.ins, self.out_shapes, self.aliases, self.sem_shapes = list(ins), list(out_shapes), dict(aliases), list(sem_shapes)
        self.start, self.mid, self.finish, self.mid_frac = start, mid, finish, mid_frac


def _pcall(body, *, name, grid, in_specs, out_specs, out_shape, args, scratch_shapes=(), carries=(), prefetch=()):
    in_specs, out_specs, out_shape = list(in_specs), list(out_specs), list(out_shape)
    scratch_shapes, args = list(scratch_shapes), list(args)
    n_in, n_out, n_scr, n_pre = len(in_specs), len(out_shape), len(scratch_shapes), len(prefetch)
    steps = 1
    for g in grid:
        steps *= g
    any_spec = pl.BlockSpec(memory_space=pl.ANY)
    aliases = {}
    spans = []
    for cr in carries:
        spans.append((len(args), len(out_shape), len(scratch_shapes)))
        for a, b in cr.aliases.items():
            aliases[n_pre + len(args) + a] = len(out_shape) + b
        args += cr.ins
        in_specs += [any_spec] * len(cr.ins)
        out_shape += cr.out_shapes
        out_specs += [any_spec] * len(cr.out_shapes)
        scratch_shapes += cr.sem_shapes
    n_all_in = len(args)
    n_all_out = len(out_shape)

    def wrapped(*refs):
        pre, refs = refs[:n_pre], refs[n_pre:]
        ins, outs, scr = refs[:n_all_in], refs[n_all_in:n_all_in + n_all_out], refs[n_all_in + n_all_out:]
        parts = [(cr, ins[a:a + len(cr.ins)], outs[b:b + len(cr.out_shapes)], scr[s:s + len(cr.sem_shapes)])
                 for cr, (a, b, s) in zip(carries, spans)]
        lin = 0
        for ax, g in enumerate(grid):
            lin = lin * g + pl.program_id(ax)

        def at(step, fn):
            if steps == 1:
                fn()
            else:
                pl.when(lin == step)(fn)

        def start_all():
            for cr, ci, co, cs in parts:
                cr.start(ci, co, cs)

        def finish_all():
            for cr, ci, co, cs in parts:
                cr.finish(ci, co, cs)

        if parts:
            at(0, start_all)
        body(*pre, *ins[:n_in], *outs[:n_out], *scr[:n_scr])
        for cr, ci, co, cs in parts:
            if cr.mid is not None:
                at(min(steps - 1, int(steps * cr.mid_frac)), lambda cr=cr, ci=ci, co=co, cs=cs: cr.mid(ci, co, cs))
        if parts:
            at(steps - 1, finish_all)

    if n_pre:
        res = pl.pallas_call(
            wrapped, name=name, out_shape=out_shape,
            grid_spec=pltpu.PrefetchScalarGridSpec(num_scalar_prefetch=n_pre, grid=tuple(grid), in_specs=in_specs,
                                                   out_specs=out_specs, scratch_shapes=scratch_shapes),
            input_output_aliases=aliases, compiler_params=_cparams(len(grid)),
        )(*prefetch, *args)
    else:
        res = pl.pallas_call(
            wrapped, name=name, grid=tuple(grid), out_shape=out_shape, in_specs=in_specs, out_specs=out_specs,
            scratch_shapes=scratch_shapes, input_output_aliases=aliases, compiler_params=_cparams(len(grid)),
        )(*args)
    res = list(res)
    return res[:n_out], [res[b:b + len(cr.out_shapes)] for cr, (_, b, _) in zip(carries, spans)]


def _run_carries(carries, name):
    return _pcall(lambda: None, name=name, grid=(), in_specs=[], out_specs=[], out_shape=[], args=[], carries=carries)[1]


CAST_STEPS = 8


def _cast_place_all(shards, q_arr, name, carries=()):
    n = len(shards)

    def body(q_ref, *refs):
        for k in range(n):
            refs[n + k][...] = refs[k][...].astype(BF16)

    def spec_in(k):
        w, layer = shards[k]
        return pl.BlockSpec((1, w.shape[1] // CAST_STEPS, w.shape[2]), lambda i, q_ref: (layer, i, 0))

    def spec_out(k):
        w, _ = shards[k]
        return pl.BlockSpec((1, w.shape[1] // CAST_STEPS, w.shape[2]), lambda i, q_ref: (q_ref[0], i, 0))

    return _pcall(
        body, name=name, grid=(CAST_STEPS,),
        out_shape=[jax.ShapeDtypeStruct((N_CHIP,) + w.shape[1:], BF16) for w, _ in shards],
        in_specs=[spec_in(k) for k in range(n)], out_specs=[spec_out(k) for k in range(n)],
        args=[w for w, _ in shards], carries=carries, prefetch=[q_arr])


def _gather_carry(bufs, mid_frac=0.85):
    n = len(bufs)

    def copies(o_refs, sems):
        send_sems, recv_sems = sems
        x, y, c = _place()
        q = 2 * x + y
        sibling = (x, y, 1 - c)
        chips, qs = _other_chips(x, y)

        def half(w, shard, pc):
            rh = bufs[w].shape[1] // 2
            return o_refs[w].at[shard, pl.ds(pc * rh, rh), :]

        def over_ici(w, j, shard):
            blk = half(w, shard, c)
            return pltpu.make_async_remote_copy(
                src_ref=blk, dst_ref=blk, send_sem=send_sems.at[w, j], recv_sem=recv_sems.at[w, j],
                device_id=(*chips[j], c), device_id_type=MESH)

        def to_sibling(w, j, pc):
            blk = half(w, qs[j], pc)
            return pltpu.make_async_remote_copy(
                src_ref=blk, dst_ref=blk, send_sem=send_sems.at[w, 3 + j], recv_sem=recv_sems.at[w, 3 + j],
                device_id=sibling, device_id_type=MESH)

        return q, c, qs, over_ici, to_sibling

    pairs = [(w, j) for w in range(n) for j in range(3)]

    def start(i_refs, o_refs, sems):
        q, _, _, over_ici, _ = copies(o_refs, sems)
        for w, j in pairs:
            over_ici(w, j, q).start()

    def mid(i_refs, o_refs, sems):
        _, c, qs, over_ici, to_sibling = copies(o_refs, sems)
        for w, j in pairs:
            over_ici(w, j, qs[j]).wait_recv()
            to_sibling(w, j, c).start()

    def finish(i_refs, o_refs, sems):
        q, c, _, over_ici, to_sibling = copies(o_refs, sems)
        for w, j in pairs:
            to_sibling(w, j, 1 - c).wait_recv()
        for w, j in pairs:
            over_ici(w, j, q).wait_send()
            to_sibling(w, j, c).wait_send()

    return _Carry(bufs, [jax.ShapeDtypeStruct(b.shape, b.dtype) for b in bufs], {w: w for w in range(n)},
                  [pltpu.SemaphoreType.DMA((n, 6)), pltpu.SemaphoreType.DMA((n, 6))], start, finish, mid, mid_frac)


def _exchange_carry(grads):
    n = len(grads)

    def copies(g_refs, l_refs, sems):
        send_sems, recv_sems = sems
        x, y, c = _place()
        out = []
        for w in range(n):
            rh = grads[w].shape[1] // 2
            out.append(pltpu.make_async_remote_copy(
                src_ref=g_refs[w].at[:, pl.ds((1 - c) * rh, rh), :], dst_ref=l_refs[w],
                send_sem=send_sems.at[w], recv_sem=recv_sems.at[w], device_id=(x, y, 1 - c), device_id_type=MESH))
        return out

    def start(g_refs, l_refs, sems):
        for cp in copies(g_refs, l_refs, sems):
            cp.start()

    def finish(g_refs, l_refs, sems):
        for cp in copies(g_refs, l_refs, sems):
            cp.wait()

    return _Carry(grads, [jax.ShapeDtypeStruct((N_CHIP, g.shape[1] // 2, g.shape[2]), g.dtype) for g in grads], {},
                  [pltpu.SemaphoreType.DMA((n,)), pltpu.SemaphoreType.DMA((n,))], start, finish)


def _scatter_carry(sums):
    n = len(sums)

    def copies(s_refs, l_refs, sems):
        send_sems, recv_sems = sems
        x, y, c = _place()
        chips, _ = _other_chips(x, y)
        return [pltpu.make_async_remote_copy(
            src_ref=s_refs[w].at[j], dst_ref=l_refs[w].at[j], send_sem=send_sems.at[w, j], recv_sem=recv_sems.at[w, j],
            device_id=(*chips[j], c), device_id_type=MESH) for w in range(n) for j in range(3)]

    def start(s_refs, l_refs, sems):
        for cp in copies(s_refs, l_refs, sems):
            cp.start()

    def finish(s_refs, l_refs, sems):
        for cp in copies(s_refs, l_refs, sems):
            cp.wait()

    return _Carry(sums, [jax.ShapeDtypeStruct(s.shape, s.dtype) for s in sums], {},
                  [pltpu.SemaphoreType.DMA((n, 3)), pltpu.SemaphoreType.DMA((n, 3))], start, finish)


def _join_carry(outs, layers):
    n = len(outs)

    def copy(o_refs, sems, w, mine):
        send_sems, recv_sems = sems
        x, y, c = _place()
        r = outs[w].shape[1]
        rows = o_refs[w].at[layers[w], pl.ds((c if mine else 1 - c) * (r // 2), r // 2), :]
        return pltpu.make_async_remote_copy(
            src_ref=rows, dst_ref=rows, send_sem=send_sems.at[w], recv_sem=recv_sems.at[w],
            device_id=(x, y, 1 - c), device_id_type=MESH)

    def start(i_refs, o_refs, sems):
        for w in range(n):
            copy(o_refs, sems, w, True).start()

    def finish(i_refs, o_refs, sems):
        for w in range(n):
            copy(o_refs, sems, w, True).wait_send()
        for w in range(n):
            copy(o_refs, sems, w, False).wait_recv()

    return _Carry(outs, [jax.ShapeDtypeStruct(o.shape, o.dtype) for o in outs], {w: w for w in range(n)},
                  [pltpu.SemaphoreType.DMA((n,)), pltpu.SemaphoreType.DMA((n,))], start, finish)


PF_C, PF_Q, PF_QS = 0, 1, 2


def _add_sibling_half(g, landed, pf, name):
    _, r, cols = g.shape
    rh = r // 2
    tr = _row_tile(rh, cols)
    nr = rh // tr

    def body(pf_ref, g_ref, l_ref, o_ref):
        o_ref[...] = (g_ref[...] + l_ref[...]).astype(BF16)

    return pl.pallas_call(
        body, name=name,
        out_shape=jax.ShapeDtypeStruct((3, rh, cols), BF16),
        grid_spec=pltpu.PrefetchScalarGridSpec(
            num_scalar_prefetch=1, grid=(3, nr),
            in_specs=[pl.BlockSpec((1, tr, cols), lambda j, i, pf_ref: (pf_ref[PF_QS + j], pf_ref[PF_C] * nr + i, 0)),
                      pl.BlockSpec((1, tr, cols), lambda j, i, pf_ref: (pf_ref[PF_QS + j], i, 0))],
            out_specs=pl.BlockSpec((1, tr, cols), lambda j, i, pf_ref: (j, i, 0))),
        compiler_params=_cparams(2),
    )(pf, g, landed)


def _add_chips(g, landed, from_chips, pf, prev, layer, n_layer, name):
    _, r, cols = g.shape
    rh = r // 2
    tr = _row_tile(rh, cols)
    nr = rh // tr

    def body(pf_ref, g_ref, l_ref, f_ref, *rest):
        o_ref = rest[-1]
        acc = g_ref[0] + l_ref[0]
        for j in range(3):
            acc = acc + f_ref[j].astype(F32)
        o_ref[0] = acc

    in_specs = [pl.BlockSpec((1, tr, cols), lambda i, pf_ref: (pf_ref[PF_Q], pf_ref[PF_C] * nr + i, 0)),
                pl.BlockSpec((1, tr, cols), lambda i, pf_ref: (pf_ref[PF_Q], i, 0)),
                pl.BlockSpec((3, tr, cols), lambda i, pf_ref: (0, i, 0))]
    args = [pf, g, landed, from_chips]
    aliases = {}
    if prev is not None:
        in_specs.append(pl.BlockSpec(memory_space=pl.ANY))
        args.append(prev)
        aliases = {4: 0}
    return pl.pallas_call(
        body, name=name,
        out_shape=jax.ShapeDtypeStruct((n_layer, r, cols), F32),
        grid_spec=pltpu.PrefetchScalarGridSpec(
            num_scalar_prefetch=1, grid=(nr,), in_specs=in_specs,
            out_specs=pl.BlockSpec((1, tr, cols), lambda i, pf_ref: (layer, pf_ref[PF_C] * nr + i, 0))),
        input_output_aliases=aliases,
        compiler_params=_cparams(1),
    )(*args)


def _allgather_carry(blocks):
    n = len(blocks)

    def copies(b_refs, o_refs, sems):
        send_sems, recv_sems = sems
        x, y, c = _place()
        chips, _ = _other_chips(x, y)

        def place(w, px, py, pc):
            m = blocks[w].shape[0]
            return o_refs[w].at[pl.ds((4 * px + 2 * py + pc) * m, m), :]

        def own_to(w, k, to):
            dst = place(w, x, y, c)
            return pltpu.make_async_remote_copy(src_ref=b_refs[w], dst_ref=dst, send_sem=send_sems.at[w, k],
                                                recv_sem=recv_sems.at[w, k], device_id=to, device_id_type=MESH)

        def landed_from(w, k, px, py, pc):
            blk = place(w, px, py, pc)
            return pltpu.make_async_remote_copy(src_ref=blk, dst_ref=blk, send_sem=send_sems.at[w, k],
                                                recv_sem=recv_sems.at[w, k], device_id=(x, y, 1 - c), device_id_type=MESH)

        return x, y, c, chips, own_to, landed_from

    def start(b_refs, o_refs, sems):
        x, y, c, chips, own_to, _ = copies(b_refs, o_refs, sems)
        for w in range(n):
            own_to(w, 0, (x, y, 1 - c)).start()
            for j, chip in enumerate(chips):
                own_to(w, 1 + j, (*chip, c)).start()

    def mid(b_refs, o_refs, sems):
        x, y, c, chips, _, landed_from = copies(b_refs, o_refs, sems)
        for w in range(n):
            for j, chip in enumerate(chips):
                landed_from(w, 1 + j, *chip, c).wait_recv()
                landed_from(w, 4 + j, *chip, c).start()

    def finish(b_refs, o_refs, sems):
        x, y, c, chips, own_to, landed_from = copies(b_refs, o_refs, sems)
        for w in range(n):
            landed_from(w, 0, x, y, 1 - c).wait_recv()
            for j, chip in enumerate(chips):
                landed_from(w, 4 + j, *chip, 1 - c).wait_recv()
            own_to(w, 0, (x, y, 1 - c)).wait_send()
            for j, chip in enumerate(chips):
                own_to(w, 1 + j, (*chip, c)).wait_send()
                landed_from(w, 4 + j, *chip, c).wait_send()

    return _Carry(blocks, [jax.ShapeDtypeStruct((N_DEV * b.shape[0], b.shape[1]), b.dtype) for b in blocks], {},
                  [pltpu.SemaphoreType.DMA((n, 7)), pltpu.SemaphoreType.DMA((n, 7))], start, finish, mid)


def _sum_devices(gathered, own, me_arr, name):
    m, n = own.shape
    tr = _row_tile(m, n, itemsize=own.dtype.itemsize, target_bytes=256 * 1024)
    nr = m // tr

    def body(me_ref, *refs):
        g_refs, own_ref, o_ref = refs[:N_DEV], refs[N_DEV], refs[N_DEV + 1]
        me = me_ref[0]
        acc = None
        for dev in range(N_DEV):
            term = jnp.where(me == dev, own_ref[...], g_refs[dev][...]).astype(F32)
            acc = term if acc is None else acc + term
        o_ref[...] = acc

    def dev_rows(dev):
        return pl.BlockSpec((tr, n), lambda i, me_ref: (dev * nr + i, 0))

    return pl.pallas_call(
        body, name=name,
        out_shape=jax.ShapeDtypeStruct((m, n), F32),
        grid_spec=pltpu.PrefetchScalarGridSpec(
            num_scalar_prefetch=1, grid=(nr,),
            in_specs=[dev_rows(dev) for dev in range(N_DEV)] + [pl.BlockSpec((tr, n), lambda i, me_ref: (i, 0))],
            out_specs=pl.BlockSpec((tr, n), lambda i, me_ref: (i, 0))),
        compiler_params=_cparams(1),
    )(me_arr, *([gathered] * N_DEV), own)


def _mod_forward(c_all, w_mod, b_mod_shard, name):
    n_layer, d, mq = w_mod.shape

    def body(c_ref, w_ref, b_ref, o_ref):
        cv = c_ref[...]
        o_ref[...] = _dot(cv * _sigmoid(cv), w_ref[0]) + b_ref[0]

    return pl.pallas_call(
        body, name=name, grid=(n_layer,),
        out_shape=jax.ShapeDtypeStruct((n_layer * N_DEV, mq), F32),
        in_specs=[_full((N_DEV, d)), pl.BlockSpec((1, d, mq), lambda l: (l, 0, 0)),
                  pl.BlockSpec((1, 1, mq), lambda l: (l, 0, 0))],
        out_specs=pl.BlockSpec((N_DEV, mq), lambda l: (l, 0)),
        compiler_params=_cparams(1),
    )(c_all, w_mod, b_mod_shard.reshape(n_layer, 1, mq))


def _mod_backward(c_all_t, dmod_shard, name):
    n_layer, _, mq = dmod_shard.shape
    d = c_all_t.shape[0]

    def body(c_ref, dm_ref, o_ref):
        cv = c_ref[...]
        o_ref[0] = _dot(cv * _sigmoid(cv), dm_ref[0])

    return pl.pallas_call(
        body, name=name, grid=(n_layer,),
        out_shape=jax.ShapeDtypeStruct((n_layer, d, mq), F32),
        in_specs=[_full((d, N_DEV)), pl.BlockSpec((1, N_DEV, mq), lambda l: (l, 0, 0))],
        out_specs=pl.BlockSpec((1, d, mq), lambda l: (l, 0, 0)),
        compiler_params=_cparams(1),
    )(c_all_t, dmod_shard)


def _norm_proj(x, mod, vec, w_in, name, carries=()):
    s, d = x.shape
    nq = w_in.shape[2]
    ts = min(TOKENS_MATMUL_TILE, s)

    def body(x_ref, mod_ref, vec_ref, w_ref, h_ref, p_ref, dgel_ref):
        xn, _ = _rms(x_ref[...])
        gm = vec_ref[V_G_PRE_MIX:V_G_PRE_MIX + 1, :] * (1.0 + mod_ref[M_SC_M:M_SC_M + 1, :])
        h = (xn * gm + mod_ref[M_SH_M:M_SH_M + 1, :]).astype(BF16)
        h_ref[...] = h
        for qb in range(N_CHIP):
            pq = _dot(h, w_ref[qb])
            for k in range(N_CHIP * nq // d):
                lo, hi = max(qb * nq, k * d), min((qb + 1) * nq, (k + 1) * d)
                if lo >= hi:
                    continue
                piece = pq[:, lo - qb * nq:hi - qb * nq]
                if k == 4:
                    piece, dgel = _gelu_and_grad(piece)
                    dgel_ref[:, lo - 4 * d:hi - 4 * d] = dgel.astype(BF16)
                elif k >= 5:
                    piece = _sigmoid(piece)
                p_ref[:, lo:hi] = piece.astype(BF16)

    tile = pl.BlockSpec((ts, d), lambda i: (i, 0))
    return _pcall(
        body, name=name, grid=(s // ts,),
        out_shape=[jax.ShapeDtypeStruct((s, d), BF16), jax.ShapeDtypeStruct((s, N_CHIP * nq), BF16),
                   jax.ShapeDtypeStruct((s, d), BF16)],
        in_specs=[tile, _full(mod.shape), _full(vec.shape), _full(w_in.shape)],
        out_specs=[tile, pl.BlockSpec((ts, N_CHIP * nq), lambda i: (i, 0)), tile],
        args=[x, mod, vec, w_in], carries=carries)


def _gate_pre(xb2_b, wg_ref, n_head, bw):
    zr, zi = [], []
    for hd in range(n_head):
        z = _dot(xb2_b[:, hd * bw:(hd + 1) * bw], wg_ref[hd])
        zr.append(z[:, :bw])
        zi.append(z[:, bw:])
    return jnp.concatenate(zr, axis=1), jnp.concatenate(zi, axis=1)


def _lru_coeffs(xb2, wg_ref, vec_ref, n_head, bw):
    zr, zi = _gate_pre(xb2.astype(BF16), wg_ref, n_head, bw)
    r = _sigmoid(zr + vec_ref[V_B_GATE_R:V_B_GATE_R + 1, :])
    gi = _sigmoid(zi + vec_ref[V_B_GATE_I:V_B_GATE_I + 1, :])
    sp = _softplus(-vec_ref[V_LAMBDA:V_LAMBDA + 1, :])
    log_a = (-LRU_C) * r * sp
    a = jnp.exp(log_a)
    mult = jnp.sqrt(_neg_expm1(2.0 * log_a))
    return r, gi, sp, a, mult


def _mixer_forward(x, proj, mod, vec, wg, w_a_out, w_b_out, w_o, name, carries=()):
    s, d = x.shape
    n_head, bw, _ = wg.shape
    ts = min(TOKENS_MIXER_TILE, s)

    def body(x_ref, p_ref, mod_ref, vec_ref, wg_ref, wa_ref, wb_ref, wo_ref,
             x1_ref, conva_ref, xb2_ref, hh_ref, abm_ref, pa_ref, pb_ref, y_ref,
             r_ref, gi_ref, a_ref, mult_ref,
             cv_tail, xb_tail, h_last, a_buf, b_buf, c_buf):
        i = pl.program_id(0)

        @pl.when(i == 0)
        def _():
            cv_tail[...] = jnp.zeros_like(cv_tail)
            xb_tail[...] = jnp.zeros_like(xb_tail)
            h_last[...] = jnp.zeros_like(h_last)

        def seg(k):
            return p_ref[:, k * d:(k + 1) * d].astype(F32)

        def vrow(k):
            return vec_ref[k:k + 1, :]

        b_a, c_a, v_a, x_b, gel, sa, sb = (seg(k) for k in range(7))
        cv = c_a * v_a
        prev_cv = cv_tail[...]
        conv_a = (vrow(V_CONV_A_B) + vrow(V_CONV_A_W) * _shift_down(cv, 2, prev_cv)
                  + vrow(V_CONV_A_W + 1) * _shift_down(cv, 1, prev_cv) + vrow(V_CONV_A_W + 2) * cv)
        cv_tail[...] = cv[ts - SUBLANES:]
        y_a = b_a * conv_a
        prev_xb = xb_tail[...]
        xb2 = (vrow(V_CONV_B_B) + vrow(V_CONV_B_W) * _shift_down(x_b, 3, prev_xb)
               + vrow(V_CONV_B_W + 1) * _shift_down(x_b, 2, prev_xb)
               + vrow(V_CONV_B_W + 2) * _shift_down(x_b, 1, prev_xb) + vrow(V_CONV_B_W + 3) * x_b)
        xb_tail[...] = x_b[ts - SUBLANES:]
        r, gi, _, a, mult = _lru_coeffs(xb2, wg_ref, vec_ref, n_head, bw)
        r_ref[...] = r
        gi_ref[...] = gi
        a_ref[...] = a
        mult_ref[...] = mult
        hh = _scan_two_level(a, mult * gi * xb2, h_last[SUBLANES - 1:SUBLANES, :], a_buf, b_buf, c_buf, reverse=False)
        h_last[...] = hh[ts - SUBLANES:]
        y_b = hh * gel
        ya_b, yb_b = y_a.astype(BF16), y_b.astype(BF16)
        pa = _dot(ya_b, wa_ref[...])
        pb = _dot(yb_b, wb_ref[...])
        m = (sa * pa + sb * pb).astype(BF16)
        y = _dot(m, wo_ref[...])
        yn, _ = _rms(y)
        gg = mod_ref[M_GT_M:M_GT_M + 1, :] * vrow(V_G_POST_MIX)
        x1_ref[...] = x_ref[...] + yn * gg
        conva_ref[...] = conv_a.astype(BF16)
        xb2_ref[...] = xb2
        hh_ref[...] = hh
        abm_ref[0] = ya_b
        abm_ref[1] = yb_b
        abm_ref[2] = m
        pa_ref[...] = pa.astype(BF16)
        pb_ref[...] = pb.astype(BF16)
        y_ref[...] = y.astype(BF16)

    tile = pl.BlockSpec((ts, d), lambda i: (i, 0))
    tile3 = pl.BlockSpec((3, ts, d), lambda i: (0, i, 0))
    sd = lambda dt: jax.ShapeDtypeStruct((s, d), dt)
    return _pcall(
        body, name=name, grid=(s // ts,),
        out_shape=[sd(F32), sd(BF16), sd(F32), sd(F32), jax.ShapeDtypeStruct((3, s, d), BF16), sd(BF16), sd(BF16), sd(BF16),
                   sd(F32), sd(F32), sd(F32), sd(F32)],
        in_specs=[tile, pl.BlockSpec((ts, 7 * d), lambda i: (i, 0)), _full(mod.shape), _full(vec.shape),
                  _full(wg.shape), _full(w_a_out.shape), _full(w_b_out.shape), _full(w_o.shape)],
        out_specs=[tile] * 4 + [tile3] + [tile] * 7,
        scratch_shapes=[pltpu.VMEM((SUBLANES, d), F32)] * 3 + [pltpu.VMEM((d // LANES, ts, LANES), F32)] * 2
                       + [pltpu.VMEM((ts // SCAN_GROUP, d), F32)],
        args=[x, proj, mod, vec, wg, w_a_out, w_b_out, w_o], carries=carries)


def _mlp_forward(x1, mod, vec, w_up, w_down, name, carries=(), target=None):
    s, d = x1.shape
    fq = w_up.shape[2]
    ts = min(TOKENS_MATMUL_TILE, s)

    def body(x_ref, *refs):
        if target is None:
            mod_ref, vec_ref, wu_ref, wd_ref, x2_ref, h2_ref, up_ref, y2_ref = refs
        else:
            t_ref, mod_ref, vec_ref, wu_ref, wd_ref, x2_ref, h2_ref, up_ref, y2_ref, loss_ref = refs
        x = x_ref[...]
        xn, _ = _rms(x)
        gm = vec_ref[V_G_PRE_MLP:V_G_PRE_MLP + 1, :] * (1.0 + mod_ref[M_SC_F:M_SC_F + 1, :])
        h2 = (xn * gm + mod_ref[M_SH_F:M_SH_F + 1, :]).astype(BF16)
        h2_ref[...] = h2
        y2 = jnp.zeros((ts, d), F32)
        for qb in range(N_CHIP):
            up = _dot(h2, wu_ref[qb])
            up_ref[:, qb * fq:(qb + 1) * fq] = up.astype(BF16)
            ru = jnp.maximum(up, 0.0)
            y2 = y2 + _dot((ru * ru).astype(BF16), wd_ref[qb])
        y2_ref[...] = y2.astype(BF16)
        yn, _ = _rms(y2)
        gg = mod_ref[M_GT_F:M_GT_F + 1, :] * vec_ref[V_G_POST_MLP:V_G_POST_MLP + 1, :]
        x2 = x + yn * gg
        if target is None:
            x2_ref[...] = x2
        else:
            @pl.when(pl.program_id(0) == 0)
            def _():
                loss_ref[...] = jnp.zeros_like(loss_ref)

            err = x2 - t_ref[...]
            x2_ref[...] = err * (1.0 / d)
            loss_ref[...] += jnp.sum(jnp.sum(err * err, axis=1, keepdims=True), axis=0, keepdims=True) * (0.5 / d)

    tile = pl.BlockSpec((ts, d), lambda i: (i, 0))
    last = target is not None
    return _pcall(
        body, name=name, grid=(s // ts,),
        out_shape=[jax.ShapeDtypeStruct((s, d), F32), jax.ShapeDtypeStruct((s, d), BF16),
                   jax.ShapeDtypeStruct((s, N_CHIP * fq), BF16), jax.ShapeDtypeStruct((s, d), BF16)]
                  + ([jax.ShapeDtypeStruct((SUBLANES, LANES), F32)] if last else []),
        in_specs=[tile] + ([tile] if last else []) + [_full(mod.shape), _full(vec.shape), _full(w_up.shape), _full(w_down.shape)],
        out_specs=[tile, tile, pl.BlockSpec((ts, N_CHIP * fq), lambda i: (i, 0)), tile]
                 + ([_full((SUBLANES, LANES))] if last else []),
        args=[x1] + ([target] if last else []) + [mod, vec, w_up, w_down], carries=carries)


SB3_DSH, SB3_DSC, SB3_DGT, SB3_DG_PRE, SB3_DG_POST = range(5)
SB1_DSH, SB1_DSC, SB1_DG_PRE = range(3)
(SB2_DGT, SB2_DG_POST, SB2_DWA, SB2_DBA, SB2_DWB, SB2_DBB, SB2_DLAM, SB2_DBR, SB2_DBI) = (0, 1, 2, 5, 6, 10, 11, 12, 13)


def _mlp_backward(dx2, x1, y2, up, mod, vec, w_up, w_down, name, carries=()):
    s, d = dx2.shape
    fq = w_up.shape[2]
    ts = min(TOKENS_MATMUL_TILE, s)
    n_t = s // ts

    def body(dx2_ref, x_ref, y2_ref, up_ref, mod_ref, vec_ref, wu_ref, wd_ref,
             dx1_ref, dy2_ref, dup_ref, small_ref):
        i = pl.program_id(0)

        @pl.when(i == 0)
        def _():
            small_ref[...] = jnp.zeros_like(small_ref)

        dout = dx2_ref[...]
        y2n, ry = _rms(y2_ref[...].astype(F32))
        g_post = vec_ref[V_G_POST_MLP:V_G_POST_MLP + 1, :]
        gt = mod_ref[M_GT_F:M_GT_F + 1, :]
        dgg = _colsum(dout * y2n)
        dy2 = _rms_bwd(dout * (gt * g_post), y2n, ry).astype(BF16)
        dy2_ref[...] = dy2
        dh2 = jnp.zeros((ts, d), F32)
        for qb in range(N_CHIP):
            cols = slice(qb * fq, (qb + 1) * fq)
            dact = _dot_tb(dy2, wd_ref[qb])
            ru = jnp.maximum(up_ref[:, cols].astype(F32), 0.0)
            dup = (dact * (2.0 * ru)).astype(BF16)
            dup_ref[:, cols] = dup
            dh2 = dh2 + _dot_tb(dup, wu_ref[qb])
        xn, r = _rms(x_ref[...])
        g_pre = vec_ref[V_G_PRE_MLP:V_G_PRE_MLP + 1, :]
        sc1 = 1.0 + mod_ref[M_SC_F:M_SC_F + 1, :]
        dsh = _colsum(dh2)
        dgm = _colsum(dh2 * xn)
        dx1_ref[...] = dout + _rms_bwd(dh2 * (g_pre * sc1), xn, r)
        small_ref[SB3_DSH:SB3_DSH + 1, :] += dsh
        small_ref[SB3_DSC:SB3_DSC + 1, :] += dgm
        small_ref[SB3_DGT:SB3_DGT + 1, :] += dgg

        @pl.when(i == n_t - 1)
        def _():
            dgm_t = small_ref[SB3_DSC:SB3_DSC + 1, :]
            dgg_t = small_ref[SB3_DGT:SB3_DGT + 1, :]
            small_ref[SB3_DSC:SB3_DSC + 1, :] = dgm_t * g_pre
            small_ref[SB3_DG_PRE:SB3_DG_PRE + 1, :] = dgm_t * sc1
            small_ref[SB3_DGT:SB3_DGT + 1, :] = dgg_t * g_post
            small_ref[SB3_DG_POST:SB3_DG_POST + 1, :] = dgg_t * gt

    tile = pl.BlockSpec((ts, d), lambda i: (i, 0))
    wide = pl.BlockSpec((ts, N_CHIP * fq), lambda i: (i, 0))
    return _pcall(
        body, name=name, grid=(n_t,),
        out_shape=[jax.ShapeDtypeStruct((s, d), F32), jax.ShapeDtypeStruct((s, d), BF16),
                   jax.ShapeDtypeStruct((s, N_CHIP * fq), BF16), jax.ShapeDtypeStruct((SUBLANES, d), F32)],
        in_specs=[tile, tile, tile, wide, _full(mod.shape), _full(vec.shape), _full(w_up.shape), _full(w_down.shape)],
        out_specs=[tile, tile, wide, _full((SUBLANES, d))],
        args=[dx2, x1, y2, up, mod, vec, w_up, w_down], carries=carries)


def _mixer_backward(dx1, proj, conva, xb2s, hhs, pas, pbs, ys, rs_, gis, as_, mults, dgels, mod, vec, wg, w_a_out, w_b_out,
                    w_o, name, carries=()):
    s, d = dx1.shape
    n_head, bw, _ = wg.shape
    ts = min(TOKENS_MIXER_TILE, s)
    n_t = s // ts

    def body(dx1_ref, p_ref, conva_ref, xb2_ref, hh_ref, pa_ref, pb_ref, y_ref, r_ref, gi_ref, a_ref, mult_ref, dgel_ref,
             mod_ref, vec_ref, wg_ref, wa_ref, wb_ref, wo_ref,
             dp_ref, dab_ref, small_ref, dwg_ref,
             dconv_head, dxb2_head, a_head, g_head, a_buf, b_buf, c_buf):
        i = pl.program_id(0)

        @pl.when(i == 0)
        def _():
            small_ref[...] = jnp.zeros_like(small_ref)
            dwg_ref[...] = jnp.zeros_like(dwg_ref)
            dconv_head[...] = jnp.zeros_like(dconv_head)
            dxb2_head[...] = jnp.zeros_like(dxb2_head)
            a_head[...] = jnp.zeros_like(a_head)
            g_head[...] = jnp.zeros_like(g_head)

        def seg(k):
            return p_ref[:, k * d:(k + 1) * d].astype(F32)

        def vrow(k):
            return vec_ref[k:k + 1, :]

        def acc(row, val):
            small_ref[row:row + 1, :] += val

        dout = dx1_ref[...]
        yn, ry = _rms(y_ref[...].astype(F32))
        g_post = vrow(V_G_POST_MIX)
        gt = mod_ref[M_GT_M:M_GT_M + 1, :]
        acc(SB2_DGT, _colsum(dout * yn))
        dy = _rms_bwd(dout * (gt * g_post), yn, ry).astype(BF16)
        dab_ref[2] = dy
        dm = _dot_tb(dy, wo_ref[...])
        sa, sb = seg(5), seg(6)
        dpa = (dm * sa).astype(BF16)
        dpb = (dm * sb).astype(BF16)
        dab_ref[0] = dpa
        dab_ref[1] = dpb
        du_a = dm * pa_ref[...].astype(F32) * (sa * (1.0 - sa))
        du_b = dm * pb_ref[...].astype(F32) * (sb * (1.0 - sb))
        dp_ref[:, 5 * d:6 * d] = du_a.astype(BF16)
        dp_ref[:, 6 * d:7 * d] = du_b.astype(BF16)
        dy_a = _dot_tb(dpa, wa_ref[...])
        dy_b = _dot_tb(dpb, wb_ref[...])

        b_a, c_a, v_a = seg(0), seg(1), seg(2)
        dp_ref[:, 0:d] = (dy_a * conva_ref[...].astype(F32)).astype(BF16)
        dconv = dy_a * b_a
        nxt = dconv_head[...]
        d1 = _shift_up(dconv, 1, nxt)
        d2 = _shift_up(dconv, 2, nxt)
        dconv_head[...] = dconv[:SUBLANES]
        dcv = vrow(V_CONV_A_W + 2) * dconv + vrow(V_CONV_A_W + 1) * d1 + vrow(V_CONV_A_W) * d2
        cv = c_a * v_a
        acc(SB2_DWA + 2, _colsum(cv * dconv))
        acc(SB2_DWA + 1, _colsum(cv * d1))
        acc(SB2_DWA, _colsum(cv * d2))
        acc(SB2_DBA, _colsum(dconv))
        dp_ref[:, d:2 * d] = (dcv * v_a).astype(BF16)
        dp_ref[:, 2 * d:3 * d] = (dcv * c_a).astype(BF16)

        x_b, gel = seg(3), seg(4)
        hh = hh_ref[...]
        dp_ref[:, 4 * d:5 * d] = (dy_b * hh * dgel_ref[...].astype(F32)).astype(BF16)
        dhh = dy_b * gel
        xb2 = xb2_ref[...]
        r, gi, a, mult = r_ref[...], gi_ref[...], a_ref[...], mult_ref[...]
        sp = _softplus(-vrow(V_LAMBDA))
        a_next = _shift_up(a, 1, a_head[...])
        g = _scan_two_level(a_next, dhh, g_head[0:1, :], a_buf, b_buf, c_buf, reverse=True)
        a_head[...] = a[:SUBLANES]
        g_head[...] = g[:SUBLANES]
        gix = gi * xb2
        gm = g * mult
        dlog_a = g * (hh - mult * gix) - (g * gix) * (a * a / mult)
        dgi = gm * xb2
        dxb2 = gm * gi
        acc(SB2_DLAM, _colsum(dlog_a * r))
        dzr = dlog_a * ((-LRU_C) * sp) * (r * (1.0 - r))
        dzi = dgi * (gi * (1.0 - gi))
        acc(SB2_DBR, _colsum(dzr))
        acc(SB2_DBI, _colsum(dzi))
        xb2_b = xb2.astype(BF16)
        back = []
        for hd in range(n_head):
            cols = slice(hd * bw, (hd + 1) * bw)
            dz = jnp.concatenate([dzr[:, cols], dzi[:, cols]], axis=1).astype(BF16)
            back.append(_dot_tb(dz, wg_ref[hd]))
            dwg_ref[hd] += _dot_ta(xb2_b[:, cols], dz)
        dxb2 = dxb2 + jnp.concatenate(back, axis=1)
        nxt = dxb2_head[...]
        e1 = _shift_up(dxb2, 1, nxt)
        e2 = _shift_up(dxb2, 2, nxt)
        e3 = _shift_up(dxb2, 3, nxt)
        dxb2_head[...] = dxb2[:SUBLANES]
        dp_ref[:, 3 * d:4 * d] = (vrow(V_CONV_B_W + 3) * dxb2 + vrow(V_CONV_B_W + 2) * e1
                                  + vrow(V_CONV_B_W + 1) * e2 + vrow(V_CONV_B_W) * e3).astype(BF16)
        acc(SB2_DWB + 3, _colsum(x_b * dxb2))
        acc(SB2_DWB + 2, _colsum(x_b * e1))
        acc(SB2_DWB + 1, _colsum(x_b * e2))
        acc(SB2_DWB, _colsum(x_b * e3))
        acc(SB2_DBB, _colsum(dxb2))

        @pl.when(i == n_t - 1)
        def _():
            dgg_t = small_ref[SB2_DGT:SB2_DGT + 1, :]
            small_ref[SB2_DGT:SB2_DGT + 1, :] = dgg_t * g_post
            small_ref[SB2_DG_POST:SB2_DG_POST + 1, :] = dgg_t * gt
            lam = vrow(V_LAMBDA)
            small_ref[SB2_DLAM:SB2_DLAM + 1, :] = small_ref[SB2_DLAM:SB2_DLAM + 1, :] * (LRU_C * _sigmoid(-lam))

    rev = lambda i: (n_t - 1 - i, 0)
    tile = pl.BlockSpec((ts, d), rev)
    wide = pl.BlockSpec((ts, 7 * d), rev)
    sd = lambda dt: jax.ShapeDtypeStruct((s, d), dt)
    return _pcall(
        body, name=name, grid=(n_t,),
        out_shape=[jax.ShapeDtypeStruct((s, 7 * d), BF16), jax.ShapeDtypeStruct((3, s, d), BF16),
                   jax.ShapeDtypeStruct((2 * SUBLANES, d), F32), jax.ShapeDtypeStruct(wg.shape, F32)],
        in_specs=[tile, wide] + [tile] * 11 + [_full(mod.shape), _full(vec.shape),
                  _full(wg.shape), _full(w_a_out.shape), _full(w_b_out.shape), _full(w_o.shape)],
        out_specs=[wide, pl.BlockSpec((3, ts, d), lambda i: (0, n_t - 1 - i, 0)), _full((2 * SUBLANES, d)), _full(wg.shape)],
        scratch_shapes=[pltpu.VMEM((SUBLANES, d), F32)] * 4 + [pltpu.VMEM((d // LANES, ts, LANES), F32)] * 2
                       + [pltpu.VMEM((ts // SCAN_GROUP, d), F32)],
        args=[dx1, proj, conva, xb2s, hhs, pas, pbs, ys, rs_, gis, as_, mults, dgels, mod, vec, wg, w_a_out, w_b_out, w_o],
        carries=carries)


def _proj_backward(dproj, dx1, x, mod, vec, w_in, name, carries=()):
    s, d = x.shape
    nq = w_in.shape[2]
    ts = min(TOKENS_MATMUL_TILE, s)
    n_t = s // ts

    def body(dp_ref, dx1_ref, x_ref, mod_ref, vec_ref, w_ref, dx_ref, small_ref):
        i = pl.program_id(0)

        @pl.when(i == 0)
        def _():
            small_ref[...] = jnp.zeros_like(small_ref)

        dh = jnp.zeros((ts, d), F32)
        for qb in range(N_CHIP):
            dh = dh + _dot_tb(dp_ref[:, qb * nq:(qb + 1) * nq], w_ref[qb])
        xn, r = _rms(x_ref[...])
        g_pre = vec_ref[V_G_PRE_MIX:V_G_PRE_MIX + 1, :]
        sc1 = 1.0 + mod_ref[M_SC_M:M_SC_M + 1, :]
        dx_ref[...] = dx1_ref[...] + _rms_bwd(dh * (g_pre * sc1), xn, r)
        small_ref[SB1_DSH:SB1_DSH + 1, :] += _colsum(dh)
        small_ref[SB1_DSC:SB1_DSC + 1, :] += _colsum(dh * xn)

        @pl.when(i == n_t - 1)
        def _():
            dgm_t = small_ref[SB1_DSC:SB1_DSC + 1, :]
            small_ref[SB1_DSC:SB1_DSC + 1, :] = dgm_t * g_pre
            small_ref[SB1_DG_PRE:SB1_DG_PRE + 1, :] = dgm_t * sc1

    tile = pl.BlockSpec((ts, d), lambda i: (i, 0))
    return _pcall(
        body, name=name, grid=(n_t,),
        out_shape=[jax.ShapeDtypeStruct((s, d), F32), jax.ShapeDtypeStruct((SUBLANES, d), F32)],
        in_specs=[pl.BlockSpec((ts, N_CHIP * nq), lambda i: (i, 0)), tile, tile, _full(mod.shape), _full(vec.shape),
                  _full(w_in.shape)],
        out_specs=[tile, _full((SUBLANES, d))],
        args=[dproj, dx1, x, mod, vec, w_in], carries=carries)


def _weight_grad(a, b, name, col_blocks=1, tk=512, carries=(), square_relu=False):
    s, k = a.shape
    n = b.shape[1]
    tn = n // col_blocks
    tk = min(tk, k)

    def body(a_ref, b_ref, o_ref):
        av = a_ref[...]
        if square_relu:
            ru = jnp.maximum(av.astype(F32), 0.0)
            av = (ru * ru).astype(BF16)
        o_ref[0] = _dot_ta(av, b_ref[...])

    (out,), carried = _pcall(
        body, name=name, grid=(col_blocks, k // tk),
        out_shape=[jax.ShapeDtypeStruct((col_blocks, k, tn), F32)],
        in_specs=[pl.BlockSpec((s, tk), lambda j, i: (0, i)), pl.BlockSpec((s, tn), lambda j, i: (0, j))],
        out_specs=[pl.BlockSpec((1, tk, tn), lambda j, i: (j, i, 0))],
        args=[a, b], carries=carries)
    return out, carried


def _weight_grad_stacked(a3, b3, name, tk=512, carries=()):
    n_g, s, k = a3.shape
    n = b3.shape[2]
    kq = k // N_CHIP
    tk = min(tk, k)
    chips_per_tile = tk // kq

    def body(a_ref, b_ref, o_ref):
        o_ref[...] = _dot_ta(a_ref[...], b_ref[...]).reshape(chips_per_tile, kq, n)

    (out,), carried = _pcall(
        body, name=name, grid=(n_g, k // tk),
        out_shape=[jax.ShapeDtypeStruct((N_CHIP, n_g, kq, n), F32)],
        in_specs=[pl.BlockSpec((None, s, tk), lambda g, i: (g, 0, i)), pl.BlockSpec((None, s, n), lambda g, i: (g, 0, 0))],
        out_specs=[pl.BlockSpec((chips_per_tile, None, kq, n), lambda g, i: (i, g, 0, 0))],
        args=[a3, b3], carries=carries)
    return out.reshape(N_CHIP, n_g * kq, n), carried


def _adamw(items, name, copy_grad=False, carries=()):
    shape = items[0][0].shape
    cols = shape[-1]
    rows = items[0][0].size // cols
    tr = _row_tile(rows, cols, target_bytes=1024 * 1024 // len(items))
    c1 = 1.0 - ADAM_B1 ** ADAM_STEP
    c2 = 1.0 - ADAM_B2 ** ADAM_STEP
    n_out = 4 if copy_grad else 3
    n = len(items)

    def body(*refs):
        for k in range(n):
            w_ref, g_ref, m_ref, v_ref = refs[4 * k:4 * k + 4]
            outs = refs[4 * n + n_out * k:4 * n + n_out * (k + 1)]
            gv = g_ref[...]
            nm = ADAM_B1 * m_ref[...] + (1.0 - ADAM_B1) * gv
            nv = ADAM_B2 * v_ref[...] + (1.0 - ADAM_B2) * (gv * gv)
            outs[0][...] = (-ADAM_LR) * ((nm / c1) / (jnp.sqrt(nv / c2) + ADAM_EPS) + ADAM_WD * w_ref[...])
            outs[1][...] = nm
            outs[2][...] = nv
            if copy_grad:
                outs[3][...] = gv

    spec = pl.BlockSpec((tr, cols), lambda i: (i, 0))
    outs, carried = _pcall(
        body, name=name, grid=(rows // tr,),
        out_shape=[jax.ShapeDtypeStruct((rows, cols), F32)] * (n_out * n),
        in_specs=[spec] * (4 * n), out_specs=[spec] * (n_out * n),
        args=[t.reshape(rows, cols) for item in items for t in item], carries=carries)
    return [tuple(o.reshape(shape) for o in outs[n_out * k:n_out * (k + 1)]) for k in range(n)], carried


def kernel(x, c, w_mod, b_mod, g_pre_mix, g_post_mix, w_in, conv_a_w, conv_a_b, w_a_out, conv_b_w, conv_b_b, w_gate_r, b_gate_r, w_gate_i, b_gate_i, lru_lambda, w_b_out, w_o, g_pre_mlp, g_post_mlp, w_mlp_up, w_mlp_down, loss_target, m_w_mod, m_b_mod, m_g_pre_mix, m_g_post_mix, m_w_in, m_conv_a_w, m_conv_a_b, m_w_a_out, m_conv_b_w, m_conv_b_b, m_w_gate_r, m_b_gate_r, m_w_gate_i, m_b_gate_i, m_lru_lambda, m_w_b_out, m_w_o, m_g_pre_mlp, m_g_post_mlp, m_w_mlp_up, m_w_mlp_down, v_w_mod, v_b_mod, v_g_pre_mix, v_g_post_mix, v_w_in, v_conv_a_w, v_conv_a_b, v_w_a_out, v_conv_b_w, v_conv_b_b, v_w_gate_r, v_b_gate_r, v_w_gate_i, v_b_gate_i, v_lru_lambda, v_w_b_out, v_w_o, v_g_pre_mlp, v_g_post_mlp, v_w_mlp_up, v_w_mlp_down):
    weights = dict(w_mod=w_mod, b_mod=b_mod, g_pre_mix=g_pre_mix, g_post_mix=g_post_mix, w_in=w_in, conv_a_w=conv_a_w,
                   conv_a_b=conv_a_b, w_a_out=w_a_out, conv_b_w=conv_b_w, conv_b_b=conv_b_b, w_gate_r=w_gate_r,
                   b_gate_r=b_gate_r, w_gate_i=w_gate_i, b_gate_i=b_gate_i, lru_lambda=lru_lambda, w_b_out=w_b_out,
                   w_o=w_o, g_pre_mlp=g_pre_mlp, g_post_mlp=g_post_mlp, w_mlp_up=w_mlp_up, w_mlp_down=w_mlp_down)
    mom1 = dict(w_mod=m_w_mod, b_mod=m_b_mod, g_pre_mix=m_g_pre_mix, g_post_mix=m_g_post_mix, w_in=m_w_in,
                conv_a_w=m_conv_a_w, conv_a_b=m_conv_a_b, w_a_out=m_w_a_out, conv_b_w=m_conv_b_w, conv_b_b=m_conv_b_b,
                w_gate_r=m_w_gate_r, b_gate_r=m_b_gate_r, w_gate_i=m_w_gate_i, b_gate_i=m_b_gate_i,
                lru_lambda=m_lru_lambda, w_b_out=m_w_b_out, w_o=m_w_o, g_pre_mlp=m_g_pre_mlp, g_post_mlp=m_g_post_mlp,
                w_mlp_up=m_w_mlp_up, w_mlp_down=m_w_mlp_down)
    mom2 = dict(w_mod=v_w_mod, b_mod=v_b_mod, g_pre_mix=v_g_pre_mix, g_post_mix=v_g_post_mix, w_in=v_w_in,
                conv_a_w=v_conv_a_w, conv_a_b=v_conv_a_b, w_a_out=v_w_a_out, conv_b_w=v_conv_b_w, conv_b_b=v_conv_b_b,
                w_gate_r=v_w_gate_r, b_gate_r=v_b_gate_r, w_gate_i=v_w_gate_i, b_gate_i=v_b_gate_i,
                lru_lambda=v_lru_lambda, w_b_out=v_w_b_out, w_o=v_w_o, g_pre_mlp=v_g_pre_mlp, g_post_mlp=v_g_post_mlp,
                w_mlp_up=v_w_mlp_up, w_mlp_down=v_w_mlp_down)
    names = list(weights)

    n_layer = w_in.shape[0]
    s, d = x.shape[1], x.shape[2]
    n_head, bw = w_gate_r.shape[1], w_gate_r.shape[2]
    dq = d // N_CHIP
    mq = w_mod.shape[2]
    n_mod = (N_CHIP * mq) // d
    ka, kb = conv_a_w.shape[1], conv_b_w.shape[1]

    mx, my, mc = _place()
    q_me = 2 * mx + my
    q_arr = jnp.reshape(q_me, (1,)).astype(jnp.int32)

    me_dev = 4 * mx + 2 * my + mc
    me_arr = jnp.reshape(me_dev, (1,)).astype(jnp.int32)

    big_names = ["w_in", "w_a_out", "w_b_out", "w_o", "w_mlp_up", "w_mlp_down"]
    groups = [["w_in"], ["w_a_out", "w_b_out", "w_o"], ["w_mlp_up", "w_mlp_down"]]
    placed = {("w_in", 0): _cast_place_all([(w_in, 0)], q_arr, "cast_place_first")[0][0]}
    wfull = [dict() for _ in range(n_layer)]
    riders = {}
    for l in range(n_layer):
        riders.setdefault(3 * l - 1, []).append(([("w_in", l)], 0.9 if l else 1.0))
        riders.setdefault(3 * l - 2 if l else 0, []).append(([(nm, l) for nm in groups[1]], 0.9 if l else 0.5))
        riders.setdefault(3 * l, []).append(([("w_mlp_up", l)], 0.7 if l else 0.9))
        riders.setdefault(3 * l + 1, []).insert(0, ([("w_mlp_down", l)], 0.5))

    def gather_carry(call):
        return [_gather_carry([placed[k] for k in keys], frac) for keys, frac in riders.get(call, [])]

    def gathered(call, carried):
        for (keys, _), ws in zip(riders.get(call, []), carried):
            for (nm, l), w in zip(keys, ws):
                wfull[l][nm] = w.reshape(d, d) if nm in groups[1] else w

    n_conv_rows = n_layer * (ka + kb)
    conv_blk = -(-n_conv_rows // SUBLANES) * SUBLANES
    blk_rows = SUBLANES + conv_blk
    conv_rows = jnp.concatenate([jnp.concatenate([conv_a_w[l], conv_b_w[l]], axis=0) for l in range(n_layer)], axis=0)
    conv_rows = jnp.pad(conv_rows, ((0, conv_blk - n_conv_rows), (0, d - dq)))
    c_conv = jnp.concatenate([jnp.pad(c, ((0, SUBLANES - 1), (0, 0))), conv_rows], axis=0)
    rest = [(nm, l) for l in range(n_layer) for nm in big_names if (nm, l) != ("w_in", 0)]
    rest_placed, carried = _cast_place_all([(weights[nm], l) for nm, l in rest], q_arr, "cast_place_rest",
                                           carries=gather_carry(-1) + [_allgather_carry([c_conv])])
    placed.update(zip(rest, rest_placed))
    gathered(-1, carried[:1])
    gathered1 = lax.dynamic_update_slice(carried[1][0], c_conv, (me_dev * blk_rows, 0)).reshape(N_DEV, blk_rows, d)
    c_all = gathered1[:, 0, :]
    conv_full = jnp.concatenate([gathered1[2 * qb, SUBLANES:SUBLANES + n_conv_rows, :dq] for qb in range(N_CHIP)], axis=1)

    b_mod_shard = lax.dynamic_slice_in_dim(b_mod, q_me * mq, mq, axis=1)
    mod_part = _mod_forward(c_all, w_mod, b_mod_shard, "mod_forward")
    gathered2 = _all_gather_small(mod_part, "gather_mod").reshape(N_DEV, n_layer, N_DEV, mq)
    mod_rows = jnp.concatenate(
        [lax.dynamic_index_in_dim(gathered2[2 * qb], me_dev, axis=1, keepdims=False) for qb in range(N_CHIP)], axis=1)
    mods = [jnp.pad(mod_rows[l].reshape(n_mod, d), ((0, SUBLANES - n_mod), (0, 0))) for l in range(n_layer)]

    vecs = []
    for l in range(n_layer):
        base = l * (ka + kb)
        rows = [g_pre_mix[l], g_post_mix[l], conv_a_b[l], conv_b_b[l], b_gate_r[l], b_gate_i[l], lru_lambda[l],
                g_pre_mlp[l], g_post_mlp[l]]
        vecs.append(jnp.concatenate([jnp.stack(rows, axis=0), conv_full[base:base + ka + kb]], axis=0))

    wgs =[jnp.concatenate([w_gate_r[l], w_gate_i[l]], axis=-1).astype(BF16) for l in range(n_layer)]

    xs = x[0]
    saved = []
    for l in range(n_layer):
        wl = wfull[l]
        (h, proj, dgel), carried = _norm_proj(xs, mods[l], vecs[l], wl["w_in"], f"norm_proj_{l}", gather_carry(3 * l))
        gathered(3 * l, carried)
        (x1, conva, xb2, hh, abm, pa, pb, yy, gr, ggi, ga, gmult), carried = _mixer_forward(
            xs, proj, mods[l], vecs[l], wgs[l], wl["w_a_out"], wl["w_b_out"], wl["w_o"], f"mixer_forward_{l}",
            gather_carry(3 * l + 1))
        gathered(3 * l + 1, carried)
        (x2, h2, up, y2, *loss_tile), carried = _mlp_forward(
            x1, mods[l], vecs[l], wl["w_mlp_up"], wl["w_mlp_down"], f"mlp_forward_{l}", gather_carry(3 * l + 2),
            target=loss_target[0] if l == n_layer - 1 else None)
        gathered(3 * l + 2, carried)
        saved.append(dict(x=xs, h=h, proj=proj, x1=x1, conva=conva, xb2=xb2, hh=hh, abm=abm, pa=pa, pb=pb,
                          y=yy, r=gr, gi=ggi, a=ga, mult=gmult, dgel=dgel, h2=h2, up=up, y2=y2))
        xs = x2
    dxs = xs
    loss_block = jnp.pad(loss_tile[0], ((0, 0), (0, d - LANES)))

    chips_q = [q_me ^ 2, q_me ^ 1, q_me ^ 3]
    pf = jnp.stack([mc, q_me] + chips_q).astype(jnp.int32)
    rs = dict(grad={}, landed={}, to_send={}, from_chips={}, out={})
    to_exchange, to_scatter, to_join, to_gather = [], [], [], []
    small_own, small_all = {}, {}

    def ride(call, what, name=None):
        ex = list(to_exchange) if "x" in what else []
        sc = list(to_scatter) if "s" in what else []
        ga = list(to_gather) if "g" in what else []
        jn = []
        for key in (to_join if "j" in what else []):
            if key[0] not in [k[0] for k in jn]:
                jn.append(key)
        carries = []
        if ex:
            carries.append(_exchange_carry([rs["grad"][k] for k in ex]))
        if sc:
            carries.append(_scatter_carry([rs["to_send"][k] for k in sc]))
        if jn:
            carries.append(_join_carry([rs["out"][k[0]] for k in jn], [k[1] for k in jn]))
        if ga:
            carries.append(_allgather_carry([small_own[k] for k in ga]))
        if call is None:
            carried = _run_carries(carries, name) if carries else []
            res = None
        else:
            res, carried = call(carries)
        carried = list(carried)
        if ex:
            for k, ld in zip(ex, carried.pop(0)):
                to_exchange.remove(k)
                rs["landed"][k] = ld
                rs["to_send"][k] = _add_sibling_half(rs["grad"][k], ld, pf, f"rs_add_sibling_{k[0]}_{k[1]}")
                to_scatter.append(k)
        if sc:
            for k, fc in zip(sc, carried.pop(0)):
                to_scatter.remove(k)
                rs["out"][k[0]] = _add_chips(rs["grad"][k], rs["landed"][k], fc, pf, rs["out"].get(k[0]), k[1], n_layer,
                                             f"rs_add_chips_{k[0]}_{k[1]}")
                to_join.append(k)
        if jn:
            for k, o in zip(jn, carried.pop(0)):
                to_join.remove(k)
                rs["out"][k[0]] = o
        if ga:
            for k, o in zip(ga, carried.pop(0)):
                to_gather.remove(k)
                small_all[k] = o
        return res

    def gather_small(key, parts):
        small_own[key] = parts[0] if len(parts) == 1 else jnp.concatenate(parts, axis=0)
        to_gather.append(key)

    def ready(nm, l, g):
        rs["grad"][(nm, l)] = g
        to_exchange.append((nm, l))

    rowblk = lambda t: t.reshape(N_CHIP, t.shape[1] // N_CHIP, t.shape[2])
    small1_prev = None
    for l in reversed(range(n_layer)):
        wl, sv = wfull[l], saved[l]
        dx1, dy2, dup, small3 = ride(lambda cr: _mlp_backward(
            dxs, sv["x1"], sv["y2"], sv["up"], mods[l], vecs[l], wl["w_mlp_up"], wl["w_mlp_down"], f"mlp_backward_{l}", cr), "xsjg")
        ready("w_mlp_up", l, _weight_grad(sv["h2"], dup, f"grad_w_mlp_up_{l}", col_blocks=N_CHIP)[0])
        g_down = ride(lambda cr: _weight_grad(sv["up"], dy2, f"grad_w_mlp_down_{l}", carries=cr, square_relu=True), "x")
        ready("w_mlp_down", l, rowblk(g_down))
        dproj, dab, small2, dwg = ride(lambda cr: _mixer_backward(
            dx1, sv["proj"], sv["conva"], sv["xb2"], sv["hh"], sv["pa"], sv["pb"], sv["y"],
            sv["r"], sv["gi"], sv["a"], sv["mult"], sv["dgel"], mods[l], vecs[l], wgs[l],
            wl["w_a_out"], wl["w_b_out"], wl["w_o"], f"mixer_backward_{l}", cr), "xsjg")
        gather_small(("late", l, "s"), ([small1_prev] if small1_prev is not None else []) + [small2, small3])
        gather_small(("late", l, "w"), [dwg.reshape(2 * bw, d).astype(BF16)])
        g_in = ride(lambda cr: _weight_grad(sv["h"], dproj, f"grad_w_in_{l}", col_blocks=N_CHIP, carries=cr), "xsj")
        ready("w_in", l, g_in)
        g_abo = ride(lambda cr: _weight_grad_stacked(sv["abm"], dab, f"grad_w_abo_{l}", carries=cr), "xg")
        ready("w_abo", l, g_abo)
        dxs, small1_prev = ride(lambda cr: _proj_backward(dproj, dx1, sv["x"], mods[l], vecs[l], wl["w_in"],
                                                          f"proj_backward_{l}", cr), "xsjg")
    grad_x = dxs[None]
    gather_small(("last", 0, "s"), [small1_prev, loss_block])

    tail = 0
    while to_exchange or to_scatter or to_join or to_gather:
        ride(None, "xsjg", f"rs_tail_{tail}")
        tail += 1
    grads, deltas, new_m, new_v = {}, {}, {}, {}

    def adam(nms, copy_grad=False):
        items = [(weights[nm], grads[nm], mom1[nm], mom2[nm]) for nm in nms]
        res, _ = _adamw(items, "adamw_" + "_".join(nms), copy_grad)
        for nm, r in zip(nms, res):
            deltas[nm], new_m[nm], new_v[nm] = r[:3]
            if copy_grad:
                grads[nm] = r[3]

    for nms in (["w_mlp_up", "w_mlp_down"], ["w_in"]):
        for nm in nms:
            grads[nm] = rs["out"][nm].reshape(weights[nm].shape)
        adam(nms, True)

    sums ={k: _sum_devices(small_all[k], small_own[k], me_arr, f"sum_small_{k[0]}_{k[1]}_{k[2]}") for k in small_own}

    loss = sums[("last", 0, "s")][SUBLANES, 0]
    small_full = {}

    def rows_of(l, part):
        if part == 0:
            return (("late", l - 1, "s"), 0) if l >= 1 else (("last", 0, "s"), 0)
        if part == 3:
            return ("late", l, "w"), 0
        base = SUBLANES if l < n_layer - 1 else 0
        return ("late", l, "s"), base + (0, 0, 2 * SUBLANES)[part]

    def summed(l, part, row, n_rows=1):
        key, base = rows_of(l, part)
        return sums[key][base + row:base + row + n_rows]

    def per_device(l, part, row):
        key, base = rows_of(l, part)
        own = small_own[key]
        if key not in small_full:
            small_full[key] = lax.dynamic_update_slice(small_all[key], own, (me_dev * own.shape[0], 0)).reshape(
                (N_DEV,) + own.shape)
        return small_full[key][:, base + row:base + row + 1]

    mod_rows = [(0, SB1_DSH), (0, SB1_DSC), (1, SB2_DGT), (2, SB3_DSH), (2, SB3_DSC), (2, SB3_DGT)]
    dmod_all = jnp.stack([jnp.concatenate([per_device(l, p, r)[:, 0, :] for p, r in mod_rows], axis=1)
                          for l in range(n_layer)], axis=0)
    o1, o2, o3, o4 = 0, SUBLANES, 3 * SUBLANES, 4 * SUBLANES
    small_sum = jnp.stack([jnp.concatenate([summed(l, 0, 0, SUBLANES), summed(l, 1, 0, 2 * SUBLANES),
                                            summed(l, 2, 0, SUBLANES), summed(l, 3, 0, 2 * bw)], axis=0)
                           for l in range(n_layer)], axis=0)
    mod_rows_of = [o1 + SB1_DSH, o1 + SB1_DSC, o2 + SB2_DGT, o3 + SB3_DSH, o3 + SB3_DSC, o3 + SB3_DGT]
    grads["w_mod"] = _mod_backward(c_all.T, lax.dynamic_slice_in_dim(dmod_all, q_me * mq, mq, axis=2), "mod_backward")
    grads["b_mod"] = jnp.concatenate([small_sum[:, k, :] for k in mod_rows_of], axis=1)
    grads["g_pre_mix"] = small_sum[:, o1 + SB1_DG_PRE]
    grads["g_post_mix"] = small_sum[:, o2 + SB2_DG_POST]
    grads["conv_a_w"] = lax.dynamic_slice_in_dim(small_sum[:, o2 + SB2_DWA:o2 + SB2_DWA + ka], q_me * dq, dq, axis=2)
    grads["conv_a_b"] = small_sum[:, o2 + SB2_DBA]
    grads["conv_b_w"] = lax.dynamic_slice_in_dim(small_sum[:, o2 + SB2_DWB:o2 + SB2_DWB + kb], q_me * dq, dq, axis=2)
    grads["conv_b_b"] = small_sum[:, o2 + SB2_DBB]
    grads["lru_lambda"] = small_sum[:, o2 + SB2_DLAM]
    grads["b_gate_r"] = small_sum[:, o2 + SB2_DBR]
    grads["b_gate_i"] = small_sum[:, o2 + SB2_DBI]
    grads["g_pre_mlp"] = small_sum[:, o3 + SB3_DG_PRE]
    grads["g_post_mlp"] = small_sum[:, o3 + SB3_DG_POST]
    dwg_sum = small_sum[:, o4:].reshape(n_layer, n_head, bw, 2 * bw)
    grads["w_gate_r"] = dwg_sum[..., :bw]
    grads["w_gate_i"] = dwg_sum[..., bw:]

    for k, nm in enumerate(groups[1]):
        grads[nm] = rs["out"]["w_abo"][:, k * dq:(k + 1) * dq]

    by_shape = {}
    for nm in names:
        if nm not in deltas:
            by_shape.setdefault(weights[nm].shape, []).append(nm)
    for nms in by_shape.values():
        adam(nms)
    return (loss, grad_x, *[grads[nm] for nm in names], *[deltas[nm] for nm in names],
            *[new_m[nm] for nm in names], *[new_v[nm] for nm in names])
```

```python
import jax
import jax.numpy as jnp
from jax import lax
from jax.experimental import pallas as pl
from jax.experimental.pallas import tpu as pltpu

F32 = jnp.float32
BF16 = jnp.bfloat16
MESH = pl.DeviceIdType.MESH

EPS = 1e-6
LRU_C = 8.0
N_CHIP = 4
N_DEV = 8
ADAM_LR = 0.001
ADAM_B1 = 0.9
ADAM_B2 = 0.999
ADAM_EPS = 1e-08
ADAM_WD = 0.01
ADAM_STEP = 10

VMEM_LIMIT_BYTES = 56 * 1024 * 1024
SUBLANES = 8
LANES = 128
TOKENS_MATMUL_TILE = 512
TOKENS_MIXER_TILE = 256
GELU_K0 = 0.7978845608028654
GELU_K1 = 0.044715

V_G_PRE_MIX, V_G_POST_MIX, V_CONV_A_B, V_CONV_B_B, V_B_GATE_R, V_B_GATE_I, V_LAMBDA, V_G_PRE_MLP, V_G_POST_MLP = range(9)
V_CONV_A_W = 9
V_CONV_B_W = 12
M_SH_M, M_SC_M, M_GT_M, M_SH_F, M_SC_F, M_GT_F = range(6)


def _cparams(n_grid=0):
    sem = ("arbitrary",) * n_grid if n_grid else None
    return pltpu.CompilerParams(dimension_semantics=sem, vmem_limit_bytes=VMEM_LIMIT_BYTES)


def _full(shape):
    return pl.BlockSpec(shape, lambda *_: (0,) * len(shape))


def _dot(a, b):
    return jnp.dot(a, b, preferred_element_type=F32)


def _dot_tb(a, b):
    return lax.dot_general(a, b, (((1,), (1,)), ((), ())), preferred_element_type=F32)


def _dot_ta(a, b):
    return lax.dot_general(a, b, (((0,), (0,)), ((), ())), preferred_element_type=F32)


def _sigmoid(x):
    return 1.0 / (1.0 + jnp.exp(-x))


def _softplus(x):
    return jnp.maximum(x, 0.0) + jnp.log1p(jnp.exp(-jnp.abs(x)))


def _neg_expm1(x):
    series = -x * (1.0 + 0.5 * x * (1.0 + (x / 3.0) * (1.0 + 0.25 * x)))
    return jnp.where(x > -1e-2, series, 1.0 - jnp.exp(x))


def _gelu_and_grad(x):
    x2 = x * x
    s = _sigmoid(x * (2.0 * GELU_K0 + (2.0 * GELU_K0 * GELU_K1) * x2))
    gel = x * s
    return gel, s + gel * (1.0 - s) * (2.0 * GELU_K0 + (6.0 * GELU_K0 * GELU_K1) * x2)


def _rms(x):
    r = lax.rsqrt(jnp.mean(x * x, axis=-1, keepdims=True) + EPS)
    return x * r, r


def _rms_bwd(dxn, xn, r):
    return r * (dxn - xn * jnp.mean(dxn * xn, axis=-1, keepdims=True))


def _colsum(x):
    return jnp.sum(x, axis=0, keepdims=True)


def _rows(t, w):
    return lax.broadcasted_iota(jnp.int32, (t, w), 0)


def _shift_down(x, k, prev8):
    t, w = x.shape
    rolled = pltpu.roll(x, k, 0)
    head = jnp.where(_rows(SUBLANES, w) < k, pltpu.roll(prev8, k, 0), rolled[:SUBLANES])
    return jnp.concatenate([head, rolled[SUBLANES:]], axis=0)


def _shift_up(x, k, next8):
    t, w = x.shape
    rolled = pltpu.roll(x, t - k, 0)
    tail = jnp.where(_rows(SUBLANES, w) >= SUBLANES - k, pltpu.roll(next8, SUBLANES - k, 0), rolled[t - SUBLANES:])
    return jnp.concatenate([rolled[:t - SUBLANES], tail], axis=0)


SCAN_GROUP = 16


def _scan_steps(a, b, group, reverse):
    t, w = a.shape
    pos = _rows(t, w) & (group - 1)
    s = 1
    while s < group:
        keep = (pos < group - s) if reverse else (pos >= s)
        shift = (t - s) if reverse else s
        b = b + a * jnp.where(keep, pltpu.roll(b, shift, 0), 0.0)
        a = a * jnp.where(keep, pltpu.roll(a, shift, 0), 1.0)
        s *= 2
    return b, a


def _scan_two_level(a, b, carry_row, a_buf, b_buf, c_buf, reverse):
    t, w = a.shape
    grp = SCAN_GROUP
    n_grp = t // grp
    h_loc, a_cum = _scan_steps(a, b, grp, reverse)
    end = 0 if reverse else grp - 1
    a_end, h_end = [], []
    for j in range(w // LANES):
        a_buf[j] = a_cum[:, j * LANES:(j + 1) * LANES]
        b_buf[j] = h_loc[:, j * LANES:(j + 1) * LANES]
        a_end.append(a_buf[j, pl.ds(end, n_grp, stride=grp), :])
        h_end.append(b_buf[j, pl.ds(end, n_grp, stride=grp), :])
    a_end = jnp.concatenate(a_end, axis=1)
    h_end = jnp.concatenate(h_end, axis=1)
    h_grp, a_grp = _scan_steps(a_end, h_end, n_grp, reverse)
    h_grp = h_grp + a_grp * carry_row
    rows = _rows(n_grp, w)
    if reverse:
        entering = jnp.where(rows == n_grp - 1, carry_row, pltpu.roll(h_grp, n_grp - 1, 0))
    else:
        entering = jnp.where(rows == 0, carry_row, pltpu.roll(h_grp, 1, 0))
    c_buf[...] = entering
    out = [h_loc[g * grp:(g + 1) * grp] + a_cum[g * grp:(g + 1) * grp] * c_buf[g:g + 1, :] for g in range(n_grp)]
    return jnp.concatenate(out, axis=0)


def _row_tile(rows, cols, itemsize=4, target_bytes=2 * 1024 * 1024):
    if rows * cols * itemsize <= target_bytes or rows % SUBLANES:
        return rows
    t = max(SUBLANES, (target_bytes // (cols * itemsize)) // SUBLANES * SUBLANES)
    while rows % t:
        t -= SUBLANES
    return t


def _place():
    return lax.axis_index("x"), lax.axis_index("y"), lax.axis_index("c")


def _other_chips(x, y):
    chips = [(1 - x, y), (x, 1 - y), (1 - x, 1 - y)]
    return chips, [2 * cx + cy for cx, cy in chips]


def _all_gather_small(block, name):
    m_per, n = block.shape

    def body(x_ref, out_ref, send_sems, recv_sems, local_sem):
        x, y, c = _place()
        me, sibling = (x, y, c), (x, y, 1 - c)
        chips, _ = _other_chips(x, y)

        def rows(px, py, pc):
            return out_ref.at[pl.ds((4 * px + 2 * py + pc) * m_per, m_per), :]

        def copy(k, blk, to, src=None):
            return pltpu.make_async_remote_copy(
                src_ref=rows(*blk) if src is None else src, dst_ref=rows(*blk),
                send_sem=send_sems.at[k], recv_sem=recv_sems.at[k], device_id=to, device_id_type=MESH)

        mine = pltpu.make_async_copy(x_ref, rows(*me), local_sem)
        mine.start()
        first = [copy(0, me, sibling, src=x_ref)]
        first += [copy(1 + j, me, (*chip, c), src=x_ref) for j, chip in enumerate(chips)]
        for cp in first:
            cp.start()
        passed = [copy(4 + j, (*chip, c), sibling) for j, chip in enumerate(chips)]
        for j, chip in enumerate(chips):
            copy(1 + j, (*chip, c), me).wait_recv()
            passed[j].start()
        copy(0, sibling, me).wait_recv()
        for j, chip in enumerate(chips):
            copy(4 + j, (*chip, 1 - c), me).wait_recv()
        for cp in first + passed:
            cp.wait_send()
        mine.wait()

    return pl.pallas_call(
        body, name=name,
        out_shape=jax.ShapeDtypeStruct((N_DEV * m_per, n), block.dtype),
        in_specs=[pl.BlockSpec(memory_space=pltpu.VMEM)],
        out_specs=pl.BlockSpec(memory_space=pltpu.VMEM),
        scratch_shapes=[pltpu.SemaphoreType.DMA((7,)), pltpu.SemaphoreType.DMA((7,)), pltpu.SemaphoreType.DMA],
        compiler_params=pltpu.CompilerParams(vmem_limit_bytes=VMEM_LIMIT_BYTES),
    )(block)


class _Carry:
    def __init__(self, ins, out_shapes, aliases, sem_shapes, start, finish, mid=None, mid_frac=0.85):
        self.ins, self.out_shapes, self.aliases, self.sem_shapes = list(ins), list(out_shapes), dict(aliases), list(sem_shapes)
        self.start, self.mid, self.finish, self.mid_frac = start, mid, finish, mid_frac


def _pcall(body, *, name, grid, in_specs, out_specs, out_shape, args, scratch_shapes=(), carries=(), prefetch=()):
    in_specs, out_specs, out_shape = list(in_specs), list(out_specs), list(out_shape)
    scratch_shapes, args = list(scratch_shapes), list(args)
    n_in, n_out, n_scr, n_pre = len(in_specs), len(out_shape), len(scratch_shapes), len(prefetch)
    steps = 1
    for g in grid:
        steps *= g
    any_spec = pl.BlockSpec(memory_space=pl.ANY)
    aliases = {}
    spans = []
    for cr in carries:
        spans.append((len(args), len(out_shape), len(scratch_shapes)))
        for a, b in cr.aliases.items():
            aliases[n_pre + len(args) + a] = len(out_shape) + b
        args += cr.ins
        in_specs += [any_spec] * len(cr.ins)
        out_shape += cr.out_shapes
        out_specs += [any_spec] * len(cr.out_shapes)
        scratch_shapes += cr.sem_shapes
    n_all_in = len(args)
    n_all_out = len(out_shape)

    def wrapped(*refs):
        pre, refs = refs[:n_pre], refs[n_pre:]
        ins, outs, scr = refs[:n_all_in], refs[n_all_in:n_all_in + n_all_out], refs[n_all_in + n_all_out:]
        parts = [(cr, ins[a:a + len(cr.ins)], outs[b:b + len(cr.out_shapes)], scr[s:s + len(cr.sem_shapes)])
                 for cr, (a, b, s) in zip(carries, spans)]
        lin = 0
        for ax, g in enumerate(grid):
            lin = lin * g + pl.program_id(ax)

        def at(step, fn):
            if steps == 1:
                fn()
            else:
                pl.when(lin == step)(fn)

        def start_all():
            for cr, ci, co, cs in parts:
                cr.start(ci, co, cs)

        def finish_all():
            for cr, ci, co, cs in parts:
                cr.finish(ci, co, cs)

        if parts:
            at(0, start_all)
        body(*pre, *ins[:n_in], *outs[:n_out], *scr[:n_scr])
        for cr, ci, co, cs in parts:
            if cr.mid is not None:
                at(min(steps - 1, int(steps * cr.mid_frac)), lambda cr=cr, ci=ci, co=co, cs=cs: cr.mid(ci, co, cs))
        if parts:
            at(steps - 1, finish_all)

    if n_pre:
        res = pl.pallas_call(
            wrapped, name=name, out_shape=out_shape,
            grid_spec=pltpu.PrefetchScalarGridSpec(num_scalar_prefetch=n_pre, grid=tuple(grid), in_specs=in_specs,
                                                   out_specs=out_specs, scratch_shapes=scratch_shapes),
            input_output_aliases=aliases, compiler_params=_cparams(len(grid)),
        )(*prefetch, *args)
    else:
        res = pl.pallas_call(
            wrapped, name=name, grid=tuple(grid), out_shape=out_shape, in_specs=in_specs, out_specs=out_specs,
            scratch_shapes=scratch_shapes, input_output_aliases=aliases, compiler_params=_cparams(len(grid)),
        )(*args)
    res = list(res)
    return res[:n_out], [res[b:b + len(cr.out_shapes)] for cr, (_, b, _) in zip(carries, spans)]


def _run_carries(carries, name):
    return _pcall(lambda: None, name=name, grid=(), in_specs=[], out_specs=[], out_shape=[], args=[], carries=carries)[1]


CAST_STEPS = 8


def _cast_place_all(shards, q_arr, name, carries=()):
    n = len(shards)

    def body(q_ref, *refs):
        for k in range(n):
            refs[n + k][...] = refs[k][...].astype(BF16)

    def spec_in(k):
        w, layer = shards[k]
        return pl.BlockSpec((1, w.shape[1] // CAST_STEPS, w.shape[2]), lambda i, q_ref: (layer, i, 0))

    def spec_out(k):
        w, _ = shards[k]
        return pl.BlockSpec((1, w.shape[1] // CAST_STEPS, w.shape[2]), lambda i, q_ref: (q_ref[0], i, 0))

    return _pcall(
        body, name=name, grid=(CAST_STEPS,),
        out_shape=[jax.ShapeDtypeStruct((N_CHIP,) + w.shape[1:], BF16) for w, _ in shards],
        in_specs=[spec_in(k) for k in range(n)], out_specs=[spec_out(k) for k in range(n)],
        args=[w for w, _ in shards], carries=carries, prefetch=[q_arr])


def _gather_carry(bufs, mid_frac=0.85):
    n = len(bufs)

    def copies(o_refs, sems):
        send_sems, recv_sems = sems
        x, y, c = _place()
        q = 2 * x + y
        sibling = (x, y, 1 - c)
        chips, qs = _other_chips(x, y)

        def half(w, shard, pc):
            rh = bufs[w].shape[1] // 2
            return o_refs[w].at[shard, pl.ds(pc * rh, rh), :]

        def over_ici(w, j, shard):
            blk = half(w, shard, c)
            return pltpu.make_async_remote_copy(
                src_ref=blk, dst_ref=blk, send_sem=send_sems.at[w, j], recv_sem=recv_sems.at[w, j],
                device_id=(*chips[j], c), device_id_type=MESH)

        def to_sibling(w, j, pc):
            blk = half(w, qs[j], pc)
            return pltpu.make_async_remote_copy(
                src_ref=blk, dst_ref=blk, send_sem=send_sems.at[w, 3 + j], recv_sem=recv_sems.at[w, 3 + j],
                device_id=sibling, device_id_type=MESH)

        return q, c, qs, over_ici, to_sibling

    pairs = [(w, j) for w in range(n) for j in range(3)]

    def start(i_refs, o_refs, sems):
        q, _, _, over_ici, _ = copies(o_refs, sems)
        for w, j in pairs:
            over_ici(w, j, q).start()

    def mid(i_refs, o_refs, sems):
        _, c, qs, over_ici, to_sibling = copies(o_refs, sems)
        for w, j in pairs:
            over_ici(w, j, qs[j]).wait_recv()
            to_sibling(w, j, c).start()

    def finish(i_refs, o_refs, sems):
        q, c, _, over_ici, to_sibling = copies(o_refs, sems)
        for w, j in pairs:
            to_sibling(w, j, 1 - c).wait_recv()
        for w, j in pairs:
            over_ici(w, j, q).wait_send()
            to_sibling(w, j, c).wait_send()

    return _Carry(bufs, [jax.ShapeDtypeStruct(b.shape, b.dtype) for b in bufs], {w: w for w in range(n)},
                  [pltpu.SemaphoreType.DMA((n, 6)), pltpu.SemaphoreType.DMA((n, 6))], start, finish, mid, mid_frac)


def _exchange_carry(grads):
    n = len(grads)

    def copies(g_refs, l_refs, sems):
        send_sems, recv_sems = sems
        x, y, c = _place()
        out = []
        for w in range(n):
            rh = grads[w].shape[1] // 2
            out.append(pltpu.make_async_remote_copy(
                src_ref=g_refs[w].at[:, pl.ds((1 - c) * rh, rh), :], dst_ref=l_refs[w],
                send_sem=send_sems.at[w], recv_sem=recv_sems.at[w], device_id=(x, y, 1 - c), device_id_type=MESH))
        return out

    def start(g_refs, l_refs, sems):
        for cp in copies(g_refs, l_refs, sems):
            cp.start()

    def finish(g_refs, l_refs, sems):
        for cp in copies(g_refs, l_refs, sems):
            cp.wait()

    return _Carry(grads, [jax.ShapeDtypeStruct((N_CHIP, g.shape[1] // 2, g.shape[2]), g.dtype) for g in grads], {},
                  [pltpu.SemaphoreType.DMA((n,)), pltpu.SemaphoreType.DMA((n,))], start, finish)


def _scatter_carry(sums):
    n = len(sums)

    def copies(s_refs, l_refs, sems):
        send_sems, recv_sems = sems
        x, y, c = _place()
        chips, _ = _other_chips(x, y)
        return [pltpu.make_async_remote_copy(
            src_ref=s_refs[w].at[j], dst_ref=l_refs[w].at[j], send_sem=send_sems.at[w, j], recv_sem=recv_sems.at[w, j],
            device_id=(*chips[j], c), device_id_type=MESH) for w in range(n) for j in range(3)]

    def start(s_refs, l_refs, sems):
        for cp in copies(s_refs, l_refs, sems):
            cp.start()

    def finish(s_refs, l_refs, sems):
        for cp in copies(s_refs, l_refs, sems):
            cp.wait()

    return _Carry(sums, [jax.ShapeDtypeStruct(s.shape, s.dtype) for s in sums], {},
                  [pltpu.SemaphoreType.DMA((n, 3)), pltpu.SemaphoreType.DMA((n, 3))], start, finish)


def _join_carry(outs, layers):
    n = len(outs)

    def copy(o_refs, sems, w, mine):
        send_sems, recv_sems = sems
        x, y, c = _place()
        r = outs[w].shape[1]
        rows = o_refs[w].at[layers[w], pl.ds((c if mine else 1 - c) * (r // 2), r // 2), :]
        return pltpu.make_async_remote_copy(
            src_ref=rows, dst_ref=rows, send_sem=send_sems.at[w], recv_sem=recv_sems.at[w],
            device_id=(x, y, 1 - c), device_id_type=MESH)

    def start(i_refs, o_refs, sems):
        for w in range(n):
            copy(o_refs, sems, w, True).start()

    def finish(i_refs, o_refs, sems):
        for w in range(n):
            copy(o_refs, sems, w, True).wait_send()
        for w in range(n):
            copy(o_refs, sems, w, False).wait_recv()

    return _Carry(outs, [jax.ShapeDtypeStruct(o.shape, o.dtype) for o in outs], {w: w for w in range(n)},
                  [pltpu.SemaphoreType.DMA((n,)), pltpu.SemaphoreType.DMA((n,))], start, finish)


PF_C, PF_Q, PF_QS = 0, 1, 2


def _add_sibling_half(g, landed, pf, name):
    _, r, cols = g.shape
    rh = r // 2
    tr = _row_tile(rh, cols)
    nr = rh // tr

    def body(pf_ref, g_ref, l_ref, o_ref):
        o_ref[...] = (g_ref[...] + l_ref[...]).astype(BF16)

    return pl.pallas_call(
        body, name=name,
        out_shape=jax.ShapeDtypeStruct((3, rh, cols), BF16),
        grid_spec=pltpu.PrefetchScalarGridSpec(
            num_scalar_prefetch=1, grid=(3, nr),
            in_specs=[pl.BlockSpec((1, tr, cols), lambda j, i, pf_ref: (pf_ref[PF_QS + j], pf_ref[PF_C] * nr + i, 0)),
                      pl.BlockSpec((1, tr, cols), lambda j, i, pf_ref: (pf_ref[PF_QS + j], i, 0))],
            out_specs=pl.BlockSpec((1, tr, cols), lambda j, i, pf_ref: (j, i, 0))),
        compiler_params=_cparams(2),
    )(pf, g, landed)


def _add_chips(g, landed, from_chips, pf, prev, layer, n_layer, name):
    _, r, cols = g.shape
    rh = r // 2
    tr = _row_tile(rh, cols)
    nr = rh // tr

    def body(pf_ref, g_ref, l_ref, f_ref, *rest):
        o_ref = rest[-1]
        acc = g_ref[0] + l_ref[0]
        for j in range(3):
            acc = acc + f_ref[j].astype(F32)
        o_ref[0] = acc

    in_specs = [pl.BlockSpec((1, tr, cols), lambda i, pf_ref: (pf_ref[PF_Q], pf_ref[PF_C] * nr + i, 0)),
                pl.BlockSpec((1, tr, cols), lambda i, pf_ref: (pf_ref[PF_Q], i, 0)),
                pl.BlockSpec((3, tr, cols), lambda i, pf_ref: (0, i, 0))]
    args = [pf, g, landed, from_chips]
    aliases = {}
    if prev is not None:
        in_specs.append(pl.BlockSpec(memory_space=pl.ANY))
        args.append(prev)
        aliases = {4: 0}
    return pl.pallas_call(
        body, name=name,
        out_shape=jax.ShapeDtypeStruct((n_layer, r, cols), F32),
        grid_spec=pltpu.PrefetchScalarGridSpec(
            num_scalar_prefetch=1, grid=(nr,), in_specs=in_specs,
            out_specs=pl.BlockSpec((1, tr, cols), lambda i, pf_ref: (layer, pf_ref[PF_C] * nr + i, 0))),
        input_output_aliases=aliases,
        compiler_params=_cparams(1),
    )(*args)


def _allgather_carry(blocks):
    n = len(blocks)

    def copies(b_refs, o_refs, sems):
        send_sems, recv_sems = sems
        x, y, c = _place()
        chips, _ = _other_chips(x, y)

        def place(w, px, py, pc):
            m = blocks[w].shape[0]
            return o_refs[w].at[pl.ds((4 * px + 2 * py + pc) * m, m), :]

        def own_to(w, k, to):
            dst = place(w, x, y, c)
            return pltpu.make_async_remote_copy(src_ref=b_refs[w], dst_ref=dst, send_sem=send_sems.at[w, k],
                                                recv_sem=recv_sems.at[w, k], device_id=to, device_id_type=MESH)

        def landed_from(w, k, px, py, pc):
            blk = place(w, px, py, pc)
            return pltpu.make_async_remote_copy(src_ref=blk, dst_ref=blk, send_sem=send_sems.at[w, k],
                                                recv_sem=recv_sems.at[w, k], device_id=(x, y, 1 - c), device_id_type=MESH)

        return x, y, c, chips, own_to, landed_from

    def start(b_refs, o_refs, sems):
        x, y, c, chips, own_to, _ = copies(b_refs, o_refs, sems)
        for w in range(n):
            own_to(w, 0, (x, y, 1 - c)).start()
            for j, chip in enumerate(chips):
                own_to(w, 1 + j, (*chip, c)).start()

    def mid(b_refs, o_refs, sems):
        x, y, c, chips, _, landed_from = copies(b_refs, o_refs, sems)
        for w in range(n):
            for j, chip in enumerate(chips):
                landed_from(w, 1 + j, *chip, c).wait_recv()
                landed_from(w, 4 + j, *chip, c).start()

    def finish(b_refs, o_refs, sems):
        x, y, c, chips, own_to, landed_from = copies(b_refs, o_refs, sems)
        for w in range(n):
            landed_from(w, 0, x, y, 1 - c).wait_recv()
            for j, chip in enumerate(chips):
                landed_from(w, 4 + j, *chip, 1 - c).wait_recv()
            own_to(w, 0, (x, y, 1 - c)).wait_send()
            for j, chip in enumerate(chips):
                own_to(w, 1 + j, (*chip, c)).wait_send()
                landed_from(w, 4 + j, *chip, c).wait_send()

    return _Carry(blocks, [jax.ShapeDtypeStruct((N_DEV * b.shape[0], b.shape[1]), b.dtype) for b in blocks], {},
                  [pltpu.SemaphoreType.DMA((n, 7)), pltpu.SemaphoreType.DMA((n, 7))], start, finish, mid)


def _sum_devices(gathered, own, me_arr, name):
    m, n = own.shape
    tr = _row_tile(m, n, itemsize=own.dtype.itemsize, target_bytes=256 * 1024)
    nr = m // tr

    def body(me_ref, *refs):
        g_refs, own_ref, o_ref = refs[:N_DEV], refs[N_DEV], refs[N_DEV + 1]
        me = me_ref[0]
        acc = None
        for dev in range(N_DEV):
            term = jnp.where(me == dev, own_ref[...], g_refs[dev][...]).astype(F32)
            acc = term if acc is None else acc + term
        o_ref[...] = acc

    def dev_rows(dev):
        return pl.BlockSpec((tr, n), lambda i, me_ref: (dev * nr + i, 0))

    return pl.pallas_call(
        body, name=name,
        out_shape=jax.ShapeDtypeStruct((m, n), F32),
        grid_spec=pltpu.PrefetchScalarGridSpec(
            num_scalar_prefetch=1, grid=(nr,),
            in_specs=[dev_rows(dev) for dev in range(N_DEV)] + [pl.BlockSpec((tr, n), lambda i, me_ref: (i, 0))],
            out_specs=pl.BlockSpec((tr, n), lambda i, me_ref: (i, 0))),
        compiler_params=_cparams(1),
    )(me_arr, *([gathered] * N_DEV), own)


def _mod_forward(c_all, w_mod, b_mod_shard, name):
    n_layer, d, mq = w_mod.shape

    def body(c_ref, w_ref, b_ref, o_ref):
        cv = c_ref[...]
        o_ref[...] = _dot(cv * _sigmoid(cv), w_ref[0]) + b_ref[0]

    return pl.pallas_call(
        body, name=name, grid=(n_layer,),
        out_shape=jax.ShapeDtypeStruct((n_layer * N_DEV, mq), F32),
        in_specs=[_full((N_DEV, d)), pl.BlockSpec((1, d, mq), lambda l: (l, 0, 0)),
                  pl.BlockSpec((1, 1, mq), lambda l: (l, 0, 0))],
        out_specs=pl.BlockSpec((N_DEV, mq), lambda l: (l, 0)),
        compiler_params=_cparams(1),
    )(c_all, w_mod, b_mod_shard.reshape(n_layer, 1, mq))


def _mod_backward(c_all_t, dmod_shard, name):
    n_layer, _, mq = dmod_shard.shape
    d = c_all_t.shape[0]

    def body(c_ref, dm_ref, o_ref):
        cv = c_ref[...]
        o_ref[0] = _dot(cv * _sigmoid(cv), dm_ref[0])

    return pl.pallas_call(
        body, name=name, grid=(n_layer,),
        out_shape=jax.ShapeDtypeStruct((n_layer, d, mq), F32),
        in_specs=[_full((d, N_DEV)), pl.BlockSpec((1, N_DEV, mq), lambda l: (l, 0, 0))],
        out_specs=pl.BlockSpec((1, d, mq), lambda l: (l, 0, 0)),
        compiler_params=_cparams(1),
    )(c_all_t, dmod_shard)


def _norm_proj(x, mod, vec, w_in, name, carries=()):
    s, d = x.shape
    nq = w_in.shape[2]
    ts = min(TOKENS_MATMUL_TILE, s)

    def body(x_ref, mod_ref, vec_ref, w_ref, h_ref, p_ref, dgel_ref):
        xn, _ = _rms(x_ref[...])
        gm = vec_ref[V_G_PRE_MIX:V_G_PRE_MIX + 1, :] * (1.0 + mod_ref[M_SC_M:M_SC_M + 1, :])
        h = (xn * gm + mod_ref[M_SH_M:M_SH_M + 1, :]).astype(BF16)
        h_ref[...] = h
        for qb in range(N_CHIP):
            pq = _dot(h, w_ref[qb])
            for k in range(N_CHIP * nq // d):
                lo, hi = max(qb * nq, k * d), min((qb + 1) * nq, (k + 1) * d)
                if lo >= hi:
                    continue
                piece = pq[:, lo - qb * nq:hi - qb * nq]
                if k == 4:
                    piece, dgel = _gelu_and_grad(piece)
                    dgel_ref[:, lo - 4 * d:hi - 4 * d] = dgel.astype(BF16)
                elif k >= 5:
                    piece = _sigmoid(piece)
                p_ref[:, lo:hi] = piece.astype(BF16)

    tile = pl.BlockSpec((ts, d), lambda i: (i, 0))
    return _pcall(
        body, name=name, grid=(s // ts,),
        out_shape=[jax.ShapeDtypeStruct((s, d), BF16), jax.ShapeDtypeStruct((s, N_CHIP * nq), BF16),
                   jax.ShapeDtypeStruct((s, d), BF16)],
        in_specs=[tile, _full(mod.shape), _full(vec.shape), _full(w_in.shape)],
        out_specs=[tile, pl.BlockSpec((ts, N_CHIP * nq), lambda i: (i, 0)), tile],
        args=[x, mod, vec, w_in], carries=carries)


def _gate_pre(xb2_b, wg_ref, n_head, bw):
    zr, zi = [], []
    for hd in range(n_head):
        z = _dot(xb2_b[:, hd * bw:(hd + 1) * bw], wg_ref[hd])
        zr.append(z[:, :bw])
        zi.append(z[:, bw:])
    return jnp.concatenate(zr, axis=1), jnp.concatenate(zi, axis=1)


def _lru_coeffs(xb2, wg_ref, vec_ref, n_head, bw):
    zr, zi = _gate_pre(xb2.astype(BF16), wg_ref, n_head, bw)
    r = _sigmoid(zr + vec_ref[V_B_GATE_R:V_B_GATE_R + 1, :])
    gi = _sigmoid(zi + vec_ref[V_B_GATE_I:V_B_GATE_I + 1, :])
    sp = _softplus(-vec_ref[V_LAMBDA:V_LAMBDA + 1, :])
    log_a = (-LRU_C) * r * sp
    a = jnp.exp(log_a)
    mult = jnp.sqrt(_neg_expm1(2.0 * log_a))
    return r, gi, sp, a, mult


def _mixer_forward(x, proj, mod, vec, wg, w_a_out, w_b_out, w_o, name, carries=()):
    s, d = x.shape
    n_head, bw, _ = wg.shape
    ts = min(TOKENS_MIXER_TILE, s)

    def body(x_ref, p_ref, mod_ref, vec_ref, wg_ref, wa_ref, wb_ref, wo_ref,
             x1_ref, conva_ref, xb2_ref, hh_ref, abm_ref, pa_ref, pb_ref, y_ref,
             r_ref, gi_ref, a_ref, mult_ref,
             cv_tail, xb_tail, h_last, a_buf, b_buf, c_buf):
        i = pl.program_id(0)

        @pl.when(i == 0)
        def _():
            cv_tail[...] = jnp.zeros_like(cv_tail)
            xb_tail[...] = jnp.zeros_like(xb_tail)
            h_last[...] = jnp.zeros_like(h_last)

        def seg(k):
            return p_ref[:, k * d:(k + 1) * d].astype(F32)

        def vrow(k):
            return vec_ref[k:k + 1, :]

        b_a, c_a, v_a, x_b, gel, sa, sb = (seg(k) for k in range(7))
        cv = c_a * v_a
        prev_cv = cv_tail[...]
        conv_a = (vrow(V_CONV_A_B) + vrow(V_CONV_A_W) * _shift_down(cv, 2, prev_cv)
                  + vrow(V_CONV_A_W + 1) * _shift_down(cv, 1, prev_cv) + vrow(V_CONV_A_W + 2) * cv)
        cv_tail[...] = cv[ts - SUBLANES:]
        y_a = b_a * conv_a
        prev_xb = xb_tail[...]
        xb2 = (vrow(V_CONV_B_B) + vrow(V_CONV_B_W) * _shift_down(x_b, 3, prev_xb)
               + vrow(V_CONV_B_W + 1) * _shift_down(x_b, 2, prev_xb)
               + vrow(V_CONV_B_W + 2) * _shift_down(x_b, 1, prev_xb) + vrow(V_CONV_B_W + 3) * x_b)
        xb_tail[...] = x_b[ts - SUBLANES:]
        r, gi, _, a, mult = _lru_coeffs(xb2, wg_ref, vec_ref, n_head, bw)
        r_ref[...] = r
        gi_ref[...] = gi
        a_ref[...] = a
        mult_ref[...] = mult
        hh = _scan_two_level(a, mult * gi * xb2, h_last[SUBLANES - 1:SUBLANES, :], a_buf, b_buf, c_buf, reverse=False)
        h_last[...] = hh[ts - SUBLANES:]
        y_b = hh * gel
        ya_b, yb_b = y_a.astype(BF16), y_b.astype(BF16)
        pa = _dot(ya_b, wa_ref[...])
        pb = _dot(yb_b, wb_ref[...])
        m = (sa * pa + sb * pb).astype(BF16)
        y = _dot(m, wo_ref[...])
        yn, _ = _rms(y)
        gg = mod_ref[M_GT_M:M_GT_M + 1, :] * vrow(V_G_POST_MIX)
        x1_ref[...] = x_ref[...] + yn * gg
        conva_ref[...] = conv_a.astype(BF16)
        xb2_ref[...] = xb2
        hh_ref[...] = hh
        abm_ref[0] = ya_b
        abm_ref[1] = yb_b
        abm_ref[2] = m
        pa_ref[...] = pa.astype(BF16)
        pb_ref[...] = pb.astype(BF16)
        y_ref[...] = y.astype(BF16)

    tile = pl.BlockSpec((ts, d), lambda i: (i, 0))
    tile3 = pl.BlockSpec((3, ts, d), lambda i: (0, i, 0))
    sd = lambda dt: jax.ShapeDtypeStruct((s, d), dt)
    return _pcall(
        body, name=name, grid=(s // ts,),
        out_shape=[sd(F32), sd(BF16), sd(F32), sd(F32), jax.ShapeDtypeStruct((3, s, d), BF16), sd(BF16), sd(BF16), sd(BF16),
                   sd(F32), sd(F32), sd(F32), sd(F32)],
        in_specs=[tile, pl.BlockSpec((ts, 7 * d), lambda i: (i, 0)), _full(mod.shape), _full(vec.shape),
                  _full(wg.shape), _full(w_a_out.shape), _full(w_b_out.shape), _full(w_o.shape)],
        out_specs=[tile] * 4 + [tile3] + [tile] * 7,
        scratch_shapes=[pltpu.VMEM((SUBLANES, d), F32)] * 3 + [pltpu.VMEM((d // LANES, ts, LANES), F32)] * 2
                       + [pltpu.VMEM((ts // SCAN_GROUP, d), F32)],
        args=[x, proj, mod, vec, wg, w_a_out, w_b_out, w_o], carries=carries)


def _mlp_forward(x1, mod, vec, w_up, w_down, name, carries=(), target=None):
    s, d = x1.shape
    fq = w_up.shape[2]
    ts = min(TOKENS_MATMUL_TILE, s)

    def body(x_ref, *refs):
        if target is None:
            mod_ref, vec_ref, wu_ref, wd_ref, x2_ref, h2_ref, up_ref, y2_ref = refs
        else:
            t_ref, mod_ref, vec_ref, wu_ref, wd_ref, x2_ref, h2_ref, up_ref, y2_ref, loss_ref = refs
        x = x_ref[...]
        xn, _ = _rms(x)
        gm = vec_ref[V_G_PRE_MLP:V_G_PRE_MLP + 1, :] * (1.0 + mod_ref[M_SC_F:M_SC_F + 1, :])
        h2 = (xn * gm + mod_ref[M_SH_F:M_SH_F + 1, :]).astype(BF16)
        h2_ref[...] = h2
        y2 = jnp.zeros((ts, d), F32)
        for qb in range(N_CHIP):
            up = _dot(h2, wu_ref[qb])
            up_ref[:, qb * fq:(qb + 1) * fq] = up.astype(BF16)
            ru = jnp.maximum(up, 0.0)
            y2 = y2 + _dot((ru * ru).astype(BF16), wd_ref[qb])
        y2_ref[...] = y2.astype(BF16)
        yn, _ = _rms(y2)
        gg = mod_ref[M_GT_F:M_GT_F + 1, :] * vec_ref[V_G_POST_MLP:V_G_POST_MLP + 1, :]
        x2 = x + yn * gg
        if target is None:
            x2_ref[...] = x2
        else:
            @pl.when(pl.program_id(0) == 0)
            def _():
                loss_ref[...] = jnp.zeros_like(loss_ref)

            err = x2 - t_ref[...]
            x2_ref[...] = err * (1.0 / d)
            loss_ref[...] += jnp.sum(jnp.sum(err * err, axis=1, keepdims=True), axis=0, keepdims=True) * (0.5 / d)

    tile = pl.BlockSpec((ts, d), lambda i: (i, 0))
    last = target is not None
    return _pcall(
        body, name=name, grid=(s // ts,),
        out_shape=[jax.ShapeDtypeStruct((s, d), F32), jax.ShapeDtypeStruct((s, d), BF16),
                   jax.ShapeDtypeStruct((s, N_CHIP * fq), BF16), jax.ShapeDtypeStruct((s, d), BF16)]
                  + ([jax.ShapeDtypeStruct((SUBLANES, LANES), F32)] if last else []),
        in_specs=[tile] + ([tile] if last else []) + [_full(mod.shape), _full(vec.shape), _full(w_up.shape), _full(w_down.shape)],
        out_specs=[tile, tile, pl.BlockSpec((ts, N_CHIP * fq), lambda i: (i, 0)), tile]
                 + ([_full((SUBLANES, LANES))] if last else []),
        args=[x1] + ([target] if last else []) + [mod, vec, w_up, w_down], carries=carries)


SB3_DSH, SB3_DSC, SB3_DGT, SB3_DG_PRE, SB3_DG_POST = range(5)
SB1_DSH, SB1_DSC, SB1_DG_PRE = range(3)
(SB2_DGT, SB2_DG_POST, SB2_DWA, SB2_DBA, SB2_DWB, SB2_DBB, SB2_DLAM, SB2_DBR, SB2_DBI) = (0, 1, 2, 5, 6, 10, 11, 12, 13)


def _mlp_backward(dx2, x1, y2, up, mod, vec, w_up, w_down, name, carries=()):
    s, d = dx2.shape
    fq = w_up.shape[2]
    ts = min(TOKENS_MATMUL_TILE, s)
    n_t = s // ts

    def body(dx2_ref, x_ref, y2_ref, up_ref, mod_ref, vec_ref, wu_ref, wd_ref,
             dx1_ref, dy2_ref, dup_ref, small_ref):
        i = pl.program_id(0)

        @pl.when(i == 0)
        def _():
            small_ref[...] = jnp.zeros_like(small_ref)

        dout = dx2_ref[...]
        y2n, ry = _rms(y2_ref[...].astype(F32))
        g_post = vec_ref[V_G_POST_MLP:V_G_POST_MLP + 1, :]
        gt = mod_ref[M_GT_F:M_GT_F + 1, :]
        dgg = _colsum(dout * y2n)
        dy2 = _rms_bwd(dout * (gt * g_post), y2n, ry).astype(BF16)
        dy2_ref[...] = dy2
        dh2 = jnp.zeros((ts, d), F32)
        for qb in range(N_CHIP):
            cols = slice(qb * fq, (qb + 1) * fq)
            dact = _dot_tb(dy2, wd_ref[qb])
            ru = jnp.maximum(up_ref[:, cols].astype(F32), 0.0)
            dup = (dact * (2.0 * ru)).astype(BF16)
            dup_ref[:, cols] = dup
            dh2 = dh2 + _dot_tb(dup, wu_ref[qb])
        xn, r = _rms(x_ref[...])
        g_pre = vec_ref[V_G_PRE_MLP:V_G_PRE_MLP + 1, :]
        sc1 = 1.0 + mod_ref[M_SC_F:M_SC_F + 1, :]
        dsh = _colsum(dh2)
        dgm = _colsum(dh2 * xn)
        dx1_ref[...] = dout + _rms_bwd(dh2 * (g_pre * sc1), xn, r)
        small_ref[SB3_DSH:SB3_DSH + 1, :] += dsh
        small_ref[SB3_DSC:SB3_DSC + 1, :] += dgm
        small_ref[SB3_DGT:SB3_DGT + 1, :] += dgg

        @pl.when(i == n_t - 1)
        def _():
            dgm_t = small_ref[SB3_DSC:SB3_DSC + 1, :]
            dgg_t = small_ref[SB3_DGT:SB3_DGT + 1, :]
            small_ref[SB3_DSC:SB3_DSC + 1, :] = dgm_t * g_pre
            small_ref[SB3_DG_PRE:SB3_DG_PRE + 1, :] = dgm_t * sc1
            small_ref[SB3_DGT:SB3_DGT + 1, :] = dgg_t * g_post
            small_ref[SB3_DG_POST:SB3_DG_POST + 1, :] = dgg_t * gt

    tile = pl.BlockSpec((ts, d), lambda i: (i, 0))
    wide = pl.BlockSpec((ts, N_CHIP * fq), lambda i: (i, 0))
    return _pcall(
        body, name=name, grid=(n_t,),
        out_shape=[jax.ShapeDtypeStruct((s, d), F32), jax.ShapeDtypeStruct((s, d), BF16),
                   jax.ShapeDtypeStruct((s, N_CHIP * fq), BF16), jax.ShapeDtypeStruct((SUBLANES, d), F32)],
        in_specs=[tile, tile, tile, wide, _full(mod.shape), _full(vec.shape), _full(w_up.shape), _full(w_down.shape)],
        out_specs=[tile, tile, wide, _full((SUBLANES, d))],
        args=[dx2, x1, y2, up, mod, vec, w_up, w_down], carries=carries)


def _mixer_backward(dx1, proj, conva, xb2s, hhs, pas, pbs, ys, rs_, gis, as_, mults, dgels, mod, vec, wg, w_a_out, w_b_out,
                    w_o, name, carries=()):
    s, d = dx1.shape
    n_head, bw, _ = wg.shape
    ts = min(TOKENS_MIXER_TILE, s)
    n_t = s // ts

    def body(dx1_ref, p_ref, conva_ref, xb2_ref, hh_ref, pa_ref, pb_ref, y_ref, r_ref, gi_ref, a_ref, mult_ref, dgel_ref,
             mod_ref, vec_ref, wg_ref, wa_ref, wb_ref, wo_ref,
             dp_ref, dab_ref, small_ref, dwg_ref,
             dconv_head, dxb2_head, a_head, g_head, a_buf, b_buf, c_buf):
        i = pl.program_id(0)

        @pl.when(i == 0)
        def _():
            small_ref[...] = jnp.zeros_like(small_ref)
            dwg_ref[...] = jnp.zeros_like(dwg_ref)
            dconv_head[...] = jnp.zeros_like(dconv_head)
            dxb2_head[...] = jnp.zeros_like(dxb2_head)
            a_head[...] = jnp.zeros_like(a_head)
            g_head[...] = jnp.zeros_like(g_head)

        def seg(k):
            return p_ref[:, k * d:(k + 1) * d].astype(F32)

        def vrow(k):
            return vec_ref[k:k + 1, :]

        def acc(row, val):
            small_ref[row:row + 1, :] += val

        dout = dx1_ref[...]
        yn, ry = _rms(y_ref[...].astype(F32))
        g_post = vrow(V_G_POST_MIX)
        gt = mod_ref[M_GT_M:M_GT_M + 1, :]
        acc(SB2_DGT, _colsum(dout * yn))
        dy = _rms_bwd(dout * (gt * g_post), yn, ry).astype(BF16)
        dab_ref[2] = dy
        dm = _dot_tb(dy, wo_ref[...])
        sa, sb = seg(5), seg(6)
        dpa = (dm * sa).astype(BF16)
        dpb = (dm * sb).astype(BF16)
        dab_ref[0] = dpa
        dab_ref[1] = dpb
        du_a = dm * pa_ref[...].astype(F32) * (sa * (1.0 - sa))
        du_b = dm * pb_ref[...].astype(F32) * (sb * (1.0 - sb))
        dp_ref[:, 5 * d:6 * d] = du_a.astype(BF16)
        dp_ref[:, 6 * d:7 * d] = du_b.astype(BF16)
        dy_a = _dot_tb(dpa, wa_ref[...])
        dy_b = _dot_tb(dpb, wb_ref[...])

        b_a, c_a, v_a = seg(0), seg(1), seg(2)
        dp_ref[:, 0:d] = (dy_a * conva_ref[...].astype(F32)).astype(BF16)
        dconv = dy_a * b_a
        nxt = dconv_head[...]
        d1 = _shift_up(dconv, 1, nxt)
        d2 = _shift_up(dconv, 2, nxt)
        dconv_head[...] = dconv[:SUBLANES]
        dcv = vrow(V_CONV_A_W + 2) * dconv + vrow(V_CONV_A_W + 1) * d1 + vrow(V_CONV_A_W) * d2
        cv = c_a * v_a
        acc(SB2_DWA + 2, _colsum(cv * dconv))
        acc(SB2_DWA + 1, _colsum(cv * d1))
        acc(SB2_DWA, _colsum(cv * d2))
        acc(SB2_DBA, _colsum(dconv))
        dp_ref[:, d:2 * d] = (dcv * v_a).astype(BF16)
        dp_ref[:, 2 * d:3 * d] = (dcv * c_a).astype(BF16)

        x_b, gel = seg(3), seg(4)
        hh = hh_ref[...]
        dp_ref[:, 4 * d:5 * d] = (dy_b * hh * dgel_ref[...].astype(F32)).astype(BF16)
        dhh = dy_b * gel
        xb2 = xb2_ref[...]
        r, gi, a, mult = r_ref[...], gi_ref[...], a_ref[...], mult_ref[...]
        sp = _softplus(-vrow(V_LAMBDA))
        a_next = _shift_up(a, 1, a_head[...])
        g = _scan_two_level(a_next, dhh, g_head[0:1, :], a_buf, b_buf, c_buf, reverse=True)
        a_head[...] = a[:SUBLANES]
        g_head[...] = g[:SUBLANES]
        gix = gi * xb2
        gm = g * mult
        dlog_a = g * (hh - mult * gix) - (g * gix) * (a * a / mult)
        dgi = gm * xb2
        dxb2 = gm * gi
        acc(SB2_DLAM, _colsum(dlog_a * r))
        dzr = dlog_a * ((-LRU_C) * sp) * (r * (1.0 - r))
        dzi = dgi * (gi * (1.0 - gi))
        acc(SB2_DBR, _colsum(dzr))
        acc(SB2_DBI, _colsum(dzi))
        xb2_b = xb2.astype(BF16)
        back = []
        for hd in range(n_head):
            cols = slice(hd * bw, (hd + 1) * bw)
            dz = jnp.concatenate([dzr[:, cols], dzi[:, cols]], axis=1).astype(BF16)
            back.append(_dot_tb(dz, wg_ref[hd]))
            dwg_ref[hd] += _dot_ta(xb2_b[:, cols], dz)
        dxb2 = dxb2 + jnp.concatenate(back, axis=1)
        nxt = dxb2_head[...]
        e1 = _shift_up(dxb2, 1, nxt)
        e2 = _shift_up(dxb2, 2, nxt)
        e3 = _shift_up(dxb2, 3, nxt)
        dxb2_head[...] = dxb2[:SUBLANES]
        dp_ref[:, 3 * d:4 * d] = (vrow(V_CONV_B_W + 3) * dxb2 + vrow(V_CONV_B_W + 2) * e1
                                  + vrow(V_CONV_B_W + 1) * e2 + vrow(V_CONV_B_W) * e3).astype(BF16)
        acc(SB2_DWB + 3, _colsum(x_b * dxb2))
        acc(SB2_DWB + 2, _colsum(x_b * e1))
        acc(SB2_DWB + 1, _colsum(x_b * e2))
        acc(SB2_DWB, _colsum(x_b * e3))
        acc(SB2_DBB, _colsum(dxb2))

        @pl.when(i == n_t - 1)
        def _():
            dgg_t = small_ref[SB2_DGT:SB2_DGT + 1, :]
            small_ref[SB2_DGT:SB2_DGT + 1, :] = dgg_t * g_post
            small_ref[SB2_DG_POST:SB2_DG_POST + 1, :] = dgg_t * gt
            lam = vrow(V_LAMBDA)
            small_ref[SB2_DLAM:SB2_DLAM + 1, :] = small_ref[SB2_DLAM:SB2_DLAM + 1, :] * (LRU_C * _sigmoid(-lam))

    rev = lambda i: (n_t - 1 - i, 0)
    tile = pl.BlockSpec((ts, d), rev)
    wide = pl.BlockSpec((ts, 7 * d), rev)
    sd = lambda dt: jax.ShapeDtypeStruct((s, d), dt)
    return _pcall(
        body, name=name, grid=(n_t,),
        out_shape=[jax.ShapeDtypeStruct((s, 7 * d), BF16), jax.ShapeDtypeStruct((3, s, d), BF16),
                   jax.ShapeDtypeStruct((2 * SUBLANES, d), F32), jax.ShapeDtypeStruct(wg.shape, F32)],
        in_specs=[tile, wide] + [tile] * 11 + [_full(mod.shape), _full(vec.shape),
                  _full(wg.shape), _full(w_a_out.shape), _full(w_b_out.shape), _full(w_o.shape)],
        out_specs=[wide, pl.BlockSpec((3, ts, d), lambda i: (0, n_t - 1 - i, 0)), _full((2 * SUBLANES, d)), _full(wg.shape)],
        scratch_shapes=[pltpu.VMEM((SUBLANES, d), F32)] * 4 + [pltpu.VMEM((d // LANES, ts, LANES), F32)] * 2
                       + [pltpu.VMEM((ts // SCAN_GROUP, d), F32)],
        args=[dx1, proj, conva, xb2s, hhs, pas, pbs, ys, rs_, gis, as_, mults, dgels, mod, vec, wg, w_a_out, w_b_out, w_o],
        carries=carries)


def _proj_backward(dproj, dx1, x, mod, vec, w_in, name, carries=()):
    s, d = x.shape
    nq = w_in.shape[2]
    ts = min(TOKENS_MATMUL_TILE, s)
    n_t = s // ts

    def body(dp_ref, dx1_ref, x_ref, mod_ref, vec_ref, w_ref, dx_ref, small_ref):
        i = pl.program_id(0)

        @pl.when(i == 0)
        def _():
            small_ref[...] = jnp.zeros_like(small_ref)

        dh = jnp.zeros((ts, d), F32)
        for qb in range(N_CHIP):
            dh = dh + _dot_tb(dp_ref[:, qb * nq:(qb + 1) * nq], w_ref[qb])
        xn, r = _rms(x_ref[...])
        g_pre = vec_ref[V_G_PRE_MIX:V_G_PRE_MIX + 1, :]
        sc1 = 1.0 + mod_ref[M_SC_M:M_SC_M + 1, :]
        dx_ref[...] = dx1_ref[...] + _rms_bwd(dh * (g_pre * sc1), xn, r)
        small_ref[SB1_DSH:SB1_DSH + 1, :] += _colsum(dh)
        small_ref[SB1_DSC:SB1_DSC + 1, :] += _colsum(dh * xn)

        @pl.when(i == n_t - 1)
        def _():
            dgm_t = small_ref[SB1_DSC:SB1_DSC + 1, :]
            small_ref[SB1_DSC:SB1_DSC + 1, :] = dgm_t * g_pre
            small_ref[SB1_DG_PRE:SB1_DG_PRE + 1, :] = dgm_t * sc1

    tile = pl.BlockSpec((ts, d), lambda i: (i, 0))
    return _pcall(
        body, name=name, grid=(n_t,),
        out_shape=[jax.ShapeDtypeStruct((s, d), F32), jax.ShapeDtypeStruct((SUBLANES, d), F32)],
        in_specs=[pl.BlockSpec((ts, N_CHIP * nq), lambda i: (i, 0)), tile, tile, _full(mod.shape), _full(vec.shape),
                  _full(w_in.shape)],
        out_specs=[tile, _full((SUBLANES, d))],
        args=[dproj, dx1, x, mod, vec, w_in], carries=carries)


def _weight_grad(a, b, name, col_blocks=1, tk=512, carries=(), square_relu=False):
    s, k = a.shape
    n = b.shape[1]
    tn = n // col_blocks
    tk = min(tk, k)

    def body(a_ref, b_ref, o_ref):
        av = a_ref[...]
        if square_relu:
            ru = jnp.maximum(av.astype(F32), 0.0)
            av = (ru * ru).astype(BF16)
        o_ref[0] = _dot_ta(av, b_ref[...])

    (out,), carried = _pcall(
        body, name=name, grid=(col_blocks, k // tk),
        out_shape=[jax.ShapeDtypeStruct((col_blocks, k, tn), F32)],
        in_specs=[pl.BlockSpec((s, tk), lambda j, i: (0, i)), pl.BlockSpec((s, tn), lambda j, i: (0, j))],
        out_specs=[pl.BlockSpec((1, tk, tn), lambda j, i: (j, i, 0))],
        args=[a, b], carries=carries)
    return out, carried


def _weight_grad_stacked(a3, b3, name, tk=512, carries=()):
    n_g, s, k = a3.shape
    n = b3.shape[2]
    kq = k // N_CHIP
    tk = min(tk, k)
    chips_per_tile = tk // kq

    def body(a_ref, b_ref, o_ref):
        o_ref[...] = _dot_ta(a_ref[...], b_ref[...]).reshape(chips_per_tile, kq, n)

    (out,), carried = _pcall(
        body, name=name, grid=(n_g, k // tk),
        out_shape=[jax.ShapeDtypeStruct((N_CHIP, n_g, kq, n), F32)],
        in_specs=[pl.BlockSpec((None, s, tk), lambda g, i: (g, 0, i)), pl.BlockSpec((None, s, n), lambda g, i: (g, 0, 0))],
        out_specs=[pl.BlockSpec((chips_per_tile, None, kq, n), lambda g, i: (i, g, 0, 0))],
        args=[a3, b3], carries=carries)
    return out.reshape(N_CHIP, n_g * kq, n), carried


def _adamw(items, name, copy_grad=False, carries=()):
    shape = items[0][0].shape
    cols = shape[-1]
    rows = items[0][0].size // cols
    tr = _row_tile(rows, cols, target_bytes=1024 * 1024 // len(items))
    c1 = 1.0 - ADAM_B1 ** ADAM_STEP
    c2 = 1.0 - ADAM_B2 ** ADAM_STEP
    n_out = 4 if copy_grad else 3
    n = len(items)

    def body(*refs):
        for k in range(n):
            w_ref, g_ref, m_ref, v_ref = refs[4 * k:4 * k + 4]
            outs = refs[4 * n + n_out * k:4 * n + n_out * (k + 1)]
            gv = g_ref[...]
            nm = ADAM_B1 * m_ref[...] + (1.0 - ADAM_B1) * gv
            nv = ADAM_B2 * v_ref[...] + (1.0 - ADAM_B2) * (gv * gv)
            outs[0][...] = (-ADAM_LR) * ((nm / c1) / (jnp.sqrt(nv / c2) + ADAM_EPS) + ADAM_WD * w_ref[...])
            outs[1][...] = nm
            outs[2][...] = nv
            if copy_grad:
                outs[3][...] = gv

    spec = pl.BlockSpec((tr, cols), lambda i: (i, 0))
    outs, carried = _pcall(
        body, name=name, grid=(rows // tr,),
        out_shape=[jax.ShapeDtypeStruct((rows, cols), F32)] * (n_out * n),
        in_specs=[spec] * (4 * n), out_specs=[spec] * (n_out * n),
        args=[t.reshape(rows, cols) for item in items for t in item], carries=carries)
    return [tuple(o.reshape(shape) for o in outs[n_out * k:n_out * (k + 1)]) for k in range(n)], carried


def kernel(x, c, w_mod, b_mod, g_pre_mix, g_post_mix, w_in, conv_a_w, conv_a_b, w_a_out, conv_b_w, conv_b_b, w_gate_r, b_gate_r, w_gate_i, b_gate_i, lru_lambda, w_b_out, w_o, g_pre_mlp, g_post_mlp, w_mlp_up, w_mlp_down, loss_target, m_w_mod, m_b_mod, m_g_pre_mix, m_g_post_mix, m_w_in, m_conv_a_w, m_conv_a_b, m_w_a_out, m_conv_b_w, m_conv_b_b, m_w_gate_r, m_b_gate_r, m_w_gate_i, m_b_gate_i, m_lru_lambda, m_w_b_out, m_w_o, m_g_pre_mlp, m_g_post_mlp, m_w_mlp_up, m_w_mlp_down, v_w_mod, v_b_mod, v_g_pre_mix, v_g_post_mix, v_w_in, v_conv_a_w, v_conv_a_b, v_w_a_out, v_conv_b_w, v_conv_b_b, v_w_gate_r, v_b_gate_r, v_w_gate_i, v_b_gate_i, v_lru_lambda, v_w_b_out, v_w_o, v_g_pre_mlp, v_g_post_mlp, v_w_mlp_up, v_w_mlp_down):
    weights = dict(w_mod=w_mod, b_mod=b_mod, g_pre_mix=g_pre_mix, g_post_mix=g_post_mix, w_in=w_in, conv_a_w=conv_a_w,
                   conv_a_b=conv_a_b, w_a_out=w_a_out, conv_b_w=conv_b_w, conv_b_b=conv_b_b, w_gate_r=w_gate_r,
                   b_gate_r=b_gate_r, w_gate_i=w_gate_i, b_gate_i=b_gate_i, lru_lambda=lru_lambda, w_b_out=w_b_out,
                   w_o=w_o, g_pre_mlp=g_pre_mlp, g_post_mlp=g_post_mlp, w_mlp_up=w_mlp_up, w_mlp_down=w_mlp_down)
    mom1 = dict(w_mod=m_w_mod, b_mod=m_b_mod, g_pre_mix=m_g_pre_mix, g_post_mix=m_g_post_mix, w_in=m_w_in,
                conv_a_w=m_conv_a_w, conv_a_b=m_conv_a_b, w_a_out=m_w_a_out, conv_b_w=m_conv_b_w, conv_b_b=m_conv_b_b,
                w_gate_r=m_w_gate_r, b_gate_r=m_b_gate_r, w_gate_i=m_w_gate_i, b_gate_i=m_b_gate_i,
                lru_lambda=m_lru_lambda, w_b_out=m_w_b_out, w_o=m_w_o, g_pre_mlp=m_g_pre_mlp, g_post_mlp=m_g_post_mlp,
                w_mlp_up=m_w_mlp_up, w_mlp_down=m_w_mlp_down)
    mom2 = dict(w_mod=v_w_mod, b_mod=v_b_mod, g_pre_mix=v_g_pre_mix, g_post_mix=v_g_post_mix, w_in=v_w_in,
                conv_a_w=v_conv_a_w, conv_a_b=v_conv_a_b, w_a_out=v_w_a_out, conv_b_w=v_conv_b_w, conv_b_b=v_conv_b_b,
                w_gate_r=v_w_gate_r, b_gate_r=v_b_gate_r, w_gate_i=v_w_gate_i, b_gate_i=v_b_gate_i,
                lru_lambda=v_lru_lambda, w_b_out=v_w_b_out, w_o=v_w_o, g_pre_mlp=v_g_pre_mlp, g_post_mlp=v_g_post_mlp,
                w_mlp_up=v_w_mlp_up, w_mlp_down=v_w_mlp_down)
    names = list(weights)

    n_layer = w_in.shape[0]
    s, d = x.shape[1], x.shape[2]
    n_head, bw = w_gate_r.shape[1], w_gate_r.shape[2]
    dq = d // N_CHIP
    mq = w_mod.shape[2]
    n_mod = (N_CHIP * mq) // d
    ka, kb = conv_a_w.shape[1], conv_b_w.shape[1]

    mx, my, mc = _place()
    q_me = 2 * mx + my
    q_arr = jnp.reshape(q_me, (1,)).astype(jnp.int32)

    me_dev = 4 * mx + 2 * my + mc
    me_arr = jnp.reshape(me_dev, (1,)).astype(jnp.int32)

    big_names = ["w_in", "w_a_out", "w_b_out", "w_o", "w_mlp_up", "w_mlp_down"]
    groups = [["w_in"], ["w_a_out", "w_b_out", "w_o"], ["w_mlp_up", "w_mlp_down"]]
    placed = {("w_in", 0): _cast_place_all([(w_in, 0)], q_arr, "cast_place_first")[0][0]}
    wfull = [dict() for _ in range(n_layer)]
    riders = {}
    for l in range(n_layer):
        riders.setdefault(3 * l - 1, []).append(([("w_in", l)], 0.9 if l else 1.0))
        riders.setdefault(3 * l - 2 if l else 0, []).append(([(nm, l) for nm in groups[1]], 0.9 if l else 0.5))
        riders.setdefault(3 * l, []).append(([("w_mlp_up", l)], 0.7 if l else 0.9))
        riders.setdefault(3 * l + 1, []).insert(0, ([("w_mlp_down", l)], 0.5))

    def gather_carry(call):
        return [_gather_carry([placed[k] for k in keys], frac) for keys, frac in riders.get(call, [])]

    def gathered(call, carried):
        for (keys, _), ws in zip(riders.get(call, []), carried):
            for (nm, l), w in zip(keys, ws):
                wfull[l][nm] = w.reshape(d, d) if nm in groups[1] else w

    n_conv_rows = n_layer * (ka + kb)
    conv_blk = -(-n_conv_rows // SUBLANES) * SUBLANES
    blk_rows = SUBLANES + conv_blk
    conv_rows = jnp.concatenate([jnp.concatenate([conv_a_w[l], conv_b_w[l]], axis=0) for l in range(n_layer)], axis=0)
    conv_rows = jnp.pad(conv_rows, ((0, conv_blk - n_conv_rows), (0, d - dq)))
    c_conv = jnp.concatenate([jnp.pad(c, ((0, SUBLANES - 1), (0, 0))), conv_rows], axis=0)
    rest = [(nm, l) for l in range(n_layer) for nm in big_names if (nm, l) != ("w_in", 0)]
    rest_placed, carried = _cast_place_all([(weights[nm], l) for nm, l in rest], q_arr, "cast_place_rest",
                                           carries=gather_carry(-1) + [_allgather_carry([c_conv])])
    placed.update(zip(rest, rest_placed))
    gathered(-1, carried[:1])
    gathered1 = lax.dynamic_update_slice(carried[1][0], c_conv, (me_dev * blk_rows, 0)).reshape(N_DEV, blk_rows, d)
    c_all = gathered1[:, 0, :]
    conv_full = jnp.concatenate([gathered1[2 * qb, SUBLANES:SUBLANES + n_conv_rows, :dq] for qb in range(N_CHIP)], axis=1)

    b_mod_shard = lax.dynamic_slice_in_dim(b_mod, q_me * mq, mq, axis=1)
    mod_part = _mod_forward(c_all, w_mod, b_mod_shard, "mod_forward")
    gathered2 = _all_gather_small(mod_part, "gather_mod").reshape(N_DEV, n_layer, N_DEV, mq)
    mod_rows = jnp.concatenate(
        [lax.dynamic_index_in_dim(gathered2[2 * qb], me_dev, axis=1, keepdims=False) for qb in range(N_CHIP)], axis=1)
    mods = [jnp.pad(mod_rows[l].reshape(n_mod, d), ((0, SUBLANES - n_mod), (0, 0))) for l in range(n_layer)]

    vecs = []
    for l in range(n_layer):
        base = l * (ka + kb)
        rows = [g_pre_mix[l], g_post_mix[l], conv_a_b[l], conv_b_b[l], b_gate_r[l], b_gate_i[l], lru_lambda[l],
                g_pre_mlp[l], g_post_mlp[l]]
        vecs.append(jnp.concatenate([jnp.stack(rows, axis=0), conv_full[base:base + ka + kb]], axis=0))

    wgs =[jnp.concatenate([w_gate_r[l], w_gate_i[l]], axis=-1).astype(BF16) for l in range(n_layer)]

    xs = x[0]
    saved = []
    for l in range(n_layer):
        wl = wfull[l]
        (h, proj, dgel), carried = _norm_proj(xs, mods[l], vecs[l], wl["w_in"], f"norm_proj_{l}", gather_carry(3 * l))
        gathered(3 * l, carried)
        (x1, conva, xb2, hh, abm, pa, pb, yy, gr, ggi, ga, gmult), carried = _mixer_forward(
            xs, proj, mods[l], vecs[l], wgs[l], wl["w_a_out"], wl["w_b_out"], wl["w_o"], f"mixer_forward_{l}",
            gather_carry(3 * l + 1))
        gathered(3 * l + 1, carried)
        (x2, h2, up, y2, *loss_tile), carried = _mlp_forward(
            x1, mods[l], vecs[l], wl["w_mlp_up"], wl["w_mlp_down"], f"mlp_forward_{l}", gather_carry(3 * l + 2),
            target=loss_target[0] if l == n_layer - 1 else None)
        gathered(3 * l + 2, carried)
        saved.append(dict(x=xs, h=h, proj=proj, x1=x1, conva=conva, xb2=xb2, hh=hh, abm=abm, pa=pa, pb=pb,
                          y=yy, r=gr, gi=ggi, a=ga, mult=gmult, dgel=dgel, h2=h2, up=up, y2=y2))
        xs = x2
    dxs = xs
    loss_block = jnp.pad(loss_tile[0], ((0, 0), (0, d - LANES)))

    chips_q = [q_me ^ 2, q_me ^ 1, q_me ^ 3]
    pf = jnp.stack([mc, q_me] + chips_q).astype(jnp.int32)
    rs = dict(grad={}, landed={}, to_send={}, from_chips={}, out={})
    to_exchange, to_scatter, to_join, to_gather = [], [], [], []
    small_own, small_all = {}, {}

    def ride(call, what, name=None):
        ex = list(to_exchange) if "x" in what else []
        sc = list(to_scatter) if "s" in what else []
        ga = list(to_gather) if "g" in what else []
        jn = []
        for key in (to_join if "j" in what else []):
            if key[0] not in [k[0] for k in jn]:
                jn.append(key)
        carries = []
        if ex:
            carries.append(_exchange_carry([rs["grad"][k] for k in ex]))
        if sc:
            carries.append(_scatter_carry([rs["to_send"][k] for k in sc]))
        if jn:
            carries.append(_join_carry([rs["out"][k[0]] for k in jn], [k[1] for k in jn]))
        if ga:
            carries.append(_allgather_carry([small_own[k] for k in ga]))
        if call is None:
            carried = _run_carries(carries, name) if carries else []
            res = None
        else:
            res, carried = call(carries)
        carried = list(carried)
        if ex:
            for k, ld in zip(ex, carried.pop(0)):
                to_exchange.remove(k)
                rs["landed"][k] = ld
                rs["to_send"][k] = _add_sibling_half(rs["grad"][k], ld, pf, f"rs_add_sibling_{k[0]}_{k[1]}")
                to_scatter.append(k)
        if sc:
            for k, fc in zip(sc, carried.pop(0)):
                to_scatter.remove(k)
                rs["out"][k[0]] = _add_chips(rs["grad"][k], rs["landed"][k], fc, pf, rs["out"].get(k[0]), k[1], n_layer,
                                             f"rs_add_chips_{k[0]}_{k[1]}")
                to_join.append(k)
        if jn:
            for k, o in zip(jn, carried.pop(0)):
                to_join.remove(k)
                rs["out"][k[0]] = o
        if ga:
            for k, o in zip(ga, carried.pop(0)):
                to_gather.remove(k)
                small_all[k] = o
        return res

    def gather_small(key, parts):
        small_own[key] = parts[0] if len(parts) == 1 else jnp.concatenate(parts, axis=0)
        to_gather.append(key)

    def ready(nm, l, g):
        rs["grad"][(nm, l)] = g
        to_exchange.append((nm, l))

    rowblk = lambda t: t.reshape(N_CHIP, t.shape[1] // N_CHIP, t.shape[2])
    small1_prev = None
    for l in reversed(range(n_layer)):
        wl, sv = wfull[l], saved[l]
        dx1, dy2, dup, small3 = ride(lambda cr: _mlp_backward(
            dxs, sv["x1"], sv["y2"], sv["up"], mods[l], vecs[l], wl["w_mlp_up"], wl["w_mlp_down"], f"mlp_backward_{l}", cr), "xsjg")
        ready("w_mlp_up", l, _weight_grad(sv["h2"], dup, f"grad_w_mlp_up_{l}", col_blocks=N_CHIP, tk=d)[0])
        g_down = ride(lambda cr: _weight_grad(sv["up"], dy2, f"grad_w_mlp_down_{l}", carries=cr, square_relu=True), "x")
        ready("w_mlp_down", l, rowblk(g_down))
        dproj, dab, small2, dwg = ride(lambda cr: _mixer_backward(
            dx1, sv["proj"], sv["conva"], sv["xb2"], sv["hh"], sv["pa"], sv["pb"], sv["y"],
            sv["r"], sv["gi"], sv["a"], sv["mult"], sv["dgel"], mods[l], vecs[l], wgs[l],
            wl["w_a_out"], wl["w_b_out"], wl["w_o"], f"mixer_backward_{l}", cr), "xsjg")
        gather_small(("late", l, "s"), ([small1_prev] if small1_prev is not None else []) + [small2, small3])
        gather_small(("late", l, "w"), [dwg.reshape(2 * bw, d).astype(BF16)])
        g_in = ride(lambda cr: _weight_grad(sv["h"], dproj, f"grad_w_in_{l}", col_blocks=N_CHIP, carries=cr), "xsj")
        ready("w_in", l, g_in)
        g_abo = ride(lambda cr: _weight_grad_stacked(sv["abm"], dab, f"grad_w_abo_{l}", tk=d, carries=cr), "xg")
        ready("w_abo", l, g_abo)
        dxs, small1_prev = ride(lambda cr: _proj_backward(dproj, dx1, sv["x"], mods[l], vecs[l], wl["w_in"],
                                                          f"proj_backward_{l}", cr), "xsjg")
    grad_x = dxs[None]
    gather_small(("last", 0, "s"), [small1_prev, loss_block])

    tail = 0
    while to_exchange or to_scatter or to_join or to_gather:
        ride(None, "xsjg", f"rs_tail_{tail}")
        tail += 1
    grads, deltas, new_m, new_v = {}, {}, {}, {}

    def adam(nms, copy_grad=False):
        items = [(weights[nm], grads[nm], mom1[nm], mom2[nm]) for nm in nms]
        res, _ = _adamw(items, "adamw_" + "_".join(nms), copy_grad)
        for nm, r in zip(nms, res):
            deltas[nm], new_m[nm], new_v[nm] = r[:3]
            if copy_grad:
                grads[nm] = r[3]

    for nms in (["w_mlp_up", "w_mlp_down"], ["w_in"]):
        for nm in nms:
            grads[nm] = rs["out"][nm].reshape(weights[nm].shape)
        adam(nms, True)

    sums ={k: _sum_devices(small_all[k], small_own[k], me_arr, f"sum_small_{k[0]}_{k[1]}_{k[2]}") for k in small_own}

    loss = sums[("last", 0, "s")][SUBLANES, 0]
    small_full = {}

    def rows_of(l, part):
        if part == 0:
            return (("late", l - 1, "s"), 0) if l >= 1 else (("last", 0, "s"), 0)
        if part == 3:
            return ("late", l, "w"), 0
        base = SUBLANES if l < n_layer - 1 else 0
        return ("late", l, "s"), base + (0, 0, 2 * SUBLANES)[part]

    def summed(l, part, row, n_rows=1):
        key, base = rows_of(l, part)
        return sums[key][base + row:base + row + n_rows]

    def per_device(l, part, row):
        key, base = rows_of(l, part)
        own = small_own[key]
        if key not in small_full:
            small_full[key] = lax.dynamic_update_slice(small_all[key], own, (me_dev * own.shape[0], 0)).reshape(
                (N_DEV,) + own.shape)
        return small_full[key][:, base + row:base + row + 1]

    mod_rows = [(0, SB1_DSH), (0, SB1_DSC), (1, SB2_DGT), (2, SB3_DSH), (2, SB3_DSC), (2, SB3_DGT)]
    dmod_all = jnp.stack([jnp.concatenate([per_device(l, p, r)[:, 0, :] for p, r in mod_rows], axis=1)
                          for l in range(n_layer)], axis=0)
    o1, o2, o3, o4 = 0, SUBLANES, 3 * SUBLANES, 4 * SUBLANES
    small_sum = jnp.stack([jnp.concatenate([summed(l, 0, 0, SUBLANES), summed(l, 1, 0, 2 * SUBLANES),
                                            summed(l, 2, 0, SUBLANES), summed(l, 3, 0, 2 * bw)], axis=0)
                           for l in range(n_layer)], axis=0)
    mod_rows_of = [o1 + SB1_DSH, o1 + SB1_DSC, o2 + SB2_DGT, o3 + SB3_DSH, o3 + SB3_DSC, o3 + SB3_DGT]
    grads["w_mod"] = _mod_backward(c_all.T, lax.dynamic_slice_in_dim(dmod_all, q_me * mq, mq, axis=2), "mod_backward")
    grads["b_mod"] = jnp.concatenate([small_sum[:, k, :] for k in mod_rows_of], axis=1)
    grads["g_pre_mix"] = small_sum[:, o1 + SB1_DG_PRE]
    grads["g_post_mix"] = small_sum[:, o2 + SB2_DG_POST]
    grads["conv_a_w"] = lax.dynamic_slice_in_dim(small_sum[:, o2 + SB2_DWA:o2 + SB2_DWA + ka], q_me * dq, dq, axis=2)
    grads["conv_a_b"] = small_sum[:, o2 + SB2_DBA]
    grads["conv_b_w"] = lax.dynamic_slice_in_dim(small_sum[:, o2 + SB2_DWB:o2 + SB2_DWB + kb], q_me * dq, dq, axis=2)
    grads["conv_b_b"] = small_sum[:, o2 + SB2_DBB]
    grads["lru_lambda"] = small_sum[:, o2 + SB2_DLAM]
    grads["b_gate_r"] = small_sum[:, o2 + SB2_DBR]
    grads["b_gate_i"] = small_sum[:, o2 + SB2_DBI]
    grads["g_pre_mlp"] = small_sum[:, o3 + SB3_DG_PRE]
    grads["g_post_mlp"] = small_sum[:, o3 + SB3_DG_POST]
    dwg_sum = small_sum[:, o4:].reshape(n_layer, n_head, bw, 2 * bw)
    grads["w_gate_r"] = dwg_sum[..., :bw]
    grads["w_gate_i"] = dwg_sum[..., bw:]

    for k, nm in enumerate(groups[1]):
        grads[nm] = rs["out"]["w_abo"][:, k * dq:(k + 1) * dq]

    by_shape = {}
    for nm in names:
        if nm not in deltas:
            by_shape.setdefault(weights[nm].shape, []).append(nm)
    for nms in by_shape.values():
        adam(nms)
    return (loss, grad_x, *[grads[nm] for nm in names], *[deltas[nm] for nm in names],
            *[new_m[nm] for nm in names], *[new_v[nm] for nm in names])
```

```python
import jax
import jax.numpy as jnp
from jax import lax
from jax.experimental import pallas as pl
from jax.experimental.pallas import tpu as pltpu

F32 = jnp.float32
BF16 = jnp.bfloat16
MESH = pl.DeviceIdType.MESH

EPS = 1e-6
LRU_C = 8.0
N_CHIP = 4
N_DEV = 8
ADAM_LR = 0.001
ADAM_B1 = 0.9
ADAM_B2 = 0.999
ADAM_EPS = 1e-08
ADAM_WD = 0.01
ADAM_STEP = 10

VMEM_LIMIT_BYTES = 56 * 1024 * 1024
SUBLANES = 8
LANES = 128
TOKENS_MATMUL_TILE = 512
TOKENS_MIXER_TILE = 256
GELU_K0 = 0.7978845608028654
GELU_K1 = 0.044715

V_G_PRE_MIX, V_G_POST_MIX, V_CONV_A_B, V_CONV_B_B, V_B_GATE_R, V_B_GATE_I, V_LAMBDA, V_G_PRE_MLP, V_G_POST_MLP = range(9)
V_CONV_A_W = 9
V_CONV_B_W = 12
M_SH_M, M_SC_M, M_GT_M, M_SH_F, M_SC_F, M_GT_F = range(6)


def _cparams(n_grid=0):
    sem = ("arbitrary",) * n_grid if n_grid else None
    return pltpu.CompilerParams(dimension_semantics=sem, vmem_limit_bytes=VMEM_LIMIT_BYTES)


def _full(shape):
    return pl.BlockSpec(shape, lambda *_: (0,) * len(shape))


def _dot(a, b):
    return jnp.dot(a, b, preferred_element_type=F32)


def _dot_tb(a, b):
    return lax.dot_general(a, b, (((1,), (1,)), ((), ())), preferred_element_type=F32)


def _dot_ta(a, b):
    return lax.dot_general(a, b, (((0,), (0,)), ((), ())), preferred_element_type=F32)


def _sigmoid(x):
    return 1.0 / (1.0 + jnp.exp(-x))


def _softplus(x):
    return jnp.maximum(x, 0.0) + jnp.log1p(jnp.exp(-jnp.abs(x)))


def _neg_expm1(x):
    series = -x * (1.0 + 0.5 * x * (1.0 + (x / 3.0) * (1.0 + 0.25 * x)))
    return jnp.where(x > -1e-2, series, 1.0 - jnp.exp(x))


def _gelu_and_grad(x):
    x2 = x * x
    s = _sigmoid(x * (2.0 * GELU_K0 + (2.0 * GELU_K0 * GELU_K1) * x2))
    gel = x * s
    return gel, s + gel * (1.0 - s) * (2.0 * GELU_K0 + (6.0 * GELU_K0 * GELU_K1) * x2)


def _rms(x):
    r = lax.rsqrt(jnp.mean(x * x, axis=-1, keepdims=True) + EPS)
    return x * r, r


def _rms_bwd(dxn, xn, r):
    return r * (dxn - xn * jnp.mean(dxn * xn, axis=-1, keepdims=True))


def _colsum(x):
    return jnp.sum(x, axis=0, keepdims=True)


def _rows(t, w):
    return lax.broadcasted_iota(jnp.int32, (t, w), 0)


def _shift_down(x, k, prev8):
    t, w = x.shape
    rolled = pltpu.roll(x, k, 0)
    head = jnp.where(_rows(SUBLANES, w) < k, pltpu.roll(prev8, k, 0), rolled[:SUBLANES])
    return jnp.concatenate([head, rolled[SUBLANES:]], axis=0)


def _shift_up(x, k, next8):
    t, w = x.shape
    rolled = pltpu.roll(x, t - k, 0)
    tail = jnp.where(_rows(SUBLANES, w) >= SUBLANES - k, pltpu.roll(next8, SUBLANES - k, 0), rolled[t - SUBLANES:])
    return jnp.concatenate([rolled[:t - SUBLANES], tail], axis=0)


SCAN_GROUP = 16


def _scan_steps(a, b, group, reverse):
    t, w = a.shape
    pos = _rows(t, w) & (group - 1)
    s = 1
    while s < group:
        keep = (pos < group - s) if reverse else (pos >= s)
        shift = (t - s) if reverse else s
        b = b + a * jnp.where(keep, pltpu.roll(b, shift, 0), 0.0)
        a = a * jnp.where(keep, pltpu.roll(a, shift, 0), 1.0)
        s *= 2
    return b, a


def _scan_two_level(a, b, carry_row, a_buf, b_buf, c_buf, reverse):
    t, w = a.shape
    grp = SCAN_GROUP
    n_grp = t // grp
    h_loc, a_cum = _scan_steps(a, b, grp, reverse)
    end = 0 if reverse else grp - 1
    a_end, h_end = [], []
    for j in range(w // LANES):
        a_buf[j] = a_cum[:, j * LANES:(j + 1) * LANES]
        b_buf[j] = h_loc[:, j * LANES:(j + 1) * LANES]
        a_end.append(a_buf[j, pl.ds(end, n_grp, stride=grp), :])
        h_end.append(b_buf[j, pl.ds(end, n_grp, stride=grp), :])
    a_end = jnp.concatenate(a_end, axis=1)
    h_end = jnp.concatenate(h_end, axis=1)
    h_grp, a_grp = _scan_steps(a_end, h_end, n_grp, reverse)
    h_grp = h_grp + a_grp * carry_row
    rows = _rows(n_grp, w)
    if reverse:
        entering = jnp.where(rows == n_grp - 1, carry_row, pltpu.roll(h_grp, n_grp - 1, 0))
    else:
        entering = jnp.where(rows == 0, carry_row, pltpu.roll(h_grp, 1, 0))
    c_buf[...] = entering
    out = [h_loc[g * grp:(g + 1) * grp] + a_cum[g * grp:(g + 1) * grp] * c_buf[g:g + 1, :] for g in range(n_grp)]
    return jnp.concatenate(out, axis=0)


def _row_tile(rows, cols, itemsize=4, target_bytes=2 * 1024 * 1024):
    if rows * cols * itemsize <= target_bytes or rows % SUBLANES:
        return rows
    t = max(SUBLANES, (target_bytes // (cols * itemsize)) // SUBLANES * SUBLANES)
    while rows % t:
        t -= SUBLANES
    return t


def _place():
    return lax.axis_index("x"), lax.axis_index("y"), lax.axis_index("c")


def _other_chips(x, y):
    chips = [(1 - x, y), (x, 1 - y), (1 - x, 1 - y)]
    return chips, [2 * cx + cy for cx, cy in chips]


def _all_gather_small(block, name):
    m_per, n = block.shape

    def body(x_ref, out_ref, send_sems, recv_sems, local_sem):
        x, y, c = _place()
        me, sibling = (x, y, c), (x, y, 1 - c)
        chips, _ = _other_chips(x, y)

        def rows(px, py, pc):
            return out_ref.at[pl.ds((4 * px + 2 * py + pc) * m_per, m_per), :]

        def copy(k, blk, to, src=None):
            return pltpu.make_async_remote_copy(
                src_ref=rows(*blk) if src is None else src, dst_ref=rows(*blk),
                send_sem=send_sems.at[k], recv_sem=recv_sems.at[k], device_id=to, device_id_type=MESH)

        mine = pltpu.make_async_copy(x_ref, rows(*me), local_sem)
        mine.start()
        first = [copy(0, me, sibling, src=x_ref)]
        first += [copy(1 + j, me, (*chip, c), src=x_ref) for j, chip in enumerate(chips)]
        for cp in first:
            cp.start()
        passed = [copy(4 + j, (*chip, c), sibling) for j, chip in enumerate(chips)]
        for j, chip in enumerate(chips):
            copy(1 + j, (*chip, c), me).wait_recv()
            passed[j].start()
        copy(0, sibling, me).wait_recv()
        for j, chip in enumerate(chips):
            copy(4 + j, (*chip, 1 - c), me).wait_recv()
        for cp in first + passed:
            cp.wait_send()
        mine.wait()

    return pl.pallas_call(
        body, name=name,
        out_shape=jax.ShapeDtypeStruct((N_DEV * m_per, n), block.dtype),
        in_specs=[pl.BlockSpec(memory_space=pltpu.VMEM)],
        out_specs=pl.BlockSpec(memory_space=pltpu.VMEM),
        scratch_shapes=[pltpu.SemaphoreType.DMA((7,)), pltpu.SemaphoreType.DMA((7,)), pltpu.SemaphoreType.DMA],
        compiler_params=pltpu.CompilerParams(vmem_limit_bytes=VMEM_LIMIT_BYTES),
    )(block)


class _Carry:
    def __init__(self, ins, out_shapes, aliases, sem_shapes, start, finish, mid=None, mid_frac=0.85):
        self.ins, self.out_shapes, self.aliases, self.sem_shapes = list(ins), list(out_shapes), dict(aliases), list(sem_shapes)
        self.start, self.mid, self.finish, self.mid_frac = start, mid, finish, mid_frac


def _pcall(body, *, name, grid, in_specs, out_specs, out_shape, args, scratch_shapes=(), carries=(), prefetch=()):
    in_specs, out_specs, out_shape = list(in_specs), list(out_specs), list(out_shape)
    scratch_shapes, args = list(scratch_shapes), list(args)
    n_in, n_out, n_scr, n_pre = len(in_specs), len(out_shape), len(scratch_shapes), len(prefetch)
    steps = 1
    for g in grid:
        steps *= g
    any_spec = pl.BlockSpec(memory_space=pl.ANY)
    aliases = {}
    spans = []
    for cr in carries:
        spans.append((len(args), len(out_shape), len(scratch_shapes)))
        for a, b in cr.aliases.items():
            aliases[n_pre + len(args) + a] = len(out_shape) + b
        args += cr.ins
        in_specs += [any_spec] * len(cr.ins)
        out_shape += cr.out_shapes
        out_specs += [any_spec] * len(cr.out_shapes)
        scratch_shapes += cr.sem_shapes
    n_all_in = len(args)
    n_all_out = len(out_shape)

    def wrapped(*refs):
        pre, refs = refs[:n_pre], refs[n_pre:]
        ins, outs, scr = refs[:n_all_in], refs[n_all_in:n_all_in + n_all_out], refs[n_all_in + n_all_out:]
        parts = [(cr, ins[a:a + len(cr.ins)], outs[b:b + len(cr.out_shapes)], scr[s:s + len(cr.sem_shapes)])
                 for cr, (a, b, s) in zip(carries, spans)]
        lin = 0
        for ax, g in enumerate(grid):
            lin = lin * g + pl.program_id(ax)

        def at(step, fn):
            if steps == 1:
                fn()
            else:
                pl.when(lin == step)(fn)

        def start_all():
            for cr, ci, co, cs in parts:
                cr.start(ci, co, cs)

        def finish_all():
            for cr, ci, co, cs in parts:
                cr.finish(ci, co, cs)

        if parts:
            at(0, start_all)
        body(*pre, *ins[:n_in], *outs[:n_out], *scr[:n_scr])
        for cr, ci, co, cs in parts:
            if cr.mid is not None:
                at(min(steps - 1, int(steps * cr.mid_frac)), lambda cr=cr, ci=ci, co=co, cs=cs: cr.mid(ci, co, cs))
        if parts:
            at(steps - 1, finish_all)

    if n_pre:
        res = pl.pallas_call(
            wrapped, name=name, out_shape=out_shape,
            grid_spec=pltpu.PrefetchScalarGridSpec(num_scalar_prefetch=n_pre, grid=tuple(grid), in_specs=in_specs,
                                                   out_specs=out_specs, scratch_shapes=scratch_shapes),
            input_output_aliases=aliases, compiler_params=_cparams(len(grid)),
        )(*prefetch, *args)
    else:
        res = pl.pallas_call(
            wrapped, name=name, grid=tuple(grid), out_shape=out_shape, in_specs=in_specs, out_specs=out_specs,
            scratch_shapes=scratch_shapes, input_output_aliases=aliases, compiler_params=_cparams(len(grid)),
        )(*args)
    res = list(res)
    return res[:n_out], [res[b:b + len(cr.out_shapes)] for cr, (_, b, _) in zip(carries, spans)]


def _run_carries(carries, name):
    return _pcall(lambda: None, name=name, grid=(), in_specs=[], out_specs=[], out_shape=[], args=[], carries=carries)[1]


CAST_STEPS = 8


def _cast_place_all(shards, q_arr, name, carries=()):
    n = len(shards)

    def body(q_ref, *refs):
        for k in range(n):
            refs[n + k][...] = refs[k][...].astype(BF16)

    def spec_in(k):
        w, layer = shards[k]
        return pl.BlockSpec((1, w.shape[1] // CAST_STEPS, w.shape[2]), lambda i, q_ref: (layer, i, 0))

    def spec_out(k):
        w, _ = shards[k]
        return pl.BlockSpec((1, w.shape[1] // CAST_STEPS, w.shape[2]), lambda i, q_ref: (q_ref[0], i, 0))

    return _pcall(
        body, name=name, grid=(CAST_STEPS,),
        out_shape=[jax.ShapeDtypeStruct((N_CHIP,) + w.shape[1:], BF16) for w, _ in shards],
        in_specs=[spec_in(k) for k in range(n)], out_specs=[spec_out(k) for k in range(n)],
        args=[w for w, _ in shards], carries=carries, prefetch=[q_arr])


def _gather_carry(bufs, mid_frac=0.85):
    n = len(bufs)

    def copies(o_refs, sems):
        send_sems, recv_sems = sems
        x, y, c = _place()
        q = 2 * x + y
        sibling = (x, y, 1 - c)
        chips, qs = _other_chips(x, y)

        def half(w, shard, pc):
            rh = bufs[w].shape[1] // 2
            return o_refs[w].at[shard, pl.ds(pc * rh, rh), :]

        def over_ici(w, j, shard):
            blk = half(w, shard, c)
            return pltpu.make_async_remote_copy(
                src_ref=blk, dst_ref=blk, send_sem=send_sems.at[w, j], recv_sem=recv_sems.at[w, j],
                device_id=(*chips[j], c), device_id_type=MESH)

        def to_sibling(w, j, pc):
            blk = half(w, qs[j], pc)
            return pltpu.make_async_remote_copy(
                src_ref=blk, dst_ref=blk, send_sem=send_sems.at[w, 3 + j], recv_sem=recv_sems.at[w, 3 + j],
                device_id=sibling, device_id_type=MESH)

        return q, c, qs, over_ici, to_sibling

    pairs = [(w, j) for w in range(n) for j in range(3)]

    def start(i_refs, o_refs, sems):
        q, _, _, over_ici, _ = copies(o_refs, sems)
        for w, j in pairs:
            over_ici(w, j, q).start()

    def mid(i_refs, o_refs, sems):
        _, c, qs, over_ici, to_sibling = copies(o_refs, sems)
        for w, j in pairs:
            over_ici(w, j, qs[j]).wait_recv()
            to_sibling(w, j, c).start()

    def finish(i_refs, o_refs, sems):
        q, c, _, over_ici, to_sibling = copies(o_refs, sems)
        for w, j in pairs:
            to_sibling(w, j, 1 - c).wait_recv()
        for w, j in pairs:
            over_ici(w, j, q).wait_send()
            to_sibling(w, j, c).wait_send()

    return _Carry(bufs, [jax.ShapeDtypeStruct(b.shape, b.dtype) for b in bufs], {w: w for w in range(n)},
                  [pltpu.SemaphoreType.DMA((n, 6)), pltpu.SemaphoreType.DMA((n, 6))], start, finish, mid, mid_frac)


def _exchange_carry(grads):
    n = len(grads)

    def copies(g_refs, l_refs, sems):
        send_sems, recv_sems = sems
        x, y, c = _place()
        out = []
        for w in range(n):
            rh = grads[w].shape[1] // 2
            out.append(pltpu.make_async_remote_copy(
                src_ref=g_refs[w].at[:, pl.ds((1 - c) * rh, rh), :], dst_ref=l_refs[w],
                send_sem=send_sems.at[w], recv_sem=recv_sems.at[w], device_id=(x, y, 1 - c), device_id_type=MESH))
        return out

    def start(g_refs, l_refs, sems):
        for cp in copies(g_refs, l_refs, sems):
            cp.start()

    def finish(g_refs, l_refs, sems):
        for cp in copies(g_refs, l_refs, sems):
            cp.wait()

    return _Carry(grads, [jax.ShapeDtypeStruct((N_CHIP, g.shape[1] // 2, g.shape[2]), g.dtype) for g in grads], {},
                  [pltpu.SemaphoreType.DMA((n,)), pltpu.SemaphoreType.DMA((n,))], start, finish)


def _scatter_carry(sums):
    n = len(sums)

    def copies(s_refs, l_refs, sems):
        send_sems, recv_sems = sems
        x, y, c = _place()
        chips, _ = _other_chips(x, y)
        return [pltpu.make_async_remote_copy(
            src_ref=s_refs[w].at[j], dst_ref=l_refs[w].at[j], send_sem=send_sems.at[w, j], recv_sem=recv_sems.at[w, j],
            device_id=(*chips[j], c), device_id_type=MESH) for w in range(n) for j in range(3)]

    def start(s_refs, l_refs, sems):
        for cp in copies(s_refs, l_refs, sems):
            cp.start()

    def finish(s_refs, l_refs, sems):
        for cp in copies(s_refs, l_refs, sems):
            cp.wait()

    return _Carry(sums, [jax.ShapeDtypeStruct(s.shape, s.dtype) for s in sums], {},
                  [pltpu.SemaphoreType.DMA((n, 3)), pltpu.SemaphoreType.DMA((n, 3))], start, finish)


def _join_carry(outs, layers):
    n = len(outs)

    def copy(o_refs, sems, w, mine):
        send_sems, recv_sems = sems
        x, y, c = _place()
        r = outs[w].shape[1]
        rows = o_refs[w].at[layers[w], pl.ds((c if mine else 1 - c) * (r // 2), r // 2), :]
        return pltpu.make_async_remote_copy(
            src_ref=rows, dst_ref=rows, send_sem=send_sems.at[w], recv_sem=recv_sems.at[w],
            device_id=(x, y, 1 - c), device_id_type=MESH)

    def start(i_refs, o_refs, sems):
        for w in range(n):
            copy(o_refs, sems, w, True).start()

    def finish(i_refs, o_refs, sems):
        for w in range(n):
            copy(o_refs, sems, w, True).wait_send()
        for w in range(n):
            copy(o_refs, sems, w, False).wait_recv()

    return _Carry(outs, [jax.ShapeDtypeStruct(o.shape, o.dtype) for o in outs], {w: w for w in range(n)},
                  [pltpu.SemaphoreType.DMA((n,)), pltpu.SemaphoreType.DMA((n,))], start, finish)


PF_C, PF_Q, PF_QS = 0, 1, 2


def _add_sibling_half(g, landed, pf, name):
    _, r, cols = g.shape
    rh = r // 2
    tr = _row_tile(rh, cols)
    nr = rh // tr

    def body(pf_ref, g_ref, l_ref, o_ref):
        o_ref[...] = (g_ref[...] + l_ref[...]).astype(BF16)

    return pl.pallas_call(
        body, name=name,
        out_shape=jax.ShapeDtypeStruct((3, rh, cols), BF16),
        grid_spec=pltpu.PrefetchScalarGridSpec(
            num_scalar_prefetch=1, grid=(3, nr),
            in_specs=[pl.BlockSpec((1, tr, cols), lambda j, i, pf_ref: (pf_ref[PF_QS + j], pf_ref[PF_C] * nr + i, 0)),
                      pl.BlockSpec((1, tr, cols), lambda j, i, pf_ref: (pf_ref[PF_QS + j], i, 0))],
            out_specs=pl.BlockSpec((1, tr, cols), lambda j, i, pf_ref: (j, i, 0))),
        compiler_params=_cparams(2),
    )(pf, g, landed)


def _add_chips(g, landed, from_chips, pf, prev, layer, n_layer, name):
    _, r, cols = g.shape
    rh = r // 2
    tr = _row_tile(rh, cols)
    nr = rh // tr

    def body(pf_ref, g_ref, l_ref, f_ref, *rest):
        o_ref = rest[-1]
        acc = g_ref[0] + l_ref[0]
        for j in range(3):
            acc = acc + f_ref[j].astype(F32)
        o_ref[0] = acc

    in_specs = [pl.BlockSpec((1, tr, cols), lambda i, pf_ref: (pf_ref[PF_Q], pf_ref[PF_C] * nr + i, 0)),
                pl.BlockSpec((1, tr, cols), lambda i, pf_ref: (pf_ref[PF_Q], i, 0)),
                pl.BlockSpec((3, tr, cols), lambda i, pf_ref: (0, i, 0))]
    args = [pf, g, landed, from_chips]
    aliases = {}
    if prev is not None:
        in_specs.append(pl.BlockSpec(memory_space=pl.ANY))
        args.append(prev)
        aliases = {4: 0}
    return pl.pallas_call(
        body, name=name,
        out_shape=jax.ShapeDtypeStruct((n_layer, r, cols), F32),
        grid_spec=pltpu.PrefetchScalarGridSpec(
            num_scalar_prefetch=1, grid=(nr,), in_specs=in_specs,
            out_specs=pl.BlockSpec((1, tr, cols), lambda i, pf_ref: (layer, pf_ref[PF_C] * nr + i, 0))),
        input_output_aliases=aliases,
        compiler_params=_cparams(1),
    )(*args)


def _allgather_carry(blocks):
    n = len(blocks)

    def copies(b_refs, o_refs, sems):
        send_sems, recv_sems = sems
        x, y, c = _place()
        chips, _ = _other_chips(x, y)

        def place(w, px, py, pc):
            m = blocks[w].shape[0]
            return o_refs[w].at[pl.ds((4 * px + 2 * py + pc) * m, m), :]

        def own_to(w, k, to):
            dst = place(w, x, y, c)
            return pltpu.make_async_remote_copy(src_ref=b_refs[w], dst_ref=dst, send_sem=send_sems.at[w, k],
                                                recv_sem=recv_sems.at[w, k], device_id=to, device_id_type=MESH)

        def landed_from(w, k, px, py, pc):
            blk = place(w, px, py, pc)
            return pltpu.make_async_remote_copy(src_ref=blk, dst_ref=blk, send_sem=send_sems.at[w, k],
                                                recv_sem=recv_sems.at[w, k], device_id=(x, y, 1 - c), device_id_type=MESH)

        return x, y, c, chips, own_to, landed_from

    def start(b_refs, o_refs, sems):
        x, y, c, chips, own_to, _ = copies(b_refs, o_refs, sems)
        for w in range(n):
            own_to(w, 0, (x, y, 1 - c)).start()
            for j, chip in enumerate(chips):
                own_to(w, 1 + j, (*chip, c)).start()

    def mid(b_refs, o_refs, sems):
        x, y, c, chips, _, landed_from = copies(b_refs, o_refs, sems)
        for w in range(n):
            for j, chip in enumerate(chips):
                landed_from(w, 1 + j, *chip, c).wait_recv()
                landed_from(w, 4 + j, *chip, c).start()

    def finish(b_refs, o_refs, sems):
        x, y, c, chips, own_to, landed_from = copies(b_refs, o_refs, sems)
        for w in range(n):
            landed_from(w, 0, x, y, 1 - c).wait_recv()
            for j, chip in enumerate(chips):
                landed_from(w, 4 + j, *chip, 1 - c).wait_recv()
            own_to(w, 0, (x, y, 1 - c)).wait_send()
            for j, chip in enumerate(chips):
                own_to(w, 1 + j, (*chip, c)).wait_send()
                landed_from(w, 4 + j, *chip, c).wait_send()

    return _Carry(blocks, [jax.ShapeDtypeStruct((N_DEV * b.shape[0], b.shape[1]), b.dtype) for b in blocks], {},
                  [pltpu.SemaphoreType.DMA((n, 7)), pltpu.SemaphoreType.DMA((n, 7))], start, finish, mid)


def _sum_devices(gathered, own, me_arr, name):
    m, n = own.shape
    tr = _row_tile(m, n, itemsize=own.dtype.itemsize, target_bytes=256 * 1024)
    nr = m // tr

    def body(me_ref, *refs):
        g_refs, own_ref, o_ref = refs[:N_DEV], refs[N_DEV], refs[N_DEV + 1]
        me = me_ref[0]
        acc = None
        for dev in range(N_DEV):
            term = jnp.where(me == dev, own_ref[...], g_refs[dev][...]).astype(F32)
            acc = term if acc is None else acc + term
        o_ref[...] = acc

    def dev_rows(dev):
        return pl.BlockSpec((tr, n), lambda i, me_ref: (dev * nr + i, 0))

    return pl.pallas_call(
        body, name=name,
        out_shape=jax.ShapeDtypeStruct((m, n), F32),
        grid_spec=pltpu.PrefetchScalarGridSpec(
            num_scalar_prefetch=1, grid=(nr,),
            in_specs=[dev_rows(dev) for dev in range(N_DEV)] + [pl.BlockSpec((tr, n), lambda i, me_ref: (i, 0))],
            out_specs=pl.BlockSpec((tr, n), lambda i, me_ref: (i, 0))),
        compiler_params=_cparams(1),
    )(me_arr, *([gathered] * N_DEV), own)


def _mod_forward(c_all, w_mod, b_mod_shard, name):
    n_layer, d, mq = w_mod.shape

    def body(c_ref, w_ref, b_ref, o_ref):
        cv = c_ref[...]
        o_ref[...] = _dot(cv * _sigmoid(cv), w_ref[0]) + b_ref[0]

    return pl.pallas_call(
        body, name=name, grid=(n_layer,),
        out_shape=jax.ShapeDtypeStruct((n_layer * N_DEV, mq), F32),
        in_specs=[_full((N_DEV, d)), pl.BlockSpec((1, d, mq), lambda l: (l, 0, 0)),
                  pl.BlockSpec((1, 1, mq), lambda l: (l, 0, 0))],
        out_specs=pl.BlockSpec((N_DEV, mq), lambda l: (l, 0)),
        compiler_params=_cparams(1),
    )(c_all, w_mod, b_mod_shard.reshape(n_layer, 1, mq))


def _mod_backward(c_all_t, dmod_shard, name):
    n_layer, _, mq = dmod_shard.shape
    d = c_all_t.shape[0]

    def body(c_ref, dm_ref, o_ref):
        cv = c_ref[...]
        o_ref[0] = _dot(cv * _sigmoid(cv), dm_ref[0])

    return pl.pallas_call(
        body, name=name, grid=(n_layer,),
        out_shape=jax.ShapeDtypeStruct((n_layer, d, mq), F32),
        in_specs=[_full((d, N_DEV)), pl.BlockSpec((1, N_DEV, mq), lambda l: (l, 0, 0))],
        out_specs=pl.BlockSpec((1, d, mq), lambda l: (l, 0, 0)),
        compiler_params=_cparams(1),
    )(c_all_t, dmod_shard)


def _norm_proj(x, mod, vec, w_in, name, carries=()):
    s, d = x.shape
    nq = w_in.shape[2]
    ts = min(TOKENS_MATMUL_TILE, s)

    def body(x_ref, mod_ref, vec_ref, w_ref, h_ref, p_ref, dgel_ref):
        xn, _ = _rms(x_ref[...])
        gm = vec_ref[V_G_PRE_MIX:V_G_PRE_MIX + 1, :] * (1.0 + mod_ref[M_SC_M:M_SC_M + 1, :])
        h = (xn * gm + mod_ref[M_SH_M:M_SH_M + 1, :]).astype(BF16)
        h_ref[...] = h
        for qb in range(N_CHIP):
            pq = _dot(h, w_ref[qb])
            for k in range(N_CHIP * nq // d):
                lo, hi = max(qb * nq, k * d), min((qb + 1) * nq, (k + 1) * d)
                if lo >= hi:
                    continue
                piece = pq[:, lo - qb * nq:hi - qb * nq]
                if k == 4:
                    piece, dgel = _gelu_and_grad(piece)
                    dgel_ref[:, lo - 4 * d:hi - 4 * d] = dgel.astype(BF16)
                elif k >= 5:
                    piece = _sigmoid(piece)
                p_ref[:, lo:hi] = piece.astype(BF16)

    tile = pl.BlockSpec((ts, d), lambda i: (i, 0))
    return _pcall(
        body, name=name, grid=(s // ts,),
        out_shape=[jax.ShapeDtypeStruct((s, d), BF16), jax.ShapeDtypeStruct((s, N_CHIP * nq), BF16),
                   jax.ShapeDtypeStruct((s, d), BF16)],
        in_specs=[tile, _full(mod.shape), _full(vec.shape), _full(w_in.shape)],
        out_specs=[tile, pl.BlockSpec((ts, N_CHIP * nq), lambda i: (i, 0)), tile],
        args=[x, mod, vec, w_in], carries=carries)


def _gate_pre(xb2_b, wg_ref, n_head, bw):
    zr, zi = [], []
    for hd in range(n_head):
        z = _dot(xb2_b[:, hd * bw:(hd + 1) * bw], wg_ref[hd])
        zr.append(z[:, :bw])
        zi.append(z[:, bw:])
    return jnp.concatenate(zr, axis=1), jnp.concatenate(zi, axis=1)


def _lru_coeffs(xb2, wg_ref, vec_ref, n_head, bw):
    zr, zi = _gate_pre(xb2.astype(BF16), wg_ref, n_head, bw)
    r = _sigmoid(zr + vec_ref[V_B_GATE_R:V_B_GATE_R + 1, :])
    gi = _sigmoid(zi + vec_ref[V_B_GATE_I:V_B_GATE_I + 1, :])
    sp = _softplus(-vec_ref[V_LAMBDA:V_LAMBDA + 1, :])
    log_a = (-LRU_C) * r * sp
    a = jnp.exp(log_a)
    mult = jnp.sqrt(_neg_expm1(2.0 * log_a))
    return r, gi, sp, a, mult


def _mixer_forward(x, proj, mod, vec, wg, w_a_out, w_b_out, w_o, name, carries=()):
    s, d = x.shape
    n_head, bw, _ = wg.shape
    ts = min(TOKENS_MIXER_TILE, s)

    def body(x_ref, p_ref, mod_ref, vec_ref, wg_ref, wa_ref, wb_ref, wo_ref,
             x1_ref, conva_ref, xb2_ref, hh_ref, abm_ref, pa_ref, pb_ref, y_ref,
             r_ref, gi_ref, a_ref, mult_ref,
             cv_tail, xb_tail, h_last, a_buf, b_buf, c_buf):
        i = pl.program_id(0)

        @pl.when(i == 0)
        def _():
            cv_tail[...] = jnp.zeros_like(cv_tail)
            xb_tail[...] = jnp.zeros_like(xb_tail)
            h_last[...] = jnp.zeros_like(h_last)

        def seg(k):
            return p_ref[:, k * d:(k + 1) * d].astype(F32)

        def vrow(k):
            return vec_ref[k:k + 1, :]

        b_a, c_a, v_a, x_b, gel, sa, sb = (seg(k) for k in range(7))
        cv = c_a * v_a
        prev_cv = cv_tail[...]
        conv_a = (vrow(V_CONV_A_B) + vrow(V_CONV_A_W) * _shift_down(cv, 2, prev_cv)
                  + vrow(V_CONV_A_W + 1) * _shift_down(cv, 1, prev_cv) + vrow(V_CONV_A_W + 2) * cv)
        cv_tail[...] = cv[ts - SUBLANES:]
        y_a = b_a * conv_a
        prev_xb = xb_tail[...]
        xb2 = (vrow(V_CONV_B_B) + vrow(V_CONV_B_W) * _shift_down(x_b, 3, prev_xb)
               + vrow(V_CONV_B_W + 1) * _shift_down(x_b, 2, prev_xb)
               + vrow(V_CONV_B_W + 2) * _shift_down(x_b, 1, prev_xb) + vrow(V_CONV_B_W + 3) * x_b)
        xb_tail[...] = x_b[ts - SUBLANES:]
        r, gi, _, a, mult = _lru_coeffs(xb2, wg_ref, vec_ref, n_head, bw)
        r_ref[...] = r
        gi_ref[...] = gi
        a_ref[...] = a
        mult_ref[...] = mult
        hh = _scan_two_level(a, mult * gi * xb2, h_last[SUBLANES - 1:SUBLANES, :], a_buf, b_buf, c_buf, reverse=False)
        h_last[...] = hh[ts - SUBLANES:]
        y_b = hh * gel
        ya_b, yb_b = y_a.astype(BF16), y_b.astype(BF16)
        pa = _dot(ya_b, wa_ref[...])
        pb = _dot(yb_b, wb_ref[...])
        m = (sa * pa + sb * pb).astype(BF16)
        y = _dot(m, wo_ref[...])
        yn, _ = _rms(y)
        gg = mod_ref[M_GT_M:M_GT_M + 1, :] * vrow(V_G_POST_MIX)
        x1_ref[...] = x_ref[...] + yn * gg
        conva_ref[...] = conv_a.astype(BF16)
        xb2_ref[...] = xb2
        hh_ref[...] = hh
        abm_ref[0] = ya_b
        abm_ref[1] = yb_b
        abm_ref[2] = m
        pa_ref[...] = pa.astype(BF16)
        pb_ref[...] = pb.astype(BF16)
        y_ref[...] = y.astype(BF16)

    tile = pl.BlockSpec((ts, d), lambda i: (i, 0))
    tile3 = pl.BlockSpec((3, ts, d), lambda i: (0, i, 0))
    sd = lambda dt: jax.ShapeDtypeStruct((s, d), dt)
    return _pcall(
        body, name=name, grid=(s // ts,),
        out_shape=[sd(F32), sd(BF16), sd(F32), sd(F32), jax.ShapeDtypeStruct((3, s, d), BF16), sd(BF16), sd(BF16), sd(BF16),
                   sd(F32), sd(F32), sd(F32), sd(F32)],
        in_specs=[tile, pl.BlockSpec((ts, 7 * d), lambda i: (i, 0)), _full(mod.shape), _full(vec.shape),
                  _full(wg.shape), _full(w_a_out.shape), _full(w_b_out.shape), _full(w_o.shape)],
        out_specs=[tile] * 4 + [tile3] + [tile] * 7,
        scratch_shapes=[pltpu.VMEM((SUBLANES, d), F32)] * 3 + [pltpu.VMEM((d // LANES, ts, LANES), F32)] * 2
                       + [pltpu.VMEM((ts // SCAN_GROUP, d), F32)],
        args=[x, proj, mod, vec, wg, w_a_out, w_b_out, w_o], carries=carries)


def _mlp_forward(x1, mod, vec, w_up, w_down, name, carries=(), target=None):
    s, d = x1.shape
    fq = w_up.shape[2]
    ts = min(TOKENS_MATMUL_TILE, s)

    def body(x_ref, *refs):
        if target is None:
            mod_ref, vec_ref, wu_ref, wd_ref, x2_ref, h2_ref, up_ref, y2_ref = refs
        else:
            t_ref, mod_ref, vec_ref, wu_ref, wd_ref, x2_ref, h2_ref, up_ref, y2_ref, loss_ref = refs
        x = x_ref[...]
        xn, _ = _rms(x)
        gm = vec_ref[V_G_PRE_MLP:V_G_PRE_MLP + 1, :] * (1.0 + mod_ref[M_SC_F:M_SC_F + 1, :])
        h2 = (xn * gm + mod_ref[M_SH_F:M_SH_F + 1, :]).astype(BF16)
        h2_ref[...] = h2
        y2 = jnp.zeros((ts, d), F32)
        for qb in range(N_CHIP):
            up = _dot(h2, wu_ref[qb])
            up_ref[:, qb * fq:(qb + 1) * fq] = up.astype(BF16)
            ru = jnp.maximum(up, 0.0)
            y2 = y2 + _dot((ru * ru).astype(BF16), wd_ref[qb])
        y2_ref[...] = y2.astype(BF16)
        yn, _ = _rms(y2)
        gg = mod_ref[M_GT_F:M_GT_F + 1, :] * vec_ref[V_G_POST_MLP:V_G_POST_MLP + 1, :]
        x2 = x + yn * gg
        if target is None:
            x2_ref[...] = x2
        else:
            @pl.when(pl.program_id(0) == 0)
            def _():
                loss_ref[...] = jnp.zeros_like(loss_ref)

            err = x2 - t_ref[...]
            x2_ref[...] = err * (1.0 / d)
            loss_ref[...] += jnp.sum(jnp.sum(err * err, axis=1, keepdims=True), axis=0, keepdims=True) * (0.5 / d)

    tile = pl.BlockSpec((ts, d), lambda i: (i, 0))
    last = target is not None
    return _pcall(
        body, name=name, grid=(s // ts,),
        out_shape=[jax.ShapeDtypeStruct((s, d), F32), jax.ShapeDtypeStruct((s, d), BF16),
                   jax.ShapeDtypeStruct((s, N_CHIP * fq), BF16), jax.ShapeDtypeStruct((s, d), BF16)]
                  + ([jax.ShapeDtypeStruct((SUBLANES, LANES), F32)] if last else []),
        in_specs=[tile] + ([tile] if last else []) + [_full(mod.shape), _full(vec.shape), _full(w_up.shape), _full(w_down.shape)],
        out_specs=[tile, tile, pl.BlockSpec((ts, N_CHIP * fq), lambda i: (i, 0)), tile]
                 + ([_full((SUBLANES, LANES))] if last else []),
        args=[x1] + ([target] if last else []) + [mod, vec, w_up, w_down], carries=carries)


SB3_DSH, SB3_DSC, SB3_DGT, SB3_DG_PRE, SB3_DG_POST = range(5)
SB1_DSH, SB1_DSC, SB1_DG_PRE = range(3)
(SB2_DGT, SB2_DG_POST, SB2_DWA, SB2_DBA, SB2_DWB, SB2_DBB, SB2_DLAM, SB2_DBR, SB2_DBI) = (0, 1, 2, 5, 6, 10, 11, 12, 13)


def _mlp_backward(dx2, x1, y2, up, mod, vec, w_up, w_down, name, carries=()):
    s, d = dx2.shape
    fq = w_up.shape[2]
    ts = min(TOKENS_MATMUL_TILE, s)
    n_t = s // ts

    def body(dx2_ref, x_ref, y2_ref, up_ref, mod_ref, vec_ref, wu_ref, wd_ref,
             dx1_ref, dy2_ref, dup_ref, small_ref):
        i = pl.program_id(0)

        @pl.when(i == 0)
        def _():
            small_ref[...] = jnp.zeros_like(small_ref)

        dout = dx2_ref[...]
        y2n, ry = _rms(y2_ref[...].astype(F32))
        g_post = vec_ref[V_G_POST_MLP:V_G_POST_MLP + 1, :]
        gt = mod_ref[M_GT_F:M_GT_F + 1, :]
        dgg = _colsum(dout * y2n)
        dy2 = _rms_bwd(dout * (gt * g_post), y2n, ry).astype(BF16)
        dy2_ref[...] = dy2
        dh2 = jnp.zeros((ts, d), F32)
        for qb in range(N_CHIP):
            cols = slice(qb * fq, (qb + 1) * fq)
            dact = _dot_tb(dy2, wd_ref[qb])
            ru = jnp.maximum(up_ref[:, cols].astype(F32), 0.0)
            dup = (dact * (2.0 * ru)).astype(BF16)
            dup_ref[:, cols] = dup
            dh2 = dh2 + _dot_tb(dup, wu_ref[qb])
        xn, r = _rms(x_ref[...])
        g_pre = vec_ref[V_G_PRE_MLP:V_G_PRE_MLP + 1, :]
        sc1 = 1.0 + mod_ref[M_SC_F:M_SC_F + 1, :]
        dsh = _colsum(dh2)
        dgm = _colsum(dh2 * xn)
        dx1_ref[...] = dout + _rms_bwd(dh2 * (g_pre * sc1), xn, r)
        small_ref[SB3_DSH:SB3_DSH + 1, :] += dsh
        small_ref[SB3_DSC:SB3_DSC + 1, :] += dgm
        small_ref[SB3_DGT:SB3_DGT + 1, :] += dgg

        @pl.when(i == n_t - 1)
        def _():
            dgm_t = small_ref[SB3_DSC:SB3_DSC + 1, :]
            dgg_t = small_ref[SB3_DGT:SB3_DGT + 1, :]
            small_ref[SB3_DSC:SB3_DSC + 1, :] = dgm_t * g_pre
            small_ref[SB3_DG_PRE:SB3_DG_PRE + 1, :] = dgm_t * sc1
            small_ref[SB3_DGT:SB3_DGT + 1, :] = dgg_t * g_post
            small_ref[SB3_DG_POST:SB3_DG_POST + 1, :] = dgg_t * gt

    tile = pl.BlockSpec((ts, d), lambda i: (i, 0))
    wide = pl.BlockSpec((ts, N_CHIP * fq), lambda i: (i, 0))
    return _pcall(
        body, name=name, grid=(n_t,),
        out_shape=[jax.ShapeDtypeStruct((s, d), F32), jax.ShapeDtypeStruct((s, d), BF16),
                   jax.ShapeDtypeStruct((s, N_CHIP * fq), BF16), jax.ShapeDtypeStruct((SUBLANES, d), F32)],
        in_specs=[tile, tile, tile, wide, _full(mod.shape), _full(vec.shape), _full(w_up.shape), _full(w_down.shape)],
        out_specs=[tile, tile, wide, _full((SUBLANES, d))],
        args=[dx2, x1, y2, up, mod, vec, w_up, w_down], carries=carries)


def _mixer_backward(dx1, proj, conva, xb2s, hhs, pas, pbs, ys, rs_, gis, as_, mults, dgels, mod, vec, wg, w_a_out, w_b_out,
                    w_o, name, carries=()):
    s, d = dx1.shape
    n_head, bw, _ = wg.shape
    ts = min(TOKENS_MIXER_TILE, s)
    n_t = s // ts

    def body(dx1_ref, p_ref, conva_ref, xb2_ref, hh_ref, pa_ref, pb_ref, y_ref, r_ref, gi_ref, a_ref, mult_ref, dgel_ref,
             mod_ref, vec_ref, wg_ref, wa_ref, wb_ref, wo_ref,
             dp_ref, dab_ref, small_ref, dwg_ref,
             dconv_head, dxb2_head, a_head, g_head, a_buf, b_buf, c_buf):
        i = pl.program_id(0)

        @pl.when(i == 0)
        def _():
            small_ref[...] = jnp.zeros_like(small_ref)
            dwg_ref[...] = jnp.zeros_like(dwg_ref)
            dconv_head[...] = jnp.zeros_like(dconv_head)
            dxb2_head[...] = jnp.zeros_like(dxb2_head)
            a_head[...] = jnp.zeros_like(a_head)
            g_head[...] = jnp.zeros_like(g_head)

        def seg(k):
            return p_ref[:, k * d:(k + 1) * d].astype(F32)

        def vrow(k):
            return vec_ref[k:k + 1, :]

        def acc(row, val):
            small_ref[row:row + 1, :] += val

        dout = dx1_ref[...]
        yn, ry = _rms(y_ref[...].astype(F32))
        g_post = vrow(V_G_POST_MIX)
        gt = mod_ref[M_GT_M:M_GT_M + 1, :]
        acc(SB2_DGT, _colsum(dout * yn))
        dy = _rms_bwd(dout * (gt * g_post), yn, ry).astype(BF16)
        dab_ref[2] = dy
        dm = _dot_tb(dy, wo_ref[...])
        sa, sb = seg(5), seg(6)
        dpa = (dm * sa).astype(BF16)
        dpb = (dm * sb).astype(BF16)
        dab_ref[0] = dpa
        dab_ref[1] = dpb
        du_a = dm * pa_ref[...].astype(F32) * (sa * (1.0 - sa))
        du_b = dm * pb_ref[...].astype(F32) * (sb * (1.0 - sb))
        dp_ref[:, 5 * d:6 * d] = du_a.astype(BF16)
        dp_ref[:, 6 * d:7 * d] = du_b.astype(BF16)
        dy_a = _dot_tb(dpa, wa_ref[...])
        dy_b = _dot_tb(dpb, wb_ref[...])

        b_a, c_a, v_a = seg(0), seg(1), seg(2)
        dp_ref[:, 0:d] = (dy_a * conva_ref[...].astype(F32)).astype(BF16)
        dconv = dy_a * b_a
        nxt = dconv_head[...]
        d1 = _shift_up(dconv, 1, nxt)
        d2 = _shift_up(dconv, 2, nxt)
        dconv_head[...] = dconv[:SUBLANES]
        dcv = vrow(V_CONV_A_W + 2) * dconv + vrow(V_CONV_A_W + 1) * d1 + vrow(V_CONV_A_W) * d2
        cv = c_a * v_a
        acc(SB2_DWA + 2, _colsum(cv * dconv))
        acc(SB2_DWA + 1, _colsum(cv * d1))
        acc(SB2_DWA, _colsum(cv * d2))
        acc(SB2_DBA, _colsum(dconv))
        dp_ref[:, d:2 * d] = (dcv * v_a).astype(BF16)
        dp_ref[:, 2 * d:3 * d] = (dcv * c_a).astype(BF16)

        x_b, gel = seg(3), seg(4)
        hh = hh_ref[...]
        dp_ref[:, 4 * d:5 * d] = (dy_b * hh * dgel_ref[...].astype(F32)).astype(BF16)
        dhh = dy_b * gel
        xb2 = xb2_ref[...]
        r, gi, a, mult = r_ref[...], gi_ref[...], a_ref[...], mult_ref[...]
        sp = _softplus(-vrow(V_LAMBDA))
        a_next = _shift_up(a, 1, a_head[...])
        g = _scan_two_level(a_next, dhh, g_head[0:1, :], a_buf, b_buf, c_buf, reverse=True)
        a_head[...] = a[:SUBLANES]
        g_head[...] = g[:SUBLANES]
        gix = gi * xb2
        gm = g * mult
        dlog_a = g * (hh - mult * gix) - (g * gix) * (a * a / mult)
        dgi = gm * xb2
        dxb2 = gm * gi
        acc(SB2_DLAM, _colsum(dlog_a * r))
        dzr = dlog_a * ((-LRU_C) * sp) * (r * (1.0 - r))
        dzi = dgi * (gi * (1.0 - gi))
        acc(SB2_DBR, _colsum(dzr))
        acc(SB2_DBI, _colsum(dzi))
        xb2_b = xb2.astype(BF16)
        back = []
        for hd in range(n_head):
            cols = slice(hd * bw, (hd + 1) * bw)
            dz = jnp.concatenate([dzr[:, cols], dzi[:, cols]], axis=1).astype(BF16)
            back.append(_dot_tb(dz, wg_ref[hd]))
            dwg_ref[hd] += _dot_ta(xb2_b[:, cols], dz)
        dxb2 = dxb2 + jnp.concatenate(back, axis=1)
        nxt = dxb2_head[...]
        e1 = _shift_up(dxb2, 1, nxt)
        e2 = _shift_up(dxb2, 2, nxt)
        e3 = _shift_up(dxb2, 3, nxt)
        dxb2_head[...] = dxb2[:SUBLANES]
        dp_ref[:, 3 * d:4 * d] = (vrow(V_CONV_B_W + 3) * dxb2 + vrow(V_CONV_B_W + 2) * e1
                                  + vrow(V_CONV_B_W + 1) * e2 + vrow(V_CONV_B_W) * e3).astype(BF16)
        acc(SB2_DWB + 3, _colsum(x_b * dxb2))
        acc(SB2_DWB + 2, _colsum(x_b * e1))
        acc(SB2_DWB + 1, _colsum(x_b * e2))
        acc(SB2_DWB, _colsum(x_b * e3))
        acc(SB2_DBB, _colsum(dxb2))

        @pl.when(i == n_t - 1)
        def _():
            dgg_t = small_ref[SB2_DGT:SB2_DGT + 1, :]
            small_ref[SB2_DGT:SB2_DGT + 1, :] = dgg_t * g_post
            small_ref[SB2_DG_POST:SB2_DG_POST + 1, :] = dgg_t * gt
            lam = vrow(V_LAMBDA)
            small_ref[SB2_DLAM:SB2_DLAM + 1, :] = small_ref[SB2_DLAM:SB2_DLAM + 1, :] * (LRU_C * _sigmoid(-lam))

    rev = lambda i: (n_t - 1 - i, 0)
    tile = pl.BlockSpec((ts, d), rev)
    wide = pl.BlockSpec((ts, 7 * d), rev)
    sd = lambda dt: jax.ShapeDtypeStruct((s, d), dt)
    return _pcall(
        body, name=name, grid=(n_t,),
        out_shape=[jax.ShapeDtypeStruct((s, 7 * d), BF16), jax.ShapeDtypeStruct((3, s, d), BF16),
                   jax.ShapeDtypeStruct((2 * SUBLANES, d), F32), jax.ShapeDtypeStruct(wg.shape, F32)],
        in_specs=[tile, wide] + [tile] * 11 + [_full(mod.shape), _full(vec.shape),
                  _full(wg.shape), _full(w_a_out.shape), _full(w_b_out.shape), _full(w_o.shape)],
        out_specs=[wide, pl.BlockSpec((3, ts, d), lambda i: (0, n_t - 1 - i, 0)), _full((2 * SUBLANES, d)), _full(wg.shape)],
        scratch_shapes=[pltpu.VMEM((SUBLANES, d), F32)] * 4 + [pltpu.VMEM((d // LANES, ts, LANES), F32)] * 2
                       + [pltpu.VMEM((ts // SCAN_GROUP, d), F32)],
        args=[dx1, proj, conva, xb2s, hhs, pas, pbs, ys, rs_, gis, as_, mults, dgels, mod, vec, wg, w_a_out, w_b_out, w_o],
        carries=carries)


def _proj_backward(dproj, dx1, x, mod, vec, w_in, name, carries=()):
    s, d = x.shape
    nq = w_in.shape[2]
    ts = min(TOKENS_MATMUL_TILE, s)
    n_t = s // ts

    def body(dp_ref, dx1_ref, x_ref, mod_ref, vec_ref, w_ref, dx_ref, small_ref):
        i = pl.program_id(0)

        @pl.when(i == 0)
        def _():
            small_ref[...] = jnp.zeros_like(small_ref)

        dh = jnp.zeros((ts, d), F32)
        for qb in range(N_CHIP):
            dh = dh + _dot_tb(dp_ref[:, qb * nq:(qb + 1) * nq], w_ref[qb])
        xn, r = _rms(x_ref[...])
        g_pre = vec_ref[V_G_PRE_MIX:V_G_PRE_MIX + 1, :]
        sc1 = 1.0 + mod_ref[M_SC_M:M_SC_M + 1, :]
        dx_ref[...] = dx1_ref[...] + _rms_bwd(dh * (g_pre * sc1), xn, r)
        small_ref[SB1_DSH:SB1_DSH + 1, :] += _colsum(dh)
        small_ref[SB1_DSC:SB1_DSC + 1, :] += _colsum(dh * xn)

        @pl.when(i == n_t - 1)
        def _():
            dgm_t = small_ref[SB1_DSC:SB1_DSC + 1, :]
            small_ref[SB1_DSC:SB1_DSC + 1, :] = dgm_t * g_pre
            small_ref[SB1_DG_PRE:SB1_DG_PRE + 1, :] = dgm_t * sc1

    tile = pl.BlockSpec((ts, d), lambda i: (i, 0))
    return _pcall(
        body, name=name, grid=(n_t,),
        out_shape=[jax.ShapeDtypeStruct((s, d), F32), jax.ShapeDtypeStruct((SUBLANES, d), F32)],
        in_specs=[pl.BlockSpec((ts, N_CHIP * nq), lambda i: (i, 0)), tile, tile, _full(mod.shape), _full(vec.shape),
                  _full(w_in.shape)],
        out_specs=[tile, _full((SUBLANES, d))],
        args=[dproj, dx1, x, mod, vec, w_in], carries=carries)


def _weight_grad(a, b, name, col_blocks=1, tk=512, carries=(), square_relu=False):
    s, k = a.shape
    n = b.shape[1]
    tn = n // col_blocks
    tk = min(tk, k)

    def body(a_ref, b_ref, o_ref):
        av = a_ref[...]
        if square_relu:
            ru = jnp.maximum(av.astype(F32), 0.0)
            av = (ru * ru).astype(BF16)
        o_ref[0] = _dot_ta(av, b_ref[...])

    (out,), carried = _pcall(
        body, name=name, grid=(col_blocks, k // tk),
        out_shape=[jax.ShapeDtypeStruct((col_blocks, k, tn), F32)],
        in_specs=[pl.BlockSpec((s, tk), lambda j, i: (0, i)), pl.BlockSpec((s, tn), lambda j, i: (0, j))],
        out_specs=[pl.BlockSpec((1, tk, tn), lambda j, i: (j, i, 0))],
        args=[a, b], carries=carries)
    return out, carried


def _weight_grad_stacked(a3, b3, name, tk=512, carries=()):
    n_g, s, k = a3.shape
    n = b3.shape[2]
    kq = k // N_CHIP
    tk = min(tk, k)
    chips_per_tile = tk // kq

    def body(a_ref, b_ref, o_ref):
        o_ref[...] = _dot_ta(a_ref[...], b_ref[...]).reshape(chips_per_tile, kq, n)

    (out,), carried = _pcall(
        body, name=name, grid=(n_g, k // tk),
        out_shape=[jax.ShapeDtypeStruct((N_CHIP, n_g, kq, n), F32)],
        in_specs=[pl.BlockSpec((None, s, tk), lambda g, i: (g, 0, i)), pl.BlockSpec((None, s, n), lambda g, i: (g, 0, 0))],
        out_specs=[pl.BlockSpec((chips_per_tile, None, kq, n), lambda g, i: (i, g, 0, 0))],
        args=[a3, b3], carries=carries)
    return out.reshape(N_CHIP, n_g * kq, n), carried


def _adamw(items, name, copy_grad=False, carries=()):
    shape = items[0][0].shape
    cols = shape[-1]
    rows = items[0][0].size // cols
    tr = _row_tile(rows, cols, target_bytes=2 * 1024 * 1024 // len(items))
    c1 = 1.0 - ADAM_B1 ** ADAM_STEP
    c2 = 1.0 - ADAM_B2 ** ADAM_STEP
    n_out = 4 if copy_grad else 3
    n = len(items)

    def body(*refs):
        for k in range(n):
            w_ref, g_ref, m_ref, v_ref = refs[4 * k:4 * k + 4]
            outs = refs[4 * n + n_out * k:4 * n + n_out * (k + 1)]
            gv = g_ref[...]
            nm = ADAM_B1 * m_ref[...] + (1.0 - ADAM_B1) * gv
            nv = ADAM_B2 * v_ref[...] + (1.0 - ADAM_B2) * (gv * gv)
            outs[0][...] = (-ADAM_LR) * ((nm / c1) / (jnp.sqrt(nv / c2) + ADAM_EPS) + ADAM_WD * w_ref[...])
            outs[1][...] = nm
            outs[2][...] = nv
            if copy_grad:
                outs[3][...] = gv

    spec = pl.BlockSpec((tr, cols), lambda i: (i, 0))
    outs, carried = _pcall(
        body, name=name, grid=(rows // tr,),
        out_shape=[jax.ShapeDtypeStruct((rows, cols), F32)] * (n_out * n),
        in_specs=[spec] * (4 * n), out_specs=[spec] * (n_out * n),
        args=[t.reshape(rows, cols) for item in items for t in item], carries=carries)
    return [tuple(o.reshape(shape) for o in outs[n_out * k:n_out * (k + 1)]) for k in range(n)], carried


def kernel(x, c, w_mod, b_mod, g_pre_mix, g_post_mix, w_in, conv_a_w, conv_a_b, w_a_out, conv_b_w, conv_b_b, w_gate_r, b_gate_r, w_gate_i, b_gate_i, lru_lambda, w_b_out, w_o, g_pre_mlp, g_post_mlp, w_mlp_up, w_mlp_down, loss_target, m_w_mod, m_b_mod, m_g_pre_mix, m_g_post_mix, m_w_in, m_conv_a_w, m_conv_a_b, m_w_a_out, m_conv_b_w, m_conv_b_b, m_w_gate_r, m_b_gate_r, m_w_gate_i, m_b_gate_i, m_lru_lambda, m_w_b_out, m_w_o, m_g_pre_mlp, m_g_post_mlp, m_w_mlp_up, m_w_mlp_down, v_w_mod, v_b_mod, v_g_pre_mix, v_g_post_mix, v_w_in, v_conv_a_w, v_conv_a_b, v_w_a_out, v_conv_b_w, v_conv_b_b, v_w_gate_r, v_b_gate_r, v_w_gate_i, v_b_gate_i, v_lru_lambda, v_w_b_out, v_w_o, v_g_pre_mlp, v_g_post_mlp, v_w_mlp_up, v_w_mlp_down):
    weights = dict(w_mod=w_mod, b_mod=b_mod, g_pre_mix=g_pre_mix, g_post_mix=g_post_mix, w_in=w_in, conv_a_w=conv_a_w,
                   conv_a_b=conv_a_b, w_a_out=w_a_out, conv_b_w=conv_b_w, conv_b_b=conv_b_b, w_gate_r=w_gate_r,
                   b_gate_r=b_gate_r, w_gate_i=w_gate_i, b_gate_i=b_gate_i, lru_lambda=lru_lambda, w_b_out=w_b_out,
                   w_o=w_o, g_pre_mlp=g_pre_mlp, g_post_mlp=g_post_mlp, w_mlp_up=w_mlp_up, w_mlp_down=w_mlp_down)
    mom1 = dict(w_mod=m_w_mod, b_mod=m_b_mod, g_pre_mix=m_g_pre_mix, g_post_mix=m_g_post_mix, w_in=m_w_in,
                conv_a_w=m_conv_a_w, conv_a_b=m_conv_a_b, w_a_out=m_w_a_out, conv_b_w=m_conv_b_w, conv_b_b=m_conv_b_b,
                w_gate_r=m_w_gate_r, b_gate_r=m_b_gate_r, w_gate_i=m_w_gate_i, b_gate_i=m_b_gate_i,
                lru_lambda=m_lru_lambda, w_b_out=m_w_b_out, w_o=m_w_o, g_pre_mlp=m_g_pre_mlp, g_post_mlp=m_g_post_mlp,
                w_mlp_up=m_w_mlp_up, w_mlp_down=m_w_mlp_down)
    mom2 = dict(w_mod=v_w_mod, b_mod=v_b_mod, g_pre_mix=v_g_pre_mix, g_post_mix=v_g_post_mix, w_in=v_w_in,
                conv_a_w=v_conv_a_w, conv_a_b=v_conv_a_b, w_a_out=v_w_a_out, conv_b_w=v_conv_b_w, conv_b_b=v_conv_b_b,
                w_gate_r=v_w_gate_r, b_gate_r=v_b_gate_r, w_gate_i=v_w_gate_i, b_gate_i=v_b_gate_i,
                lru_lambda=v_lru_lambda, w_b_out=v_w_b_out, w_o=v_w_o, g_pre_mlp=v_g_pre_mlp, g_post_mlp=v_g_post_mlp,
                w_mlp_up=v_w_mlp_up, w_mlp_down=v_w_mlp_down)
    names = list(weights)

    n_layer = w_in.shape[0]
    s, d = x.shape[1], x.shape[2]
    n_head, bw = w_gate_r.shape[1], w_gate_r.shape[2]
    dq = d // N_CHIP
    mq = w_mod.shape[2]
    n_mod = (N_CHIP * mq) // d
    ka, kb = conv_a_w.shape[1], conv_b_w.shape[1]

    mx, my, mc = _place()
    q_me = 2 * mx + my
    q_arr = jnp.reshape(q_me, (1,)).astype(jnp.int32)

    me_dev = 4 * mx + 2 * my + mc
    me_arr = jnp.reshape(me_dev, (1,)).astype(jnp.int32)

    big_names = ["w_in", "w_a_out", "w_b_out", "w_o", "w_mlp_up", "w_mlp_down"]
    groups = [["w_in"], ["w_a_out", "w_b_out", "w_o"], ["w_mlp_up", "w_mlp_down"]]
    placed = {("w_in", 0): _cast_place_all([(w_in, 0)], q_arr, "cast_place_first")[0][0]}
    wfull = [dict() for _ in range(n_layer)]
    riders = {}
    for l in range(n_layer):
        riders.setdefault(3 * l - 1, []).append(([("w_in", l)], 0.9 if l else 1.0))
        riders.setdefault(3 * l - 2 if l else 0, []).append(([(nm, l) for nm in groups[1]], 0.9 if l else 0.5))
        riders.setdefault(3 * l, []).append(([("w_mlp_up", l)], 0.7 if l else 0.9))
        riders.setdefault(3 * l + 1, []).insert(0, ([("w_mlp_down", l)], 0.5))

    def gather_carry(call):
        return [_gather_carry([placed[k] for k in keys], frac) for keys, frac in riders.get(call, [])]

    def gathered(call, carried):
        for (keys, _), ws in zip(riders.get(call, []), carried):
            for (nm, l), w in zip(keys, ws):
                wfull[l][nm] = w.reshape(d, d) if nm in groups[1] else w

    n_conv_rows = n_layer * (ka + kb)
    conv_blk = -(-n_conv_rows // SUBLANES) * SUBLANES
    blk_rows = SUBLANES + conv_blk
    conv_rows = jnp.concatenate([jnp.concatenate([conv_a_w[l], conv_b_w[l]], axis=0) for l in range(n_layer)], axis=0)
    conv_rows = jnp.pad(conv_rows, ((0, conv_blk - n_conv_rows), (0, d - dq)))
    c_conv = jnp.concatenate([jnp.pad(c, ((0, SUBLANES - 1), (0, 0))), conv_rows], axis=0)
    rest = [(nm, l) for l in range(n_layer) for nm in big_names if (nm, l) != ("w_in", 0)]
    rest_placed, carried = _cast_place_all([(weights[nm], l) for nm, l in rest], q_arr, "cast_place_rest",
                                           carries=gather_carry(-1) + [_allgather_carry([c_conv])])
    placed.update(zip(rest, rest_placed))
    gathered(-1, carried[:1])
    gathered1 = lax.dynamic_update_slice(carried[1][0], c_conv, (me_dev * blk_rows, 0)).reshape(N_DEV, blk_rows, d)
    c_all = gathered1[:, 0, :]
    conv_full = jnp.concatenate([gathered1[2 * qb, SUBLANES:SUBLANES + n_conv_rows, :dq] for qb in range(N_CHIP)], axis=1)

    b_mod_shard = lax.dynamic_slice_in_dim(b_mod, q_me * mq, mq, axis=1)
    mod_part = _mod_forward(c_all, w_mod, b_mod_shard, "mod_forward")
    gathered2 = _all_gather_small(mod_part, "gather_mod").reshape(N_DEV, n_layer, N_DEV, mq)
    mod_rows = jnp.concatenate(
        [lax.dynamic_index_in_dim(gathered2[2 * qb], me_dev, axis=1, keepdims=False) for qb in range(N_CHIP)], axis=1)
    mods = [jnp.pad(mod_rows[l].reshape(n_mod, d), ((0, SUBLANES - n_mod), (0, 0))) for l in range(n_layer)]

    vecs = []
    for l in range(n_layer):
        base = l * (ka + kb)
        rows = [g_pre_mix[l], g_post_mix[l], conv_a_b[l], conv_b_b[l], b_gate_r[l], b_gate_i[l], lru_lambda[l],
                g_pre_mlp[l], g_post_mlp[l]]
        vecs.append(jnp.concatenate([jnp.stack(rows, axis=0), conv_full[base:base + ka + kb]], axis=0))

    wgs =[jnp.concatenate([w_gate_r[l], w_gate_i[l]], axis=-1).astype(BF16) for l in range(n_layer)]

    xs = x[0]
    saved = []
    for l in range(n_layer):
        wl = wfull[l]
        (h, proj, dgel), carried = _norm_proj(xs, mods[l], vecs[l], wl["w_in"], f"norm_proj_{l}", gather_carry(3 * l))
        gathered(3 * l, carried)
        (x1, conva, xb2, hh, abm, pa, pb, yy, gr, ggi, ga, gmult), carried = _mixer_forward(
            xs, proj, mods[l], vecs[l], wgs[l], wl["w_a_out"], wl["w_b_out"], wl["w_o"], f"mixer_forward_{l}",
            gather_carry(3 * l + 1))
        gathered(3 * l + 1, carried)
        (x2, h2, up, y2, *loss_tile), carried = _mlp_forward(
            x1, mods[l], vecs[l], wl["w_mlp_up"], wl["w_mlp_down"], f"mlp_forward_{l}", gather_carry(3 * l + 2),
            target=loss_target[0] if l == n_layer - 1 else None)
        gathered(3 * l + 2, carried)
        saved.append(dict(x=xs, h=h, proj=proj, x1=x1, conva=conva, xb2=xb2, hh=hh, abm=abm, pa=pa, pb=pb,
                          y=yy, r=gr, gi=ggi, a=ga, mult=gmult, dgel=dgel, h2=h2, up=up, y2=y2))
        xs = x2
    dxs = xs
    loss_block = jnp.pad(loss_tile[0], ((0, 0), (0, d - LANES)))

    chips_q = [q_me ^ 2, q_me ^ 1, q_me ^ 3]
    pf = jnp.stack([mc, q_me] + chips_q).astype(jnp.int32)
    rs = dict(grad={}, landed={}, to_send={}, from_chips={}, out={})
    to_exchange, to_scatter, to_join, to_gather = [], [], [], []
    small_own, small_all = {}, {}

    def ride(call, what, name=None):
        ex = list(to_exchange) if "x" in what else []
        sc = list(to_scatter) if "s" in what else []
        ga = list(to_gather) if "g" in what else []
        jn = []
        for key in (to_join if "j" in what else []):
            if key[0] not in [k[0] for k in jn]:
                jn.append(key)
        carries = []
        if ex:
            carries.append(_exchange_carry([rs["grad"][k] for k in ex]))
        if sc:
            carries.append(_scatter_carry([rs["to_send"][k] for k in sc]))
        if jn:
            carries.append(_join_carry([rs["out"][k[0]] for k in jn], [k[1] for k in jn]))
        if ga:
            carries.append(_allgather_carry([small_own[k] for k in ga]))
        if call is None:
            carried = _run_carries(carries, name) if carries else []
            res = None
        else:
            res, carried = call(carries)
        carried = list(carried)
        if ex:
            for k, ld in zip(ex, carried.pop(0)):
                to_exchange.remove(k)
                rs["landed"][k] = ld
                rs["to_send"][k] = _add_sibling_half(rs["grad"][k], ld, pf, f"rs_add_sibling_{k[0]}_{k[1]}")
                to_scatter.append(k)
        if sc:
            for k, fc in zip(sc, carried.pop(0)):
                to_scatter.remove(k)
                rs["out"][k[0]] = _add_chips(rs["grad"][k], rs["landed"][k], fc, pf, rs["out"].get(k[0]), k[1], n_layer,
                                             f"rs_add_chips_{k[0]}_{k[1]}")
                to_join.append(k)
        if jn:
            for k, o in zip(jn, carried.pop(0)):
                to_join.remove(k)
                rs["out"][k[0]] = o
        if ga:
            for k, o in zip(ga, carried.pop(0)):
                to_gather.remove(k)
                small_all[k] = o
        return res

    def gather_small(key, parts):
        small_own[key] = parts[0] if len(parts) == 1 else jnp.concatenate(parts, axis=0)
        to_gather.append(key)

    def ready(nm, l, g):
        rs["grad"][(nm, l)] = g
        to_exchange.append((nm, l))

    rowblk = lambda t: t.reshape(N_CHIP, t.shape[1] // N_CHIP, t.shape[2])
    small1_prev = None
    for l in reversed(range(n_layer)):
        wl, sv = wfull[l], saved[l]
        dx1, dy2, dup, small3 = ride(lambda cr: _mlp_backward(
            dxs, sv["x1"], sv["y2"], sv["up"], mods[l], vecs[l], wl["w_mlp_up"], wl["w_mlp_down"], f"mlp_backward_{l}", cr), "xsjg")
        ready("w_mlp_up", l, _weight_grad(sv["h2"], dup, f"grad_w_mlp_up_{l}", col_blocks=N_CHIP, tk=d)[0])
        g_down = ride(lambda cr: _weight_grad(sv["up"], dy2, f"grad_w_mlp_down_{l}", carries=cr, square_relu=True), "x")
        ready("w_mlp_down", l, rowblk(g_down))
        dproj, dab, small2, dwg = ride(lambda cr: _mixer_backward(
            dx1, sv["proj"], sv["conva"], sv["xb2"], sv["hh"], sv["pa"], sv["pb"], sv["y"],
            sv["r"], sv["gi"], sv["a"], sv["mult"], sv["dgel"], mods[l], vecs[l], wgs[l],
            wl["w_a_out"], wl["w_b_out"], wl["w_o"], f"mixer_backward_{l}", cr), "xsjg")
        gather_small(("late", l, "s"), ([small1_prev] if small1_prev is not None else []) + [small2, small3])
        gather_small(("late", l, "w"), [dwg.reshape(2 * bw, d).astype(BF16)])
        g_in = ride(lambda cr: _weight_grad(sv["h"], dproj, f"grad_w_in_{l}", col_blocks=N_CHIP, carries=cr), "xsj")
        ready("w_in", l, g_in)
        g_abo = ride(lambda cr: _weight_grad_stacked(sv["abm"], dab, f"grad_w_abo_{l}", tk=d, carries=cr), "xg")
        ready("w_abo", l, g_abo)
        dxs, small1_prev = ride(lambda cr: _proj_backward(dproj, dx1, sv["x"], mods[l], vecs[l], wl["w_in"],
                                                          f"proj_backward_{l}", cr), "xsjg")
    grad_x = dxs[None]
    gather_small(("last", 0, "s"), [small1_prev, loss_block])

    tail = 0
    while to_exchange or to_scatter or to_join or to_gather:
        ride(None, "xsjg", f"rs_tail_{tail}")
        tail += 1
    grads, deltas, new_m, new_v = {}, {}, {}, {}

    def adam(nms, copy_grad=False):
        items = [(weights[nm], grads[nm], mom1[nm], mom2[nm]) for nm in nms]
        res, _ = _adamw(items, "adamw_" + "_".join(nms), copy_grad)
        for nm, r in zip(nms, res):
            deltas[nm], new_m[nm], new_v[nm] = r[:3]
            if copy_grad:
                grads[nm] = r[3]

    for nms in (["w_mlp_up", "w_mlp_down"], ["w_in"]):
        for nm in nms:
            grads[nm] = rs["out"][nm].reshape(weights[nm].shape)
        adam(nms, True)

    sums ={k: _sum_devices(small_all[k], small_own[k], me_arr, f"sum_small_{k[0]}_{k[1]}_{k[2]}") for k in small_own}

    loss = sums[("last", 0, "s")][SUBLANES, 0]
    small_full = {}

    def rows_of(l, part):
        if part == 0:
            return (("late", l - 1, "s"), 0) if l >= 1 else (("last", 0, "s"), 0)
        if part == 3:
            return ("late", l, "w"), 0
        base = SUBLANES if l < n_layer - 1 else 0
        return ("late", l, "s"), base + (0, 0, 2 * SUBLANES)[part]

    def summed(l, part, row, n_rows=1):
        key, base = rows_of(l, part)
        return sums[key][base + row:base + row + n_rows]

    def per_device(l, part, row):
        key, base = rows_of(l, part)
        own = small_own[key]
        if key not in small_full:
            small_full[key] = lax.dynamic_update_slice(small_all[key], own, (me_dev * own.shape[0], 0)).reshape(
                (N_DEV,) + own.shape)
        return small_full[key][:, base + row:base + row + 1]

    mod_rows = [(0, SB1_DSH), (0, SB1_DSC), (1, SB2_DGT), (2, SB3_DSH), (2, SB3_DSC), (2, SB3_DGT)]
    dmod_all = jnp.stack([jnp.concatenate([per_device(l, p, r)[:, 0, :] for p, r in mod_rows], axis=1)
                          for l in range(n_layer)], axis=0)
    o1, o2, o3, o4 = 0, SUBLANES, 3 * SUBLANES, 4 * SUBLANES
    small_sum = jnp.stack([jnp.concatenate([summed(l, 0, 0, SUBLANES), summed(l, 1, 0, 2 * SUBLANES),
                                            summed(l, 2, 0, SUBLANES), summed(l, 3, 0, 2 * bw)], axis=0)
                           for l in range(n_layer)], axis=0)
    mod_rows_of = [o1 + SB1_DSH, o1 + SB1_DSC, o2 + SB2_DGT, o3 + SB3_DSH, o3 + SB3_DSC, o3 + SB3_DGT]
    grads["w_mod"] = _mod_backward(c_all.T, lax.dynamic_slice_in_dim(dmod_all, q_me * mq, mq, axis=2), "mod_backward")
    grads["b_mod"] = jnp.concatenate([small_sum[:, k, :] for k in mod_rows_of], axis=1)
    grads["g_pre_mix"] = small_sum[:, o1 + SB1_DG_PRE]
    grads["g_post_mix"] = small_sum[:, o2 + SB2_DG_POST]
    grads["conv_a_w"] = lax.dynamic_slice_in_dim(small_sum[:, o2 + SB2_DWA:o2 + SB2_DWA + ka], q_me * dq, dq, axis=2)
    grads["conv_a_b"] = small_sum[:, o2 + SB2_DBA]
    grads["conv_b_w"] = lax.dynamic_slice_in_dim(small_sum[:, o2 + SB2_DWB:o2 + SB2_DWB + kb], q_me * dq, dq, axis=2)
    grads["conv_b_b"] = small_sum[:, o2 + SB2_DBB]
    grads["lru_lambda"] = small_sum[:, o2 + SB2_DLAM]
    grads["b_gate_r"] = small_sum[:, o2 + SB2_DBR]
    grads["b_gate_i"] = small_sum[:, o2 + SB2_DBI]
    grads["g_pre_mlp"] = small_sum[:, o3 + SB3_DG_PRE]
    grads["g_post_mlp"] = small_sum[:, o3 + SB3_DG_POST]
    dwg_sum = small_sum[:, o4:].reshape(n_layer, n_head, bw, 2 * bw)
    grads["w_gate_r"] = dwg_sum[..., :bw]
    grads["w_gate_i"] = dwg_sum[..., bw:]

    for k, nm in enumerate(groups[1]):
        grads[nm] = rs["out"]["w_abo"][:, k * dq:(k + 1) * dq]

    by_shape = {}
    for nm in names:
        if nm not in deltas:
            by_shape.setdefault(weights[nm].shape, []).append(nm)
    for nms in by_shape.values():
        adam(nms)
    return (loss, grad_x, *[grads[nm] for nm in names], *[deltas[nm] for nm in names],
            *[new_m[nm] for nm in names], *[new_v[nm] for nm in names])
```

```python
import jax
import jax.numpy as jnp
from jax import lax
from jax.experimental import pallas as pl
from jax.experimental.pallas import tpu as pltpu

F32 = jnp.float32
BF16 = jnp.bfloat16
MESH = pl.DeviceIdType.MESH

EPS = 1e-6
LRU_C = 8.0
N_CHIP = 4
N_DEV = 8
ADAM_LR = 0.001
ADAM_B1 = 0.9
ADAM_B2 = 0.999
ADAM_EPS = 1e-08
ADAM_WD = 0.01
ADAM_STEP = 10

VMEM_LIMIT_BYTES = 56 * 1024 * 1024
SUBLANES = 8
LANES = 128
TOKENS_MATMUL_TILE = 512
TOKENS_MIXER_TILE = 256
GELU_K0 = 0.7978845608028654
GELU_K1 = 0.044715

V_G_PRE_MIX, V_G_POST_MIX, V_CONV_A_B, V_CONV_B_B, V_B_GATE_R, V_B_GATE_I, V_LAMBDA, V_G_PRE_MLP, V_G_POST_MLP = range(9)
V_CONV_A_W = 9
V_CONV_B_W = 12
M_SH_M, M_SC_M, M_GT_M, M_SH_F, M_SC_F, M_GT_F = range(6)


def _cparams(n_grid=0):
    sem = ("arbitrary",) * n_grid if n_grid else None
    return pltpu.CompilerParams(dimension_semantics=sem, vmem_limit_bytes=VMEM_LIMIT_BYTES)


def _full(shape):
    return pl.BlockSpec(shape, lambda *_: (0,) * len(shape))


def _dot(a, b):
    return jnp.dot(a, b, preferred_element_type=F32)


def _dot_tb(a, b):
    return lax.dot_general(a, b, (((1,), (1,)), ((), ())), preferred_element_type=F32)


def _dot_ta(a, b):
    return lax.dot_general(a, b, (((0,), (0,)), ((), ())), preferred_element_type=F32)


def _sigmoid(x):
    return 1.0 / (1.0 + jnp.exp(-x))


def _softplus(x):
    return jnp.maximum(x, 0.0) + jnp.log1p(jnp.exp(-jnp.abs(x)))


def _neg_expm1(x):
    series = -x * (1.0 + 0.5 * x * (1.0 + (x / 3.0) * (1.0 + 0.25 * x)))
    return jnp.where(x > -1e-2, series, 1.0 - jnp.exp(x))


def _gelu_and_grad(x):
    x2 = x * x
    s = _sigmoid(x * (2.0 * GELU_K0 + (2.0 * GELU_K0 * GELU_K1) * x2))
    gel = x * s
    return gel, s + gel * (1.0 - s) * (2.0 * GELU_K0 + (6.0 * GELU_K0 * GELU_K1) * x2)


def _rms(x):
    r = lax.rsqrt(jnp.mean(x * x, axis=-1, keepdims=True) + EPS)
    return x * r, r


def _rms_bwd(dxn, xn, r):
    return r * (dxn - xn * jnp.mean(dxn * xn, axis=-1, keepdims=True))


def _colsum(x):
    return jnp.sum(x, axis=0, keepdims=True)


def _rows(t, w):
    return lax.broadcasted_iota(jnp.int32, (t, w), 0)


def _shift_down(x, k, prev8):
    t, w = x.shape
    rolled = pltpu.roll(x, k, 0)
    head = jnp.where(_rows(SUBLANES, w) < k, pltpu.roll(prev8, k, 0), rolled[:SUBLANES])
    return jnp.concatenate([head, rolled[SUBLANES:]], axis=0)


def _shift_up(x, k, next8):
    t, w = x.shape
    rolled = pltpu.roll(x, t - k, 0)
    tail = jnp.where(_rows(SUBLANES, w) >= SUBLANES - k, pltpu.roll(next8, SUBLANES - k, 0), rolled[t - SUBLANES:])
    return jnp.concatenate([rolled[:t - SUBLANES], tail], axis=0)


SCAN_GROUP = 16


def _scan_steps(a, b, group, reverse):
    t, w = a.shape
    pos = _rows(t, w) & (group - 1)
    s = 1
    while s < group:
        keep = (pos < group - s) if reverse else (pos >= s)
        shift = (t - s) if reverse else s
        b = b + a * jnp.where(keep, pltpu.roll(b, shift, 0), 0.0)
        a = a * jnp.where(keep, pltpu.roll(a, shift, 0), 1.0)
        s *= 2
    return b, a


def _scan_two_level(a, b, carry_row, a_buf, b_buf, c_buf, reverse):
    t, w = a.shape
    grp = SCAN_GROUP
    n_grp = t // grp
    h_loc, a_cum = _scan_steps(a, b, grp, reverse)
    end = 0 if reverse else grp - 1
    a_end, h_end = [], []
    for j in range(w // LANES):
        a_buf[j] = a_cum[:, j * LANES:(j + 1) * LANES]
        b_buf[j] = h_loc[:, j * LANES:(j + 1) * LANES]
        a_end.append(a_buf[j, pl.ds(end, n_grp, stride=grp), :])
        h_end.append(b_buf[j, pl.ds(end, n_grp, stride=grp), :])
    a_end = jnp.concatenate(a_end, axis=1)
    h_end = jnp.concatenate(h_end, axis=1)
    h_grp, a_grp = _scan_steps(a_end, h_end, n_grp, reverse)
    h_grp = h_grp + a_grp * carry_row
    rows = _rows(n_grp, w)
    if reverse:
        entering = jnp.where(rows == n_grp - 1, carry_row, pltpu.roll(h_grp, n_grp - 1, 0))
    else:
        entering = jnp.where(rows == 0, carry_row, pltpu.roll(h_grp, 1, 0))
    c_buf[...] = entering
    out = [h_loc[g * grp:(g + 1) * grp] + a_cum[g * grp:(g + 1) * grp] * c_buf[g:g + 1, :] for g in range(n_grp)]
    return jnp.concatenate(out, axis=0)


def _row_tile(rows, cols, itemsize=4, target_bytes=2 * 1024 * 1024):
    if rows * cols * itemsize <= target_bytes or rows % SUBLANES:
        return rows
    t = max(SUBLANES, (target_bytes // (cols * itemsize)) // SUBLANES * SUBLANES)
    while rows % t:
        t -= SUBLANES
    return t


def _place():
    return lax.axis_index("x"), lax.axis_index("y"), lax.axis_index("c")


def _other_chips(x, y):
    chips = [(1 - x, y), (x, 1 - y), (1 - x, 1 - y)]
    return chips, [2 * cx + cy for cx, cy in chips]


def _all_gather_small(block, name):
    m_per, n = block.shape

    def body(x_ref, out_ref, send_sems, recv_sems, local_sem):
        x, y, c = _place()
        me, sibling = (x, y, c), (x, y, 1 - c)
        chips, _ = _other_chips(x, y)

        def rows(px, py, pc):
            return out_ref.at[pl.ds((4 * px + 2 * py + pc) * m_per, m_per), :]

        def copy(k, blk, to, src=None):
            return pltpu.make_async_remote_copy(
                src_ref=rows(*blk) if src is None else src, dst_ref=rows(*blk),
                send_sem=send_sems.at[k], recv_sem=recv_sems.at[k], device_id=to, device_id_type=MESH)

        mine = pltpu.make_async_copy(x_ref, rows(*me), local_sem)
        mine.start()
        first = [copy(0, me, sibling, src=x_ref)]
        first += [copy(1 + j, me, (*chip, c), src=x_ref) for j, chip in enumerate(chips)]
        for cp in first:
            cp.start()
        passed = [copy(4 + j, (*chip, c), sibling) for j, chip in enumerate(chips)]
        for j, chip in enumerate(chips):
            copy(1 + j, (*chip, c), me).wait_recv()
            passed[j].start()
        copy(0, sibling, me).wait_recv()
        for j, chip in enumerate(chips):
            copy(4 + j, (*chip, 1 - c), me).wait_recv()
        for cp in first + passed:
            cp.wait_send()
        mine.wait()

    return pl.pallas_call(
        body, name=name,
        out_shape=jax.ShapeDtypeStruct((N_DEV * m_per, n), block.dtype),
        in_specs=[pl.BlockSpec(memory_space=pltpu.VMEM)],
        out_specs=pl.BlockSpec(memory_space=pltpu.VMEM),
        scratch_shapes=[pltpu.SemaphoreType.DMA((7,)), pltpu.SemaphoreType.DMA((7,)), pltpu.SemaphoreType.DMA],
        compiler_params=pltpu.CompilerParams(vmem_limit_bytes=VMEM_LIMIT_BYTES),
    )(block)


class _Carry:
    def __init__(self, ins, out_shapes, aliases, sem_shapes, start, finish, mid=None, mid_frac=0.85):
        self.ins, self.out_shapes, self.aliases, self.sem_shapes = list(ins), list(out_shapes), dict(aliases), list(sem_shapes)
        self.start, self.mid, self.finish, self.mid_frac = start, mid, finish, mid_frac


def _pcall(body, *, name, grid, in_specs, out_specs, out_shape, args, scratch_shapes=(), carries=(), prefetch=()):
    in_specs, out_specs, out_shape = list(in_specs), list(out_specs), list(out_shape)
    scratch_shapes, args = list(scratch_shapes), list(args)
    n_in, n_out, n_scr, n_pre = len(in_specs), len(out_shape), len(scratch_shapes), len(prefetch)
    steps = 1
    for g in grid:
        steps *= g
    any_spec = pl.BlockSpec(memory_space=pl.ANY)
    aliases = {}
    spans = []
    for cr in carries:
        spans.append((len(args), len(out_shape), len(scratch_shapes)))
        for a, b in cr.aliases.items():
            aliases[n_pre + len(args) + a] = len(out_shape) + b
        args += cr.ins
        in_specs += [any_spec] * len(cr.ins)
        out_shape += cr.out_shapes
        out_specs += [any_spec] * len(cr.out_shapes)
        scratch_shapes += cr.sem_shapes
    n_all_in = len(args)
    n_all_out = len(out_shape)

    def wrapped(*refs):
        pre, refs = refs[:n_pre], refs[n_pre:]
        ins, outs, scr = refs[:n_all_in], refs[n_all_in:n_all_in + n_all_out], refs[n_all_in + n_all_out:]
        parts = [(cr, ins[a:a + len(cr.ins)], outs[b:b + len(cr.out_shapes)], scr[s:s + len(cr.sem_shapes)])
                 for cr, (a, b, s) in zip(carries, spans)]
        lin = 0
        for ax, g in enumerate(grid):
            lin = lin * g + pl.program_id(ax)

        def at(step, fn):
            if steps == 1:
                fn()
            else:
                pl.when(lin == step)(fn)

        def start_all():
            for cr, ci, co, cs in parts:
                cr.start(ci, co, cs)

        def finish_all():
            for cr, ci, co, cs in parts:
                cr.finish(ci, co, cs)

        if parts:
            at(0, start_all)
        body(*pre, *ins[:n_in], *outs[:n_out], *scr[:n_scr])
        for cr, ci, co, cs in parts:
            if cr.mid is not None:
                at(min(steps - 1, int(steps * cr.mid_frac)), lambda cr=cr, ci=ci, co=co, cs=cs: cr.mid(ci, co, cs))
        if parts:
            at(steps - 1, finish_all)

    if n_pre:
        res = pl.pallas_call(
            wrapped, name=name, out_shape=out_shape,
            grid_spec=pltpu.PrefetchScalarGridSpec(num_scalar_prefetch=n_pre, grid=tuple(grid), in_specs=in_specs,
                                                   out_specs=out_specs, scratch_shapes=scratch_shapes),
            input_output_aliases=aliases, compiler_params=_cparams(len(grid)),
        )(*prefetch, *args)
    else:
        res = pl.pallas_call(
            wrapped, name=name, grid=tuple(grid), out_shape=out_shape, in_specs=in_specs, out_specs=out_specs,
            scratch_shapes=scratch_shapes, input_output_aliases=aliases, compiler_params=_cparams(len(grid)),
        )(*args)
    res = list(res)
    return res[:n_out], [res[b:b + len(cr.out_shapes)] for cr, (_, b, _) in zip(carries, spans)]


def _run_carries(carries, name):
    return _pcall(lambda: None, name=name, grid=(), in_specs=[], out_specs=[], out_shape=[], args=[], carries=carries)[1]


CAST_STEPS = 8


def _cast_place_all(shards, q_arr, name, carries=()):
    n = len(shards)

    def body(q_ref, *refs):
        for k in range(n):
            refs[n + k][...] = refs[k][...].astype(BF16)

    def spec_in(k):
        w, layer = shards[k]
        return pl.BlockSpec((1, w.shape[1] // CAST_STEPS, w.shape[2]), lambda i, q_ref: (layer, i, 0))

    def spec_out(k):
        w, _ = shards[k]
        return pl.BlockSpec((1, w.shape[1] // CAST_STEPS, w.shape[2]), lambda i, q_ref: (q_ref[0], i, 0))

    return _pcall(
        body, name=name, grid=(CAST_STEPS,),
        out_shape=[jax.ShapeDtypeStruct((N_CHIP,) + w.shape[1:], BF16) for w, _ in shards],
        in_specs=[spec_in(k) for k in range(n)], out_specs=[spec_out(k) for k in range(n)],
        args=[w for w, _ in shards], carries=carries, prefetch=[q_arr])


def _gather_carry(bufs, mid_frac=0.85):
    n = len(bufs)

    def copies(o_refs, sems):
        send_sems, recv_sems = sems
        x, y, c = _place()
        q = 2 * x + y
        sibling = (x, y, 1 - c)
        chips, qs = _other_chips(x, y)

        def half(w, shard, pc):
            rh = bufs[w].shape[1] // 2
            return o_refs[w].at[shard, pl.ds(pc * rh, rh), :]

        def over_ici(w, j, shard):
            blk = half(w, shard, c)
            return pltpu.make_async_remote_copy(
                src_ref=blk, dst_ref=blk, send_sem=send_sems.at[w, j], recv_sem=recv_sems.at[w, j],
                device_id=(*chips[j], c), device_id_type=MESH)

        def to_sibling(w, j, pc):
            blk = half(w, qs[j], pc)
            return pltpu.make_async_remote_copy(
                src_ref=blk, dst_ref=blk, send_sem=send_sems.at[w, 3 + j], recv_sem=recv_sems.at[w, 3 + j],
                device_id=sibling, device_id_type=MESH)

        return q, c, qs, over_ici, to_sibling

    pairs = [(w, j) for w in range(n) for j in range(3)]

    def start(i_refs, o_refs, sems):
        q, _, _, over_ici, _ = copies(o_refs, sems)
        for w, j in pairs:
            over_ici(w, j, q).start()

    def mid(i_refs, o_refs, sems):
        _, c, qs, over_ici, to_sibling = copies(o_refs, sems)
        for w, j in pairs:
            over_ici(w, j, qs[j]).wait_recv()
            to_sibling(w, j, c).start()

    def finish(i_refs, o_refs, sems):
        q, c, _, over_ici, to_sibling = copies(o_refs, sems)
        for w, j in pairs:
            to_sibling(w, j, 1 - c).wait_recv()
        for w, j in pairs:
            over_ici(w, j, q).wait_send()
            to_sibling(w, j, c).wait_send()

    return _Carry(bufs, [jax.ShapeDtypeStruct(b.shape, b.dtype) for b in bufs], {w: w for w in range(n)},
                  [pltpu.SemaphoreType.DMA((n, 6)), pltpu.SemaphoreType.DMA((n, 6))], start, finish, mid, mid_frac)


def _exchange_carry(grads):
    n = len(grads)

    def copies(g_refs, l_refs, sems):
        send_sems, recv_sems = sems
        x, y, c = _place()
        out = []
        for w in range(n):
            rh = grads[w].shape[1] // 2
            out.append(pltpu.make_async_remote_copy(
                src_ref=g_refs[w].at[:, pl.ds((1 - c) * rh, rh), :], dst_ref=l_refs[w],
                send_sem=send_sems.at[w], recv_sem=recv_sems.at[w], device_id=(x, y, 1 - c), device_id_type=MESH))
        return out

    def start(g_refs, l_refs, sems):
        for cp in copies(g_refs, l_refs, sems):
            cp.start()

    def finish(g_refs, l_refs, sems):
        for cp in copies(g_refs, l_refs, sems):
            cp.wait()

    return _Carry(grads, [jax.ShapeDtypeStruct((N_CHIP, g.shape[1] // 2, g.shape[2]), g.dtype) for g in grads], {},
                  [pltpu.SemaphoreType.DMA((n,)), pltpu.SemaphoreType.DMA((n,))], start, finish)


def _scatter_carry(sums):
    n = len(sums)

    def copies(s_refs, l_refs, sems):
        send_sems, recv_sems = sems
        x, y, c = _place()
        chips, _ = _other_chips(x, y)
        return [pltpu.make_async_remote_copy(
            src_ref=s_refs[w].at[j], dst_ref=l_refs[w].at[j], send_sem=send_sems.at[w, j], recv_sem=recv_sems.at[w, j],
            device_id=(*chips[j], c), device_id_type=MESH) for w in range(n) for j in range(3)]

    def start(s_refs, l_refs, sems):
        for cp in copies(s_refs, l_refs, sems):
            cp.start()

    def finish(s_refs, l_refs, sems):
        for cp in copies(s_refs, l_refs, sems):
            cp.wait()

    return _Carry(sums, [jax.ShapeDtypeStruct(s.shape, s.dtype) for s in sums], {},
                  [pltpu.SemaphoreType.DMA((n, 3)), pltpu.SemaphoreType.DMA((n, 3))], start, finish)


def _join_carry(outs, layers):
    n = len(outs)

    def copy(o_refs, sems, w, mine):
        send_sems, recv_sems = sems
        x, y, c = _place()
        r = outs[w].shape[1]
        rows = o_refs[w].at[layers[w], pl.ds((c if mine else 1 - c) * (r // 2), r // 2), :]
        return pltpu.make_async_remote_copy(
            src_ref=rows, dst_ref=rows, send_sem=send_sems.at[w], recv_sem=recv_sems.at[w],
            device_id=(x, y, 1 - c), device_id_type=MESH)

    def start(i_refs, o_refs, sems):
        for w in range(n):
            copy(o_refs, sems, w, True).start()

    def finish(i_refs, o_refs, sems):
        for w in range(n):
            copy(o_refs, sems, w, True).wait_send()
        for w in range(n):
            copy(o_refs, sems, w, False).wait_recv()

    return _Carry(outs, [jax.ShapeDtypeStruct(o.shape, o.dtype) for o in outs], {w: w for w in range(n)},
                  [pltpu.SemaphoreType.DMA((n,)), pltpu.SemaphoreType.DMA((n,))], start, finish)


PF_C, PF_Q, PF_QS = 0, 1, 2


def _add_sibling_half(g, landed, pf, name):
    _, r, cols = g.shape
    rh = r // 2
    tr = _row_tile(rh, cols)
    nr = rh // tr

    def body(pf_ref, g_ref, l_ref, o_ref):
        o_ref[...] = (g_ref[...] + l_ref[...]).astype(BF16)

    return pl.pallas_call(
        body, name=name,
        out_shape=jax.ShapeDtypeStruct((3, rh, cols), BF16),
        grid_spec=pltpu.PrefetchScalarGridSpec(
            num_scalar_prefetch=1, grid=(3, nr),
            in_specs=[pl.BlockSpec((1, tr, cols), lambda j, i, pf_ref: (pf_ref[PF_QS + j], pf_ref[PF_C] * nr + i, 0)),
                      pl.BlockSpec((1, tr, cols), lambda j, i, pf_ref: (pf_ref[PF_QS + j], i, 0))],
            out_specs=pl.BlockSpec((1, tr, cols), lambda j, i, pf_ref: (j, i, 0))),
        compiler_params=_cparams(2),
    )(pf, g, landed)


def _add_chips(g, landed, from_chips, pf, prev, layer, n_layer, name):
    _, r, cols = g.shape
    rh = r // 2
    tr = _row_tile(rh, cols)
    nr = rh // tr

    def body(pf_ref, g_ref, l_ref, f_ref, *rest):
        o_ref = rest[-1]
        acc = g_ref[0] + l_ref[0]
        for j in range(3):
            acc = acc + f_ref[j].astype(F32)
        o_ref[0] = acc

    in_specs = [pl.BlockSpec((1, tr, cols), lambda i, pf_ref: (pf_ref[PF_Q], pf_ref[PF_C] * nr + i, 0)),
                pl.BlockSpec((1, tr, cols), lambda i, pf_ref: (pf_ref[PF_Q], i, 0)),
                pl.BlockSpec((3, tr, cols), lambda i, pf_ref: (0, i, 0))]
    args = [pf, g, landed, from_chips]
    aliases = {}
    if prev is not None:
        in_specs.append(pl.BlockSpec(memory_space=pl.ANY))
        args.append(prev)
        aliases = {4: 0}
    return pl.pallas_call(
        body, name=name,
        out_shape=jax.ShapeDtypeStruct((n_layer, r, cols), F32),
        grid_spec=pltpu.PrefetchScalarGridSpec(
            num_scalar_prefetch=1, grid=(nr,), in_specs=in_specs,
            out_specs=pl.BlockSpec((1, tr, cols), lambda i, pf_ref: (layer, pf_ref[PF_C] * nr + i, 0))),
        input_output_aliases=aliases,
        compiler_params=_cparams(1),
    )(*args)


def _allgather_carry(blocks):
    n = len(blocks)

    def copies(b_refs, o_refs, sems):
        send_sems, recv_sems = sems
        x, y, c = _place()
        chips, _ = _other_chips(x, y)

        def place(w, px, py, pc):
            m = blocks[w].shape[0]
            return o_refs[w].at[pl.ds((4 * px + 2 * py + pc) * m, m), :]

        def own_to(w, k, to):
            dst = place(w, x, y, c)
            return pltpu.make_async_remote_copy(src_ref=b_refs[w], dst_ref=dst, send_sem=send_sems.at[w, k],
                                                recv_sem=recv_sems.at[w, k], device_id=to, device_id_type=MESH)

        def landed_from(w, k, px, py, pc):
            blk = place(w, px, py, pc)
            return pltpu.make_async_remote_copy(src_ref=blk, dst_ref=blk, send_sem=send_sems.at[w, k],
                                                recv_sem=recv_sems.at[w, k], device_id=(x, y, 1 - c), device_id_type=MESH)

        return x, y, c, chips, own_to, landed_from

    def start(b_refs, o_refs, sems):
        x, y, c, chips, own_to, _ = copies(b_refs, o_refs, sems)
        for w in range(n):
            own_to(w, 0, (x, y, 1 - c)).start()
            for j, chip in enumerate(chips):
                own_to(w, 1 + j, (*chip, c)).start()

    def mid(b_refs, o_refs, sems):
        x, y, c, chips, _, landed_from = copies(b_refs, o_refs, sems)
        for w in range(n):
            for j, chip in enumerate(chips):
                landed_from(w, 1 + j, *chip, c).wait_recv()
                landed_from(w, 4 + j, *chip, c).start()

    def finish(b_refs, o_refs, sems):
        x, y, c, chips, own_to, landed_from = copies(b_refs, o_refs, sems)
        for w in range(n):
            landed_from(w, 0, x, y, 1 - c).wait_recv()
            for j, chip in enumerate(chips):
                landed_from(w, 4 + j, *chip, 1 - c).wait_recv()
            own_to(w, 0, (x, y, 1 - c)).wait_send()
            for j, chip in enumerate(chips):
                own_to(w, 1 + j, (*chip, c)).wait_send()
                landed_from(w, 4 + j, *chip, c).wait_send()

    return _Carry(blocks, [jax.ShapeDtypeStruct((N_DEV * b.shape[0], b.shape[1]), b.dtype) for b in blocks], {},
                  [pltpu.SemaphoreType.DMA((n, 7)), pltpu.SemaphoreType.DMA((n, 7))], start, finish, mid)


def _sum_devices(gathered, own, me_arr, name):
    m, n = own.shape
    tr = _row_tile(m, n, itemsize=own.dtype.itemsize, target_bytes=256 * 1024)
    nr = m // tr

    def body(me_ref, *refs):
        g_refs, own_ref, o_ref = refs[:N_DEV], refs[N_DEV], refs[N_DEV + 1]
        me = me_ref[0]
        acc = None
        for dev in range(N_DEV):
            term = jnp.where(me == dev, own_ref[...], g_refs[dev][...]).astype(F32)
            acc = term if acc is None else acc + term
        o_ref[...] = acc

    def dev_rows(dev):
        return pl.BlockSpec((tr, n), lambda i, me_ref: (dev * nr + i, 0))

    return pl.pallas_call(
        body, name=name,
        out_shape=jax.ShapeDtypeStruct((m, n), F32),
        grid_spec=pltpu.PrefetchScalarGridSpec(
            num_scalar_prefetch=1, grid=(nr,),
            in_specs=[dev_rows(dev) for dev in range(N_DEV)] + [pl.BlockSpec((tr, n), lambda i, me_ref: (i, 0))],
            out_specs=pl.BlockSpec((tr, n), lambda i, me_ref: (i, 0))),
        compiler_params=_cparams(1),
    )(me_arr, *([gathered] * N_DEV), own)


def _mod_forward(c_all, w_mod, b_mod_shard, name):
    n_layer, d, mq = w_mod.shape

    def body(c_ref, w_ref, b_ref, o_ref):
        cv = c_ref[...]
        o_ref[...] = _dot(cv * _sigmoid(cv), w_ref[0]) + b_ref[0]

    return pl.pallas_call(
        body, name=name, grid=(n_layer,),
        out_shape=jax.ShapeDtypeStruct((n_layer * N_DEV, mq), F32),
        in_specs=[_full((N_DEV, d)), pl.BlockSpec((1, d, mq), lambda l: (l, 0, 0)),
                  pl.BlockSpec((1, 1, mq), lambda l: (l, 0, 0))],
        out_specs=pl.BlockSpec((N_DEV, mq), lambda l: (l, 0)),
        compiler_params=_cparams(1),
    )(c_all, w_mod, b_mod_shard.reshape(n_layer, 1, mq))


def _mod_backward(c_all_t, dmod_shard, name):
    n_layer, _, mq = dmod_shard.shape
    d = c_all_t.shape[0]

    def body(c_ref, dm_ref, o_ref):
        cv = c_ref[...]
        o_ref[0] = _dot(cv * _sigmoid(cv), dm_ref[0])

    return pl.pallas_call(
        body, name=name, grid=(n_layer,),
        out_shape=jax.ShapeDtypeStruct((n_layer, d, mq), F32),
        in_specs=[_full((d, N_DEV)), pl.BlockSpec((1, N_DEV, mq), lambda l: (l, 0, 0))],
        out_specs=pl.BlockSpec((1, d, mq), lambda l: (l, 0, 0)),
        compiler_params=_cparams(1),
    )(c_all_t, dmod_shard)


def _norm_proj(x, mod, vec, w_in, name, carries=()):
    s, d = x.shape
    nq = w_in.shape[2]
    ts = min(TOKENS_MATMUL_TILE, s)

    def body(x_ref, mod_ref, vec_ref, w_ref, h_ref, p_ref, dgel_ref):
        xn, _ = _rms(x_ref[...])
        gm = vec_ref[V_G_PRE_MIX:V_G_PRE_MIX + 1, :] * (1.0 + mod_ref[M_SC_M:M_SC_M + 1, :])
        h = (xn * gm + mod_ref[M_SH_M:M_SH_M + 1, :]).astype(BF16)
        h_ref[...] = h
        for qb in range(N_CHIP):
            pq = _dot(h, w_ref[qb])
            for k in range(N_CHIP * nq // d):
                lo, hi = max(qb * nq, k * d), min((qb + 1) * nq, (k + 1) * d)
                if lo >= hi:
                    continue
                piece = pq[:, lo - qb * nq:hi - qb * nq]
                if k == 4:
                    piece, dgel = _gelu_and_grad(piece)
                    dgel_ref[:, lo - 4 * d:hi - 4 * d] = dgel.astype(BF16)
                elif k >= 5:
                    piece = _sigmoid(piece)
                p_ref[:, lo:hi] = piece.astype(BF16)

    tile = pl.BlockSpec((ts, d), lambda i: (i, 0))
    return _pcall(
        body, name=name, grid=(s // ts,),
        out_shape=[jax.ShapeDtypeStruct((s, d), BF16), jax.ShapeDtypeStruct((s, N_CHIP * nq), BF16),
                   jax.ShapeDtypeStruct((s, d), BF16)],
        in_specs=[tile, _full(mod.shape), _full(vec.shape), _full(w_in.shape)],
        out_specs=[tile, pl.BlockSpec((ts, N_CHIP * nq), lambda i: (i, 0)), tile],
        args=[x, mod, vec, w_in], carries=carries)


def _gate_pre(xb2_b, wg_ref, n_head, bw):
    zr, zi = [], []
    for hd in range(n_head):
        z = _dot(xb2_b[:, hd * bw:(hd + 1) * bw], wg_ref[hd])
        zr.append(z[:, :bw])
        zi.append(z[:, bw:])
    return jnp.concatenate(zr, axis=1), jnp.concatenate(zi, axis=1)


def _lru_coeffs(xb2, wg_ref, vec_ref, n_head, bw):
    zr, zi = _gate_pre(xb2.astype(BF16), wg_ref, n_head, bw)
    r = _sigmoid(zr + vec_ref[V_B_GATE_R:V_B_GATE_R + 1, :])
    gi = _sigmoid(zi + vec_ref[V_B_GATE_I:V_B_GATE_I + 1, :])
    sp = _softplus(-vec_ref[V_LAMBDA:V_LAMBDA + 1, :])
    log_a = (-LRU_C) * r * sp
    a = jnp.exp(log_a)
    mult = jnp.sqrt(_neg_expm1(2.0 * log_a))
    return r, gi, sp, a, mult


def _mixer_forward(x, proj, mod, vec, wg, w_a_out, w_b_out, w_o, name, carries=()):
    s, d = x.shape
    n_head, bw, _ = wg.shape
    ts = min(TOKENS_MIXER_TILE, s)

    def body(x_ref, p_ref, mod_ref, vec_ref, wg_ref, wa_ref, wb_ref, wo_ref,
             x1_ref, conva_ref, xb2_ref, hh_ref, abm_ref, pa_ref, pb_ref, y_ref,
             r_ref, gi_ref, a_ref, mult_ref,
             cv_tail, xb_tail, h_last, a_buf, b_buf, c_buf):
        i = pl.program_id(0)

        @pl.when(i == 0)
        def _():
            cv_tail[...] = jnp.zeros_like(cv_tail)
            xb_tail[...] = jnp.zeros_like(xb_tail)
            h_last[...] = jnp.zeros_like(h_last)

        def seg(k):
            return p_ref[:, k * d:(k + 1) * d].astype(F32)

        def vrow(k):
            return vec_ref[k:k + 1, :]

        b_a, c_a, v_a, x_b, gel, sa, sb = (seg(k) for k in range(7))
        cv = c_a * v_a
        prev_cv = cv_tail[...]
        conv_a = (vrow(V_CONV_A_B) + vrow(V_CONV_A_W) * _shift_down(cv, 2, prev_cv)
                  + vrow(V_CONV_A_W + 1) * _shift_down(cv, 1, prev_cv) + vrow(V_CONV_A_W + 2) * cv)
        cv_tail[...] = cv[ts - SUBLANES:]
        y_a = b_a * conv_a
        prev_xb = xb_tail[...]
        xb2 = (vrow(V_CONV_B_B) + vrow(V_CONV_B_W) * _shift_down(x_b, 3, prev_xb)
               + vrow(V_CONV_B_W + 1) * _shift_down(x_b, 2, prev_xb)
               + vrow(V_CONV_B_W + 2) * _shift_down(x_b, 1, prev_xb) + vrow(V_CONV_B_W + 3) * x_b)
        xb_tail[...] = x_b[ts - SUBLANES:]
        r, gi, _, a, mult = _lru_coeffs(xb2, wg_ref, vec_ref, n_head, bw)
        r_ref[...] = r
        gi_ref[...] = gi
        a_ref[...] = a
        mult_ref[...] = mult
        hh = _scan_two_level(a, mult * gi * xb2, h_last[SUBLANES - 1:SUBLANES, :], a_buf, b_buf, c_buf, reverse=False)
        h_last[...] = hh[ts - SUBLANES:]
        y_b = hh * gel
        ya_b, yb_b = y_a.astype(BF16), y_b.astype(BF16)
        pa = _dot(ya_b, wa_ref[...])
        pb = _dot(yb_b, wb_ref[...])
        m = (sa * pa + sb * pb).astype(BF16)
        y = _dot(m, wo_ref[...])
        yn, _ = _rms(y)
        gg = mod_ref[M_GT_M:M_GT_M + 1, :] * vrow(V_G_POST_MIX)
        x1_ref[...] = x_ref[...] + yn * gg
        conva_ref[...] = conv_a.astype(BF16)
        xb2_ref[...] = xb2
        hh_ref[...] = hh
        abm_ref[0] = ya_b
        abm_ref[1] = yb_b
        abm_ref[2] = m
        pa_ref[...] = pa.astype(BF16)
        pb_ref[...] = pb.astype(BF16)
        y_ref[...] = y.astype(BF16)

    tile = pl.BlockSpec((ts, d), lambda i: (i, 0))
    tile3 = pl.BlockSpec((3, ts, d), lambda i: (0, i, 0))
    sd = lambda dt: jax.ShapeDtypeStruct((s, d), dt)
    return _pcall(
        body, name=name, grid=(s // ts,),
        out_shape=[sd(F32), sd(BF16), sd(F32), sd(F32), jax.ShapeDtypeStruct((3, s, d), BF16), sd(BF16), sd(BF16), sd(BF16),
                   sd(F32), sd(F32), sd(F32), sd(F32)],
        in_specs=[tile, pl.BlockSpec((ts, 7 * d), lambda i: (i, 0)), _full(mod.shape), _full(vec.shape),
                  _full(wg.shape), _full(w_a_out.shape), _full(w_b_out.shape), _full(w_o.shape)],
        out_specs=[tile] * 4 + [tile3] + [tile] * 7,
        scratch_shapes=[pltpu.VMEM((SUBLANES, d), F32)] * 3 + [pltpu.VMEM((d // LANES, ts, LANES), F32)] * 2
                       + [pltpu.VMEM((ts // SCAN_GROUP, d), F32)],
        args=[x, proj, mod, vec, wg, w_a_out, w_b_out, w_o], carries=carries)


def _mlp_forward(x1, mod, vec, w_up, w_down, name, carries=(), target=None):
    s, d = x1.shape
    fq = w_up.shape[2]
    ts = min(TOKENS_MATMUL_TILE, s)

    def body(x_ref, *refs):
        if target is None:
            mod_ref, vec_ref, wu_ref, wd_ref, x2_ref, h2_ref, up_ref, y2_ref = refs
        else:
            t_ref, mod_ref, vec_ref, wu_ref, wd_ref, x2_ref, h2_ref, up_ref, y2_ref, loss_ref = refs
        x = x_ref[...]
        xn, _ = _rms(x)
        gm = vec_ref[V_G_PRE_MLP:V_G_PRE_MLP + 1, :] * (1.0 + mod_ref[M_SC_F:M_SC_F + 1, :])
        h2 = (xn * gm + mod_ref[M_SH_F:M_SH_F + 1, :]).astype(BF16)
        h2_ref[...] = h2
        y2 = jnp.zeros((ts, d), F32)
        for qb in range(N_CHIP):
            up = _dot(h2, wu_ref[qb])
            up_ref[:, qb * fq:(qb + 1) * fq] = up.astype(BF16)
            ru = jnp.maximum(up, 0.0)
            y2 = y2 + _dot((ru * ru).astype(BF16), wd_ref[qb])
        y2_ref[...] = y2.astype(BF16)
        yn, _ = _rms(y2)
        gg = mod_ref[M_GT_F:M_GT_F + 1, :] * vec_ref[V_G_POST_MLP:V_G_POST_MLP + 1, :]
        x2 = x + yn * gg
        if target is None:
            x2_ref[...] = x2
        else:
            @pl.when(pl.program_id(0) == 0)
            def _():
                loss_ref[...] = jnp.zeros_like(loss_ref)

            err = x2 - t_ref[...]
            x2_ref[...] = err * (1.0 / d)
            loss_ref[...] += jnp.sum(jnp.sum(err * err, axis=1, keepdims=True), axis=0, keepdims=True) * (0.5 / d)

    tile = pl.BlockSpec((ts, d), lambda i: (i, 0))
    last = target is not None
    return _pcall(
        body, name=name, grid=(s // ts,),
        out_shape=[jax.ShapeDtypeStruct((s, d), F32), jax.ShapeDtypeStruct((s, d), BF16),
                   jax.ShapeDtypeStruct((s, N_CHIP * fq), BF16), jax.ShapeDtypeStruct((s, d), BF16)]
                  + ([jax.ShapeDtypeStruct((SUBLANES, LANES), F32)] if last else []),
        in_specs=[tile] + ([tile] if last else []) + [_full(mod.shape), _full(vec.shape), _full(w_up.shape), _full(w_down.shape)],
        out_specs=[tile, tile, pl.BlockSpec((ts, N_CHIP * fq), lambda i: (i, 0)), tile]
                 + ([_full((SUBLANES, LANES))] if last else []),
        args=[x1] + ([target] if last else []) + [mod, vec, w_up, w_down], carries=carries)


SB3_DSH, SB3_DSC, SB3_DGT, SB3_DG_PRE, SB3_DG_POST = range(5)
SB1_DSH, SB1_DSC, SB1_DG_PRE = range(3)
(SB2_DGT, SB2_DG_POST, SB2_DWA, SB2_DBA, SB2_DWB, SB2_DBB, SB2_DLAM, SB2_DBR, SB2_DBI) = (0, 1, 2, 5, 6, 10, 11, 12, 13)


def _mlp_backward(dx2, x1, y2, up, mod, vec, w_up, w_down, name, carries=()):
    s, d = dx2.shape
    fq = w_up.shape[2]
    ts = min(TOKENS_MATMUL_TILE, s)
    n_t = s // ts

    def body(dx2_ref, x_ref, y2_ref, up_ref, mod_ref, vec_ref, wu_ref, wd_ref,
             dx1_ref, dy2_ref, dup_ref, small_ref):
        i = pl.program_id(0)

        @pl.when(i == 0)
        def _():
            small_ref[...] = jnp.zeros_like(small_ref)

        dout = dx2_ref[...]
        y2n, ry = _rms(y2_ref[...].astype(F32))
        g_post = vec_ref[V_G_POST_MLP:V_G_POST_MLP + 1, :]
        gt = mod_ref[M_GT_F:M_GT_F + 1, :]
        dgg = _colsum(dout * y2n)
        dy2 = _rms_bwd(dout * (gt * g_post), y2n, ry).astype(BF16)
        dy2_ref[...] = dy2
        dh2 = jnp.zeros((ts, d), F32)
        for qb in range(N_CHIP):
            cols = slice(qb * fq, (qb + 1) * fq)
            dact = _dot_tb(dy2, wd_ref[qb])
            ru = jnp.maximum(up_ref[:, cols].astype(F32), 0.0)
            dup = (dact * (2.0 * ru)).astype(BF16)
            dup_ref[:, cols] = dup
            dh2 = dh2 + _dot_tb(dup, wu_ref[qb])
        xn, r = _rms(x_ref[...])
        g_pre = vec_ref[V_G_PRE_MLP:V_G_PRE_MLP + 1, :]
        sc1 = 1.0 + mod_ref[M_SC_F:M_SC_F + 1, :]
        dsh = _colsum(dh2)
        dgm = _colsum(dh2 * xn)
        dx1_ref[...] = dout + _rms_bwd(dh2 * (g_pre * sc1), xn, r)
        small_ref[SB3_DSH:SB3_DSH + 1, :] += dsh
        small_ref[SB3_DSC:SB3_DSC + 1, :] += dgm
        small_ref[SB3_DGT:SB3_DGT + 1, :] += dgg

        @pl.when(i == n_t - 1)
        def _():
            dgm_t = small_ref[SB3_DSC:SB3_DSC + 1, :]
            dgg_t = small_ref[SB3_DGT:SB3_DGT + 1, :]
            small_ref[SB3_DSC:SB3_DSC + 1, :] = dgm_t * g_pre
            small_ref[SB3_DG_PRE:SB3_DG_PRE + 1, :] = dgm_t * sc1
            small_ref[SB3_DGT:SB3_DGT + 1, :] = dgg_t * g_post
            small_ref[SB3_DG_POST:SB3_DG_POST + 1, :] = dgg_t * gt

    tile = pl.BlockSpec((ts, d), lambda i: (i, 0))
    wide = pl.BlockSpec((ts, N_CHIP * fq), lambda i: (i, 0))
    return _pcall(
        body, name=name, grid=(n_t,),
        out_shape=[jax.ShapeDtypeStruct((s, d), F32), jax.ShapeDtypeStruct((s, d), BF16),
                   jax.ShapeDtypeStruct((s, N_CHIP * fq), BF16), jax.ShapeDtypeStruct((SUBLANES, d), F32)],
        in_specs=[tile, tile, tile, wide, _full(mod.shape), _full(vec.shape), _full(w_up.shape), _full(w_down.shape)],
        out_specs=[tile, tile, wide, _full((SUBLANES, d))],
        args=[dx2, x1, y2, up, mod, vec, w_up, w_down], carries=carries)


def _mixer_backward(dx1, proj, conva, xb2s, hhs, pas, pbs, ys, rs_, gis, as_, mults, dgels, mod, vec, wg, w_a_out, w_b_out,
                    w_o, name, carries=()):
    s, d = dx1.shape
    n_head, bw, _ = wg.shape
    ts = min(TOKENS_MIXER_TILE, s)
    n_t = s // ts

    def body(dx1_ref, p_ref, conva_ref, xb2_ref, hh_ref, pa_ref, pb_ref, y_ref, r_ref, gi_ref, a_ref, mult_ref, dgel_ref,
             mod_ref, vec_ref, wg_ref, wa_ref, wb_ref, wo_ref,
             dp_ref, dab_ref, small_ref, dwg_ref,
             dconv_head, dxb2_head, a_head, g_head, a_buf, b_buf, c_buf):
        i = pl.program_id(0)

        @pl.when(i == 0)
        def _():
            small_ref[...] = jnp.zeros_like(small_ref)
            dwg_ref[...] = jnp.zeros_like(dwg_ref)
            dconv_head[...] = jnp.zeros_like(dconv_head)
            dxb2_head[...] = jnp.zeros_like(dxb2_head)
            a_head[...] = jnp.zeros_like(a_head)
            g_head[...] = jnp.zeros_like(g_head)

        def seg(k):
            return p_ref[:, k * d:(k + 1) * d].astype(F32)

        def vrow(k):
            return vec_ref[k:k + 1, :]

        def acc(row, val):
            small_ref[row:row + 1, :] += val

        dout = dx1_ref[...]
        yn, ry = _rms(y_ref[...].astype(F32))
        g_post = vrow(V_G_POST_MIX)
        gt = mod_ref[M_GT_M:M_GT_M + 1, :]
        acc(SB2_DGT, _colsum(dout * yn))
        dy = _rms_bwd(dout * (gt * g_post), yn, ry).astype(BF16)
        dab_ref[2] = dy
        dm = _dot_tb(dy, wo_ref[...])
        sa, sb = seg(5), seg(6)
        dpa = (dm * sa).astype(BF16)
        dpb = (dm * sb).astype(BF16)
        dab_ref[0] = dpa
        dab_ref[1] = dpb
        du_a = dm * pa_ref[...].astype(F32) * (sa * (1.0 - sa))
        du_b = dm * pb_ref[...].astype(F32) * (sb * (1.0 - sb))
        dp_ref[:, 5 * d:6 * d] = du_a.astype(BF16)
        dp_ref[:, 6 * d:7 * d] = du_b.astype(BF16)
        dy_a = _dot_tb(dpa, wa_ref[...])
        dy_b = _dot_tb(dpb, wb_ref[...])

        b_a, c_a, v_a = seg(0), seg(1), seg(2)
        dp_ref[:, 0:d] = (dy_a * conva_ref[...].astype(F32)).astype(BF16)
        dconv = dy_a * b_a
        nxt = dconv_head[...]
        d1 = _shift_up(dconv, 1, nxt)
        d2 = _shift_up(dconv, 2, nxt)
        dconv_head[...] = dconv[:SUBLANES]
        dcv = vrow(V_CONV_A_W + 2) * dconv + vrow(V_CONV_A_W + 1) * d1 + vrow(V_CONV_A_W) * d2
        cv = c_a * v_a
        acc(SB2_DWA + 2, _colsum(cv * dconv))
        acc(SB2_DWA + 1, _colsum(cv * d1))
        acc(SB2_DWA, _colsum(cv * d2))
        acc(SB2_DBA, _colsum(dconv))
        dp_ref[:, d:2 * d] = (dcv * v_a).astype(BF16)
        dp_ref[:, 2 * d:3 * d] = (dcv * c_a).astype(BF16)

        x_b, gel = seg(3), seg(4)
        hh = hh_ref[...]
        dp_ref[:, 4 * d:5 * d] = (dy_b * hh * dgel_ref[...].astype(F32)).astype(BF16)
        dhh = dy_b * gel
        xb2 = xb2_ref[...]
        r, gi, a, mult = r_ref[...], gi_ref[...], a_ref[...], mult_ref[...]
        sp = _softplus(-vrow(V_LAMBDA))
        a_next = _shift_up(a, 1, a_head[...])
        g = _scan_two_level(a_next, dhh, g_head[0:1, :], a_buf, b_buf, c_buf, reverse=True)
        a_head[...] = a[:SUBLANES]
        g_head[...] = g[:SUBLANES]
        gix = gi * xb2
        gm = g * mult
        dlog_a = g * (hh - mult * gix) - (g * gix) * (a * a / mult)
        dgi = gm * xb2
        dxb2 = gm * gi
        acc(SB2_DLAM, _colsum(dlog_a * r))
        dzr = dlog_a * ((-LRU_C) * sp) * (r * (1.0 - r))
        dzi = dgi * (gi * (1.0 - gi))
        acc(SB2_DBR, _colsum(dzr))
        acc(SB2_DBI, _colsum(dzi))
        xb2_b = xb2.astype(BF16)
        back = []
        for hd in range(n_head):
            cols = slice(hd * bw, (hd + 1) * bw)
            dz = jnp.concatenate([dzr[:, cols], dzi[:, cols]], axis=1).astype(BF16)
            back.append(_dot_tb(dz, wg_ref[hd]))
            dwg_ref[hd] += _dot_ta(xb2_b[:, cols], dz)
        dxb2 = dxb2 + jnp.concatenate(back, axis=1)
        nxt = dxb2_head[...]
        e1 = _shift_up(dxb2, 1, nxt)
        e2 = _shift_up(dxb2, 2, nxt)
        e3 = _shift_up(dxb2, 3, nxt)
        dxb2_head[...] = dxb2[:SUBLANES]
        dp_ref[:, 3 * d:4 * d] = (vrow(V_CONV_B_W + 3) * dxb2 + vrow(V_CONV_B_W + 2) * e1
                                  + vrow(V_CONV_B_W + 1) * e2 + vrow(V_CONV_B_W) * e3).astype(BF16)
        acc(SB2_DWB + 3, _colsum(x_b * dxb2))
        acc(SB2_DWB + 2, _colsum(x_b * e1))
        acc(SB2_DWB + 1, _colsum(x_b * e2))
        acc(SB2_DWB, _colsum(x_b * e3))
        acc(SB2_DBB, _colsum(dxb2))

        @pl.when(i == n_t - 1)
        def _():
            dgg_t = small_ref[SB2_DGT:SB2_DGT + 1, :]
            small_ref[SB2_DGT:SB2_DGT + 1, :] = dgg_t * g_post
            small_ref[SB2_DG_POST:SB2_DG_POST + 1, :] = dgg_t * gt
            lam = vrow(V_LAMBDA)
            small_ref[SB2_DLAM:SB2_DLAM + 1, :] = small_ref[SB2_DLAM:SB2_DLAM + 1, :] * (LRU_C * _sigmoid(-lam))

    rev = lambda i: (n_t - 1 - i, 0)
    tile = pl.BlockSpec((ts, d), rev)
    wide = pl.BlockSpec((ts, 7 * d), rev)
    sd = lambda dt: jax.ShapeDtypeStruct((s, d), dt)
    return _pcall(
        body, name=name, grid=(n_t,),
        out_shape=[jax.ShapeDtypeStruct((s, 7 * d), BF16), jax.ShapeDtypeStruct((3, s, d), BF16),
                   jax.ShapeDtypeStruct((2 * SUBLANES, d), F32), jax.ShapeDtypeStruct(wg.shape, F32)],
        in_specs=[tile, wide] + [tile] * 11 + [_full(mod.shape), _full(vec.shape),
                  _full(wg.shape), _full(w_a_out.shape), _full(w_b_out.shape), _full(w_o.shape)],
        out_specs=[wide, pl.BlockSpec((3, ts, d), lambda i: (0, n_t - 1 - i, 0)), _full((2 * SUBLANES, d)), _full(wg.shape)],
        scratch_shapes=[pltpu.VMEM((SUBLANES, d), F32)] * 4 + [pltpu.VMEM((d // LANES, ts, LANES), F32)] * 2
                       + [pltpu.VMEM((ts // SCAN_GROUP, d), F32)],
        args=[dx1, proj, conva, xb2s, hhs, pas, pbs, ys, rs_, gis, as_, mults, dgels, mod, vec, wg, w_a_out, w_b_out, w_o],
        carries=carries)


def _proj_backward(dproj, dx1, x, mod, vec, w_in, name, carries=()):
    s, d = x.shape
    nq = w_in.shape[2]
    ts = min(TOKENS_MATMUL_TILE, s)
    n_t = s // ts

    def body(dp_ref, dx1_ref, x_ref, mod_ref, vec_ref, w_ref, dx_ref, small_ref):
        i = pl.program_id(0)

        @pl.when(i == 0)
        def _():
            small_ref[...] = jnp.zeros_like(small_ref)

        dh = jnp.zeros((ts, d), F32)
        for qb in range(N_CHIP):
            dh = dh + _dot_tb(dp_ref[:, qb * nq:(qb + 1) * nq], w_ref[qb])
        xn, r = _rms(x_ref[...])
        g_pre = vec_ref[V_G_PRE_MIX:V_G_PRE_MIX + 1, :]
        sc1 = 1.0 + mod_ref[M_SC_M:M_SC_M + 1, :]
        dx_ref[...] = dx1_ref[...] + _rms_bwd(dh * (g_pre * sc1), xn, r)
        small_ref[SB1_DSH:SB1_DSH + 1, :] += _colsum(dh)
        small_ref[SB1_DSC:SB1_DSC + 1, :] += _colsum(dh * xn)

        @pl.when(i == n_t - 1)
        def _():
            dgm_t = small_ref[SB1_DSC:SB1_DSC + 1, :]
            small_ref[SB1_DSC:SB1_DSC + 1, :] = dgm_t * g_pre
            small_ref[SB1_DG_PRE:SB1_DG_PRE + 1, :] = dgm_t * sc1

    tile = pl.BlockSpec((ts, d), lambda i: (i, 0))
    return _pcall(
        body, name=name, grid=(n_t,),
        out_shape=[jax.ShapeDtypeStruct((s, d), F32), jax.ShapeDtypeStruct((SUBLANES, d), F32)],
        in_specs=[pl.BlockSpec((ts, N_CHIP * nq), lambda i: (i, 0)), tile, tile, _full(mod.shape), _full(vec.shape),
                  _full(w_in.shape)],
        out_specs=[tile, _full((SUBLANES, d))],
        args=[dproj, dx1, x, mod, vec, w_in], carries=carries)


def _weight_grad(a, b, name, col_blocks=1, tk=512, carries=(), square_relu=False):
    s, k = a.shape
    n = b.shape[1]
    tn = n // col_blocks
    tk = min(tk, k)

    def body(a_ref, b_ref, o_ref):
        av = a_ref[...]
        if square_relu:
            ru = jnp.maximum(av.astype(F32), 0.0)
            av = (ru * ru).astype(BF16)
        o_ref[0] = _dot_ta(av, b_ref[...])

    (out,), carried = _pcall(
        body, name=name, grid=(col_blocks, k // tk),
        out_shape=[jax.ShapeDtypeStruct((col_blocks, k, tn), F32)],
        in_specs=[pl.BlockSpec((s, tk), lambda j, i: (0, i)), pl.BlockSpec((s, tn), lambda j, i: (0, j))],
        out_specs=[pl.BlockSpec((1, tk, tn), lambda j, i: (j, i, 0))],
        args=[a, b], carries=carries)
    return out, carried


def _weight_grad_stacked(a3, b3, name, tk=512, carries=()):
    n_g, s, k = a3.shape
    n = b3.shape[2]
    kq = k // N_CHIP
    tk = min(tk, k)
    chips_per_tile = tk // kq

    def body(a_ref, b_ref, o_ref):
        o_ref[...] = _dot_ta(a_ref[...], b_ref[...]).reshape(chips_per_tile, kq, n)

    (out,), carried = _pcall(
        body, name=name, grid=(n_g, k // tk),
        out_shape=[jax.ShapeDtypeStruct((N_CHIP, n_g, kq, n), F32)],
        in_specs=[pl.BlockSpec((None, s, tk), lambda g, i: (g, 0, i)), pl.BlockSpec((None, s, n), lambda g, i: (g, 0, 0))],
        out_specs=[pl.BlockSpec((chips_per_tile, None, kq, n), lambda g, i: (i, g, 0, 0))],
        args=[a3, b3], carries=carries)
    return out.reshape(N_CHIP, n_g * kq, n), carried


def _adamw(items, name, copy_grad=False, carries=()):
    shape = items[0][0].shape
    cols = shape[-1]
    rows = items[0][0].size // cols
    tr = _row_tile(rows, cols, target_bytes=1024 * 1024 // len(items))
    c1 = 1.0 - ADAM_B1 ** ADAM_STEP
    c2 = 1.0 - ADAM_B2 ** ADAM_STEP
    n_out = 4 if copy_grad else 3
    n = len(items)

    def body(*refs):
        for k in range(n):
            w_ref, g_ref, m_ref, v_ref = refs[4 * k:4 * k + 4]
            outs = refs[4 * n + n_out * k:4 * n + n_out * (k + 1)]
            gv = g_ref[...]
            nm = ADAM_B1 * m_ref[...] + (1.0 - ADAM_B1) * gv
            nv = ADAM_B2 * v_ref[...] + (1.0 - ADAM_B2) * (gv * gv)
            outs[0][...] = (-ADAM_LR) * ((nm / c1) / (jnp.sqrt(nv / c2) + ADAM_EPS) + ADAM_WD * w_ref[...])
            outs[1][...] = nm
            outs[2][...] = nv
            if copy_grad:
                outs[3][...] = gv

    spec = pl.BlockSpec((tr, cols), lambda i: (i, 0))
    outs, carried = _pcall(
        body, name=name, grid=(rows // tr,),
        out_shape=[jax.ShapeDtypeStruct((rows, cols), F32)] * (n_out * n),
        in_specs=[spec] * (4 * n), out_specs=[spec] * (n_out * n),
        args=[t.reshape(rows, cols) for item in items for t in item], carries=carries)
    return [tuple(o.reshape(shape) for o in outs[n_out * k:n_out * (k + 1)]) for k in range(n)], carried


def kernel(x, c, w_mod, b_mod, g_pre_mix, g_post_mix, w_in, conv_a_w, conv_a_b, w_a_out, conv_b_w, conv_b_b, w_gate_r, b_gate_r, w_gate_i, b_gate_i, lru_lambda, w_b_out, w_o, g_pre_mlp, g_post_mlp, w_mlp_up, w_mlp_down, loss_target, m_w_mod, m_b_mod, m_g_pre_mix, m_g_post_mix, m_w_in, m_conv_a_w, m_conv_a_b, m_w_a_out, m_conv_b_w, m_conv_b_b, m_w_gate_r, m_b_gate_r, m_w_gate_i, m_b_gate_i, m_lru_lambda, m_w_b_out, m_w_o, m_g_pre_mlp, m_g_post_mlp, m_w_mlp_up, m_w_mlp_down, v_w_mod, v_b_mod, v_g_pre_mix, v_g_post_mix, v_w_in, v_conv_a_w, v_conv_a_b, v_w_a_out, v_conv_b_w, v_conv_b_b, v_w_gate_r, v_b_gate_r, v_w_gate_i, v_b_gate_i, v_lru_lambda, v_w_b_out, v_w_o, v_g_pre_mlp, v_g_post_mlp, v_w_mlp_up, v_w_mlp_down):
    weights = dict(w_mod=w_mod, b_mod=b_mod, g_pre_mix=g_pre_mix, g_post_mix=g_post_mix, w_in=w_in, conv_a_w=conv_a_w,
                   conv_a_b=conv_a_b, w_a_out=w_a_out, conv_b_w=conv_b_w, conv_b_b=conv_b_b, w_gate_r=w_gate_r,
                   b_gate_r=b_gate_r, w_gate_i=w_gate_i, b_gate_i=b_gate_i, lru_lambda=lru_lambda, w_b_out=w_b_out,
                   w_o=w_o, g_pre_mlp=g_pre_mlp, g_post_mlp=g_post_mlp, w_mlp_up=w_mlp_up, w_mlp_down=w_mlp_down)
    mom1 = dict(w_mod=m_w_mod, b_mod=m_b_mod, g_pre_mix=m_g_pre_mix, g_post_mix=m_g_post_mix, w_in=m_w_in,
                conv_a_w=m_conv_a_w, conv_a_b=m_conv_a_b, w_a_out=m_w_a_out, conv_b_w=m_conv_b_w, conv_b_b=m_conv_b_b,
                w_gate_r=m_w_gate_r, b_gate_r=m_b_gate_r, w_gate_i=m_w_gate_i, b_gate_i=m_b_gate_i,
                lru_lambda=m_lru_lambda, w_b_out=m_w_b_out, w_o=m_w_o, g_pre_mlp=m_g_pre_mlp, g_post_mlp=m_g_post_mlp,
                w_mlp_up=m_w_mlp_up, w_mlp_down=m_w_mlp_down)
    mom2 = dict(w_mod=v_w_mod, b_mod=v_b_mod, g_pre_mix=v_g_pre_mix, g_post_mix=v_g_post_mix, w_in=v_w_in,
                conv_a_w=v_conv_a_w, conv_a_b=v_conv_a_b, w_a_out=v_w_a_out, conv_b_w=v_conv_b_w, conv_b_b=v_conv_b_b,
                w_gate_r=v_w_gate_r, b_gate_r=v_b_gate_r, w_gate_i=v_w_gate_i, b_gate_i=v_b_gate_i,
                lru_lambda=v_lru_lambda, w_b_out=v_w_b_out, w_o=v_w_o, g_pre_mlp=v_g_pre_mlp, g_post_mlp=v_g_post_mlp,
                w_mlp_up=v_w_mlp_up, w_mlp_down=v_w_mlp_down)
    names = list(weights)

    n_layer = w_in.shape[0]
    s, d = x.shape[1], x.shape[2]
    n_head, bw = w_gate_r.shape[1], w_gate_r.shape[2]
    dq = d // N_CHIP
    mq = w_mod.shape[2]
    n_mod = (N_CHIP * mq) // d
    ka, kb = conv_a_w.shape[1], conv_b_w.shape[1]

    mx, my, mc = _place()
    q_me = 2 * mx + my
    q_arr = jnp.reshape(q_me, (1,)).astype(jnp.int32)

    me_dev = 4 * mx + 2 * my + mc
    me_arr = jnp.reshape(me_dev, (1,)).astype(jnp.int32)

    big_names = ["w_in", "w_a_out", "w_b_out", "w_o", "w_mlp_up", "w_mlp_down"]
    groups = [["w_in"], ["w_a_out", "w_b_out", "w_o"], ["w_mlp_up", "w_mlp_down"]]
    placed = {("w_in", 0): _cast_place_all([(w_in, 0)], q_arr, "cast_place_first")[0][0]}
    wfull = [dict() for _ in range(n_layer)]
    riders = {}
    for l in range(n_layer):
        riders.setdefault(3 * l - 1, []).append(([("w_in", l)], 0.9 if l else 1.0))
        riders.setdefault(3 * l - 2 if l else 0, []).append(([(nm, l) for nm in groups[1]], 0.9 if l else 0.5))
        riders.setdefault(3 * l, []).append(([("w_mlp_up", l)], 0.7 if l else 0.9))
        riders.setdefault(3 * l + 1, []).insert(0, ([("w_mlp_down", l)], 0.5))

    def gather_carry(call):
        return [_gather_carry([placed[k] for k in keys], frac) for keys, frac in riders.get(call, [])]

    def gathered(call, carried):
        for (keys, _), ws in zip(riders.get(call, []), carried):
            for (nm, l), w in zip(keys, ws):
                wfull[l][nm] = w.reshape(d, d) if nm in groups[1] else w

    n_conv_rows = n_layer * (ka + kb)
    conv_blk = -(-n_conv_rows // SUBLANES) * SUBLANES
    blk_rows = SUBLANES + conv_blk
    conv_rows = jnp.concatenate([jnp.concatenate([conv_a_w[l], conv_b_w[l]], axis=0) for l in range(n_layer)], axis=0)
    conv_rows = jnp.pad(conv_rows, ((0, conv_blk - n_conv_rows), (0, d - dq)))
    c_conv = jnp.concatenate([jnp.pad(c, ((0, SUBLANES - 1), (0, 0))), conv_rows], axis=0)
    rest = [(nm, l) for l in range(n_layer) for nm in big_names if (nm, l) != ("w_in", 0)]
    rest_placed, carried = _cast_place_all([(weights[nm], l) for nm, l in rest], q_arr, "cast_place_rest",
                                           carries=gather_carry(-1) + [_allgather_carry([c_conv])])
    placed.update(zip(rest, rest_placed))
    gathered(-1, carried[:1])
    gathered1 = lax.dynamic_update_slice(carried[1][0], c_conv, (me_dev * blk_rows, 0)).reshape(N_DEV, blk_rows, d)
    c_all = gathered1[:, 0, :]
    conv_full = jnp.concatenate([gathered1[2 * qb, SUBLANES:SUBLANES + n_conv_rows, :dq] for qb in range(N_CHIP)], axis=1)

    b_mod_shard = lax.dynamic_slice_in_dim(b_mod, q_me * mq, mq, axis=1)
    mod_part = _mod_forward(c_all, w_mod, b_mod_shard, "mod_forward")
    gathered2 = _all_gather_small(mod_part, "gather_mod").reshape(N_DEV, n_layer, N_DEV, mq)
    mod_rows = jnp.concatenate(
        [lax.dynamic_index_in_dim(gathered2[2 * qb], me_dev, axis=1, keepdims=False) for qb in range(N_CHIP)], axis=1)
    mods = [jnp.pad(mod_rows[l].reshape(n_mod, d), ((0, SUBLANES - n_mod), (0, 0))) for l in range(n_layer)]

    vecs = []
    for l in range(n_layer):
        base = l * (ka + kb)
        rows = [g_pre_mix[l], g_post_mix[l], conv_a_b[l], conv_b_b[l], b_gate_r[l], b_gate_i[l], lru_lambda[l],
                g_pre_mlp[l], g_post_mlp[l]]
        vecs.append(jnp.concatenate([jnp.stack(rows, axis=0), conv_full[base:base + ka + kb]], axis=0))

    wgs =[jnp.concatenate([w_gate_r[l], w_gate_i[l]], axis=-1).astype(BF16) for l in range(n_layer)]

    xs = x[0]
    saved = []
    for l in range(n_layer):
        wl = wfull[l]
        (h, proj, dgel), carried = _norm_proj(xs, mods[l], vecs[l], wl["w_in"], f"norm_proj_{l}", gather_carry(3 * l))
        gathered(3 * l, carried)
        (x1, conva, xb2, hh, abm, pa, pb, yy, gr, ggi, ga, gmult), carried = _mixer_forward(
            xs, proj, mods[l], vecs[l], wgs[l], wl["w_a_out"], wl["w_b_out"], wl["w_o"], f"mixer_forward_{l}",
            gather_carry(3 * l + 1))
        gathered(3 * l + 1, carried)
        (x2, h2, up, y2, *loss_tile), carried = _mlp_forward(
            x1, mods[l], vecs[l], wl["w_mlp_up"], wl["w_mlp_down"], f"mlp_forward_{l}", gather_carry(3 * l + 2),
            target=loss_target[0] if l == n_layer - 1 else None)
        gathered(3 * l + 2, carried)
        saved.append(dict(x=xs, h=h, proj=proj, x1=x1, conva=conva, xb2=xb2, hh=hh, abm=abm, pa=pa, pb=pb,
                          y=yy, r=gr, gi=ggi, a=ga, mult=gmult, dgel=dgel, h2=h2, up=up, y2=y2))
        xs = x2
    dxs = xs
    loss_block = jnp.pad(loss_tile[0], ((0, 0), (0, d - LANES)))

    chips_q = [q_me ^ 2, q_me ^ 1, q_me ^ 3]
    pf = jnp.stack([mc, q_me] + chips_q).astype(jnp.int32)
    rs = dict(grad={}, landed={}, to_send={}, from_chips={}, out={})
    to_exchange, to_scatter, to_join, to_gather = [], [], [], []
    small_own, small_all = {}, {}

    def ride(call, what, name=None):
        ex = list(to_exchange) if "x" in what else []
        sc = list(to_scatter) if "s" in what else []
        ga = list(to_gather) if "g" in what else []
        jn = []
        for key in (to_join if "j" in what else []):
            if key[0] not in [k[0] for k in jn]:
                jn.append(key)
        carries = []
        if ex:
            carries.append(_exchange_carry([rs["grad"][k] for k in ex]))
        if sc:
            carries.append(_scatter_carry([rs["to_send"][k] for k in sc]))
        if jn:
            carries.append(_join_carry([rs["out"][k[0]] for k in jn], [k[1] for k in jn]))
        if ga:
            carries.append(_allgather_carry([small_own[k] for k in ga]))
        if call is None:
            carried = _run_carries(carries, name) if carries else []
            res = None
        else:
            res, carried = call(carries)
        carried = list(carried)
        if ex:
            for k, ld in zip(ex, carried.pop(0)):
                to_exchange.remove(k)
                rs["landed"][k] = ld
                rs["to_send"][k] = _add_sibling_half(rs["grad"][k], ld, pf, f"rs_add_sibling_{k[0]}_{k[1]}")
                to_scatter.append(k)
        if sc:
            for k, fc in zip(sc, carried.pop(0)):
                to_scatter.remove(k)
                rs["out"][k[0]] = _add_chips(rs["grad"][k], rs["landed"][k], fc, pf, rs["out"].get(k[0]), k[1], n_layer,
                                             f"rs_add_chips_{k[0]}_{k[1]}")
                to_join.append(k)
        if jn:
            for k, o in zip(jn, carried.pop(0)):
                to_join.remove(k)
                rs["out"][k[0]] = o
        if ga:
            for k, o in zip(ga, carried.pop(0)):
                to_gather.remove(k)
                small_all[k] = o
        return res

    def gather_small(key, parts):
        small_own[key] = parts[0] if len(parts) == 1 else jnp.concatenate(parts, axis=0)
        to_gather.append(key)

    def ready(nm, l, g):
        rs["grad"][(nm, l)] = g
        to_exchange.append((nm, l))

    rowblk = lambda t: t.reshape(N_CHIP, t.shape[1] // N_CHIP, t.shape[2])
    small1_prev = None
    for l in reversed(range(n_layer)):
        wl, sv = wfull[l], saved[l]
        dx1, dy2, dup, small3 = ride(lambda cr: _mlp_backward(
            dxs, sv["x1"], sv["y2"], sv["up"], mods[l], vecs[l], wl["w_mlp_up"], wl["w_mlp_down"], f"mlp_backward_{l}", cr), "xsjg")
        ready("w_mlp_up", l, _weight_grad(sv["h2"], dup, f"grad_w_mlp_up_{l}", col_blocks=N_CHIP, tk=d)[0])
        g_down = ride(lambda cr: _weight_grad(sv["up"], dy2, f"grad_w_mlp_down_{l}", tk=d, carries=cr, square_relu=True), "x")
        ready("w_mlp_down", l, rowblk(g_down))
        dproj, dab, small2, dwg = ride(lambda cr: _mixer_backward(
            dx1, sv["proj"], sv["conva"], sv["xb2"], sv["hh"], sv["pa"], sv["pb"], sv["y"],
            sv["r"], sv["gi"], sv["a"], sv["mult"], sv["dgel"], mods[l], vecs[l], wgs[l],
            wl["w_a_out"], wl["w_b_out"], wl["w_o"], f"mixer_backward_{l}", cr), "xsjg")
        gather_small(("late", l, "s"), ([small1_prev] if small1_prev is not None else []) + [small2, small3])
        gather_small(("late", l, "w"), [dwg.reshape(2 * bw, d).astype(BF16)])
        g_in = ride(lambda cr: _weight_grad(sv["h"], dproj, f"grad_w_in_{l}", col_blocks=N_CHIP, carries=cr), "xsj")
        ready("w_in", l, g_in)
        g_abo = ride(lambda cr: _weight_grad_stacked(sv["abm"], dab, f"grad_w_abo_{l}", tk=d, carries=cr), "xg")
        ready("w_abo", l, g_abo)
        dxs, small1_prev = ride(lambda cr: _proj_backward(dproj, dx1, sv["x"], mods[l], vecs[l], wl["w_in"],
                                                          f"proj_backward_{l}", cr), "xsjg")
    grad_x = dxs[None]
    gather_small(("last", 0, "s"), [small1_prev, loss_block])

    tail = 0
    while to_exchange or to_scatter or to_join or to_gather:
        ride(None, "xsjg", f"rs_tail_{tail}")
        tail += 1
    grads, deltas, new_m, new_v = {}, {}, {}, {}

    def adam(nms, copy_grad=False):
        items = [(weights[nm], grads[nm], mom1[nm], mom2[nm]) for nm in nms]
        res, _ = _adamw(items, "adamw_" + "_".join(nms), copy_grad)
        for nm, r in zip(nms, res):
            deltas[nm], new_m[nm], new_v[nm] = r[:3]
            if copy_grad:
                grads[nm] = r[3]

    for nms in (["w_mlp_up", "w_mlp_down"], ["w_in"]):
        for nm in nms:
            grads[nm] = rs["out"][nm].reshape(weights[nm].shape)
        adam(nms, True)

    sums ={k: _sum_devices(small_all[k], small_own[k], me_arr, f"sum_small_{k[0]}_{k[1]}_{k[2]}") for k in small_own}

    loss = sums[("last", 0, "s")][SUBLANES, 0]
    small_full = {}

    def rows_of(l, part):
        if part == 0:
            return (("late", l - 1, "s"), 0) if l >= 1 else (("last", 0, "s"), 0)
        if part == 3:
            return ("late", l, "w"), 0
        base = SUBLANES if l < n_layer - 1 else 0
        return ("late", l, "s"), base + (0, 0, 2 * SUBLANES)[part]

    def summed(l, part, row, n_rows=1):
        key, base = rows_of(l, part)
        return sums[key][base + row:base + row + n_rows]

    def per_device(l, part, row):
        key, base = rows_of(l, part)
        own = small_own[key]
        if key not in small_full:
            small_full[key] = lax.dynamic_update_slice(small_all[key], own, (me_dev * own.shape[0], 0)).reshape(
                (N_DEV,) + own.shape)
        return small_full[key][:, base + row:base + row + 1]

    mod_rows = [(0, SB1_DSH), (0, SB1_DSC), (1, SB2_DGT), (2, SB3_DSH), (2, SB3_DSC), (2, SB3_DGT)]
    dmod_all = jnp.stack([jnp.concatenate([per_device(l, p, r)[:, 0, :] for p, r in mod_rows], axis=1)
                          for l in range(n_layer)], axis=0)
    o1, o2, o3, o4 = 0, SUBLANES, 3 * SUBLANES, 4 * SUBLANES
    small_sum = jnp.stack([jnp.concatenate([summed(l, 0, 0, SUBLANES), summed(l, 1, 0, 2 * SUBLANES),
                                            summed(l, 2, 0, SUBLANES), summed(l, 3, 0, 2 * bw)], axis=0)
                           for l in range(n_layer)], axis=0)
    mod_rows_of = [o1 + SB1_DSH, o1 + SB1_DSC, o2 + SB2_DGT, o3 + SB3_DSH, o3 + SB3_DSC, o3 + SB3_DGT]
    grads["w_mod"] = _mod_backward(c_all.T, lax.dynamic_slice_in_dim(dmod_all, q_me * mq, mq, axis=2), "mod_backward")
    grads["b_mod"] = jnp.concatenate([small_sum[:, k, :] for k in mod_rows_of], axis=1)
    grads["g_pre_mix"] = small_sum[:, o1 + SB1_DG_PRE]
    grads["g_post_mix"] = small_sum[:, o2 + SB2_DG_POST]
    grads["conv_a_w"] = lax.dynamic_slice_in_dim(small_sum[:, o2 + SB2_DWA:o2 + SB2_DWA + ka], q_me * dq, dq, axis=2)
    grads["conv_a_b"] = small_sum[:, o2 + SB2_DBA]
    grads["conv_b_w"] = lax.dynamic_slice_in_dim(small_sum[:, o2 + SB2_DWB:o2 + SB2_DWB + kb], q_me * dq, dq, axis=2)
    grads["conv_b_b"] = small_sum[:, o2 + SB2_DBB]
    grads["lru_lambda"] = small_sum[:, o2 + SB2_DLAM]
    grads["b_gate_r"] = small_sum[:, o2 + SB2_DBR]
    grads["b_gate_i"] = small_sum[:, o2 + SB2_DBI]
    grads["g_pre_mlp"] = small_sum[:, o3 + SB3_DG_PRE]
    grads["g_post_mlp"] = small_sum[:, o3 + SB3_DG_POST]
    dwg_sum = small_sum[:, o4:].reshape(n_layer, n_head, bw, 2 * bw)
    grads["w_gate_r"] = dwg_sum[..., :bw]
    grads["w_gate_i"] = dwg_sum[..., bw:]

    for k, nm in enumerate(groups[1]):
        grads[nm] = rs["out"]["w_abo"][:, k * dq:(k + 1) * dq]

    by_shape = {}
    for nm in names:
        if nm not in deltas:
            by_shape.setdefault(weights[nm].shape, []).append(nm)
    for nms in by_shape.values():
        adam(nms)
    return (loss, grad_x, *[grads[nm] for nm in names], *[deltas[nm] for nm in names],
            *[new_m[nm] for nm in names], *[new_v[nm] for nm in names])
```

```python
import jax
import jax.numpy as jnp
from jax import lax
from jax.experimental import pallas as pl
from jax.experimental.pallas import tpu as pltpu

F32 = jnp.float32
BF16 = jnp.bfloat16
MESH = pl.DeviceIdType.MESH

EPS = 1e-6
LRU_C = 8.0
N_CHIP = 4
N_DEV = 8
ADAM_LR = 0.001
ADAM_B1 = 0.9
ADAM_B2 = 0.999
ADAM_EPS = 1e-08
ADAM_WD = 0.01
ADAM_STEP = 10

VMEM_LIMIT_BYTES = 56 * 1024 * 1024
SUBLANES = 8
LANES = 128
TOKENS_MATMUL_TILE = 512
TOKENS_MIXER_TILE = 256
GELU_K0 = 0.7978845608028654
GELU_K1 = 0.044715

V_G_PRE_MIX, V_G_POST_MIX, V_CONV_A_B, V_CONV_B_B, V_B_GATE_R, V_B_GATE_I, V_LAMBDA, V_G_PRE_MLP, V_G_POST_MLP = range(9)
V_CONV_A_W = 9
V_CONV_B_W = 12
M_SH_M, M_SC_M, M_GT_M, M_SH_F, M_SC_F, M_GT_F = range(6)


def _cparams(n_grid=0):
    sem = ("arbitrary",) * n_grid if n_grid else None
    return pltpu.CompilerParams(dimension_semantics=sem, vmem_limit_bytes=VMEM_LIMIT_BYTES)


def _full(shape):
    return pl.BlockSpec(shape, lambda *_: (0,) * len(shape))


def _dot(a, b):
    return jnp.dot(a, b, preferred_element_type=F32)


def _dot_tb(a, b):
    return lax.dot_general(a, b, (((1,), (1,)), ((), ())), preferred_element_type=F32)


def _dot_ta(a, b):
    return lax.dot_general(a, b, (((0,), (0,)), ((), ())), preferred_element_type=F32)


def _sigmoid(x):
    return 1.0 / (1.0 + jnp.exp(-x))


def _softplus(x):
    return jnp.maximum(x, 0.0) + jnp.log1p(jnp.exp(-jnp.abs(x)))


def _neg_expm1(x):
    series = -x * (1.0 + 0.5 * x * (1.0 + (x / 3.0) * (1.0 + 0.25 * x)))
    return jnp.where(x > -1e-2, series, 1.0 - jnp.exp(x))


def _gelu_and_grad(x):
    x2 = x * x
    s = _sigmoid(x * (2.0 * GELU_K0 + (2.0 * GELU_K0 * GELU_K1) * x2))
    gel = x * s
    return gel, s + gel * (1.0 - s) * (2.0 * GELU_K0 + (6.0 * GELU_K0 * GELU_K1) * x2)


def _rms(x):
    r = lax.rsqrt(jnp.mean(x * x, axis=-1, keepdims=True) + EPS)
    return x * r, r


def _rms_bwd(dxn, xn, r):
    return r * (dxn - xn * jnp.mean(dxn * xn, axis=-1, keepdims=True))


def _colsum(x):
    return jnp.sum(x, axis=0, keepdims=True)


def _rows(t, w):
    return lax.broadcasted_iota(jnp.int32, (t, w), 0)


def _shift_down(x, k, prev8):
    t, w = x.shape
    rolled = pltpu.roll(x, k, 0)
    head = jnp.where(_rows(SUBLANES, w) < k, pltpu.roll(prev8, k, 0), rolled[:SUBLANES])
    return jnp.concatenate([head, rolled[SUBLANES:]], axis=0)


def _shift_up(x, k, next8):
    t, w = x.shape
    rolled = pltpu.roll(x, t - k, 0)
    tail = jnp.where(_rows(SUBLANES, w) >= SUBLANES - k, pltpu.roll(next8, SUBLANES - k, 0), rolled[t - SUBLANES:])
    return jnp.concatenate([rolled[:t - SUBLANES], tail], axis=0)


SCAN_GROUP = 16
FS_XB2, FS_HH, FS_R, FS_GI, FS_A, FS_MULT, FS_N = 0, 1, 2, 3, 4, 5, 6
BS_CONVA, BS_PA, BS_PB, BS_Y, BS_N = 0, 1, 2, 3, 4


def _scan_steps(a, b, group, reverse):
    t, w = a.shape
    pos = _rows(t, w) & (group - 1)
    s = 1
    while s < group:
        keep = (pos < group - s) if reverse else (pos >= s)
        shift = (t - s) if reverse else s
        b = b + a * jnp.where(keep, pltpu.roll(b, shift, 0), 0.0)
        a = a * jnp.where(keep, pltpu.roll(a, shift, 0), 1.0)
        s *= 2
    return b, a


def _scan_two_level(a, b, carry_row, a_buf, b_buf, c_buf, reverse):
    t, w = a.shape
    grp = SCAN_GROUP
    n_grp = t // grp
    h_loc, a_cum = _scan_steps(a, b, grp, reverse)
    end = 0 if reverse else grp - 1
    a_end, h_end = [], []
    for j in range(w // LANES):
        a_buf[j] = a_cum[:, j * LANES:(j + 1) * LANES]
        b_buf[j] = h_loc[:, j * LANES:(j + 1) * LANES]
        a_end.append(a_buf[j, pl.ds(end, n_grp, stride=grp), :])
        h_end.append(b_buf[j, pl.ds(end, n_grp, stride=grp), :])
    a_end = jnp.concatenate(a_end, axis=1)
    h_end = jnp.concatenate(h_end, axis=1)
    h_grp, a_grp = _scan_steps(a_end, h_end, n_grp, reverse)
    h_grp = h_grp + a_grp * carry_row
    rows = _rows(n_grp, w)
    if reverse:
        entering = jnp.where(rows == n_grp - 1, carry_row, pltpu.roll(h_grp, n_grp - 1, 0))
    else:
        entering = jnp.where(rows == 0, carry_row, pltpu.roll(h_grp, 1, 0))
    c_buf[...] = entering
    out = [h_loc[g * grp:(g + 1) * grp] + a_cum[g * grp:(g + 1) * grp] * c_buf[g:g + 1, :] for g in range(n_grp)]
    return jnp.concatenate(out, axis=0)


def _row_tile(rows, cols, itemsize=4, target_bytes=2 * 1024 * 1024):
    if rows * cols * itemsize <= target_bytes or rows % SUBLANES:
        return rows
    t = max(SUBLANES, (target_bytes // (cols * itemsize)) // SUBLANES * SUBLANES)
    while rows % t:
        t -= SUBLANES
    return t


def _place():
    return lax.axis_index("x"), lax.axis_index("y"), lax.axis_index("c")


def _other_chips(x, y):
    chips = [(1 - x, y), (x, 1 - y), (1 - x, 1 - y)]
    return chips, [2 * cx + cy for cx, cy in chips]


def _all_gather_small(block, name):
    m_per, n = block.shape

    def body(x_ref, out_ref, send_sems, recv_sems, local_sem):
        x, y, c = _place()
        me, sibling = (x, y, c), (x, y, 1 - c)
        chips, _ = _other_chips(x, y)

        def rows(px, py, pc):
            return out_ref.at[pl.ds((4 * px + 2 * py + pc) * m_per, m_per), :]

        def copy(k, blk, to, src=None):
            return pltpu.make_async_remote_copy(
                src_ref=rows(*blk) if src is None else src, dst_ref=rows(*blk),
                send_sem=send_sems.at[k], recv_sem=recv_sems.at[k], device_id=to, device_id_type=MESH)

        mine = pltpu.make_async_copy(x_ref, rows(*me), local_sem)
        mine.start()
        first = [copy(0, me, sibling, src=x_ref)]
        first += [copy(1 + j, me, (*chip, c), src=x_ref) for j, chip in enumerate(chips)]
        for cp in first:
            cp.start()
        passed = [copy(4 + j, (*chip, c), sibling) for j, chip in enumerate(chips)]
        for j, chip in enumerate(chips):
            copy(1 + j, (*chip, c), me).wait_recv()
            passed[j].start()
        copy(0, sibling, me).wait_recv()
        for j, chip in enumerate(chips):
            copy(4 + j, (*chip, 1 - c), me).wait_recv()
        for cp in first + passed:
            cp.wait_send()
        mine.wait()

    return pl.pallas_call(
        body, name=name,
        out_shape=jax.ShapeDtypeStruct((N_DEV * m_per, n), block.dtype),
        in_specs=[pl.BlockSpec(memory_space=pltpu.VMEM)],
        out_specs=pl.BlockSpec(memory_space=pltpu.VMEM),
        scratch_shapes=[pltpu.SemaphoreType.DMA((7,)), pltpu.SemaphoreType.DMA((7,)), pltpu.SemaphoreType.DMA],
        compiler_params=pltpu.CompilerParams(vmem_limit_bytes=VMEM_LIMIT_BYTES),
    )(block)


class _Carry:
    def __init__(self, ins, out_shapes, aliases, sem_shapes, start, finish, mid=None, mid_frac=0.85):
        self.ins, self.out_shapes, self.aliases, self.sem_shapes = list(ins), list(out_shapes), dict(aliases), list(sem_shapes)
        self.start, self.mid, self.finish, self.mid_frac = start, mid, finish, mid_frac


def _pcall(body, *, name, grid, in_specs, out_specs, out_shape, args, scratch_shapes=(), carries=(), prefetch=()):
    in_specs, out_specs, out_shape = list(in_specs), list(out_specs), list(out_shape)
    scratch_shapes, args = list(scratch_shapes), list(args)
    n_in, n_out, n_scr, n_pre = len(in_specs), len(out_shape), len(scratch_shapes), len(prefetch)
    steps = 1
    for g in grid:
        steps *= g
    any_spec = pl.BlockSpec(memory_space=pl.ANY)
    aliases = {}
    spans = []
    for cr in carries:
        spans.append((len(args), len(out_shape), len(scratch_shapes)))
        for a, b in cr.aliases.items():
            aliases[n_pre + len(args) + a] = len(out_shape) + b
        args += cr.ins
        in_specs += [any_spec] * len(cr.ins)
        out_shape += cr.out_shapes
        out_specs += [any_spec] * len(cr.out_shapes)
        scratch_shapes += cr.sem_shapes
    n_all_in = len(args)
    n_all_out = len(out_shape)

    def wrapped(*refs):
        pre, refs = refs[:n_pre], refs[n_pre:]
        ins, outs, scr = refs[:n_all_in], refs[n_all_in:n_all_in + n_all_out], refs[n_all_in + n_all_out:]
        parts = [(cr, ins[a:a + len(cr.ins)], outs[b:b + len(cr.out_shapes)], scr[s:s + len(cr.sem_shapes)])
                 for cr, (a, b, s) in zip(carries, spans)]
        lin = 0
        for ax, g in enumerate(grid):
            lin = lin * g + pl.program_id(ax)

        def at(step, fn):
            if steps == 1:
                fn()
            else:
                pl.when(lin == step)(fn)

        def start_all():
            for cr, ci, co, cs in parts:
                cr.start(ci, co, cs)

        def finish_all():
            for cr, ci, co, cs in parts:
                cr.finish(ci, co, cs)

        if parts:
            at(0, start_all)
        body(*pre, *ins[:n_in], *outs[:n_out], *scr[:n_scr])
        for cr, ci, co, cs in parts:
            if cr.mid is not None:
                at(min(steps - 1, int(steps * cr.mid_frac)), lambda cr=cr, ci=ci, co=co, cs=cs: cr.mid(ci, co, cs))
        if parts:
            at(steps - 1, finish_all)

    if n_pre:
        res = pl.pallas_call(
            wrapped, name=name, out_shape=out_shape,
            grid_spec=pltpu.PrefetchScalarGridSpec(num_scalar_prefetch=n_pre, grid=tuple(grid), in_specs=in_specs,
                                                   out_specs=out_specs, scratch_shapes=scratch_shapes),
            input_output_aliases=aliases, compiler_params=_cparams(len(grid)),
        )(*prefetch, *args)
    else:
        res = pl.pallas_call(
            wrapped, name=name, grid=tuple(grid), out_shape=out_shape, in_specs=in_specs, out_specs=out_specs,
            scratch_shapes=scratch_shapes, input_output_aliases=aliases, compiler_params=_cparams(len(grid)),
        )(*args)
    res = list(res)
    return res[:n_out], [res[b:b + len(cr.out_shapes)] for cr, (_, b, _) in zip(carries, spans)]


def _run_carries(carries, name):
    return _pcall(lambda: None, name=name, grid=(), in_specs=[], out_specs=[], out_shape=[], args=[], carries=carries)[1]


CAST_STEPS = 8


def _cast_place_all(shards, q_arr, name, carries=()):
    n = len(shards)

    def body(q_ref, *refs):
        for k in range(n):
            refs[n + k][...] = refs[k][...].astype(BF16)

    def spec_in(k):
        w, layer = shards[k]
        return pl.BlockSpec((1, w.shape[1] // CAST_STEPS, w.shape[2]), lambda i, q_ref: (layer, i, 0))

    def spec_out(k):
        w, _ = shards[k]
        return pl.BlockSpec((1, w.shape[1] // CAST_STEPS, w.shape[2]), lambda i, q_ref: (q_ref[0], i, 0))

    return _pcall(
        body, name=name, grid=(CAST_STEPS,),
        out_shape=[jax.ShapeDtypeStruct((N_CHIP,) + w.shape[1:], BF16) for w, _ in shards],
        in_specs=[spec_in(k) for k in range(n)], out_specs=[spec_out(k) for k in range(n)],
        args=[w for w, _ in shards], carries=carries, prefetch=[q_arr])


def _gather_carry(bufs, mid_frac=0.85):
    n = len(bufs)

    def copies(o_refs, sems):
        send_sems, recv_sems = sems
        x, y, c = _place()
        q = 2 * x + y
        sibling = (x, y, 1 - c)
        chips, qs = _other_chips(x, y)

        def half(w, shard, pc):
            rh = bufs[w].shape[1] // 2
            return o_refs[w].at[shard, pl.ds(pc * rh, rh), :]

        def over_ici(w, j, shard):
            blk = half(w, shard, c)
            return pltpu.make_async_remote_copy(
                src_ref=blk, dst_ref=blk, send_sem=send_sems.at[w, j], recv_sem=recv_sems.at[w, j],
                device_id=(*chips[j], c), device_id_type=MESH)

        def to_sibling(w, j, pc):
            blk = half(w, qs[j], pc)
            return pltpu.make_async_remote_copy(
                src_ref=blk, dst_ref=blk, send_sem=send_sems.at[w, 3 + j], recv_sem=recv_sems.at[w, 3 + j],
                device_id=sibling, device_id_type=MESH)

        return q, c, qs, over_ici, to_sibling

    pairs = [(w, j) for w in range(n) for j in range(3)]

    def start(i_refs, o_refs, sems):
        q, _, _, over_ici, _ = copies(o_refs, sems)
        for w, j in pairs:
            over_ici(w, j, q).start()

    def mid(i_refs, o_refs, sems):
        _, c, qs, over_ici, to_sibling = copies(o_refs, sems)
        for w, j in pairs:
            over_ici(w, j, qs[j]).wait_recv()
            to_sibling(w, j, c).start()

    def finish(i_refs, o_refs, sems):
        q, c, _, over_ici, to_sibling = copies(o_refs, sems)
        for w, j in pairs:
            to_sibling(w, j, 1 - c).wait_recv()
        for w, j in pairs:
            over_ici(w, j, q).wait_send()
            to_sibling(w, j, c).wait_send()

    return _Carry(bufs, [jax.ShapeDtypeStruct(b.shape, b.dtype) for b in bufs], {w: w for w in range(n)},
                  [pltpu.SemaphoreType.DMA((n, 6)), pltpu.SemaphoreType.DMA((n, 6))], start, finish, mid, mid_frac)


def _exchange_carry(grads):
    n = len(grads)

    def copies(g_refs, l_refs, sems):
        send_sems, recv_sems = sems
        x, y, c = _place()
        out = []
        for w in range(n):
            rh = grads[w].shape[1] // 2
            out.append(pltpu.make_async_remote_copy(
                src_ref=g_refs[w].at[:, pl.ds((1 - c) * rh, rh), :], dst_ref=l_refs[w],
                send_sem=send_sems.at[w], recv_sem=recv_sems.at[w], device_id=(x, y, 1 - c), device_id_type=MESH))
        return out

    def start(g_refs, l_refs, sems):
        for cp in copies(g_refs, l_refs, sems):
            cp.start()

    def finish(g_refs, l_refs, sems):
        for cp in copies(g_refs, l_refs, sems):
            cp.wait()

    return _Carry(grads, [jax.ShapeDtypeStruct((N_CHIP, g.shape[1] // 2, g.shape[2]), g.dtype) for g in grads], {},
                  [pltpu.SemaphoreType.DMA((n,)), pltpu.SemaphoreType.DMA((n,))], start, finish)


def _scatter_carry(sums):
    n = len(sums)

    def copies(s_refs, l_refs, sems):
        send_sems, recv_sems = sems
        x, y, c = _place()
        chips, _ = _other_chips(x, y)
        return [pltpu.make_async_remote_copy(
            src_ref=s_refs[w].at[j], dst_ref=l_refs[w].at[j], send_sem=send_sems.at[w, j], recv_sem=recv_sems.at[w, j],
            device_id=(*chips[j], c), device_id_type=MESH) for w in range(n) for j in range(3)]

    def start(s_refs, l_refs, sems):
        for cp in copies(s_refs, l_refs, sems):
            cp.start()

    def finish(s_refs, l_refs, sems):
        for cp in copies(s_refs, l_refs, sems):
            cp.wait()

    return _Carry(sums, [jax.ShapeDtypeStruct(s.shape, s.dtype) for s in sums], {},
                  [pltpu.SemaphoreType.DMA((n, 3)), pltpu.SemaphoreType.DMA((n, 3))], start, finish)


def _join_carry(outs, layers):
    n = len(outs)

    def copy(o_refs, sems, w, mine):
        send_sems, recv_sems = sems
        x, y, c = _place()
        r = outs[w].shape[1]
        rows = o_refs[w].at[layers[w], pl.ds((c if mine else 1 - c) * (r // 2), r // 2), :]
        return pltpu.make_async_remote_copy(
            src_ref=rows, dst_ref=rows, send_sem=send_sems.at[w], recv_sem=recv_sems.at[w],
            device_id=(x, y, 1 - c), device_id_type=MESH)

    def start(i_refs, o_refs, sems):
        for w in range(n):
            copy(o_refs, sems, w, True).start()

    def finish(i_refs, o_refs, sems):
        for w in range(n):
            copy(o_refs, sems, w, True).wait_send()
        for w in range(n):
            copy(o_refs, sems, w, False).wait_recv()

    return _Carry(outs, [jax.ShapeDtypeStruct(o.shape, o.dtype) for o in outs], {w: w for w in range(n)},
                  [pltpu.SemaphoreType.DMA((n,)), pltpu.SemaphoreType.DMA((n,))], start, finish)


PF_C, PF_Q, PF_QS = 0, 1, 2


def _add_sibling_half(g, landed, pf, name):
    _, r, cols = g.shape
    rh = r // 2
    tr = _row_tile(rh, cols)
    nr = rh // tr

    def body(pf_ref, g_ref, l_ref, o_ref):
        o_ref[...] = (g_ref[...] + l_ref[...]).astype(BF16)

    return pl.pallas_call(
        body, name=name,
        out_shape=jax.ShapeDtypeStruct((3, rh, cols), BF16),
        grid_spec=pltpu.PrefetchScalarGridSpec(
            num_scalar_prefetch=1, grid=(3, nr),
            in_specs=[pl.BlockSpec((1, tr, cols), lambda j, i, pf_ref: (pf_ref[PF_QS + j], pf_ref[PF_C] * nr + i, 0)),
                      pl.BlockSpec((1, tr, cols), lambda j, i, pf_ref: (pf_ref[PF_QS + j], i, 0))],
            out_specs=pl.BlockSpec((1, tr, cols), lambda j, i, pf_ref: (j, i, 0))),
        compiler_params=_cparams(2),
    )(pf, g, landed)


def _add_chips(g, landed, from_chips, pf, prev, layer, n_layer, name):
    _, r, cols = g.shape
    rh = r // 2
    tr = _row_tile(rh, cols)
    nr = rh // tr

    def body(pf_ref, g_ref, l_ref, f_ref, *rest):
        o_ref = rest[-1]
        acc = g_ref[0] + l_ref[0]
        for j in range(3):
            acc = acc + f_ref[j].astype(F32)
        o_ref[0] = acc

    in_specs = [pl.BlockSpec((1, tr, cols), lambda i, pf_ref: (pf_ref[PF_Q], pf_ref[PF_C] * nr + i, 0)),
                pl.BlockSpec((1, tr, cols), lambda i, pf_ref: (pf_ref[PF_Q], i, 0)),
                pl.BlockSpec((3, tr, cols), lambda i, pf_ref: (0, i, 0))]
    args = [pf, g, landed, from_chips]
    aliases = {}
    if prev is not None:
        in_specs.append(pl.BlockSpec(memory_space=pl.ANY))
        args.append(prev)
        aliases = {4: 0}
    return pl.pallas_call(
        body, name=name,
        out_shape=jax.ShapeDtypeStruct((n_layer, r, cols), F32),
        grid_spec=pltpu.PrefetchScalarGridSpec(
            num_scalar_prefetch=1, grid=(nr,), in_specs=in_specs,
            out_specs=pl.BlockSpec((1, tr, cols), lambda i, pf_ref: (layer, pf_ref[PF_C] * nr + i, 0))),
        input_output_aliases=aliases,
        compiler_params=_cparams(1),
    )(*args)


def _allgather_carry(blocks):
    n = len(blocks)

    def copies(b_refs, o_refs, sems):
        send_sems, recv_sems = sems
        x, y, c = _place()
        chips, _ = _other_chips(x, y)

        def place(w, px, py, pc):
            m = blocks[w].shape[0]
            return o_refs[w].at[pl.ds((4 * px + 2 * py + pc) * m, m), :]

        def own_to(w, k, to):
            dst = place(w, x, y, c)
            return pltpu.make_async_remote_copy(src_ref=b_refs[w], dst_ref=dst, send_sem=send_sems.at[w, k],
                                                recv_sem=recv_sems.at[w, k], device_id=to, device_id_type=MESH)

        def landed_from(w, k, px, py, pc):
            blk = place(w, px, py, pc)
            return pltpu.make_async_remote_copy(src_ref=blk, dst_ref=blk, send_sem=send_sems.at[w, k],
                                                recv_sem=recv_sems.at[w, k], device_id=(x, y, 1 - c), device_id_type=MESH)

        return x, y, c, chips, own_to, landed_from

    def start(b_refs, o_refs, sems):
        x, y, c, chips, own_to, _ = copies(b_refs, o_refs, sems)
        for w in range(n):
            own_to(w, 0, (x, y, 1 - c)).start()
            for j, chip in enumerate(chips):
                own_to(w, 1 + j, (*chip, c)).start()

    def mid(b_refs, o_refs, sems):
        x, y, c, chips, _, landed_from = copies(b_refs, o_refs, sems)
        for w in range(n):
            for j, chip in enumerate(chips):
                landed_from(w, 1 + j, *chip, c).wait_recv()
                landed_from(w, 4 + j, *chip, c).start()

    def finish(b_refs, o_refs, sems):
        x, y, c, chips, own_to, landed_from = copies(b_refs, o_refs, sems)
        for w in range(n):
            landed_from(w, 0, x, y, 1 - c).wait_recv()
            for j, chip in enumerate(chips):
                landed_from(w, 4 + j, *chip, 1 - c).wait_recv()
            own_to(w, 0, (x, y, 1 - c)).wait_send()
            for j, chip in enumerate(chips):
                own_to(w, 1 + j, (*chip, c)).wait_send()
                landed_from(w, 4 + j, *chip, c).wait_send()

    return _Carry(blocks, [jax.ShapeDtypeStruct((N_DEV * b.shape[0], b.shape[1]), b.dtype) for b in blocks], {},
                  [pltpu.SemaphoreType.DMA((n, 7)), pltpu.SemaphoreType.DMA((n, 7))], start, finish, mid)


def _sum_devices(gathered, own, me_arr, name):
    m, n = own.shape
    tr = _row_tile(m, n, itemsize=own.dtype.itemsize, target_bytes=256 * 1024)
    nr = m // tr

    def body(me_ref, *refs):
        g_refs, own_ref, o_ref = refs[:N_DEV], refs[N_DEV], refs[N_DEV + 1]
        me = me_ref[0]
        acc = None
        for dev in range(N_DEV):
            term = jnp.where(me == dev, own_ref[...], g_refs[dev][...]).astype(F32)
            acc = term if acc is None else acc + term
        o_ref[...] = acc

    def dev_rows(dev):
        return pl.BlockSpec((tr, n), lambda i, me_ref: (dev * nr + i, 0))

    return pl.pallas_call(
        body, name=name,
        out_shape=jax.ShapeDtypeStruct((m, n), F32),
        grid_spec=pltpu.PrefetchScalarGridSpec(
            num_scalar_prefetch=1, grid=(nr,),
            in_specs=[dev_rows(dev) for dev in range(N_DEV)] + [pl.BlockSpec((tr, n), lambda i, me_ref: (i, 0))],
            out_specs=pl.BlockSpec((tr, n), lambda i, me_ref: (i, 0))),
        compiler_params=_cparams(1),
    )(me_arr, *([gathered] * N_DEV), own)


def _mod_forward(c_all, w_mod, b_mod_shard, name):
    n_layer, d, mq = w_mod.shape

    def body(c_ref, w_ref, b_ref, o_ref):
        cv = c_ref[...]
        o_ref[...] = _dot(cv * _sigmoid(cv), w_ref[0]) + b_ref[0]

    return pl.pallas_call(
        body, name=name, grid=(n_layer,),
        out_shape=jax.ShapeDtypeStruct((n_layer * N_DEV, mq), F32),
        in_specs=[_full((N_DEV, d)), pl.BlockSpec((1, d, mq), lambda l: (l, 0, 0)),
                  pl.BlockSpec((1, 1, mq), lambda l: (l, 0, 0))],
        out_specs=pl.BlockSpec((N_DEV, mq), lambda l: (l, 0)),
        compiler_params=_cparams(1),
    )(c_all, w_mod, b_mod_shard.reshape(n_layer, 1, mq))


def _mod_backward(c_all_t, dmod_shard, name):
    n_layer, _, mq = dmod_shard.shape
    d = c_all_t.shape[0]

    def body(c_ref, dm_ref, o_ref):
        cv = c_ref[...]
        o_ref[0] = _dot(cv * _sigmoid(cv), dm_ref[0])

    return pl.pallas_call(
        body, name=name, grid=(n_layer,),
        out_shape=jax.ShapeDtypeStruct((n_layer, d, mq), F32),
        in_specs=[_full((d, N_DEV)), pl.BlockSpec((1, N_DEV, mq), lambda l: (l, 0, 0))],
        out_specs=pl.BlockSpec((1, d, mq), lambda l: (l, 0, 0)),
        compiler_params=_cparams(1),
    )(c_all_t, dmod_shard)


def _norm_proj(x, mod, vec, w_in, name, carries=()):
    s, d = x.shape
    nq = w_in.shape[2]
    ts = min(TOKENS_MATMUL_TILE, s)

    def body(x_ref, mod_ref, vec_ref, w_ref, h_ref, p_ref, dgel_ref):
        xn, _ = _rms(x_ref[...])
        gm = vec_ref[V_G_PRE_MIX:V_G_PRE_MIX + 1, :] * (1.0 + mod_ref[M_SC_M:M_SC_M + 1, :])
        h = (xn * gm + mod_ref[M_SH_M:M_SH_M + 1, :]).astype(BF16)
        h_ref[...] = h
        for qb in range(N_CHIP):
            pq = _dot(h, w_ref[qb])
            for k in range(N_CHIP * nq // d):
                lo, hi = max(qb * nq, k * d), min((qb + 1) * nq, (k + 1) * d)
                if lo >= hi:
                    continue
                piece = pq[:, lo - qb * nq:hi - qb * nq]
                if k == 4:
                    piece, dgel = _gelu_and_grad(piece)
                    dgel_ref[:, lo - 4 * d:hi - 4 * d] = dgel.astype(BF16)
                elif k >= 5:
                    piece = _sigmoid(piece)
                p_ref[:, lo:hi] = piece.astype(BF16)

    tile = pl.BlockSpec((ts, d), lambda i: (i, 0))
    return _pcall(
        body, name=name, grid=(s // ts,),
        out_shape=[jax.ShapeDtypeStruct((s, d), BF16), jax.ShapeDtypeStruct((s, N_CHIP * nq), BF16),
                   jax.ShapeDtypeStruct((s, d), BF16)],
        in_specs=[tile, _full(mod.shape), _full(vec.shape), _full(w_in.shape)],
        out_specs=[tile, pl.BlockSpec((ts, N_CHIP * nq), lambda i: (i, 0)), tile],
        args=[x, mod, vec, w_in], carries=carries)


def _gate_pre(xb2_b, wg_ref, n_head, bw):
    zr, zi = [], []
    for hd in range(n_head):
        z = _dot(xb2_b[:, hd * bw:(hd + 1) * bw], wg_ref[hd])
        zr.append(z[:, :bw])
        zi.append(z[:, bw:])
    return jnp.concatenate(zr, axis=1), jnp.concatenate(zi, axis=1)


def _lru_coeffs(xb2, wg_ref, vec_ref, n_head, bw):
    zr, zi = _gate_pre(xb2.astype(BF16), wg_ref, n_head, bw)
    r = _sigmoid(zr + vec_ref[V_B_GATE_R:V_B_GATE_R + 1, :])
    gi = _sigmoid(zi + vec_ref[V_B_GATE_I:V_B_GATE_I + 1, :])
    sp = _softplus(-vec_ref[V_LAMBDA:V_LAMBDA + 1, :])
    log_a = (-LRU_C) * r * sp
    a = jnp.exp(log_a)
    mult = jnp.sqrt(_neg_expm1(2.0 * log_a))
    return r, gi, sp, a, mult


def _mixer_forward(x, proj, mod, vec, wg, w_a_out, w_b_out, w_o, name, carries=()):
    s, d = x.shape
    n_head, bw, _ = wg.shape
    ts = min(TOKENS_MIXER_TILE, s)

    def body(x_ref, p_ref, mod_ref, vec_ref, wg_ref, wa_ref, wb_ref, wo_ref,
             x1_ref, fs_ref, abm_ref, bs_ref,
             cv_tail, xb_tail, h_last, a_buf, b_buf, c_buf):
        i = pl.program_id(0)

        @pl.when(i == 0)
        def _():
            cv_tail[...] = jnp.zeros_like(cv_tail)
            xb_tail[...] = jnp.zeros_like(xb_tail)
            h_last[...] = jnp.zeros_like(h_last)

        def seg(k):
            return p_ref[:, k * d:(k + 1) * d].astype(F32)

        def vrow(k):
            return vec_ref[k:k + 1, :]

        b_a, c_a, v_a, x_b, gel, sa, sb = (seg(k) for k in range(7))
        cv = c_a * v_a
        prev_cv = cv_tail[...]
        conv_a = (vrow(V_CONV_A_B) + vrow(V_CONV_A_W) * _shift_down(cv, 2, prev_cv)
                  + vrow(V_CONV_A_W + 1) * _shift_down(cv, 1, prev_cv) + vrow(V_CONV_A_W + 2) * cv)
        cv_tail[...] = cv[ts - SUBLANES:]
        y_a = b_a * conv_a
        prev_xb = xb_tail[...]
        xb2 = (vrow(V_CONV_B_B) + vrow(V_CONV_B_W) * _shift_down(x_b, 3, prev_xb)
               + vrow(V_CONV_B_W + 1) * _shift_down(x_b, 2, prev_xb)
               + vrow(V_CONV_B_W + 2) * _shift_down(x_b, 1, prev_xb) + vrow(V_CONV_B_W + 3) * x_b)
        xb_tail[...] = x_b[ts - SUBLANES:]
        r, gi, _, a, mult = _lru_coeffs(xb2, wg_ref, vec_ref, n_head, bw)
        fs_ref[FS_R] = r
        fs_ref[FS_GI] = gi
        fs_ref[FS_A] = a
        fs_ref[FS_MULT] = mult
        hh = _scan_two_level(a, mult * gi * xb2, h_last[SUBLANES - 1:SUBLANES, :], a_buf, b_buf, c_buf, reverse=False)
        h_last[...] = hh[ts - SUBLANES:]
        y_b = hh * gel
        ya_b, yb_b = y_a.astype(BF16), y_b.astype(BF16)
        pa = _dot(ya_b, wa_ref[...])
        pb = _dot(yb_b, wb_ref[...])
        m = (sa * pa + sb * pb).astype(BF16)
        y = _dot(m, wo_ref[...])
        yn, _ = _rms(y)
        gg = mod_ref[M_GT_M:M_GT_M + 1, :] * vrow(V_G_POST_MIX)
        x1_ref[...] = x_ref[...] + yn * gg
        bs_ref[BS_CONVA] = conv_a.astype(BF16)
        fs_ref[FS_XB2] = xb2
        fs_ref[FS_HH] = hh
        abm_ref[0] = ya_b
        abm_ref[1] = yb_b
        abm_ref[2] = m
        bs_ref[BS_PA] = pa.astype(BF16)
        bs_ref[BS_PB] = pb.astype(BF16)
        bs_ref[BS_Y] = y.astype(BF16)

    tile = pl.BlockSpec((ts, d), lambda i: (i, 0))
    stack = lambda n: pl.BlockSpec((n, ts, d), lambda i: (0, i, 0))
    return _pcall(
        body, name=name, grid=(s // ts,),
        out_shape=[jax.ShapeDtypeStruct((s, d), F32), jax.ShapeDtypeStruct((FS_N, s, d), F32),
                   jax.ShapeDtypeStruct((3, s, d), BF16), jax.ShapeDtypeStruct((BS_N, s, d), BF16)],
        in_specs=[tile, pl.BlockSpec((ts, 7 * d), lambda i: (i, 0)), _full(mod.shape), _full(vec.shape),
                  _full(wg.shape), _full(w_a_out.shape), _full(w_b_out.shape), _full(w_o.shape)],
        out_specs=[tile, stack(FS_N), stack(3), stack(BS_N)],
        scratch_shapes=[pltpu.VMEM((SUBLANES, d), F32)] * 3 + [pltpu.VMEM((d // LANES, ts, LANES), F32)] * 2
                       + [pltpu.VMEM((ts // SCAN_GROUP, d), F32)],
        args=[x, proj, mod, vec, wg, w_a_out, w_b_out, w_o], carries=carries)


def _mlp_forward(x1, mod, vec, w_up, w_down, name, carries=(), target=None):
    s, d = x1.shape
    fq = w_up.shape[2]
    ts = min(TOKENS_MATMUL_TILE, s)

    def body(x_ref, *refs):
        if target is None:
            mod_ref, vec_ref, wu_ref, wd_ref, x2_ref, h2_ref, up_ref, y2_ref = refs
        else:
            t_ref, mod_ref, vec_ref, wu_ref, wd_ref, x2_ref, h2_ref, up_ref, y2_ref, loss_ref = refs
        x = x_ref[...]
        xn, _ = _rms(x)
        gm = vec_ref[V_G_PRE_MLP:V_G_PRE_MLP + 1, :] * (1.0 + mod_ref[M_SC_F:M_SC_F + 1, :])
        h2 = (xn * gm + mod_ref[M_SH_F:M_SH_F + 1, :]).astype(BF16)
        h2_ref[...] = h2
        y2 = jnp.zeros((ts, d), F32)
        for qb in range(N_CHIP):
            up = _dot(h2, wu_ref[qb])
            up_ref[:, qb * fq:(qb + 1) * fq] = up.astype(BF16)
            ru = jnp.maximum(up, 0.0)
            y2 = y2 + _dot((ru * ru).astype(BF16), wd_ref[qb])
        y2_ref[...] = y2.astype(BF16)
        yn, _ = _rms(y2)
        gg = mod_ref[M_GT_F:M_GT_F + 1, :] * vec_ref[V_G_POST_MLP:V_G_POST_MLP + 1, :]
        x2 = x + yn * gg
        if target is None:
            x2_ref[...] = x2
        else:
            @pl.when(pl.program_id(0) == 0)
            def _():
                loss_ref[...] = jnp.zeros_like(loss_ref)

            err = x2 - t_ref[...]
            x2_ref[...] = err * (1.0 / d)
            loss_ref[...] += jnp.sum(jnp.sum(err * err, axis=1, keepdims=True), axis=0, keepdims=True) * (0.5 / d)

    tile = pl.BlockSpec((ts, d), lambda i: (i, 0))
    last = target is not None
    return _pcall(
        body, name=name, grid=(s // ts,),
        out_shape=[jax.ShapeDtypeStruct((s, d), F32), jax.ShapeDtypeStruct((s, d), BF16),
                   jax.ShapeDtypeStruct((s, N_CHIP * fq), BF16), jax.ShapeDtypeStruct((s, d), BF16)]
                  + ([jax.ShapeDtypeStruct((SUBLANES, LANES), F32)] if last else []),
        in_specs=[tile] + ([tile] if last else []) + [_full(mod.shape), _full(vec.shape), _full(w_up.shape), _full(w_down.shape)],
        out_specs=[tile, tile, pl.BlockSpec((ts, N_CHIP * fq), lambda i: (i, 0)), tile]
                 + ([_full((SUBLANES, LANES))] if last else []),
        args=[x1] + ([target] if last else []) + [mod, vec, w_up, w_down], carries=carries)


SB3_DSH, SB3_DSC, SB3_DGT, SB3_DG_PRE, SB3_DG_POST = range(5)
SB1_DSH, SB1_DSC, SB1_DG_PRE = range(3)
(SB2_DGT, SB2_DG_POST, SB2_DWA, SB2_DBA, SB2_DWB, SB2_DBB, SB2_DLAM, SB2_DBR, SB2_DBI) = (0, 1, 2, 5, 6, 10, 11, 12, 13)


def _mlp_backward(dx2, x1, y2, up, mod, vec, w_up, w_down, name, carries=()):
    s, d = dx2.shape
    fq = w_up.shape[2]
    ts = min(TOKENS_MATMUL_TILE, s)
    n_t = s // ts

    def body(dx2_ref, x_ref, y2_ref, up_ref, mod_ref, vec_ref, wu_ref, wd_ref,
             dx1_ref, dy2_ref, dup_ref, small_ref):
        i = pl.program_id(0)

        @pl.when(i == 0)
        def _():
            small_ref[...] = jnp.zeros_like(small_ref)

        dout = dx2_ref[...]
        y2n, ry = _rms(y2_ref[...].astype(F32))
        g_post = vec_ref[V_G_POST_MLP:V_G_POST_MLP + 1, :]
        gt = mod_ref[M_GT_F:M_GT_F + 1, :]
        dgg = _colsum(dout * y2n)
        dy2 = _rms_bwd(dout * (gt * g_post), y2n, ry).astype(BF16)
        dy2_ref[...] = dy2
        dh2 = jnp.zeros((ts, d), F32)
        for qb in range(N_CHIP):
            cols = slice(qb * fq, (qb + 1) * fq)
            dact = _dot_tb(dy2, wd_ref[qb])
            ru = jnp.maximum(up_ref[:, cols].astype(F32), 0.0)
            dup = (dact * (2.0 * ru)).astype(BF16)
            dup_ref[:, cols] = dup
            dh2 = dh2 + _dot_tb(dup, wu_ref[qb])
        xn, r = _rms(x_ref[...])
        g_pre = vec_ref[V_G_PRE_MLP:V_G_PRE_MLP + 1, :]
        sc1 = 1.0 + mod_ref[M_SC_F:M_SC_F + 1, :]
        dsh = _colsum(dh2)
        dgm = _colsum(dh2 * xn)
        dx1_ref[...] = dout + _rms_bwd(dh2 * (g_pre * sc1), xn, r)
        small_ref[SB3_DSH:SB3_DSH + 1, :] += dsh
        small_ref[SB3_DSC:SB3_DSC + 1, :] += dgm
        small_ref[SB3_DGT:SB3_DGT + 1, :] += dgg

        @pl.when(i == n_t - 1)
        def _():
            dgm_t = small_ref[SB3_DSC:SB3_DSC + 1, :]
            dgg_t = small_ref[SB3_DGT:SB3_DGT + 1, :]
            small_ref[SB3_DSC:SB3_DSC + 1, :] = dgm_t * g_pre
            small_ref[SB3_DG_PRE:SB3_DG_PRE + 1, :] = dgm_t * sc1
            small_ref[SB3_DGT:SB3_DGT + 1, :] = dgg_t * g_post
            small_ref[SB3_DG_POST:SB3_DG_POST + 1, :] = dgg_t * gt

    tile = pl.BlockSpec((ts, d), lambda i: (i, 0))
    wide = pl.BlockSpec((ts, N_CHIP * fq), lambda i: (i, 0))
    return _pcall(
        body, name=name, grid=(n_t,),
        out_shape=[jax.ShapeDtypeStruct((s, d), F32), jax.ShapeDtypeStruct((s, d), BF16),
                   jax.ShapeDtypeStruct((s, N_CHIP * fq), BF16), jax.ShapeDtypeStruct((SUBLANES, d), F32)],
        in_specs=[tile, tile, tile, wide, _full(mod.shape), _full(vec.shape), _full(w_up.shape), _full(w_down.shape)],
        out_specs=[tile, tile, wide, _full((SUBLANES, d))],
        args=[dx2, x1, y2, up, mod, vec, w_up, w_down], carries=carries)


def _mixer_backward(dx1, proj, fstack, bstack, dgels, mod, vec, wg, w_a_out, w_b_out, w_o, name, carries=()):
    s, d = dx1.shape
    n_head, bw, _ = wg.shape
    ts = min(TOKENS_MIXER_TILE, s)
    n_t = s // ts

    def body(dx1_ref, p_ref, fs_ref, bs_ref, dgel_ref,
             mod_ref, vec_ref, wg_ref, wa_ref, wb_ref, wo_ref,
             dp_ref, dab_ref, small_ref, dwg_ref,
             dconv_head, dxb2_head, a_head, g_head, a_buf, b_buf, c_buf):
        i = pl.program_id(0)

        @pl.when(i == 0)
        def _():
            small_ref[...] = jnp.zeros_like(small_ref)
            dwg_ref[...] = jnp.zeros_like(dwg_ref)
            dconv_head[...] = jnp.zeros_like(dconv_head)
            dxb2_head[...] = jnp.zeros_like(dxb2_head)
            a_head[...] = jnp.zeros_like(a_head)
            g_head[...] = jnp.zeros_like(g_head)

        def seg(k):
            return p_ref[:, k * d:(k + 1) * d].astype(F32)

        def vrow(k):
            return vec_ref[k:k + 1, :]

        def acc(row, val):
            small_ref[row:row + 1, :] += val

        dout = dx1_ref[...]
        yn, ry = _rms(bs_ref[BS_Y].astype(F32))
        g_post = vrow(V_G_POST_MIX)
        gt = mod_ref[M_GT_M:M_GT_M + 1, :]
        acc(SB2_DGT, _colsum(dout * yn))
        dy = _rms_bwd(dout * (gt * g_post), yn, ry).astype(BF16)
        dab_ref[2] = dy
        dm = _dot_tb(dy, wo_ref[...])
        sa, sb = seg(5), seg(6)
        dpa = (dm * sa).astype(BF16)
        dpb = (dm * sb).astype(BF16)
        dab_ref[0] = dpa
        dab_ref[1] = dpb
        du_a = dm * bs_ref[BS_PA].astype(F32) * (sa * (1.0 - sa))
        du_b = dm * bs_ref[BS_PB].astype(F32) * (sb * (1.0 - sb))
        dp_ref[:, 5 * d:6 * d] = du_a.astype(BF16)
        dp_ref[:, 6 * d:7 * d] = du_b.astype(BF16)
        dy_a = _dot_tb(dpa, wa_ref[...])
        dy_b = _dot_tb(dpb, wb_ref[...])

        b_a, c_a, v_a = seg(0), seg(1), seg(2)
        dp_ref[:, 0:d] = (dy_a * bs_ref[BS_CONVA].astype(F32)).astype(BF16)
        dconv = dy_a * b_a
        nxt = dconv_head[...]
        d1 = _shift_up(dconv, 1, nxt)
        d2 = _shift_up(dconv, 2, nxt)
        dconv_head[...] = dconv[:SUBLANES]
        dcv = vrow(V_CONV_A_W + 2) * dconv + vrow(V_CONV_A_W + 1) * d1 + vrow(V_CONV_A_W) * d2
        cv = c_a * v_a
        acc(SB2_DWA + 2, _colsum(cv * dconv))
        acc(SB2_DWA + 1, _colsum(cv * d1))
        acc(SB2_DWA, _colsum(cv * d2))
        acc(SB2_DBA, _colsum(dconv))
        dp_ref[:, d:2 * d] = (dcv * v_a).astype(BF16)
        dp_ref[:, 2 * d:3 * d] = (dcv * c_a).astype(BF16)

        x_b, gel = seg(3), seg(4)
        hh = fs_ref[FS_HH]
        dp_ref[:, 4 * d:5 * d] = (dy_b * hh * dgel_ref[...].astype(F32)).astype(BF16)
        dhh = dy_b * gel
        xb2 = fs_ref[FS_XB2]
        r, gi, a, mult = fs_ref[FS_R], fs_ref[FS_GI], fs_ref[FS_A], fs_ref[FS_MULT]
        sp = _softplus(-vrow(V_LAMBDA))
        a_next = _shift_up(a, 1, a_head[...])
        g = _scan_two_level(a_next, dhh, g_head[0:1, :], a_buf, b_buf, c_buf, reverse=True)
        a_head[...] = a[:SUBLANES]
        g_head[...] = g[:SUBLANES]
        gix = gi * xb2
        gm = g * mult
        dlog_a = g * (hh - mult * gix) - (g * gix) * (a * a / mult)
        dgi = gm * xb2
        dxb2 = gm * gi
        acc(SB2_DLAM, _colsum(dlog_a * r))
        dzr = dlog_a * ((-LRU_C) * sp) * (r * (1.0 - r))
        dzi = dgi * (gi * (1.0 - gi))
        acc(SB2_DBR, _colsum(dzr))
        acc(SB2_DBI, _colsum(dzi))
        xb2_b = xb2.astype(BF16)
        back = []
        for hd in range(n_head):
            cols = slice(hd * bw, (hd + 1) * bw)
            dz = jnp.concatenate([dzr[:, cols], dzi[:, cols]], axis=1).astype(BF16)
            back.append(_dot_tb(dz, wg_ref[hd]))
            dwg_ref[hd] += _dot_ta(xb2_b[:, cols], dz)
        dxb2 = dxb2 + jnp.concatenate(back, axis=1)
        nxt = dxb2_head[...]
        e1 = _shift_up(dxb2, 1, nxt)
        e2 = _shift_up(dxb2, 2, nxt)
        e3 = _shift_up(dxb2, 3, nxt)
        dxb2_head[...] = dxb2[:SUBLANES]
        dp_ref[:, 3 * d:4 * d] = (vrow(V_CONV_B_W + 3) * dxb2 + vrow(V_CONV_B_W + 2) * e1
                                  + vrow(V_CONV_B_W + 1) * e2 + vrow(V_CONV_B_W) * e3).astype(BF16)
        acc(SB2_DWB + 3, _colsum(x_b * dxb2))
        acc(SB2_DWB + 2, _colsum(x_b * e1))
        acc(SB2_DWB + 1, _colsum(x_b * e2))
        acc(SB2_DWB, _colsum(x_b * e3))
        acc(SB2_DBB, _colsum(dxb2))

        @pl.when(i == n_t - 1)
        def _():
            dgg_t = small_ref[SB2_DGT:SB2_DGT + 1, :]
            small_ref[SB2_DGT:SB2_DGT + 1, :] = dgg_t * g_post
            small_ref[SB2_DG_POST:SB2_DG_POST + 1, :] = dgg_t * gt
            lam = vrow(V_LAMBDA)
            small_ref[SB2_DLAM:SB2_DLAM + 1, :] = small_ref[SB2_DLAM:SB2_DLAM + 1, :] * (LRU_C * _sigmoid(-lam))

    rev = lambda i: (n_t - 1 - i, 0)
    tile = pl.BlockSpec((ts, d), rev)
    wide = pl.BlockSpec((ts, 7 * d), rev)
    sd = lambda dt: jax.ShapeDtypeStruct((s, d), dt)
    return _pcall(
        body, name=name, grid=(n_t,),
        out_shape=[jax.ShapeDtypeStruct((s, 7 * d), BF16), jax.ShapeDtypeStruct((3, s, d), BF16),
                   jax.ShapeDtypeStruct((2 * SUBLANES, d), F32), jax.ShapeDtypeStruct(wg.shape, F32)],
        in_specs=[tile, wide, pl.BlockSpec((FS_N, ts, d), lambda i: (0, n_t - 1 - i, 0)),
                  pl.BlockSpec((BS_N, ts, d), lambda i: (0, n_t - 1 - i, 0)), tile] + [_full(mod.shape), _full(vec.shape),
                  _full(wg.shape), _full(w_a_out.shape), _full(w_b_out.shape), _full(w_o.shape)],
        out_specs=[wide, pl.BlockSpec((3, ts, d), lambda i: (0, n_t - 1 - i, 0)), _full((2 * SUBLANES, d)), _full(wg.shape)],
        scratch_shapes=[pltpu.VMEM((SUBLANES, d), F32)] * 4 + [pltpu.VMEM((d // LANES, ts, LANES), F32)] * 2
                       + [pltpu.VMEM((ts // SCAN_GROUP, d), F32)],
        args=[dx1, proj, fstack, bstack, dgels, mod, vec, wg, w_a_out, w_b_out, w_o],
        carries=carries)


def _proj_backward(dproj, dx1, x, mod, vec, w_in, name, carries=()):
    s, d = x.shape
    nq = w_in.shape[2]
    ts = min(TOKENS_MATMUL_TILE, s)
    n_t = s // ts

    def body(dp_ref, dx1_ref, x_ref, mod_ref, vec_ref, w_ref, dx_ref, small_ref):
        i = pl.program_id(0)

        @pl.when(i == 0)
        def _():
            small_ref[...] = jnp.zeros_like(small_ref)

        dh = jnp.zeros((ts, d), F32)
        for qb in range(N_CHIP):
            dh = dh + _dot_tb(dp_ref[:, qb * nq:(qb + 1) * nq], w_ref[qb])
        xn, r = _rms(x_ref[...])
        g_pre = vec_ref[V_G_PRE_MIX:V_G_PRE_MIX + 1, :]
        sc1 = 1.0 + mod_ref[M_SC_M:M_SC_M + 1, :]
        dx_ref[...] = dx1_ref[...] + _rms_bwd(dh * (g_pre * sc1), xn, r)
        small_ref[SB1_DSH:SB1_DSH + 1, :] += _colsum(dh)
        small_ref[SB1_DSC:SB1_DSC + 1, :] += _colsum(dh * xn)

        @pl.when(i == n_t - 1)
        def _():
            dgm_t = small_ref[SB1_DSC:SB1_DSC + 1, :]
            small_ref[SB1_DSC:SB1_DSC + 1, :] = dgm_t * g_pre
            small_ref[SB1_DG_PRE:SB1_DG_PRE + 1, :] = dgm_t * sc1

    tile = pl.BlockSpec((ts, d), lambda i: (i, 0))
    return _pcall(
        body, name=name, grid=(n_t,),
        out_shape=[jax.ShapeDtypeStruct((s, d), F32), jax.ShapeDtypeStruct((SUBLANES, d), F32)],
        in_specs=[pl.BlockSpec((ts, N_CHIP * nq), lambda i: (i, 0)), tile, tile, _full(mod.shape), _full(vec.shape),
                  _full(w_in.shape)],
        out_specs=[tile, _full((SUBLANES, d))],
        args=[dproj, dx1, x, mod, vec, w_in], carries=carries)


def _weight_grad(a, b, name, col_blocks=1, tk=512, carries=(), square_relu=False):
    s, k = a.shape
    n = b.shape[1]
    tn = n // col_blocks
    tk = min(tk, k)

    def body(a_ref, b_ref, o_ref):
        av = a_ref[...]
        if square_relu:
            ru = jnp.maximum(av.astype(F32), 0.0)
            av = (ru * ru).astype(BF16)
        o_ref[0] = _dot_ta(av, b_ref[...])

    (out,), carried = _pcall(
        body, name=name, grid=(col_blocks, k // tk),
        out_shape=[jax.ShapeDtypeStruct((col_blocks, k, tn), F32)],
        in_specs=[pl.BlockSpec((s, tk), lambda j, i: (0, i)), pl.BlockSpec((s, tn), lambda j, i: (0, j))],
        out_specs=[pl.BlockSpec((1, tk, tn), lambda j, i: (j, i, 0))],
        args=[a, b], carries=carries)
    return out, carried


def _weight_grad_stacked(a3, b3, name, tk=512, carries=()):
    n_g, s, k = a3.shape
    n = b3.shape[2]
    kq = k // N_CHIP
    tk = min(tk, k)
    chips_per_tile = tk // kq

    def body(a_ref, b_ref, o_ref):
        o_ref[...] = _dot_ta(a_ref[...], b_ref[...]).reshape(chips_per_tile, kq, n)

    (out,), carried = _pcall(
        body, name=name, grid=(n_g, k // tk),
        out_shape=[jax.ShapeDtypeStruct((N_CHIP, n_g, kq, n), F32)],
        in_specs=[pl.BlockSpec((None, s, tk), lambda g, i: (g, 0, i)), pl.BlockSpec((None, s, n), lambda g, i: (g, 0, 0))],
        out_specs=[pl.BlockSpec((chips_per_tile, None, kq, n), lambda g, i: (i, g, 0, 0))],
        args=[a3, b3], carries=carries)
    return out.reshape(N_CHIP, n_g * kq, n), carried


def _adamw(items, name, copy_grad=False, carries=()):
    shape = items[0][0].shape
    cols = shape[-1]
    rows = items[0][0].size // cols
    tr = _row_tile(rows, cols, target_bytes=1024 * 1024 // len(items))
    c1 = 1.0 - ADAM_B1 ** ADAM_STEP
    c2 = 1.0 - ADAM_B2 ** ADAM_STEP
    n_out = 4 if copy_grad else 3
    n = len(items)

    def body(*refs):
        for k in range(n):
            w_ref, g_ref, m_ref, v_ref = refs[4 * k:4 * k + 4]
            outs = refs[4 * n + n_out * k:4 * n + n_out * (k + 1)]
            gv = g_ref[...]
            nm = ADAM_B1 * m_ref[...] + (1.0 - ADAM_B1) * gv
            nv = ADAM_B2 * v_ref[...] + (1.0 - ADAM_B2) * (gv * gv)
            outs[0][...] = (-ADAM_LR) * ((nm / c1) / (jnp.sqrt(nv / c2) + ADAM_EPS) + ADAM_WD * w_ref[...])
            outs[1][...] = nm
            outs[2][...] = nv
            if copy_grad:
                outs[3][...] = gv

    spec = pl.BlockSpec((tr, cols), lambda i: (i, 0))
    outs, carried = _pcall(
        body, name=name, grid=(rows // tr,),
        out_shape=[jax.ShapeDtypeStruct((rows, cols), F32)] * (n_out * n),
        in_specs=[spec] * (4 * n), out_specs=[spec] * (n_out * n),
        args=[t.reshape(rows, cols) for item in items for t in item], carries=carries)
    return [tuple(o.reshape(shape) for o in outs[n_out * k:n_out * (k + 1)]) for k in range(n)], carried


def kernel(x, c, w_mod, b_mod, g_pre_mix, g_post_mix, w_in, conv_a_w, conv_a_b, w_a_out, conv_b_w, conv_b_b, w_gate_r, b_gate_r, w_gate_i, b_gate_i, lru_lambda, w_b_out, w_o, g_pre_mlp, g_post_mlp, w_mlp_up, w_mlp_down, loss_target, m_w_mod, m_b_mod, m_g_pre_mix, m_g_post_mix, m_w_in, m_conv_a_w, m_conv_a_b, m_w_a_out, m_conv_b_w, m_conv_b_b, m_w_gate_r, m_b_gate_r, m_w_gate_i, m_b_gate_i, m_lru_lambda, m_w_b_out, m_w_o, m_g_pre_mlp, m_g_post_mlp, m_w_mlp_up, m_w_mlp_down, v_w_mod, v_b_mod, v_g_pre_mix, v_g_post_mix, v_w_in, v_conv_a_w, v_conv_a_b, v_w_a_out, v_conv_b_w, v_conv_b_b, v_w_gate_r, v_b_gate_r, v_w_gate_i, v_b_gate_i, v_lru_lambda, v_w_b_out, v_w_o, v_g_pre_mlp, v_g_post_mlp, v_w_mlp_up, v_w_mlp_down):
    weights = dict(w_mod=w_mod, b_mod=b_mod, g_pre_mix=g_pre_mix, g_post_mix=g_post_mix, w_in=w_in, conv_a_w=conv_a_w,
                   conv_a_b=conv_a_b, w_a_out=w_a_out, conv_b_w=conv_b_w, conv_b_b=conv_b_b, w_gate_r=w_gate_r,
                   b_gate_r=b_gate_r, w_gate_i=w_gate_i, b_gate_i=b_gate_i, lru_lambda=lru_lambda, w_b_out=w_b_out,
                   w_o=w_o, g_pre_mlp=g_pre_mlp, g_post_mlp=g_post_mlp, w_mlp_up=w_mlp_up, w_mlp_down=w_mlp_down)
    mom1 = dict(w_mod=m_w_mod, b_mod=m_b_mod, g_pre_mix=m_g_pre_mix, g_post_mix=m_g_post_mix, w_in=m_w_in,
                conv_a_w=m_conv_a_w, conv_a_b=m_conv_a_b, w_a_out=m_w_a_out, conv_b_w=m_conv_b_w, conv_b_b=m_conv_b_b,
                w_gate_r=m_w_gate_r, b_gate_r=m_b_gate_r, w_gate_i=m_w_gate_i, b_gate_i=m_b_gate_i,
                lru_lambda=m_lru_lambda, w_b_out=m_w_b_out, w_o=m_w_o, g_pre_mlp=m_g_pre_mlp, g_post_mlp=m_g_post_mlp,
                w_mlp_up=m_w_mlp_up, w_mlp_down=m_w_mlp_down)
    mom2 = dict(w_mod=v_w_mod, b_mod=v_b_mod, g_pre_mix=v_g_pre_mix, g_post_mix=v_g_post_mix, w_in=v_w_in,
                conv_a_w=v_conv_a_w, conv_a_b=v_conv_a_b, w_a_out=v_w_a_out, conv_b_w=v_conv_b_w, conv_b_b=v_conv_b_b,
                w_gate_r=v_w_gate_r, b_gate_r=v_b_gate_r, w_gate_i=v_w_gate_i, b_gate_i=v_b_gate_i,
                lru_lambda=v_lru_lambda, w_b_out=v_w_b_out, w_o=v_w_o, g_pre_mlp=v_g_pre_mlp, g_post_mlp=v_g_post_mlp,
                w_mlp_up=v_w_mlp_up, w_mlp_down=v_w_mlp_down)
    names = list(weights)

    n_layer = w_in.shape[0]
    s, d = x.shape[1], x.shape[2]
    n_head, bw = w_gate_r.shape[1], w_gate_r.shape[2]
    dq = d // N_CHIP
    mq = w_mod.shape[2]
    n_mod = (N_CHIP * mq) // d
    ka, kb = conv_a_w.shape[1], conv_b_w.shape[1]

    mx, my, mc = _place()
    q_me = 2 * mx + my
    q_arr = jnp.reshape(q_me, (1,)).astype(jnp.int32)

    me_dev = 4 * mx + 2 * my + mc
    me_arr = jnp.reshape(me_dev, (1,)).astype(jnp.int32)

    big_names = ["w_in", "w_a_out", "w_b_out", "w_o", "w_mlp_up", "w_mlp_down"]
    groups = [["w_in"], ["w_a_out", "w_b_out", "w_o"], ["w_mlp_up", "w_mlp_down"]]
    placed = {("w_in", 0): _cast_place_all([(w_in, 0)], q_arr, "cast_place_first")[0][0]}
    wfull = [dict() for _ in range(n_layer)]
    riders = {}
    for l in range(n_layer):
        riders.setdefault(3 * l - 1, []).append(([("w_in", l)], 0.9 if l else 1.0))
        riders.setdefault(3 * l - 2 if l else 0, []).append(([(nm, l) for nm in groups[1]], 0.9 if l else 0.5))
        riders.setdefault(3 * l, []).append(([("w_mlp_up", l)], 0.7 if l else 0.9))
        riders.setdefault(3 * l + 1, []).insert(0, ([("w_mlp_down", l)], 0.5))

    def gather_carry(call):
        return [_gather_carry([placed[k] for k in keys], frac) for keys, frac in riders.get(call, [])]

    def gathered(call, carried):
        for (keys, _), ws in zip(riders.get(call, []), carried):
            for (nm, l), w in zip(keys, ws):
                wfull[l][nm] = w.reshape(d, d) if nm in groups[1] else w

    n_conv_rows = n_layer * (ka + kb)
    conv_blk = -(-n_conv_rows // SUBLANES) * SUBLANES
    blk_rows = SUBLANES + conv_blk
    conv_rows = jnp.concatenate([jnp.concatenate([conv_a_w[l], conv_b_w[l]], axis=0) for l in range(n_layer)], axis=0)
    conv_rows = jnp.pad(conv_rows, ((0, conv_blk - n_conv_rows), (0, d - dq)))
    c_conv = jnp.concatenate([jnp.pad(c, ((0, SUBLANES - 1), (0, 0))), conv_rows], axis=0)
    rest = [(nm, l) for l in range(n_layer) for nm in big_names if (nm, l) != ("w_in", 0)]
    rest_placed, carried = _cast_place_all([(weights[nm], l) for nm, l in rest], q_arr, "cast_place_rest",
                                           carries=gather_carry(-1) + [_allgather_carry([c_conv])])
    placed.update(zip(rest, rest_placed))
    gathered(-1, carried[:1])
    gathered1 = lax.dynamic_update_slice(carried[1][0], c_conv, (me_dev * blk_rows, 0)).reshape(N_DEV, blk_rows, d)
    c_all = gathered1[:, 0, :]
    conv_full = jnp.concatenate([gathered1[2 * qb, SUBLANES:SUBLANES + n_conv_rows, :dq] for qb in range(N_CHIP)], axis=1)

    b_mod_shard = lax.dynamic_slice_in_dim(b_mod, q_me * mq, mq, axis=1)
    mod_part = _mod_forward(c_all, w_mod, b_mod_shard, "mod_forward")
    gathered2 = _all_gather_small(mod_part, "gather_mod").reshape(N_DEV, n_layer, N_DEV, mq)
    mod_rows = jnp.concatenate(
        [lax.dynamic_index_in_dim(gathered2[2 * qb], me_dev, axis=1, keepdims=False) for qb in range(N_CHIP)], axis=1)
    mods = [jnp.pad(mod_rows[l].reshape(n_mod, d), ((0, SUBLANES - n_mod), (0, 0))) for l in range(n_layer)]

    vecs = []
    for l in range(n_layer):
        base = l * (ka + kb)
        rows = [g_pre_mix[l], g_post_mix[l], conv_a_b[l], conv_b_b[l], b_gate_r[l], b_gate_i[l], lru_lambda[l],
                g_pre_mlp[l], g_post_mlp[l]]
        vecs.append(jnp.concatenate([jnp.stack(rows, axis=0), conv_full[base:base + ka + kb]], axis=0))

    wgs =[jnp.concatenate([w_gate_r[l], w_gate_i[l]], axis=-1).astype(BF16) for l in range(n_layer)]

    xs = x[0]
    saved = []
    for l in range(n_layer):
        wl = wfull[l]
        (h, proj, dgel), carried = _norm_proj(xs, mods[l], vecs[l], wl["w_in"], f"norm_proj_{l}", gather_carry(3 * l))
        gathered(3 * l, carried)
        (x1, fstack, abm, bstack), carried = _mixer_forward(
            xs, proj, mods[l], vecs[l], wgs[l], wl["w_a_out"], wl["w_b_out"], wl["w_o"], f"mixer_forward_{l}",
            gather_carry(3 * l + 1))
        gathered(3 * l + 1, carried)
        (x2, h2, up, y2, *loss_tile), carried = _mlp_forward(
            x1, mods[l], vecs[l], wl["w_mlp_up"], wl["w_mlp_down"], f"mlp_forward_{l}", gather_carry(3 * l + 2),
            target=loss_target[0] if l == n_layer - 1 else None)
        gathered(3 * l + 2, carried)
        saved.append(dict(x=xs, h=h, proj=proj, x1=x1, fstack=fstack, abm=abm, bstack=bstack, dgel=dgel, h2=h2, up=up, y2=y2))
        xs = x2
    dxs = xs
    loss_block = jnp.pad(loss_tile[0], ((0, 0), (0, d - LANES)))

    chips_q = [q_me ^ 2, q_me ^ 1, q_me ^ 3]
    pf = jnp.stack([mc, q_me] + chips_q).astype(jnp.int32)
    rs = dict(grad={}, landed={}, to_send={}, from_chips={}, out={})
    to_exchange, to_scatter, to_join, to_gather = [], [], [], []
    small_own, small_all = {}, {}

    def ride(call, what, name=None):
        ex = list(to_exchange) if "x" in what else []
        sc = list(to_scatter) if "s" in what else []
        ga = list(to_gather) if "g" in what else []
        jn = []
        for key in (to_join if "j" in what else []):
            if key[0] not in [k[0] for k in jn]:
                jn.append(key)
        carries = []
        if ex:
            carries.append(_exchange_carry([rs["grad"][k] for k in ex]))
        if sc:
            carries.append(_scatter_carry([rs["to_send"][k] for k in sc]))
        if jn:
            carries.append(_join_carry([rs["out"][k[0]] for k in jn], [k[1] for k in jn]))
        if ga:
            carries.append(_allgather_carry([small_own[k] for k in ga]))
        if call is None:
            carried = _run_carries(carries, name) if carries else []
            res = None
        else:
            res, carried = call(carries)
        carried = list(carried)
        if ex:
            for k, ld in zip(ex, carried.pop(0)):
                to_exchange.remove(k)
                rs["landed"][k] = ld
                rs["to_send"][k] = _add_sibling_half(rs["grad"][k], ld, pf, f"rs_add_sibling_{k[0]}_{k[1]}")
                to_scatter.append(k)
        if sc:
            for k, fc in zip(sc, carried.pop(0)):
                to_scatter.remove(k)
                rs["out"][k[0]] = _add_chips(rs["grad"][k], rs["landed"][k], fc, pf, rs["out"].get(k[0]), k[1], n_layer,
                                             f"rs_add_chips_{k[0]}_{k[1]}")
                to_join.append(k)
        if jn:
            for k, o in zip(jn, carried.pop(0)):
                to_join.remove(k)
                rs["out"][k[0]] = o
        if ga:
            for k, o in zip(ga, carried.pop(0)):
                to_gather.remove(k)
                small_all[k] = o
        return res

    def gather_small(key, parts):
        small_own[key] = parts[0] if len(parts) == 1 else jnp.concatenate(parts, axis=0)
        to_gather.append(key)

    def ready(nm, l, g):
        rs["grad"][(nm, l)] = g
        to_exchange.append((nm, l))

    rowblk = lambda t: t.reshape(N_CHIP, t.shape[1] // N_CHIP, t.shape[2])
    small1_prev = None
    for l in reversed(range(n_layer)):
        wl, sv = wfull[l], saved[l]
        dx1, dy2, dup, small3 = ride(lambda cr: _mlp_backward(
            dxs, sv["x1"], sv["y2"], sv["up"], mods[l], vecs[l], wl["w_mlp_up"], wl["w_mlp_down"], f"mlp_backward_{l}", cr), "xsjg")
        ready("w_mlp_up", l, _weight_grad(sv["h2"], dup, f"grad_w_mlp_up_{l}", col_blocks=N_CHIP, tk=d)[0])
        g_down = ride(lambda cr: _weight_grad(sv["up"], dy2, f"grad_w_mlp_down_{l}", tk=d, carries=cr, square_relu=True), "x")
        ready("w_mlp_down", l, rowblk(g_down))
        dproj, dab, small2, dwg = ride(lambda cr: _mixer_backward(
            dx1, sv["proj"], sv["fstack"], sv["bstack"], sv["dgel"], mods[l], vecs[l], wgs[l],
            wl["w_a_out"], wl["w_b_out"], wl["w_o"], f"mixer_backward_{l}", cr), "xsjg")
        gather_small(("late", l, "s"), ([small1_prev] if small1_prev is not None else []) + [small2, small3])
        gather_small(("late", l, "w"), [dwg.reshape(2 * bw, d).astype(BF16)])
        g_in = ride(lambda cr: _weight_grad(sv["h"], dproj, f"grad_w_in_{l}", col_blocks=N_CHIP, carries=cr), "xsj")
        ready("w_in", l, g_in)
        g_abo = ride(lambda cr: _weight_grad_stacked(sv["abm"], dab, f"grad_w_abo_{l}", tk=d, carries=cr), "xg")
        ready("w_abo", l, g_abo)
        dxs, small1_prev = ride(lambda cr: _proj_backward(dproj, dx1, sv["x"], mods[l], vecs[l], wl["w_in"],
                                                          f"proj_backward_{l}", cr), "xsjg")
    grad_x = dxs[None]
    gather_small(("last", 0, "s"), [small1_prev, loss_block])

    tail = 0
    while to_exchange or to_scatter or to_join or to_gather:
        ride(None, "xsjg", f"rs_tail_{tail}")
        tail += 1
    grads, deltas, new_m, new_v = {}, {}, {}, {}

    def adam(nms, copy_grad=False):
        items = [(weights[nm], grads[nm], mom1[nm], mom2[nm]) for nm in nms]
        res, _ = _adamw(items, "adamw_" + "_".join(nms), copy_grad)
        for nm, r in zip(nms, res):
            deltas[nm], new_m[nm], new_v[nm] = r[:3]
            if copy_grad:
                grads[nm] = r[3]

    for nms in (["w_mlp_up", "w_mlp_down"], ["w_in"]):
        for nm in nms:
            grads[nm] = rs["out"][nm].reshape(weights[nm].shape)
        adam(nms, True)

    sums ={k: _sum_devices(small_all[k], small_own[k], me_arr, f"sum_small_{k[0]}_{k[1]}_{k[2]}") for k in small_own}

    loss = sums[("last", 0, "s")][SUBLANES, 0]
    small_full = {}

    def rows_of(l, part):
        if part == 0:
            return (("late", l - 1, "s"), 0) if l >= 1 else (("last", 0, "s"), 0)
        if part == 3:
            return ("late", l, "w"), 0
        base = SUBLANES if l < n_layer - 1 else 0
        return ("late", l, "s"), base + (0, 0, 2 * SUBLANES)[part]

    def summed(l, part, row, n_rows=1):
        key, base = rows_of(l, part)
        return sums[key][base + row:base + row + n_rows]

    def per_device(l, part, row):
        key, base = rows_of(l, part)
        own = small_own[key]
        if key not in small_full:
            small_full[key] = lax.dynamic_update_slice(small_all[key], own, (me_dev * own.shape[0], 0)).reshape(
                (N_DEV,) + own.shape)
        return small_full[key][:, base + row:base + row + 1]

    mod_rows = [(0, SB1_DSH), (0, SB1_DSC), (1, SB2_DGT), (2, SB3_DSH), (2, SB3_DSC), (2, SB3_DGT)]
    dmod_all = jnp.stack([jnp.concatenate([per_device(l, p, r)[:, 0, :] for p, r in mod_rows], axis=1)
                          for l in range(n_layer)], axis=0)
    o1, o2, o3, o4 = 0, SUBLANES, 3 * SUBLANES, 4 * SUBLANES
    small_sum = jnp.stack([jnp.concatenate([summed(l, 0, 0, SUBLANES), summed(l, 1, 0, 2 * SUBLANES),
                                            summed(l, 2, 0, SUBLANES), summed(l, 3, 0, 2 * bw)], axis=0)
                           for l in range(n_layer)], axis=0)
    mod_rows_of = [o1 + SB1_DSH, o1 + SB1_DSC, o2 + SB2_DGT, o3 + SB3_DSH, o3 + SB3_DSC, o3 + SB3_DGT]
    grads["w_mod"] = _mod_backward(c_all.T, lax.dynamic_slice_in_dim(dmod_all, q_me * mq, mq, axis=2), "mod_backward")
    grads["b_mod"] = jnp.concatenate([small_sum[:, k, :] for k in mod_rows_of], axis=1)
    grads["g_pre_mix"] = small_sum[:, o1 + SB1_DG_PRE]
    grads["g_post_mix"] = small_sum[:, o2 + SB2_DG_POST]
    grads["conv_a_w"] = lax.dynamic_slice_in_dim(small_sum[:, o2 + SB2_DWA:o2 + SB2_DWA + ka], q_me * dq, dq, axis=2)
    grads["conv_a_b"] = small_sum[:, o2 + SB2_DBA]
    grads["conv_b_w"] = lax.dynamic_slice_in_dim(small_sum[:, o2 + SB2_DWB:o2 + SB2_DWB + kb], q_me * dq, dq, axis=2)
    grads["conv_b_b"] = small_sum[:, o2 + SB2_DBB]
    grads["lru_lambda"] = small_sum[:, o2 + SB2_DLAM]
    grads["b_gate_r"] = small_sum[:, o2 + SB2_DBR]
    grads["b_gate_i"] = small_sum[:, o2 + SB2_DBI]
    grads["g_pre_mlp"] = small_sum[:, o3 + SB3_DG_PRE]
    grads["g_post_mlp"] = small_sum[:, o3 + SB3_DG_POST]
    dwg_sum = small_sum[:, o4:].reshape(n_layer, n_head, bw, 2 * bw)
    grads["w_gate_r"] = dwg_sum[..., :bw]
    grads["w_gate_i"] = dwg_sum[..., bw:]

    for k, nm in enumerate(groups[1]):
        grads[nm] = rs["out"]["w_abo"][:, k * dq:(k + 1) * dq]

    by_shape = {}
    for nm in names:
        if nm not in deltas:
            by_shape.setdefault(weights[nm].shape, []).append(nm)
    for nms in by_shape.values():
        adam(nms)
    return (loss, grad_x, *[grads[nm] for nm in names], *[deltas[nm] for nm in names],
            *[new_m[nm] for nm in names], *[new_v[nm] for nm in names])
```

```python
import jax
import jax.numpy as jnp
from jax import lax
from jax.experimental import pallas as pl
from jax.experimental.pallas import tpu as pltpu

F32 = jnp.float32
BF16 = jnp.bfloat16
MESH = pl.DeviceIdType.MESH

EPS = 1e-6
LRU_C = 8.0
N_CHIP = 4
N_DEV = 8
ADAM_LR = 0.001
ADAM_B1 = 0.9
ADAM_B2 = 0.999
ADAM_EPS = 1e-08
ADAM_WD = 0.01
ADAM_STEP = 10

VMEM_LIMIT_BYTES = 56 * 1024 * 1024
SUBLANES = 8
LANES = 128
TOKENS_MATMUL_TILE = 512
TOKENS_MIXER_TILE = 256
GELU_K0 = 0.7978845608028654
GELU_K1 = 0.044715

V_G_PRE_MIX, V_G_POST_MIX, V_CONV_A_B, V_CONV_B_B, V_B_GATE_R, V_B_GATE_I, V_LAMBDA, V_G_PRE_MLP, V_G_POST_MLP = range(9)
V_CONV_A_W = 9
V_CONV_B_W = 12
M_SH_M, M_SC_M, M_GT_M, M_SH_F, M_SC_F, M_GT_F = range(6)


def _cparams(n_grid=0):
    sem = ("arbitrary",) * n_grid if n_grid else None
    return pltpu.CompilerParams(dimension_semantics=sem, vmem_limit_bytes=VMEM_LIMIT_BYTES)


def _full(shape):
    return pl.BlockSpec(shape, lambda *_: (0,) * len(shape))


def _dot(a, b):
    return jnp.dot(a, b, preferred_element_type=F32)


def _dot_tb(a, b):
    return lax.dot_general(a, b, (((1,), (1,)), ((), ())), preferred_element_type=F32)


def _dot_ta(a, b):
    return lax.dot_general(a, b, (((0,), (0,)), ((), ())), preferred_element_type=F32)


def _sigmoid(x):
    return 1.0 / (1.0 + jnp.exp(-x))


def _softplus(x):
    return jnp.maximum(x, 0.0) + jnp.log1p(jnp.exp(-jnp.abs(x)))


def _neg_expm1(x):
    series = -x * (1.0 + 0.5 * x * (1.0 + (x / 3.0) * (1.0 + 0.25 * x)))
    return jnp.where(x > -1e-2, series, 1.0 - jnp.exp(x))


def _gelu_and_grad(x):
    x2 = x * x
    s = _sigmoid(x * (2.0 * GELU_K0 + (2.0 * GELU_K0 * GELU_K1) * x2))
    gel = x * s
    return gel, s + gel * (1.0 - s) * (2.0 * GELU_K0 + (6.0 * GELU_K0 * GELU_K1) * x2)


def _rms(x):
    r = lax.rsqrt(jnp.mean(x * x, axis=-1, keepdims=True) + EPS)
    return x * r, r


def _rms_bwd(dxn, xn, r):
    return r * (dxn - xn * jnp.mean(dxn * xn, axis=-1, keepdims=True))


def _colsum(x):
    return jnp.sum(x, axis=0, keepdims=True)


def _rows(t, w):
    return lax.broadcasted_iota(jnp.int32, (t, w), 0)


def _shift_down(x, k, prev8):
    t, w = x.shape
    rolled = pltpu.roll(x, k, 0)
    head = jnp.where(_rows(SUBLANES, w) < k, pltpu.roll(prev8, k, 0), rolled[:SUBLANES])
    return jnp.concatenate([head, rolled[SUBLANES:]], axis=0)


def _shift_up(x, k, next8):
    t, w = x.shape
    rolled = pltpu.roll(x, t - k, 0)
    tail = jnp.where(_rows(SUBLANES, w) >= SUBLANES - k, pltpu.roll(next8, SUBLANES - k, 0), rolled[t - SUBLANES:])
    return jnp.concatenate([rolled[:t - SUBLANES], tail], axis=0)


SCAN_GROUP = 16


def _scan_steps(a, b, group, reverse):
    t, w = a.shape
    pos = _rows(t, w) & (group - 1)
    s = 1
    while s < group:
        keep = (pos < group - s) if reverse else (pos >= s)
        shift = (t - s) if reverse else s
        b = b + a * jnp.where(keep, pltpu.roll(b, shift, 0), 0.0)
        a = a * jnp.where(keep, pltpu.roll(a, shift, 0), 1.0)
        s *= 2
    return b, a


def _scan_two_level(a, b, carry_row, a_buf, b_buf, c_buf, reverse):
    t, w = a.shape
    grp = SCAN_GROUP
    n_grp = t // grp
    h_loc, a_cum = _scan_steps(a, b, grp, reverse)
    end = 0 if reverse else grp - 1
    a_end, h_end = [], []
    for j in range(w // LANES):
        a_buf[j] = a_cum[:, j * LANES:(j + 1) * LANES]
        b_buf[j] = h_loc[:, j * LANES:(j + 1) * LANES]
        a_end.append(a_buf[j, pl.ds(end, n_grp, stride=grp), :])
        h_end.append(b_buf[j, pl.ds(end, n_grp, stride=grp), :])
    a_end = jnp.concatenate(a_end, axis=1)
    h_end = jnp.concatenate(h_end, axis=1)
    h_grp, a_grp = _scan_steps(a_end, h_end, n_grp, reverse)
    h_grp = h_grp + a_grp * carry_row
    rows = _rows(n_grp, w)
    if reverse:
        entering = jnp.where(rows == n_grp - 1, carry_row, pltpu.roll(h_grp, n_grp - 1, 0))
    else:
        entering = jnp.where(rows == 0, carry_row, pltpu.roll(h_grp, 1, 0))
    c_buf[...] = entering
    out = [h_loc[g * grp:(g + 1) * grp] + a_cum[g * grp:(g + 1) * grp] * c_buf[g:g + 1, :] for g in range(n_grp)]
    return jnp.concatenate(out, axis=0)


def _row_tile(rows, cols, itemsize=4, target_bytes=2 * 1024 * 1024):
    if rows * cols * itemsize <= target_bytes or rows % SUBLANES:
        return rows
    t = max(SUBLANES, (target_bytes // (cols * itemsize)) // SUBLANES * SUBLANES)
    while rows % t:
        t -= SUBLANES
    return t


def _place():
    return lax.axis_index("x"), lax.axis_index("y"), lax.axis_index("c")


def _other_chips(x, y):
    chips = [(1 - x, y), (x, 1 - y), (1 - x, 1 - y)]
    return chips, [2 * cx + cy for cx, cy in chips]


def _all_gather_small(block, name):
    m_per, n = block.shape

    def body(x_ref, out_ref, send_sems, recv_sems, local_sem):
        x, y, c = _place()
        me, sibling = (x, y, c), (x, y, 1 - c)
        chips, _ = _other_chips(x, y)

        def rows(px, py, pc):
            return out_ref.at[pl.ds((4 * px + 2 * py + pc) * m_per, m_per), :]

        def copy(k, blk, to, src=None):
            return pltpu.make_async_remote_copy(
                src_ref=rows(*blk) if src is None else src, dst_ref=rows(*blk),
                send_sem=send_sems.at[k], recv_sem=recv_sems.at[k], device_id=to, device_id_type=MESH)

        mine = pltpu.make_async_copy(x_ref, rows(*me), local_sem)
        mine.start()
        first = [copy(0, me, sibling, src=x_ref)]
        first += [copy(1 + j, me, (*chip, c), src=x_ref) for j, chip in enumerate(chips)]
        for cp in first:
            cp.start()
        passed = [copy(4 + j, (*chip, c), sibling) for j, chip in enumerate(chips)]
        for j, chip in enumerate(chips):
            copy(1 + j, (*chip, c), me).wait_recv()
            passed[j].start()
        copy(0, sibling, me).wait_recv()
        for j, chip in enumerate(chips):
            copy(4 + j, (*chip, 1 - c), me).wait_recv()
        for cp in first + passed:
            cp.wait_send()
        mine.wait()

    return pl.pallas_call(
        body, name=name,
        out_shape=jax.ShapeDtypeStruct((N_DEV * m_per, n), block.dtype),
        in_specs=[pl.BlockSpec(memory_space=pltpu.VMEM)],
        out_specs=pl.BlockSpec(memory_space=pltpu.VMEM),
        scratch_shapes=[pltpu.SemaphoreType.DMA((7,)), pltpu.SemaphoreType.DMA((7,)), pltpu.SemaphoreType.DMA],
        compiler_params=pltpu.CompilerParams(vmem_limit_bytes=VMEM_LIMIT_BYTES),
    )(block)


class _Carry:
    def __init__(self, ins, out_shapes, aliases, sem_shapes, start, finish, mid=None, mid_frac=0.85):
        self.ins, self.out_shapes, self.aliases, self.sem_shapes = list(ins), list(out_shapes), dict(aliases), list(sem_shapes)
        self.start, self.mid, self.finish, self.mid_frac = start, mid, finish, mid_frac


def _pcall(body, *, name, grid, in_specs, out_specs, out_shape, args, scratch_shapes=(), carries=(), prefetch=()):
    in_specs, out_specs, out_shape = list(in_specs), list(out_specs), list(out_shape)
    scratch_shapes, args = list(scratch_shapes), list(args)
    n_in, n_out, n_scr, n_pre = len(in_specs), len(out_shape), len(scratch_shapes), len(prefetch)
    steps = 1
    for g in grid:
        steps *= g
    any_spec = pl.BlockSpec(memory_space=pl.ANY)
    aliases = {}
    spans = []
    for cr in carries:
        spans.append((len(args), len(out_shape), len(scratch_shapes)))
        for a, b in cr.aliases.items():
            aliases[n_pre + len(args) + a] = len(out_shape) + b
        args += cr.ins
        in_specs += [any_spec] * len(cr.ins)
        out_shape += cr.out_shapes
        out_specs += [any_spec] * len(cr.out_shapes)
        scratch_shapes += cr.sem_shapes
    n_all_in = len(args)
    n_all_out = len(out_shape)

    def wrapped(*refs):
        pre, refs = refs[:n_pre], refs[n_pre:]
        ins, outs, scr = refs[:n_all_in], refs[n_all_in:n_all_in + n_all_out], refs[n_all_in + n_all_out:]
        parts = [(cr, ins[a:a + len(cr.ins)], outs[b:b + len(cr.out_shapes)], scr[s:s + len(cr.sem_shapes)])
                 for cr, (a, b, s) in zip(carries, spans)]
        lin = 0
        for ax, g in enumerate(grid):
            lin = lin * g + pl.program_id(ax)

        def at(step, fn):
            if steps == 1:
                fn()
            else:
                pl.when(lin == step)(fn)

        def start_all():
            for cr, ci, co, cs in parts:
                cr.start(ci, co, cs)

        def finish_all():
            for cr, ci, co, cs in parts:
                cr.finish(ci, co, cs)

        if parts:
            at(0, start_all)
        body(*pre, *ins[:n_in], *outs[:n_out], *scr[:n_scr])
        for cr, ci, co, cs in parts:
            if cr.mid is not None:
                at(min(steps - 1, int(steps * cr.mid_frac)), lambda cr=cr, ci=ci, co=co, cs=cs: cr.mid(ci, co, cs))
        if parts:
            at(steps - 1, finish_all)

    if n_pre:
        res = pl.pallas_call(
            wrapped, name=name, out_shape=out_shape,
            grid_spec=pltpu.PrefetchScalarGridSpec(num_scalar_prefetch=n_pre, grid=tuple(grid), in_specs=in_specs,
                                                   out_specs=out_specs, scratch_shapes=scratch_shapes),
            input_output_aliases=aliases, compiler_params=_cparams(len(grid)),
        )(*prefetch, *args)
    else:
        res = pl.pallas_call(
            wrapped, name=name, grid=tuple(grid), out_shape=out_shape, in_specs=in_specs, out_specs=out_specs,
            scratch_shapes=scratch_shapes, input_output_aliases=aliases, compiler_params=_cparams(len(grid)),
        )(*args)
    res = list(res)
    return res[:n_out], [res[b:b + len(cr.out_shapes)] for cr, (_, b, _) in zip(carries, spans)]


def _run_carries(carries, name):
    return _pcall(lambda: None, name=name, grid=(), in_specs=[], out_specs=[], out_shape=[], args=[], carries=carries)[1]


CAST_STEPS = 8


def _cast_place_all(shards, q_arr, name, carries=()):
    n = len(shards)

    def body(q_ref, *refs):
        for k in range(n):
            refs[n + k][...] = refs[k][...].astype(BF16)

    def spec_in(k):
        w, layer = shards[k]
        return pl.BlockSpec((1, w.shape[1] // CAST_STEPS, w.shape[2]), lambda i, q_ref: (layer, i, 0))

    def spec_out(k):
        w, _ = shards[k]
        return pl.BlockSpec((1, w.shape[1] // CAST_STEPS, w.shape[2]), lambda i, q_ref: (q_ref[0], i, 0))

    return _pcall(
        body, name=name, grid=(CAST_STEPS,),
        out_shape=[jax.ShapeDtypeStruct((N_CHIP,) + w.shape[1:], BF16) for w, _ in shards],
        in_specs=[spec_in(k) for k in range(n)], out_specs=[spec_out(k) for k in range(n)],
        args=[w for w, _ in shards], carries=carries, prefetch=[q_arr])


def _gather_carry(bufs, mid_frac=0.85):
    n = len(bufs)

    def copies(o_refs, sems):
        send_sems, recv_sems = sems
        x, y, c = _place()
        q = 2 * x + y
        sibling = (x, y, 1 - c)
        chips, qs = _other_chips(x, y)

        def half(w, shard, pc):
            rh = bufs[w].shape[1] // 2
            return o_refs[w].at[shard, pl.ds(pc * rh, rh), :]

        def over_ici(w, j, shard):
            blk = half(w, shard, c)
            return pltpu.make_async_remote_copy(
                src_ref=blk, dst_ref=blk, send_sem=send_sems.at[w, j], recv_sem=recv_sems.at[w, j],
                device_id=(*chips[j], c), device_id_type=MESH)

        def to_sibling(w, j, pc):
            blk = half(w, qs[j], pc)
            return pltpu.make_async_remote_copy(
                src_ref=blk, dst_ref=blk, send_sem=send_sems.at[w, 3 + j], recv_sem=recv_sems.at[w, 3 + j],
                device_id=sibling, device_id_type=MESH)

        return q, c, qs, over_ici, to_sibling

    pairs = [(w, j) for w in range(n) for j in range(3)]

    def start(i_refs, o_refs, sems):
        q, _, _, over_ici, _ = copies(o_refs, sems)
        for w, j in pairs:
            over_ici(w, j, q).start()

    def mid(i_refs, o_refs, sems):
        _, c, qs, over_ici, to_sibling = copies(o_refs, sems)
        for w, j in pairs:
            over_ici(w, j, qs[j]).wait_recv()
            to_sibling(w, j, c).start()

    def finish(i_refs, o_refs, sems):
        q, c, _, over_ici, to_sibling = copies(o_refs, sems)
        for w, j in pairs:
            to_sibling(w, j, 1 - c).wait_recv()
        for w, j in pairs:
            over_ici(w, j, q).wait_send()
            to_sibling(w, j, c).wait_send()

    return _Carry(bufs, [jax.ShapeDtypeStruct(b.shape, b.dtype) for b in bufs], {w: w for w in range(n)},
                  [pltpu.SemaphoreType.DMA((n, 6)), pltpu.SemaphoreType.DMA((n, 6))], start, finish, mid, mid_frac)


def _exchange_carry(grads):
    n = len(grads)

    def copies(g_refs, l_refs, sems):
        send_sems, recv_sems = sems
        x, y, c = _place()
        out = []
        for w in range(n):
            rh = grads[w].shape[1] // 2
            out.append(pltpu.make_async_remote_copy(
                src_ref=g_refs[w].at[:, pl.ds((1 - c) * rh, rh), :], dst_ref=l_refs[w],
                send_sem=send_sems.at[w], recv_sem=recv_sems.at[w], device_id=(x, y, 1 - c), device_id_type=MESH))
        return out

    def start(g_refs, l_refs, sems):
        for cp in copies(g_refs, l_refs, sems):
            cp.start()

    def finish(g_refs, l_refs, sems):
        for cp in copies(g_refs, l_refs, sems):
            cp.wait()

    return _Carry(grads, [jax.ShapeDtypeStruct((N_CHIP, g.shape[1] // 2, g.shape[2]), g.dtype) for g in grads], {},
                  [pltpu.SemaphoreType.DMA((n,)), pltpu.SemaphoreType.DMA((n,))], start, finish)


def _scatter_carry(sums):
    n = len(sums)

    def copies(s_refs, l_refs, sems):
        send_sems, recv_sems = sems
        x, y, c = _place()
        chips, _ = _other_chips(x, y)
        return [pltpu.make_async_remote_copy(
            src_ref=s_refs[w].at[j], dst_ref=l_refs[w].at[j], send_sem=send_sems.at[w, j], recv_sem=recv_sems.at[w, j],
            device_id=(*chips[j], c), device_id_type=MESH) for w in range(n) for j in range(3)]

    def start(s_refs, l_refs, sems):
        for cp in copies(s_refs, l_refs, sems):
            cp.start()

    def finish(s_refs, l_refs, sems):
        for cp in copies(s_refs, l_refs, sems):
            cp.wait()

    return _Carry(sums, [jax.ShapeDtypeStruct(s.shape, s.dtype) for s in sums], {},
                  [pltpu.SemaphoreType.DMA((n, 3)), pltpu.SemaphoreType.DMA((n, 3))], start, finish)


def _join_carry(outs, layers):
    n = len(outs)

    def copy(o_refs, sems, w, mine):
        send_sems, recv_sems = sems
        x, y, c = _place()
        r = outs[w].shape[1]
        rows = o_refs[w].at[layers[w], pl.ds((c if mine else 1 - c) * (r // 2), r // 2), :]
        return pltpu.make_async_remote_copy(
            src_ref=rows, dst_ref=rows, send_sem=send_sems.at[w], recv_sem=recv_sems.at[w],
            device_id=(x, y, 1 - c), device_id_type=MESH)

    def start(i_refs, o_refs, sems):
        for w in range(n):
            copy(o_refs, sems, w, True).start()

    def finish(i_refs, o_refs, sems):
        for w in range(n):
            copy(o_refs, sems, w, True).wait_send()
        for w in range(n):
            copy(o_refs, sems, w, False).wait_recv()

    return _Carry(outs, [jax.ShapeDtypeStruct(o.shape, o.dtype) for o in outs], {w: w for w in range(n)},
                  [pltpu.SemaphoreType.DMA((n,)), pltpu.SemaphoreType.DMA((n,))], start, finish)


PF_C, PF_Q, PF_QS = 0, 1, 2


def _add_sibling_half(g, landed, pf, name):
    _, r, cols = g.shape
    rh = r // 2
    tr = _row_tile(rh, cols)
    nr = rh // tr

    def body(pf_ref, g_ref, l_ref, o_ref):
        o_ref[...] = (g_ref[...] + l_ref[...]).astype(BF16)

    return pl.pallas_call(
        body, name=name,
        out_shape=jax.ShapeDtypeStruct((3, rh, cols), BF16),
        grid_spec=pltpu.PrefetchScalarGridSpec(
            num_scalar_prefetch=1, grid=(3, nr),
            in_specs=[pl.BlockSpec((1, tr, cols), lambda j, i, pf_ref: (pf_ref[PF_QS + j], pf_ref[PF_C] * nr + i, 0)),
                      pl.BlockSpec((1, tr, cols), lambda j, i, pf_ref: (pf_ref[PF_QS + j], i, 0))],
            out_specs=pl.BlockSpec((1, tr, cols), lambda j, i, pf_ref: (j, i, 0))),
        compiler_params=_cparams(2),
    )(pf, g, landed)


def _add_chips(g, landed, from_chips, pf, prev, layer, n_layer, name):
    _, r, cols = g.shape
    rh = r // 2
    tr = _row_tile(rh, cols)
    nr = rh // tr

    def body(pf_ref, g_ref, l_ref, f_ref, *rest):
        o_ref = rest[-1]
        acc = g_ref[0] + l_ref[0]
        for j in range(3):
            acc = acc + f_ref[j].astype(F32)
        o_ref[0] = acc

    in_specs = [pl.BlockSpec((1, tr, cols), lambda i, pf_ref: (pf_ref[PF_Q], pf_ref[PF_C] * nr + i, 0)),
                pl.BlockSpec((1, tr, cols), lambda i, pf_ref: (pf_ref[PF_Q], i, 0)),
                pl.BlockSpec((3, tr, cols), lambda i, pf_ref: (0, i, 0))]
    args = [pf, g, landed, from_chips]
    aliases = {}
    if prev is not None:
        in_specs.append(pl.BlockSpec(memory_space=pl.ANY))
        args.append(prev)
        aliases = {4: 0}
    return pl.pallas_call(
        body, name=name,
        out_shape=jax.ShapeDtypeStruct((n_layer, r, cols), F32),
        grid_spec=pltpu.PrefetchScalarGridSpec(
            num_scalar_prefetch=1, grid=(nr,), in_specs=in_specs,
            out_specs=pl.BlockSpec((1, tr, cols), lambda i, pf_ref: (layer, pf_ref[PF_C] * nr + i, 0))),
        input_output_aliases=aliases,
        compiler_params=_cparams(1),
    )(*args)


def _allgather_carry(blocks):
    n = len(blocks)

    def copies(b_refs, o_refs, sems):
        send_sems, recv_sems = sems
        x, y, c = _place()
        chips, _ = _other_chips(x, y)

        def place(w, px, py, pc):
            m = blocks[w].shape[0]
            return o_refs[w].at[pl.ds((4 * px + 2 * py + pc) * m, m), :]

        def own_to(w, k, to):
            dst = place(w, x, y, c)
            return pltpu.make_async_remote_copy(src_ref=b_refs[w], dst_ref=dst, send_sem=send_sems.at[w, k],
                                                recv_sem=recv_sems.at[w, k], device_id=to, device_id_type=MESH)

        def landed_from(w, k, px, py, pc):
            blk = place(w, px, py, pc)
            return pltpu.make_async_remote_copy(src_ref=blk, dst_ref=blk, send_sem=send_sems.at[w, k],
                                                recv_sem=recv_sems.at[w, k], device_id=(x, y, 1 - c), device_id_type=MESH)

        return x, y, c, chips, own_to, landed_from

    def start(b_refs, o_refs, sems):
        x, y, c, chips, own_to, _ = copies(b_refs, o_refs, sems)
        for w in range(n):
            own_to(w, 0, (x, y, 1 - c)).start()
            for j, chip in enumerate(chips):
                own_to(w, 1 + j, (*chip, c)).start()

    def mid(b_refs, o_refs, sems):
        x, y, c, chips, _, landed_from = copies(b_refs, o_refs, sems)
        for w in range(n):
            for j, chip in enumerate(chips):
                landed_from(w, 1 + j, *chip, c).wait_recv()
                landed_from(w, 4 + j, *chip, c).start()

    def finish(b_refs, o_refs, sems):
        x, y, c, chips, own_to, landed_from = copies(b_refs, o_refs, sems)
        for w in range(n):
            landed_from(w, 0, x, y, 1 - c).wait_recv()
            for j, chip in enumerate(chips):
                landed_from(w, 4 + j, *chip, 1 - c).wait_recv()
            own_to(w, 0, (x, y, 1 - c)).wait_send()
            for j, chip in enumerate(chips):
                own_to(w, 1 + j, (*chip, c)).wait_send()
                landed_from(w, 4 + j, *chip, c).wait_send()

    return _Carry(blocks, [jax.ShapeDtypeStruct((N_DEV * b.shape[0], b.shape[1]), b.dtype) for b in blocks], {},
                  [pltpu.SemaphoreType.DMA((n, 7)), pltpu.SemaphoreType.DMA((n, 7))], start, finish, mid)


def _sum_devices(gathered, own, me_arr, name):
    m, n = own.shape
    tr = _row_tile(m, n, itemsize=own.dtype.itemsize, target_bytes=256 * 1024)
    nr = m // tr

    def body(me_ref, *refs):
        g_refs, own_ref, o_ref = refs[:N_DEV], refs[N_DEV], refs[N_DEV + 1]
        me = me_ref[0]
        acc = None
        for dev in range(N_DEV):
            term = jnp.where(me == dev, own_ref[...], g_refs[dev][...]).astype(F32)
            acc = term if acc is None else acc + term
        o_ref[...] = acc

    def dev_rows(dev):
        return pl.BlockSpec((tr, n), lambda i, me_ref: (dev * nr + i, 0))

    return pl.pallas_call(
        body, name=name,
        out_shape=jax.ShapeDtypeStruct((m, n), F32),
        grid_spec=pltpu.PrefetchScalarGridSpec(
            num_scalar_prefetch=1, grid=(nr,),
            in_specs=[dev_rows(dev) for dev in range(N_DEV)] + [pl.BlockSpec((tr, n), lambda i, me_ref: (i, 0))],
            out_specs=pl.BlockSpec((tr, n), lambda i, me_ref: (i, 0))),
        compiler_params=_cparams(1),
    )(me_arr, *([gathered] * N_DEV), own)


def _mod_forward(c_all, w_mod, b_mod_shard, name):
    n_layer, d, mq = w_mod.shape

    def body(c_ref, w_ref, b_ref, o_ref):
        cv = c_ref[...]
        o_ref[...] = _dot(cv * _sigmoid(cv), w_ref[0]) + b_ref[0]

    return pl.pallas_call(
        body, name=name, grid=(n_layer,),
        out_shape=jax.ShapeDtypeStruct((n_layer * N_DEV, mq), F32),
        in_specs=[_full((N_DEV, d)), pl.BlockSpec((1, d, mq), lambda l: (l, 0, 0)),
                  pl.BlockSpec((1, 1, mq), lambda l: (l, 0, 0))],
        out_specs=pl.BlockSpec((N_DEV, mq), lambda l: (l, 0)),
        compiler_params=_cparams(1),
    )(c_all, w_mod, b_mod_shard.reshape(n_layer, 1, mq))


def _mod_backward(c_all_t, dmod_shard, name):
    n_layer, _, mq = dmod_shard.shape
    d = c_all_t.shape[0]

    def body(c_ref, dm_ref, o_ref):
        cv = c_ref[...]
        o_ref[0] = _dot(cv * _sigmoid(cv), dm_ref[0])

    return pl.pallas_call(
        body, name=name, grid=(n_layer,),
        out_shape=jax.ShapeDtypeStruct((n_layer, d, mq), F32),
        in_specs=[_full((d, N_DEV)), pl.BlockSpec((1, N_DEV, mq), lambda l: (l, 0, 0))],
        out_specs=pl.BlockSpec((1, d, mq), lambda l: (l, 0, 0)),
        compiler_params=_cparams(1),
    )(c_all_t, dmod_shard)


def _norm_proj(x, mod, vec, w_in, name, carries=()):
    s, d = x.shape
    nq = w_in.shape[2]
    ts = min(TOKENS_MATMUL_TILE, s)

    def body(x_ref, mod_ref, vec_ref, w_ref, h_ref, p_ref, dgel_ref):
        xn, _ = _rms(x_ref[...])
        gm = vec_ref[V_G_PRE_MIX:V_G_PRE_MIX + 1, :] * (1.0 + mod_ref[M_SC_M:M_SC_M + 1, :])
        h = (xn * gm + mod_ref[M_SH_M:M_SH_M + 1, :]).astype(BF16)
        h_ref[...] = h
        for qb in range(N_CHIP):
            pq = _dot(h, w_ref[qb])
            for k in range(N_CHIP * nq // d):
                lo, hi = max(qb * nq, k * d), min((qb + 1) * nq, (k + 1) * d)
                if lo >= hi:
                    continue
                piece = pq[:, lo - qb * nq:hi - qb * nq]
                if k == 4:
                    piece, dgel = _gelu_and_grad(piece)
                    dgel_ref[:, lo - 4 * d:hi - 4 * d] = dgel.astype(BF16)
                elif k >= 5:
                    piece = _sigmoid(piece)
                p_ref[:, lo:hi] = piece.astype(BF16)

    tile = pl.BlockSpec((ts, d), lambda i: (i, 0))
    return _pcall(
        body, name=name, grid=(s // ts,),
        out_shape=[jax.ShapeDtypeStruct((s, d), BF16), jax.ShapeDtypeStruct((s, N_CHIP * nq), BF16),
                   jax.ShapeDtypeStruct((s, d), BF16)],
        in_specs=[tile, _full(mod.shape), _full(vec.shape), _full(w_in.shape)],
        out_specs=[tile, pl.BlockSpec((ts, N_CHIP * nq), lambda i: (i, 0)), tile],
        args=[x, mod, vec, w_in], carries=carries)


def _gate_pre(xb2_b, wg_ref, n_head, bw):
    zr, zi = [], []
    for hd in range(n_head):
        z = _dot(xb2_b[:, hd * bw:(hd + 1) * bw], wg_ref[hd])
        zr.append(z[:, :bw])
        zi.append(z[:, bw:])
    return jnp.concatenate(zr, axis=1), jnp.concatenate(zi, axis=1)


def _lru_coeffs(xb2, wg_ref, vec_ref, n_head, bw):
    zr, zi = _gate_pre(xb2.astype(BF16), wg_ref, n_head, bw)
    r = _sigmoid(zr + vec_ref[V_B_GATE_R:V_B_GATE_R + 1, :])
    gi = _sigmoid(zi + vec_ref[V_B_GATE_I:V_B_GATE_I + 1, :])
    sp = _softplus(-vec_ref[V_LAMBDA:V_LAMBDA + 1, :])
    log_a = (-LRU_C) * r * sp
    a = jnp.exp(log_a)
    mult = jnp.sqrt(_neg_expm1(2.0 * log_a))
    return r, gi, sp, a, mult


def _mixer_forward(x, proj, mod, vec, wg, w_a_out, w_b_out, w_o, name, carries=()):
    s, d = x.shape
    n_head, bw, _ = wg.shape
    ts = min(TOKENS_MIXER_TILE, s)

    def body(x_ref, p_ref, mod_ref, vec_ref, wg_ref, wa_ref, wb_ref, wo_ref,
             x1_ref, conva_ref, xb2_ref, hh_ref, abm_ref, pa_ref, pb_ref, y_ref,
             r_ref, gi_ref, a_ref, mult_ref,
             cv_tail, xb_tail, h_last, a_buf, b_buf, c_buf):
        i = pl.program_id(0)

        @pl.when(i == 0)
        def _():
            cv_tail[...] = jnp.zeros_like(cv_tail)
            xb_tail[...] = jnp.zeros_like(xb_tail)
            h_last[...] = jnp.zeros_like(h_last)

        def seg(k):
            return p_ref[:, k * d:(k + 1) * d].astype(F32)

        def vrow(k):
            return vec_ref[k:k + 1, :]

        b_a, c_a, v_a, x_b, gel, sa, sb = (seg(k) for k in range(7))
        cv = c_a * v_a
        prev_cv = cv_tail[...]
        conv_a = (vrow(V_CONV_A_B) + vrow(V_CONV_A_W) * _shift_down(cv, 2, prev_cv)
                  + vrow(V_CONV_A_W + 1) * _shift_down(cv, 1, prev_cv) + vrow(V_CONV_A_W + 2) * cv)
        cv_tail[...] = cv[ts - SUBLANES:]
        y_a = b_a * conv_a
        prev_xb = xb_tail[...]
        xb2 = (vrow(V_CONV_B_B) + vrow(V_CONV_B_W) * _shift_down(x_b, 3, prev_xb)
               + vrow(V_CONV_B_W + 1) * _shift_down(x_b, 2, prev_xb)
               + vrow(V_CONV_B_W + 2) * _shift_down(x_b, 1, prev_xb) + vrow(V_CONV_B_W + 3) * x_b)
        xb_tail[...] = x_b[ts - SUBLANES:]
        r, gi, _, a, mult = _lru_coeffs(xb2, wg_ref, vec_ref, n_head, bw)
        r_ref[...] = r.astype(BF16)
        gi_ref[...] = gi.astype(BF16)
        a_ref[...] = a
        mult_ref[...] = mult.astype(BF16)
        hh = _scan_two_level(a, mult * gi * xb2, h_last[SUBLANES - 1:SUBLANES, :], a_buf, b_buf, c_buf, reverse=False)
        h_last[...] = hh[ts - SUBLANES:]
        y_b = hh * gel
        ya_b, yb_b = y_a.astype(BF16), y_b.astype(BF16)
        pa = _dot(ya_b, wa_ref[...])
        pb = _dot(yb_b, wb_ref[...])
        m = (sa * pa + sb * pb).astype(BF16)
        y = _dot(m, wo_ref[...])
        yn, _ = _rms(y)
        gg = mod_ref[M_GT_M:M_GT_M + 1, :] * vrow(V_G_POST_MIX)
        x1_ref[...] = x_ref[...] + yn * gg
        conva_ref[...] = conv_a.astype(BF16)
        xb2_ref[...] = xb2
        hh_ref[...] = hh
        abm_ref[0] = ya_b
        abm_ref[1] = yb_b
        abm_ref[2] = m
        pa_ref[...] = pa.astype(BF16)
        pb_ref[...] = pb.astype(BF16)
        y_ref[...] = y.astype(BF16)

    tile = pl.BlockSpec((ts, d), lambda i: (i, 0))
    tile3 = pl.BlockSpec((3, ts, d), lambda i: (0, i, 0))
    sd = lambda dt: jax.ShapeDtypeStruct((s, d), dt)
    return _pcall(
        body, name=name, grid=(s // ts,),
        out_shape=[sd(F32), sd(BF16), sd(F32), sd(F32), jax.ShapeDtypeStruct((3, s, d), BF16), sd(BF16), sd(BF16), sd(BF16),
                   sd(BF16), sd(BF16), sd(F32), sd(BF16)],
        in_specs=[tile, pl.BlockSpec((ts, 7 * d), lambda i: (i, 0)), _full(mod.shape), _full(vec.shape),
                  _full(wg.shape), _full(w_a_out.shape), _full(w_b_out.shape), _full(w_o.shape)],
        out_specs=[tile] * 4 + [tile3] + [tile] * 7,
        scratch_shapes=[pltpu.VMEM((SUBLANES, d), F32)] * 3 + [pltpu.VMEM((d // LANES, ts, LANES), F32)] * 2
                       + [pltpu.VMEM((ts // SCAN_GROUP, d), F32)],
        args=[x, proj, mod, vec, wg, w_a_out, w_b_out, w_o], carries=carries)


def _mlp_forward(x1, mod, vec, w_up, w_down, name, carries=(), target=None):
    s, d = x1.shape
    fq = w_up.shape[2]
    ts = min(TOKENS_MATMUL_TILE, s)

    def body(x_ref, *refs):
        if target is None:
            mod_ref, vec_ref, wu_ref, wd_ref, x2_ref, h2_ref, up_ref, y2_ref = refs
        else:
            t_ref, mod_ref, vec_ref, wu_ref, wd_ref, x2_ref, h2_ref, up_ref, y2_ref, loss_ref = refs
        x = x_ref[...]
        xn, _ = _rms(x)
        gm = vec_ref[V_G_PRE_MLP:V_G_PRE_MLP + 1, :] * (1.0 + mod_ref[M_SC_F:M_SC_F + 1, :])
        h2 = (xn * gm + mod_ref[M_SH_F:M_SH_F + 1, :]).astype(BF16)
        h2_ref[...] = h2
        y2 = jnp.zeros((ts, d), F32)
        for qb in range(N_CHIP):
            up = _dot(h2, wu_ref[qb])
            up_ref[:, qb * fq:(qb + 1) * fq] = up.astype(BF16)
            ru = jnp.maximum(up, 0.0)
            y2 = y2 + _dot((ru * ru).astype(BF16), wd_ref[qb])
        y2_ref[...] = y2.astype(BF16)
        yn, _ = _rms(y2)
        gg = mod_ref[M_GT_F:M_GT_F + 1, :] * vec_ref[V_G_POST_MLP:V_G_POST_MLP + 1, :]
        x2 = x + yn * gg
        if target is None:
            x2_ref[...] = x2
        else:
            @pl.when(pl.program_id(0) == 0)
            def _():
                loss_ref[...] = jnp.zeros_like(loss_ref)

            err = x2 - t_ref[...]
            x2_ref[...] = err * (1.0 / d)
            loss_ref[...] += jnp.sum(jnp.sum(err * err, axis=1, keepdims=True), axis=0, keepdims=True) * (0.5 / d)

    tile = pl.BlockSpec((ts, d), lambda i: (i, 0))
    last = target is not None
    return _pcall(
        body, name=name, grid=(s // ts,),
        out_shape=[jax.ShapeDtypeStruct((s, d), F32), jax.ShapeDtypeStruct((s, d), BF16),
                   jax.ShapeDtypeStruct((s, N_CHIP * fq), BF16), jax.ShapeDtypeStruct((s, d), BF16)]
                  + ([jax.ShapeDtypeStruct((SUBLANES, LANES), F32)] if last else []),
        in_specs=[tile] + ([tile] if last else []) + [_full(mod.shape), _full(vec.shape), _full(w_up.shape), _full(w_down.shape)],
        out_specs=[tile, tile, pl.BlockSpec((ts, N_CHIP * fq), lambda i: (i, 0)), tile]
                 + ([_full((SUBLANES, LANES))] if last else []),
        args=[x1] + ([target] if last else []) + [mod, vec, w_up, w_down], carries=carries)


SB3_DSH, SB3_DSC, SB3_DGT, SB3_DG_PRE, SB3_DG_POST = range(5)
SB1_DSH, SB1_DSC, SB1_DG_PRE = range(3)
(SB2_DGT, SB2_DG_POST, SB2_DWA, SB2_DBA, SB2_DWB, SB2_DBB, SB2_DLAM, SB2_DBR, SB2_DBI) = (0, 1, 2, 5, 6, 10, 11, 12, 13)


def _mlp_backward(dx2, x1, y2, up, mod, vec, w_up, w_down, name, carries=()):
    s, d = dx2.shape
    fq = w_up.shape[2]
    ts = min(TOKENS_MATMUL_TILE, s)
    n_t = s // ts

    def body(dx2_ref, x_ref, y2_ref, up_ref, mod_ref, vec_ref, wu_ref, wd_ref,
             dx1_ref, dy2_ref, dup_ref, small_ref):
        i = pl.program_id(0)

        @pl.when(i == 0)
        def _():
            small_ref[...] = jnp.zeros_like(small_ref)

        dout = dx2_ref[...]
        y2n, ry = _rms(y2_ref[...].astype(F32))
        g_post = vec_ref[V_G_POST_MLP:V_G_POST_MLP + 1, :]
        gt = mod_ref[M_GT_F:M_GT_F + 1, :]
        dgg = _colsum(dout * y2n)
        dy2 = _rms_bwd(dout * (gt * g_post), y2n, ry).astype(BF16)
        dy2_ref[...] = dy2
        dh2 = jnp.zeros((ts, d), F32)
        for qb in range(N_CHIP):
            cols = slice(qb * fq, (qb + 1) * fq)
            dact = _dot_tb(dy2, wd_ref[qb])
            ru = jnp.maximum(up_ref[:, cols].astype(F32), 0.0)
            dup = (dact * (2.0 * ru)).astype(BF16)
            dup_ref[:, cols] = dup
            dh2 = dh2 + _dot_tb(dup, wu_ref[qb])
        xn, r = _rms(x_ref[...])
        g_pre = vec_ref[V_G_PRE_MLP:V_G_PRE_MLP + 1, :]
        sc1 = 1.0 + mod_ref[M_SC_F:M_SC_F + 1, :]
        dsh = _colsum(dh2)
        dgm = _colsum(dh2 * xn)
        dx1_ref[...] = dout + _rms_bwd(dh2 * (g_pre * sc1), xn, r)
        small_ref[SB3_DSH:SB3_DSH + 1, :] += dsh
        small_ref[SB3_DSC:SB3_DSC + 1, :] += dgm
        small_ref[SB3_DGT:SB3_DGT + 1, :] += dgg

        @pl.when(i == n_t - 1)
        def _():
            dgm_t = small_ref[SB3_DSC:SB3_DSC + 1, :]
            dgg_t = small_ref[SB3_DGT:SB3_DGT + 1, :]
            small_ref[SB3_DSC:SB3_DSC + 1, :] = dgm_t * g_pre
            small_ref[SB3_DG_PRE:SB3_DG_PRE + 1, :] = dgm_t * sc1
            small_ref[SB3_DGT:SB3_DGT + 1, :] = dgg_t * g_post
            small_ref[SB3_DG_POST:SB3_DG_POST + 1, :] = dgg_t * gt

    tile = pl.BlockSpec((ts, d), lambda i: (i, 0))
    wide = pl.BlockSpec((ts, N_CHIP * fq), lambda i: (i, 0))
    return _pcall(
        body, name=name, grid=(n_t,),
        out_shape=[jax.ShapeDtypeStruct((s, d), F32), jax.ShapeDtypeStruct((s, d), BF16),
                   jax.ShapeDtypeStruct((s, N_CHIP * fq), BF16), jax.ShapeDtypeStruct((SUBLANES, d), F32)],
        in_specs=[tile, tile, tile, wide, _full(mod.shape), _full(vec.shape), _full(w_up.shape), _full(w_down.shape)],
        out_specs=[tile, tile, wide, _full((SUBLANES, d))],
        args=[dx2, x1, y2, up, mod, vec, w_up, w_down], carries=carries)


def _mixer_backward(dx1, proj, conva, xb2s, hhs, pas, pbs, ys, rs_, gis, as_, mults, dgels, mod, vec, wg, w_a_out, w_b_out,
                    w_o, name, carries=()):
    s, d = dx1.shape
    n_head, bw, _ = wg.shape
    ts = min(TOKENS_MIXER_TILE, s)
    n_t = s // ts

    def body(dx1_ref, p_ref, conva_ref, xb2_ref, hh_ref, pa_ref, pb_ref, y_ref, r_ref, gi_ref, a_ref, mult_ref, dgel_ref,
             mod_ref, vec_ref, wg_ref, wa_ref, wb_ref, wo_ref,
             dp_ref, dab_ref, small_ref, dwg_ref,
             dconv_head, dxb2_head, a_head, g_head, a_buf, b_buf, c_buf):
        i = pl.program_id(0)

        @pl.when(i == 0)
        def _():
            small_ref[...] = jnp.zeros_like(small_ref)
            dwg_ref[...] = jnp.zeros_like(dwg_ref)
            dconv_head[...] = jnp.zeros_like(dconv_head)
            dxb2_head[...] = jnp.zeros_like(dxb2_head)
            a_head[...] = jnp.zeros_like(a_head)
            g_head[...] = jnp.zeros_like(g_head)

        def seg(k):
            return p_ref[:, k * d:(k + 1) * d].astype(F32)

        def vrow(k):
            return vec_ref[k:k + 1, :]

        def acc(row, val):
            small_ref[row:row + 1, :] += val

        dout = dx1_ref[...]
        yn, ry = _rms(y_ref[...].astype(F32))
        g_post = vrow(V_G_POST_MIX)
        gt = mod_ref[M_GT_M:M_GT_M + 1, :]
        acc(SB2_DGT, _colsum(dout * yn))
        dy = _rms_bwd(dout * (gt * g_post), yn, ry).astype(BF16)
        dab_ref[2] = dy
        dm = _dot_tb(dy, wo_ref[...])
        sa, sb = seg(5), seg(6)
        dpa = (dm * sa).astype(BF16)
        dpb = (dm * sb).astype(BF16)
        dab_ref[0] = dpa
        dab_ref[1] = dpb
        du_a = dm * pa_ref[...].astype(F32) * (sa * (1.0 - sa))
        du_b = dm * pb_ref[...].astype(F32) * (sb * (1.0 - sb))
        dp_ref[:, 5 * d:6 * d] = du_a.astype(BF16)
        dp_ref[:, 6 * d:7 * d] = du_b.astype(BF16)
        dy_a = _dot_tb(dpa, wa_ref[...])
        dy_b = _dot_tb(dpb, wb_ref[...])

        b_a, c_a, v_a = seg(0), seg(1), seg(2)
        dp_ref[:, 0:d] = (dy_a * conva_ref[...].astype(F32)).astype(BF16)
        dconv = dy_a * b_a
        nxt = dconv_head[...]
        d1 = _shift_up(dconv, 1, nxt)
        d2 = _shift_up(dconv, 2, nxt)
        dconv_head[...] = dconv[:SUBLANES]
        dcv = vrow(V_CONV_A_W + 2) * dconv + vrow(V_CONV_A_W + 1) * d1 + vrow(V_CONV_A_W) * d2
        cv = c_a * v_a
        acc(SB2_DWA + 2, _colsum(cv * dconv))
        acc(SB2_DWA + 1, _colsum(cv * d1))
        acc(SB2_DWA, _colsum(cv * d2))
        acc(SB2_DBA, _colsum(dconv))
        dp_ref[:, d:2 * d] = (dcv * v_a).astype(BF16)
        dp_ref[:, 2 * d:3 * d] = (dcv * c_a).astype(BF16)

        x_b, gel = seg(3), seg(4)
        hh = hh_ref[...]
        dp_ref[:, 4 * d:5 * d] = (dy_b * hh * dgel_ref[...].astype(F32)).astype(BF16)
        dhh = dy_b * gel
        xb2 = xb2_ref[...]
        r, gi, a, mult = r_ref[...].astype(F32), gi_ref[...].astype(F32), a_ref[...], mult_ref[...].astype(F32)
        sp = _softplus(-vrow(V_LAMBDA))
        a_next = _shift_up(a, 1, a_head[...])
        g = _scan_two_level(a_next, dhh, g_head[0:1, :], a_buf, b_buf, c_buf, reverse=True)
        a_head[...] = a[:SUBLANES]
        g_head[...] = g[:SUBLANES]
        gix = gi * xb2
        gm = g * mult
        dlog_a = g * (hh - mult * gix) - (g * gix) * (a * a / mult)
        dgi = gm * xb2
        dxb2 = gm * gi
        acc(SB2_DLAM, _colsum(dlog_a * r))
        dzr = dlog_a * ((-LRU_C) * sp) * (r * (1.0 - r))
        dzi = dgi * (gi * (1.0 - gi))
        acc(SB2_DBR, _colsum(dzr))
        acc(SB2_DBI, _colsum(dzi))
        xb2_b = xb2.astype(BF16)
        back = []
        for hd in range(n_head):
            cols = slice(hd * bw, (hd + 1) * bw)
            dz = jnp.concatenate([dzr[:, cols], dzi[:, cols]], axis=1).astype(BF16)
            back.append(_dot_tb(dz, wg_ref[hd]))
            dwg_ref[hd] += _dot_ta(xb2_b[:, cols], dz)
        dxb2 = dxb2 + jnp.concatenate(back, axis=1)
        nxt = dxb2_head[...]
        e1 = _shift_up(dxb2, 1, nxt)
        e2 = _shift_up(dxb2, 2, nxt)
        e3 = _shift_up(dxb2, 3, nxt)
        dxb2_head[...] = dxb2[:SUBLANES]
        dp_ref[:, 3 * d:4 * d] = (vrow(V_CONV_B_W + 3) * dxb2 + vrow(V_CONV_B_W + 2) * e1
                                  + vrow(V_CONV_B_W + 1) * e2 + vrow(V_CONV_B_W) * e3).astype(BF16)
        acc(SB2_DWB + 3, _colsum(x_b * dxb2))
        acc(SB2_DWB + 2, _colsum(x_b * e1))
        acc(SB2_DWB + 1, _colsum(x_b * e2))
        acc(SB2_DWB, _colsum(x_b * e3))
        acc(SB2_DBB, _colsum(dxb2))

        @pl.when(i == n_t - 1)
        def _():
            dgg_t = small_ref[SB2_DGT:SB2_DGT + 1, :]
            small_ref[SB2_DGT:SB2_DGT + 1, :] = dgg_t * g_post
            small_ref[SB2_DG_POST:SB2_DG_POST + 1, :] = dgg_t * gt
            lam = vrow(V_LAMBDA)
            small_ref[SB2_DLAM:SB2_DLAM + 1, :] = small_ref[SB2_DLAM:SB2_DLAM + 1, :] * (LRU_C * _sigmoid(-lam))

    rev = lambda i: (n_t - 1 - i, 0)
    tile = pl.BlockSpec((ts, d), rev)
    wide = pl.BlockSpec((ts, 7 * d), rev)
    sd = lambda dt: jax.ShapeDtypeStruct((s, d), dt)
    return _pcall(
        body, name=name, grid=(n_t,),
        out_shape=[jax.ShapeDtypeStruct((s, 7 * d), BF16), jax.ShapeDtypeStruct((3, s, d), BF16),
                   jax.ShapeDtypeStruct((2 * SUBLANES, d), F32), jax.ShapeDtypeStruct(wg.shape, F32)],
        in_specs=[tile, wide] + [tile] * 11 + [_full(mod.shape), _full(vec.shape),
                  _full(wg.shape), _full(w_a_out.shape), _full(w_b_out.shape), _full(w_o.shape)],
        out_specs=[wide, pl.BlockSpec((3, ts, d), lambda i: (0, n_t - 1 - i, 0)), _full((2 * SUBLANES, d)), _full(wg.shape)],
        scratch_shapes=[pltpu.VMEM((SUBLANES, d), F32)] * 4 + [pltpu.VMEM((d // LANES, ts, LANES), F32)] * 2
                       + [pltpu.VMEM((ts // SCAN_GROUP, d), F32)],
        args=[dx1, proj, conva, xb2s, hhs, pas, pbs, ys, rs_, gis, as_, mults, dgels, mod, vec, wg, w_a_out, w_b_out, w_o],
        carries=carries)


def _proj_backward(dproj, dx1, x, mod, vec, w_in, name, carries=()):
    s, d = x.shape
    nq = w_in.shape[2]
    ts = min(TOKENS_MATMUL_TILE, s)
    n_t = s // ts

    def body(dp_ref, dx1_ref, x_ref, mod_ref, vec_ref, w_ref, dx_ref, small_ref):
        i = pl.program_id(0)

        @pl.when(i == 0)
        def _():
            small_ref[...] = jnp.zeros_like(small_ref)

        dh = jnp.zeros((ts, d), F32)
        for qb in range(N_CHIP):
            dh = dh + _dot_tb(dp_ref[:, qb * nq:(qb + 1) * nq], w_ref[qb])
        xn, r = _rms(x_ref[...])
        g_pre = vec_ref[V_G_PRE_MIX:V_G_PRE_MIX + 1, :]
        sc1 = 1.0 + mod_ref[M_SC_M:M_SC_M + 1, :]
        dx_ref[...] = dx1_ref[...] + _rms_bwd(dh * (g_pre * sc1), xn, r)
        small_ref[SB1_DSH:SB1_DSH + 1, :] += _colsum(dh)
        small_ref[SB1_DSC:SB1_DSC + 1, :] += _colsum(dh * xn)

        @pl.when(i == n_t - 1)
        def _():
            dgm_t = small_ref[SB1_DSC:SB1_DSC + 1, :]
            small_ref[SB1_DSC:SB1_DSC + 1, :] = dgm_t * g_pre
            small_ref[SB1_DG_PRE:SB1_DG_PRE + 1, :] = dgm_t * sc1

    tile = pl.BlockSpec((ts, d), lambda i: (i, 0))
    return _pcall(
        body, name=name, grid=(n_t,),
        out_shape=[jax.ShapeDtypeStruct((s, d), F32), jax.ShapeDtypeStruct((SUBLANES, d), F32)],
        in_specs=[pl.BlockSpec((ts, N_CHIP * nq), lambda i: (i, 0)), tile, tile, _full(mod.shape), _full(vec.shape),
                  _full(w_in.shape)],
        out_specs=[tile, _full((SUBLANES, d))],
        args=[dproj, dx1, x, mod, vec, w_in], carries=carries)


def _weight_grad(a, b, name, col_blocks=1, tk=512, carries=(), square_relu=False):
    s, k = a.shape
    n = b.shape[1]
    tn = n // col_blocks
    tk = min(tk, k)

    def body(a_ref, b_ref, o_ref):
        av = a_ref[...]
        if square_relu:
            ru = jnp.maximum(av.astype(F32), 0.0)
            av = (ru * ru).astype(BF16)
        o_ref[0] = _dot_ta(av, b_ref[...])

    (out,), carried = _pcall(
        body, name=name, grid=(col_blocks, k // tk),
        out_shape=[jax.ShapeDtypeStruct((col_blocks, k, tn), F32)],
        in_specs=[pl.BlockSpec((s, tk), lambda j, i: (0, i)), pl.BlockSpec((s, tn), lambda j, i: (0, j))],
        out_specs=[pl.BlockSpec((1, tk, tn), lambda j, i: (j, i, 0))],
        args=[a, b], carries=carries)
    return out, carried


def _weight_grad_stacked(a3, b3, name, tk=512, carries=()):
    n_g, s, k = a3.shape
    n = b3.shape[2]
    kq = k // N_CHIP
    tk = min(tk, k)
    chips_per_tile = tk // kq

    def body(a_ref, b_ref, o_ref):
        o_ref[...] = _dot_ta(a_ref[...], b_ref[...]).reshape(chips_per_tile, kq, n)

    (out,), carried = _pcall(
        body, name=name, grid=(n_g, k // tk),
        out_shape=[jax.ShapeDtypeStruct((N_CHIP, n_g, kq, n), F32)],
        in_specs=[pl.BlockSpec((None, s, tk), lambda g, i: (g, 0, i)), pl.BlockSpec((None, s, n), lambda g, i: (g, 0, 0))],
        out_specs=[pl.BlockSpec((chips_per_tile, None, kq, n), lambda g, i: (i, g, 0, 0))],
        args=[a3, b3], carries=carries)
    return out.reshape(N_CHIP, n_g * kq, n), carried


def _adamw(items, name, copy_grad=False, carries=()):
    shape = items[0][0].shape
    cols = shape[-1]
    rows = items[0][0].size // cols
    tr = _row_tile(rows, cols, target_bytes=1024 * 1024 // len(items))
    c1 = 1.0 - ADAM_B1 ** ADAM_STEP
    c2 = 1.0 - ADAM_B2 ** ADAM_STEP
    n_out = 4 if copy_grad else 3
    n = len(items)

    def body(*refs):
        for k in range(n):
            w_ref, g_ref, m_ref, v_ref = refs[4 * k:4 * k + 4]
            outs = refs[4 * n + n_out * k:4 * n + n_out * (k + 1)]
            gv = g_ref[...]
            nm = ADAM_B1 * m_ref[...] + (1.0 - ADAM_B1) * gv
            nv = ADAM_B2 * v_ref[...] + (1.0 - ADAM_B2) * (gv * gv)
            outs[0][...] = (-ADAM_LR) * ((nm / c1) / (jnp.sqrt(nv / c2) + ADAM_EPS) + ADAM_WD * w_ref[...])
            outs[1][...] = nm
            outs[2][...] = nv
            if copy_grad:
                outs[3][...] = gv

    spec = pl.BlockSpec((tr, cols), lambda i: (i, 0))
    outs, carried = _pcall(
        body, name=name, grid=(rows // tr,),
        out_shape=[jax.ShapeDtypeStruct((rows, cols), F32)] * (n_out * n),
        in_specs=[spec] * (4 * n), out_specs=[spec] * (n_out * n),
        args=[t.reshape(rows, cols) for item in items for t in item], carries=carries)
    return [tuple(o.reshape(shape) for o in outs[n_out * k:n_out * (k + 1)]) for k in range(n)], carried


def kernel(x, c, w_mod, b_mod, g_pre_mix, g_post_mix, w_in, conv_a_w, conv_a_b, w_a_out, conv_b_w, conv_b_b, w_gate_r, b_gate_r, w_gate_i, b_gate_i, lru_lambda, w_b_out, w_o, g_pre_mlp, g_post_mlp, w_mlp_up, w_mlp_down, loss_target, m_w_mod, m_b_mod, m_g_pre_mix, m_g_post_mix, m_w_in, m_conv_a_w, m_conv_a_b, m_w_a_out, m_conv_b_w, m_conv_b_b, m_w_gate_r, m_b_gate_r, m_w_gate_i, m_b_gate_i, m_lru_lambda, m_w_b_out, m_w_o, m_g_pre_mlp, m_g_post_mlp, m_w_mlp_up, m_w_mlp_down, v_w_mod, v_b_mod, v_g_pre_mix, v_g_post_mix, v_w_in, v_conv_a_w, v_conv_a_b, v_w_a_out, v_conv_b_w, v_conv_b_b, v_w_gate_r, v_b_gate_r, v_w_gate_i, v_b_gate_i, v_lru_lambda, v_w_b_out, v_w_o, v_g_pre_mlp, v_g_post_mlp, v_w_mlp_up, v_w_mlp_down):
    weights = dict(w_mod=w_mod, b_mod=b_mod, g_pre_mix=g_pre_mix, g_post_mix=g_post_mix, w_in=w_in, conv_a_w=conv_a_w,
                   conv_a_b=conv_a_b, w_a_out=w_a_out, conv_b_w=conv_b_w, conv_b_b=conv_b_b, w_gate_r=w_gate_r,
                   b_gate_r=b_gate_r, w_gate_i=w_gate_i, b_gate_i=b_gate_i, lru_lambda=lru_lambda, w_b_out=w_b_out,
                   w_o=w_o, g_pre_mlp=g_pre_mlp, g_post_mlp=g_post_mlp, w_mlp_up=w_mlp_up, w_mlp_down=w_mlp_down)
    mom1 = dict(w_mod=m_w_mod, b_mod=m_b_mod, g_pre_mix=m_g_pre_mix, g_post_mix=m_g_post_mix, w_in=m_w_in,
                conv_a_w=m_conv_a_w, conv_a_b=m_conv_a_b, w_a_out=m_w_a_out, conv_b_w=m_conv_b_w, conv_b_b=m_conv_b_b,
                w_gate_r=m_w_gate_r, b_gate_r=m_b_gate_r, w_gate_i=m_w_gate_i, b_gate_i=m_b_gate_i,
                lru_lambda=m_lru_lambda, w_b_out=m_w_b_out, w_o=m_w_o, g_pre_mlp=m_g_pre_mlp, g_post_mlp=m_g_post_mlp,
                w_mlp_up=m_w_mlp_up, w_mlp_down=m_w_mlp_down)
    mom2 = dict(w_mod=v_w_mod, b_mod=v_b_mod, g_pre_mix=v_g_pre_mix, g_post_mix=v_g_post_mix, w_in=v_w_in,
                conv_a_w=v_conv_a_w, conv_a_b=v_conv_a_b, w_a_out=v_w_a_out, conv_b_w=v_conv_b_w, conv_b_b=v_conv_b_b,
                w_gate_r=v_w_gate_r, b_gate_r=v_b_gate_r, w_gate_i=v_w_gate_i, b_gate_i=v_b_gate_i,
                lru_lambda=v_lru_lambda, w_b_out=v_w_b_out, w_o=v_w_o, g_pre_mlp=v_g_pre_mlp, g_post_mlp=v_g_post_mlp,
                w_mlp_up=v_w_mlp_up, w_mlp_down=v_w_mlp_down)
    names = list(weights)

    n_layer = w_in.shape[0]
    s, d = x.shape[1], x.shape[2]
    n_head, bw = w_gate_r.shape[1], w_gate_r.shape[2]
    dq = d // N_CHIP
    mq = w_mod.shape[2]
    n_mod = (N_CHIP * mq) // d
    ka, kb = conv_a_w.shape[1], conv_b_w.shape[1]

    mx, my, mc = _place()
    q_me = 2 * mx + my
    q_arr = jnp.reshape(q_me, (1,)).astype(jnp.int32)

    me_dev = 4 * mx + 2 * my + mc
    me_arr = jnp.reshape(me_dev, (1,)).astype(jnp.int32)

    big_names = ["w_in", "w_a_out", "w_b_out", "w_o", "w_mlp_up", "w_mlp_down"]
    groups = [["w_in"], ["w_a_out", "w_b_out", "w_o"], ["w_mlp_up", "w_mlp_down"]]
    placed = {("w_in", 0): _cast_place_all([(w_in, 0)], q_arr, "cast_place_first")[0][0]}
    wfull = [dict() for _ in range(n_layer)]
    riders = {}
    for l in range(n_layer):
        riders.setdefault(3 * l - 1, []).append(([("w_in", l)], 0.9 if l else 1.0))
        riders.setdefault(3 * l - 2 if l else 0, []).append(([(nm, l) for nm in groups[1]], 0.9 if l else 0.5))
        riders.setdefault(3 * l, []).append(([("w_mlp_up", l)], 0.7 if l else 0.9))
        riders.setdefault(3 * l + 1, []).insert(0, ([("w_mlp_down", l)], 0.5))

    def gather_carry(call):
        return [_gather_carry([placed[k] for k in keys], frac) for keys, frac in riders.get(call, [])]

    def gathered(call, carried):
        for (keys, _), ws in zip(riders.get(call, []), carried):
            for (nm, l), w in zip(keys, ws):
                wfull[l][nm] = w.reshape(d, d) if nm in groups[1] else w

    n_conv_rows = n_layer * (ka + kb)
    conv_blk = -(-n_conv_rows // SUBLANES) * SUBLANES
    blk_rows = SUBLANES + conv_blk
    conv_rows = jnp.concatenate([jnp.concatenate([conv_a_w[l], conv_b_w[l]], axis=0) for l in range(n_layer)], axis=0)
    conv_rows = jnp.pad(conv_rows, ((0, conv_blk - n_conv_rows), (0, d - dq)))
    c_conv = jnp.concatenate([jnp.pad(c, ((0, SUBLANES - 1), (0, 0))), conv_rows], axis=0)
    rest = [(nm, l) for l in range(n_layer) for nm in big_names if (nm, l) != ("w_in", 0)]
    rest_placed, carried = _cast_place_all([(weights[nm], l) for nm, l in rest], q_arr, "cast_place_rest",
                                           carries=gather_carry(-1) + [_allgather_carry([c_conv])])
    placed.update(zip(rest, rest_placed))
    gathered(-1, carried[:1])
    gathered1 = lax.dynamic_update_slice(carried[1][0], c_conv, (me_dev * blk_rows, 0)).reshape(N_DEV, blk_rows, d)
    c_all = gathered1[:, 0, :]
    conv_full = jnp.concatenate([gathered1[2 * qb, SUBLANES:SUBLANES + n_conv_rows, :dq] for qb in range(N_CHIP)], axis=1)

    b_mod_shard = lax.dynamic_slice_in_dim(b_mod, q_me * mq, mq, axis=1)
    mod_part = _mod_forward(c_all, w_mod, b_mod_shard, "mod_forward")
    gathered2 = _all_gather_small(mod_part, "gather_mod").reshape(N_DEV, n_layer, N_DEV, mq)
    mod_rows = jnp.concatenate(
        [lax.dynamic_index_in_dim(gathered2[2 * qb], me_dev, axis=1, keepdims=False) for qb in range(N_CHIP)], axis=1)
    mods = [jnp.pad(mod_rows[l].reshape(n_mod, d), ((0, SUBLANES - n_mod), (0, 0))) for l in range(n_layer)]

    vecs = []
    for l in range(n_layer):
        base = l * (ka + kb)
        rows = [g_pre_mix[l], g_post_mix[l], conv_a_b[l], conv_b_b[l], b_gate_r[l], b_gate_i[l], lru_lambda[l],
                g_pre_mlp[l], g_post_mlp[l]]
        vecs.append(jnp.concatenate([jnp.stack(rows, axis=0), conv_full[base:base + ka + kb]], axis=0))

    wgs =[jnp.concatenate([w_gate_r[l], w_gate_i[l]], axis=-1).astype(BF16) for l in range(n_layer)]

    xs = x[0]
    saved = []
    for l in range(n_layer):
        wl = wfull[l]
        (h, proj, dgel), carried = _norm_proj(xs, mods[l], vecs[l], wl["w_in"], f"norm_proj_{l}", gather_carry(3 * l))
        gathered(3 * l, carried)
        (x1, conva, xb2, hh, abm, pa, pb, yy, gr, ggi, ga, gmult), carried = _mixer_forward(
            xs, proj, mods[l], vecs[l], wgs[l], wl["w_a_out"], wl["w_b_out"], wl["w_o"], f"mixer_forward_{l}",
            gather_carry(3 * l + 1))
        gathered(3 * l + 1, carried)
        (x2, h2, up, y2, *loss_tile), carried = _mlp_forward(
            x1, mods[l], vecs[l], wl["w_mlp_up"], wl["w_mlp_down"], f"mlp_forward_{l}", gather_carry(3 * l + 2),
            target=loss_target[0] if l == n_layer - 1 else None)
        gathered(3 * l + 2, carried)
        saved.append(dict(x=xs, h=h, proj=proj, x1=x1, conva=conva, xb2=xb2, hh=hh, abm=abm, pa=pa, pb=pb,
                          y=yy, r=gr, gi=ggi, a=ga, mult=gmult, dgel=dgel, h2=h2, up=up, y2=y2))
        xs = x2
    dxs = xs
    loss_block = jnp.pad(loss_tile[0], ((0, 0), (0, d - LANES)))

    chips_q = [q_me ^ 2, q_me ^ 1, q_me ^ 3]
    pf = jnp.stack([mc, q_me] + chips_q).astype(jnp.int32)
    rs = dict(grad={}, landed={}, to_send={}, from_chips={}, out={})
    to_exchange, to_scatter, to_join, to_gather = [], [], [], []
    small_own, small_all = {}, {}

    def ride(call, what, name=None):
        ex = list(to_exchange) if "x" in what else []
        sc = list(to_scatter) if "s" in what else []
        ga = list(to_gather) if "g" in what else []
        jn = []
        for key in (to_join if "j" in what else []):
            if key[0] not in [k[0] for k in jn]:
                jn.append(key)
        carries = []
        if ex:
            carries.append(_exchange_carry([rs["grad"][k] for k in ex]))
        if sc:
            carries.append(_scatter_carry([rs["to_send"][k] for k in sc]))
        if jn:
            carries.append(_join_carry([rs["out"][k[0]] for k in jn], [k[1] for k in jn]))
        if ga:
            carries.append(_allgather_carry([small_own[k] for k in ga]))
        if call is None:
            carried = _run_carries(carries, name) if carries else []
            res = None
        else:
            res, carried = call(carries)
        carried = list(carried)
        if ex:
            for k, ld in zip(ex, carried.pop(0)):
                to_exchange.remove(k)
                rs["landed"][k] = ld
                rs["to_send"][k] = _add_sibling_half(rs["grad"][k], ld, pf, f"rs_add_sibling_{k[0]}_{k[1]}")
                to_scatter.append(k)
        if sc:
            for k, fc in zip(sc, carried.pop(0)):
                to_scatter.remove(k)
                rs["out"][k[0]] = _add_chips(rs["grad"][k], rs["landed"][k], fc, pf, rs["out"].get(k[0]), k[1], n_layer,
                                             f"rs_add_chips_{k[0]}_{k[1]}")
                to_join.append(k)
        if jn:
            for k, o in zip(jn, carried.pop(0)):
                to_join.remove(k)
                rs["out"][k[0]] = o
        if ga:
            for k, o in zip(ga, carried.pop(0)):
                to_gather.remove(k)
                small_all[k] = o
        return res

    def gather_small(key, parts):
        small_own[key] = parts[0] if len(parts) == 1 else jnp.concatenate(parts, axis=0)
        to_gather.append(key)

    def ready(nm, l, g):
        rs["grad"][(nm, l)] = g
        to_exchange.append((nm, l))

    rowblk = lambda t: t.reshape(N_CHIP, t.shape[1] // N_CHIP, t.shape[2])
    small1_prev = None
    for l in reversed(range(n_layer)):
        wl, sv = wfull[l], saved[l]
        dx1, dy2, dup, small3 = ride(lambda cr: _mlp_backward(
            dxs, sv["x1"], sv["y2"], sv["up"], mods[l], vecs[l], wl["w_mlp_up"], wl["w_mlp_down"], f"mlp_backward_{l}", cr), "xsjg")
        ready("w_mlp_up", l, _weight_grad(sv["h2"], dup, f"grad_w_mlp_up_{l}", col_blocks=N_CHIP, tk=d)[0])
        g_down = ride(lambda cr: _weight_grad(sv["up"], dy2, f"grad_w_mlp_down_{l}", tk=d, carries=cr, square_relu=True), "x")
        ready("w_mlp_down", l, rowblk(g_down))
        dproj, dab, small2, dwg = ride(lambda cr: _mixer_backward(
            dx1, sv["proj"], sv["conva"], sv["xb2"], sv["hh"], sv["pa"], sv["pb"], sv["y"],
            sv["r"], sv["gi"], sv["a"], sv["mult"], sv["dgel"], mods[l], vecs[l], wgs[l],
            wl["w_a_out"], wl["w_b_out"], wl["w_o"], f"mixer_backward_{l}", cr), "xsjg")
        gather_small(("late", l, "s"), ([small1_prev] if small1_prev is not None else []) + [small2, small3])
        gather_small(("late", l, "w"), [dwg.reshape(2 * bw, d).astype(BF16)])
        g_in = ride(lambda cr: _weight_grad(sv["h"], dproj, f"grad_w_in_{l}", col_blocks=N_CHIP, carries=cr), "xsj")
        ready("w_in", l, g_in)
        g_abo = ride(lambda cr: _weight_grad_stacked(sv["abm"], dab, f"grad_w_abo_{l}", tk=d, carries=cr), "xg")
        ready("w_abo", l, g_abo)
        dxs, small1_prev = ride(lambda cr: _proj_backward(dproj, dx1, sv["x"], mods[l], vecs[l], wl["w_in"],
                                                          f"proj_backward_{l}", cr), "xsjg")
    grad_x = dxs[None]
    gather_small(("last", 0, "s"), [small1_prev, loss_block])

    tail = 0
    while to_exchange or to_scatter or to_join or to_gather:
        ride(None, "xsjg", f"rs_tail_{tail}")
        tail += 1
    grads, deltas, new_m, new_v = {}, {}, {}, {}

    def adam(nms, copy_grad=False):
        items = [(weights[nm], grads[nm], mom1[nm], mom2[nm]) for nm in nms]
        res, _ = _adamw(items, "adamw_" + "_".join(nms), copy_grad)
        for nm, r in zip(nms, res):
            deltas[nm], new_m[nm], new_v[nm] = r[:3]
            if copy_grad:
                grads[nm] = r[3]

    for nms in (["w_mlp_up", "w_mlp_down"], ["w_in"]):
        for nm in nms:
            grads[nm] = rs["out"][nm].reshape(weights[nm].shape)
        adam(nms, True)

    sums ={k: _sum_devices(small_all[k], small_own[k], me_arr, f"sum_small_{k[0]}_{k[1]}_{k[2]}") for k in small_own}

    loss = sums[("last", 0, "s")][SUBLANES, 0]
    small_full = {}

    def rows_of(l, part):
        if part == 0:
            return (("late", l - 1, "s"), 0) if l >= 1 else (("last", 0, "s"), 0)
        if part == 3:
            return ("late", l, "w"), 0
        base = SUBLANES if l < n_layer - 1 else 0
        return ("late", l, "s"), base + (0, 0, 2 * SUBLANES)[part]

    def summed(l, part, row, n_rows=1):
        key, base = rows_of(l, part)
        return sums[key][base + row:base + row + n_rows]

    def per_device(l, part, row):
        key, base = rows_of(l, part)
        own = small_own[key]
        if key not in small_full:
            small_full[key] = lax.dynamic_update_slice(small_all[key], own, (me_dev * own.shape[0], 0)).reshape(
                (N_DEV,) + own.shape)
        return small_full[key][:, base + row:base + row + 1]

    mod_rows = [(0, SB1_DSH), (0, SB1_DSC), (1, SB2_DGT), (2, SB3_DSH), (2, SB3_DSC), (2, SB3_DGT)]
    dmod_all = jnp.stack([jnp.concatenate([per_device(l, p, r)[:, 0, :] for p, r in mod_rows], axis=1)
                          for l in range(n_layer)], axis=0)
    o1, o2, o3, o4 = 0, SUBLANES, 3 * SUBLANES, 4 * SUBLANES
    small_sum = jnp.stack([jnp.concatenate([summed(l, 0, 0, SUBLANES), summed(l, 1, 0, 2 * SUBLANES),
                                            summed(l, 2, 0, SUBLANES), summed(l, 3, 0, 2 * bw)], axis=0)
                           for l in range(n_layer)], axis=0)
    mod_rows_of = [o1 + SB1_DSH, o1 + SB1_DSC, o2 + SB2_DGT, o3 + SB3_DSH, o3 + SB3_DSC, o3 + SB3_DGT]
    grads["w_mod"] = _mod_backward(c_all.T, lax.dynamic_slice_in_dim(dmod_all, q_me * mq, mq, axis=2), "mod_backward")
    grads["b_mod"] = jnp.concatenate([small_sum[:, k, :] for k in mod_rows_of], axis=1)
    grads["g_pre_mix"] = small_sum[:, o1 + SB1_DG_PRE]
    grads["g_post_mix"] = small_sum[:, o2 + SB2_DG_POST]
    grads["conv_a_w"] = lax.dynamic_slice_in_dim(small_sum[:, o2 + SB2_DWA:o2 + SB2_DWA + ka], q_me * dq, dq, axis=2)
    grads["conv_a_b"] = small_sum[:, o2 + SB2_DBA]
    grads["conv_b_w"] = lax.dynamic_slice_in_dim(small_sum[:, o2 + SB2_DWB:o2 + SB2_DWB + kb], q_me * dq, dq, axis=2)
    grads["conv_b_b"] = small_sum[:, o2 + SB2_DBB]
    grads["lru_lambda"] = small_sum[:, o2 + SB2_DLAM]
    grads["b_gate_r"] = small_sum[:, o2 + SB2_DBR]
    grads["b_gate_i"] = small_sum[:, o2 + SB2_DBI]
    grads["g_pre_mlp"] = small_sum[:, o3 + SB3_DG_PRE]
    grads["g_post_mlp"] = small_sum[:, o3 + SB3_DG_POST]
    dwg_sum = small_sum[:, o4:].reshape(n_layer, n_head, bw, 2 * bw)
    grads["w_gate_r"] = dwg_sum[..., :bw]
    grads["w_gate_i"] = dwg_sum[..., bw:]

    for k, nm in enumerate(groups[1]):
        grads[nm] = rs["out"]["w_abo"][:, k * dq:(k + 1) * dq]

    by_shape = {}
    for nm in names:
        if nm not in deltas:
            by_shape.setdefault(weights[nm].shape, []).append(nm)
    for nms in by_shape.values():
        adam(nms)
    return (loss, grad_x, *[grads[nm] for nm in names], *[deltas[nm] for nm in names],
            *[new_m[nm] for nm in names], *[new_v[nm] for nm in names])
```
